```python
import jax, jax.numpy as jnp
from jax import lax
import numpy as np

D_MODEL = 1024
BATCH = 4
SEQ = 4096
DEPTH = 1
DEC_BATCH = 128
DEC_SEQ = 4
PAST_LEN = 16384
PAGE_SIZE = 128

D_MIX = D_MODEL
D_ATTN = D_MIX // 2
D_GMLP = D_MIX - D_ATTN
HEAD_DIM = 64
N_HEADS = D_ATTN // HEAD_DIM
N_KV = 2
GQA = N_HEADS // N_KV
WINDOW = 128
ROPE_THETA = 10000.0
CHUNK = 128
G_DIM = 64
G_HEADS = D_GMLP // G_DIM
D_FF = 2816
EPS = 1e-6
Q_W = N_HEADS * HEAD_DIM
KV_W = N_KV * HEAD_DIM
D_IN = Q_W + 2 * KV_W + 2 * D_GMLP

kernel_name = "hymba_swa_sink_gmlp_macaron_step"


def _rmsnorm(x, g):
    xf = x.astype(jnp.float32)
    y = xf * lax.rsqrt(jnp.mean(xf * xf, axis=-1, keepdims=True) + EPS)
    return (y * g.astype(jnp.float32)).astype(x.dtype)


def _rope(x, pos):
    inv_freq = ROPE_THETA ** (-jnp.arange(0, HEAD_DIM, 2, dtype=jnp.float32) / HEAD_DIM)
    ang = pos.astype(jnp.float32)[:, None] * inv_freq[None, :]
    c = jnp.cos(ang)[:, None, :]
    s = jnp.sin(ang)[:, None, :]
    xf = x.astype(jnp.float32)
    x1, x2 = xf[..., :HEAD_DIM // 2], xf[..., HEAD_DIM // 2:]
    return jnp.concatenate([x1 * c - x2 * s, x2 * c + x1 * s], axis=-1).astype(x.dtype)


def _ffn_half(x, g, wg, wu, wd):
    h = _rmsnorm(x, g)
    return x + 0.5 * ((jax.nn.silu(h @ wg) * (h @ wu)) @ wd)


def _in_proj(h, w_in, g_v):
    z = h @ w_in
    lead = z.shape[:-1]
    i1 = Q_W
    i2 = i1 + KV_W
    i3 = i2 + KV_W
    i4 = i3 + D_GMLP
    q = z[..., :i1].reshape(*lead, N_HEADS, HEAD_DIM)
    k = z[..., i1:i2].reshape(*lead, N_KV, HEAD_DIM)
    v = z[..., i2:i3].reshape(*lead, N_KV, HEAD_DIM)
    u = jax.nn.gelu(z[..., i3:i4])
    gv = _rmsnorm(jax.nn.gelu(z[..., i4:]), g_v).reshape(*lead, G_HEADS, G_DIM)
    return q, k, v, u, gv


def _sink_attend(q, k, v, valid, sinks):
    s = jnp.einsum('...qkgd,...skd->...kgqs', q, k).astype(jnp.float32) * (HEAD_DIM ** -0.5)
    s = jnp.where(valid, s, -jnp.inf)
    sink = jnp.broadcast_to(sinks.astype(jnp.float32).reshape(N_KV, GQA, 1, 1), s.shape[:-1] + (1,))
    p = jax.nn.softmax(jnp.concatenate([s, sink], axis=-1), axis=-1)[..., :-1]
    return jnp.einsum('...kgqs,...skd->...qkgd', p.astype(v.dtype), v)


def _swa_prompt(q, k, v, sinks):
    B, S = q.shape[:2]
    nb = S // WINDOW
    qb = q.reshape(B, nb, WINDOW, N_KV, GQA, HEAD_DIM)
    kp = jnp.pad(k, ((0, 0), (WINDOW, 0), (0, 0), (0, 0)))
    vp = jnp.pad(v, ((0, 0), (WINDOW, 0), (0, 0), (0, 0)))
    kb = jnp.concatenate([kp[:, :S].reshape(B, nb, WINDOW, N_KV, HEAD_DIM),
                          k.reshape(B, nb, WINDOW, N_KV, HEAD_DIM)], axis=2)
    vb = jnp.concatenate([vp[:, :S].reshape(B, nb, WINDOW, N_KV, HEAD_DIM),
                          v.reshape(B, nb, WINDOW, N_KV, HEAD_DIM)], axis=2)
    qi = jnp.arange(WINDOW)
    sj = jnp.arange(2 * WINDOW)
    n = jnp.arange(nb)
    dist = WINDOW + qi[:, None] - sj[None, :]
    kpos = (n[:, None] - 1) * WINDOW + sj[None, :]
    valid = ((dist >= 0) & (dist < WINDOW))[None] & (kpos >= 0)[:, None, :]
    o = _sink_attend(qb, kb, vb, valid[None, :, None, None], sinks)
    return o.reshape(B, S, Q_W)


def _swa_sample(q, k_all, v_all, kpos, qpos, sinks):
    Bd, T = q.shape[:2]
    qg = q.reshape(Bd, T, N_KV, GQA, HEAD_DIM)
    dist = qpos[:, None] - kpos[None, :]
    valid = (dist >= 0) & (dist < WINDOW)
    o = _sink_attend(qg, k_all, v_all, valid, sinks)
    return o.reshape(Bd, T, Q_W)


def _gmlp(u, gv, ws, b_s):
    B, L = u.shape[:2]
    c = min(L, CHUNK)
    vb = gv.reshape(B, L // c, c, G_HEADS, G_DIM)
    mixed = jnp.einsum('hts,bnshd->bnthd', ws[:, :c, :c], vb) + b_s[:, :c].T[None, None, :, :, None]
    return u * mixed.reshape(B, L, D_GMLP)


def _out_proj(ya, yg, g_a, g_g, w_out):
    return jnp.concatenate([_rmsnorm(ya, g_a), _rmsnorm(yg, g_g)], axis=-1) @ w_out


def setup_inputs(seed: int = 0) -> dict:
    key = jax.random.key(seed)
    ks = jax.random.split(key, 24)
    f32 = jnp.float32
    w_buf = min(WINDOW, PAST_LEN)

    def nrm(k, shape, scale):
        return jax.random.normal(k, shape, f32) * scale

    def gain(k, shape):
        return 1.0 + 0.02 * jax.random.normal(k, shape, f32)

    return {
        "x_prompt": nrm(ks[0], (BATCH, SEQ, D_MODEL), 1.0),
        "x_sample": nrm(ks[1], (DEC_BATCH, DEC_SEQ, D_MODEL), 1.0),
        "cache_k_win": nrm(ks[2], (DEPTH, DEC_BATCH, w_buf, N_KV, HEAD_DIM), 1.0),
        "cache_v_win": nrm(ks[3], (DEPTH, DEC_BATCH, w_buf, N_KV, HEAD_DIM), 1.0),
        "norm_ffn1": gain(ks[4], (DEPTH, D_MODEL)),
        "ffn1_gate": nrm(ks[5], (DEPTH, D_MODEL, D_FF), D_MODEL ** -0.5),
        "ffn1_up": nrm(ks[6], (DEPTH, D_MODEL, D_FF), D_MODEL ** -0.5),
        "ffn1_down": nrm(ks[7], (DEPTH, D_FF, D_MODEL), D_FF ** -0.5),
        "norm_mix": gain(ks[8], (DEPTH, D_MODEL)),
        "w_in": nrm(ks[9], (DEPTH, D_MODEL, D_IN), D_MODEL ** -0.5),
        "attn_sinks": nrm(ks[10], (DEPTH, N_HEADS), 0.5),
        "gmlp_v_norm": gain(ks[11], (DEPTH, D_GMLP)),
        "gmlp_w_s": nrm(ks[12], (DEPTH, G_HEADS, CHUNK, CHUNK), CHUNK ** -0.5),
        "gmlp_b_s": gain(ks[13], (DEPTH, G_HEADS, CHUNK)),
        "norm_attn_out": gain(ks[14], (DEPTH, D_ATTN)),
        "norm_gmlp_out": gain(ks[15], (DEPTH, D_GMLP)),
        "w_out": nrm(ks[16], (DEPTH, D_MIX, D_MODEL), D_MIX ** -0.5),
        "norm_ffn2": gain(ks[17], (DEPTH, D_MODEL)),
        "ffn2_gate": nrm(ks[18], (DEPTH, D_MODEL, D_FF), D_MODEL ** -0.5),
        "ffn2_up": nrm(ks[19], (DEPTH, D_MODEL, D_FF), D_MODEL ** -0.5),
        "ffn2_down": nrm(ks[20], (DEPTH, D_FF, D_MODEL), D_FF ** -0.5),
        "norm_final": gain(ks[21], (D_MODEL,)),
    }


def reference(x_prompt, x_sample, cache_k_win, cache_v_win, norm_ffn1, ffn1_gate, ffn1_up, ffn1_down,
              norm_mix, w_in, attn_sinks, gmlp_v_norm, gmlp_w_s, gmlp_b_s, norm_attn_out, norm_gmlp_out,
              w_out, norm_ffn2, ffn2_gate, ffn2_up, ffn2_down, norm_final):
    S = x_prompt.shape[1]
    T = x_sample.shape[1]
    w_buf = cache_k_win.shape[2]
    pos_p = jnp.arange(S, dtype=jnp.int32)
    pos_s = PAST_LEN + jnp.arange(T, dtype=jnp.int32)
    kpos_s = jnp.concatenate([PAST_LEN - w_buf + jnp.arange(w_buf, dtype=jnp.int32), pos_s])
    tril = jnp.tril(jnp.ones((CHUNK, CHUNK), dtype=bool))

    hp, hs = x_prompt, x_sample
    kwp, vwp, kws, vws, gvp, gvs = [], [], [], [], [], []
    for l in range(DEPTH):
        hp = _ffn_half(hp, norm_ffn1[l], ffn1_gate[l], ffn1_up[l], ffn1_down[l])
        hs = _ffn_half(hs, norm_ffn1[l], ffn1_gate[l], ffn1_up[l], ffn1_down[l])
        ws = jnp.where(tril, gmlp_w_s[l], jnp.zeros((), gmlp_w_s.dtype))

        q, k, v, u, gv = _in_proj(_rmsnorm(hp, norm_mix[l]), w_in[l], gmlp_v_norm[l])
        q = _rope(q, pos_p)
        k = _rope(k, pos_p)
        ya = _swa_prompt(q, k, v, attn_sinks[l])
        yg = _gmlp(u, gv, ws, gmlp_b_s[l])
        hp = hp + _out_proj(ya, yg, norm_attn_out[l], norm_gmlp_out[l], w_out[l])
        kwp.append(k[:, -min(WINDOW, S):])
        vwp.append(v[:, -min(WINDOW, S):])
        gvp.append(gv[:, -CHUNK:])

        q, k, v, u, gv = _in_proj(_rmsnorm(hs, norm_mix[l]), w_in[l], gmlp_v_norm[l])
        q = _rope(q, pos_s)
        k = _rope(k, pos_s)
        k_all = jnp.concatenate([cache_k_win[l].astype(k.dtype), k], axis=1)
        v_all = jnp.concatenate([cache_v_win[l].astype(v.dtype), v], axis=1)
        ya = _swa_sample(q, k_all, v_all, kpos_s, pos_s, attn_sinks[l])
        yg = _gmlp(u, gv, ws, gmlp_b_s[l])
        hs = hs + _out_proj(ya, yg, norm_attn_out[l], norm_gmlp_out[l], w_out[l])
        kws.append(k_all[:, -w_buf:])
        vws.append(v_all[:, -w_buf:])
        gvs.append(gv)

        hp = _ffn_half(hp, norm_ffn2[l], ffn2_gate[l], ffn2_up[l], ffn2_down[l])
        hs = _ffn_half(hs, norm_ffn2[l], ffn2_gate[l], ffn2_up[l], ffn2_down[l])

    y_prompt = _rmsnorm(hp, norm_final)
    y_sample = _rmsnorm(hs, norm_final)
    return (y_prompt, y_sample, jnp.stack(kwp), jnp.stack(vwp), jnp.stack(kws), jnp.stack(vws),
            jnp.stack(gvp), jnp.stack(gvs))
```

```python
import functools

import jax
import jax.numpy as jnp
from jax import lax
from jax.experimental import pallas as pl
from jax.experimental.pallas import tpu as pltpu

F32 = jnp.float32
BF16 = jnp.bfloat16

D_MODEL = 1024
D_FF = 2816
HEAD_DIM = 64
N_HEADS = 8
N_KV = 2
GQA = N_HEADS // N_KV
WINDOW = 128
CHUNK = 128
G_HEADS = 8
Q_W = N_HEADS * HEAD_DIM
KV_W = N_KV * HEAD_DIM
D_GMLP = 512
D_IN = Q_W + 2 * KV_W + 2 * D_GMLP
K_OFF = Q_W
V_OFF = K_OFF + KV_W
U_OFF = V_OFF + KV_W
GV_OFF = U_OFF + D_GMLP
ROPE_THETA = 10000.0
PAST_LEN = 16384
EPS = 1e-6
Q_SCALE = HEAD_DIM ** -0.5
LANES = 128

TOKEN_TILE = 512
FF_CHUNK = 256
SEQ_GROUP = 8
VMEM_LIMIT_BYTES = 56 * 1024 * 1024

_NT = (((1,), (1,)), ((), ()))


def _rms(x, g):
    ms = jnp.mean(x * x, axis=-1, keepdims=True)
    return (x * lax.rsqrt(ms + EPS)) * g


def _resident(shape):
    zeros = (0,) * len(shape)
    return pl.BlockSpec(shape, lambda *_: zeros, pipeline_mode=pl.Buffered(1))


def _ffn_kernel(*refs, final_norm):
    if final_norm:
        x_ref, g_ref, wg_ref, wu_ref, wd_ref, gf_ref, o_ref, act_ref = refs
    else:
        x_ref, g_ref, wg_ref, wu_ref, wd_ref, o_ref, act_ref = refs
    x = x_ref[...]
    h = _rms(x, g_ref[...]).astype(BF16)
    for c in range(D_FF // FF_CHUNK):
        sl = slice(c * FF_CHUNK, (c + 1) * FF_CHUNK)
        gate = jnp.dot(h, wg_ref[:, sl], preferred_element_type=F32)
        up = jnp.dot(h, wu_ref[:, sl], preferred_element_type=F32)
        act_ref[:, sl] = (gate * jax.nn.sigmoid(gate) * up).astype(BF16)
    y = x + 0.5 * jnp.dot(act_ref[...], wd_ref[...], preferred_element_type=F32)
    if final_norm:
        y = _rms(y, gf_ref[...])
    o_ref[...] = y


def _ffn_half(x, g, wg, wu, wd, gf=None):
    n = x.shape[0]
    tm = min(TOKEN_TILE, n)
    assert n % tm == 0
    row_spec = pl.BlockSpec((tm, D_MODEL), lambda i: (i, 0))
    in_specs = [row_spec, _resident((1, D_MODEL)), _resident((D_MODEL, D_FF)), _resident((D_MODEL, D_FF)),
                _resident((D_FF, D_MODEL))]
    args = [x, g.reshape(1, D_MODEL), wg, wu, wd]
    if gf is not None:
        in_specs.append(_resident((1, D_MODEL)))
        args.append(gf.reshape(1, D_MODEL))
    return pl.pallas_call(
        functools.partial(_ffn_kernel, final_norm=gf is not None),
        grid=(n // tm,),
        in_specs=in_specs,
        out_specs=row_spec,
        out_shape=jax.ShapeDtypeStruct((n, D_MODEL), F32),
        scratch_shapes=[pltpu.VMEM((tm, D_FF), BF16)],
        compiler_params=pltpu.CompilerParams(dimension_semantics=("arbitrary",), vmem_limit_bytes=VMEM_LIMIT_BYTES),
        name="ffn_final" if gf is not None else "ffn_half",
    )(*args)


def _rope(xg, cos, sin_signed):
    lane = lax.broadcasted_iota(jnp.int32, xg.shape, 1)
    first_half = (lane & (HEAD_DIM - 1)) < HEAD_DIM // 2
    swapped = jnp.where(first_half, pltpu.roll(xg, LANES - HEAD_DIM // 2, 1), pltpu.roll(xg, HEAD_DIM // 2, 1))
    return xg * cos + swapped * sin_signed


def _in_proj(x, gmix, w_in, cos, sin_signed, gvn_gain):
    h = _rms(x, gmix).astype(BF16)
    z = jnp.dot(h, w_in, preferred_element_type=F32)
    q = jnp.concatenate(
        [(_rope(z[:, LANES * i:LANES * (i + 1)], cos, sin_signed) * Q_SCALE).astype(BF16) for i in range(Q_W // LANES)],
        axis=1)
    k = _rope(z[:, K_OFF:V_OFF], cos, sin_signed)
    v = z[:, V_OFF:U_OFF]
    u = jax.nn.gelu(z[:, U_OFF:GV_OFF])
    gvn = _rms(jax.nn.gelu(z[:, GV_OFF:]), gvn_gain)
    return q, k, v, u, gvn


def _out_proj(x, ya, yg, ga, gg, w_out):
    cat = jnp.concatenate([_rms(ya, ga).astype(BF16), _rms(yg, gg).astype(BF16)], axis=1)
    return x + jnp.dot(cat, w_out, preferred_element_type=F32)


def _softmax_sink(s, sink):
    m = jnp.maximum(jnp.max(s, axis=1, keepdims=True), sink)
    p = jnp.exp(s - m)
    return p, jnp.sum(p, axis=1, keepdims=True) + jnp.exp(sink - m)


def _mix_prompt_kernel(x_ref, cos_ref, sin_ref, gmix_ref, win_ref, sinks_ref, gvg_ref, wcat_ref, bias_ref, ga_ref,
                       gg_ref, wout_ref, o_ref, ko_ref, vo_ref, gvo_ref, k_s, v_s, ya_s, yg_s, *, tm):
    t = pl.program_id(1)

    @pl.when(t == 0)
    def _():
        k_s[0:WINDOW, :] = jnp.zeros((WINDOW, KV_W), BF16)
        v_s[0:WINDOW, :] = jnp.zeros((WINDOW, KV_W), BF16)

    x = x_ref[0]
    q, k, v, u, gvn = _in_proj(x, gmix_ref[...], win_ref[...], cos_ref[...], sin_ref[...], gvg_ref[...])
    k_s[WINDOW:, :] = k.astype(BF16)
    v_s[WINDOW:, :] = v.astype(BF16)
    ko_ref[0] = k[tm - WINDOW:]
    vo_ref[0] = v[tm - WINDOW:]
    gvo_ref[0] = gvn[tm - CHUNK:]
    gvb = gvn.astype(BF16)

    wrow = lax.broadcasted_iota(jnp.int32, (CHUNK, 2 * CHUNK), 0)
    wcol = lax.broadcasted_iota(jnp.int32, (CHUNK, 2 * CHUNK), 1) & (CHUNK - 1)
    wmix = [jnp.where(wcol <= wrow, wcat_ref[p], 0.0).astype(BF16) for p in range(G_HEADS // 2)]
    lane = lax.broadcasted_iota(jnp.int32, (CHUNK, LANES), 1)
    low_head = lane < HEAD_DIM

    qi = lax.broadcasted_iota(jnp.int32, (WINDOW, 2 * WINDOW), 0)
    sj = lax.broadcasted_iota(jnp.int32, (WINDOW, 2 * WINDOW), 1)
    dist = WINDOW + qi - sj
    band = (dist >= 0) & (dist < WINDOW)
    first_lo = jnp.where(t == 0, WINDOW, 0)

    for j in range(tm // WINDOW):
        rows = slice(j * WINDOW, (j + 1) * WINDOW)
        mask = band & (sj >= first_lo) if j == 0 else band
        qb = q[rows]
        for kv in range(N_KV):
            heads = [kv * GQA + g for g in range(GQA)]
            qs = jnp.concatenate([qb[:, HEAD_DIM * hd:HEAD_DIM * (hd + 1)] for hd in heads], axis=0)
            kb = k_s[j * WINDOW:(j + 2) * WINDOW, HEAD_DIM * kv:HEAD_DIM * (kv + 1)]
            vb = v_s[j * WINDOW:(j + 2) * WINDOW, HEAD_DIM * kv:HEAD_DIM * (kv + 1)]
            s = lax.dot_general(qs, kb, _NT, preferred_element_type=F32)
            ps, dens = [], []
            for g, hd in enumerate(heads):
                sg = jnp.where(mask, s[g * WINDOW:(g + 1) * WINDOW], -jnp.inf)
                p, den = _softmax_sink(sg, sinks_ref[hd])
                ps.append(p.astype(BF16))
                dens.append(den)
            o = jnp.dot(jnp.concatenate(ps, axis=0), vb, preferred_element_type=F32)
            for g, hd in enumerate(heads):
                ya_s[rows, HEAD_DIM * hd:HEAD_DIM * (hd + 1)] = o[g * WINDOW:(g + 1) * WINDOW] / dens[g]

        mixed = []
        for p in range(G_HEADS // 2):
            r = gvb[rows, LANES * p:LANES * (p + 1)]
            zero = jnp.zeros_like(r)
            rhs = jnp.concatenate([jnp.where(low_head, r, zero), jnp.where(low_head, zero, r)], axis=0)
            mixed.append(jnp.dot(wmix[p], rhs, preferred_element_type=F32))
        yg_s[rows, :] = u[rows] * (jnp.concatenate(mixed, axis=1) + bias_ref[...])

    o_ref[0] = _out_proj(x, ya_s[...], yg_s[...], ga_ref[...], gg_ref[...], wout_ref[...])
    k_s[0:WINDOW, :] = k_s[tm:tm + WINDOW, :]
    v_s[0:WINDOW, :] = v_s[tm:tm + WINDOW, :]


def _mix_prompt(x, cos, sin_signed, gmix, w_in, sinks, gvg, wcat, bias_full, ga, gg, w_out):
    b, s, _ = x.shape
    tm = TOKEN_TILE
    assert s % tm == 0 and tm % WINDOW == 0
    tile_spec = pl.BlockSpec((1, tm, D_MODEL), lambda i, t: (i, t, 0))
    tab_spec = pl.BlockSpec((tm, LANES), lambda i, t: (t, 0))
    last = lambda width: pl.BlockSpec((1, WINDOW, width), lambda i, t: (i, 0, 0))
    return pl.pallas_call(
        functools.partial(_mix_prompt_kernel, tm=tm),
        grid=(b, s // tm),
        in_specs=[tile_spec, tab_spec, tab_spec, _resident((1, D_MODEL)), _resident((D_MODEL, D_IN)),
                  pl.BlockSpec(memory_space=pltpu.SMEM), _resident((1, D_GMLP)),
                  _resident((G_HEADS // 2, CHUNK, 2 * CHUNK)), _resident((CHUNK, D_GMLP)), _resident((1, Q_W)),
                  _resident((1, D_GMLP)), _resident((D_MODEL, D_MODEL))],
        out_specs=[tile_spec, last(KV_W), last(KV_W), last(D_GMLP)],
        out_shape=[jax.ShapeDtypeStruct((b, s, D_MODEL), F32), jax.ShapeDtypeStruct((b, WINDOW, KV_W), F32),
                   jax.ShapeDtypeStruct((b, WINDOW, KV_W), F32), jax.ShapeDtypeStruct((b, CHUNK, D_GMLP), F32)],
        scratch_shapes=[pltpu.VMEM((tm + WINDOW, KV_W), BF16), pltpu.VMEM((tm + WINDOW, KV_W), BF16),
                        pltpu.VMEM((tm, Q_W), F32), pltpu.VMEM((tm, D_GMLP), F32)],
        compiler_params=pltpu.CompilerParams(dimension_semantics=("arbitrary", "arbitrary"),
                                             vmem_limit_bytes=VMEM_LIMIT_BYTES),
        name="mix_prompt",
    )(x, cos, sin_signed, gmix, w_in, sinks, gvg, wcat, bias_full, ga, gg, w_out)


def _mix_sample_kernel(x_ref, ck_ref, cv_ref, cos_ref, sin_ref, gmix_ref, win_ref, sinks_ref, gvg_ref, coef_ref,
                       biasr_ref, ga_ref, gg_ref, wout_ref, o_ref, ko_ref, vo_ref, gvo_ref, q_s, k_s, v_s, ya_s, yg_s,
                       *, t_new, w_buf):
    g = pl.program_id(0)
    n_tok = x_ref.shape[0]
    grp_rows = SEQ_GROUP * t_new
    grp_keys = SEQ_GROUP * w_buf
    sub = 8

    @pl.when(g == 0)
    def _():
        q, k, v, u, gvn = _in_proj(x_ref[...], gmix_ref[...], win_ref[...], cos_ref[...], sin_ref[...], gvg_ref[...])
        q_s[...] = q
        k_s[...] = k
        v_s[...] = v
        gvo_ref[...] = gvn
        g3 = gvn.reshape(n_tok // sub, sub, D_GMLP)
        trow = lax.broadcasted_iota(jnp.int32, (1, sub, D_GMLP), 1) & (t_new - 1)
        mixed = biasr_ref[...][None] + coef_ref[0][None] * g3
        for d in range(1, t_new):
            shifted = jnp.where(trow >= d, pltpu.roll(g3, d, 1), 0.0)
            mixed = mixed + coef_ref[d][None] * shifted
        yg_s[...] = (u.reshape(n_tok // sub, sub, D_GMLP) * mixed).reshape(n_tok, D_GMLP)

    row0 = pl.multiple_of(g * grp_rows, grp_rows)
    qg = q_s[pl.ds(row0, grp_rows), :]
    kn = k_s[pl.ds(row0, grp_rows), :]
    vn = v_s[pl.ds(row0, grp_rows), :]

    ko_ref[:, 0:w_buf - t_new, :] = ck_ref[:, t_new:w_buf, :]
    vo_ref[:, 0:w_buf - t_new, :] = cv_ref[:, t_new:w_buf, :]
    for bl in range(SEQ_GROUP):
        ko_ref[bl, w_buf - t_new:w_buf, :] = kn[t_new * bl:t_new * (bl + 1)]
        vo_ref[bl, w_buf - t_new:w_buf, :] = vn[t_new * bl:t_new * (bl + 1)]

    kc = ck_ref[...].reshape(grp_keys, KV_W).astype(BF16)
    vc = cv_ref[...].reshape(grp_keys, KV_W).astype(BF16)
    knb = kn.astype(BF16)
    vnb = vn.astype(BF16)

    n_rows = GQA * grp_rows
    shift_t = t_new.bit_length() - 1
    shift_w = w_buf.bit_length() - 1
    r = lax.broadcasted_iota(jnp.int32, (n_rows, grp_keys), 0)
    c = lax.broadcasted_iota(jnp.int32, (n_rows, grp_keys), 1)
    mask_c = ((c >> shift_w) == ((r & (grp_rows - 1)) >> shift_t)) & ((c & (w_buf - 1)) > (r & (t_new - 1)))
    r2 = lax.broadcasted_iota(jnp.int32, (n_rows, grp_rows), 0)
    c2 = lax.broadcasted_iota(jnp.int32, (n_rows, grp_rows), 1)
    mask_n = ((c2 >> shift_t) == ((r2 & (grp_rows - 1)) >> shift_t)) & ((c2 & (t_new - 1)) <= (r2 & (t_new - 1)))
    row_head = lax.broadcasted_iota(jnp.int32, (n_rows, 1), 0) >> (grp_rows.bit_length() - 1)

    for kv in range(N_KV):
        heads = [kv * GQA + i for i in range(GQA)]
        lanes = slice(HEAD_DIM * kv, HEAD_DIM * (kv + 1))
        qs = jnp.concatenate([qg[:, HEAD_DIM * hd:HEAD_DIM * (hd + 1)] for hd in heads], axis=0)
        s_c = jnp.where(mask_c, lax.dot_general(qs, kc[:, lanes], _NT, preferred_element_type=F32), -jnp.inf)
        s_n = jnp.where(mask_n, lax.dot_general(qs, knb[:, lanes], _NT, preferred_element_type=F32), -jnp.inf)
        sink = jnp.full((n_rows, 1), sinks_ref[heads[0]], F32)
        for i in range(1, GQA):
            sink = jnp.where(row_head == i, sinks_ref[heads[i]], sink)
        m = jnp.maximum(jnp.maximum(jnp.max(s_c, axis=1, keepdims=True), jnp.max(s_n, axis=1, keepdims=True)), sink)
        p_c = jnp.exp(s_c - m)
        p_n = jnp.exp(s_n - m)
        den = jnp.sum(p_c, axis=1, keepdims=True) + jnp.sum(p_n, axis=1, keepdims=True) + jnp.exp(sink - m)
        o = (jnp.dot(p_c.astype(BF16), vc[:, lanes], preferred_element_type=F32)
             + jnp.dot(p_n.astype(BF16), vnb[:, lanes], preferred_element_type=F32)) / den
        for i, hd in enumerate(heads):
            ya_s[pl.ds(row0, grp_rows), HEAD_DIM * hd:HEAD_DIM * (hd + 1)] = o[i * grp_rows:(i + 1) * grp_rows]

    @pl.when(g == pl.num_programs(0) - 1)
    def _():
        o_ref[...] = _out_proj(x_ref[...], ya_s[...], yg_s[...], ga_ref[...], gg_ref[...], wout_ref[...])


def _mix_sample(x, cache_k, cache_v, cos, sin_signed, gmix, w_in, sinks, gvg, coef, bias_rows, ga, gg, w_out, t_new):
    n_tok = x.shape[0]
    n_seq, w_buf, _ = cache_k.shape
    assert n_seq % SEQ_GROUP == 0 and n_tok == n_seq * t_new
    assert t_new & (t_new - 1) == 0 and w_buf & (w_buf - 1) == 0 and 8 % t_new == 0
    cache_spec = pl.BlockSpec((SEQ_GROUP, w_buf, KV_W), lambda i: (i, 0, 0))
    return pl.pallas_call(
        functools.partial(_mix_sample_kernel, t_new=t_new, w_buf=w_buf),
        grid=(n_seq // SEQ_GROUP,),
        in_specs=[_resident((n_tok, D_MODEL)), cache_spec, cache_spec, _resident((n_tok, LANES)),
                  _resident((n_tok, LANES)), _resident((1, D_MODEL)), _resident((D_MODEL, D_IN)),
                  pl.BlockSpec(memory_space=pltpu.SMEM), _resident((1, D_GMLP)), _resident((t_new, 8, D_GMLP)),
                  _resident((8, D_GMLP)), _resident((1, Q_W)), _resident((1, D_GMLP)),
                  _resident((D_MODEL, D_MODEL))],
        out_specs=[pl.BlockSpec((n_tok, D_MODEL), lambda i: (0, 0)), cache_spec, cache_spec,
                   pl.BlockSpec((n_tok, D_GMLP), lambda i: (0, 0))],
        out_shape=[jax.ShapeDtypeStruct((n_tok, D_MODEL), F32), jax.ShapeDtypeStruct(cache_k.shape, F32),
                   jax.ShapeDtypeStruct(cache_v.shape, F32), jax.ShapeDtypeStruct((n_tok, D_GMLP), F32)],
        scratch_shapes=[pltpu.VMEM((n_tok, Q_W), BF16), pltpu.VMEM((n_tok, KV_W), F32), pltpu.VMEM((n_tok, KV_W), F32),
                        pltpu.VMEM((n_tok, Q_W), F32), pltpu.VMEM((n_tok, D_GMLP), F32)],
        compiler_params=pltpu.CompilerParams(dimension_semantics=("arbitrary",), vmem_limit_bytes=VMEM_LIMIT_BYTES),
        name="mix_sample",
    )(x, cache_k, cache_v, cos, sin_signed, gmix, w_in, sinks, gvg, coef, bias_rows, ga, gg, w_out)


def _rope_tables(pos):
    inv_freq = ROPE_THETA ** (-jnp.arange(0, HEAD_DIM, 2, dtype=F32) / HEAD_DIM)
    ang = pos.astype(F32)[:, None] * inv_freq[None, :]
    c, s = jnp.cos(ang), jnp.sin(ang)
    return jnp.tile(c, (1, 4)), jnp.tile(jnp.concatenate([-s, s], axis=1), (1, 2))


def kernel(x_prompt, x_sample, cache_k_win, cache_v_win, norm_ffn1, ffn1_gate, ffn1_up, ffn1_down, norm_mix, w_in,
           attn_sinks, gmlp_v_norm, gmlp_w_s, gmlp_b_s, norm_attn_out, norm_gmlp_out, w_out, norm_ffn2, ffn2_gate,
           ffn2_up, ffn2_down, norm_final):
    depth = norm_ffn1.shape[0]
    b, s, _ = x_prompt.shape
    bd, t_new, _ = x_sample.shape
    w_buf = cache_k_win.shape[2]

    cos_p, sin_p = _rope_tables(jnp.arange(s, dtype=jnp.int32))
    cos_s, sin_s = _rope_tables(PAST_LEN + jnp.arange(t_new, dtype=jnp.int32))
    cos_s, sin_s = jnp.tile(cos_s, (bd, 1)), jnp.tile(sin_s, (bd, 1))

    hp = x_prompt
    hs = x_sample.reshape(bd * t_new, D_MODEL)
    outs = [[] for _ in range(6)]
    for l in range(depth):
        last = l == depth - 1
        row = lambda a: a[l].reshape(1, -1)
        wg1, wu1, wd1 = (w[l].astype(BF16) for w in (ffn1_gate, ffn1_up, ffn1_down))
        wg2, wu2, wd2 = (w[l].astype(BF16) for w in (ffn2_gate, ffn2_up, ffn2_down))
        w_in_b, w_out_b = w_in[l].astype(BF16), w_out[l].astype(BF16)

        wcat = gmlp_w_s[l].reshape(G_HEADS // 2, 2, CHUNK, CHUNK).transpose(0, 2, 1, 3).reshape(
            G_HEADS // 2, CHUNK, 2 * CHUNK)
        bias_full = jnp.repeat(gmlp_b_s[l].T, HEAD_DIM, axis=1)
        ws_small = gmlp_w_s[l][:, :t_new, :t_new]
        coef = jnp.stack([
            jnp.stack([ws_small[:, t, t - d] if t >= d else jnp.zeros((G_HEADS,), F32) for t in range(t_new)])
            for d in range(t_new)])
        coef = jnp.tile(jnp.repeat(coef, HEAD_DIM, axis=2), (1, 8 // t_new, 1))
        bias_rows = jnp.tile(jnp.repeat(gmlp_b_s[l][:, :t_new].T, HEAD_DIM, axis=1), (8 // t_new, 1))

        hp = _ffn_half(hp.reshape(b * s, D_MODEL), norm_ffn1[l], wg1, wu1, wd1).reshape(b, s, D_MODEL)
        hs = _ffn_half(hs, norm_ffn1[l], wg1, wu1, wd1)

        hp, kp, vp, gvp = _mix_prompt(hp, cos_p, sin_p, row(norm_mix), w_in_b, attn_sinks[l], row(gmlp_v_norm), wcat,
                                      bias_full, row(norm_attn_out), row(norm_gmlp_out), w_out_b)
        hs, ks, vs, gvs = _mix_sample(hs, cache_k_win[l].reshape(bd, w_buf, KV_W),
                                      cache_v_win[l].reshape(bd, w_buf, KV_W), cos_s, sin_s, row(norm_mix), w_in_b,
                                      attn_sinks[l], row(gmlp_v_norm), coef, bias_rows, row(norm_attn_out),
                                      row(norm_gmlp_out), w_out_b, t_new)

        gf = norm_final if last else None
        hp = _ffn_half(hp.reshape(b * s, D_MODEL), norm_ffn2[l], wg2, wu2, wd2, gf).reshape(b, s, D_MODEL)
        hs = _ffn_half(hs, norm_ffn2[l], wg2, wu2, wd2, gf)

        outs[0].append(kp.reshape(b, WINDOW, N_KV, HEAD_DIM))
        outs[1].append(vp.reshape(b, WINDOW, N_KV, HEAD_DIM))
        outs[2].append(ks.reshape(bd, w_buf, N_KV, HEAD_DIM))
        outs[3].append(vs.reshape(bd, w_buf, N_KV, HEAD_DIM))
        outs[4].append(gvp.reshape(b, CHUNK, G_HEADS, D_GMLP // G_HEADS))
        outs[5].append(gvs.reshape(bd, t_new, G_HEADS, D_GMLP // G_HEADS))

    return (hp, hs.reshape(bd, t_new, D_MODEL)) + tuple(jnp.stack(o) for o in outs)
```

```python
import functools

import jax
import jax.numpy as jnp
from jax import lax
from jax.experimental import pallas as pl
from jax.experimental.pallas import tpu as pltpu

F32 = jnp.float32
BF16 = jnp.bfloat16

D_MODEL = 1024
D_FF = 2816
HEAD_DIM = 64
N_HEADS = 8
N_KV = 2
GQA = N_HEADS // N_KV
WINDOW = 128
CHUNK = 128
G_HEADS = 8
Q_W = N_HEADS * HEAD_DIM
KV_W = N_KV * HEAD_DIM
D_GMLP = 512
D_IN = Q_W + 2 * KV_W + 2 * D_GMLP
K_OFF = Q_W
V_OFF = K_OFF + KV_W
U_OFF = V_OFF + KV_W
GV_OFF = U_OFF + D_GMLP
ROPE_THETA = 10000.0
PAST_LEN = 16384
EPS = 1e-6
Q_SCALE = HEAD_DIM ** -0.5
LANES = 128

TOKEN_TILE = 512
FF_CHUNK = 256
SEQ_GROUP = 8
VMEM_LIMIT_BYTES = 56 * 1024 * 1024

_NT = (((1,), (1,)), ((), ()))


def _rms(x, g):
    ms = jnp.mean(x * x, axis=-1, keepdims=True)
    return (x * lax.rsqrt(ms + EPS)) * g


def _resident(shape):
    zeros = (0,) * len(shape)
    return pl.BlockSpec(shape, lambda *_: zeros, pipeline_mode=pl.Buffered(1))


def _ffn_kernel(*refs, final_norm):
    if final_norm:
        x_ref, g_ref, wg_ref, wu_ref, wd_ref, gf_ref, o_ref, act_ref = refs
    else:
        x_ref, g_ref, wg_ref, wu_ref, wd_ref, o_ref, act_ref = refs
    x = x_ref[...]
    h = _rms(x, g_ref[...]).astype(BF16)
    for c in range(D_FF // FF_CHUNK):
        sl = slice(c * FF_CHUNK, (c + 1) * FF_CHUNK)
        gate = jnp.dot(h, wg_ref[:, sl], preferred_element_type=F32)
        up = jnp.dot(h, wu_ref[:, sl], preferred_element_type=F32)
        act_ref[:, sl] = (gate * jax.nn.sigmoid(gate) * up).astype(BF16)
    y = x + 0.5 * jnp.dot(act_ref[...], wd_ref[...], preferred_element_type=F32)
    if final_norm:
        y = _rms(y, gf_ref[...])
    o_ref[...] = y


def _ffn_half(x, g, wg, wu, wd, gf=None):
    n = x.shape[0]
    tm = min(TOKEN_TILE, n)
    assert n % tm == 0
    row_spec = pl.BlockSpec((tm, D_MODEL), lambda i: (i, 0))
    in_specs = [row_spec, _resident((1, D_MODEL)), _resident((D_MODEL, D_FF)), _resident((D_MODEL, D_FF)),
                _resident((D_FF, D_MODEL))]
    args = [x, g.reshape(1, D_MODEL), wg, wu, wd]
    if gf is not None:
        in_specs.append(_resident((1, D_MODEL)))
        args.append(gf.reshape(1, D_MODEL))
    return pl.pallas_call(
        functools.partial(_ffn_kernel, final_norm=gf is not None),
        grid=(n // tm,),
        in_specs=in_specs,
        out_specs=row_spec,
        out_shape=jax.ShapeDtypeStruct((n, D_MODEL), F32),
        scratch_shapes=[pltpu.VMEM((tm, D_FF), BF16)],
        compiler_params=pltpu.CompilerParams(dimension_semantics=("arbitrary",), vmem_limit_bytes=VMEM_LIMIT_BYTES),
        name="ffn_final" if gf is not None else "ffn_half",
    )(*args)


def _rope(xg, cos, sin_signed):
    lane = lax.broadcasted_iota(jnp.int32, xg.shape, 1)
    first_half = (lane & (HEAD_DIM - 1)) < HEAD_DIM // 2
    swapped = jnp.where(first_half, pltpu.roll(xg, LANES - HEAD_DIM // 2, 1), pltpu.roll(xg, HEAD_DIM // 2, 1))
    return xg * cos + swapped * sin_signed


def _in_proj(x, gmix, w_in, cos, sin_signed, gvn_gain):
    h = _rms(x, gmix).astype(BF16)
    z = jnp.dot(h, w_in, preferred_element_type=F32)
    q = jnp.concatenate(
        [(_rope(z[:, LANES * i:LANES * (i + 1)], cos, sin_signed) * Q_SCALE).astype(BF16) for i in range(Q_W // LANES)],
        axis=1)
    k = _rope(z[:, K_OFF:V_OFF], cos, sin_signed)
    v = z[:, V_OFF:U_OFF]
    u = jax.nn.gelu(z[:, U_OFF:GV_OFF])
    gvn = _rms(jax.nn.gelu(z[:, GV_OFF:]), gvn_gain)
    return q, k, v, u, gvn


def _out_proj(x, ya, yg, ga, gg, w_out):
    cat = jnp.concatenate([_rms(ya, ga).astype(BF16), _rms(yg, gg).astype(BF16)], axis=1)
    return x + jnp.dot(cat, w_out, preferred_element_type=F32)


def _softmax_sink(s, sink):
    m = jnp.maximum(jnp.max(s, axis=1, keepdims=True), sink)
    p = jnp.exp(s - m)
    return p, jnp.sum(p, axis=1, keepdims=True) + jnp.exp(sink - m)


def _mix_prompt_kernel(xa_ref, xc_ref, cos_ref, sin_ref, gmix_ref, win_ref, sinks_ref, gvg_ref, wcat_ref, bias_ref,
                       ga_ref, gg_ref, wout_ref, o_ref, ko_ref, vo_ref, gvo_ref, q_s, k_s, v_s, u_s, gv_s, cat_s, *,
                       tm, tiles_per_seq):
    s = pl.program_id(0)
    cur = s % 2
    oth = 1 - cur

    @pl.when(s == 0)
    def _():
        q_s[1] = jnp.zeros((tm, Q_W), BF16)
        k_s[1] = jnp.zeros((tm + WINDOW, KV_W), BF16)
        v_s[1] = jnp.zeros((tm + WINDOW, KV_W), BF16)
        u_s[1] = jnp.zeros((tm, D_GMLP), F32)
        gv_s[1] = jnp.zeros((tm, D_GMLP), BF16)
        cat_s[0] = jnp.zeros((tm, D_MODEL), BF16)

    h = _rms(xa_ref[0], gmix_ref[...]).astype(BF16)
    cos = cos_ref[...]
    sin = sin_ref[...]

    def proj_q():
        z = jnp.dot(h, win_ref[:, 0:Q_W], preferred_element_type=F32)
        for i in range(Q_W // LANES):
            q_s[cur, :, LANES * i:LANES * (i + 1)] = (
                _rope(z[:, LANES * i:LANES * (i + 1)], cos, sin) * Q_SCALE).astype(BF16)

    def proj_kv():
        z = jnp.dot(h, win_ref[:, K_OFF:U_OFF], preferred_element_type=F32)
        k = _rope(z[:, 0:KV_W], cos, sin)
        v = z[:, KV_W:]
        k_s[cur, 0:WINDOW, :] = k_s[oth, tm:tm + WINDOW, :]
        v_s[cur, 0:WINDOW, :] = v_s[oth, tm:tm + WINDOW, :]
        k_s[cur, WINDOW:, :] = k.astype(BF16)
        v_s[cur, WINDOW:, :] = v.astype(BF16)
        ko_ref[0] = k[tm - WINDOW:]
        vo_ref[0] = v[tm - WINDOW:]

    def proj_u():
        u_s[cur] = jax.nn.gelu(jnp.dot(h, win_ref[:, U_OFF:GV_OFF], preferred_element_type=F32))

    def proj_gv():
        gvn = _rms(jax.nn.gelu(jnp.dot(h, win_ref[:, GV_OFF:], preferred_element_type=F32)), gvg_ref[...])
        gv_s[cur] = gvn.astype(BF16)
        gvo_ref[0] = gvn[tm - CHUNK:]

    def out_half(c):
        cols = slice(c * (D_MODEL // 2), (c + 1) * (D_MODEL // 2))

        def run():
            o_ref[0, :, cols] = xc_ref[0, :, cols] + jnp.dot(cat_s[cur], wout_ref[:, cols], preferred_element_type=F32)
        return run

    mid_fill = [[out_half(0)], [proj_q], [proj_kv, proj_u], [proj_gv]]
    end_fill = [[], [], [], [out_half(1)]]
    assert len(mid_fill) == len(end_fill) == tm // WINDOW

    wrow = lax.broadcasted_iota(jnp.int32, (CHUNK, 2 * CHUNK), 0)
    wcol = lax.broadcasted_iota(jnp.int32, (CHUNK, 2 * CHUNK), 1) & (CHUNK - 1)
    wmix = [jnp.where(wcol <= wrow, wcat_ref[p], 0.0).astype(BF16) for p in range(G_HEADS // 2)]
    lane = lax.broadcasted_iota(jnp.int32, (CHUNK, LANES), 1)
    low_head = lane < HEAD_DIM

    qi = lax.broadcasted_iota(jnp.int32, (WINDOW, 2 * WINDOW), 0)
    sj = lax.broadcasted_iota(jnp.int32, (WINDOW, 2 * WINDOW), 1)
    dist = WINDOW + qi - sj
    band = (dist >= 0) & (dist < WINDOW)
    first_lo = jnp.where((s + tiles_per_seq - 1) % tiles_per_seq == 0, WINDOW, 0)

    for j in range(tm // WINDOW):
        rows = slice(j * WINDOW, (j + 1) * WINDOW)
        mask = band & (sj >= first_lo) if j == 0 else band
        qb = q_s[oth, rows, :]
        scores = []
        for kv in range(N_KV):
            qs = jnp.concatenate(
                [qb[:, HEAD_DIM * hd:HEAD_DIM * (hd + 1)] for hd in range(kv * GQA, (kv + 1) * GQA)], axis=0)
            kb = k_s[oth, j * WINDOW:(j + 2) * WINDOW, HEAD_DIM * kv:HEAD_DIM * (kv + 1)]
            scores.append(lax.dot_general(qs, kb, _NT, preferred_element_type=F32))
        for run in mid_fill[j]:
            run()
        outs, dens = [], []
        for kv in range(N_KV):
            vb = v_s[oth, j * WINDOW:(j + 2) * WINDOW, HEAD_DIM * kv:HEAD_DIM * (kv + 1)]
            ps = []
            for g in range(GQA):
                sg = jnp.where(mask, scores[kv][g * WINDOW:(g + 1) * WINDOW], -jnp.inf)
                p, den = _softmax_sink(sg, sinks_ref[kv * GQA + g])
                ps.append(p.astype(BF16))
                dens.append(den)
            outs.append(jnp.dot(jnp.concatenate(ps, axis=0), vb, preferred_element_type=F32))
        mixed = []
        for p in range(G_HEADS // 2):
            r = gv_s[oth, rows, LANES * p:LANES * (p + 1)]
            zero = jnp.zeros_like(r)
            rhs = jnp.concatenate([jnp.where(low_head, r, zero), jnp.where(low_head, zero, r)], axis=0)
            mixed.append(jnp.dot(wmix[p], rhs, preferred_element_type=F32))
        for run in end_fill[j]:
            run()
        ya = [outs[hd // GQA][(hd % GQA) * WINDOW:(hd % GQA + 1) * WINDOW] / dens[hd] for hd in range(N_HEADS)]
        cat_s[oth, rows, 0:Q_W] = _rms(jnp.concatenate(ya, axis=1), ga_ref[...]).astype(BF16)
        yg = u_s[oth, rows, :] * (jnp.concatenate(mixed, axis=1) + bias_ref[...])
        cat_s[oth, rows, Q_W:] = _rms(yg, gg_ref[...]).astype(BF16)


def _mix_prompt(x, cos, sin_signed, gmix, w_in, sinks, gvg, wcat, bias_full, ga, gg, w_out):
    b, s, _ = x.shape
    tm = TOKEN_TILE
    assert s % tm == 0 and tm % WINDOW == 0
    tiles_per_seq = s // tm
    n_tiles = b * tiles_per_seq
    proj_tile = lambda i: jnp.minimum(i, n_tiles - 1)
    out_tile = lambda i: jnp.maximum(i - 2, 0)
    x_tiles = x.reshape(n_tiles, tm, D_MODEL)
    last = lambda width: pl.BlockSpec((1, WINDOW, width), lambda i: (proj_tile(i) // tiles_per_seq, 0, 0))
    tab_spec = pl.BlockSpec((tm, LANES), lambda i: (proj_tile(i) % tiles_per_seq, 0))
    out, ko, vo, gvo = pl.pallas_call(
        functools.partial(_mix_prompt_kernel, tm=tm, tiles_per_seq=tiles_per_seq),
        grid=(n_tiles + 2,),
        in_specs=[pl.BlockSpec((1, tm, D_MODEL), lambda i: (proj_tile(i), 0, 0)),
                  pl.BlockSpec((1, tm, D_MODEL), lambda i: (out_tile(i), 0, 0)),
                  tab_spec, tab_spec, _resident((1, D_MODEL)), _resident((D_MODEL, D_IN)),
                  pl.BlockSpec(memory_space=pltpu.SMEM), _resident((1, D_GMLP)),
                  _resident((G_HEADS // 2, CHUNK, 2 * CHUNK)), _resident((CHUNK, D_GMLP)), _resident((1, Q_W)),
                  _resident((1, D_GMLP)), _resident((D_MODEL, D_MODEL))],
        out_specs=[pl.BlockSpec((1, tm, D_MODEL), lambda i: (out_tile(i), 0, 0)), last(KV_W), last(KV_W),
                   last(D_GMLP)],
        out_shape=[jax.ShapeDtypeStruct((n_tiles, tm, D_MODEL), F32), jax.ShapeDtypeStruct((b, WINDOW, KV_W), F32),
                   jax.ShapeDtypeStruct((b, WINDOW, KV_W), F32), jax.ShapeDtypeStruct((b, CHUNK, D_GMLP), F32)],
        scratch_shapes=[pltpu.VMEM((2, tm, Q_W), BF16), pltpu.VMEM((2, tm + WINDOW, KV_W), BF16),
                        pltpu.VMEM((2, tm + WINDOW, KV_W), BF16), pltpu.VMEM((2, tm, D_GMLP), F32),
                        pltpu.VMEM((2, tm, D_GMLP), BF16), pltpu.VMEM((2, tm, D_MODEL), BF16)],
        compiler_params=pltpu.CompilerParams(dimension_semantics=("arbitrary",), vmem_limit_bytes=VMEM_LIMIT_BYTES),
        name="mix_prompt",
    )(x_tiles, x_tiles, cos, sin_signed, gmix, w_in, sinks, gvg, wcat, bias_full, ga, gg, w_out)
    return out.reshape(b, s, D_MODEL), ko, vo, gvo


def _mix_sample_kernel(x_ref, ck_ref, cv_ref, cos_ref, sin_ref, gmix_ref, win_ref, sinks_ref, gvg_ref, coef_ref,
                       biasr_ref, ga_ref, gg_ref, wout_ref, o_ref, ko_ref, vo_ref, gvo_ref, q_s, k_s, v_s, ya_s, yg_s,
                       *, t_new, w_buf):
    g = pl.program_id(0)
    n_tok = x_ref.shape[0]
    grp_rows = SEQ_GROUP * t_new
    grp_keys = SEQ_GROUP * w_buf
    sub = 8

    @pl.when(g == 0)
    def _():
        q, k, v, u, gvn = _in_proj(x_ref[...], gmix_ref[...], win_ref[...], cos_ref[...], sin_ref[...], gvg_ref[...])
        q_s[...] = q
        k_s[...] = k
        v_s[...] = v
        gvo_ref[...] = gvn
        g3 = gvn.reshape(n_tok // sub, sub, D_GMLP)
        trow = lax.broadcasted_iota(jnp.int32, (1, sub, D_GMLP), 1) & (t_new - 1)
        mixed = biasr_ref[...][None] + coef_ref[0][None] * g3
        for d in range(1, t_new):
            shifted = jnp.where(trow >= d, pltpu.roll(g3, d, 1), 0.0)
            mixed = mixed + coef_ref[d][None] * shifted
        yg_s[...] = (u.reshape(n_tok // sub, sub, D_GMLP) * mixed).reshape(n_tok, D_GMLP)

    row0 = pl.multiple_of(g * grp_rows, grp_rows)
    qg = q_s[pl.ds(row0, grp_rows), :]
    kn = k_s[pl.ds(row0, grp_rows), :]
    vn = v_s[pl.ds(row0, grp_rows), :]

    ko_ref[:, 0:w_buf - t_new, :] = ck_ref[:, t_new:w_buf, :]
    vo_ref[:, 0:w_buf - t_new, :] = cv_ref[:, t_new:w_buf, :]
    for bl in range(SEQ_GROUP):
        ko_ref[bl, w_buf - t_new:w_buf, :] = kn[t_new * bl:t_new * (bl + 1)]
        vo_ref[bl, w_buf - t_new:w_buf, :] = vn[t_new * bl:t_new * (bl + 1)]

    kc = ck_ref[...].reshape(grp_keys, KV_W).astype(BF16)
    vc = cv_ref[...].reshape(grp_keys, KV_W).astype(BF16)
    knb = kn.astype(BF16)
    vnb = vn.astype(BF16)

    n_rows = GQA * grp_rows
    shift_t = t_new.bit_length() - 1
    shift_w = w_buf.bit_length() - 1
    r = lax.broadcasted_iota(jnp.int32, (n_rows, grp_keys), 0)
    c = lax.broadcasted_iota(jnp.int32, (n_rows, grp_keys), 1)
    mask_c = ((c >> shift_w) == ((r & (grp_rows - 1)) >> shift_t)) & ((c & (w_buf - 1)) > (r & (t_new - 1)))
    r2 = lax.broadcasted_iota(jnp.int32, (n_rows, grp_rows), 0)
    c2 = lax.broadcasted_iota(jnp.int32, (n_rows, grp_rows), 1)
    mask_n = ((c2 >> shift_t) == ((r2 & (grp_rows - 1)) >> shift_t)) & ((c2 & (t_new - 1)) <= (r2 & (t_new - 1)))
    row_head = lax.broadcasted_iota(jnp.int32, (n_rows, 1), 0) >> (grp_rows.bit_length() - 1)

    for kv in range(N_KV):
        heads = [kv * GQA + i for i in range(GQA)]
        lanes = slice(HEAD_DIM * kv, HEAD_DIM * (kv + 1))
        qs = jnp.concatenate([qg[:, HEAD_DIM * hd:HEAD_DIM * (hd + 1)] for hd in heads], axis=0)
        s_c = jnp.where(mask_c, lax.dot_general(qs, kc[:, lanes], _NT, preferred_element_type=F32), -jnp.inf)
        s_n = jnp.where(mask_n, lax.dot_general(qs, knb[:, lanes], _NT, preferred_element_type=F32), -jnp.inf)
        sink = jnp.full((n_rows, 1), sinks_ref[heads[0]], F32)
        for i in range(1, GQA):
            sink = jnp.where(row_head == i, sinks_ref[heads[i]], sink)
        m = jnp.maximum(jnp.maximum(jnp.max(s_c, axis=1, keepdims=True), jnp.max(s_n, axis=1, keepdims=True)), sink)
        p_c = jnp.exp(s_c - m)
        p_n = jnp.exp(s_n - m)
        den = jnp.sum(p_c, axis=1, keepdims=True) + jnp.sum(p_n, axis=1, keepdims=True) + jnp.exp(sink - m)
        o = (jnp.dot(p_c.astype(BF16), vc[:, lanes], preferred_element_type=F32)
             + jnp.dot(p_n.astype(BF16), vnb[:, lanes], preferred_element_type=F32)) / den
        for i, hd in enumerate(heads):
            ya_s[pl.ds(row0, grp_rows), HEAD_DIM * hd:HEAD_DIM * (hd + 1)] = o[i * grp_rows:(i + 1) * grp_rows]

    @pl.when(g == pl.num_programs(0) - 1)
    def _():
        o_ref[...] = _out_proj(x_ref[...], ya_s[...], yg_s[...], ga_ref[...], gg_ref[...], wout_ref[...])


def _mix_sample(x, cache_k, cache_v, cos, sin_signed, gmix, w_in, sinks, gvg, coef, bias_rows, ga, gg, w_out, t_new):
    n_tok = x.shape[0]
    n_seq, w_buf, _ = cache_k.shape
    assert n_seq % SEQ_GROUP == 0 and n_tok == n_seq * t_new
    assert t_new & (t_new - 1) == 0 and w_buf & (w_buf - 1) == 0 and 8 % t_new == 0
    cache_spec = pl.BlockSpec((SEQ_GROUP, w_buf, KV_W), lambda i: (i, 0, 0))
    return pl.pallas_call(
        functools.partial(_mix_sample_kernel, t_new=t_new, w_buf=w_buf),
        grid=(n_seq // SEQ_GROUP,),
        in_specs=[_resident((n_tok, D_MODEL)), cache_spec, cache_spec, _resident((n_tok, LANES)),
                  _resident((n_tok, LANES)), _resident((1, D_MODEL)), _resident((D_MODEL, D_IN)),
                  pl.BlockSpec(memory_space=pltpu.SMEM), _resident((1, D_GMLP)), _resident((t_new, 8, D_GMLP)),
                  _resident((8, D_GMLP)), _resident((1, Q_W)), _resident((1, D_GMLP)),
                  _resident((D_MODEL, D_MODEL))],
        out_specs=[pl.BlockSpec((n_tok, D_MODEL), lambda i: (0, 0)), cache_spec, cache_spec,
                   pl.BlockSpec((n_tok, D_GMLP), lambda i: (0, 0))],
        out_shape=[jax.ShapeDtypeStruct((n_tok, D_MODEL), F32), jax.ShapeDtypeStruct(cache_k.shape, F32),
                   jax.ShapeDtypeStruct(cache_v.shape, F32), jax.ShapeDtypeStruct((n_tok, D_GMLP), F32)],
        scratch_shapes=[pltpu.VMEM((n_tok, Q_W), BF16), pltpu.VMEM((n_tok, KV_W), F32), pltpu.VMEM((n_tok, KV_W), F32),
                        pltpu.VMEM((n_tok, Q_W), F32), pltpu.VMEM((n_tok, D_GMLP), F32)],
        compiler_params=pltpu.CompilerParams(dimension_semantics=("arbitrary",), vmem_limit_bytes=VMEM_LIMIT_BYTES),
        name="mix_sample",
    )(x, cache_k, cache_v, cos, sin_signed, gmix, w_in, sinks, gvg, coef, bias_rows, ga, gg, w_out)


def _rope_tables(pos):
    inv_freq = ROPE_THETA ** (-jnp.arange(0, HEAD_DIM, 2, dtype=F32) / HEAD_DIM)
    ang = pos.astype(F32)[:, None] * inv_freq[None, :]
    c, s = jnp.cos(ang), jnp.sin(ang)
    return jnp.tile(c, (1, 4)), jnp.tile(jnp.concatenate([-s, s], axis=1), (1, 2))


def kernel(x_prompt, x_sample, cache_k_win, cache_v_win, norm_ffn1, ffn1_gate, ffn1_up, ffn1_down, norm_mix, w_in,
           attn_sinks, gmlp_v_norm, gmlp_w_s, gmlp_b_s, norm_attn_out, norm_gmlp_out, w_out, norm_ffn2, ffn2_gate,
           ffn2_up, ffn2_down, norm_final):
    depth = norm_ffn1.shape[0]
    b, s, _ = x_prompt.shape
    bd, t_new, _ = x_sample.shape
    w_buf = cache_k_win.shape[2]

    cos_p, sin_p = _rope_tables(jnp.arange(s, dtype=jnp.int32))
    cos_s, sin_s = _rope_tables(PAST_LEN + jnp.arange(t_new, dtype=jnp.int32))
    cos_s, sin_s = jnp.tile(cos_s, (bd, 1)), jnp.tile(sin_s, (bd, 1))

    hp = x_prompt
    hs = x_sample.reshape(bd * t_new, D_MODEL)
    outs = [[] for _ in range(6)]
    for l in range(depth):
        last = l == depth - 1
        row = lambda a: a[l].reshape(1, -1)
        wg1, wu1, wd1 = (w[l].astype(BF16) for w in (ffn1_gate, ffn1_up, ffn1_down))
        wg2, wu2, wd2 = (w[l].astype(BF16) for w in (ffn2_gate, ffn2_up, ffn2_down))
        w_in_b, w_out_b = w_in[l].astype(BF16), w_out[l].astype(BF16)

        wcat = gmlp_w_s[l].reshape(G_HEADS // 2, 2, CHUNK, CHUNK).transpose(0, 2, 1, 3).reshape(
            G_HEADS // 2, CHUNK, 2 * CHUNK)
        bias_full = jnp.repeat(gmlp_b_s[l].T, HEAD_DIM, axis=1)
        ws_small = gmlp_w_s[l][:, :t_new, :t_new]
        coef = jnp.stack([
            jnp.stack([ws_small[:, t, t - d] if t >= d else jnp.zeros((G_HEADS,), F32) for t in range(t_new)])
            for d in range(t_new)])
        coef = jnp.tile(jnp.repeat(coef, HEAD_DIM, axis=2), (1, 8 // t_new, 1))
        bias_rows = jnp.tile(jnp.repeat(gmlp_b_s[l][:, :t_new].T, HEAD_DIM, axis=1), (8 // t_new, 1))

        hp = _ffn_half(hp.reshape(b * s, D_MODEL), norm_ffn1[l], wg1, wu1, wd1).reshape(b, s, D_MODEL)
        hs = _ffn_half(hs, norm_ffn1[l], wg1, wu1, wd1)

        hp, kp, vp, gvp = _mix_prompt(hp, cos_p, sin_p, row(norm_mix), w_in_b, attn_sinks[l], row(gmlp_v_norm), wcat,
                                      bias_full, row(norm_attn_out), row(norm_gmlp_out), w_out_b)
        hs, ks, vs, gvs = _mix_sample(hs, cache_k_win[l].reshape(bd, w_buf, KV_W),
                                      cache_v_win[l].reshape(bd, w_buf, KV_W), cos_s, sin_s, row(norm_mix), w_in_b,
                                      attn_sinks[l], row(gmlp_v_norm), coef, bias_rows, row(norm_attn_out),
                                      row(norm_gmlp_out), w_out_b, t_new)

        gf = norm_final if last else None
        hp = _ffn_half(hp.reshape(b * s, D_MODEL), norm_ffn2[l], wg2, wu2, wd2, gf).reshape(b, s, D_MODEL)
        hs = _ffn_half(hs, norm_ffn2[l], wg2, wu2, wd2, gf)

        outs[0].append(kp.reshape(b, WINDOW, N_KV, HEAD_DIM))
        outs[1].append(vp.reshape(b, WINDOW, N_KV, HEAD_DIM))
        outs[2].append(ks.reshape(bd, w_buf, N_KV, HEAD_DIM))
        outs[3].append(vs.reshape(bd, w_buf, N_KV, HEAD_DIM))
        outs[4].append(gvp.reshape(b, CHUNK, G_HEADS, D_GMLP // G_HEADS))
        outs[5].append(gvs.reshape(bd, t_new, G_HEADS, D_GMLP // G_HEADS))

    return (hp, hs.reshape(bd, t_new, D_MODEL)) + tuple(jnp.stack(o) for o in outs)
```

```python
import functools

import jax
import jax.numpy as jnp
from jax import lax
from jax.experimental import pallas as pl
from jax.experimental.pallas import tpu as pltpu

F32 = jnp.float32
BF16 = jnp.bfloat16

D_MODEL = 1024
D_FF = 2816
HEAD_DIM = 64
N_HEADS = 8
N_KV = 2
GQA = N_HEADS // N_KV
WINDOW = 128
CHUNK = 128
G_HEADS = 8
Q_W = N_HEADS * HEAD_DIM
KV_W = N_KV * HEAD_DIM
D_GMLP = 512
D_IN = Q_W + 2 * KV_W + 2 * D_GMLP
K_OFF = Q_W
V_OFF = K_OFF + KV_W
U_OFF = V_OFF + KV_W
GV_OFF = U_OFF + D_GMLP
ROPE_THETA = 10000.0
PAST_LEN = 16384
EPS = 1e-6
Q_SCALE = HEAD_DIM ** -0.5
LANES = 128

TOKEN_TILE = 512
FF_CHUNK = 256
SEQ_GROUP = 8
VMEM_LIMIT_BYTES = 56 * 1024 * 1024

_NT = (((1,), (1,)), ((), ()))


def _rms(x, g):
    ms = jnp.mean(x * x, axis=-1, keepdims=True)
    return (x * lax.rsqrt(ms + EPS)) * g


def _resident(shape):
    zeros = (0,) * len(shape)
    return pl.BlockSpec(shape, lambda *_: zeros, pipeline_mode=pl.Buffered(1))


def _ffn_kernel(*refs, final_norm):
    if final_norm:
        x_ref, g_ref, wg_ref, wu_ref, wd_ref, gf_ref, o_ref, act_ref = refs
    else:
        x_ref, g_ref, wg_ref, wu_ref, wd_ref, o_ref, act_ref = refs
    x = x_ref[...]
    h = _rms(x, g_ref[...]).astype(BF16)
    for c in range(D_FF // FF_CHUNK):
        sl = slice(c * FF_CHUNK, (c + 1) * FF_CHUNK)
        gate = jnp.dot(h, wg_ref[:, sl], preferred_element_type=F32)
        up = jnp.dot(h, wu_ref[:, sl], preferred_element_type=F32)
        act_ref[:, sl] = (gate * jax.nn.sigmoid(gate) * up).astype(BF16)
    y = x + 0.5 * jnp.dot(act_ref[...], wd_ref[...], preferred_element_type=F32)
    if final_norm:
        y = _rms(y, gf_ref[...])
    o_ref[...] = y


def _ffn_half(x, g, wg, wu, wd, gf=None):
    n = x.shape[0]
    tm = min(TOKEN_TILE, n)
    assert n % tm == 0
    row_spec = pl.BlockSpec((tm, D_MODEL), lambda i: (i, 0))
    in_specs = [row_spec, _resident((1, D_MODEL)), _resident((D_MODEL, D_FF)), _resident((D_MODEL, D_FF)),
                _resident((D_FF, D_MODEL))]
    args = [x, g.reshape(1, D_MODEL), wg, wu, wd]
    if gf is not None:
        in_specs.append(_resident((1, D_MODEL)))
        args.append(gf.reshape(1, D_MODEL))
    return pl.pallas_call(
        functools.partial(_ffn_kernel, final_norm=gf is not None),
        grid=(n // tm,),
        in_specs=in_specs,
        out_specs=row_spec,
        out_shape=jax.ShapeDtypeStruct((n, D_MODEL), F32),
        scratch_shapes=[pltpu.VMEM((tm, D_FF), BF16)],
        compiler_params=pltpu.CompilerParams(dimension_semantics=("arbitrary",), vmem_limit_bytes=VMEM_LIMIT_BYTES),
        name="ffn_final" if gf is not None else "ffn_half",
    )(*args)


def _rope(xg, cos, sin_signed):
    lane = lax.broadcasted_iota(jnp.int32, xg.shape, 1)
    first_half = (lane & (HEAD_DIM - 1)) < HEAD_DIM // 2
    swapped = jnp.where(first_half, pltpu.roll(xg, LANES - HEAD_DIM // 2, 1), pltpu.roll(xg, HEAD_DIM // 2, 1))
    return xg * cos + swapped * sin_signed


def _in_proj(x, gmix, w_in, cos, sin_signed, gvn_gain):
    h = _rms(x, gmix).astype(BF16)
    z = jnp.dot(h, w_in, preferred_element_type=F32)
    q = jnp.concatenate(
        [(_rope(z[:, LANES * i:LANES * (i + 1)], cos, sin_signed) * Q_SCALE).astype(BF16) for i in range(Q_W // LANES)],
        axis=1)
    k = _rope(z[:, K_OFF:V_OFF], cos, sin_signed)
    v = z[:, V_OFF:U_OFF]
    u = jax.nn.gelu(z[:, U_OFF:GV_OFF])
    gvn = _rms(jax.nn.gelu(z[:, GV_OFF:]), gvn_gain)
    return q, k, v, u, gvn


def _out_proj(x, ya, yg, ga, gg, w_out):
    cat = jnp.concatenate([_rms(ya, ga).astype(BF16), _rms(yg, gg).astype(BF16)], axis=1)
    return x + jnp.dot(cat, w_out, preferred_element_type=F32)


def _softmax_sink(s, sink):
    m = jnp.maximum(jnp.max(s, axis=1, keepdims=True), sink)
    p = jnp.exp(s - m)
    return p, jnp.sum(p, axis=1, keepdims=True) + jnp.exp(sink - m)


def _mix_prompt_kernel(xa_ref, xc_ref, cos_ref, sin_ref, gmix_ref, win_ref, sinks_ref, gvg_ref, wcat_ref, bias_ref,
                       ga_ref, gg_ref, wout_ref, o_ref, ko_ref, vo_ref, gvo_ref, q_s, k_s, v_s, u_s, gv_s, cat_s, *,
                       tm, tiles_per_seq):
    s = pl.program_id(0)

    @pl.when(s == 0)
    def _():
        q_s[1] = jnp.zeros((tm, Q_W), BF16)
        k_s[1] = jnp.zeros((tm + WINDOW, KV_W), BF16)
        v_s[1] = jnp.zeros((tm + WINDOW, KV_W), BF16)
        u_s[1] = jnp.zeros((tm, D_GMLP), F32)
        gv_s[1] = jnp.zeros((tm, D_GMLP), BF16)
        cat_s[0] = jnp.zeros((tm, D_MODEL), BF16)

    for parity in range(2):
        pl.when(s % 2 == parity)(functools.partial(
            _mix_prompt_step, s, parity, xa_ref, xc_ref, cos_ref, sin_ref, gmix_ref, win_ref, sinks_ref, gvg_ref,
            wcat_ref, bias_ref, ga_ref, gg_ref, wout_ref, o_ref, ko_ref, vo_ref, gvo_ref, q_s, k_s, v_s, u_s, gv_s,
            cat_s, tm=tm, tiles_per_seq=tiles_per_seq))


def _mix_prompt_step(s, cur, xa_ref, xc_ref, cos_ref, sin_ref, gmix_ref, win_ref, sinks_ref, gvg_ref, wcat_ref, bias_ref,
                     ga_ref, gg_ref, wout_ref, o_ref, ko_ref, vo_ref, gvo_ref, q_s, k_s, v_s, u_s, gv_s, cat_s, *, tm,
                     tiles_per_seq):
    oth = 1 - cur

    h = _rms(xa_ref[0], gmix_ref[...]).astype(BF16)
    cos = cos_ref[...]
    sin = sin_ref[...]

    def proj_q():
        z = jnp.dot(h, win_ref[:, 0:Q_W], preferred_element_type=F32)
        for i in range(Q_W // LANES):
            q_s[cur, :, LANES * i:LANES * (i + 1)] = (
                _rope(z[:, LANES * i:LANES * (i + 1)], cos, sin) * Q_SCALE).astype(BF16)

    def proj_kv():
        z = jnp.dot(h, win_ref[:, K_OFF:U_OFF], preferred_element_type=F32)
        k = _rope(z[:, 0:KV_W], cos, sin)
        v = z[:, KV_W:]
        k_s[cur, 0:WINDOW, :] = k_s[oth, tm:tm + WINDOW, :]
        v_s[cur, 0:WINDOW, :] = v_s[oth, tm:tm + WINDOW, :]
        k_s[cur, WINDOW:, :] = k.astype(BF16)
        v_s[cur, WINDOW:, :] = v.astype(BF16)
        ko_ref[0] = k[tm - WINDOW:]
        vo_ref[0] = v[tm - WINDOW:]

    def proj_u():
        u_s[cur] = jax.nn.gelu(jnp.dot(h, win_ref[:, U_OFF:GV_OFF], preferred_element_type=F32))

    def proj_gv():
        gvn = _rms(jax.nn.gelu(jnp.dot(h, win_ref[:, GV_OFF:], preferred_element_type=F32)), gvg_ref[...])
        gv_s[cur] = gvn.astype(BF16)
        gvo_ref[0] = gvn[tm - CHUNK:]

    def out_half(c):
        cols = slice(c * (D_MODEL // 2), (c + 1) * (D_MODEL // 2))

        def run():
            o_ref[0, :, cols] = xc_ref[0, :, cols] + jnp.dot(cat_s[cur], wout_ref[:, cols], preferred_element_type=F32)
        return run

    mid_fill = [[out_half(0)], [proj_q], [proj_kv, proj_u], [proj_gv]]
    end_fill = [[], [], [], [out_half(1)]]
    assert len(mid_fill) == len(end_fill) == tm // WINDOW

    wrow = lax.broadcasted_iota(jnp.int32, (CHUNK, 2 * CHUNK), 0)
    wcol = lax.broadcasted_iota(jnp.int32, (CHUNK, 2 * CHUNK), 1) & (CHUNK - 1)
    wmix = [jnp.where(wcol <= wrow, wcat_ref[p], 0.0).astype(BF16) for p in range(G_HEADS // 2)]
    lane = lax.broadcasted_iota(jnp.int32, (CHUNK, LANES), 1)
    low_head = lane < HEAD_DIM

    qi = lax.broadcasted_iota(jnp.int32, (WINDOW, 2 * WINDOW), 0)
    sj = lax.broadcasted_iota(jnp.int32, (WINDOW, 2 * WINDOW), 1)
    dist = WINDOW + qi - sj
    band = (dist >= 0) & (dist < WINDOW)
    first_lo = jnp.where((s + tiles_per_seq - 1) % tiles_per_seq == 0, WINDOW, 0)

    for j in range(tm // WINDOW):
        rows = slice(j * WINDOW, (j + 1) * WINDOW)
        mask = band & (sj >= first_lo) if j == 0 else band
        qb = q_s[oth, rows, :]
        scores = []
        for kv in range(N_KV):
            qs = jnp.concatenate(
                [qb[:, HEAD_DIM * hd:HEAD_DIM * (hd + 1)] for hd in range(kv * GQA, (kv + 1) * GQA)], axis=0)
            kb = k_s[oth, j * WINDOW:(j + 2) * WINDOW, HEAD_DIM * kv:HEAD_DIM * (kv + 1)]
            scores.append(lax.dot_general(qs, kb, _NT, preferred_element_type=F32))
        for run in mid_fill[j]:
            run()
        outs, dens = [], []
        for kv in range(N_KV):
            vb = v_s[oth, j * WINDOW:(j + 2) * WINDOW, HEAD_DIM * kv:HEAD_DIM * (kv + 1)]
            ps = []
            for g in range(GQA):
                sg = jnp.where(mask, scores[kv][g * WINDOW:(g + 1) * WINDOW], -jnp.inf)
                p, den = _softmax_sink(sg, sinks_ref[kv * GQA + g])
                ps.append(p.astype(BF16))
                dens.append(den)
            outs.append(jnp.dot(jnp.concatenate(ps, axis=0), vb, preferred_element_type=F32))
        mixed = []
        for p in range(G_HEADS // 2):
            r = gv_s[oth, rows, LANES * p:LANES * (p + 1)]
            zero = jnp.zeros_like(r)
            rhs = jnp.concatenate([jnp.where(low_head, r, zero), jnp.where(low_head, zero, r)], axis=0)
            mixed.append(jnp.dot(wmix[p], rhs, preferred_element_type=F32))
        for run in end_fill[j]:
            run()
        ya = [outs[hd // GQA][(hd % GQA) * WINDOW:(hd % GQA + 1) * WINDOW] / dens[hd] for hd in range(N_HEADS)]
        cat_s[oth, rows, 0:Q_W] = _rms(jnp.concatenate(ya, axis=1), ga_ref[...]).astype(BF16)
        yg = u_s[oth, rows, :] * (jnp.concatenate(mixed, axis=1) + bias_ref[...])
        cat_s[oth, rows, Q_W:] = _rms(yg, gg_ref[...]).astype(BF16)


def _mix_prompt(x, cos, sin_signed, gmix, w_in, sinks, gvg, wcat, bias_full, ga, gg, w_out):
    b, s, _ = x.shape
    tm = TOKEN_TILE
    assert s % tm == 0 and tm % WINDOW == 0
    tiles_per_seq = s // tm
    n_tiles = b * tiles_per_seq
    proj_tile = lambda i: jnp.minimum(i, n_tiles - 1)
    out_tile = lambda i: jnp.maximum(i - 2, 0)
    x_tiles = x.reshape(n_tiles, tm, D_MODEL)
    last = lambda width: pl.BlockSpec((1, WINDOW, width), lambda i: (proj_tile(i) // tiles_per_seq, 0, 0))
    tab_spec = pl.BlockSpec((tm, LANES), lambda i: (proj_tile(i) % tiles_per_seq, 0))
    out, ko, vo, gvo = pl.pallas_call(
        functools.partial(_mix_prompt_kernel, tm=tm, tiles_per_seq=tiles_per_seq),
        grid=(n_tiles + 2,),
        in_specs=[pl.BlockSpec((1, tm, D_MODEL), lambda i: (proj_tile(i), 0, 0)),
                  pl.BlockSpec((1, tm, D_MODEL), lambda i: (out_tile(i), 0, 0)),
                  tab_spec, tab_spec, _resident((1, D_MODEL)), _resident((D_MODEL, D_IN)),
                  pl.BlockSpec(memory_space=pltpu.SMEM), _resident((1, D_GMLP)),
                  _resident((G_HEADS // 2, CHUNK, 2 * CHUNK)), _resident((CHUNK, D_GMLP)), _resident((1, Q_W)),
                  _resident((1, D_GMLP)), _resident((D_MODEL, D_MODEL))],
        out_specs=[pl.BlockSpec((1, tm, D_MODEL), lambda i: (out_tile(i), 0, 0)), last(KV_W), last(KV_W),
                   last(D_GMLP)],
        out_shape=[jax.ShapeDtypeStruct((n_tiles, tm, D_MODEL), F32), jax.ShapeDtypeStruct((b, WINDOW, KV_W), F32),
                   jax.ShapeDtypeStruct((b, WINDOW, KV_W), F32), jax.ShapeDtypeStruct((b, CHUNK, D_GMLP), F32)],
        scratch_shapes=[pltpu.VMEM((2, tm, Q_W), BF16), pltpu.VMEM((2, tm + WINDOW, KV_W), BF16),
                        pltpu.VMEM((2, tm + WINDOW, KV_W), BF16), pltpu.VMEM((2, tm, D_GMLP), F32),
                        pltpu.VMEM((2, tm, D_GMLP), BF16), pltpu.VMEM((2, tm, D_MODEL), BF16)],
        compiler_params=pltpu.CompilerParams(dimension_semantics=("arbitrary",), vmem_limit_bytes=VMEM_LIMIT_BYTES),
        name="mix_prompt",
    )(x_tiles, x_tiles, cos, sin_signed, gmix, w_in, sinks, gvg, wcat, bias_full, ga, gg, w_out)
    return out.reshape(b, s, D_MODEL), ko, vo, gvo


def _mix_sample_kernel(x_ref, ck_ref, cv_ref, cos_ref, sin_ref, gmix_ref, win_ref, sinks_ref, gvg_ref, coef_ref,
                       biasr_ref, ga_ref, gg_ref, wout_ref, o_ref, ko_ref, vo_ref, gvo_ref, q_s, k_s, v_s, ya_s, yg_s,
                       *, t_new, w_buf):
    g = pl.program_id(0)
    n_tok = x_ref.shape[0]
    grp_rows = SEQ_GROUP * t_new
    grp_keys = SEQ_GROUP * w_buf
    sub = 8

    @pl.when(g == 0)
    def _():
        q, k, v, u, gvn = _in_proj(x_ref[...], gmix_ref[...], win_ref[...], cos_ref[...], sin_ref[...], gvg_ref[...])
        q_s[...] = q
        k_s[...] = k
        v_s[...] = v
        gvo_ref[...] = gvn
        g3 = gvn.reshape(n_tok // sub, sub, D_GMLP)
        trow = lax.broadcasted_iota(jnp.int32, (1, sub, D_GMLP), 1) & (t_new - 1)
        mixed = biasr_ref[...][None] + coef_ref[0][None] * g3
        for d in range(1, t_new):
            shifted = jnp.where(trow >= d, pltpu.roll(g3, d, 1), 0.0)
            mixed = mixed + coef_ref[d][None] * shifted
        yg_s[...] = (u.reshape(n_tok // sub, sub, D_GMLP) * mixed).reshape(n_tok, D_GMLP)

    row0 = pl.multiple_of(g * grp_rows, grp_rows)
    qg = q_s[pl.ds(row0, grp_rows), :]
    kn = k_s[pl.ds(row0, grp_rows), :]
    vn = v_s[pl.ds(row0, grp_rows), :]

    ko_ref[:, 0:w_buf - t_new, :] = ck_ref[:, t_new:w_buf, :]
    vo_ref[:, 0:w_buf - t_new, :] = cv_ref[:, t_new:w_buf, :]
    for bl in range(SEQ_GROUP):
        ko_ref[bl, w_buf - t_new:w_buf, :] = kn[t_new * bl:t_new * (bl + 1)]
        vo_ref[bl, w_buf - t_new:w_buf, :] = vn[t_new * bl:t_new * (bl + 1)]

    kc = ck_ref[...].reshape(grp_keys, KV_W).astype(BF16)
    vc = cv_ref[...].reshape(grp_keys, KV_W).astype(BF16)
    knb = kn.astype(BF16)
    vnb = vn.astype(BF16)

    n_rows = GQA * grp_rows
    shift_t = t_new.bit_length() - 1
    shift_w = w_buf.bit_length() - 1
    r = lax.broadcasted_iota(jnp.int32, (n_rows, grp_keys), 0)
    c = lax.broadcasted_iota(jnp.int32, (n_rows, grp_keys), 1)
    mask_c = ((c >> shift_w) == ((r & (grp_rows - 1)) >> shift_t)) & ((c & (w_buf - 1)) > (r & (t_new - 1)))
    r2 = lax.broadcasted_iota(jnp.int32, (n_rows, grp_rows), 0)
    c2 = lax.broadcasted_iota(jnp.int32, (n_rows, grp_rows), 1)
    mask_n = ((c2 >> shift_t) == ((r2 & (grp_rows - 1)) >> shift_t)) & ((c2 & (t_new - 1)) <= (r2 & (t_new - 1)))
    row_head = lax.broadcasted_iota(jnp.int32, (n_rows, 1), 0) >> (grp_rows.bit_length() - 1)

    for kv in range(N_KV):
        heads = [kv * GQA + i for i in range(GQA)]
        lanes = slice(HEAD_DIM * kv, HEAD_DIM * (kv + 1))
        qs = jnp.concatenate([qg[:, HEAD_DIM * hd:HEAD_DIM * (hd + 1)] for hd in heads], axis=0)
        s_c = jnp.where(mask_c, lax.dot_general(qs, kc[:, lanes], _NT, preferred_element_type=F32), -jnp.inf)
        s_n = jnp.where(mask_n, lax.dot_general(qs, knb[:, lanes], _NT, preferred_element_type=F32), -jnp.inf)
        sink = jnp.full((n_rows, 1), sinks_ref[heads[0]], F32)
        for i in range(1, GQA):
            sink = jnp.where(row_head == i, sinks_ref[heads[i]], sink)
        m = jnp.maximum(jnp.maximum(jnp.max(s_c, axis=1, keepdims=True), jnp.max(s_n, axis=1, keepdims=True)), sink)
        p_c = jnp.exp(s_c - m)
        p_n = jnp.exp(s_n - m)
        den = jnp.sum(p_c, axis=1, keepdims=True) + jnp.sum(p_n, axis=1, keepdims=True) + jnp.exp(sink - m)
        o = (jnp.dot(p_c.astype(BF16), vc[:, lanes], preferred_element_type=F32)
             + jnp.dot(p_n.astype(BF16), vnb[:, lanes], preferred_element_type=F32)) / den
        for i, hd in enumerate(heads):
            ya_s[pl.ds(row0, grp_rows), HEAD_DIM * hd:HEAD_DIM * (hd + 1)] = o[i * grp_rows:(i + 1) * grp_rows]

    @pl.when(g == pl.num_programs(0) - 1)
    def _():
        o_ref[...] = _out_proj(x_ref[...], ya_s[...], yg_s[...], ga_ref[...], gg_ref[...], wout_ref[...])


def _mix_sample(x, cache_k, cache_v, cos, sin_signed, gmix, w_in, sinks, gvg, coef, bias_rows, ga, gg, w_out, t_new):
    n_tok = x.shape[0]
    n_seq, w_buf, _ = cache_k.shape
    assert n_seq % SEQ_GROUP == 0 and n_tok == n_seq * t_new
    assert t_new & (t_new - 1) == 0 and w_buf & (w_buf - 1) == 0 and 8 % t_new == 0
    cache_spec = pl.BlockSpec((SEQ_GROUP, w_buf, KV_W), lambda i: (i, 0, 0))
    return pl.pallas_call(
        functools.partial(_mix_sample_kernel, t_new=t_new, w_buf=w_buf),
        grid=(n_seq // SEQ_GROUP,),
        in_specs=[_resident((n_tok, D_MODEL)), cache_spec, cache_spec, _resident((n_tok, LANES)),
                  _resident((n_tok, LANES)), _resident((1, D_MODEL)), _resident((D_MODEL, D_IN)),
                  pl.BlockSpec(memory_space=pltpu.SMEM), _resident((1, D_GMLP)), _resident((t_new, 8, D_GMLP)),
                  _resident((8, D_GMLP)), _resident((1, Q_W)), _resident((1, D_GMLP)),
                  _resident((D_MODEL, D_MODEL))],
        out_specs=[pl.BlockSpec((n_tok, D_MODEL), lambda i: (0, 0)), cache_spec, cache_spec,
                   pl.BlockSpec((n_tok, D_GMLP), lambda i: (0, 0))],
        out_shape=[jax.ShapeDtypeStruct((n_tok, D_MODEL), F32), jax.ShapeDtypeStruct(cache_k.shape, F32),
                   jax.ShapeDtypeStruct(cache_v.shape, F32), jax.ShapeDtypeStruct((n_tok, D_GMLP), F32)],
        scratch_shapes=[pltpu.VMEM((n_tok, Q_W), BF16), pltpu.VMEM((n_tok, KV_W), F32), pltpu.VMEM((n_tok, KV_W), F32),
                        pltpu.VMEM((n_tok, Q_W), F32), pltpu.VMEM((n_tok, D_GMLP), F32)],
        compiler_params=pltpu.CompilerParams(dimension_semantics=("arbitrary",), vmem_limit_bytes=VMEM_LIMIT_BYTES),
        name="mix_sample",
    )(x, cache_k, cache_v, cos, sin_signed, gmix, w_in, sinks, gvg, coef, bias_rows, ga, gg, w_out)


def _rope_tables(pos):
    inv_freq = ROPE_THETA ** (-jnp.arange(0, HEAD_DIM, 2, dtype=F32) / HEAD_DIM)
    ang = pos.astype(F32)[:, None] * inv_freq[None, :]
    c, s = jnp.cos(ang), jnp.sin(ang)
    return jnp.tile(c, (1, 4)), jnp.tile(jnp.concatenate([-s, s], axis=1), (1, 2))


def kernel(x_prompt, x_sample, cache_k_win, cache_v_win, norm_ffn1, ffn1_gate, ffn1_up, ffn1_down, norm_mix, w_in,
           attn_sinks, gmlp_v_norm, gmlp_w_s, gmlp_b_s, norm_attn_out, norm_gmlp_out, w_out, norm_ffn2, ffn2_gate,
           ffn2_up, ffn2_down, norm_final):
    depth = norm_ffn1.shape[0]
    b, s, _ = x_prompt.shape
    bd, t_new, _ = x_sample.shape
    w_buf = cache_k_win.shape[2]

    cos_p, sin_p = _rope_tables(jnp.arange(s, dtype=jnp.int32))
    cos_s, sin_s = _rope_tables(PAST_LEN + jnp.arange(t_new, dtype=jnp.int32))
    cos_s, sin_s = jnp.tile(cos_s, (bd, 1)), jnp.tile(sin_s, (bd, 1))

    hp = x_prompt
    hs = x_sample.reshape(bd * t_new, D_MODEL)
    outs = [[] for _ in range(6)]
    for l in range(depth):
        last = l == depth - 1
        row = lambda a: a[l].reshape(1, -1)
        wg1, wu1, wd1 = (w[l].astype(BF16) for w in (ffn1_gate, ffn1_up, ffn1_down))
        wg2, wu2, wd2 = (w[l].astype(BF16) for w in (ffn2_gate, ffn2_up, ffn2_down))
        w_in_b, w_out_b = w_in[l].astype(BF16), w_out[l].astype(BF16)

        wcat = gmlp_w_s[l].reshape(G_HEADS // 2, 2, CHUNK, CHUNK).transpose(0, 2, 1, 3).reshape(
            G_HEADS // 2, CHUNK, 2 * CHUNK)
        bias_full = jnp.repeat(gmlp_b_s[l].T, HEAD_DIM, axis=1)
        ws_small = gmlp_w_s[l][:, :t_new, :t_new]
        coef = jnp.stack([
            jnp.stack([ws_small[:, t, t - d] if t >= d else jnp.zeros((G_HEADS,), F32) for t in range(t_new)])
            for d in range(t_new)])
        coef = jnp.tile(jnp.repeat(coef, HEAD_DIM, axis=2), (1, 8 // t_new, 1))
        bias_rows = jnp.tile(jnp.repeat(gmlp_b_s[l][:, :t_new].T, HEAD_DIM, axis=1), (8 // t_new, 1))

        hp = _ffn_half(hp.reshape(b * s, D_MODEL), norm_ffn1[l], wg1, wu1, wd1).reshape(b, s, D_MODEL)
        hs = _ffn_half(hs, norm_ffn1[l], wg1, wu1, wd1)

        hp, kp, vp, gvp = _mix_prompt(hp, cos_p, sin_p, row(norm_mix), w_in_b, attn_sinks[l], row(gmlp_v_norm), wcat,
                                      bias_full, row(norm_attn_out), row(norm_gmlp_out), w_out_b)
        hs, ks, vs, gvs = _mix_sample(hs, cache_k_win[l].reshape(bd, w_buf, KV_W),
                                      cache_v_win[l].reshape(bd, w_buf, KV_W), cos_s, sin_s, row(norm_mix), w_in_b,
                                      attn_sinks[l], row(gmlp_v_norm), coef, bias_rows, row(norm_attn_out),
                                      row(norm_gmlp_out), w_out_b, t_new)

        gf = norm_final if last else None
        hp = _ffn_half(hp.reshape(b * s, D_MODEL), norm_ffn2[l], wg2, wu2, wd2, gf).reshape(b, s, D_MODEL)
        hs = _ffn_half(hs, norm_ffn2[l], wg2, wu2, wd2, gf)

        outs[0].append(kp.reshape(b, WINDOW, N_KV, HEAD_DIM))
        outs[1].append(vp.reshape(b, WINDOW, N_KV, HEAD_DIM))
        outs[2].append(ks.reshape(bd, w_buf, N_KV, HEAD_DIM))
        outs[3].append(vs.reshape(bd, w_buf, N_KV, HEAD_DIM))
        outs[4].append(gvp.reshape(b, CHUNK, G_HEADS, D_GMLP // G_HEADS))
        outs[5].append(gvs.reshape(bd, t_new, G_HEADS, D_GMLP // G_HEADS))

    return (hp, hs.reshape(bd, t_new, D_MODEL)) + tuple(jnp.stack(o) for o in outs)
```

```python
import functools

import jax
import jax.numpy as jnp
from jax import lax
from jax.experimental import pallas as pl
from jax.experimental.pallas import tpu as pltpu

F32 = jnp.float32
BF16 = jnp.bfloat16

D_MODEL = 1024
D_FF = 2816
HEAD_DIM = 64
N_HEADS = 8
N_KV = 2
GQA = N_HEADS // N_KV
WINDOW = 128
CHUNK = 128
G_HEADS = 8
Q_W = N_HEADS * HEAD_DIM
KV_W = N_KV * HEAD_DIM
D_GMLP = 512
D_IN = Q_W + 2 * KV_W + 2 * D_GMLP
K_OFF = Q_W
V_OFF = K_OFF + KV_W
U_OFF = V_OFF + KV_W
GV_OFF = U_OFF + D_GMLP
ROPE_THETA = 10000.0
PAST_LEN = 16384
EPS = 1e-6
Q_SCALE = HEAD_DIM ** -0.5
LANES = 128
BF16_SUBLANES = 16

TOKEN_TILE = 512
FF_CHUNK = 256
SEQ_GROUP = 8
VMEM_LIMIT_BYTES = 56 * 1024 * 1024

_NT = (((1,), (1,)), ((), ()))


def _rms(x, g):
    ms = jnp.mean(x * x, axis=-1, keepdims=True)
    return (x * lax.rsqrt(ms + EPS)) * g


def _resident(shape):
    zeros = (0,) * len(shape)
    return pl.BlockSpec(shape, lambda *_: zeros, pipeline_mode=pl.Buffered(1))


def _ffn_kernel(*refs, final_norm, n_cast):
    refs = iter(refs)
    xp_ref, xs_ref, g_ref, wg_ref, wu_ref, wd_ref = (next(refs) for _ in range(6))
    gf_ref = next(refs) if final_norm else None
    cast_in = [next(refs) for _ in range(n_cast)]
    yp_ref, ys_ref = next(refs), next(refs)
    cast_out = [next(refs) for _ in range(n_cast)]
    act_ref = next(refs)

    on_sample = pl.program_id(0) == 0
    x = jnp.where(on_sample, xs_ref[...], xp_ref[...])
    h = _rms(x, g_ref[...]).astype(BF16)
    for c in range(D_FF // FF_CHUNK):
        sl = slice(c * FF_CHUNK, (c + 1) * FF_CHUNK)
        gate = jnp.dot(h, wg_ref[:, sl], preferred_element_type=F32)
        up = jnp.dot(h, wu_ref[:, sl], preferred_element_type=F32)
        act_ref[:, sl] = (gate * jax.nn.sigmoid(gate) * up).astype(BF16)
    y = x + 0.5 * jnp.dot(act_ref[...], wd_ref[...], preferred_element_type=F32)
    if final_norm:
        y = _rms(y, gf_ref[...])
    yp_ref[...] = y

    @pl.when(on_sample)
    def _():
        ys_ref[...] = yp_ref[...]

    for src, dst in zip(cast_in, cast_out):
        dst[...] = src[...].astype(BF16)


def _cast_row_blocks(rows, n_steps):
    return max(d for d in range(1, n_steps + 1) if rows % d == 0 and (rows // d) % BF16_SUBLANES == 0)


def _ffn_half(xp, xs, g, wg, wu, wd, gf=None, cast=()):
    tm = TOKEN_TILE
    n = xp.shape[0]
    assert n % tm == 0 and xs.shape[0] == tm
    n_tiles = n // tm
    prompt_spec = pl.BlockSpec((tm, D_MODEL), lambda i: (jnp.maximum(i - 1, 0), 0))
    sample_in = _resident((tm, D_MODEL))
    sample_out = pl.BlockSpec((tm, D_MODEL), lambda i: (0, 0))
    in_specs = [prompt_spec, sample_in, _resident((1, D_MODEL)), _resident((D_MODEL, D_FF)),
                _resident((D_MODEL, D_FF)), _resident((D_FF, D_MODEL))]
    args = [xp, xs, g.reshape(1, D_MODEL), wg, wu, wd]
    if gf is not None:
        in_specs.append(_resident((1, D_MODEL)))
        args.append(gf.reshape(1, D_MODEL))
    out_specs = [prompt_spec, sample_out]
    out_shape = [jax.ShapeDtypeStruct((n, D_MODEL), F32), jax.ShapeDtypeStruct((tm, D_MODEL), F32)]
    cast_specs = []
    for w in cast:
        rows, cols = w.shape
        nb = _cast_row_blocks(rows, n_tiles)
        cast_specs.append(pl.BlockSpec((rows // nb, cols), lambda i, nb=nb: (jnp.minimum(i, nb - 1), 0)))
        out_shape.append(jax.ShapeDtypeStruct(w.shape, BF16))
    outs = pl.pallas_call(
        functools.partial(_ffn_kernel, final_norm=gf is not None, n_cast=len(cast)),
        grid=(n_tiles + 1,),
        in_specs=in_specs + cast_specs,
        out_specs=out_specs + cast_specs,
        out_shape=out_shape,
        scratch_shapes=[pltpu.VMEM((tm, D_FF), BF16)],
        compiler_params=pltpu.CompilerParams(dimension_semantics=("arbitrary",), vmem_limit_bytes=VMEM_LIMIT_BYTES),
        name="ffn_final" if gf is not None else "ffn_half",
    )(*args, *cast)
    return outs[0], outs[1], list(outs[2:])


def _rope(xg, cos, sin_signed):
    lane = lax.broadcasted_iota(jnp.int32, xg.shape, 1)
    first_half = (lane & (HEAD_DIM - 1)) < HEAD_DIM // 2
    swapped = jnp.where(first_half, pltpu.roll(xg, LANES - HEAD_DIM // 2, 1), pltpu.roll(xg, HEAD_DIM // 2, 1))
    return xg * cos + swapped * sin_signed


def _in_proj(x, gmix, w_in, cos, sin_signed, gvn_gain):
    h = _rms(x, gmix).astype(BF16)
    z = jnp.dot(h, w_in, preferred_element_type=F32)
    q = jnp.concatenate(
        [(_rope(z[:, LANES * i:LANES * (i + 1)], cos, sin_signed) * Q_SCALE).astype(BF16) for i in range(Q_W // LANES)],
        axis=1)
    k = _rope(z[:, K_OFF:V_OFF], cos, sin_signed)
    v = z[:, V_OFF:U_OFF]
    u = jax.nn.gelu(z[:, U_OFF:GV_OFF])
    gvn = _rms(jax.nn.gelu(z[:, GV_OFF:]), gvn_gain)
    return q, k, v, u, gvn


def _out_proj(x, ya, yg, ga, gg, w_out):
    cat = jnp.concatenate([_rms(ya, ga).astype(BF16), _rms(yg, gg).astype(BF16)], axis=1)
    return x + jnp.dot(cat, w_out, preferred_element_type=F32)


def _softmax_sink(s, sink):
    m = jnp.maximum(jnp.max(s, axis=1, keepdims=True), sink)
    p = jnp.exp(s - m)
    return p, jnp.sum(p, axis=1, keepdims=True) + jnp.exp(sink - m)


def _mix_prompt_kernel(xa_ref, xc_ref, cos_ref, sin_ref, gmix_ref, win_ref, sinks_ref, gvg_ref, wcat_ref, bias_ref,
                       ga_ref, gg_ref, wout_ref, o_ref, ko_ref, vo_ref, gvo_ref, q_s, k_s, v_s, u_s, gv_s, cat_s, *,
                       tm, tiles_per_seq):
    s = pl.program_id(0)

    @pl.when(s == 0)
    def _():
        q_s[1] = jnp.zeros((tm, Q_W), BF16)
        k_s[1] = jnp.zeros((tm + WINDOW, KV_W), BF16)
        v_s[1] = jnp.zeros((tm + WINDOW, KV_W), BF16)
        u_s[1] = jnp.zeros((tm, D_GMLP), F32)
        gv_s[1] = jnp.zeros((tm, D_GMLP), BF16)
        cat_s[0] = jnp.zeros((tm, D_MODEL), BF16)

    for parity in range(2):
        pl.when(s % 2 == parity)(functools.partial(
            _mix_prompt_step, s, parity, xa_ref, xc_ref, cos_ref, sin_ref, gmix_ref, win_ref, sinks_ref, gvg_ref,
            wcat_ref, bias_ref, ga_ref, gg_ref, wout_ref, o_ref, ko_ref, vo_ref, gvo_ref, q_s, k_s, v_s, u_s, gv_s,
            cat_s, tm=tm, tiles_per_seq=tiles_per_seq))


def _mix_prompt_step(s, cur, xa_ref, xc_ref, cos_ref, sin_ref, gmix_ref, win_ref, sinks_ref, gvg_ref, wcat_ref, bias_ref,
                     ga_ref, gg_ref, wout_ref, o_ref, ko_ref, vo_ref, gvo_ref, q_s, k_s, v_s, u_s, gv_s, cat_s, *, tm,
                     tiles_per_seq):
    oth = 1 - cur

    h = _rms(xa_ref[0], gmix_ref[...]).astype(BF16)
    cos = cos_ref[...]
    sin = sin_ref[...]

    def proj_q():
        z = jnp.dot(h, win_ref[:, 0:Q_W], preferred_element_type=F32)
        for i in range(Q_W // LANES):
            q_s[cur, :, LANES * i:LANES * (i + 1)] = (
                _rope(z[:, LANES * i:LANES * (i + 1)], cos, sin) * Q_SCALE).astype(BF16)

    def proj_kv():
        z = jnp.dot(h, win_ref[:, K_OFF:U_OFF], preferred_element_type=F32)
        k = _rope(z[:, 0:KV_W], cos, sin)
        v = z[:, KV_W:]
        k_s[cur, 0:WINDOW, :] = k_s[oth, tm:tm + WINDOW, :]
        v_s[cur, 0:WINDOW, :] = v_s[oth, tm:tm + WINDOW, :]
        k_s[cur, WINDOW:, :] = k.astype(BF16)
        v_s[cur, WINDOW:, :] = v.astype(BF16)
        ko_ref[0] = k[tm - WINDOW:]
        vo_ref[0] = v[tm - WINDOW:]

    def proj_u():
        u_s[cur] = jax.nn.gelu(jnp.dot(h, win_ref[:, U_OFF:GV_OFF], preferred_element_type=F32))

    def proj_gv():
        gvn = _rms(jax.nn.gelu(jnp.dot(h, win_ref[:, GV_OFF:], preferred_element_type=F32)), gvg_ref[...])
        gv_s[cur] = gvn.astype(BF16)
        gvo_ref[0] = gvn[tm - CHUNK:]

    def out_half(c):
        cols = slice(c * (D_MODEL // 2), (c + 1) * (D_MODEL // 2))

        def run():
            o_ref[0, :, cols] = xc_ref[0, :, cols] + jnp.dot(cat_s[cur], wout_ref[:, cols], preferred_element_type=F32)
        return run

    mid_fill = [[out_half(0)], [proj_q], [proj_kv, proj_u], [proj_gv]]
    end_fill = [[], [], [], [out_half(1)]]
    assert len(mid_fill) == len(end_fill) == tm // WINDOW

    wrow = lax.broadcasted_iota(jnp.int32, (CHUNK, 2 * CHUNK), 0)
    wcol = lax.broadcasted_iota(jnp.int32, (CHUNK, 2 * CHUNK), 1) & (CHUNK - 1)
    wmix = [jnp.where(wcol <= wrow, wcat_ref[p], 0.0).astype(BF16) for p in range(G_HEADS // 2)]
    lane = lax.broadcasted_iota(jnp.int32, (CHUNK, LANES), 1)
    low_head = lane < HEAD_DIM

    qi = lax.broadcasted_iota(jnp.int32, (WINDOW, 2 * WINDOW), 0)
    sj = lax.broadcasted_iota(jnp.int32, (WINDOW, 2 * WINDOW), 1)
    dist = WINDOW + qi - sj
    band = (dist >= 0) & (dist < WINDOW)
    first_lo = jnp.where((s + tiles_per_seq - 1) % tiles_per_seq == 0, WINDOW, 0)

    for j in range(tm // WINDOW):
        rows = slice(j * WINDOW, (j + 1) * WINDOW)
        mask = band & (sj >= first_lo) if j == 0 else band
        qb = q_s[oth, rows, :]
        scores = []
        for kv in range(N_KV):
            qs = jnp.concatenate(
                [qb[:, HEAD_DIM * hd:HEAD_DIM * (hd + 1)] for hd in range(kv * GQA, (kv + 1) * GQA)], axis=0)
            kb = k_s[oth, j * WINDOW:(j + 2) * WINDOW, HEAD_DIM * kv:HEAD_DIM * (kv + 1)]
            scores.append(lax.dot_general(qs, kb, _NT, preferred_element_type=F32))
        for run in mid_fill[j]:
            run()
        outs, dens = [], []
        for kv in range(N_KV):
            vb = v_s[oth, j * WINDOW:(j + 2) * WINDOW, HEAD_DIM * kv:HEAD_DIM * (kv + 1)]
            ps = []
            for g in range(GQA):
                sg = jnp.where(mask, scores[kv][g * WINDOW:(g + 1) * WINDOW], -jnp.inf)
                p, den = _softmax_sink(sg, sinks_ref[kv * GQA + g])
                ps.append(p.astype(BF16))
                dens.append(den)
            outs.append(jnp.dot(jnp.concatenate(ps, axis=0), vb, preferred_element_type=F32))
        mixed = []
        for p in range(G_HEADS // 2):
            r = gv_s[oth, rows, LANES * p:LANES * (p + 1)]
            zero = jnp.zeros_like(r)
            rhs = jnp.concatenate([jnp.where(low_head, r, zero), jnp.where(low_head, zero, r)], axis=0)
            mixed.append(jnp.dot(wmix[p], rhs, preferred_element_type=F32))
        for run in end_fill[j]:
            run()
        ya = [outs[hd // GQA][(hd % GQA) * WINDOW:(hd % GQA + 1) * WINDOW] / dens[hd] for hd in range(N_HEADS)]
        cat_s[oth, rows, 0:Q_W] = _rms(jnp.concatenate(ya, axis=1), ga_ref[...]).astype(BF16)
        yg = u_s[oth, rows, :] * (jnp.concatenate(mixed, axis=1) + bias_ref[...])
        cat_s[oth, rows, Q_W:] = _rms(yg, gg_ref[...]).astype(BF16)


def _mix_prompt(x, cos, sin_signed, gmix, w_in, sinks, gvg, wcat, bias_full, ga, gg, w_out):
    b, s, _ = x.shape
    tm = TOKEN_TILE
    assert s % tm == 0 and tm % WINDOW == 0
    tiles_per_seq = s // tm
    n_tiles = b * tiles_per_seq
    proj_tile = lambda i: jnp.minimum(i, n_tiles - 1)
    out_tile = lambda i: jnp.maximum(i - 2, 0)
    x_tiles = x.reshape(n_tiles, tm, D_MODEL)
    last = lambda width: pl.BlockSpec((1, WINDOW, width), lambda i: (proj_tile(i) // tiles_per_seq, 0, 0))
    tab_spec = pl.BlockSpec((tm, LANES), lambda i: (proj_tile(i) % tiles_per_seq, 0))
    out, ko, vo, gvo = pl.pallas_call(
        functools.partial(_mix_prompt_kernel, tm=tm, tiles_per_seq=tiles_per_seq),
        grid=(n_tiles + 2,),
        in_specs=[pl.BlockSpec((1, tm, D_MODEL), lambda i: (proj_tile(i), 0, 0)),
                  pl.BlockSpec((1, tm, D_MODEL), lambda i: (out_tile(i), 0, 0)),
                  tab_spec, tab_spec, _resident((1, D_MODEL)), _resident((D_MODEL, D_IN)),
                  pl.BlockSpec(memory_space=pltpu.SMEM), _resident((1, D_GMLP)),
                  _resident((G_HEADS // 2, CHUNK, 2 * CHUNK)), _resident((CHUNK, D_GMLP)), _resident((1, Q_W)),
                  _resident((1, D_GMLP)), _resident((D_MODEL, D_MODEL))],
        out_specs=[pl.BlockSpec((1, tm, D_MODEL), lambda i: (out_tile(i), 0, 0)), last(KV_W), last(KV_W),
                   last(D_GMLP)],
        out_shape=[jax.ShapeDtypeStruct((n_tiles, tm, D_MODEL), F32), jax.ShapeDtypeStruct((b, WINDOW, KV_W), F32),
                   jax.ShapeDtypeStruct((b, WINDOW, KV_W), F32), jax.ShapeDtypeStruct((b, CHUNK, D_GMLP), F32)],
        scratch_shapes=[pltpu.VMEM((2, tm, Q_W), BF16), pltpu.VMEM((2, tm + WINDOW, KV_W), BF16),
                        pltpu.VMEM((2, tm + WINDOW, KV_W), BF16), pltpu.VMEM((2, tm, D_GMLP), F32),
                        pltpu.VMEM((2, tm, D_GMLP), BF16), pltpu.VMEM((2, tm, D_MODEL), BF16)],
        compiler_params=pltpu.CompilerParams(dimension_semantics=("arbitrary",), vmem_limit_bytes=VMEM_LIMIT_BYTES),
        name="mix_prompt",
    )(x_tiles, x_tiles, cos, sin_signed, gmix, w_in, sinks, gvg, wcat, bias_full, ga, gg, w_out)
    return out.reshape(b, s, D_MODEL), ko, vo, gvo


def _mix_sample_kernel(x_ref, ck_ref, cv_ref, cos_ref, sin_ref, gmix_ref, win_ref, sinks_ref, gvg_ref, coef_ref,
                       biasr_ref, ga_ref, gg_ref, wout_ref, o_ref, ko_ref, vo_ref, gvo_ref, q_s, k_s, v_s, ya_s, yg_s,
                       *, t_new, w_buf):
    g = pl.program_id(0)
    n_tok = x_ref.shape[0]
    grp_rows = SEQ_GROUP * t_new
    grp_keys = SEQ_GROUP * w_buf
    sub = 8

    @pl.when(g == 0)
    def _():
        q, k, v, u, gvn = _in_proj(x_ref[...], gmix_ref[...], win_ref[...], cos_ref[...], sin_ref[...], gvg_ref[...])
        q_s[...] = q
        k_s[...] = k
        v_s[...] = v
        gvo_ref[...] = gvn
        g3 = gvn.reshape(n_tok // sub, sub, D_GMLP)
        trow = lax.broadcasted_iota(jnp.int32, (1, sub, D_GMLP), 1) & (t_new - 1)
        mixed = biasr_ref[...][None] + coef_ref[0][None] * g3
        for d in range(1, t_new):
            shifted = jnp.where(trow >= d, pltpu.roll(g3, d, 1), 0.0)
            mixed = mixed + coef_ref[d][None] * shifted
        yg_s[...] = (u.reshape(n_tok // sub, sub, D_GMLP) * mixed).reshape(n_tok, D_GMLP)

    row0 = pl.multiple_of(g * grp_rows, grp_rows)
    qg = q_s[pl.ds(row0, grp_rows), :]
    kn = k_s[pl.ds(row0, grp_rows), :]
    vn = v_s[pl.ds(row0, grp_rows), :]

    ko_ref[:, 0:w_buf - t_new, :] = ck_ref[:, t_new:w_buf, :]
    vo_ref[:, 0:w_buf - t_new, :] = cv_ref[:, t_new:w_buf, :]
    for bl in range(SEQ_GROUP):
        ko_ref[bl, w_buf - t_new:w_buf, :] = kn[t_new * bl:t_new * (bl + 1)]
        vo_ref[bl, w_buf - t_new:w_buf, :] = vn[t_new * bl:t_new * (bl + 1)]

    kc = ck_ref[...].reshape(grp_keys, KV_W).astype(BF16)
    vc = cv_ref[...].reshape(grp_keys, KV_W).astype(BF16)
    knb = kn.astype(BF16)
    vnb = vn.astype(BF16)

    n_rows = GQA * grp_rows
    shift_t = t_new.bit_length() - 1
    shift_w = w_buf.bit_length() - 1
    r = lax.broadcasted_iota(jnp.int32, (n_rows, grp_keys), 0)
    c = lax.broadcasted_iota(jnp.int32, (n_rows, grp_keys), 1)
    mask_c = ((c >> shift_w) == ((r & (grp_rows - 1)) >> shift_t)) & ((c & (w_buf - 1)) > (r & (t_new - 1)))
    r2 = lax.broadcasted_iota(jnp.int32, (n_rows, grp_rows), 0)
    c2 = lax.broadcasted_iota(jnp.int32, (n_rows, grp_rows), 1)
    mask_n = ((c2 >> shift_t) == ((r2 & (grp_rows - 1)) >> shift_t)) & ((c2 & (t_new - 1)) <= (r2 & (t_new - 1)))
    row_head = lax.broadcasted_iota(jnp.int32, (n_rows, 1), 0) >> (grp_rows.bit_length() - 1)

    for kv in range(N_KV):
        heads = [kv * GQA + i for i in range(GQA)]
        lanes = slice(HEAD_DIM * kv, HEAD_DIM * (kv + 1))
        qs = jnp.concatenate([qg[:, HEAD_DIM * hd:HEAD_DIM * (hd + 1)] for hd in heads], axis=0)
        s_c = jnp.where(mask_c, lax.dot_general(qs, kc[:, lanes], _NT, preferred_element_type=F32), -jnp.inf)
        s_n = jnp.where(mask_n, lax.dot_general(qs, knb[:, lanes], _NT, preferred_element_type=F32), -jnp.inf)
        sink = jnp.full((n_rows, 1), sinks_ref[heads[0]], F32)
        for i in range(1, GQA):
            sink = jnp.where(row_head == i, sinks_ref[heads[i]], sink)
        m = jnp.maximum(jnp.maximum(jnp.max(s_c, axis=1, keepdims=True), jnp.max(s_n, axis=1, keepdims=True)), sink)
        p_c = jnp.exp(s_c - m)
        p_n = jnp.exp(s_n - m)
        den = jnp.sum(p_c, axis=1, keepdims=True) + jnp.sum(p_n, axis=1, keepdims=True) + jnp.exp(sink - m)
        o = (jnp.dot(p_c.astype(BF16), vc[:, lanes], preferred_element_type=F32)
             + jnp.dot(p_n.astype(BF16), vnb[:, lanes], preferred_element_type=F32)) / den
        for i, hd in enumerate(heads):
            ya_s[pl.ds(row0, grp_rows), HEAD_DIM * hd:HEAD_DIM * (hd + 1)] = o[i * grp_rows:(i + 1) * grp_rows]

    @pl.when(g == pl.num_programs(0) - 1)
    def _():
        o_ref[...] = _out_proj(x_ref[...], ya_s[...], yg_s[...], ga_ref[...], gg_ref[...], wout_ref[...])


def _mix_sample(x, cache_k, cache_v, cos, sin_signed, gmix, w_in, sinks, gvg, coef, bias_rows, ga, gg, w_out, t_new):
    n_tok = x.shape[0]
    n_seq, w_buf, _ = cache_k.shape
    assert n_seq % SEQ_GROUP == 0 and n_tok == n_seq * t_new
    assert t_new & (t_new - 1) == 0 and w_buf & (w_buf - 1) == 0 and 8 % t_new == 0
    cache_spec = pl.BlockSpec((SEQ_GROUP, w_buf, KV_W), lambda i: (i, 0, 0))
    return pl.pallas_call(
        functools.partial(_mix_sample_kernel, t_new=t_new, w_buf=w_buf),
        grid=(n_seq // SEQ_GROUP,),
        in_specs=[_resident((n_tok, D_MODEL)), cache_spec, cache_spec, _resident((n_tok, LANES)),
                  _resident((n_tok, LANES)), _resident((1, D_MODEL)), _resident((D_MODEL, D_IN)),
                  pl.BlockSpec(memory_space=pltpu.SMEM), _resident((1, D_GMLP)), _resident((t_new, 8, D_GMLP)),
                  _resident((8, D_GMLP)), _resident((1, Q_W)), _resident((1, D_GMLP)),
                  _resident((D_MODEL, D_MODEL))],
        out_specs=[pl.BlockSpec((n_tok, D_MODEL), lambda i: (0, 0)), cache_spec, cache_spec,
                   pl.BlockSpec((n_tok, D_GMLP), lambda i: (0, 0))],
        out_shape=[jax.ShapeDtypeStruct((n_tok, D_MODEL), F32), jax.ShapeDtypeStruct(cache_k.shape, F32),
                   jax.ShapeDtypeStruct(cache_v.shape, F32), jax.ShapeDtypeStruct((n_tok, D_GMLP), F32)],
        scratch_shapes=[pltpu.VMEM((n_tok, Q_W), BF16), pltpu.VMEM((n_tok, KV_W), F32), pltpu.VMEM((n_tok, KV_W), F32),
                        pltpu.VMEM((n_tok, Q_W), F32), pltpu.VMEM((n_tok, D_GMLP), F32)],
        compiler_params=pltpu.CompilerParams(dimension_semantics=("arbitrary",), vmem_limit_bytes=VMEM_LIMIT_BYTES),
        name="mix_sample",
    )(x, cache_k, cache_v, cos, sin_signed, gmix, w_in, sinks, gvg, coef, bias_rows, ga, gg, w_out)


def _rope_tables(pos):
    inv_freq = ROPE_THETA ** (-jnp.arange(0, HEAD_DIM, 2, dtype=F32) / HEAD_DIM)
    ang = pos.astype(F32)[:, None] * inv_freq[None, :]
    c, s = jnp.cos(ang), jnp.sin(ang)
    return jnp.tile(c, (1, 4)), jnp.tile(jnp.concatenate([-s, s], axis=1), (1, 2))


def kernel(x_prompt, x_sample, cache_k_win, cache_v_win, norm_ffn1, ffn1_gate, ffn1_up, ffn1_down, norm_mix, w_in,
           attn_sinks, gmlp_v_norm, gmlp_w_s, gmlp_b_s, norm_attn_out, norm_gmlp_out, w_out, norm_ffn2, ffn2_gate,
           ffn2_up, ffn2_down, norm_final):
    depth = norm_ffn1.shape[0]
    b, s, _ = x_prompt.shape
    bd, t_new, _ = x_sample.shape
    w_buf = cache_k_win.shape[2]

    cos_p, sin_p = _rope_tables(jnp.arange(s, dtype=jnp.int32))
    cos_s, sin_s = _rope_tables(PAST_LEN + jnp.arange(t_new, dtype=jnp.int32))
    cos_s, sin_s = jnp.tile(cos_s, (bd, 1)), jnp.tile(sin_s, (bd, 1))

    hp = x_prompt.reshape(b * s, D_MODEL)
    hs = x_sample.reshape(bd * t_new, D_MODEL)
    outs = [[] for _ in range(6)]
    ffn1_w = [w[0].astype(BF16) for w in (ffn1_gate, ffn1_up, ffn1_down)]
    for l in range(depth):
        last = l == depth - 1
        row = lambda a: a[l].reshape(1, -1)

        wcat = gmlp_w_s[l].reshape(G_HEADS // 2, 2, CHUNK, CHUNK).transpose(0, 2, 1, 3).reshape(
            G_HEADS // 2, CHUNK, 2 * CHUNK)
        bias_full = jnp.repeat(gmlp_b_s[l].T, HEAD_DIM, axis=1)
        ws_small = gmlp_w_s[l][:, :t_new, :t_new]
        coef = jnp.stack([
            jnp.stack([ws_small[:, t, t - d] if t >= d else jnp.zeros((G_HEADS,), F32) for t in range(t_new)])
            for d in range(t_new)])
        coef = jnp.tile(jnp.repeat(coef, HEAD_DIM, axis=2), (1, 8 // t_new, 1))
        bias_rows = jnp.tile(jnp.repeat(gmlp_b_s[l][:, :t_new].T, HEAD_DIM, axis=1), (8 // t_new, 1))

        hp, hs, (w_in_b, w_out_b, wg2, wu2, wd2) = _ffn_half(
            hp, hs, norm_ffn1[l], *ffn1_w, cast=(w_in[l], w_out[l], ffn2_gate[l], ffn2_up[l], ffn2_down[l]))

        hp, kp, vp, gvp = _mix_prompt(hp.reshape(b, s, D_MODEL), cos_p, sin_p, row(norm_mix), w_in_b, attn_sinks[l],
                                      row(gmlp_v_norm), wcat, bias_full, row(norm_attn_out), row(norm_gmlp_out),
                                      w_out_b)
        hs, ks, vs, gvs = _mix_sample(hs, cache_k_win[l].reshape(bd, w_buf, KV_W),
                                      cache_v_win[l].reshape(bd, w_buf, KV_W), cos_s, sin_s, row(norm_mix), w_in_b,
                                      attn_sinks[l], row(gmlp_v_norm), coef, bias_rows, row(norm_attn_out),
                                      row(norm_gmlp_out), w_out_b, t_new)

        next_ffn1 = () if last else (ffn1_gate[l + 1], ffn1_up[l + 1], ffn1_down[l + 1])
        hp, hs, ffn1_w = _ffn_half(hp.reshape(b * s, D_MODEL), hs, norm_ffn2[l], wg2, wu2, wd2,
                                   gf=norm_final if last else None, cast=next_ffn1)

        outs[0].append(kp.reshape(b, WINDOW, N_KV, HEAD_DIM))
        outs[1].append(vp.reshape(b, WINDOW, N_KV, HEAD_DIM))
        outs[2].append(ks.reshape(bd, w_buf, N_KV, HEAD_DIM))
        outs[3].append(vs.reshape(bd, w_buf, N_KV, HEAD_DIM))
        outs[4].append(gvp.reshape(b, CHUNK, G_HEADS, D_GMLP // G_HEADS))
        outs[5].append(gvs.reshape(bd, t_new, G_HEADS, D_GMLP // G_HEADS))

    return (hp.reshape(b, s, D_MODEL), hs.reshape(bd, t_new, D_MODEL)) + tuple(jnp.stack(o) for o in outs)
```

```python
import functools

import jax
import jax.numpy as jnp
from jax import lax
from jax.experimental import pallas as pl
from jax.experimental.pallas import tpu as pltpu

F32 = jnp.float32
BF16 = jnp.bfloat16

D_MODEL = 1024
D_FF = 2816
HEAD_DIM = 64
N_HEADS = 8
N_KV = 2
GQA = N_HEADS // N_KV
WINDOW = 128
CHUNK = 128
G_HEADS = 8
Q_W = N_HEADS * HEAD_DIM
KV_W = N_KV * HEAD_DIM
D_GMLP = 512
D_IN = Q_W + 2 * KV_W + 2 * D_GMLP
K_OFF = Q_W
V_OFF = K_OFF + KV_W
U_OFF = V_OFF + KV_W
GV_OFF = U_OFF + D_GMLP
ROPE_THETA = 10000.0
PAST_LEN = 16384
EPS = 1e-6
Q_SCALE = HEAD_DIM ** -0.5
LANES = 128
BF16_SUBLANES = 16

TOKEN_TILE = 512
FF_CHUNK = 256
DOWN_CHUNK = 256
SEQ_GROUP = 8
VMEM_LIMIT_BYTES = 56 * 1024 * 1024

_NT = (((1,), (1,)), ((), ()))


def _rms(x, g):
    ms = jnp.mean(x * x, axis=-1, keepdims=True)
    return (x * lax.rsqrt(ms + EPS)) * g


def _resident(shape):
    zeros = (0,) * len(shape)
    return pl.BlockSpec(shape, lambda *_: zeros, pipeline_mode=pl.Buffered(1))


def _ffn_kernel(*refs, final_norm, n_cast):
    refs = iter(refs)
    xp_ref, xn_ref, xs_ref, g_ref, wg_ref, wu_ref, wd_ref = (next(refs) for _ in range(7))
    gf_ref = next(refs) if final_norm else None
    cast_in = [next(refs) for _ in range(n_cast)]
    yp_ref, ys_ref = next(refs), next(refs)
    cast_out = [next(refs) for _ in range(n_cast)]
    h_ref, act_ref = next(refs), next(refs)

    on_sample = pl.program_id(0) == 0

    @pl.when(on_sample)
    def _():
        h_ref[...] = _rms(xs_ref[...], g_ref[...]).astype(BF16)

    for c in range(D_FF // FF_CHUNK):
        sl = slice(c * FF_CHUNK, (c + 1) * FF_CHUNK)
        gate = jnp.dot(h_ref[...], wg_ref[:, sl], preferred_element_type=F32)
        up = jnp.dot(h_ref[...], wu_ref[:, sl], preferred_element_type=F32)
        act_ref[:, sl] = (gate * jax.nn.sigmoid(gate) * up).astype(BF16)
    tm = h_ref.shape[0]
    n_down = D_MODEL // DOWN_CHUNK
    sumsq = jnp.zeros((tm, 1), F32)
    for c in range(n_down):
        rows = slice(c * (tm // n_down), (c + 1) * (tm // n_down))
        h_ref[rows, :] = _rms(xn_ref[rows, :], g_ref[...]).astype(BF16)
        sl = slice(c * DOWN_CHUNK, (c + 1) * DOWN_CHUNK)
        x = jnp.where(on_sample, xs_ref[:, sl], xp_ref[:, sl])
        y = x + 0.5 * jnp.dot(act_ref[...], wd_ref[:, sl], preferred_element_type=F32)
        if final_norm:
            sumsq = sumsq + jnp.sum(y * y, axis=-1, keepdims=True)
        yp_ref[:, sl] = y
    if final_norm:
        yp_ref[...] = (yp_ref[...] * lax.rsqrt(sumsq * (1.0 / D_MODEL) + EPS)) * gf_ref[...]

    @pl.when(on_sample)
    def _():
        ys_ref[...] = yp_ref[...]

    for src, dst in zip(cast_in, cast_out):
        dst[...] = src[...].astype(BF16)


def _cast_row_blocks(rows, n_steps):
    return max(d for d in range(1, n_steps + 1) if rows % d == 0 and (rows // d) % BF16_SUBLANES == 0)


def _ffn_half(xp, xs, g, wg, wu, wd, gf=None, cast=()):
    tm = TOKEN_TILE
    n = xp.shape[0]
    assert n % tm == 0 and xs.shape[0] == tm
    n_tiles = n // tm
    prompt_spec = pl.BlockSpec((tm, D_MODEL), lambda i: (jnp.maximum(i - 1, 0), 0))
    next_spec = pl.BlockSpec((tm, D_MODEL), lambda i: (jnp.minimum(i, n_tiles - 1), 0))
    sample_in = _resident((tm, D_MODEL))
    sample_out = pl.BlockSpec((tm, D_MODEL), lambda i: (0, 0))
    in_specs = [prompt_spec, next_spec, sample_in, _resident((1, D_MODEL)), _resident((D_MODEL, D_FF)),
                _resident((D_MODEL, D_FF)), _resident((D_FF, D_MODEL))]
    args = [xp, xp, xs, g.reshape(1, D_MODEL), wg, wu, wd]
    if gf is not None:
        in_specs.append(_resident((1, D_MODEL)))
        args.append(gf.reshape(1, D_MODEL))
    out_specs = [prompt_spec, sample_out]
    out_shape = [jax.ShapeDtypeStruct((n, D_MODEL), F32), jax.ShapeDtypeStruct((tm, D_MODEL), F32)]
    cast_specs = []
    for w in cast:
        rows, cols = w.shape
        nb = _cast_row_blocks(rows, n_tiles)
        cast_specs.append(pl.BlockSpec((rows // nb, cols), lambda i, nb=nb: (jnp.minimum(i, nb - 1), 0)))
        out_shape.append(jax.ShapeDtypeStruct(w.shape, BF16))
    outs = pl.pallas_call(
        functools.partial(_ffn_kernel, final_norm=gf is not None, n_cast=len(cast)),
        grid=(n_tiles + 1,),
        in_specs=in_specs + cast_specs,
        out_specs=out_specs + cast_specs,
        out_shape=out_shape,
        scratch_shapes=[pltpu.VMEM((tm, D_MODEL), BF16), pltpu.VMEM((tm, D_FF), BF16)],
        compiler_params=pltpu.CompilerParams(dimension_semantics=("arbitrary",), vmem_limit_bytes=VMEM_LIMIT_BYTES),
        name="ffn_final" if gf is not None else "ffn_half",
    )(*args, *cast)
    return outs[0], outs[1], list(outs[2:])


def _rope(xg, cos, sin_signed):
    lane = lax.broadcasted_iota(jnp.int32, xg.shape, 1)
    first_half = (lane & (HEAD_DIM - 1)) < HEAD_DIM // 2
    swapped = jnp.where(first_half, pltpu.roll(xg, LANES - HEAD_DIM // 2, 1), pltpu.roll(xg, HEAD_DIM // 2, 1))
    return xg * cos + swapped * sin_signed


def _in_proj(x, gmix, w_in, cos, sin_signed, gvn_gain):
    h = _rms(x, gmix).astype(BF16)
    z = jnp.dot(h, w_in, preferred_element_type=F32)
    q = jnp.concatenate(
        [(_rope(z[:, LANES * i:LANES * (i + 1)], cos, sin_signed) * Q_SCALE).astype(BF16) for i in range(Q_W // LANES)],
        axis=1)
    k = _rope(z[:, K_OFF:V_OFF], cos, sin_signed)
    v = z[:, V_OFF:U_OFF]
    u = jax.nn.gelu(z[:, U_OFF:GV_OFF])
    gvn = _rms(jax.nn.gelu(z[:, GV_OFF:]), gvn_gain)
    return q, k, v, u, gvn


def _out_proj(x, ya, yg, ga, gg, w_out):
    cat = jnp.concatenate([_rms(ya, ga).astype(BF16), _rms(yg, gg).astype(BF16)], axis=1)
    return x + jnp.dot(cat, w_out, preferred_element_type=F32)


def _softmax_sink(s, sink):
    m = jnp.maximum(jnp.max(s, axis=1, keepdims=True), sink)
    p = jnp.exp(s - m)
    return p, jnp.sum(p, axis=1, keepdims=True) + jnp.exp(sink - m)


def _mix_prompt_kernel(xa_ref, xc_ref, cos_ref, sin_ref, gmix_ref, win_ref, sinks_ref, gvg_ref, wcat_ref, bias_ref,
                       ga_ref, gg_ref, wout_ref, o_ref, ko_ref, vo_ref, gvo_ref, q_s, k_s, v_s, u_s, gv_s, cat_s, *,
                       tm, tiles_per_seq):
    s = pl.program_id(0)

    @pl.when(s == 0)
    def _():
        q_s[1] = jnp.zeros((tm, Q_W), BF16)
        k_s[1] = jnp.zeros((tm + WINDOW, KV_W), BF16)
        v_s[1] = jnp.zeros((tm + WINDOW, KV_W), BF16)
        u_s[1] = jnp.zeros((tm, D_GMLP), F32)
        gv_s[1] = jnp.zeros((tm, D_GMLP), BF16)
        cat_s[0] = jnp.zeros((tm, D_MODEL), BF16)

    for parity in range(2):
        pl.when(s % 2 == parity)(functools.partial(
            _mix_prompt_step, s, parity, xa_ref, xc_ref, cos_ref, sin_ref, gmix_ref, win_ref, sinks_ref, gvg_ref,
            wcat_ref, bias_ref, ga_ref, gg_ref, wout_ref, o_ref, ko_ref, vo_ref, gvo_ref, q_s, k_s, v_s, u_s, gv_s,
            cat_s, tm=tm, tiles_per_seq=tiles_per_seq))


def _mix_prompt_step(s, cur, xa_ref, xc_ref, cos_ref, sin_ref, gmix_ref, win_ref, sinks_ref, gvg_ref, wcat_ref, bias_ref,
                     ga_ref, gg_ref, wout_ref, o_ref, ko_ref, vo_ref, gvo_ref, q_s, k_s, v_s, u_s, gv_s, cat_s, *, tm,
                     tiles_per_seq):
    oth = 1 - cur

    h = _rms(xa_ref[0], gmix_ref[...]).astype(BF16)
    cos = cos_ref[...]
    sin = sin_ref[...]

    def proj_q():
        z = jnp.dot(h, win_ref[:, 0:Q_W], preferred_element_type=F32)
        for i in range(Q_W // LANES):
            q_s[cur, :, LANES * i:LANES * (i + 1)] = (
                _rope(z[:, LANES * i:LANES * (i + 1)], cos, sin) * Q_SCALE).astype(BF16)

    def proj_kv():
        z = jnp.dot(h, win_ref[:, K_OFF:U_OFF], preferred_element_type=F32)
        k = _rope(z[:, 0:KV_W], cos, sin)
        v = z[:, KV_W:]
        k_s[cur, 0:WINDOW, :] = k_s[oth, tm:tm + WINDOW, :]
        v_s[cur, 0:WINDOW, :] = v_s[oth, tm:tm + WINDOW, :]
        k_s[cur, WINDOW:, :] = k.astype(BF16)
        v_s[cur, WINDOW:, :] = v.astype(BF16)
        ko_ref[0] = k[tm - WINDOW:]
        vo_ref[0] = v[tm - WINDOW:]

    def proj_u():
        u_s[cur] = jax.nn.gelu(jnp.dot(h, win_ref[:, U_OFF:GV_OFF], preferred_element_type=F32))

    def proj_gv():
        gvn = _rms(jax.nn.gelu(jnp.dot(h, win_ref[:, GV_OFF:], preferred_element_type=F32)), gvg_ref[...])
        gv_s[cur] = gvn.astype(BF16)
        gvo_ref[0] = gvn[tm - CHUNK:]

    def out_half(c):
        cols = slice(c * (D_MODEL // 2), (c + 1) * (D_MODEL // 2))

        def run():
            o_ref[0, :, cols] = xc_ref[0, :, cols] + jnp.dot(cat_s[cur], wout_ref[:, cols], preferred_element_type=F32)
        return run

    mid_fill = [[out_half(0)], [proj_q], [proj_kv, proj_u], [proj_gv]]
    end_fill = [[], [], [], [out_half(1)]]
    assert len(mid_fill) == len(end_fill) == tm // WINDOW

    wrow = lax.broadcasted_iota(jnp.int32, (CHUNK, 2 * CHUNK), 0)
    wcol = lax.broadcasted_iota(jnp.int32, (CHUNK, 2 * CHUNK), 1) & (CHUNK - 1)
    wmix = [jnp.where(wcol <= wrow, wcat_ref[p], 0.0).astype(BF16) for p in range(G_HEADS // 2)]
    lane = lax.broadcasted_iota(jnp.int32, (CHUNK, LANES), 1)
    low_head = lane < HEAD_DIM

    qi = lax.broadcasted_iota(jnp.int32, (WINDOW, 2 * WINDOW), 0)
    sj = lax.broadcasted_iota(jnp.int32, (WINDOW, 2 * WINDOW), 1)
    dist = WINDOW + qi - sj
    band = (dist >= 0) & (dist < WINDOW)
    first_lo = jnp.where((s + tiles_per_seq - 1) % tiles_per_seq == 0, WINDOW, 0)

    for j in range(tm // WINDOW):
        rows = slice(j * WINDOW, (j + 1) * WINDOW)
        mask = band & (sj >= first_lo) if j == 0 else band
        qb = q_s[oth, rows, :]
        scores = []
        for kv in range(N_KV):
            qs = jnp.concatenate(
                [qb[:, HEAD_DIM * hd:HEAD_DIM * (hd + 1)] for hd in range(kv * GQA, (kv + 1) * GQA)], axis=0)
            kb = k_s[oth, j * WINDOW:(j + 2) * WINDOW, HEAD_DIM * kv:HEAD_DIM * (kv + 1)]
            scores.append(lax.dot_general(qs, kb, _NT, preferred_element_type=F32))
        for run in mid_fill[j]:
            run()
        outs, dens = [], []
        for kv in range(N_KV):
            vb = v_s[oth, j * WINDOW:(j + 2) * WINDOW, HEAD_DIM * kv:HEAD_DIM * (kv + 1)]
            ps = []
            for g in range(GQA):
                sg = jnp.where(mask, scores[kv][g * WINDOW:(g + 1) * WINDOW], -jnp.inf)
                p, den = _softmax_sink(sg, sinks_ref[kv * GQA + g])
                ps.append(p.astype(BF16))
                dens.append(den)
            outs.append(jnp.dot(jnp.concatenate(ps, axis=0), vb, preferred_element_type=F32))
        mixed = []
        for p in range(G_HEADS // 2):
            r = gv_s[oth, rows, LANES * p:LANES * (p + 1)]
            zero = jnp.zeros_like(r)
            rhs = jnp.concatenate([jnp.where(low_head, r, zero), jnp.where(low_head, zero, r)], axis=0)
            mixed.append(jnp.dot(wmix[p], rhs, preferred_element_type=F32))
        for run in end_fill[j]:
            run()
        ya = [outs[hd // GQA][(hd % GQA) * WINDOW:(hd % GQA + 1) * WINDOW] / dens[hd] for hd in range(N_HEADS)]
        cat_s[oth, rows, 0:Q_W] = _rms(jnp.concatenate(ya, axis=1), ga_ref[...]).astype(BF16)
        yg = u_s[oth, rows, :] * (jnp.concatenate(mixed, axis=1) + bias_ref[...])
        cat_s[oth, rows, Q_W:] = _rms(yg, gg_ref[...]).astype(BF16)


def _mix_prompt(x, cos, sin_signed, gmix, w_in, sinks, gvg, wcat, bias_full, ga, gg, w_out):
    b, s, _ = x.shape
    tm = TOKEN_TILE
    assert s % tm == 0 and tm % WINDOW == 0
    tiles_per_seq = s // tm
    n_tiles = b * tiles_per_seq
    proj_tile = lambda i: jnp.minimum(i, n_tiles - 1)
    out_tile = lambda i: jnp.maximum(i - 2, 0)
    x_tiles = x.reshape(n_tiles, tm, D_MODEL)
    last = lambda width: pl.BlockSpec((1, WINDOW, width), lambda i: (proj_tile(i) // tiles_per_seq, 0, 0))
    tab_spec = pl.BlockSpec((tm, LANES), lambda i: (proj_tile(i) % tiles_per_seq, 0))
    out, ko, vo, gvo = pl.pallas_call(
        functools.partial(_mix_prompt_kernel, tm=tm, tiles_per_seq=tiles_per_seq),
        grid=(n_tiles + 2,),
        in_specs=[pl.BlockSpec((1, tm, D_MODEL), lambda i: (proj_tile(i), 0, 0)),
                  pl.BlockSpec((1, tm, D_MODEL), lambda i: (out_tile(i), 0, 0)),
                  tab_spec, tab_spec, _resident((1, D_MODEL)), _resident((D_MODEL, D_IN)),
                  pl.BlockSpec(memory_space=pltpu.SMEM), _resident((1, D_GMLP)),
                  _resident((G_HEADS // 2, CHUNK, 2 * CHUNK)), _resident((CHUNK, D_GMLP)), _resident((1, Q_W)),
                  _resident((1, D_GMLP)), _resident((D_MODEL, D_MODEL))],
        out_specs=[pl.BlockSpec((1, tm, D_MODEL), lambda i: (out_tile(i), 0, 0)), last(KV_W), last(KV_W),
                   last(D_GMLP)],
        out_shape=[jax.ShapeDtypeStruct((n_tiles, tm, D_MODEL), F32), jax.ShapeDtypeStruct((b, WINDOW, KV_W), F32),
                   jax.ShapeDtypeStruct((b, WINDOW, KV_W), F32), jax.ShapeDtypeStruct((b, CHUNK, D_GMLP), F32)],
        scratch_shapes=[pltpu.VMEM((2, tm, Q_W), BF16), pltpu.VMEM((2, tm + WINDOW, KV_W), BF16),
                        pltpu.VMEM((2, tm + WINDOW, KV_W), BF16), pltpu.VMEM((2, tm, D_GMLP), F32),
                        pltpu.VMEM((2, tm, D_GMLP), BF16), pltpu.VMEM((2, tm, D_MODEL), BF16)],
        compiler_params=pltpu.CompilerParams(dimension_semantics=("arbitrary",), vmem_limit_bytes=VMEM_LIMIT_BYTES),
        name="mix_prompt",
    )(x_tiles, x_tiles, cos, sin_signed, gmix, w_in, sinks, gvg, wcat, bias_full, ga, gg, w_out)
    return out.reshape(b, s, D_MODEL), ko, vo, gvo


def _mix_sample_kernel(x_ref, ck_ref, cv_ref, cos_ref, sin_ref, gmix_ref, win_ref, sinks_ref, gvg_ref, coef_ref,
                       biasr_ref, ga_ref, gg_ref, wout_ref, o_ref, ko_ref, vo_ref, gvo_ref, q_s, k_s, v_s, ya_s, yg_s,
                       *, t_new, w_buf):
    g = pl.program_id(0)
    n_tok = x_ref.shape[0]
    grp_rows = SEQ_GROUP * t_new
    grp_keys = SEQ_GROUP * w_buf
    sub = 8

    @pl.when(g == 0)
    def _():
        q, k, v, u, gvn = _in_proj(x_ref[...], gmix_ref[...], win_ref[...], cos_ref[...], sin_ref[...], gvg_ref[...])
        q_s[...] = q
        k_s[...] = k
        v_s[...] = v
        gvo_ref[...] = gvn
        g3 = gvn.reshape(n_tok // sub, sub, D_GMLP)
        trow = lax.broadcasted_iota(jnp.int32, (1, sub, D_GMLP), 1) & (t_new - 1)
        mixed = biasr_ref[...][None] + coef_ref[0][None] * g3
        for d in range(1, t_new):
            shifted = jnp.where(trow >= d, pltpu.roll(g3, d, 1), 0.0)
            mixed = mixed + coef_ref[d][None] * shifted
        yg_s[...] = (u.reshape(n_tok // sub, sub, D_GMLP) * mixed).reshape(n_tok, D_GMLP)

    row0 = pl.multiple_of(g * grp_rows, grp_rows)
    qg = q_s[pl.ds(row0, grp_rows), :]
    kn = k_s[pl.ds(row0, grp_rows), :]
    vn = v_s[pl.ds(row0, grp_rows), :]

    ko_ref[:, 0:w_buf - t_new, :] = ck_ref[:, t_new:w_buf, :]
    vo_ref[:, 0:w_buf - t_new, :] = cv_ref[:, t_new:w_buf, :]
    for bl in range(SEQ_GROUP):
        ko_ref[bl, w_buf - t_new:w_buf, :] = kn[t_new * bl:t_new * (bl + 1)]
        vo_ref[bl, w_buf - t_new:w_buf, :] = vn[t_new * bl:t_new * (bl + 1)]

    kc = ck_ref[...].reshape(grp_keys, KV_W).astype(BF16)
    vc = cv_ref[...].reshape(grp_keys, KV_W).astype(BF16)
    knb = kn.astype(BF16)
    vnb = vn.astype(BF16)

    n_rows = GQA * grp_rows
    shift_t = t_new.bit_length() - 1
    shift_w = w_buf.bit_length() - 1
    r = lax.broadcasted_iota(jnp.int32, (n_rows, grp_keys), 0)
    c = lax.broadcasted_iota(jnp.int32, (n_rows, grp_keys), 1)
    mask_c = ((c >> shift_w) == ((r & (grp_rows - 1)) >> shift_t)) & ((c & (w_buf - 1)) > (r & (t_new - 1)))
    r2 = lax.broadcasted_iota(jnp.int32, (n_rows, grp_rows), 0)
    c2 = lax.broadcasted_iota(jnp.int32, (n_rows, grp_rows), 1)
    mask_n = ((c2 >> shift_t) == ((r2 & (grp_rows - 1)) >> shift_t)) & ((c2 & (t_new - 1)) <= (r2 & (t_new - 1)))
    row_head = lax.broadcasted_iota(jnp.int32, (n_rows, 1), 0) >> (grp_rows.bit_length() - 1)

    for kv in range(N_KV):
        heads = [kv * GQA + i for i in range(GQA)]
        lanes = slice(HEAD_DIM * kv, HEAD_DIM * (kv + 1))
        qs = jnp.concatenate([qg[:, HEAD_DIM * hd:HEAD_DIM * (hd + 1)] for hd in heads], axis=0)
        s_c = jnp.where(mask_c, lax.dot_general(qs, kc[:, lanes], _NT, preferred_element_type=F32), -jnp.inf)
        s_n = jnp.where(mask_n, lax.dot_general(qs, knb[:, lanes], _NT, preferred_element_type=F32), -jnp.inf)
        sink = jnp.full((n_rows, 1), sinks_ref[heads[0]], F32)
        for i in range(1, GQA):
            sink = jnp.where(row_head == i, sinks_ref[heads[i]], sink)
        m = jnp.maximum(jnp.maximum(jnp.max(s_c, axis=1, keepdims=True), jnp.max(s_n, axis=1, keepdims=True)), sink)
        p_c = jnp.exp(s_c - m)
        p_n = jnp.exp(s_n - m)
        den = jnp.sum(p_c, axis=1, keepdims=True) + jnp.sum(p_n, axis=1, keepdims=True) + jnp.exp(sink - m)
        o = (jnp.dot(p_c.astype(BF16), vc[:, lanes], preferred_element_type=F32)
             + jnp.dot(p_n.astype(BF16), vnb[:, lanes], preferred_element_type=F32)) / den
        for i, hd in enumerate(heads):
            ya_s[pl.ds(row0, grp_rows), HEAD_DIM * hd:HEAD_DIM * (hd + 1)] = o[i * grp_rows:(i + 1) * grp_rows]

    @pl.when(g == pl.num_programs(0) - 1)
    def _():
        o_ref[...] = _out_proj(x_ref[...], ya_s[...], yg_s[...], ga_ref[...], gg_ref[...], wout_ref[...])


def _mix_sample(x, cache_k, cache_v, cos, sin_signed, gmix, w_in, sinks, gvg, coef, bias_rows, ga, gg, w_out, t_new):
    n_tok = x.shape[0]
    n_seq, w_buf, _ = cache_k.shape
    assert n_seq % SEQ_GROUP == 0 and n_tok == n_seq * t_new
    assert t_new & (t_new - 1) == 0 and w_buf & (w_buf - 1) == 0 and 8 % t_new == 0
    cache_spec = pl.BlockSpec((SEQ_GROUP, w_buf, KV_W), lambda i: (i, 0, 0))
    return pl.pallas_call(
        functools.partial(_mix_sample_kernel, t_new=t_new, w_buf=w_buf),
        grid=(n_seq // SEQ_GROUP,),
        in_specs=[_resident((n_tok, D_MODEL)), cache_spec, cache_spec, _resident((n_tok, LANES)),
                  _resident((n_tok, LANES)), _resident((1, D_MODEL)), _resident((D_MODEL, D_IN)),
                  pl.BlockSpec(memory_space=pltpu.SMEM), _resident((1, D_GMLP)), _resident((t_new, 8, D_GMLP)),
                  _resident((8, D_GMLP)), _resident((1, Q_W)), _resident((1, D_GMLP)),
                  _resident((D_MODEL, D_MODEL))],
        out_specs=[pl.BlockSpec((n_tok, D_MODEL), lambda i: (0, 0)), cache_spec, cache_spec,
                   pl.BlockSpec((n_tok, D_GMLP), lambda i: (0, 0))],
        out_shape=[jax.ShapeDtypeStruct((n_tok, D_MODEL), F32), jax.ShapeDtypeStruct(cache_k.shape, F32),
                   jax.ShapeDtypeStruct(cache_v.shape, F32), jax.ShapeDtypeStruct((n_tok, D_GMLP), F32)],
        scratch_shapes=[pltpu.VMEM((n_tok, Q_W), BF16), pltpu.VMEM((n_tok, KV_W), F32), pltpu.VMEM((n_tok, KV_W), F32),
                        pltpu.VMEM((n_tok, Q_W), F32), pltpu.VMEM((n_tok, D_GMLP), F32)],
        compiler_params=pltpu.CompilerParams(dimension_semantics=("arbitrary",), vmem_limit_bytes=VMEM_LIMIT_BYTES),
        name="mix_sample",
    )(x, cache_k, cache_v, cos, sin_signed, gmix, w_in, sinks, gvg, coef, bias_rows, ga, gg, w_out)


def _rope_tables(pos):
    inv_freq = ROPE_THETA ** (-jnp.arange(0, HEAD_DIM, 2, dtype=F32) / HEAD_DIM)
    ang = pos.astype(F32)[:, None] * inv_freq[None, :]
    c, s = jnp.cos(ang), jnp.sin(ang)
    return jnp.tile(c, (1, 4)), jnp.tile(jnp.concatenate([-s, s], axis=1), (1, 2))


def kernel(x_prompt, x_sample, cache_k_win, cache_v_win, norm_ffn1, ffn1_gate, ffn1_up, ffn1_down, norm_mix, w_in,
           attn_sinks, gmlp_v_norm, gmlp_w_s, gmlp_b_s, norm_attn_out, norm_gmlp_out, w_out, norm_ffn2, ffn2_gate,
           ffn2_up, ffn2_down, norm_final):
    depth = norm_ffn1.shape[0]
    b, s, _ = x_prompt.shape
    bd, t_new, _ = x_sample.shape
    w_buf = cache_k_win.shape[2]

    cos_p, sin_p = _rope_tables(jnp.arange(s, dtype=jnp.int32))
    cos_s, sin_s = _rope_tables(PAST_LEN + jnp.arange(t_new, dtype=jnp.int32))
    cos_s, sin_s = jnp.tile(cos_s, (bd, 1)), jnp.tile(sin_s, (bd, 1))

    hp = x_prompt.reshape(b * s, D_MODEL)
    hs = x_sample.reshape(bd * t_new, D_MODEL)
    outs = [[] for _ in range(6)]
    ffn1_w = [w[0].astype(BF16) for w in (ffn1_gate, ffn1_up, ffn1_down)]
    for l in range(depth):
        last = l == depth - 1
        row = lambda a: a[l].reshape(1, -1)

        wcat = gmlp_w_s[l].reshape(G_HEADS // 2, 2, CHUNK, CHUNK).transpose(0, 2, 1, 3).reshape(
            G_HEADS // 2, CHUNK, 2 * CHUNK)
        bias_full = jnp.repeat(gmlp_b_s[l].T, HEAD_DIM, axis=1)
        ws_small = gmlp_w_s[l][:, :t_new, :t_new]
        coef = jnp.stack([
            jnp.stack([ws_small[:, t, t - d] if t >= d else jnp.zeros((G_HEADS,), F32) for t in range(t_new)])
            for d in range(t_new)])
        coef = jnp.tile(jnp.repeat(coef, HEAD_DIM, axis=2), (1, 8 // t_new, 1))
        bias_rows = jnp.tile(jnp.repeat(gmlp_b_s[l][:, :t_new].T, HEAD_DIM, axis=1), (8 // t_new, 1))

        hp, hs, (w_in_b, w_out_b, wg2, wu2, wd2) = _ffn_half(
            hp, hs, norm_ffn1[l], *ffn1_w, cast=(w_in[l], w_out[l], ffn2_gate[l], ffn2_up[l], ffn2_down[l]))

        hp, kp, vp, gvp = _mix_prompt(hp.reshape(b, s, D_MODEL), cos_p, sin_p, row(norm_mix), w_in_b, attn_sinks[l],
                                      row(gmlp_v_norm), wcat, bias_full, row(norm_attn_out), row(norm_gmlp_out),
                                      w_out_b)
        hs, ks, vs, gvs = _mix_sample(hs, cache_k_win[l].reshape(bd, w_buf, KV_W),
                                      cache_v_win[l].reshape(bd, w_buf, KV_W), cos_s, sin_s, row(norm_mix), w_in_b,
                                      attn_sinks[l], row(gmlp_v_norm), coef, bias_rows, row(norm_attn_out),
                                      row(norm_gmlp_out), w_out_b, t_new)

        next_ffn1 = () if last else (ffn1_gate[l + 1], ffn1_up[l + 1], ffn1_down[l + 1])
        hp, hs, ffn1_w = _ffn_half(hp.reshape(b * s, D_MODEL), hs, norm_ffn2[l], wg2, wu2, wd2,
                                   gf=norm_final if last else None, cast=next_ffn1)

        outs[0].append(kp.reshape(b, WINDOW, N_KV, HEAD_DIM))
        outs[1].append(vp.reshape(b, WINDOW, N_KV, HEAD_DIM))
        outs[2].append(ks.reshape(bd, w_buf, N_KV, HEAD_DIM))
        outs[3].append(vs.reshape(bd, w_buf, N_KV, HEAD_DIM))
        outs[4].append(gvp.reshape(b, CHUNK, G_HEADS, D_GMLP // G_HEADS))
        outs[5].append(gvs.reshape(bd, t_new, G_HEADS, D_GMLP // G_HEADS))

    return (hp.reshape(b, s, D_MODEL), hs.reshape(bd, t_new, D_MODEL)) + tuple(jnp.stack(o) for o in outs)
```

```python
import functools

import jax
import jax.numpy as jnp
from jax import lax
from jax.experimental import pallas as pl
from jax.experimental.pallas import tpu as pltpu

F32 = jnp.float32
BF16 = jnp.bfloat16

D_MODEL = 1024
D_FF = 2816
HEAD_DIM = 64
N_HEADS = 8
N_KV = 2
GQA = N_HEADS // N_KV
WINDOW = 128
CHUNK = 128
G_HEADS = 8
Q_W = N_HEADS * HEAD_DIM
KV_W = N_KV * HEAD_DIM
D_GMLP = 512
D_IN = Q_W + 2 * KV_W + 2 * D_GMLP
K_OFF = Q_W
V_OFF = K_OFF + KV_W
U_OFF = V_OFF + KV_W
GV_OFF = U_OFF + D_GMLP
ROPE_THETA = 10000.0
PAST_LEN = 16384
EPS = 1e-6
Q_SCALE = HEAD_DIM ** -0.5
LANES = 128
BF16_SUBLANES = 16

TOKEN_TILE = 512
FF_CHUNK = 256
SEQ_GROUP = 8
GROUPS_PER_STEP = 4
VMEM_LIMIT_BYTES = 56 * 1024 * 1024

_NT = (((1,), (1,)), ((), ()))


def _rms(x, g):
    ms = jnp.mean(x * x, axis=-1, keepdims=True)
    return (x * lax.rsqrt(ms + EPS)) * g


def _resident(shape):
    zeros = (0,) * len(shape)
    return pl.BlockSpec(shape, lambda *_: zeros, pipeline_mode=pl.Buffered(1))


def _ffn_kernel(*refs, final_norm, n_cast):
    refs = iter(refs)
    xp_ref, xs_ref, g_ref, wg_ref, wu_ref, wd_ref = (next(refs) for _ in range(6))
    gf_ref = next(refs) if final_norm else None
    cast_in = [next(refs) for _ in range(n_cast)]
    yp_ref, ys_ref = next(refs), next(refs)
    cast_out = [next(refs) for _ in range(n_cast)]
    act_ref = next(refs)

    on_sample = pl.program_id(0) == 0
    x = jnp.where(on_sample, xs_ref[...], xp_ref[...])
    h = _rms(x, g_ref[...]).astype(BF16)
    for c in range(D_FF // FF_CHUNK):
        sl = slice(c * FF_CHUNK, (c + 1) * FF_CHUNK)
        gate = jnp.dot(h, wg_ref[:, sl], preferred_element_type=F32)
        up = jnp.dot(h, wu_ref[:, sl], preferred_element_type=F32)
        act_ref[:, sl] = (gate * jax.nn.sigmoid(gate) * up).astype(BF16)
    y = x + 0.5 * jnp.dot(act_ref[...], wd_ref[...], preferred_element_type=F32)
    if final_norm:
        y = _rms(y, gf_ref[...])
    yp_ref[...] = y

    @pl.when(on_sample)
    def _():
        ys_ref[...] = yp_ref[...]

    for src, dst in zip(cast_in, cast_out):
        dst[...] = src[...].astype(BF16)


def _cast_row_blocks(rows, n_steps):
    return max(d for d in range(1, n_steps + 1) if rows % d == 0 and (rows // d) % BF16_SUBLANES == 0)


def _ffn_half(xp, xs, g, wg, wu, wd, gf=None, cast=()):
    tm = TOKEN_TILE
    n = xp.shape[0]
    assert n % tm == 0 and xs.shape[0] == tm
    n_tiles = n // tm
    prompt_spec = pl.BlockSpec((tm, D_MODEL), lambda i: (jnp.maximum(i - 1, 0), 0))
    sample_in = _resident((tm, D_MODEL))
    sample_out = pl.BlockSpec((tm, D_MODEL), lambda i: (0, 0))
    in_specs = [prompt_spec, sample_in, _resident((1, D_MODEL)), _resident((D_MODEL, D_FF)),
                _resident((D_MODEL, D_FF)), _resident((D_FF, D_MODEL))]
    args = [xp, xs, g.reshape(1, D_MODEL), wg, wu, wd]
    if gf is not None:
        in_specs.append(_resident((1, D_MODEL)))
        args.append(gf.reshape(1, D_MODEL))
    out_specs = [prompt_spec, sample_out]
    out_shape = [jax.ShapeDtypeStruct((n, D_MODEL), F32), jax.ShapeDtypeStruct((tm, D_MODEL), F32)]
    cast_specs = []
    for w in cast:
        rows, cols = w.shape
        nb = _cast_row_blocks(rows, n_tiles)
        cast_specs.append(pl.BlockSpec((rows // nb, cols), lambda i, nb=nb: (jnp.minimum(i, nb - 1), 0)))
        out_shape.append(jax.ShapeDtypeStruct(w.shape, BF16))
    outs = pl.pallas_call(
        functools.partial(_ffn_kernel, final_norm=gf is not None, n_cast=len(cast)),
        grid=(n_tiles + 1,),
        in_specs=in_specs + cast_specs,
        out_specs=out_specs + cast_specs,
        out_shape=out_shape,
        scratch_shapes=[pltpu.VMEM((tm, D_FF), BF16)],
        compiler_params=pltpu.CompilerParams(dimension_semantics=("arbitrary",), vmem_limit_bytes=VMEM_LIMIT_BYTES),
        name="ffn_final" if gf is not None else "ffn_half",
    )(*args, *cast)
    return outs[0], outs[1], list(outs[2:])


def _rope(xg, cos, sin_signed):
    lane = lax.broadcasted_iota(jnp.int32, xg.shape, 1)
    first_half = (lane & (HEAD_DIM - 1)) < HEAD_DIM // 2
    swapped = jnp.where(first_half, pltpu.roll(xg, LANES - HEAD_DIM // 2, 1), pltpu.roll(xg, HEAD_DIM // 2, 1))
    return xg * cos + swapped * sin_signed


def _in_proj(x, gmix, w_in, cos, sin_signed, gvn_gain):
    h = _rms(x, gmix).astype(BF16)
    z = jnp.dot(h, w_in, preferred_element_type=F32)
    q = jnp.concatenate(
        [(_rope(z[:, LANES * i:LANES * (i + 1)], cos, sin_signed) * Q_SCALE).astype(BF16) for i in range(Q_W // LANES)],
        axis=1)
    k = _rope(z[:, K_OFF:V_OFF], cos, sin_signed)
    v = z[:, V_OFF:U_OFF]
    u = jax.nn.gelu(z[:, U_OFF:GV_OFF])
    gvn = _rms(jax.nn.gelu(z[:, GV_OFF:]), gvn_gain)
    return q, k, v, u, gvn


def _out_proj(x, ya, yg, ga, gg, w_out):
    cat = jnp.concatenate([_rms(ya, ga).astype(BF16), _rms(yg, gg).astype(BF16)], axis=1)
    return x + jnp.dot(cat, w_out, preferred_element_type=F32)


def _softmax_sink(s, sink):
    m = jnp.maximum(jnp.max(s, axis=1, keepdims=True), sink)
    p = jnp.exp(s - m)
    return p, jnp.sum(p, axis=1, keepdims=True) + jnp.exp(sink - m)


def _mix_prompt_kernel(xa_ref, xc_ref, cos_ref, sin_ref, gmix_ref, win_ref, sinks_ref, gvg_ref, wcat_ref, bias_ref,
                       ga_ref, gg_ref, wout_ref, o_ref, ko_ref, vo_ref, gvo_ref, q_s, k_s, v_s, u_s, gv_s, cat_s, *,
                       tm, tiles_per_seq):
    s = pl.program_id(0)

    @pl.when(s == 0)
    def _():
        q_s[1] = jnp.zeros((tm, Q_W), BF16)
        k_s[1] = jnp.zeros((tm + WINDOW, KV_W), BF16)
        v_s[1] = jnp.zeros((tm + WINDOW, KV_W), BF16)
        u_s[1] = jnp.zeros((tm, D_GMLP), F32)
        gv_s[1] = jnp.zeros((tm, D_GMLP), BF16)
        cat_s[0] = jnp.zeros((tm, D_MODEL), BF16)

    for parity in range(2):
        pl.when(s % 2 == parity)(functools.partial(
            _mix_prompt_step, s, parity, xa_ref, xc_ref, cos_ref, sin_ref, gmix_ref, win_ref, sinks_ref, gvg_ref,
            wcat_ref, bias_ref, ga_ref, gg_ref, wout_ref, o_ref, ko_ref, vo_ref, gvo_ref, q_s, k_s, v_s, u_s, gv_s,
            cat_s, tm=tm, tiles_per_seq=tiles_per_seq))


def _mix_prompt_step(s, cur, xa_ref, xc_ref, cos_ref, sin_ref, gmix_ref, win_ref, sinks_ref, gvg_ref, wcat_ref, bias_ref,
                     ga_ref, gg_ref, wout_ref, o_ref, ko_ref, vo_ref, gvo_ref, q_s, k_s, v_s, u_s, gv_s, cat_s, *, tm,
                     tiles_per_seq):
    oth = 1 - cur

    h = _rms(xa_ref[0], gmix_ref[...]).astype(BF16)
    cos = cos_ref[...]
    sin = sin_ref[...]

    def proj_q():
        z = jnp.dot(h, win_ref[:, 0:Q_W], preferred_element_type=F32)
        for i in range(Q_W // LANES):
            q_s[cur, :, LANES * i:LANES * (i + 1)] = (
                _rope(z[:, LANES * i:LANES * (i + 1)], cos, sin) * Q_SCALE).astype(BF16)

    def proj_kv():
        z = jnp.dot(h, win_ref[:, K_OFF:U_OFF], preferred_element_type=F32)
        k = _rope(z[:, 0:KV_W], cos, sin)
        v = z[:, KV_W:]
        k_s[cur, 0:WINDOW, :] = k_s[oth, tm:tm + WINDOW, :]
        v_s[cur, 0:WINDOW, :] = v_s[oth, tm:tm + WINDOW, :]
        k_s[cur, WINDOW:, :] = k.astype(BF16)
        v_s[cur, WINDOW:, :] = v.astype(BF16)
        ko_ref[0] = k[tm - WINDOW:]
        vo_ref[0] = v[tm - WINDOW:]

    def proj_u():
        u_s[cur] = jax.nn.gelu(jnp.dot(h, win_ref[:, U_OFF:GV_OFF], preferred_element_type=F32))

    def proj_gv():
        gvn = _rms(jax.nn.gelu(jnp.dot(h, win_ref[:, GV_OFF:], preferred_element_type=F32)), gvg_ref[...])
        gv_s[cur] = gvn.astype(BF16)
        gvo_ref[0] = gvn[tm - CHUNK:]

    def out_half(c):
        cols = slice(c * (D_MODEL // 2), (c + 1) * (D_MODEL // 2))

        def run():
            o_ref[0, :, cols] = xc_ref[0, :, cols] + jnp.dot(cat_s[cur], wout_ref[:, cols], preferred_element_type=F32)
        return run

    mid_fill = [[out_half(0)], [proj_q], [proj_kv, proj_u], [proj_gv]]
    end_fill = [[], [], [], [out_half(1)]]
    assert len(mid_fill) == len(end_fill) == tm // WINDOW

    wrow = lax.broadcasted_iota(jnp.int32, (CHUNK, 2 * CHUNK), 0)
    wcol = lax.broadcasted_iota(jnp.int32, (CHUNK, 2 * CHUNK), 1) & (CHUNK - 1)
    wmix = [jnp.where(wcol <= wrow, wcat_ref[p], 0.0).astype(BF16) for p in range(G_HEADS // 2)]
    lane = lax.broadcasted_iota(jnp.int32, (CHUNK, LANES), 1)
    low_head = lane < HEAD_DIM

    qi = lax.broadcasted_iota(jnp.int32, (WINDOW, 2 * WINDOW), 0)
    sj = lax.broadcasted_iota(jnp.int32, (WINDOW, 2 * WINDOW), 1)
    dist = WINDOW + qi - sj
    band = (dist >= 0) & (dist < WINDOW)
    first_lo = jnp.where((s + tiles_per_seq - 1) % tiles_per_seq == 0, WINDOW, 0)

    for j in range(tm // WINDOW):
        rows = slice(j * WINDOW, (j + 1) * WINDOW)
        mask = band & (sj >= first_lo) if j == 0 else band
        qb = q_s[oth, rows, :]
        scores = []
        for kv in range(N_KV):
            qs = jnp.concatenate(
                [qb[:, HEAD_DIM * hd:HEAD_DIM * (hd + 1)] for hd in range(kv * GQA, (kv + 1) * GQA)], axis=0)
            kb = k_s[oth, j * WINDOW:(j + 2) * WINDOW, HEAD_DIM * kv:HEAD_DIM * (kv + 1)]
            scores.append(lax.dot_general(qs, kb, _NT, preferred_element_type=F32))
        for run in mid_fill[j]:
            run()
        outs, dens = [], []
        for kv in range(N_KV):
            vb = v_s[oth, j * WINDOW:(j + 2) * WINDOW, HEAD_DIM * kv:HEAD_DIM * (kv + 1)]
            ps = []
            for g in range(GQA):
                sg = jnp.where(mask, scores[kv][g * WINDOW:(g + 1) * WINDOW], -jnp.inf)
                p, den = _softmax_sink(sg, sinks_ref[kv * GQA + g])
                ps.append(p.astype(BF16))
                dens.append(den)
            outs.append(jnp.dot(jnp.concatenate(ps, axis=0), vb, preferred_element_type=F32))
        mixed = []
        for p in range(G_HEADS // 2):
            r = gv_s[oth, rows, LANES * p:LANES * (p + 1)]
            zero = jnp.zeros_like(r)
            rhs = jnp.concatenate([jnp.where(low_head, r, zero), jnp.where(low_head, zero, r)], axis=0)
            mixed.append(jnp.dot(wmix[p], rhs, preferred_element_type=F32))
        for run in end_fill[j]:
            run()
        ya = [outs[hd // GQA][(hd % GQA) * WINDOW:(hd % GQA + 1) * WINDOW] / dens[hd] for hd in range(N_HEADS)]
        cat_s[oth, rows, 0:Q_W] = _rms(jnp.concatenate(ya, axis=1), ga_ref[...]).astype(BF16)
        yg = u_s[oth, rows, :] * (jnp.concatenate(mixed, axis=1) + bias_ref[...])
        cat_s[oth, rows, Q_W:] = _rms(yg, gg_ref[...]).astype(BF16)


def _mix_prompt(x, cos, sin_signed, gmix, w_in, sinks, gvg, wcat, bias_full, ga, gg, w_out):
    b, s, _ = x.shape
    tm = TOKEN_TILE
    assert s % tm == 0 and tm % WINDOW == 0
    tiles_per_seq = s // tm
    n_tiles = b * tiles_per_seq
    proj_tile = lambda i: jnp.minimum(i, n_tiles - 1)
    out_tile = lambda i: jnp.maximum(i - 2, 0)
    x_tiles = x.reshape(n_tiles, tm, D_MODEL)
    last = lambda width: pl.BlockSpec((1, WINDOW, width), lambda i: (proj_tile(i) // tiles_per_seq, 0, 0))
    tab_spec = pl.BlockSpec((tm, LANES), lambda i: (proj_tile(i) % tiles_per_seq, 0))
    out, ko, vo, gvo = pl.pallas_call(
        functools.partial(_mix_prompt_kernel, tm=tm, tiles_per_seq=tiles_per_seq),
        grid=(n_tiles + 2,),
        in_specs=[pl.BlockSpec((1, tm, D_MODEL), lambda i: (proj_tile(i), 0, 0)),
                  pl.BlockSpec((1, tm, D_MODEL), lambda i: (out_tile(i), 0, 0)),
                  tab_spec, tab_spec, _resident((1, D_MODEL)), _resident((D_MODEL, D_IN)),
                  pl.BlockSpec(memory_space=pltpu.SMEM), _resident((1, D_GMLP)),
                  _resident((G_HEADS // 2, CHUNK, 2 * CHUNK)), _resident((CHUNK, D_GMLP)), _resident((1, Q_W)),
                  _resident((1, D_GMLP)), _resident((D_MODEL, D_MODEL))],
        out_specs=[pl.BlockSpec((1, tm, D_MODEL), lambda i: (out_tile(i), 0, 0)), last(KV_W), last(KV_W),
                   last(D_GMLP)],
        out_shape=[jax.ShapeDtypeStruct((n_tiles, tm, D_MODEL), F32), jax.ShapeDtypeStruct((b, WINDOW, KV_W), F32),
                   jax.ShapeDtypeStruct((b, WINDOW, KV_W), F32), jax.ShapeDtypeStruct((b, CHUNK, D_GMLP), F32)],
        scratch_shapes=[pltpu.VMEM((2, tm, Q_W), BF16), pltpu.VMEM((2, tm + WINDOW, KV_W), BF16),
                        pltpu.VMEM((2, tm + WINDOW, KV_W), BF16), pltpu.VMEM((2, tm, D_GMLP), F32),
                        pltpu.VMEM((2, tm, D_GMLP), BF16), pltpu.VMEM((2, tm, D_MODEL), BF16)],
        compiler_params=pltpu.CompilerParams(dimension_semantics=("arbitrary",), vmem_limit_bytes=VMEM_LIMIT_BYTES),
        name="mix_prompt",
    )(x_tiles, x_tiles, cos, sin_signed, gmix, w_in, sinks, gvg, wcat, bias_full, ga, gg, w_out)
    return out.reshape(b, s, D_MODEL), ko, vo, gvo


def _mix_sample_kernel(x_ref, ckt_ref, cvt_ref, cos_ref, sin_ref, gmix_ref, win_ref, sinks_ref, gvg_ref, coef_ref,
                       biasr_ref, ga_ref, gg_ref, wout_ref, o_ref, kot_ref, vot_ref, gvo_ref, q_s, k_s, v_s, ya_s,
                       yg_s, *, t_new, w_buf):
    step = pl.program_id(0)
    n_tok = x_ref.shape[0]
    step_seqs = SEQ_GROUP * GROUPS_PER_STEP
    step_rows = step_seqs * t_new
    grp_rows = SEQ_GROUP * t_new
    grp_keys = SEQ_GROUP * w_buf
    sub = 8

    @pl.when(step == 0)
    def _():
        q, k, v, u, gvn = _in_proj(x_ref[...], gmix_ref[...], win_ref[...], cos_ref[...], sin_ref[...], gvg_ref[...])
        q_s[...] = q
        k_s[...] = k
        v_s[...] = v
        gvo_ref[...] = gvn
        g3 = gvn.reshape(n_tok // sub, sub, D_GMLP)
        trow = lax.broadcasted_iota(jnp.int32, (1, sub, D_GMLP), 1) & (t_new - 1)
        mixed = biasr_ref[...][None] + coef_ref[0][None] * g3
        for d in range(1, t_new):
            shifted = jnp.where(trow >= d, pltpu.roll(g3, d, 1), 0.0)
            mixed = mixed + coef_ref[d][None] * shifted
        yg_s[...] = (u.reshape(n_tok // sub, sub, D_GMLP) * mixed).reshape(n_tok, D_GMLP)

    row0 = pl.multiple_of(step * step_rows, step_rows)
    q_step = q_s[pl.ds(row0, step_rows), :]
    kn = k_s[pl.ds(row0, step_rows), :]
    vn = v_s[pl.ds(row0, step_rows), :]

    knt = kn.T
    vnt = vn.T
    tail = lax.broadcasted_iota(jnp.int32, (KV_W, w_buf), 1) >= w_buf - t_new
    for b in range(step_seqs):
        shift = (w_buf - t_new - t_new * b) % w_buf
        kot_ref[b] = jnp.where(tail, pltpu.roll(knt, shift, 1) if shift else knt,
                               pltpu.roll(ckt_ref[b], w_buf - t_new, 1))
        vot_ref[b] = jnp.where(tail, pltpu.roll(vnt, shift, 1) if shift else vnt,
                               pltpu.roll(cvt_ref[b], w_buf - t_new, 1))

    knb = kn.astype(BF16)
    vnb = vn.astype(BF16)

    n_rows = GQA * grp_rows
    shift_t = t_new.bit_length() - 1
    shift_w = w_buf.bit_length() - 1
    r = lax.broadcasted_iota(jnp.int32, (n_rows, grp_keys), 0)
    c = lax.broadcasted_iota(jnp.int32, (n_rows, grp_keys), 1)
    mask_c = ((c >> shift_w) == ((r & (grp_rows - 1)) >> shift_t)) & ((c & (w_buf - 1)) > (r & (t_new - 1)))
    r2 = lax.broadcasted_iota(jnp.int32, (n_rows, grp_rows), 0)
    c2 = lax.broadcasted_iota(jnp.int32, (n_rows, grp_rows), 1)
    mask_n = ((c2 >> shift_t) == ((r2 & (grp_rows - 1)) >> shift_t)) & ((c2 & (t_new - 1)) <= (r2 & (t_new - 1)))
    row_head = lax.broadcasted_iota(jnp.int32, (n_rows, 1), 0) >> (grp_rows.bit_length() - 1)

    for grp in range(GROUPS_PER_STEP):
        rows = slice(grp * grp_rows, (grp + 1) * grp_rows)
        seqs = range(grp * SEQ_GROUP, (grp + 1) * SEQ_GROUP)
        for kv in range(N_KV):
            heads = [kv * GQA + i for i in range(GQA)]
            lanes = slice(HEAD_DIM * kv, HEAD_DIM * (kv + 1))
            kt = jnp.concatenate([ckt_ref[b, lanes, :] for b in seqs], axis=1).astype(BF16)
            vt = jnp.concatenate([cvt_ref[b, lanes, :] for b in seqs], axis=1).astype(BF16)
            qs = jnp.concatenate([q_step[rows, HEAD_DIM * hd:HEAD_DIM * (hd + 1)] for hd in heads], axis=0)
            s_c = jnp.where(mask_c, jnp.dot(qs, kt, preferred_element_type=F32), -jnp.inf)
            s_n = jnp.where(mask_n, lax.dot_general(qs, knb[rows, lanes], _NT, preferred_element_type=F32), -jnp.inf)
            sink = jnp.full((n_rows, 1), sinks_ref[heads[0]], F32)
            for i in range(1, GQA):
                sink = jnp.where(row_head == i, sinks_ref[heads[i]], sink)
            m = jnp.maximum(jnp.maximum(jnp.max(s_c, axis=1, keepdims=True), jnp.max(s_n, axis=1, keepdims=True)),
                            sink)
            p_c = jnp.exp(s_c - m)
            p_n = jnp.exp(s_n - m)
            den = jnp.sum(p_c, axis=1, keepdims=True) + jnp.sum(p_n, axis=1, keepdims=True) + jnp.exp(sink - m)
            o = (lax.dot_general(p_c.astype(BF16), vt, _NT, preferred_element_type=F32)
                 + jnp.dot(p_n.astype(BF16), vnb[rows, lanes], preferred_element_type=F32)) / den
            for i, hd in enumerate(heads):
                ya_s[pl.ds(row0 + grp * grp_rows, grp_rows), HEAD_DIM * hd:HEAD_DIM * (hd + 1)] = (
                    o[i * grp_rows:(i + 1) * grp_rows])

    @pl.when(step == pl.num_programs(0) - 1)
    def _():
        o_ref[...] = _out_proj(x_ref[...], ya_s[...], yg_s[...], ga_ref[...], gg_ref[...], wout_ref[...])


def _mix_sample(x, cache_kt, cache_vt, cos, sin_signed, gmix, w_in, sinks, gvg, coef, bias_rows, ga, gg, w_out, t_new):
    n_tok = x.shape[0]
    n_seq, _, w_buf = cache_kt.shape
    step_seqs = SEQ_GROUP * GROUPS_PER_STEP
    assert n_seq % step_seqs == 0 and n_tok == n_seq * t_new
    assert t_new & (t_new - 1) == 0 and w_buf & (w_buf - 1) == 0 and 8 % t_new == 0
    assert step_seqs * t_new == w_buf == LANES
    cache_spec = pl.BlockSpec((step_seqs, KV_W, w_buf), lambda i: (i, 0, 0))
    return pl.pallas_call(
        functools.partial(_mix_sample_kernel, t_new=t_new, w_buf=w_buf),
        grid=(n_seq // step_seqs,),
        in_specs=[_resident((n_tok, D_MODEL)), cache_spec, cache_spec, _resident((n_tok, LANES)),
                  _resident((n_tok, LANES)), _resident((1, D_MODEL)), _resident((D_MODEL, D_IN)),
                  pl.BlockSpec(memory_space=pltpu.SMEM), _resident((1, D_GMLP)), _resident((t_new, 8, D_GMLP)),
                  _resident((8, D_GMLP)), _resident((1, Q_W)), _resident((1, D_GMLP)),
                  _resident((D_MODEL, D_MODEL))],
        out_specs=[pl.BlockSpec((n_tok, D_MODEL), lambda i: (0, 0)), cache_spec, cache_spec,
                   pl.BlockSpec((n_tok, D_GMLP), lambda i: (0, 0))],
        out_shape=[jax.ShapeDtypeStruct((n_tok, D_MODEL), F32), jax.ShapeDtypeStruct(cache_kt.shape, F32),
                   jax.ShapeDtypeStruct(cache_vt.shape, F32), jax.ShapeDtypeStruct((n_tok, D_GMLP), F32)],
        scratch_shapes=[pltpu.VMEM((n_tok, Q_W), BF16), pltpu.VMEM((n_tok, KV_W), F32), pltpu.VMEM((n_tok, KV_W), F32),
                        pltpu.VMEM((n_tok, Q_W), F32), pltpu.VMEM((n_tok, D_GMLP), F32)],
        compiler_params=pltpu.CompilerParams(dimension_semantics=("arbitrary",), vmem_limit_bytes=VMEM_LIMIT_BYTES),
        name="mix_sample",
    )(x, cache_kt, cache_vt, cos, sin_signed, gmix, w_in, sinks, gvg, coef, bias_rows, ga, gg, w_out)


def _rope_tables(pos):
    inv_freq = ROPE_THETA ** (-jnp.arange(0, HEAD_DIM, 2, dtype=F32) / HEAD_DIM)
    ang = pos.astype(F32)[:, None] * inv_freq[None, :]
    c, s = jnp.cos(ang), jnp.sin(ang)
    return jnp.tile(c, (1, 4)), jnp.tile(jnp.concatenate([-s, s], axis=1), (1, 2))


def kernel(x_prompt, x_sample, cache_k_win, cache_v_win, norm_ffn1, ffn1_gate, ffn1_up, ffn1_down, norm_mix, w_in,
           attn_sinks, gmlp_v_norm, gmlp_w_s, gmlp_b_s, norm_attn_out, norm_gmlp_out, w_out, norm_ffn2, ffn2_gate,
           ffn2_up, ffn2_down, norm_final):
    depth = norm_ffn1.shape[0]
    b, s, _ = x_prompt.shape
    bd, t_new, _ = x_sample.shape
    w_buf = cache_k_win.shape[2]

    cos_p, sin_p = _rope_tables(jnp.arange(s, dtype=jnp.int32))
    cos_s, sin_s = _rope_tables(PAST_LEN + jnp.arange(t_new, dtype=jnp.int32))
    cos_s, sin_s = jnp.tile(cos_s, (bd, 1)), jnp.tile(sin_s, (bd, 1))

    hp = x_prompt.reshape(b * s, D_MODEL)
    hs = x_sample.reshape(bd * t_new, D_MODEL)
    outs = [[] for _ in range(6)]
    ffn1_w = [w[0].astype(BF16) for w in (ffn1_gate, ffn1_up, ffn1_down)]
    for l in range(depth):
        last = l == depth - 1
        row = lambda a: a[l].reshape(1, -1)

        wcat = gmlp_w_s[l].reshape(G_HEADS // 2, 2, CHUNK, CHUNK).transpose(0, 2, 1, 3).reshape(
            G_HEADS // 2, CHUNK, 2 * CHUNK)
        bias_full = jnp.repeat(gmlp_b_s[l].T, HEAD_DIM, axis=1)
        ws_small = gmlp_w_s[l][:, :t_new, :t_new]
        coef = jnp.stack([
            jnp.stack([ws_small[:, t, t - d] if t >= d else jnp.zeros((G_HEADS,), F32) for t in range(t_new)])
            for d in range(t_new)])
        coef = jnp.tile(jnp.repeat(coef, HEAD_DIM, axis=2), (1, 8 // t_new, 1))
        bias_rows = jnp.tile(jnp.repeat(gmlp_b_s[l][:, :t_new].T, HEAD_DIM, axis=1), (8 // t_new, 1))

        hp, hs, (w_in_b, w_out_b, wg2, wu2, wd2) = _ffn_half(
            hp, hs, norm_ffn1[l], *ffn1_w, cast=(w_in[l], w_out[l], ffn2_gate[l], ffn2_up[l], ffn2_down[l]))

        hp, kp, vp, gvp = _mix_prompt(hp.reshape(b, s, D_MODEL), cos_p, sin_p, row(norm_mix), w_in_b, attn_sinks[l],
                                      row(gmlp_v_norm), wcat, bias_full, row(norm_attn_out), row(norm_gmlp_out),
                                      w_out_b)
        to_t = lambda c: c.transpose(0, 2, 3, 1).reshape(bd, KV_W, w_buf)
        hs, kst, vst, gvs = _mix_sample(hs, to_t(cache_k_win[l]), to_t(cache_v_win[l]), cos_s, sin_s, row(norm_mix),
                                        w_in_b, attn_sinks[l], row(gmlp_v_norm), coef, bias_rows, row(norm_attn_out),
                                        row(norm_gmlp_out), w_out_b, t_new)
        from_t = lambda c: c.reshape(bd, N_KV, HEAD_DIM, w_buf).transpose(0, 3, 1, 2)

        next_ffn1 = () if last else (ffn1_gate[l + 1], ffn1_up[l + 1], ffn1_down[l + 1])
        hp, hs, ffn1_w = _ffn_half(hp.reshape(b * s, D_MODEL), hs, norm_ffn2[l], wg2, wu2, wd2,
                                   gf=norm_final if last else None, cast=next_ffn1)

        outs[0].append(kp.reshape(b, WINDOW, N_KV, HEAD_DIM))
        outs[1].append(vp.reshape(b, WINDOW, N_KV, HEAD_DIM))
        outs[2].append(from_t(kst))
        outs[3].append(from_t(vst))
        outs[4].append(gvp.reshape(b, CHUNK, G_HEADS, D_GMLP // G_HEADS))
        outs[5].append(gvs.reshape(bd, t_new, G_HEADS, D_GMLP // G_HEADS))

    return (hp.reshape(b, s, D_MODEL), hs.reshape(bd, t_new, D_MODEL)) + tuple(jnp.stack(o) for o in outs)
```

```python
import functools

import jax
import jax.numpy as jnp
from jax import lax
from jax.experimental import pallas as pl
from jax.experimental.pallas import tpu as pltpu

F32 = jnp.float32
BF16 = jnp.bfloat16

D_MODEL = 1024
D_FF = 2816
HEAD_DIM = 64
N_HEADS = 8
N_KV = 2
GQA = N_HEADS // N_KV
WINDOW = 128
CHUNK = 128
G_HEADS = 8
Q_W = N_HEADS * HEAD_DIM
KV_W = N_KV * HEAD_DIM
D_GMLP = 512
D_IN = Q_W + 2 * KV_W + 2 * D_GMLP
K_OFF = Q_W
V_OFF = K_OFF + KV_W
U_OFF = V_OFF + KV_W
GV_OFF = U_OFF + D_GMLP
ROPE_THETA = 10000.0
PAST_LEN = 16384
EPS = 1e-6
Q_SCALE = HEAD_DIM ** -0.5
LANES = 128
BF16_SUBLANES = 16

TOKEN_TILE = 512
FF_CHUNK = 256
SEQ_GROUP = 8
GROUPS_PER_STEP = 4
VMEM_LIMIT_BYTES = 56 * 1024 * 1024

_NT = (((1,), (1,)), ((), ()))


def _rms(x, g):
    ms = jnp.mean(x * x, axis=-1, keepdims=True)
    return (x * lax.rsqrt(ms + EPS)) * g


def _resident(shape):
    zeros = (0,) * len(shape)
    return pl.BlockSpec(shape, lambda *_: zeros, pipeline_mode=pl.Buffered(1))


def _ffn_kernel(*refs, final_norm, n_cast):
    refs = iter(refs)
    xp_ref, xn_ref, xs_ref, g_ref, wg_ref, wu_ref, wd_ref = (next(refs) for _ in range(7))
    gf_ref = next(refs) if final_norm else None
    cast_in = [next(refs) for _ in range(n_cast)]
    yp_ref, ys_ref = next(refs), next(refs)
    cast_out = [next(refs) for _ in range(n_cast)]
    h_ref, act0_ref, act_ref = next(refs), next(refs), next(refs)

    def gate_up(h, c):
        sl = slice(c * FF_CHUNK, (c + 1) * FF_CHUNK)
        gate = jnp.dot(h, wg_ref[:, sl], preferred_element_type=F32)
        up = jnp.dot(h, wu_ref[:, sl], preferred_element_type=F32)
        return (gate * jax.nn.sigmoid(gate) * up).astype(BF16)

    on_sample = pl.program_id(0) == 0

    @pl.when(on_sample)
    def _():
        h0 = _rms(xs_ref[...], g_ref[...]).astype(BF16)
        h_ref[...] = h0
        act0_ref[...] = gate_up(h0, 0)

    for c in range(1, D_FF // FF_CHUNK):
        act_ref[:, (c - 1) * FF_CHUNK:c * FF_CHUNK] = gate_up(h_ref[...], c)
    down0 = jnp.dot(act0_ref[...], wd_ref[0:FF_CHUNK, :], preferred_element_type=F32)
    hn = _rms(xn_ref[...], g_ref[...]).astype(BF16)
    h_ref[...] = hn
    act0_ref[...] = gate_up(hn, 0)
    x = jnp.where(on_sample, xs_ref[...], xp_ref[...])
    y = x + 0.5 * (down0 + jnp.dot(act_ref[...], wd_ref[FF_CHUNK:, :], preferred_element_type=F32))
    if final_norm:
        y = _rms(y, gf_ref[...])
    yp_ref[...] = y

    @pl.when(on_sample)
    def _():
        ys_ref[...] = yp_ref[...]

    for src, dst in zip(cast_in, cast_out):
        dst[...] = src[...].astype(BF16)


def _cast_row_blocks(rows, n_steps):
    return max(d for d in range(1, n_steps + 1) if rows % d == 0 and (rows // d) % BF16_SUBLANES == 0)


def _ffn_half(xp, xs, g, wg, wu, wd, gf=None, cast=()):
    tm = TOKEN_TILE
    n = xp.shape[0]
    assert n % tm == 0 and xs.shape[0] == tm
    n_tiles = n // tm
    prompt_spec = pl.BlockSpec((tm, D_MODEL), lambda i: (jnp.maximum(i - 1, 0), 0))
    next_spec = pl.BlockSpec((tm, D_MODEL), lambda i: (jnp.minimum(i, n_tiles - 1), 0))
    sample_in = _resident((tm, D_MODEL))
    sample_out = pl.BlockSpec((tm, D_MODEL), lambda i: (0, 0))
    in_specs = [prompt_spec, next_spec, sample_in, _resident((1, D_MODEL)), _resident((D_MODEL, D_FF)),
                _resident((D_MODEL, D_FF)), _resident((D_FF, D_MODEL))]
    args = [xp, xp, xs, g.reshape(1, D_MODEL), wg, wu, wd]
    if gf is not None:
        in_specs.append(_resident((1, D_MODEL)))
        args.append(gf.reshape(1, D_MODEL))
    out_specs = [prompt_spec, sample_out]
    out_shape = [jax.ShapeDtypeStruct((n, D_MODEL), F32), jax.ShapeDtypeStruct((tm, D_MODEL), F32)]
    cast_specs = []
    for w in cast:
        rows, cols = w.shape
        nb = _cast_row_blocks(rows, n_tiles)
        cast_specs.append(pl.BlockSpec((rows // nb, cols), lambda i, nb=nb: (jnp.minimum(i, nb - 1), 0)))
        out_shape.append(jax.ShapeDtypeStruct(w.shape, BF16))
    outs = pl.pallas_call(
        functools.partial(_ffn_kernel, final_norm=gf is not None, n_cast=len(cast)),
        grid=(n_tiles + 1,),
        in_specs=in_specs + cast_specs,
        out_specs=out_specs + cast_specs,
        out_shape=out_shape,
        scratch_shapes=[pltpu.VMEM((tm, D_MODEL), BF16), pltpu.VMEM((tm, FF_CHUNK), BF16),
                        pltpu.VMEM((tm, D_FF - FF_CHUNK), BF16)],
        compiler_params=pltpu.CompilerParams(dimension_semantics=("arbitrary",), vmem_limit_bytes=VMEM_LIMIT_BYTES),
        name="ffn_final" if gf is not None else "ffn_half",
    )(*args, *cast)
    return outs[0], outs[1], list(outs[2:])


def _rope(xg, cos, sin_signed):
    lane = lax.broadcasted_iota(jnp.int32, xg.shape, 1)
    first_half = (lane & (HEAD_DIM - 1)) < HEAD_DIM // 2
    swapped = jnp.where(first_half, pltpu.roll(xg, LANES - HEAD_DIM // 2, 1), pltpu.roll(xg, HEAD_DIM // 2, 1))
    return xg * cos + swapped * sin_signed


def _in_proj(x, gmix, w_in, cos, sin_signed, gvn_gain):
    h = _rms(x, gmix).astype(BF16)
    z = jnp.dot(h, w_in, preferred_element_type=F32)
    q = jnp.concatenate(
        [(_rope(z[:, LANES * i:LANES * (i + 1)], cos, sin_signed) * Q_SCALE).astype(BF16) for i in range(Q_W // LANES)],
        axis=1)
    k = _rope(z[:, K_OFF:V_OFF], cos, sin_signed)
    v = z[:, V_OFF:U_OFF]
    u = jax.nn.gelu(z[:, U_OFF:GV_OFF])
    gvn = _rms(jax.nn.gelu(z[:, GV_OFF:]), gvn_gain)
    return q, k, v, u, gvn


def _out_proj(x, ya, yg, ga, gg, w_out):
    cat = jnp.concatenate([_rms(ya, ga).astype(BF16), _rms(yg, gg).astype(BF16)], axis=1)
    return x + jnp.dot(cat, w_out, preferred_element_type=F32)


def _softmax_sink(s, sink):
    m = jnp.maximum(jnp.max(s, axis=1, keepdims=True), sink)
    p = jnp.exp(s - m)
    return p, jnp.sum(p, axis=1, keepdims=True) + jnp.exp(sink - m)


def _mix_prompt_kernel(xa_ref, xc_ref, cos_ref, sin_ref, gmix_ref, win_ref, sinks_ref, gvg_ref, wcat_ref, bias_ref,
                       ga_ref, gg_ref, wout_ref, o_ref, ko_ref, vo_ref, gvo_ref, q_s, k_s, v_s, u_s, gv_s, cat_s, *,
                       tm, tiles_per_seq):
    s = pl.program_id(0)

    @pl.when(s == 0)
    def _():
        q_s[1] = jnp.zeros((tm, Q_W), BF16)
        k_s[1] = jnp.zeros((tm + WINDOW, KV_W), BF16)
        v_s[1] = jnp.zeros((tm + WINDOW, KV_W), BF16)
        u_s[1] = jnp.zeros((tm, D_GMLP), F32)
        gv_s[1] = jnp.zeros((tm, D_GMLP), BF16)
        cat_s[0] = jnp.zeros((tm, D_MODEL), BF16)

    for parity in range(2):
        pl.when(s % 2 == parity)(functools.partial(
            _mix_prompt_step, s, parity, xa_ref, xc_ref, cos_ref, sin_ref, gmix_ref, win_ref, sinks_ref, gvg_ref,
            wcat_ref, bias_ref, ga_ref, gg_ref, wout_ref, o_ref, ko_ref, vo_ref, gvo_ref, q_s, k_s, v_s, u_s, gv_s,
            cat_s, tm=tm, tiles_per_seq=tiles_per_seq))


def _mix_prompt_step(s, cur, xa_ref, xc_ref, cos_ref, sin_ref, gmix_ref, win_ref, sinks_ref, gvg_ref, wcat_ref, bias_ref,
                     ga_ref, gg_ref, wout_ref, o_ref, ko_ref, vo_ref, gvo_ref, q_s, k_s, v_s, u_s, gv_s, cat_s, *, tm,
                     tiles_per_seq):
    oth = 1 - cur

    h = _rms(xa_ref[0], gmix_ref[...]).astype(BF16)
    cos = cos_ref[...]
    sin = sin_ref[...]

    def proj_q():
        z = jnp.dot(h, win_ref[:, 0:Q_W], preferred_element_type=F32)
        for i in range(Q_W // LANES):
            q_s[cur, :, LANES * i:LANES * (i + 1)] = (
                _rope(z[:, LANES * i:LANES * (i + 1)], cos, sin) * Q_SCALE).astype(BF16)

    def proj_kv():
        z = jnp.dot(h, win_ref[:, K_OFF:U_OFF], preferred_element_type=F32)
        k = _rope(z[:, 0:KV_W], cos, sin)
        v = z[:, KV_W:]
        k_s[cur, 0:WINDOW, :] = k_s[oth, tm:tm + WINDOW, :]
        v_s[cur, 0:WINDOW, :] = v_s[oth, tm:tm + WINDOW, :]
        k_s[cur, WINDOW:, :] = k.astype(BF16)
        v_s[cur, WINDOW:, :] = v.astype(BF16)
        ko_ref[0] = k[tm - WINDOW:]
        vo_ref[0] = v[tm - WINDOW:]

    def proj_u():
        u_s[cur] = jax.nn.gelu(jnp.dot(h, win_ref[:, U_OFF:GV_OFF], preferred_element_type=F32))

    def proj_gv():
        gvn = _rms(jax.nn.gelu(jnp.dot(h, win_ref[:, GV_OFF:], preferred_element_type=F32)), gvg_ref[...])
        gv_s[cur] = gvn.astype(BF16)
        gvo_ref[0] = gvn[tm - CHUNK:]

    def out_half(c):
        cols = slice(c * (D_MODEL // 2), (c + 1) * (D_MODEL // 2))

        def run():
            o_ref[0, :, cols] = xc_ref[0, :, cols] + jnp.dot(cat_s[cur], wout_ref[:, cols], preferred_element_type=F32)
        return run

    mid_fill = [[out_half(0)], [proj_q], [proj_kv, proj_u], [proj_gv]]
    end_fill = [[], [], [], [out_half(1)]]
    assert len(mid_fill) == len(end_fill) == tm // WINDOW

    wrow = lax.broadcasted_iota(jnp.int32, (CHUNK, 2 * CHUNK), 0)
    wcol = lax.broadcasted_iota(jnp.int32, (CHUNK, 2 * CHUNK), 1) & (CHUNK - 1)
    wmix = [jnp.where(wcol <= wrow, wcat_ref[p], 0.0).astype(BF16) for p in range(G_HEADS // 2)]
    lane = lax.broadcasted_iota(jnp.int32, (CHUNK, LANES), 1)
    low_head = lane < HEAD_DIM

    qi = lax.broadcasted_iota(jnp.int32, (WINDOW, 2 * WINDOW), 0)
    sj = lax.broadcasted_iota(jnp.int32, (WINDOW, 2 * WINDOW), 1)
    dist = WINDOW + qi - sj
    band = (dist >= 0) & (dist < WINDOW)
    first_lo = jnp.where((s + tiles_per_seq - 1) % tiles_per_seq == 0, WINDOW, 0)

    for j in range(tm // WINDOW):
        rows = slice(j * WINDOW, (j + 1) * WINDOW)
        mask = band & (sj >= first_lo) if j == 0 else band
        qb = q_s[oth, rows, :]
        scores = []
        for kv in range(N_KV):
            qs = jnp.concatenate(
                [qb[:, HEAD_DIM * hd:HEAD_DIM * (hd + 1)] for hd in range(kv * GQA, (kv + 1) * GQA)], axis=0)
            kb = k_s[oth, j * WINDOW:(j + 2) * WINDOW, HEAD_DIM * kv:HEAD_DIM * (kv + 1)]
            scores.append(lax.dot_general(qs, kb, _NT, preferred_element_type=F32))
        for run in mid_fill[j]:
            run()
        outs, dens = [], []
        for kv in range(N_KV):
            vb = v_s[oth, j * WINDOW:(j + 2) * WINDOW, HEAD_DIM * kv:HEAD_DIM * (kv + 1)]
            ps = []
            for g in range(GQA):
                sg = jnp.where(mask, scores[kv][g * WINDOW:(g + 1) * WINDOW], -jnp.inf)
                p, den = _softmax_sink(sg, sinks_ref[kv * GQA + g])
                ps.append(p.astype(BF16))
                dens.append(den)
            outs.append(jnp.dot(jnp.concatenate(ps, axis=0), vb, preferred_element_type=F32))
        mixed = []
        for p in range(G_HEADS // 2):
            r = gv_s[oth, rows, LANES * p:LANES * (p + 1)]
            zero = jnp.zeros_like(r)
            rhs = jnp.concatenate([jnp.where(low_head, r, zero), jnp.where(low_head, zero, r)], axis=0)
            mixed.append(jnp.dot(wmix[p], rhs, preferred_element_type=F32))
        for run in end_fill[j]:
            run()
        ya = [outs[hd // GQA][(hd % GQA) * WINDOW:(hd % GQA + 1) * WINDOW] / dens[hd] for hd in range(N_HEADS)]
        cat_s[oth, rows, 0:Q_W] = _rms(jnp.concatenate(ya, axis=1), ga_ref[...]).astype(BF16)
        yg = u_s[oth, rows, :] * (jnp.concatenate(mixed, axis=1) + bias_ref[...])
        cat_s[oth, rows, Q_W:] = _rms(yg, gg_ref[...]).astype(BF16)


def _mix_prompt(x, cos, sin_signed, gmix, w_in, sinks, gvg, wcat, bias_full, ga, gg, w_out):
    b, s, _ = x.shape
    tm = TOKEN_TILE
    assert s % tm == 0 and tm % WINDOW == 0
    tiles_per_seq = s // tm
    n_tiles = b * tiles_per_seq
    proj_tile = lambda i: jnp.minimum(i, n_tiles - 1)
    out_tile = lambda i: jnp.maximum(i - 2, 0)
    x_tiles = x.reshape(n_tiles, tm, D_MODEL)
    last = lambda width: pl.BlockSpec((1, WINDOW, width), lambda i: (proj_tile(i) // tiles_per_seq, 0, 0))
    tab_spec = pl.BlockSpec((tm, LANES), lambda i: (proj_tile(i) % tiles_per_seq, 0))
    out, ko, vo, gvo = pl.pallas_call(
        functools.partial(_mix_prompt_kernel, tm=tm, tiles_per_seq=tiles_per_seq),
        grid=(n_tiles + 2,),
        in_specs=[pl.BlockSpec((1, tm, D_MODEL), lambda i: (proj_tile(i), 0, 0)),
                  pl.BlockSpec((1, tm, D_MODEL), lambda i: (out_tile(i), 0, 0)),
                  tab_spec, tab_spec, _resident((1, D_MODEL)), _resident((D_MODEL, D_IN)),
                  pl.BlockSpec(memory_space=pltpu.SMEM), _resident((1, D_GMLP)),
                  _resident((G_HEADS // 2, CHUNK, 2 * CHUNK)), _resident((CHUNK, D_GMLP)), _resident((1, Q_W)),
                  _resident((1, D_GMLP)), _resident((D_MODEL, D_MODEL))],
        out_specs=[pl.BlockSpec((1, tm, D_MODEL), lambda i: (out_tile(i), 0, 0)), last(KV_W), last(KV_W),
                   last(D_GMLP)],
        out_shape=[jax.ShapeDtypeStruct((n_tiles, tm, D_MODEL), F32), jax.ShapeDtypeStruct((b, WINDOW, KV_W), F32),
                   jax.ShapeDtypeStruct((b, WINDOW, KV_W), F32), jax.ShapeDtypeStruct((b, CHUNK, D_GMLP), F32)],
        scratch_shapes=[pltpu.VMEM((2, tm, Q_W), BF16), pltpu.VMEM((2, tm + WINDOW, KV_W), BF16),
                        pltpu.VMEM((2, tm + WINDOW, KV_W), BF16), pltpu.VMEM((2, tm, D_GMLP), F32),
                        pltpu.VMEM((2, tm, D_GMLP), BF16), pltpu.VMEM((2, tm, D_MODEL), BF16)],
        compiler_params=pltpu.CompilerParams(dimension_semantics=("arbitrary",), vmem_limit_bytes=VMEM_LIMIT_BYTES),
        name="mix_prompt",
    )(x_tiles, x_tiles, cos, sin_signed, gmix, w_in, sinks, gvg, wcat, bias_full, ga, gg, w_out)
    return out.reshape(b, s, D_MODEL), ko, vo, gvo


def _mix_sample_kernel(x_ref, ckt_ref, cvt_ref, cos_ref, sin_ref, gmix_ref, win_ref, sinks_ref, gvg_ref, coef_ref,
                       biasr_ref, ga_ref, gg_ref, wout_ref, o_ref, kot_ref, vot_ref, gvo_ref, q_s, k_s, v_s, ya_s,
                       yg_s, *, t_new, w_buf):
    step = pl.program_id(0)
    n_tok = x_ref.shape[0]
    step_seqs = SEQ_GROUP * GROUPS_PER_STEP
    step_rows = step_seqs * t_new
    grp_rows = SEQ_GROUP * t_new
    grp_keys = SEQ_GROUP * w_buf
    sub = 8

    @pl.when(step == 0)
    def _():
        q, k, v, u, gvn = _in_proj(x_ref[...], gmix_ref[...], win_ref[...], cos_ref[...], sin_ref[...], gvg_ref[...])
        q_s[...] = q
        k_s[...] = k
        v_s[...] = v
        gvo_ref[...] = gvn
        g3 = gvn.reshape(n_tok // sub, sub, D_GMLP)
        trow = lax.broadcasted_iota(jnp.int32, (1, sub, D_GMLP), 1) & (t_new - 1)
        mixed = biasr_ref[...][None] + coef_ref[0][None] * g3
        for d in range(1, t_new):
            shifted = jnp.where(trow >= d, pltpu.roll(g3, d, 1), 0.0)
            mixed = mixed + coef_ref[d][None] * shifted
        yg_s[...] = (u.reshape(n_tok // sub, sub, D_GMLP) * mixed).reshape(n_tok, D_GMLP)

    row0 = pl.multiple_of(step * step_rows, step_rows)
    q_step = q_s[pl.ds(row0, step_rows), :]
    kn = k_s[pl.ds(row0, step_rows), :]
    vn = v_s[pl.ds(row0, step_rows), :]

    knt = kn.T
    vnt = vn.T
    tail = lax.broadcasted_iota(jnp.int32, (KV_W, w_buf), 1) >= w_buf - t_new
    for b in range(step_seqs):
        shift = (w_buf - t_new - t_new * b) % w_buf
        kot_ref[b] = jnp.where(tail, pltpu.roll(knt, shift, 1) if shift else knt,
                               pltpu.roll(ckt_ref[b], w_buf - t_new, 1))
        vot_ref[b] = jnp.where(tail, pltpu.roll(vnt, shift, 1) if shift else vnt,
                               pltpu.roll(cvt_ref[b], w_buf - t_new, 1))

    knb = kn.astype(BF16)
    vnb = vn.astype(BF16)

    n_rows = GQA * grp_rows
    shift_t = t_new.bit_length() - 1
    shift_w = w_buf.bit_length() - 1
    r = lax.broadcasted_iota(jnp.int32, (n_rows, grp_keys), 0)
    c = lax.broadcasted_iota(jnp.int32, (n_rows, grp_keys), 1)
    mask_c = ((c >> shift_w) == ((r & (grp_rows - 1)) >> shift_t)) & ((c & (w_buf - 1)) > (r & (t_new - 1)))
    r2 = lax.broadcasted_iota(jnp.int32, (n_rows, grp_rows), 0)
    c2 = lax.broadcasted_iota(jnp.int32, (n_rows, grp_rows), 1)
    mask_n = ((c2 >> shift_t) == ((r2 & (grp_rows - 1)) >> shift_t)) & ((c2 & (t_new - 1)) <= (r2 & (t_new - 1)))
    row_head = lax.broadcasted_iota(jnp.int32, (n_rows, 1), 0) >> (grp_rows.bit_length() - 1)

    for grp in range(GROUPS_PER_STEP):
        rows = slice(grp * grp_rows, (grp + 1) * grp_rows)
        seqs = range(grp * SEQ_GROUP, (grp + 1) * SEQ_GROUP)
        for kv in range(N_KV):
            heads = [kv * GQA + i for i in range(GQA)]
            lanes = slice(HEAD_DIM * kv, HEAD_DIM * (kv + 1))
            kt = jnp.concatenate([ckt_ref[b, lanes, :] for b in seqs], axis=1).astype(BF16)
            vt = jnp.concatenate([cvt_ref[b, lanes, :] for b in seqs], axis=1).astype(BF16)
            qs = jnp.concatenate([q_step[rows, HEAD_DIM * hd:HEAD_DIM * (hd + 1)] for hd in heads], axis=0)
            s_c = jnp.where(mask_c, jnp.dot(qs, kt, preferred_element_type=F32), -jnp.inf)
            s_n = jnp.where(mask_n, lax.dot_general(qs, knb[rows, lanes], _NT, preferred_element_type=F32), -jnp.inf)
            sink = jnp.full((n_rows, 1), sinks_ref[heads[0]], F32)
            for i in range(1, GQA):
                sink = jnp.where(row_head == i, sinks_ref[heads[i]], sink)
            m = jnp.maximum(jnp.maximum(jnp.max(s_c, axis=1, keepdims=True), jnp.max(s_n, axis=1, keepdims=True)),
                            sink)
            p_c = jnp.exp(s_c - m)
            p_n = jnp.exp(s_n - m)
            den = jnp.sum(p_c, axis=1, keepdims=True) + jnp.sum(p_n, axis=1, keepdims=True) + jnp.exp(sink - m)
            o = (lax.dot_general(p_c.astype(BF16), vt, _NT, preferred_element_type=F32)
                 + jnp.dot(p_n.astype(BF16), vnb[rows, lanes], preferred_element_type=F32)) / den
            for i, hd in enumerate(heads):
                ya_s[pl.ds(row0 + grp * grp_rows, grp_rows), HEAD_DIM * hd:HEAD_DIM * (hd + 1)] = (
                    o[i * grp_rows:(i + 1) * grp_rows])

    @pl.when(step == pl.num_programs(0) - 1)
    def _():
        o_ref[...] = _out_proj(x_ref[...], ya_s[...], yg_s[...], ga_ref[...], gg_ref[...], wout_ref[...])


def _mix_sample(x, cache_kt, cache_vt, cos, sin_signed, gmix, w_in, sinks, gvg, coef, bias_rows, ga, gg, w_out, t_new):
    n_tok = x.shape[0]
    n_seq, _, w_buf = cache_kt.shape
    step_seqs = SEQ_GROUP * GROUPS_PER_STEP
    assert n_seq % step_seqs == 0 and n_tok == n_seq * t_new
    assert t_new & (t_new - 1) == 0 and w_buf & (w_buf - 1) == 0 and 8 % t_new == 0
    assert step_seqs * t_new == w_buf == LANES
    cache_spec = pl.BlockSpec((step_seqs, KV_W, w_buf), lambda i: (i, 0, 0))
    return pl.pallas_call(
        functools.partial(_mix_sample_kernel, t_new=t_new, w_buf=w_buf),
        grid=(n_seq // step_seqs,),
        in_specs=[_resident((n_tok, D_MODEL)), cache_spec, cache_spec, _resident((n_tok, LANES)),
                  _resident((n_tok, LANES)), _resident((1, D_MODEL)), _resident((D_MODEL, D_IN)),
                  pl.BlockSpec(memory_space=pltpu.SMEM), _resident((1, D_GMLP)), _resident((t_new, 8, D_GMLP)),
                  _resident((8, D_GMLP)), _resident((1, Q_W)), _resident((1, D_GMLP)),
                  _resident((D_MODEL, D_MODEL))],
        out_specs=[pl.BlockSpec((n_tok, D_MODEL), lambda i: (0, 0)), cache_spec, cache_spec,
                   pl.BlockSpec((n_tok, D_GMLP), lambda i: (0, 0))],
        out_shape=[jax.ShapeDtypeStruct((n_tok, D_MODEL), F32), jax.ShapeDtypeStruct(cache_kt.shape, F32),
                   jax.ShapeDtypeStruct(cache_vt.shape, F32), jax.ShapeDtypeStruct((n_tok, D_GMLP), F32)],
        scratch_shapes=[pltpu.VMEM((n_tok, Q_W), BF16), pltpu.VMEM((n_tok, KV_W), F32), pltpu.VMEM((n_tok, KV_W), F32),
                        pltpu.VMEM((n_tok, Q_W), F32), pltpu.VMEM((n_tok, D_GMLP), F32)],
        compiler_params=pltpu.CompilerParams(dimension_semantics=("arbitrary",), vmem_limit_bytes=VMEM_LIMIT_BYTES),
        name="mix_sample",
    )(x, cache_kt, cache_vt, cos, sin_signed, gmix, w_in, sinks, gvg, coef, bias_rows, ga, gg, w_out)


def _rope_tables(pos):
    inv_freq = ROPE_THETA ** (-jnp.arange(0, HEAD_DIM, 2, dtype=F32) / HEAD_DIM)
    ang = pos.astype(F32)[:, None] * inv_freq[None, :]
    c, s = jnp.cos(ang), jnp.sin(ang)
    return jnp.tile(c, (1, 4)), jnp.tile(jnp.concatenate([-s, s], axis=1), (1, 2))


def kernel(x_prompt, x_sample, cache_k_win, cache_v_win, norm_ffn1, ffn1_gate, ffn1_up, ffn1_down, norm_mix, w_in,
           attn_sinks, gmlp_v_norm, gmlp_w_s, gmlp_b_s, norm_attn_out, norm_gmlp_out, w_out, norm_ffn2, ffn2_gate,
           ffn2_up, ffn2_down, norm_final):
    depth = norm_ffn1.shape[0]
    b, s, _ = x_prompt.shape
    bd, t_new, _ = x_sample.shape
    w_buf = cache_k_win.shape[2]

    cos_p, sin_p = _rope_tables(jnp.arange(s, dtype=jnp.int32))
    cos_s, sin_s = _rope_tables(PAST_LEN + jnp.arange(t_new, dtype=jnp.int32))
    cos_s, sin_s = jnp.tile(cos_s, (bd, 1)), jnp.tile(sin_s, (bd, 1))

    hp = x_prompt.reshape(b * s, D_MODEL)
    hs = x_sample.reshape(bd * t_new, D_MODEL)
    outs = [[] for _ in range(6)]
    ffn1_w = [w[0].astype(BF16) for w in (ffn1_gate, ffn1_up, ffn1_down)]
    for l in range(depth):
        last = l == depth - 1
        row = lambda a: a[l].reshape(1, -1)

        wcat = gmlp_w_s[l].reshape(G_HEADS // 2, 2, CHUNK, CHUNK).transpose(0, 2, 1, 3).reshape(
            G_HEADS // 2, CHUNK, 2 * CHUNK)
        bias_full = jnp.repeat(gmlp_b_s[l].T, HEAD_DIM, axis=1)
        ws_small = gmlp_w_s[l][:, :t_new, :t_new]
        coef = jnp.stack([
            jnp.stack([ws_small[:, t, t - d] if t >= d else jnp.zeros((G_HEADS,), F32) for t in range(t_new)])
            for d in range(t_new)])
        coef = jnp.tile(jnp.repeat(coef, HEAD_DIM, axis=2), (1, 8 // t_new, 1))
        bias_rows = jnp.tile(jnp.repeat(gmlp_b_s[l][:, :t_new].T, HEAD_DIM, axis=1), (8 // t_new, 1))

        hp, hs, (w_in_b, w_out_b, wg2, wu2, wd2) = _ffn_half(
            hp, hs, norm_ffn1[l], *ffn1_w, cast=(w_in[l], w_out[l], ffn2_gate[l], ffn2_up[l], ffn2_down[l]))

        hp, kp, vp, gvp = _mix_prompt(hp.reshape(b, s, D_MODEL), cos_p, sin_p, row(norm_mix), w_in_b, attn_sinks[l],
                                      row(gmlp_v_norm), wcat, bias_full, row(norm_attn_out), row(norm_gmlp_out),
                                      w_out_b)
        to_t = lambda c: c.transpose(0, 2, 3, 1).reshape(bd, KV_W, w_buf)
        hs, kst, vst, gvs = _mix_sample(hs, to_t(cache_k_win[l]), to_t(cache_v_win[l]), cos_s, sin_s, row(norm_mix),
                                        w_in_b, attn_sinks[l], row(gmlp_v_norm), coef, bias_rows, row(norm_attn_out),
                                        row(norm_gmlp_out), w_out_b, t_new)
        from_t = lambda c: c.reshape(bd, N_KV, HEAD_DIM, w_buf).transpose(0, 3, 1, 2)

        next_ffn1 = () if last else (ffn1_gate[l + 1], ffn1_up[l + 1], ffn1_down[l + 1])
        hp, hs, ffn1_w = _ffn_half(hp.reshape(b * s, D_MODEL), hs, norm_ffn2[l], wg2, wu2, wd2,
                                   gf=norm_final if last else None, cast=next_ffn1)

        outs[0].append(kp.reshape(b, WINDOW, N_KV, HEAD_DIM))
        outs[1].append(vp.reshape(b, WINDOW, N_KV, HEAD_DIM))
        outs[2].append(from_t(kst))
        outs[3].append(from_t(vst))
        outs[4].append(gvp.reshape(b, CHUNK, G_HEADS, D_GMLP // G_HEADS))
        outs[5].append(gvs.reshape(bd, t_new, G_HEADS, D_GMLP // G_HEADS))

    return (hp.reshape(b, s, D_MODEL), hs.reshape(bd, t_new, D_MODEL)) + tuple(jnp.stack(o) for o in outs)
```

```python
import functools

import jax
import jax.numpy as jnp
from jax import lax
from jax.experimental import pallas as pl
from jax.experimental.pallas import tpu as pltpu

F32 = jnp.float32
BF16 = jnp.bfloat16

D_MODEL = 1024
D_FF = 2816
HEAD_DIM = 64
N_HEADS = 8
N_KV = 2
GQA = N_HEADS // N_KV
WINDOW = 128
CHUNK = 128
G_HEADS = 8
Q_W = N_HEADS * HEAD_DIM
KV_W = N_KV * HEAD_DIM
D_GMLP = 512
D_IN = Q_W + 2 * KV_W + 2 * D_GMLP
K_OFF = Q_W
V_OFF = K_OFF + KV_W
U_OFF = V_OFF + KV_W
GV_OFF = U_OFF + D_GMLP
ROPE_THETA = 10000.0
PAST_LEN = 16384
EPS = 1e-6
Q_SCALE = HEAD_DIM ** -0.5
LANES = 128
BF16_SUBLANES = 16

TOKEN_TILE = 512
FF_CHUNK = 256
SEQ_GROUP = 8
GROUPS_PER_STEP = 4
VMEM_LIMIT_BYTES = 56 * 1024 * 1024

_NT = (((1,), (1,)), ((), ()))


def _rms(x, g):
    ms = jnp.mean(x * x, axis=-1, keepdims=True)
    return (x * lax.rsqrt(ms + EPS)) * g


def _resident(shape):
    zeros = (0,) * len(shape)
    return pl.BlockSpec(shape, lambda *_: zeros, pipeline_mode=pl.Buffered(1))


def _ffn_kernel(*refs, final_norm, n_cast):
    refs = iter(refs)
    xp_ref, xn_ref, xs_ref, g_ref, wg_ref, wu_ref, wd_ref = (next(refs) for _ in range(7))
    gf_ref = next(refs) if final_norm else None
    cast_in = [next(refs) for _ in range(n_cast)]
    yp_ref, ys_ref = next(refs), next(refs)
    cast_out = [next(refs) for _ in range(n_cast)]
    h_ref, act0_ref, act_ref = next(refs), next(refs), next(refs)

    def gate_up(h, c):
        sl = slice(c * FF_CHUNK, (c + 1) * FF_CHUNK)
        gate = jnp.dot(h, wg_ref[:, sl], preferred_element_type=F32)
        up = jnp.dot(h, wu_ref[:, sl], preferred_element_type=F32)
        return (gate * jax.nn.sigmoid(gate) * up).astype(BF16)

    on_sample = pl.program_id(0) == 0

    @pl.when(on_sample)
    def _():
        h0 = _rms(xs_ref[...], g_ref[...]).astype(BF16)
        h_ref[...] = h0
        act0_ref[...] = gate_up(h0, 0)

    for c in range(1, D_FF // FF_CHUNK):
        act_ref[:, (c - 1) * FF_CHUNK:c * FF_CHUNK] = gate_up(h_ref[...], c)
    down0 = jnp.dot(act0_ref[...], wd_ref[0:FF_CHUNK, :], preferred_element_type=F32)
    hn = _rms(xn_ref[...], g_ref[...]).astype(BF16)
    h_ref[...] = hn
    act0_ref[...] = gate_up(hn, 0)
    x = jnp.where(on_sample, xs_ref[...], xp_ref[...])
    y = x + 0.5 * (down0 + jnp.dot(act_ref[...], wd_ref[FF_CHUNK:, :], preferred_element_type=F32))
    if final_norm:
        y = _rms(y, gf_ref[...])
    yp_ref[...] = y

    @pl.when(on_sample)
    def _():
        ys_ref[...] = yp_ref[...]

    for src, dst in zip(cast_in, cast_out):
        dst[...] = src[...].astype(BF16)


def _cast_row_blocks(rows, n_steps):
    return max(d for d in range(1, n_steps + 1) if rows % d == 0 and (rows // d) % BF16_SUBLANES == 0)


def _ffn_half(xp, xs, g, wg, wu, wd, gf=None, cast=()):
    tm = TOKEN_TILE
    n = xp.shape[0]
    assert n % tm == 0 and xs.shape[0] == tm
    n_tiles = n // tm
    prompt_spec = pl.BlockSpec((tm, D_MODEL), lambda i: (jnp.maximum(i - 1, 0), 0))
    next_spec = pl.BlockSpec((tm, D_MODEL), lambda i: (jnp.minimum(i, n_tiles - 1), 0))
    sample_in = _resident((tm, D_MODEL))
    sample_out = pl.BlockSpec((tm, D_MODEL), lambda i: (0, 0))
    in_specs = [prompt_spec, next_spec, sample_in, _resident((1, D_MODEL)), _resident((D_MODEL, D_FF)),
                _resident((D_MODEL, D_FF)), _resident((D_FF, D_MODEL))]
    args = [xp, xp, xs, g.reshape(1, D_MODEL), wg, wu, wd]
    if gf is not None:
        in_specs.append(_resident((1, D_MODEL)))
        args.append(gf.reshape(1, D_MODEL))
    out_specs = [prompt_spec, sample_out]
    out_shape = [jax.ShapeDtypeStruct((n, D_MODEL), F32), jax.ShapeDtypeStruct((tm, D_MODEL), F32)]
    cast_specs = []
    for w in cast:
        rows, cols = w.shape
        nb = _cast_row_blocks(rows, n_tiles)
        cast_specs.append(pl.BlockSpec((rows // nb, cols), lambda i, nb=nb: (jnp.minimum(i, nb - 1), 0)))
        out_shape.append(jax.ShapeDtypeStruct(w.shape, BF16))
    outs = pl.pallas_call(
        functools.partial(_ffn_kernel, final_norm=gf is not None, n_cast=len(cast)),
        grid=(n_tiles + 1,),
        in_specs=in_specs + cast_specs,
        out_specs=out_specs + cast_specs,
        out_shape=out_shape,
        scratch_shapes=[pltpu.VMEM((tm, D_MODEL), BF16), pltpu.VMEM((tm, FF_CHUNK), BF16),
                        pltpu.VMEM((tm, D_FF - FF_CHUNK), BF16)],
        compiler_params=pltpu.CompilerParams(dimension_semantics=("arbitrary",), vmem_limit_bytes=VMEM_LIMIT_BYTES),
        name="ffn_final" if gf is not None else "ffn_half",
    )(*args, *cast)
    return outs[0], outs[1], list(outs[2:])


def _first_half(shape):
    return (lax.broadcasted_iota(jnp.int32, shape, 1) & (HEAD_DIM - 1)) < HEAD_DIM // 2


def _sign_sin(sin):
    return jnp.where(_first_half(sin.shape), -sin, sin)


def _rope(xg, cos, sin_signed):
    swapped = jnp.where(_first_half(xg.shape), pltpu.roll(xg, LANES - HEAD_DIM // 2, 1),
                        pltpu.roll(xg, HEAD_DIM // 2, 1))
    return xg * cos + swapped * sin_signed


def _in_proj(x, gmix, w_in, cos, sin_signed, gvn_gain):
    h = _rms(x, gmix).astype(BF16)
    z = jnp.dot(h, w_in, preferred_element_type=F32)
    q = jnp.concatenate(
        [(_rope(z[:, LANES * i:LANES * (i + 1)], cos, sin_signed) * Q_SCALE).astype(BF16) for i in range(Q_W // LANES)],
        axis=1)
    k = _rope(z[:, K_OFF:V_OFF], cos, sin_signed)
    v = z[:, V_OFF:U_OFF]
    u = jax.nn.gelu(z[:, U_OFF:GV_OFF])
    gvn = _rms(jax.nn.gelu(z[:, GV_OFF:]), gvn_gain)
    return q, k, v, u, gvn


def _out_proj(x, ya, yg, ga, gg, w_out):
    cat = jnp.concatenate([_rms(ya, ga).astype(BF16), _rms(yg, gg).astype(BF16)], axis=1)
    return x + jnp.dot(cat, w_out, preferred_element_type=F32)


def _softmax_sink(s, sink):
    m = jnp.maximum(jnp.max(s, axis=1, keepdims=True), sink)
    p = jnp.exp(s - m)
    return p, jnp.sum(p, axis=1, keepdims=True) + jnp.exp(sink - m)


def _mix_prompt_kernel(xa_ref, xc_ref, coff_ref, soff_ref, cbase_ref, sbase_ref, gmix_ref, win_ref, sinks_ref, gvg_ref,
                       wcat_ref, bias_ref, ga_ref, gg_ref, wout_ref, o_ref, ko_ref, vo_ref, gvo_ref, q_s, k_s, v_s, u_s,
                       gv_s, cat_s, *, tm, tiles_per_seq, n_tiles):
    s = pl.program_id(0)

    @pl.when(s == 0)
    def _():
        q_s[1] = jnp.zeros((tm, Q_W), BF16)
        k_s[1] = jnp.zeros((tm + WINDOW, KV_W), BF16)
        v_s[1] = jnp.zeros((tm + WINDOW, KV_W), BF16)
        u_s[1] = jnp.zeros((tm, D_GMLP), F32)
        gv_s[1] = jnp.zeros((tm, D_GMLP), BF16)
        cat_s[0] = jnp.zeros((tm, D_MODEL), BF16)

    for parity in range(2):
        pl.when(s % 2 == parity)(functools.partial(
            _mix_prompt_step, s, parity, xa_ref, xc_ref, coff_ref, soff_ref, cbase_ref, sbase_ref, gmix_ref, win_ref,
            sinks_ref, gvg_ref, wcat_ref, bias_ref, ga_ref, gg_ref, wout_ref, o_ref, ko_ref, vo_ref, gvo_ref, q_s, k_s,
            v_s, u_s, gv_s, cat_s, tm=tm, tiles_per_seq=tiles_per_seq, n_tiles=n_tiles))


def _mix_prompt_step(s, cur, xa_ref, xc_ref, coff_ref, soff_ref, cbase_ref, sbase_ref, gmix_ref, win_ref, sinks_ref,
                     gvg_ref, wcat_ref, bias_ref, ga_ref, gg_ref, wout_ref, o_ref, ko_ref, vo_ref, gvo_ref, q_s, k_s, v_s,
                     u_s, gv_s, cat_s, *, tm, tiles_per_seq, n_tiles):
    oth = 1 - cur

    h = _rms(xa_ref[0], gmix_ref[...]).astype(BF16)
    tile_in_seq = jnp.minimum(s, n_tiles - 1) % tiles_per_seq
    cb = cbase_ref[pl.ds(tile_in_seq, 1), :]
    sb = sbase_ref[pl.ds(tile_in_seq, 1), :]
    cos = cb * coff_ref[...] - sb * soff_ref[...]
    sin = _sign_sin(sb * coff_ref[...] + cb * soff_ref[...])

    def proj_q():
        z = jnp.dot(h, win_ref[:, 0:Q_W], preferred_element_type=F32)
        for i in range(Q_W // LANES):
            q_s[cur, :, LANES * i:LANES * (i + 1)] = (
                _rope(z[:, LANES * i:LANES * (i + 1)], cos, sin) * Q_SCALE).astype(BF16)

    def proj_kv():
        z = jnp.dot(h, win_ref[:, K_OFF:U_OFF], preferred_element_type=F32)
        k = _rope(z[:, 0:KV_W], cos, sin)
        v = z[:, KV_W:]
        k_s[cur, 0:WINDOW, :] = k_s[oth, tm:tm + WINDOW, :]
        v_s[cur, 0:WINDOW, :] = v_s[oth, tm:tm + WINDOW, :]
        k_s[cur, WINDOW:, :] = k.astype(BF16)
        v_s[cur, WINDOW:, :] = v.astype(BF16)
        ko_ref[0] = k[tm - WINDOW:].T
        vo_ref[0] = v[tm - WINDOW:].T

    def proj_u():
        u_s[cur] = jax.nn.gelu(jnp.dot(h, win_ref[:, U_OFF:GV_OFF], preferred_element_type=F32))

    def proj_gv():
        gvn = _rms(jax.nn.gelu(jnp.dot(h, win_ref[:, GV_OFF:], preferred_element_type=F32)), gvg_ref[...])
        gv_s[cur] = gvn.astype(BF16)
        gvo_ref[0] = gvn[tm - CHUNK:].T

    def out_half(c):
        cols = slice(c * (D_MODEL // 2), (c + 1) * (D_MODEL // 2))

        def run():
            o_ref[0, :, cols] = xc_ref[0, :, cols] + jnp.dot(cat_s[cur], wout_ref[:, cols], preferred_element_type=F32)
        return run

    mid_fill = [[out_half(0)], [proj_q], [proj_kv, proj_u], [proj_gv]]
    end_fill = [[], [], [], [out_half(1)]]
    assert len(mid_fill) == len(end_fill) == tm // WINDOW

    wrow = lax.broadcasted_iota(jnp.int32, (CHUNK, 2 * CHUNK), 0)
    wcol = lax.broadcasted_iota(jnp.int32, (CHUNK, 2 * CHUNK), 1) & (CHUNK - 1)
    wmix = [jnp.where(wcol <= wrow, wcat_ref[p], 0.0).astype(BF16) for p in range(G_HEADS // 2)]
    lane = lax.broadcasted_iota(jnp.int32, (CHUNK, LANES), 1)
    low_head = lane < HEAD_DIM

    qi = lax.broadcasted_iota(jnp.int32, (WINDOW, 2 * WINDOW), 0)
    sj = lax.broadcasted_iota(jnp.int32, (WINDOW, 2 * WINDOW), 1)
    dist = WINDOW + qi - sj
    band = (dist >= 0) & (dist < WINDOW)
    first_lo = jnp.where((s + tiles_per_seq - 1) % tiles_per_seq == 0, WINDOW, 0)

    for j in range(tm // WINDOW):
        rows = slice(j * WINDOW, (j + 1) * WINDOW)
        mask = band & (sj >= first_lo) if j == 0 else band
        qb = q_s[oth, rows, :]
        scores = []
        for kv in range(N_KV):
            qs = jnp.concatenate(
                [qb[:, HEAD_DIM * hd:HEAD_DIM * (hd + 1)] for hd in range(kv * GQA, (kv + 1) * GQA)], axis=0)
            kb = k_s[oth, j * WINDOW:(j + 2) * WINDOW, HEAD_DIM * kv:HEAD_DIM * (kv + 1)]
            scores.append(lax.dot_general(qs, kb, _NT, preferred_element_type=F32))
        for run in mid_fill[j]:
            run()
        outs, dens = [], []
        for kv in range(N_KV):
            vb = v_s[oth, j * WINDOW:(j + 2) * WINDOW, HEAD_DIM * kv:HEAD_DIM * (kv + 1)]
            ps = []
            for g in range(GQA):
                sg = jnp.where(mask, scores[kv][g * WINDOW:(g + 1) * WINDOW], -jnp.inf)
                p, den = _softmax_sink(sg, sinks_ref[kv * GQA + g])
                ps.append(p.astype(BF16))
                dens.append(den)
            outs.append(jnp.dot(jnp.concatenate(ps, axis=0), vb, preferred_element_type=F32))
        mixed = []
        for p in range(G_HEADS // 2):
            r = gv_s[oth, rows, LANES * p:LANES * (p + 1)]
            zero = jnp.zeros_like(r)
            rhs = jnp.concatenate([jnp.where(low_head, r, zero), jnp.where(low_head, zero, r)], axis=0)
            mixed.append(jnp.dot(wmix[p], rhs, preferred_element_type=F32))
        for run in end_fill[j]:
            run()
        ya = [outs[hd // GQA][(hd % GQA) * WINDOW:(hd % GQA + 1) * WINDOW] / dens[hd] for hd in range(N_HEADS)]
        cat_s[oth, rows, 0:Q_W] = _rms(jnp.concatenate(ya, axis=1), ga_ref[...]).astype(BF16)
        yg = u_s[oth, rows, :] * (jnp.concatenate(mixed, axis=1) + bias_ref[...])
        cat_s[oth, rows, Q_W:] = _rms(yg, gg_ref[...]).astype(BF16)


def _mix_prompt(x, gmix, w_in, sinks, gvg, wcat, bias_full, ga, gg, w_out):
    b, s, _ = x.shape
    tm = TOKEN_TILE
    assert s % tm == 0 and tm % WINDOW == 0
    tiles_per_seq = s // tm
    n_tiles = b * tiles_per_seq
    cos_off, sin_off = _rope_tables(jnp.arange(tm, dtype=jnp.int32))
    cos_base, sin_base = _rope_tables(tm * jnp.arange(tiles_per_seq, dtype=jnp.int32))
    proj_tile = lambda i: jnp.minimum(i, n_tiles - 1)
    out_tile = lambda i: jnp.maximum(i - 2, 0)
    x_tiles = x.reshape(n_tiles, tm, D_MODEL)
    last = lambda width: pl.BlockSpec((1, width, WINDOW), lambda i: (proj_tile(i) // tiles_per_seq, 0, 0))
    out, ko, vo, gvo = pl.pallas_call(
        functools.partial(_mix_prompt_kernel, tm=tm, tiles_per_seq=tiles_per_seq, n_tiles=n_tiles),
        grid=(n_tiles + 2,),
        in_specs=[pl.BlockSpec((1, tm, D_MODEL), lambda i: (proj_tile(i), 0, 0)),
                  pl.BlockSpec((1, tm, D_MODEL), lambda i: (out_tile(i), 0, 0)),
                  _resident((tm, LANES)), _resident((tm, LANES)), _resident((tiles_per_seq, LANES)),
                  _resident((tiles_per_seq, LANES)), _resident((1, D_MODEL)), _resident((D_MODEL, D_IN)),
                  pl.BlockSpec(memory_space=pltpu.SMEM), _resident((1, D_GMLP)),
                  _resident((G_HEADS // 2, CHUNK, 2 * CHUNK)), _resident((CHUNK, D_GMLP)), _resident((1, Q_W)),
                  _resident((1, D_GMLP)), _resident((D_MODEL, D_MODEL))],
        out_specs=[pl.BlockSpec((1, tm, D_MODEL), lambda i: (out_tile(i), 0, 0)), last(KV_W), last(KV_W),
                   last(D_GMLP)],
        out_shape=[jax.ShapeDtypeStruct((n_tiles, tm, D_MODEL), F32), jax.ShapeDtypeStruct((b, KV_W, WINDOW), F32),
                   jax.ShapeDtypeStruct((b, KV_W, WINDOW), F32), jax.ShapeDtypeStruct((b, D_GMLP, CHUNK), F32)],
        scratch_shapes=[pltpu.VMEM((2, tm, Q_W), BF16), pltpu.VMEM((2, tm + WINDOW, KV_W), BF16),
                        pltpu.VMEM((2, tm + WINDOW, KV_W), BF16), pltpu.VMEM((2, tm, D_GMLP), F32),
                        pltpu.VMEM((2, tm, D_GMLP), BF16), pltpu.VMEM((2, tm, D_MODEL), BF16)],
        compiler_params=pltpu.CompilerParams(dimension_semantics=("arbitrary",), vmem_limit_bytes=VMEM_LIMIT_BYTES),
        name="mix_prompt",
    )(x_tiles, x_tiles, cos_off, sin_off, cos_base, sin_base, gmix, w_in, sinks, gvg, wcat, bias_full, ga, gg, w_out)
    return out.reshape(b, s, D_MODEL), ko, vo, gvo


def _mix_sample_kernel(x_ref, ckt_ref, cvt_ref, cos_ref, sin_ref, gmix_ref, win_ref, sinks_ref, gvg_ref, coef_ref,
                       biasr_ref, ga_ref, gg_ref, wout_ref, o_ref, kot_ref, vot_ref, gvo_ref, q_s, k_s, v_s, ya_s,
                       yg_s, *, t_new, w_buf):
    step = pl.program_id(0)
    n_tok = x_ref.shape[0]
    step_seqs = SEQ_GROUP * GROUPS_PER_STEP
    step_rows = step_seqs * t_new
    grp_rows = SEQ_GROUP * t_new
    grp_keys = SEQ_GROUP * w_buf
    sub = 8

    @pl.when(step == 0)
    def _():
        tile = lambda tab: jnp.broadcast_to(tab[None], (n_tok // sub, sub, LANES)).reshape(n_tok, LANES)
        q, k, v, u, gvn = _in_proj(x_ref[...], gmix_ref[...], win_ref[...], tile(cos_ref[...]),
                                   tile(_sign_sin(sin_ref[...])), gvg_ref[...])
        q_s[...] = q
        k_s[...] = k
        v_s[...] = v
        gvo_ref[...] = gvn
        g3 = gvn.reshape(n_tok // sub, sub, D_GMLP)
        trow = lax.broadcasted_iota(jnp.int32, (1, sub, D_GMLP), 1) & (t_new - 1)
        mixed = biasr_ref[...][None] + coef_ref[0][None] * g3
        for d in range(1, t_new):
            shifted = jnp.where(trow >= d, pltpu.roll(g3, d, 1), 0.0)
            mixed = mixed + coef_ref[d][None] * shifted
        yg_s[...] = (u.reshape(n_tok // sub, sub, D_GMLP) * mixed).reshape(n_tok, D_GMLP)

    row0 = pl.multiple_of(step * step_rows, step_rows)
    q_step = q_s[pl.ds(row0, step_rows), :]
    kn = k_s[pl.ds(row0, step_rows), :]
    vn = v_s[pl.ds(row0, step_rows), :]

    knt = kn.T
    vnt = vn.T
    tail = lax.broadcasted_iota(jnp.int32, (KV_W, w_buf), 1) >= w_buf - t_new
    for b in range(step_seqs):
        shift = (w_buf - t_new - t_new * b) % w_buf
        kot_ref[b] = jnp.where(tail, pltpu.roll(knt, shift, 1) if shift else knt,
                               pltpu.roll(ckt_ref[b], w_buf - t_new, 1))
        vot_ref[b] = jnp.where(tail, pltpu.roll(vnt, shift, 1) if shift else vnt,
                               pltpu.roll(cvt_ref[b], w_buf - t_new, 1))

    knb = kn.astype(BF16)
    vnb = vn.astype(BF16)

    n_rows = GQA * grp_rows
    shift_t = t_new.bit_length() - 1
    shift_w = w_buf.bit_length() - 1
    r = lax.broadcasted_iota(jnp.int32, (n_rows, grp_keys), 0)
    c = lax.broadcasted_iota(jnp.int32, (n_rows, grp_keys), 1)
    mask_c = ((c >> shift_w) == ((r & (grp_rows - 1)) >> shift_t)) & ((c & (w_buf - 1)) > (r & (t_new - 1)))
    r2 = lax.broadcasted_iota(jnp.int32, (n_rows, grp_rows), 0)
    c2 = lax.broadcasted_iota(jnp.int32, (n_rows, grp_rows), 1)
    mask_n = ((c2 >> shift_t) == ((r2 & (grp_rows - 1)) >> shift_t)) & ((c2 & (t_new - 1)) <= (r2 & (t_new - 1)))
    row_head = lax.broadcasted_iota(jnp.int32, (n_rows, 1), 0) >> (grp_rows.bit_length() - 1)

    for grp in range(GROUPS_PER_STEP):
        rows = slice(grp * grp_rows, (grp + 1) * grp_rows)
        seqs = range(grp * SEQ_GROUP, (grp + 1) * SEQ_GROUP)
        for kv in range(N_KV):
            heads = [kv * GQA + i for i in range(GQA)]
            lanes = slice(HEAD_DIM * kv, HEAD_DIM * (kv + 1))
            kt = jnp.concatenate([ckt_ref[b, lanes, :] for b in seqs], axis=1).astype(BF16)
            vt = jnp.concatenate([cvt_ref[b, lanes, :] for b in seqs], axis=1).astype(BF16)
            qs = jnp.concatenate([q_step[rows, HEAD_DIM * hd:HEAD_DIM * (hd + 1)] for hd in heads], axis=0)
            s_c = jnp.where(mask_c, jnp.dot(qs, kt, preferred_element_type=F32), -jnp.inf)
            s_n = jnp.where(mask_n, lax.dot_general(qs, knb[rows, lanes], _NT, preferred_element_type=F32), -jnp.inf)
            sink = jnp.full((n_rows, 1), sinks_ref[heads[0]], F32)
            for i in range(1, GQA):
                sink = jnp.where(row_head == i, sinks_ref[heads[i]], sink)
            m = jnp.maximum(jnp.maximum(jnp.max(s_c, axis=1, keepdims=True), jnp.max(s_n, axis=1, keepdims=True)),
                            sink)
            p_c = jnp.exp(s_c - m)
            p_n = jnp.exp(s_n - m)
            den = jnp.sum(p_c, axis=1, keepdims=True) + jnp.sum(p_n, axis=1, keepdims=True) + jnp.exp(sink - m)
            o = (lax.dot_general(p_c.astype(BF16), vt, _NT, preferred_element_type=F32)
                 + jnp.dot(p_n.astype(BF16), vnb[rows, lanes], preferred_element_type=F32)) / den
            for i, hd in enumerate(heads):
                ya_s[pl.ds(row0 + grp * grp_rows, grp_rows), HEAD_DIM * hd:HEAD_DIM * (hd + 1)] = (
                    o[i * grp_rows:(i + 1) * grp_rows])

    @pl.when(step == pl.num_programs(0) - 1)
    def _():
        o_ref[...] = _out_proj(x_ref[...], ya_s[...], yg_s[...], ga_ref[...], gg_ref[...], wout_ref[...])


def _mix_sample(x, cache_kt, cache_vt, cos, sin, gmix, w_in, sinks, gvg, coef, bias_rows, ga, gg, w_out, t_new):
    n_tok = x.shape[0]
    n_seq, _, w_buf = cache_kt.shape
    step_seqs = SEQ_GROUP * GROUPS_PER_STEP
    assert n_seq % step_seqs == 0 and n_tok == n_seq * t_new
    assert t_new & (t_new - 1) == 0 and w_buf & (w_buf - 1) == 0 and 8 % t_new == 0
    assert step_seqs * t_new == w_buf == LANES
    cache_spec = pl.BlockSpec((step_seqs, KV_W, w_buf), lambda i: (i, 0, 0))
    return pl.pallas_call(
        functools.partial(_mix_sample_kernel, t_new=t_new, w_buf=w_buf),
        grid=(n_seq // step_seqs,),
        in_specs=[_resident((n_tok, D_MODEL)), cache_spec, cache_spec, _resident((8, LANES)),
                  _resident((8, LANES)), _resident((1, D_MODEL)), _resident((D_MODEL, D_IN)),
                  pl.BlockSpec(memory_space=pltpu.SMEM), _resident((1, D_GMLP)), _resident((t_new, 8, D_GMLP)),
                  _resident((8, D_GMLP)), _resident((1, Q_W)), _resident((1, D_GMLP)),
                  _resident((D_MODEL, D_MODEL))],
        out_specs=[pl.BlockSpec((n_tok, D_MODEL), lambda i: (0, 0)), cache_spec, cache_spec,
                   pl.BlockSpec((n_tok, D_GMLP), lambda i: (0, 0))],
        out_shape=[jax.ShapeDtypeStruct((n_tok, D_MODEL), F32), jax.ShapeDtypeStruct(cache_kt.shape, F32),
                   jax.ShapeDtypeStruct(cache_vt.shape, F32), jax.ShapeDtypeStruct((n_tok, D_GMLP), F32)],
        scratch_shapes=[pltpu.VMEM((n_tok, Q_W), BF16), pltpu.VMEM((n_tok, KV_W), F32), pltpu.VMEM((n_tok, KV_W), F32),
                        pltpu.VMEM((n_tok, Q_W), F32), pltpu.VMEM((n_tok, D_GMLP), F32)],
        compiler_params=pltpu.CompilerParams(dimension_semantics=("arbitrary",), vmem_limit_bytes=VMEM_LIMIT_BYTES),
        name="mix_sample",
    )(x, cache_kt, cache_vt, cos, sin, gmix, w_in, sinks, gvg, coef, bias_rows, ga, gg, w_out)


def _rope_tables(pos):
    inv_freq = ROPE_THETA ** (-jnp.arange(0, HEAD_DIM, 2, dtype=F32) / HEAD_DIM)
    ang = pos.astype(F32)[:, None] * inv_freq[None, :]
    return jnp.tile(jnp.cos(ang), (1, 4)), jnp.tile(jnp.sin(ang), (1, 4))


def kernel(x_prompt, x_sample, cache_k_win, cache_v_win, norm_ffn1, ffn1_gate, ffn1_up, ffn1_down, norm_mix, w_in,
           attn_sinks, gmlp_v_norm, gmlp_w_s, gmlp_b_s, norm_attn_out, norm_gmlp_out, w_out, norm_ffn2, ffn2_gate,
           ffn2_up, ffn2_down, norm_final):
    depth = norm_ffn1.shape[0]
    b, s, _ = x_prompt.shape
    bd, t_new, _ = x_sample.shape
    w_buf = cache_k_win.shape[2]

    cos_s, sin_s = _rope_tables(PAST_LEN + jnp.arange(t_new, dtype=jnp.int32))
    cos_s, sin_s = jnp.tile(cos_s, (8 // t_new, 1)), jnp.tile(sin_s, (8 // t_new, 1))

    hp = x_prompt.reshape(b * s, D_MODEL)
    hs = x_sample.reshape(bd * t_new, D_MODEL)
    outs = [[] for _ in range(6)]
    ffn1_w = [w[0].astype(BF16) for w in (ffn1_gate, ffn1_up, ffn1_down)]
    for l in range(depth):
        last = l == depth - 1
        row = lambda a: a[l].reshape(1, -1)

        wcat = gmlp_w_s[l].reshape(G_HEADS // 2, 2, CHUNK, CHUNK).transpose(0, 2, 1, 3).reshape(
            G_HEADS // 2, CHUNK, 2 * CHUNK)
        bias_full = jnp.repeat(gmlp_b_s[l].T, HEAD_DIM, axis=1)
        ws_small = gmlp_w_s[l][:, :t_new, :t_new]
        coef = jnp.stack([
            jnp.stack([ws_small[:, t, t - d] if t >= d else jnp.zeros((G_HEADS,), F32) for t in range(t_new)])
            for d in range(t_new)])
        coef = jnp.tile(jnp.repeat(coef, HEAD_DIM, axis=2), (1, 8 // t_new, 1))
        bias_rows = jnp.tile(jnp.repeat(gmlp_b_s[l][:, :t_new].T, HEAD_DIM, axis=1), (8 // t_new, 1))

        hp, hs, (w_in_b, w_out_b, wg2, wu2, wd2) = _ffn_half(
            hp, hs, norm_ffn1[l], *ffn1_w, cast=(w_in[l], w_out[l], ffn2_gate[l], ffn2_up[l], ffn2_down[l]))

        hp, kpt, vpt, gvpt = _mix_prompt(hp.reshape(b, s, D_MODEL), row(norm_mix), w_in_b, attn_sinks[l],
                                         row(gmlp_v_norm), wcat, bias_full, row(norm_attn_out), row(norm_gmlp_out),
                                         w_out_b)
        to_t = lambda c: c.transpose(0, 2, 3, 1).reshape(bd, KV_W, w_buf)
        hs, kst, vst, gvs = _mix_sample(hs, to_t(cache_k_win[l]), to_t(cache_v_win[l]), cos_s, sin_s, row(norm_mix),
                                        w_in_b, attn_sinks[l], row(gmlp_v_norm), coef, bias_rows, row(norm_attn_out),
                                        row(norm_gmlp_out), w_out_b, t_new)
        from_t = lambda c: c.reshape(bd, N_KV, HEAD_DIM, w_buf).transpose(0, 3, 1, 2)

        next_ffn1 = () if last else (ffn1_gate[l + 1], ffn1_up[l + 1], ffn1_down[l + 1])
        hp, hs, ffn1_w = _ffn_half(hp.reshape(b * s, D_MODEL), hs, norm_ffn2[l], wg2, wu2, wd2,
                                   gf=norm_final if last else None, cast=next_ffn1)

        outs[0].append(kpt.reshape(b, N_KV, HEAD_DIM, WINDOW).transpose(0, 3, 1, 2))
        outs[1].append(vpt.reshape(b, N_KV, HEAD_DIM, WINDOW).transpose(0, 3, 1, 2))
        outs[2].append(from_t(kst))
        outs[3].append(from_t(vst))
        outs[4].append(gvpt.reshape(b, G_HEADS, D_GMLP // G_HEADS, CHUNK).transpose(0, 3, 1, 2))
        outs[5].append(gvs.reshape(bd, t_new, G_HEADS, D_GMLP // G_HEADS))

    return (hp.reshape(b, s, D_MODEL), hs.reshape(bd, t_new, D_MODEL)) + tuple(jnp.stack(o) for o in outs)
```

```python
import functools

import jax
import jax.numpy as jnp
from jax import lax
from jax.experimental import pallas as pl
from jax.experimental.pallas import tpu as pltpu

F32 = jnp.float32
BF16 = jnp.bfloat16

D_MODEL = 1024
D_FF = 2816
HEAD_DIM = 64
N_HEADS = 8
N_KV = 2
GQA = N_HEADS // N_KV
WINDOW = 128
CHUNK = 128
G_HEADS = 8
Q_W = N_HEADS * HEAD_DIM
KV_W = N_KV * HEAD_DIM
D_GMLP = 512
D_IN = Q_W + 2 * KV_W + 2 * D_GMLP
K_OFF = Q_W
V_OFF = K_OFF + KV_W
U_OFF = V_OFF + KV_W
GV_OFF = U_OFF + D_GMLP
ROPE_THETA = 10000.0
PAST_LEN = 16384
EPS = 1e-6
Q_SCALE = HEAD_DIM ** -0.5
LANES = 128
BF16_SUBLANES = 16

TOKEN_TILE = 512
FF_CHUNK = 256
SEQ_GROUP = 8
GROUPS_PER_STEP = 4
VMEM_LIMIT_BYTES = 56 * 1024 * 1024

_NT = (((1,), (1,)), ((), ()))


def _rms(x, g):
    ms = jnp.mean(x * x, axis=-1, keepdims=True)
    return (x * lax.rsqrt(ms + EPS)) * g


def _resident(shape):
    zeros = (0,) * len(shape)
    return pl.BlockSpec(shape, lambda *_: zeros, pipeline_mode=pl.Buffered(1))


def _ffn_kernel(*refs, final_norm, n_cast):
    refs = iter(refs)
    xp_ref, xn_ref, xs_ref, g_ref, wg_ref, wu_ref, wd_ref = (next(refs) for _ in range(7))
    gf_ref = next(refs) if final_norm else None
    cast_in = [next(refs) for _ in range(n_cast)]
    yp_ref, ys_ref = next(refs), next(refs)
    cast_out = [next(refs) for _ in range(n_cast)]
    h_ref, act0_ref, act_ref = next(refs), next(refs), next(refs)

    def gate_up(h, c):
        sl = slice(c * FF_CHUNK, (c + 1) * FF_CHUNK)
        gate = jnp.dot(h, wg_ref[:, sl], preferred_element_type=F32)
        up = jnp.dot(h, wu_ref[:, sl], preferred_element_type=F32)
        return (gate * jax.nn.sigmoid(gate) * up).astype(BF16)

    on_sample = pl.program_id(0) == 0

    @pl.when(on_sample)
    def _():
        h0 = _rms(xs_ref[...], g_ref[...]).astype(BF16)
        h_ref[...] = h0
        act0_ref[...] = gate_up(h0, 0)

    for c in range(1, D_FF // FF_CHUNK):
        act_ref[:, (c - 1) * FF_CHUNK:c * FF_CHUNK] = gate_up(h_ref[...], c)
    down0 = jnp.dot(act0_ref[...], wd_ref[0:FF_CHUNK, :], preferred_element_type=F32)
    hn = _rms(xn_ref[...], g_ref[...]).astype(BF16)
    h_ref[...] = hn
    act0_ref[...] = gate_up(hn, 0)
    x = jnp.where(on_sample, xs_ref[...], xp_ref[...])
    y = x + 0.5 * (down0 + jnp.dot(act_ref[...], wd_ref[FF_CHUNK:, :], preferred_element_type=F32))
    if final_norm:
        y = _rms(y, gf_ref[...])
    yp_ref[...] = y

    @pl.when(on_sample)
    def _():
        ys_ref[...] = yp_ref[...]

    for src, dst in zip(cast_in, cast_out):
        dst[...] = src[...].astype(BF16)


def _cast_row_blocks(rows, n_steps):
    return max(d for d in range(1, n_steps + 1) if rows % d == 0 and (rows // d) % BF16_SUBLANES == 0)


def _ffn_half(xp, xs, g, wg, wu, wd, gf=None, cast=()):
    tm = TOKEN_TILE
    n = xp.shape[0]
    assert n % tm == 0 and xs.shape[0] == tm
    n_tiles = n // tm
    prompt_spec = pl.BlockSpec((tm, D_MODEL), lambda i: (jnp.maximum(i - 1, 0), 0))
    next_spec = pl.BlockSpec((tm, D_MODEL), lambda i: (jnp.minimum(i, n_tiles - 1), 0))
    sample_in = _resident((tm, D_MODEL))
    sample_out = pl.BlockSpec((tm, D_MODEL), lambda i: (0, 0))
    in_specs = [prompt_spec, next_spec, sample_in, _resident((1, D_MODEL)), _resident((D_MODEL, D_FF)),
                _resident((D_MODEL, D_FF)), _resident((D_FF, D_MODEL))]
    args = [xp, xp, xs, g.reshape(1, D_MODEL), wg, wu, wd]
    if gf is not None:
        in_specs.append(_resident((1, D_MODEL)))
        args.append(gf.reshape(1, D_MODEL))
    out_specs = [prompt_spec, sample_out]
    out_shape = [jax.ShapeDtypeStruct((n, D_MODEL), F32), jax.ShapeDtypeStruct((tm, D_MODEL), F32)]
    cast_specs = []
    for w in cast:
        rows, cols = w.shape
        nb = _cast_row_blocks(rows, n_tiles)
        cast_specs.append(pl.BlockSpec((rows // nb, cols), lambda i, nb=nb: (jnp.minimum(i, nb - 1), 0)))
        out_shape.append(jax.ShapeDtypeStruct(w.shape, BF16))
    outs = pl.pallas_call(
        functools.partial(_ffn_kernel, final_norm=gf is not None, n_cast=len(cast)),
        grid=(n_tiles + 1,),
        in_specs=in_specs + cast_specs,
        out_specs=out_specs + cast_specs,
        out_shape=out_shape,
        scratch_shapes=[pltpu.VMEM((tm, D_MODEL), BF16), pltpu.VMEM((tm, FF_CHUNK), BF16),
                        pltpu.VMEM((tm, D_FF - FF_CHUNK), BF16)],
        compiler_params=pltpu.CompilerParams(dimension_semantics=("arbitrary",), vmem_limit_bytes=VMEM_LIMIT_BYTES),
        name="ffn_final" if gf is not None else "ffn_half",
    )(*args, *cast)
    return outs[0], outs[1], list(outs[2:])


def _first_half(shape):
    return (lax.broadcasted_iota(jnp.int32, shape, 1) & (HEAD_DIM - 1)) < HEAD_DIM // 2


def _sign_sin(sin):
    return jnp.where(_first_half(sin.shape), -sin, sin)


def _rope(xg, cos, sin_signed):
    swapped = jnp.where(_first_half(xg.shape), pltpu.roll(xg, LANES - HEAD_DIM // 2, 1),
                        pltpu.roll(xg, HEAD_DIM // 2, 1))
    return xg * cos + swapped * sin_signed


def _in_proj(x, gmix, w_in, cos, sin_signed, gvn_gain):
    h = _rms(x, gmix).astype(BF16)
    z = jnp.dot(h, w_in, preferred_element_type=F32)
    q = jnp.concatenate(
        [(_rope(z[:, LANES * i:LANES * (i + 1)], cos, sin_signed) * Q_SCALE).astype(BF16) for i in range(Q_W // LANES)],
        axis=1)
    k = _rope(z[:, K_OFF:V_OFF], cos, sin_signed)
    v = z[:, V_OFF:U_OFF]
    u = jax.nn.gelu(z[:, U_OFF:GV_OFF])
    gvn = _rms(jax.nn.gelu(z[:, GV_OFF:]), gvn_gain)
    return q, k, v, u, gvn


def _out_proj(x, ya, yg, ga, gg, w_out):
    cat = jnp.concatenate([_rms(ya, ga).astype(BF16), _rms(yg, gg).astype(BF16)], axis=1)
    return x + jnp.dot(cat, w_out, preferred_element_type=F32)


def _softmax_sink(s, sink):
    m = jnp.maximum(jnp.max(s, axis=1, keepdims=True), sink)
    p = jnp.exp(s - m)
    return p, jnp.sum(p, axis=1, keepdims=True) + jnp.exp(sink - m)


def _mix_prompt_kernel(xa_ref, xc_ref, coff_ref, soff_ref, cbase_ref, sbase_ref, gmix_ref, win_ref, sinks_ref, gvg_ref,
                       wcat_ref, bias_ref, ga_ref, gg_ref, wout_ref, o_ref, ko_ref, vo_ref, gvo_ref, q_s, k_s, v_s, u_s,
                       gv_s, cat_s, *, tm, tiles_per_seq, n_tiles):
    s = pl.program_id(0)
    step = functools.partial(
        _mix_prompt_step, s, xa_ref, xc_ref, coff_ref, soff_ref, cbase_ref, sbase_ref, gmix_ref, win_ref, sinks_ref,
        gvg_ref, wcat_ref, bias_ref, ga_ref, gg_ref, wout_ref, o_ref, ko_ref, vo_ref, gvo_ref, q_s, k_s, v_s, u_s, gv_s,
        cat_s, tm=tm, tiles_per_seq=tiles_per_seq)

    @pl.when(s == 0)
    def _():
        k_s[1, tm:tm + WINDOW, :] = jnp.zeros((WINDOW, KV_W), BF16)
        v_s[1, tm:tm + WINDOW, :] = jnp.zeros((WINDOW, KV_W), BF16)

    assert n_tiles >= 2
    special = {0: (True, False, False), 1: (True, True, False),
               n_tiles: (False, True, True), n_tiles + 1: (False, False, True)}
    for at, (proj, mix, out) in special.items():
        pl.when(s == at)(functools.partial(step, cur=at % 2, do_proj=proj, do_mix=mix, do_out=out))
    for parity in range(2):
        pl.when((s >= 2) & (s < n_tiles) & (s % 2 == parity))(
            functools.partial(step, cur=parity, do_proj=True, do_mix=True, do_out=True))


def _mix_prompt_step(s, xa_ref, xc_ref, coff_ref, soff_ref, cbase_ref, sbase_ref, gmix_ref, win_ref, sinks_ref, gvg_ref,
                     wcat_ref, bias_ref, ga_ref, gg_ref, wout_ref, o_ref, ko_ref, vo_ref, gvo_ref, q_s, k_s, v_s, u_s, gv_s,
                     cat_s, *, tm, tiles_per_seq, cur, do_proj, do_mix, do_out):
    oth = 1 - cur

    if do_proj:
        h = _rms(xa_ref[0], gmix_ref[...]).astype(BF16)
        tile_in_seq = s % tiles_per_seq
        cb = cbase_ref[pl.ds(tile_in_seq, 1), :]
        sb = sbase_ref[pl.ds(tile_in_seq, 1), :]
        cos = cb * coff_ref[...] - sb * soff_ref[...]
        sin = _sign_sin(sb * coff_ref[...] + cb * soff_ref[...])

    def proj_q():
        z = jnp.dot(h, win_ref[:, 0:Q_W], preferred_element_type=F32)
        for i in range(Q_W // LANES):
            q_s[cur, :, LANES * i:LANES * (i + 1)] = (
                _rope(z[:, LANES * i:LANES * (i + 1)], cos, sin) * Q_SCALE).astype(BF16)

    def proj_kv():
        z = jnp.dot(h, win_ref[:, K_OFF:U_OFF], preferred_element_type=F32)
        k = _rope(z[:, 0:KV_W], cos, sin)
        v = z[:, KV_W:]
        k_s[cur, 0:WINDOW, :] = k_s[oth, tm:tm + WINDOW, :]
        v_s[cur, 0:WINDOW, :] = v_s[oth, tm:tm + WINDOW, :]
        k_s[cur, WINDOW:, :] = k.astype(BF16)
        v_s[cur, WINDOW:, :] = v.astype(BF16)
        ko_ref[0] = k[tm - WINDOW:].T
        vo_ref[0] = v[tm - WINDOW:].T

    def proj_u():
        u_s[cur] = jax.nn.gelu(jnp.dot(h, win_ref[:, U_OFF:GV_OFF], preferred_element_type=F32))

    def proj_gv():
        gvn = _rms(jax.nn.gelu(jnp.dot(h, win_ref[:, GV_OFF:], preferred_element_type=F32)), gvg_ref[...])
        gv_s[cur] = gvn.astype(BF16)
        gvo_ref[0] = gvn[tm - CHUNK:].T

    def out_half(c):
        cols = slice(c * (D_MODEL // 2), (c + 1) * (D_MODEL // 2))

        def run():
            o_ref[0, :, cols] = xc_ref[0, :, cols] + jnp.dot(cat_s[cur], wout_ref[:, cols], preferred_element_type=F32)
        return run

    mid_fill = [[out_half(0)] * do_out, [proj_q] * do_proj, [proj_kv, proj_u] * do_proj, [proj_gv] * do_proj]
    end_fill = [[], [], [], [out_half(1)] * do_out]
    assert len(mid_fill) == len(end_fill) == tm // WINDOW
    if not do_mix:
        for run in sum(mid_fill + end_fill, []):
            run()
        return

    wrow = lax.broadcasted_iota(jnp.int32, (CHUNK, 2 * CHUNK), 0)
    wcol = lax.broadcasted_iota(jnp.int32, (CHUNK, 2 * CHUNK), 1) & (CHUNK - 1)
    wmix = [jnp.where(wcol <= wrow, wcat_ref[p], 0.0).astype(BF16) for p in range(G_HEADS // 2)]
    lane = lax.broadcasted_iota(jnp.int32, (CHUNK, LANES), 1)
    low_head = lane < HEAD_DIM

    qi = lax.broadcasted_iota(jnp.int32, (WINDOW, 2 * WINDOW), 0)
    sj = lax.broadcasted_iota(jnp.int32, (WINDOW, 2 * WINDOW), 1)
    dist = WINDOW + qi - sj
    band = (dist >= 0) & (dist < WINDOW)
    first_lo = jnp.where((s + tiles_per_seq - 1) % tiles_per_seq == 0, WINDOW, 0)

    for j in range(tm // WINDOW):
        rows = slice(j * WINDOW, (j + 1) * WINDOW)
        mask = band & (sj >= first_lo) if j == 0 else band
        qb = q_s[oth, rows, :]
        scores = []
        for kv in range(N_KV):
            qs = jnp.concatenate(
                [qb[:, HEAD_DIM * hd:HEAD_DIM * (hd + 1)] for hd in range(kv * GQA, (kv + 1) * GQA)], axis=0)
            kb = k_s[oth, j * WINDOW:(j + 2) * WINDOW, HEAD_DIM * kv:HEAD_DIM * (kv + 1)]
            scores.append(lax.dot_general(qs, kb, _NT, preferred_element_type=F32))
        for run in mid_fill[j]:
            run()
        outs, dens = [], []
        for kv in range(N_KV):
            vb = v_s[oth, j * WINDOW:(j + 2) * WINDOW, HEAD_DIM * kv:HEAD_DIM * (kv + 1)]
            ps = []
            for g in range(GQA):
                sg = jnp.where(mask, scores[kv][g * WINDOW:(g + 1) * WINDOW], -jnp.inf)
                p, den = _softmax_sink(sg, sinks_ref[kv * GQA + g])
                ps.append(p.astype(BF16))
                dens.append(den)
            outs.append(jnp.dot(jnp.concatenate(ps, axis=0), vb, preferred_element_type=F32))
        mixed = []
        for p in range(G_HEADS // 2):
            r = gv_s[oth, rows, LANES * p:LANES * (p + 1)]
            zero = jnp.zeros_like(r)
            rhs = jnp.concatenate([jnp.where(low_head, r, zero), jnp.where(low_head, zero, r)], axis=0)
            mixed.append(jnp.dot(wmix[p], rhs, preferred_element_type=F32))
        for run in end_fill[j]:
            run()
        ya = [outs[hd // GQA][(hd % GQA) * WINDOW:(hd % GQA + 1) * WINDOW] / dens[hd] for hd in range(N_HEADS)]
        cat_s[oth, rows, 0:Q_W] = _rms(jnp.concatenate(ya, axis=1), ga_ref[...]).astype(BF16)
        yg = u_s[oth, rows, :] * (jnp.concatenate(mixed, axis=1) + bias_ref[...])
        cat_s[oth, rows, Q_W:] = _rms(yg, gg_ref[...]).astype(BF16)


def _mix_prompt(x, gmix, w_in, sinks, gvg, wcat, bias_full, ga, gg, w_out):
    b, s, _ = x.shape
    tm = TOKEN_TILE
    assert s % tm == 0 and tm % WINDOW == 0
    tiles_per_seq = s // tm
    n_tiles = b * tiles_per_seq
    cos_off, sin_off = _rope_tables(jnp.arange(tm, dtype=jnp.int32))
    cos_base, sin_base = _rope_tables(tm * jnp.arange(tiles_per_seq, dtype=jnp.int32))
    proj_tile = lambda i: jnp.minimum(i, n_tiles - 1)
    out_tile = lambda i: jnp.maximum(i - 2, 0)
    x_tiles = x.reshape(n_tiles, tm, D_MODEL)
    last = lambda width: pl.BlockSpec((1, width, WINDOW), lambda i: (proj_tile(i) // tiles_per_seq, 0, 0))
    out, ko, vo, gvo = pl.pallas_call(
        functools.partial(_mix_prompt_kernel, tm=tm, tiles_per_seq=tiles_per_seq, n_tiles=n_tiles),
        grid=(n_tiles + 2,),
        in_specs=[pl.BlockSpec((1, tm, D_MODEL), lambda i: (proj_tile(i), 0, 0)),
                  pl.BlockSpec((1, tm, D_MODEL), lambda i: (out_tile(i), 0, 0)),
                  _resident((tm, LANES)), _resident((tm, LANES)), _resident((tiles_per_seq, LANES)),
                  _resident((tiles_per_seq, LANES)), _resident((1, D_MODEL)), _resident((D_MODEL, D_IN)),
                  pl.BlockSpec(memory_space=pltpu.SMEM), _resident((1, D_GMLP)),
                  _resident((G_HEADS // 2, CHUNK, 2 * CHUNK)), _resident((CHUNK, D_GMLP)), _resident((1, Q_W)),
                  _resident((1, D_GMLP)), _resident((D_MODEL, D_MODEL))],
        out_specs=[pl.BlockSpec((1, tm, D_MODEL), lambda i: (out_tile(i), 0, 0)), last(KV_W), last(KV_W),
                   last(D_GMLP)],
        out_shape=[jax.ShapeDtypeStruct((n_tiles, tm, D_MODEL), F32), jax.ShapeDtypeStruct((b, KV_W, WINDOW), F32),
                   jax.ShapeDtypeStruct((b, KV_W, WINDOW), F32), jax.ShapeDtypeStruct((b, D_GMLP, CHUNK), F32)],
        scratch_shapes=[pltpu.VMEM((2, tm, Q_W), BF16), pltpu.VMEM((2, tm + WINDOW, KV_W), BF16),
                        pltpu.VMEM((2, tm + WINDOW, KV_W), BF16), pltpu.VMEM((2, tm, D_GMLP), F32),
                        pltpu.VMEM((2, tm, D_GMLP), BF16), pltpu.VMEM((2, tm, D_MODEL), BF16)],
        compiler_params=pltpu.CompilerParams(dimension_semantics=("arbitrary",), vmem_limit_bytes=VMEM_LIMIT_BYTES),
        name="mix_prompt",
    )(x_tiles, x_tiles, cos_off, sin_off, cos_base, sin_base, gmix, w_in, sinks, gvg, wcat, bias_full, ga, gg, w_out)
    return out.reshape(b, s, D_MODEL), ko, vo, gvo


def _mix_sample_kernel(x_ref, ckt_ref, cvt_ref, cos_ref, sin_ref, gmix_ref, win_ref, sinks_ref, gvg_ref, coef_ref,
                       biasr_ref, ga_ref, gg_ref, wout_ref, o_ref, kot_ref, vot_ref, gvo_ref, q_s, k_s, v_s, ya_s,
                       yg_s, *, t_new, w_buf):
    step = pl.program_id(0)
    n_tok = x_ref.shape[0]
    step_seqs = SEQ_GROUP * GROUPS_PER_STEP
    step_rows = step_seqs * t_new
    grp_rows = SEQ_GROUP * t_new
    grp_keys = SEQ_GROUP * w_buf
    sub = 8

    @pl.when(step == 0)
    def _():
        tile = lambda tab: jnp.broadcast_to(tab[None], (n_tok // sub, sub, LANES)).reshape(n_tok, LANES)
        q, k, v, u, gvn = _in_proj(x_ref[...], gmix_ref[...], win_ref[...], tile(cos_ref[...]),
                                   tile(_sign_sin(sin_ref[...])), gvg_ref[...])
        q_s[...] = q
        k_s[...] = k
        v_s[...] = v
        gvo_ref[...] = gvn
        g3 = gvn.reshape(n_tok // sub, sub, D_GMLP)
        trow = lax.broadcasted_iota(jnp.int32, (1, sub, D_GMLP), 1) & (t_new - 1)
        mixed = biasr_ref[...][None] + coef_ref[0][None] * g3
        for d in range(1, t_new):
            shifted = jnp.where(trow >= d, pltpu.roll(g3, d, 1), 0.0)
            mixed = mixed + coef_ref[d][None] * shifted
        yg_s[...] = (u.reshape(n_tok // sub, sub, D_GMLP) * mixed).reshape(n_tok, D_GMLP)

    row0 = pl.multiple_of(step * step_rows, step_rows)
    q_step = q_s[pl.ds(row0, step_rows), :]
    kn = k_s[pl.ds(row0, step_rows), :]
    vn = v_s[pl.ds(row0, step_rows), :]

    knt = kn.T
    vnt = vn.T
    tail = lax.broadcasted_iota(jnp.int32, (KV_W, w_buf), 1) >= w_buf - t_new
    for b in range(step_seqs):
        shift = (w_buf - t_new - t_new * b) % w_buf
        kot_ref[b] = jnp.where(tail, pltpu.roll(knt, shift, 1) if shift else knt,
                               pltpu.roll(ckt_ref[b], w_buf - t_new, 1))
        vot_ref[b] = jnp.where(tail, pltpu.roll(vnt, shift, 1) if shift else vnt,
                               pltpu.roll(cvt_ref[b], w_buf - t_new, 1))

    knb = kn.astype(BF16)
    vnb = vn.astype(BF16)

    n_rows = GQA * grp_rows
    shift_t = t_new.bit_length() - 1
    shift_w = w_buf.bit_length() - 1
    r = lax.broadcasted_iota(jnp.int32, (n_rows, grp_keys), 0)
    c = lax.broadcasted_iota(jnp.int32, (n_rows, grp_keys), 1)
    mask_c = ((c >> shift_w) == ((r & (grp_rows - 1)) >> shift_t)) & ((c & (w_buf - 1)) > (r & (t_new - 1)))
    r2 = lax.broadcasted_iota(jnp.int32, (n_rows, grp_rows), 0)
    c2 = lax.broadcasted_iota(jnp.int32, (n_rows, grp_rows), 1)
    mask_n = ((c2 >> shift_t) == ((r2 & (grp_rows - 1)) >> shift_t)) & ((c2 & (t_new - 1)) <= (r2 & (t_new - 1)))
    row_head = lax.broadcasted_iota(jnp.int32, (n_rows, 1), 0) >> (grp_rows.bit_length() - 1)

    for grp in range(GROUPS_PER_STEP):
        rows = slice(grp * grp_rows, (grp + 1) * grp_rows)
        seqs = range(grp * SEQ_GROUP, (grp + 1) * SEQ_GROUP)
        for kv in range(N_KV):
            heads = [kv * GQA + i for i in range(GQA)]
            lanes = slice(HEAD_DIM * kv, HEAD_DIM * (kv + 1))
            kt = jnp.concatenate([ckt_ref[b, lanes, :] for b in seqs], axis=1).astype(BF16)
            vt = jnp.concatenate([cvt_ref[b, lanes, :] for b in seqs], axis=1).astype(BF16)
            qs = jnp.concatenate([q_step[rows, HEAD_DIM * hd:HEAD_DIM * (hd + 1)] for hd in heads], axis=0)
            s_c = jnp.where(mask_c, jnp.dot(qs, kt, preferred_element_type=F32), -jnp.inf)
            s_n = jnp.where(mask_n, lax.dot_general(qs, knb[rows, lanes], _NT, preferred_element_type=F32), -jnp.inf)
            sink = jnp.full((n_rows, 1), sinks_ref[heads[0]], F32)
            for i in range(1, GQA):
                sink = jnp.where(row_head == i, sinks_ref[heads[i]], sink)
            m = jnp.maximum(jnp.maximum(jnp.max(s_c, axis=1, keepdims=True), jnp.max(s_n, axis=1, keepdims=True)),
                            sink)
            p_c = jnp.exp(s_c - m)
            p_n = jnp.exp(s_n - m)
            den = jnp.sum(p_c, axis=1, keepdims=True) + jnp.sum(p_n, axis=1, keepdims=True) + jnp.exp(sink - m)
            o = (lax.dot_general(p_c.astype(BF16), vt, _NT, preferred_element_type=F32)
                 + jnp.dot(p_n.astype(BF16), vnb[rows, lanes], preferred_element_type=F32)) / den
            for i, hd in enumerate(heads):
                ya_s[pl.ds(row0 + grp * grp_rows, grp_rows), HEAD_DIM * hd:HEAD_DIM * (hd + 1)] = (
                    o[i * grp_rows:(i + 1) * grp_rows])

    @pl.when(step == pl.num_programs(0) - 1)
    def _():
        o_ref[...] = _out_proj(x_ref[...], ya_s[...], yg_s[...], ga_ref[...], gg_ref[...], wout_ref[...])


def _mix_sample(x, cache_kt, cache_vt, cos, sin, gmix, w_in, sinks, gvg, coef, bias_rows, ga, gg, w_out, t_new):
    n_tok = x.shape[0]
    n_seq, _, w_buf = cache_kt.shape
    step_seqs = SEQ_GROUP * GROUPS_PER_STEP
    assert n_seq % step_seqs == 0 and n_tok == n_seq * t_new
    assert t_new & (t_new - 1) == 0 and w_buf & (w_buf - 1) == 0 and 8 % t_new == 0
    assert step_seqs * t_new == w_buf == LANES
    cache_spec = pl.BlockSpec((step_seqs, KV_W, w_buf), lambda i: (i, 0, 0))
    return pl.pallas_call(
        functools.partial(_mix_sample_kernel, t_new=t_new, w_buf=w_buf),
        grid=(n_seq // step_seqs,),
        in_specs=[_resident((n_tok, D_MODEL)), cache_spec, cache_spec, _resident((8, LANES)),
                  _resident((8, LANES)), _resident((1, D_MODEL)), _resident((D_MODEL, D_IN)),
                  pl.BlockSpec(memory_space=pltpu.SMEM), _resident((1, D_GMLP)), _resident((t_new, 8, D_GMLP)),
                  _resident((8, D_GMLP)), _resident((1, Q_W)), _resident((1, D_GMLP)),
                  _resident((D_MODEL, D_MODEL))],
        out_specs=[pl.BlockSpec((n_tok, D_MODEL), lambda i: (0, 0)), cache_spec, cache_spec,
                   pl.BlockSpec((n_tok, D_GMLP), lambda i: (0, 0))],
        out_shape=[jax.ShapeDtypeStruct((n_tok, D_MODEL), F32), jax.ShapeDtypeStruct(cache_kt.shape, F32),
                   jax.ShapeDtypeStruct(cache_vt.shape, F32), jax.ShapeDtypeStruct((n_tok, D_GMLP), F32)],
        scratch_shapes=[pltpu.VMEM((n_tok, Q_W), BF16), pltpu.VMEM((n_tok, KV_W), F32), pltpu.VMEM((n_tok, KV_W), F32),
                        pltpu.VMEM((n_tok, Q_W), F32), pltpu.VMEM((n_tok, D_GMLP), F32)],
        compiler_params=pltpu.CompilerParams(dimension_semantics=("arbitrary",), vmem_limit_bytes=VMEM_LIMIT_BYTES),
        name="mix_sample",
    )(x, cache_kt, cache_vt, cos, sin, gmix, w_in, sinks, gvg, coef, bias_rows, ga, gg, w_out)


def _rope_tables(pos):
    inv_freq = ROPE_THETA ** (-jnp.arange(0, HEAD_DIM, 2, dtype=F32) / HEAD_DIM)
    ang = pos.astype(F32)[:, None] * inv_freq[None, :]
    return jnp.tile(jnp.cos(ang), (1, 4)), jnp.tile(jnp.sin(ang), (1, 4))


def kernel(x_prompt, x_sample, cache_k_win, cache_v_win, norm_ffn1, ffn1_gate, ffn1_up, ffn1_down, norm_mix, w_in,
           attn_sinks, gmlp_v_norm, gmlp_w_s, gmlp_b_s, norm_attn_out, norm_gmlp_out, w_out, norm_ffn2, ffn2_gate,
           ffn2_up, ffn2_down, norm_final):
    depth = norm_ffn1.shape[0]
    b, s, _ = x_prompt.shape
    bd, t_new, _ = x_sample.shape
    w_buf = cache_k_win.shape[2]

    cos_s, sin_s = _rope_tables(PAST_LEN + jnp.arange(t_new, dtype=jnp.int32))
    cos_s, sin_s = jnp.tile(cos_s, (8 // t_new, 1)), jnp.tile(sin_s, (8 // t_new, 1))

    hp = x_prompt.reshape(b * s, D_MODEL)
    hs = x_sample.reshape(bd * t_new, D_MODEL)
    outs = [[] for _ in range(6)]
    ffn1_w = [w[0].astype(BF16) for w in (ffn1_gate, ffn1_up, ffn1_down)]
    for l in range(depth):
        last = l == depth - 1
        row = lambda a: a[l].reshape(1, -1)

        wcat = gmlp_w_s[l].reshape(G_HEADS // 2, 2, CHUNK, CHUNK).transpose(0, 2, 1, 3).reshape(
            G_HEADS // 2, CHUNK, 2 * CHUNK)
        bias_full = jnp.repeat(gmlp_b_s[l].T, HEAD_DIM, axis=1)
        ws_small = gmlp_w_s[l][:, :t_new, :t_new]
        coef = jnp.stack([
            jnp.stack([ws_small[:, t, t - d] if t >= d else jnp.zeros((G_HEADS,), F32) for t in range(t_new)])
            for d in range(t_new)])
        coef = jnp.tile(jnp.repeat(coef, HEAD_DIM, axis=2), (1, 8 // t_new, 1))
        bias_rows = jnp.tile(jnp.repeat(gmlp_b_s[l][:, :t_new].T, HEAD_DIM, axis=1), (8 // t_new, 1))

        hp, hs, (w_in_b, w_out_b, wg2, wu2, wd2) = _ffn_half(
            hp, hs, norm_ffn1[l], *ffn1_w, cast=(w_in[l], w_out[l], ffn2_gate[l], ffn2_up[l], ffn2_down[l]))

        hp, kpt, vpt, gvpt = _mix_prompt(hp.reshape(b, s, D_MODEL), row(norm_mix), w_in_b, attn_sinks[l],
                                         row(gmlp_v_norm), wcat, bias_full, row(norm_attn_out), row(norm_gmlp_out),
                                         w_out_b)
        to_t = lambda c: c.transpose(0, 2, 3, 1).reshape(bd, KV_W, w_buf)
        hs, kst, vst, gvs = _mix_sample(hs, to_t(cache_k_win[l]), to_t(cache_v_win[l]), cos_s, sin_s, row(norm_mix),
                                        w_in_b, attn_sinks[l], row(gmlp_v_norm), coef, bias_rows, row(norm_attn_out),
                                        row(norm_gmlp_out), w_out_b, t_new)
        from_t = lambda c: c.reshape(bd, N_KV, HEAD_DIM, w_buf).transpose(0, 3, 1, 2)

        next_ffn1 = () if last else (ffn1_gate[l + 1], ffn1_up[l + 1], ffn1_down[l + 1])
        hp, hs, ffn1_w = _ffn_half(hp.reshape(b * s, D_MODEL), hs, norm_ffn2[l], wg2, wu2, wd2,
                                   gf=norm_final if last else None, cast=next_ffn1)

        outs[0].append(kpt.reshape(b, N_KV, HEAD_DIM, WINDOW).transpose(0, 3, 1, 2))
        outs[1].append(vpt.reshape(b, N_KV, HEAD_DIM, WINDOW).transpose(0, 3, 1, 2))
        outs[2].append(from_t(kst))
        outs[3].append(from_t(vst))
        outs[4].append(gvpt.reshape(b, G_HEADS, D_GMLP // G_HEADS, CHUNK).transpose(0, 3, 1, 2))
        outs[5].append(gvs.reshape(bd, t_new, G_HEADS, D_GMLP // G_HEADS))

    return (hp.reshape(b, s, D_MODEL), hs.reshape(bd, t_new, D_MODEL)) + tuple(jnp.stack(o) for o in outs)
```

```python
import functools

import jax
import jax.numpy as jnp
from jax import lax
from jax.experimental import pallas as pl
from jax.experimental.pallas import tpu as pltpu

F32 = jnp.float32
BF16 = jnp.bfloat16

D_MODEL = 1024
D_FF = 2816
HEAD_DIM = 64
N_HEADS = 8
N_KV = 2
GQA = N_HEADS // N_KV
WINDOW = 128
CHUNK = 128
G_HEADS = 8
Q_W = N_HEADS * HEAD_DIM
KV_W = N_KV * HEAD_DIM
D_GMLP = 512
D_IN = Q_W + 2 * KV_W + 2 * D_GMLP
K_OFF = Q_W
V_OFF = K_OFF + KV_W
U_OFF = V_OFF + KV_W
GV_OFF = U_OFF + D_GMLP
ROPE_THETA = 10000.0
PAST_LEN = 16384
EPS = 1e-6
Q_SCALE = HEAD_DIM ** -0.5
LANES = 128
BF16_SUBLANES = 16

TOKEN_TILE = 512
FF_CHUNK = 256
STAGE_ROWS = 128
SEQ_GROUP = 8
GROUPS_PER_STEP = 4
VMEM_LIMIT_BYTES = 56 * 1024 * 1024

_NT = (((1,), (1,)), ((), ()))


def _rms(x, g):
    ms = jnp.mean(x * x, axis=-1, keepdims=True)
    return (x * lax.rsqrt(ms + EPS)) * g


def _resident(shape):
    zeros = (0,) * len(shape)
    return pl.BlockSpec(shape, lambda *_: zeros, pipeline_mode=pl.Buffered(1))


def _round_rows_to_bf16(src_hbm, dst_ref, stage_ref, sem):
    rows = stage_ref.shape[1]
    n_blocks = src_hbm.shape[0] // rows

    def fetch(i, slot):
        return pltpu.make_async_copy(src_hbm.at[pl.ds(i * rows, rows), :], stage_ref.at[slot], sem.at[slot])

    fetch(0, 0).start()

    def body(i, carry):
        slot = i % 2

        @pl.when(i + 1 < n_blocks)
        def _():
            fetch(i + 1, 1 - slot).start()

        fetch(i, slot).wait()
        dst_ref[pl.ds(pl.multiple_of(i * rows, rows), rows), :] = stage_ref[slot].astype(BF16)
        return carry

    lax.fori_loop(0, n_blocks, body, 0)


def _ffn_kernel(*refs, final_norm, n_cast, f32_weights):
    refs = iter(refs)
    xp_ref, xn_ref, xs_ref, g_ref, wg_in, wu_in, wd_in = (next(refs) for _ in range(7))
    gf_ref = next(refs) if final_norm else None
    cast_in = [next(refs) for _ in range(n_cast)]
    yp_ref, ys_ref = next(refs), next(refs)
    cast_out = [next(refs) for _ in range(n_cast)]
    h_ref, act0_ref, act_ref = next(refs), next(refs), next(refs)
    if f32_weights:
        wg_ref, wu_ref, wd_ref, wide_stage, narrow_stage, wide_sem, narrow_sem = (next(refs) for _ in range(7))
    else:
        wg_ref, wu_ref, wd_ref = wg_in, wu_in, wd_in

    def gate_up(h, c):
        sl = slice(c * FF_CHUNK, (c + 1) * FF_CHUNK)
        gate = jnp.dot(h, wg_ref[:, sl], preferred_element_type=F32)
        up = jnp.dot(h, wu_ref[:, sl], preferred_element_type=F32)
        return (gate * jax.nn.sigmoid(gate) * up).astype(BF16)

    on_sample = pl.program_id(0) == 0

    @pl.when(on_sample)
    def _():
        if f32_weights:
            _round_rows_to_bf16(wg_in, wg_ref, wide_stage, wide_sem)
            _round_rows_to_bf16(wu_in, wu_ref, wide_stage, wide_sem)
            _round_rows_to_bf16(wd_in, wd_ref, narrow_stage, narrow_sem)
        h0 = _rms(xs_ref[...], g_ref[...]).astype(BF16)
        h_ref[...] = h0
        act0_ref[...] = gate_up(h0, 0)

    for c in range(1, D_FF // FF_CHUNK):
        act_ref[:, (c - 1) * FF_CHUNK:c * FF_CHUNK] = gate_up(h_ref[...], c)
    down0 = jnp.dot(act0_ref[...], wd_ref[0:FF_CHUNK, :], preferred_element_type=F32)
    hn = _rms(xn_ref[...], g_ref[...]).astype(BF16)
    h_ref[...] = hn
    act0_ref[...] = gate_up(hn, 0)
    x = jnp.where(on_sample, xs_ref[...], xp_ref[...])
    y = x + 0.5 * (down0 + jnp.dot(act_ref[...], wd_ref[FF_CHUNK:, :], preferred_element_type=F32))
    if final_norm:
        y = _rms(y, gf_ref[...])
    yp_ref[...] = y

    @pl.when(on_sample)
    def _():
        ys_ref[...] = yp_ref[...]

    for src, dst in zip(cast_in, cast_out):
        dst[...] = src[...].astype(BF16)


def _cast_row_blocks(rows, n_steps):
    return max(d for d in range(1, n_steps + 1) if rows % d == 0 and (rows // d) % BF16_SUBLANES == 0)


def _ffn_half(xp, xs, g, wg, wu, wd, gf=None, cast=()):
    tm = TOKEN_TILE
    f32_weights = wg.dtype == F32
    assert all(w.dtype == wg.dtype for w in (wu, wd))
    n = xp.shape[0]
    assert n % tm == 0 and xs.shape[0] == tm
    n_tiles = n // tm
    prompt_spec = pl.BlockSpec((tm, D_MODEL), lambda i: (jnp.maximum(i - 1, 0), 0))
    next_spec = pl.BlockSpec((tm, D_MODEL), lambda i: (jnp.minimum(i, n_tiles - 1), 0))
    sample_in = _resident((tm, D_MODEL))
    sample_out = pl.BlockSpec((tm, D_MODEL), lambda i: (0, 0))
    weight_specs = ([pl.BlockSpec(memory_space=pl.ANY)] * 3 if f32_weights else
                    [_resident((D_MODEL, D_FF)), _resident((D_MODEL, D_FF)), _resident((D_FF, D_MODEL))])
    in_specs = [prompt_spec, next_spec, sample_in, _resident((1, D_MODEL))] + weight_specs
    args = [xp, xp, xs, g.reshape(1, D_MODEL), wg, wu, wd]
    if gf is not None:
        in_specs.append(_resident((1, D_MODEL)))
        args.append(gf.reshape(1, D_MODEL))
    out_specs = [prompt_spec, sample_out]
    out_shape = [jax.ShapeDtypeStruct((n, D_MODEL), F32), jax.ShapeDtypeStruct((tm, D_MODEL), F32)]
    cast_specs = []
    for w in cast:
        rows, cols = w.shape
        nb = _cast_row_blocks(rows, n_tiles)
        cast_specs.append(pl.BlockSpec((rows // nb, cols), lambda i, nb=nb: (jnp.minimum(i, nb - 1), 0)))
        out_shape.append(jax.ShapeDtypeStruct(w.shape, BF16))
    outs = pl.pallas_call(
        functools.partial(_ffn_kernel, final_norm=gf is not None, n_cast=len(cast), f32_weights=f32_weights),
        grid=(n_tiles + 1,),
        in_specs=in_specs + cast_specs,
        out_specs=out_specs + cast_specs,
        out_shape=out_shape,
        scratch_shapes=[pltpu.VMEM((tm, D_MODEL), BF16), pltpu.VMEM((tm, FF_CHUNK), BF16),
                        pltpu.VMEM((tm, D_FF - FF_CHUNK), BF16)] + ([
                            pltpu.VMEM((D_MODEL, D_FF), BF16), pltpu.VMEM((D_MODEL, D_FF), BF16),
                            pltpu.VMEM((D_FF, D_MODEL), BF16), pltpu.VMEM((2, STAGE_ROWS, D_FF), F32),
                            pltpu.VMEM((2, STAGE_ROWS, D_MODEL), F32), pltpu.SemaphoreType.DMA((2,)),
                            pltpu.SemaphoreType.DMA((2,))] if f32_weights else []),
        compiler_params=pltpu.CompilerParams(dimension_semantics=("arbitrary",), vmem_limit_bytes=VMEM_LIMIT_BYTES),
        name="ffn_final" if gf is not None else "ffn_half",
    )(*args, *cast)
    return outs[0], outs[1], list(outs[2:])


def _first_half(shape):
    return (lax.broadcasted_iota(jnp.int32, shape, 1) & (HEAD_DIM - 1)) < HEAD_DIM // 2


def _sign_sin(sin):
    return jnp.where(_first_half(sin.shape), -sin, sin)


def _rope(xg, cos, sin_signed):
    swapped = jnp.where(_first_half(xg.shape), pltpu.roll(xg, LANES - HEAD_DIM // 2, 1),
                        pltpu.roll(xg, HEAD_DIM // 2, 1))
    return xg * cos + swapped * sin_signed


def _in_proj(x, gmix, w_in, cos, sin_signed, gvn_gain):
    h = _rms(x, gmix).astype(BF16)
    z = jnp.dot(h, w_in, preferred_element_type=F32)
    q = jnp.concatenate(
        [(_rope(z[:, LANES * i:LANES * (i + 1)], cos, sin_signed) * Q_SCALE).astype(BF16) for i in range(Q_W // LANES)],
        axis=1)
    k = _rope(z[:, K_OFF:V_OFF], cos, sin_signed)
    v = z[:, V_OFF:U_OFF]
    u = jax.nn.gelu(z[:, U_OFF:GV_OFF])
    gvn = _rms(jax.nn.gelu(z[:, GV_OFF:]), gvn_gain)
    return q, k, v, u, gvn


def _out_proj(x, ya, yg, ga, gg, w_out):
    cat = jnp.concatenate([_rms(ya, ga).astype(BF16), _rms(yg, gg).astype(BF16)], axis=1)
    return x + jnp.dot(cat, w_out, preferred_element_type=F32)


def _softmax_sink(s, sink):
    m = jnp.maximum(jnp.max(s, axis=1, keepdims=True), sink)
    p = jnp.exp(s - m)
    return p, jnp.sum(p, axis=1, keepdims=True) + jnp.exp(sink - m)


def _mix_prompt_kernel(xa_ref, xc_ref, coff_ref, soff_ref, cbase_ref, sbase_ref, gmix_ref, win_ref, sinks_ref, gvg_ref,
                       wcat_ref, bias_ref, ga_ref, gg_ref, wout_ref, o_ref, ko_ref, vo_ref, gvo_ref, q_s, k_s, v_s, u_s,
                       gv_s, cat_s, *, tm, tiles_per_seq, n_tiles):
    s = pl.program_id(0)

    @pl.when(s == 0)
    def _():
        q_s[1] = jnp.zeros((tm, Q_W), BF16)
        k_s[1] = jnp.zeros((tm + WINDOW, KV_W), BF16)
        v_s[1] = jnp.zeros((tm + WINDOW, KV_W), BF16)
        u_s[1] = jnp.zeros((tm, D_GMLP), F32)
        gv_s[1] = jnp.zeros((tm, D_GMLP), BF16)
        cat_s[0] = jnp.zeros((tm, D_MODEL), BF16)

    for parity in range(2):
        pl.when(s % 2 == parity)(functools.partial(
            _mix_prompt_step, s, parity, xa_ref, xc_ref, coff_ref, soff_ref, cbase_ref, sbase_ref, gmix_ref, win_ref,
            sinks_ref, gvg_ref, wcat_ref, bias_ref, ga_ref, gg_ref, wout_ref, o_ref, ko_ref, vo_ref, gvo_ref, q_s, k_s,
            v_s, u_s, gv_s, cat_s, tm=tm, tiles_per_seq=tiles_per_seq, n_tiles=n_tiles))


def _mix_prompt_step(s, cur, xa_ref, xc_ref, coff_ref, soff_ref, cbase_ref, sbase_ref, gmix_ref, win_ref, sinks_ref,
                     gvg_ref, wcat_ref, bias_ref, ga_ref, gg_ref, wout_ref, o_ref, ko_ref, vo_ref, gvo_ref, q_s, k_s, v_s,
                     u_s, gv_s, cat_s, *, tm, tiles_per_seq, n_tiles):
    oth = 1 - cur

    h = _rms(xa_ref[0], gmix_ref[...]).astype(BF16)
    tile_in_seq = jnp.minimum(s, n_tiles - 1) % tiles_per_seq
    cb = cbase_ref[pl.ds(tile_in_seq, 1), :]
    sb = sbase_ref[pl.ds(tile_in_seq, 1), :]
    cos = cb * coff_ref[...] - sb * soff_ref[...]
    sin = _sign_sin(sb * coff_ref[...] + cb * soff_ref[...])

    def proj_q():
        z = jnp.dot(h, win_ref[:, 0:Q_W], preferred_element_type=F32)
        for i in range(Q_W // LANES):
            q_s[cur, :, LANES * i:LANES * (i + 1)] = (
                _rope(z[:, LANES * i:LANES * (i + 1)], cos, sin) * Q_SCALE).astype(BF16)

    def proj_kv():
        z = jnp.dot(h, win_ref[:, K_OFF:U_OFF], preferred_element_type=F32)
        k = _rope(z[:, 0:KV_W], cos, sin)
        v = z[:, KV_W:]
        k_s[cur, 0:WINDOW, :] = k_s[oth, tm:tm + WINDOW, :]
        v_s[cur, 0:WINDOW, :] = v_s[oth, tm:tm + WINDOW, :]
        k_s[cur, WINDOW:, :] = k.astype(BF16)
        v_s[cur, WINDOW:, :] = v.astype(BF16)
        ko_ref[0] = k[tm - WINDOW:].T
        vo_ref[0] = v[tm - WINDOW:].T

    def proj_u():
        u_s[cur] = jax.nn.gelu(jnp.dot(h, win_ref[:, U_OFF:GV_OFF], preferred_element_type=F32))

    def proj_gv():
        gvn = _rms(jax.nn.gelu(jnp.dot(h, win_ref[:, GV_OFF:], preferred_element_type=F32)), gvg_ref[...])
        gv_s[cur] = gvn.astype(BF16)
        gvo_ref[0] = gvn[tm - CHUNK:].T

    def out_half(c):
        cols = slice(c * (D_MODEL // 2), (c + 1) * (D_MODEL // 2))

        def run():
            o_ref[0, :, cols] = xc_ref[0, :, cols] + jnp.dot(cat_s[cur], wout_ref[:, cols], preferred_element_type=F32)
        return run

    mid_fill = [[out_half(0)], [proj_q], [proj_kv, proj_u], [proj_gv]]
    end_fill = [[], [], [], [out_half(1)]]
    assert len(mid_fill) == len(end_fill) == tm // WINDOW

    wrow = lax.broadcasted_iota(jnp.int32, (CHUNK, 2 * CHUNK), 0)
    wcol = lax.broadcasted_iota(jnp.int32, (CHUNK, 2 * CHUNK), 1) & (CHUNK - 1)
    wmix = [jnp.where(wcol <= wrow, wcat_ref[p], 0.0).astype(BF16) for p in range(G_HEADS // 2)]
    lane = lax.broadcasted_iota(jnp.int32, (CHUNK, LANES), 1)
    low_head = lane < HEAD_DIM

    qi = lax.broadcasted_iota(jnp.int32, (WINDOW, 2 * WINDOW), 0)
    sj = lax.broadcasted_iota(jnp.int32, (WINDOW, 2 * WINDOW), 1)
    dist = WINDOW + qi - sj
    band = (dist >= 0) & (dist < WINDOW)
    first_lo = jnp.where((s + tiles_per_seq - 1) % tiles_per_seq == 0, WINDOW, 0)

    for j in range(tm // WINDOW):
        rows = slice(j * WINDOW, (j + 1) * WINDOW)
        mask = band & (sj >= first_lo) if j == 0 else band
        qb = q_s[oth, rows, :]
        scores = []
        for kv in range(N_KV):
            qs = jnp.concatenate(
                [qb[:, HEAD_DIM * hd:HEAD_DIM * (hd + 1)] for hd in range(kv * GQA, (kv + 1) * GQA)], axis=0)
            kb = k_s[oth, j * WINDOW:(j + 2) * WINDOW, HEAD_DIM * kv:HEAD_DIM * (kv + 1)]
            scores.append(lax.dot_general(qs, kb, _NT, preferred_element_type=F32))
        for run in mid_fill[j]:
            run()
        outs, dens = [], []
        for kv in range(N_KV):
            vb = v_s[oth, j * WINDOW:(j + 2) * WINDOW, HEAD_DIM * kv:HEAD_DIM * (kv + 1)]
            ps = []
            for g in range(GQA):
                sg = jnp.where(mask, scores[kv][g * WINDOW:(g + 1) * WINDOW], -jnp.inf)
                p, den = _softmax_sink(sg, sinks_ref[kv * GQA + g])
                ps.append(p.astype(BF16))
                dens.append(den)
            outs.append(jnp.dot(jnp.concatenate(ps, axis=0), vb, preferred_element_type=F32))
        mixed = []
        for p in range(G_HEADS // 2):
            r = gv_s[oth, rows, LANES * p:LANES * (p + 1)]
            zero = jnp.zeros_like(r)
            rhs = jnp.concatenate([jnp.where(low_head, r, zero), jnp.where(low_head, zero, r)], axis=0)
            mixed.append(jnp.dot(wmix[p], rhs, preferred_element_type=F32))
        for run in end_fill[j]:
            run()
        ya = [outs[hd // GQA][(hd % GQA) * WINDOW:(hd % GQA + 1) * WINDOW] / dens[hd] for hd in range(N_HEADS)]
        cat_s[oth, rows, 0:Q_W] = _rms(jnp.concatenate(ya, axis=1), ga_ref[...]).astype(BF16)
        yg = u_s[oth, rows, :] * (jnp.concatenate(mixed, axis=1) + bias_ref[...])
        cat_s[oth, rows, Q_W:] = _rms(yg, gg_ref[...]).astype(BF16)


def _mix_prompt(x, gmix, w_in, sinks, gvg, wcat, bias_full, ga, gg, w_out):
    b, s, _ = x.shape
    tm = TOKEN_TILE
    assert s % tm == 0 and tm % WINDOW == 0
    tiles_per_seq = s // tm
    n_tiles = b * tiles_per_seq
    cos_off, sin_off = _rope_tables(jnp.arange(tm, dtype=jnp.int32))
    cos_base, sin_base = _rope_tables(tm * jnp.arange(tiles_per_seq, dtype=jnp.int32))
    proj_tile = lambda i: jnp.minimum(i, n_tiles - 1)
    out_tile = lambda i: jnp.maximum(i - 2, 0)
    x_tiles = x.reshape(n_tiles, tm, D_MODEL)
    last = lambda width: pl.BlockSpec((1, width, WINDOW), lambda i: (proj_tile(i) // tiles_per_seq, 0, 0))
    out, ko, vo, gvo = pl.pallas_call(
        functools.partial(_mix_prompt_kernel, tm=tm, tiles_per_seq=tiles_per_seq, n_tiles=n_tiles),
        grid=(n_tiles + 2,),
        in_specs=[pl.BlockSpec((1, tm, D_MODEL), lambda i: (proj_tile(i), 0, 0)),
                  pl.BlockSpec((1, tm, D_MODEL), lambda i: (out_tile(i), 0, 0)),
                  _resident((tm, LANES)), _resident((tm, LANES)), _resident((tiles_per_seq, LANES)),
                  _resident((tiles_per_seq, LANES)), _resident((1, D_MODEL)), _resident((D_MODEL, D_IN)),
                  pl.BlockSpec(memory_space=pltpu.SMEM), _resident((1, D_GMLP)),
                  _resident((G_HEADS // 2, CHUNK, 2 * CHUNK)), _resident((CHUNK, D_GMLP)), _resident((1, Q_W)),
                  _resident((1, D_GMLP)), _resident((D_MODEL, D_MODEL))],
        out_specs=[pl.BlockSpec((1, tm, D_MODEL), lambda i: (out_tile(i), 0, 0)), last(KV_W), last(KV_W),
                   last(D_GMLP)],
        out_shape=[jax.ShapeDtypeStruct((n_tiles, tm, D_MODEL), F32), jax.ShapeDtypeStruct((b, KV_W, WINDOW), F32),
                   jax.ShapeDtypeStruct((b, KV_W, WINDOW), F32), jax.ShapeDtypeStruct((b, D_GMLP, CHUNK), F32)],
        scratch_shapes=[pltpu.VMEM((2, tm, Q_W), BF16), pltpu.VMEM((2, tm + WINDOW, KV_W), BF16),
                        pltpu.VMEM((2, tm + WINDOW, KV_W), BF16), pltpu.VMEM((2, tm, D_GMLP), F32),
                        pltpu.VMEM((2, tm, D_GMLP), BF16), pltpu.VMEM((2, tm, D_MODEL), BF16)],
        compiler_params=pltpu.CompilerParams(dimension_semantics=("arbitrary",), vmem_limit_bytes=VMEM_LIMIT_BYTES),
        name="mix_prompt",
    )(x_tiles, x_tiles, cos_off, sin_off, cos_base, sin_base, gmix, w_in, sinks, gvg, wcat, bias_full, ga, gg, w_out)
    return out.reshape(b, s, D_MODEL), ko, vo, gvo


def _mix_sample_kernel(x_ref, ckt_ref, cvt_ref, cos_ref, sin_ref, gmix_ref, win_ref, sinks_ref, gvg_ref, coef_ref,
                       biasr_ref, ga_ref, gg_ref, wout_ref, o_ref, kot_ref, vot_ref, gvo_ref, q_s, k_s, v_s, ya_s,
                       yg_s, *, t_new, w_buf):
    step = pl.program_id(0)
    n_tok = x_ref.shape[0]
    step_seqs = SEQ_GROUP * GROUPS_PER_STEP
    step_rows = step_seqs * t_new
    grp_rows = SEQ_GROUP * t_new
    grp_keys = SEQ_GROUP * w_buf
    sub = 8

    @pl.when(step == 0)
    def _():
        tile = lambda tab: jnp.broadcast_to(tab[None], (n_tok // sub, sub, LANES)).reshape(n_tok, LANES)
        q, k, v, u, gvn = _in_proj(x_ref[...], gmix_ref[...], win_ref[...], tile(cos_ref[...]),
                                   tile(_sign_sin(sin_ref[...])), gvg_ref[...])
        q_s[...] = q
        k_s[...] = k
        v_s[...] = v
        gvo_ref[...] = gvn
        g3 = gvn.reshape(n_tok // sub, sub, D_GMLP)
        trow = lax.broadcasted_iota(jnp.int32, (1, sub, D_GMLP), 1) & (t_new - 1)
        mixed = biasr_ref[...][None] + coef_ref[0][None] * g3
        for d in range(1, t_new):
            shifted = jnp.where(trow >= d, pltpu.roll(g3, d, 1), 0.0)
            mixed = mixed + coef_ref[d][None] * shifted
        yg_s[...] = (u.reshape(n_tok // sub, sub, D_GMLP) * mixed).reshape(n_tok, D_GMLP)

    row0 = pl.multiple_of(step * step_rows, step_rows)
    q_step = q_s[pl.ds(row0, step_rows), :]
    kn = k_s[pl.ds(row0, step_rows), :]
    vn = v_s[pl.ds(row0, step_rows), :]

    knt = kn.T
    vnt = vn.T
    tail = lax.broadcasted_iota(jnp.int32, (KV_W, w_buf), 1) >= w_buf - t_new
    for b in range(step_seqs):
        shift = (w_buf - t_new - t_new * b) % w_buf
        kot_ref[b] = jnp.where(tail, pltpu.roll(knt, shift, 1) if shift else knt,
                               pltpu.roll(ckt_ref[b], w_buf - t_new, 1))
        vot_ref[b] = jnp.where(tail, pltpu.roll(vnt, shift, 1) if shift else vnt,
                               pltpu.roll(cvt_ref[b], w_buf - t_new, 1))

    knb = kn.astype(BF16)
    vnb = vn.astype(BF16)

    n_rows = GQA * grp_rows
    shift_t = t_new.bit_length() - 1
    shift_w = w_buf.bit_length() - 1
    r = lax.broadcasted_iota(jnp.int32, (n_rows, grp_keys), 0)
    c = lax.broadcasted_iota(jnp.int32, (n_rows, grp_keys), 1)
    mask_c = ((c >> shift_w) == ((r & (grp_rows - 1)) >> shift_t)) & ((c & (w_buf - 1)) > (r & (t_new - 1)))
    r2 = lax.broadcasted_iota(jnp.int32, (n_rows, grp_rows), 0)
    c2 = lax.broadcasted_iota(jnp.int32, (n_rows, grp_rows), 1)
    mask_n = ((c2 >> shift_t) == ((r2 & (grp_rows - 1)) >> shift_t)) & ((c2 & (t_new - 1)) <= (r2 & (t_new - 1)))
    row_head = lax.broadcasted_iota(jnp.int32, (n_rows, 1), 0) >> (grp_rows.bit_length() - 1)

    for grp in range(GROUPS_PER_STEP):
        rows = slice(grp * grp_rows, (grp + 1) * grp_rows)
        seqs = range(grp * SEQ_GROUP, (grp + 1) * SEQ_GROUP)
        for kv in range(N_KV):
            heads = [kv * GQA + i for i in range(GQA)]
            lanes = slice(HEAD_DIM * kv, HEAD_DIM * (kv + 1))
            kt = jnp.concatenate([ckt_ref[b, lanes, :] for b in seqs], axis=1).astype(BF16)
            vt = jnp.concatenate([cvt_ref[b, lanes, :] for b in seqs], axis=1).astype(BF16)
            qs = jnp.concatenate([q_step[rows, HEAD_DIM * hd:HEAD_DIM * (hd + 1)] for hd in heads], axis=0)
            s_c = jnp.where(mask_c, jnp.dot(qs, kt, preferred_element_type=F32), -jnp.inf)
            s_n = jnp.where(mask_n, lax.dot_general(qs, knb[rows, lanes], _NT, preferred_element_type=F32), -jnp.inf)
            sink = jnp.full((n_rows, 1), sinks_ref[heads[0]], F32)
            for i in range(1, GQA):
                sink = jnp.where(row_head == i, sinks_ref[heads[i]], sink)
            m = jnp.maximum(jnp.maximum(jnp.max(s_c, axis=1, keepdims=True), jnp.max(s_n, axis=1, keepdims=True)),
                            sink)
            p_c = jnp.exp(s_c - m)
            p_n = jnp.exp(s_n - m)
            den = jnp.sum(p_c, axis=1, keepdims=True) + jnp.sum(p_n, axis=1, keepdims=True) + jnp.exp(sink - m)
            o = (lax.dot_general(p_c.astype(BF16), vt, _NT, preferred_element_type=F32)
                 + jnp.dot(p_n.astype(BF16), vnb[rows, lanes], preferred_element_type=F32)) / den
            for i, hd in enumerate(heads):
                ya_s[pl.ds(row0 + grp * grp_rows, grp_rows), HEAD_DIM * hd:HEAD_DIM * (hd + 1)] = (
                    o[i * grp_rows:(i + 1) * grp_rows])

    @pl.when(step == pl.num_programs(0) - 1)
    def _():
        o_ref[...] = _out_proj(x_ref[...], ya_s[...], yg_s[...], ga_ref[...], gg_ref[...], wout_ref[...])


def _mix_sample(x, cache_kt, cache_vt, cos, sin, gmix, w_in, sinks, gvg, coef, bias_rows, ga, gg, w_out, t_new):
    n_tok = x.shape[0]
    n_seq, _, w_buf = cache_kt.shape
    step_seqs = SEQ_GROUP * GROUPS_PER_STEP
    assert n_seq % step_seqs == 0 and n_tok == n_seq * t_new
    assert t_new & (t_new - 1) == 0 and w_buf & (w_buf - 1) == 0 and 8 % t_new == 0
    assert step_seqs * t_new == w_buf == LANES
    cache_spec = pl.BlockSpec((step_seqs, KV_W, w_buf), lambda i: (i, 0, 0))
    return pl.pallas_call(
        functools.partial(_mix_sample_kernel, t_new=t_new, w_buf=w_buf),
        grid=(n_seq // step_seqs,),
        in_specs=[_resident((n_tok, D_MODEL)), cache_spec, cache_spec, _resident((8, LANES)),
                  _resident((8, LANES)), _resident((1, D_MODEL)), _resident((D_MODEL, D_IN)),
                  pl.BlockSpec(memory_space=pltpu.SMEM), _resident((1, D_GMLP)), _resident((t_new, 8, D_GMLP)),
                  _resident((8, D_GMLP)), _resident((1, Q_W)), _resident((1, D_GMLP)),
                  _resident((D_MODEL, D_MODEL))],
        out_specs=[pl.BlockSpec((n_tok, D_MODEL), lambda i: (0, 0)), cache_spec, cache_spec,
                   pl.BlockSpec((n_tok, D_GMLP), lambda i: (0, 0))],
        out_shape=[jax.ShapeDtypeStruct((n_tok, D_MODEL), F32), jax.ShapeDtypeStruct(cache_kt.shape, F32),
                   jax.ShapeDtypeStruct(cache_vt.shape, F32), jax.ShapeDtypeStruct((n_tok, D_GMLP), F32)],
        scratch_shapes=[pltpu.VMEM((n_tok, Q_W), BF16), pltpu.VMEM((n_tok, KV_W), F32), pltpu.VMEM((n_tok, KV_W), F32),
                        pltpu.VMEM((n_tok, Q_W), F32), pltpu.VMEM((n_tok, D_GMLP), F32)],
        compiler_params=pltpu.CompilerParams(dimension_semantics=("arbitrary",), vmem_limit_bytes=VMEM_LIMIT_BYTES),
        name="mix_sample",
    )(x, cache_kt, cache_vt, cos, sin, gmix, w_in, sinks, gvg, coef, bias_rows, ga, gg, w_out)


def _rope_tables(pos):
    inv_freq = ROPE_THETA ** (-jnp.arange(0, HEAD_DIM, 2, dtype=F32) / HEAD_DIM)
    ang = pos.astype(F32)[:, None] * inv_freq[None, :]
    return jnp.tile(jnp.cos(ang), (1, 4)), jnp.tile(jnp.sin(ang), (1, 4))


def kernel(x_prompt, x_sample, cache_k_win, cache_v_win, norm_ffn1, ffn1_gate, ffn1_up, ffn1_down, norm_mix, w_in,
           attn_sinks, gmlp_v_norm, gmlp_w_s, gmlp_b_s, norm_attn_out, norm_gmlp_out, w_out, norm_ffn2, ffn2_gate,
           ffn2_up, ffn2_down, norm_final):
    depth = norm_ffn1.shape[0]
    b, s, _ = x_prompt.shape
    bd, t_new, _ = x_sample.shape
    w_buf = cache_k_win.shape[2]

    cos_s, sin_s = _rope_tables(PAST_LEN + jnp.arange(t_new, dtype=jnp.int32))
    cos_s, sin_s = jnp.tile(cos_s, (8 // t_new, 1)), jnp.tile(sin_s, (8 // t_new, 1))

    hp = x_prompt.reshape(b * s, D_MODEL)
    hs = x_sample.reshape(bd * t_new, D_MODEL)
    outs = [[] for _ in range(6)]
    ffn1_w = [ffn1_gate[0], ffn1_up[0], ffn1_down[0]]
    for l in range(depth):
        last = l == depth - 1
        row = lambda a: a[l].reshape(1, -1)

        wcat = gmlp_w_s[l].reshape(G_HEADS // 2, 2, CHUNK, CHUNK).transpose(0, 2, 1, 3).reshape(
            G_HEADS // 2, CHUNK, 2 * CHUNK)
        bias_full = jnp.repeat(gmlp_b_s[l].T, HEAD_DIM, axis=1)
        ws_small = gmlp_w_s[l][:, :t_new, :t_new]
        coef = jnp.stack([
            jnp.stack([ws_small[:, t, t - d] if t >= d else jnp.zeros((G_HEADS,), F32) for t in range(t_new)])
            for d in range(t_new)])
        coef = jnp.tile(jnp.repeat(coef, HEAD_DIM, axis=2), (1, 8 // t_new, 1))
        bias_rows = jnp.tile(jnp.repeat(gmlp_b_s[l][:, :t_new].T, HEAD_DIM, axis=1), (8 // t_new, 1))

        hp, hs, (w_in_b, w_out_b, wg2, wu2, wd2) = _ffn_half(
            hp, hs, norm_ffn1[l], *ffn1_w, cast=(w_in[l], w_out[l], ffn2_gate[l], ffn2_up[l], ffn2_down[l]))

        hp, kpt, vpt, gvpt = _mix_prompt(hp.reshape(b, s, D_MODEL), row(norm_mix), w_in_b, attn_sinks[l],
                                         row(gmlp_v_norm), wcat, bias_full, row(norm_attn_out), row(norm_gmlp_out),
                                         w_out_b)
        to_t = lambda c: c.transpose(0, 2, 3, 1).reshape(bd, KV_W, w_buf)
        hs, kst, vst, gvs = _mix_sample(hs, to_t(cache_k_win[l]), to_t(cache_v_win[l]), cos_s, sin_s, row(norm_mix),
                                        w_in_b, attn_sinks[l], row(gmlp_v_norm), coef, bias_rows, row(norm_attn_out),
                                        row(norm_gmlp_out), w_out_b, t_new)
        from_t = lambda c: c.reshape(bd, N_KV, HEAD_DIM, w_buf).transpose(0, 3, 1, 2)

        next_ffn1 = () if last else (ffn1_gate[l + 1], ffn1_up[l + 1], ffn1_down[l + 1])
        hp, hs, ffn1_w = _ffn_half(hp.reshape(b * s, D_MODEL), hs, norm_ffn2[l], wg2, wu2, wd2,
                                   gf=norm_final if last else None, cast=next_ffn1)

        outs[0].append(kpt.reshape(b, N_KV, HEAD_DIM, WINDOW).transpose(0, 3, 1, 2))
        outs[1].append(vpt.reshape(b, N_KV, HEAD_DIM, WINDOW).transpose(0, 3, 1, 2))
        outs[2].append(from_t(kst))
        outs[3].append(from_t(vst))
        outs[4].append(gvpt.reshape(b, G_HEADS, D_GMLP // G_HEADS, CHUNK).transpose(0, 3, 1, 2))
        outs[5].append(gvs.reshape(bd, t_new, G_HEADS, D_GMLP // G_HEADS))

    return (hp.reshape(b, s, D_MODEL), hs.reshape(bd, t_new, D_MODEL)) + tuple(jnp.stack(o) for o in outs)
```

```python
import functools

import jax
import jax.numpy as jnp
from jax import lax
from jax.experimental import pallas as pl
from jax.experimental.pallas import tpu as pltpu

F32 = jnp.float32
BF16 = jnp.bfloat16

D_MODEL = 1024
D_FF = 2816
HEAD_DIM = 64
N_HEADS = 8
N_KV = 2
GQA = N_HEADS // N_KV
WINDOW = 128
CHUNK = 128
G_HEADS = 8
Q_W = N_HEADS * HEAD_DIM
KV_W = N_KV * HEAD_DIM
D_GMLP = 512
D_IN = Q_W + 2 * KV_W + 2 * D_GMLP
K_OFF = Q_W
V_OFF = K_OFF + KV_W
U_OFF = V_OFF + KV_W
GV_OFF = U_OFF + D_GMLP
ROPE_THETA = 10000.0
PAST_LEN = 16384
EPS = 1e-6
Q_SCALE = HEAD_DIM ** -0.5
LANES = 128
BF16_SUBLANES = 16

TOKEN_TILE = 512
FF_CHUNK = 256
STAGE_ROWS = 64
STAGE_SLOTS = 4
SEQ_GROUP = 8
GROUPS_PER_STEP = 4
VMEM_LIMIT_BYTES = 56 * 1024 * 1024

_NT = (((1,), (1,)), ((), ()))


def _rms(x, g):
    ms = jnp.mean(x * x, axis=-1, keepdims=True)
    return (x * lax.rsqrt(ms + EPS)) * g


def _resident(shape):
    zeros = (0,) * len(shape)
    return pl.BlockSpec(shape, lambda *_: zeros, pipeline_mode=pl.Buffered(1))


def _round_rows_to_bf16(src_hbm, dst_ref, stage_ref, sem):
    n_slots, rows, _ = stage_ref.shape
    n_blocks = src_hbm.shape[0] // rows
    assert n_blocks >= n_slots

    def fetch(i, slot):
        return pltpu.make_async_copy(src_hbm.at[pl.ds(i * rows, rows), :], stage_ref.at[slot], sem.at[slot])

    for i in range(n_slots - 1):
        fetch(i, i).start()

    def body(i, carry):
        slot = i % n_slots

        @pl.when(i + n_slots - 1 < n_blocks)
        def _():
            fetch(i + n_slots - 1, (i + n_slots - 1) % n_slots).start()

        fetch(i, slot).wait()
        dst_ref[pl.ds(pl.multiple_of(i * rows, rows), rows), :] = stage_ref[slot].astype(BF16)
        return carry

    lax.fori_loop(0, n_blocks, body, 0)


def _ffn_kernel(*refs, final_norm, n_cast, f32_weights):
    refs = iter(refs)
    xp_ref, xn_ref, xs_ref, g_ref, wg_in, wu_in, wd_in = (next(refs) for _ in range(7))
    gf_ref = next(refs) if final_norm else None
    cast_in = [next(refs) for _ in range(n_cast)]
    yp_ref, ys_ref = next(refs), next(refs)
    cast_out = [next(refs) for _ in range(n_cast)]
    h_ref, act0_ref, act_ref = next(refs), next(refs), next(refs)
    if f32_weights:
        wg_ref, wu_ref, wd_ref, wide_stage, narrow_stage, wide_sem, narrow_sem = (next(refs) for _ in range(7))
    else:
        wg_ref, wu_ref, wd_ref = wg_in, wu_in, wd_in

    def gate_up(h, c):
        sl = slice(c * FF_CHUNK, (c + 1) * FF_CHUNK)
        gate = jnp.dot(h, wg_ref[:, sl], preferred_element_type=F32)
        up = jnp.dot(h, wu_ref[:, sl], preferred_element_type=F32)
        return (gate * jax.nn.sigmoid(gate) * up).astype(BF16)

    on_sample = pl.program_id(0) == 0

    @pl.when(on_sample)
    def _():
        if f32_weights:
            _round_rows_to_bf16(wg_in, wg_ref, wide_stage, wide_sem)
            _round_rows_to_bf16(wu_in, wu_ref, wide_stage, wide_sem)
            _round_rows_to_bf16(wd_in, wd_ref, narrow_stage, narrow_sem)
        h0 = _rms(xs_ref[...], g_ref[...]).astype(BF16)
        h_ref[...] = h0
        act0_ref[...] = gate_up(h0, 0)

    for c in range(1, D_FF // FF_CHUNK):
        act_ref[:, (c - 1) * FF_CHUNK:c * FF_CHUNK] = gate_up(h_ref[...], c)
    down0 = jnp.dot(act0_ref[...], wd_ref[0:FF_CHUNK, :], preferred_element_type=F32)
    hn = _rms(xn_ref[...], g_ref[...]).astype(BF16)
    h_ref[...] = hn
    act0_ref[...] = gate_up(hn, 0)
    x = jnp.where(on_sample, xs_ref[...], xp_ref[...])
    y = x + 0.5 * (down0 + jnp.dot(act_ref[...], wd_ref[FF_CHUNK:, :], preferred_element_type=F32))
    if final_norm:
        y = _rms(y, gf_ref[...])
    yp_ref[...] = y

    @pl.when(on_sample)
    def _():
        ys_ref[...] = yp_ref[...]

    for src, dst in zip(cast_in, cast_out):
        dst[...] = src[...].astype(BF16)


def _cast_row_blocks(rows, n_steps):
    return max(d for d in range(1, n_steps + 1) if rows % d == 0 and (rows // d) % BF16_SUBLANES == 0)


def _ffn_half(xp, xs, g, wg, wu, wd, gf=None, cast=()):
    tm = TOKEN_TILE
    f32_weights = wg.dtype == F32
    assert all(w.dtype == wg.dtype for w in (wu, wd))
    n = xp.shape[0]
    assert n % tm == 0 and xs.shape[0] == tm
    n_tiles = n // tm
    prompt_spec = pl.BlockSpec((tm, D_MODEL), lambda i: (jnp.maximum(i - 1, 0), 0))
    next_spec = pl.BlockSpec((tm, D_MODEL), lambda i: (jnp.minimum(i, n_tiles - 1), 0))
    sample_in = _resident((tm, D_MODEL))
    sample_out = pl.BlockSpec((tm, D_MODEL), lambda i: (0, 0))
    weight_specs = ([pl.BlockSpec(memory_space=pl.ANY)] * 3 if f32_weights else
                    [_resident((D_MODEL, D_FF)), _resident((D_MODEL, D_FF)), _resident((D_FF, D_MODEL))])
    in_specs = [prompt_spec, next_spec, sample_in, _resident((1, D_MODEL))] + weight_specs
    args = [xp, xp, xs, g.reshape(1, D_MODEL), wg, wu, wd]
    if gf is not None:
        in_specs.append(_resident((1, D_MODEL)))
        args.append(gf.reshape(1, D_MODEL))
    out_specs = [prompt_spec, sample_out]
    out_shape = [jax.ShapeDtypeStruct((n, D_MODEL), F32), jax.ShapeDtypeStruct((tm, D_MODEL), F32)]
    cast_specs = []
    for w in cast:
        rows, cols = w.shape
        nb = _cast_row_blocks(rows, n_tiles)
        cast_specs.append(pl.BlockSpec((rows // nb, cols), lambda i, nb=nb: (jnp.minimum(i, nb - 1), 0)))
        out_shape.append(jax.ShapeDtypeStruct(w.shape, BF16))
    outs = pl.pallas_call(
        functools.partial(_ffn_kernel, final_norm=gf is not None, n_cast=len(cast), f32_weights=f32_weights),
        grid=(n_tiles + 1,),
        in_specs=in_specs + cast_specs,
        out_specs=out_specs + cast_specs,
        out_shape=out_shape,
        scratch_shapes=[pltpu.VMEM((tm, D_MODEL), BF16), pltpu.VMEM((tm, FF_CHUNK), BF16),
                        pltpu.VMEM((tm, D_FF - FF_CHUNK), BF16)] + ([
                            pltpu.VMEM((D_MODEL, D_FF), BF16), pltpu.VMEM((D_MODEL, D_FF), BF16),
                            pltpu.VMEM((D_FF, D_MODEL), BF16), pltpu.VMEM((STAGE_SLOTS, STAGE_ROWS, D_FF), F32),
                            pltpu.VMEM((STAGE_SLOTS, STAGE_ROWS, D_MODEL), F32),
                            pltpu.SemaphoreType.DMA((STAGE_SLOTS,)), pltpu.SemaphoreType.DMA((STAGE_SLOTS,))]
                        if f32_weights else []),
        compiler_params=pltpu.CompilerParams(dimension_semantics=("arbitrary",), vmem_limit_bytes=VMEM_LIMIT_BYTES),
        name="ffn_final" if gf is not None else "ffn_half",
    )(*args, *cast)
    return outs[0], outs[1], list(outs[2:])


def _first_half(shape):
    return (lax.broadcasted_iota(jnp.int32, shape, 1) & (HEAD_DIM - 1)) < HEAD_DIM // 2


def _sign_sin(sin):
    return jnp.where(_first_half(sin.shape), -sin, sin)


def _rope(xg, cos, sin_signed):
    swapped = jnp.where(_first_half(xg.shape), pltpu.roll(xg, LANES - HEAD_DIM // 2, 1),
                        pltpu.roll(xg, HEAD_DIM // 2, 1))
    return xg * cos + swapped * sin_signed


def _in_proj(x, gmix, w_in, cos, sin_signed, gvn_gain):
    h = _rms(x, gmix).astype(BF16)
    z = jnp.dot(h, w_in, preferred_element_type=F32)
    q = jnp.concatenate(
        [(_rope(z[:, LANES * i:LANES * (i + 1)], cos, sin_signed) * Q_SCALE).astype(BF16) for i in range(Q_W // LANES)],
        axis=1)
    k = _rope(z[:, K_OFF:V_OFF], cos, sin_signed)
    v = z[:, V_OFF:U_OFF]
    u = jax.nn.gelu(z[:, U_OFF:GV_OFF])
    gvn = _rms(jax.nn.gelu(z[:, GV_OFF:]), gvn_gain)
    return q, k, v, u, gvn


def _out_proj(x, ya, yg, ga, gg, w_out):
    cat = jnp.concatenate([_rms(ya, ga).astype(BF16), _rms(yg, gg).astype(BF16)], axis=1)
    return x + jnp.dot(cat, w_out, preferred_element_type=F32)


def _softmax_sink(s, sink):
    m = jnp.maximum(jnp.max(s, axis=1, keepdims=True), sink)
    p = jnp.exp(s - m)
    return p, jnp.sum(p, axis=1, keepdims=True) + jnp.exp(sink - m)


def _mix_prompt_kernel(xa_ref, xc_ref, coff_ref, soff_ref, cbase_ref, sbase_ref, gmix_ref, win_ref, sinks_ref, gvg_ref,
                       wcat_ref, bias_ref, ga_ref, gg_ref, wout_ref, o_ref, ko_ref, vo_ref, gvo_ref, q_s, k_s, v_s, u_s,
                       gv_s, cat_s, *, tm, tiles_per_seq, n_tiles):
    s = pl.program_id(0)

    @pl.when(s == 0)
    def _():
        q_s[1] = jnp.zeros((tm, Q_W), BF16)
        k_s[1] = jnp.zeros((tm + WINDOW, KV_W), BF16)
        v_s[1] = jnp.zeros((tm + WINDOW, KV_W), BF16)
        u_s[1] = jnp.zeros((tm, D_GMLP), F32)
        gv_s[1] = jnp.zeros((tm, D_GMLP), BF16)
        cat_s[0] = jnp.zeros((tm, D_MODEL), BF16)

    for parity in range(2):
        pl.when(s % 2 == parity)(functools.partial(
            _mix_prompt_step, s, parity, xa_ref, xc_ref, coff_ref, soff_ref, cbase_ref, sbase_ref, gmix_ref, win_ref,
            sinks_ref, gvg_ref, wcat_ref, bias_ref, ga_ref, gg_ref, wout_ref, o_ref, ko_ref, vo_ref, gvo_ref, q_s, k_s,
            v_s, u_s, gv_s, cat_s, tm=tm, tiles_per_seq=tiles_per_seq, n_tiles=n_tiles))


def _mix_prompt_step(s, cur, xa_ref, xc_ref, coff_ref, soff_ref, cbase_ref, sbase_ref, gmix_ref, win_ref, sinks_ref,
                     gvg_ref, wcat_ref, bias_ref, ga_ref, gg_ref, wout_ref, o_ref, ko_ref, vo_ref, gvo_ref, q_s, k_s, v_s,
                     u_s, gv_s, cat_s, *, tm, tiles_per_seq, n_tiles):
    oth = 1 - cur

    h = _rms(xa_ref[0], gmix_ref[...]).astype(BF16)
    tile_in_seq = jnp.minimum(s, n_tiles - 1) % tiles_per_seq
    cb = cbase_ref[pl.ds(tile_in_seq, 1), :]
    sb = sbase_ref[pl.ds(tile_in_seq, 1), :]
    cos = cb * coff_ref[...] - sb * soff_ref[...]
    sin = _sign_sin(sb * coff_ref[...] + cb * soff_ref[...])

    def proj_q():
        z = jnp.dot(h, win_ref[:, 0:Q_W], preferred_element_type=F32)
        for i in range(Q_W // LANES):
            q_s[cur, :, LANES * i:LANES * (i + 1)] = (
                _rope(z[:, LANES * i:LANES * (i + 1)], cos, sin) * Q_SCALE).astype(BF16)

    def proj_kv():
        z = jnp.dot(h, win_ref[:, K_OFF:U_OFF], preferred_element_type=F32)
        k = _rope(z[:, 0:KV_W], cos, sin)
        v = z[:, KV_W:]
        k_s[cur, 0:WINDOW, :] = k_s[oth, tm:tm + WINDOW, :]
        v_s[cur, 0:WINDOW, :] = v_s[oth, tm:tm + WINDOW, :]
        k_s[cur, WINDOW:, :] = k.astype(BF16)
        v_s[cur, WINDOW:, :] = v.astype(BF16)
        ko_ref[0] = k[tm - WINDOW:].T
        vo_ref[0] = v[tm - WINDOW:].T

    def proj_u():
        u_s[cur] = jax.nn.gelu(jnp.dot(h, win_ref[:, U_OFF:GV_OFF], preferred_element_type=F32))

    def proj_gv():
        gvn = _rms(jax.nn.gelu(jnp.dot(h, win_ref[:, GV_OFF:], preferred_element_type=F32)), gvg_ref[...])
        gv_s[cur] = gvn.astype(BF16)
        gvo_ref[0] = gvn[tm - CHUNK:].T

    def out_half(c):
        cols = slice(c * (D_MODEL // 2), (c + 1) * (D_MODEL // 2))

        def run():
            o_ref[0, :, cols] = xc_ref[0, :, cols] + jnp.dot(cat_s[cur], wout_ref[:, cols], preferred_element_type=F32)
        return run

    mid_fill = [[out_half(0)], [proj_q], [proj_kv, proj_u], [proj_gv]]
    end_fill = [[], [], [], [out_half(1)]]
    assert len(mid_fill) == len(end_fill) == tm // WINDOW

    wrow = lax.broadcasted_iota(jnp.int32, (CHUNK, 2 * CHUNK), 0)
    wcol = lax.broadcasted_iota(jnp.int32, (CHUNK, 2 * CHUNK), 1) & (CHUNK - 1)
    wmix = [jnp.where(wcol <= wrow, wcat_ref[p], 0.0).astype(BF16) for p in range(G_HEADS // 2)]
    lane = lax.broadcasted_iota(jnp.int32, (CHUNK, LANES), 1)
    low_head = lane < HEAD_DIM

    qi = lax.broadcasted_iota(jnp.int32, (WINDOW, 2 * WINDOW), 0)
    sj = lax.broadcasted_iota(jnp.int32, (WINDOW, 2 * WINDOW), 1)
    dist = WINDOW + qi - sj
    band = (dist >= 0) & (dist < WINDOW)
    first_lo = jnp.where((s + tiles_per_seq - 1) % tiles_per_seq == 0, WINDOW, 0)

    for j in range(tm // WINDOW):
        rows = slice(j * WINDOW, (j + 1) * WINDOW)
        mask = band & (sj >= first_lo) if j == 0 else band
        qb = q_s[oth, rows, :]
        scores = []
        for kv in range(N_KV):
            qs = jnp.concatenate(
                [qb[:, HEAD_DIM * hd:HEAD_DIM * (hd + 1)] for hd in range(kv * GQA, (kv + 1) * GQA)], axis=0)
            kb = k_s[oth, j * WINDOW:(j + 2) * WINDOW, HEAD_DIM * kv:HEAD_DIM * (kv + 1)]
            scores.append(lax.dot_general(qs, kb, _NT, preferred_element_type=F32))
        for run in mid_fill[j]:
            run()
        outs, dens = [], []
        for kv in range(N_KV):
            vb = v_s[oth, j * WINDOW:(j + 2) * WINDOW, HEAD_DIM * kv:HEAD_DIM * (kv + 1)]
            ps = []
            for g in range(GQA):
                sg = jnp.where(mask, scores[kv][g * WINDOW:(g + 1) * WINDOW], -jnp.inf)
                p, den = _softmax_sink(sg, sinks_ref[kv * GQA + g])
                ps.append(p.astype(BF16))
                dens.append(den)
            outs.append(jnp.dot(jnp.concatenate(ps, axis=0), vb, preferred_element_type=F32))
        mixed = []
        for p in range(G_HEADS // 2):
            r = gv_s[oth, rows, LANES * p:LANES * (p + 1)]
            zero = jnp.zeros_like(r)
            rhs = jnp.concatenate([jnp.where(low_head, r, zero), jnp.where(low_head, zero, r)], axis=0)
            mixed.append(jnp.dot(wmix[p], rhs, preferred_element_type=F32))
        for run in end_fill[j]:
            run()
        ya = [outs[hd // GQA][(hd % GQA) * WINDOW:(hd % GQA + 1) * WINDOW] / dens[hd] for hd in range(N_HEADS)]
        cat_s[oth, rows, 0:Q_W] = _rms(jnp.concatenate(ya, axis=1), ga_ref[...]).astype(BF16)
        yg = u_s[oth, rows, :] * (jnp.concatenate(mixed, axis=1) + bias_ref[...])
        cat_s[oth, rows, Q_W:] = _rms(yg, gg_ref[...]).astype(BF16)


def _mix_prompt(x, gmix, w_in, sinks, gvg, wcat, bias_full, ga, gg, w_out):
    b, s, _ = x.shape
    tm = TOKEN_TILE
    assert s % tm == 0 and tm % WINDOW == 0
    tiles_per_seq = s // tm
    n_tiles = b * tiles_per_seq
    cos_off, sin_off = _rope_tables(jnp.arange(tm, dtype=jnp.int32))
    cos_base, sin_base = _rope_tables(tm * jnp.arange(tiles_per_seq, dtype=jnp.int32))
    proj_tile = lambda i: jnp.minimum(i, n_tiles - 1)
    out_tile = lambda i: jnp.maximum(i - 2, 0)
    x_tiles = x.reshape(n_tiles, tm, D_MODEL)
    last = lambda width: pl.BlockSpec((1, width, WINDOW), lambda i: (proj_tile(i) // tiles_per_seq, 0, 0))
    out, ko, vo, gvo = pl.pallas_call(
        functools.partial(_mix_prompt_kernel, tm=tm, tiles_per_seq=tiles_per_seq, n_tiles=n_tiles),
        grid=(n_tiles + 2,),
        in_specs=[pl.BlockSpec((1, tm, D_MODEL), lambda i: (proj_tile(i), 0, 0)),
                  pl.BlockSpec((1, tm, D_MODEL), lambda i: (out_tile(i), 0, 0)),
                  _resident((tm, LANES)), _resident((tm, LANES)), _resident((tiles_per_seq, LANES)),
                  _resident((tiles_per_seq, LANES)), _resident((1, D_MODEL)), _resident((D_MODEL, D_IN)),
                  pl.BlockSpec(memory_space=pltpu.SMEM), _resident((1, D_GMLP)),
                  _resident((G_HEADS // 2, CHUNK, 2 * CHUNK)), _resident((CHUNK, D_GMLP)), _resident((1, Q_W)),
                  _resident((1, D_GMLP)), _resident((D_MODEL, D_MODEL))],
        out_specs=[pl.BlockSpec((1, tm, D_MODEL), lambda i: (out_tile(i), 0, 0)), last(KV_W), last(KV_W),
                   last(D_GMLP)],
        out_shape=[jax.ShapeDtypeStruct((n_tiles, tm, D_MODEL), F32), jax.ShapeDtypeStruct((b, KV_W, WINDOW), F32),
                   jax.ShapeDtypeStruct((b, KV_W, WINDOW), F32), jax.ShapeDtypeStruct((b, D_GMLP, CHUNK), F32)],
        scratch_shapes=[pltpu.VMEM((2, tm, Q_W), BF16), pltpu.VMEM((2, tm + WINDOW, KV_W), BF16),
                        pltpu.VMEM((2, tm + WINDOW, KV_W), BF16), pltpu.VMEM((2, tm, D_GMLP), F32),
                        pltpu.VMEM((2, tm, D_GMLP), BF16), pltpu.VMEM((2, tm, D_MODEL), BF16)],
        compiler_params=pltpu.CompilerParams(dimension_semantics=("arbitrary",), vmem_limit_bytes=VMEM_LIMIT_BYTES),
        name="mix_prompt",
    )(x_tiles, x_tiles, cos_off, sin_off, cos_base, sin_base, gmix, w_in, sinks, gvg, wcat, bias_full, ga, gg, w_out)
    return out.reshape(b, s, D_MODEL), ko, vo, gvo


def _mix_sample_kernel(x_ref, ckt_ref, cvt_ref, cos_ref, sin_ref, gmix_ref, win_ref, sinks_ref, gvg_ref, coef_ref,
                       biasr_ref, ga_ref, gg_ref, wout_ref, o_ref, kot_ref, vot_ref, gvo_ref, q_s, k_s, v_s, ya_s,
                       yg_s, *, t_new, w_buf):
    step = pl.program_id(0)
    n_tok = x_ref.shape[0]
    step_seqs = SEQ_GROUP * GROUPS_PER_STEP
    step_rows = step_seqs * t_new
    grp_rows = SEQ_GROUP * t_new
    grp_keys = SEQ_GROUP * w_buf
    sub = 8

    @pl.when(step == 0)
    def _():
        tile = lambda tab: jnp.broadcast_to(tab[None], (n_tok // sub, sub, LANES)).reshape(n_tok, LANES)
        q, k, v, u, gvn = _in_proj(x_ref[...], gmix_ref[...], win_ref[...], tile(cos_ref[...]),
                                   tile(_sign_sin(sin_ref[...])), gvg_ref[...])
        q_s[...] = q
        k_s[...] = k
        v_s[...] = v
        gvo_ref[...] = gvn
        g3 = gvn.reshape(n_tok // sub, sub, D_GMLP)
        trow = lax.broadcasted_iota(jnp.int32, (1, sub, D_GMLP), 1) & (t_new - 1)
        mixed = biasr_ref[...][None] + coef_ref[0][None] * g3
        for d in range(1, t_new):
            shifted = jnp.where(trow >= d, pltpu.roll(g3, d, 1), 0.0)
            mixed = mixed + coef_ref[d][None] * shifted
        yg_s[...] = (u.reshape(n_tok // sub, sub, D_GMLP) * mixed).reshape(n_tok, D_GMLP)

    row0 = pl.multiple_of(step * step_rows, step_rows)
    q_step = q_s[pl.ds(row0, step_rows), :]
    kn = k_s[pl.ds(row0, step_rows), :]
    vn = v_s[pl.ds(row0, step_rows), :]

    knt = kn.T
    vnt = vn.T
    tail = lax.broadcasted_iota(jnp.int32, (KV_W, w_buf), 1) >= w_buf - t_new
    for b in range(step_seqs):
        shift = (w_buf - t_new - t_new * b) % w_buf
        kot_ref[b] = jnp.where(tail, pltpu.roll(knt, shift, 1) if shift else knt,
                               pltpu.roll(ckt_ref[b], w_buf - t_new, 1))
        vot_ref[b] = jnp.where(tail, pltpu.roll(vnt, shift, 1) if shift else vnt,
                               pltpu.roll(cvt_ref[b], w_buf - t_new, 1))

    knb = kn.astype(BF16)
    vnb = vn.astype(BF16)

    n_rows = GQA * grp_rows
    shift_t = t_new.bit_length() - 1
    shift_w = w_buf.bit_length() - 1
    r = lax.broadcasted_iota(jnp.int32, (n_rows, grp_keys), 0)
    c = lax.broadcasted_iota(jnp.int32, (n_rows, grp_keys), 1)
    mask_c = ((c >> shift_w) == ((r & (grp_rows - 1)) >> shift_t)) & ((c & (w_buf - 1)) > (r & (t_new - 1)))
    r2 = lax.broadcasted_iota(jnp.int32, (n_rows, grp_rows), 0)
    c2 = lax.broadcasted_iota(jnp.int32, (n_rows, grp_rows), 1)
    mask_n = ((c2 >> shift_t) == ((r2 & (grp_rows - 1)) >> shift_t)) & ((c2 & (t_new - 1)) <= (r2 & (t_new - 1)))
    row_head = lax.broadcasted_iota(jnp.int32, (n_rows, 1), 0) >> (grp_rows.bit_length() - 1)

    for grp in range(GROUPS_PER_STEP):
        rows = slice(grp * grp_rows, (grp + 1) * grp_rows)
        seqs = range(grp * SEQ_GROUP, (grp + 1) * SEQ_GROUP)
        for kv in range(N_KV):
            heads = [kv * GQA + i for i in range(GQA)]
            lanes = slice(HEAD_DIM * kv, HEAD_DIM * (kv + 1))
            kt = jnp.concatenate([ckt_ref[b, lanes, :] for b in seqs], axis=1).astype(BF16)
            vt = jnp.concatenate([cvt_ref[b, lanes, :] for b in seqs], axis=1).astype(BF16)
            qs = jnp.concatenate([q_step[rows, HEAD_DIM * hd:HEAD_DIM * (hd + 1)] for hd in heads], axis=0)
            s_c = jnp.where(mask_c, jnp.dot(qs, kt, preferred_element_type=F32), -jnp.inf)
            s_n = jnp.where(mask_n, lax.dot_general(qs, knb[rows, lanes], _NT, preferred_element_type=F32), -jnp.inf)
            sink = jnp.full((n_rows, 1), sinks_ref[heads[0]], F32)
            for i in range(1, GQA):
                sink = jnp.where(row_head == i, sinks_ref[heads[i]], sink)
            m = jnp.maximum(jnp.maximum(jnp.max(s_c, axis=1, keepdims=True), jnp.max(s_n, axis=1, keepdims=True)),
                            sink)
            p_c = jnp.exp(s_c - m)
            p_n = jnp.exp(s_n - m)
            den = jnp.sum(p_c, axis=1, keepdims=True) + jnp.sum(p_n, axis=1, keepdims=True) + jnp.exp(sink - m)
            o = (lax.dot_general(p_c.astype(BF16), vt, _NT, preferred_element_type=F32)
                 + jnp.dot(p_n.astype(BF16), vnb[rows, lanes], preferred_element_type=F32)) / den
            for i, hd in enumerate(heads):
                ya_s[pl.ds(row0 + grp * grp_rows, grp_rows), HEAD_DIM * hd:HEAD_DIM * (hd + 1)] = (
                    o[i * grp_rows:(i + 1) * grp_rows])

    @pl.when(step == pl.num_programs(0) - 1)
    def _():
        o_ref[...] = _out_proj(x_ref[...], ya_s[...], yg_s[...], ga_ref[...], gg_ref[...], wout_ref[...])


def _mix_sample(x, cache_kt, cache_vt, cos, sin, gmix, w_in, sinks, gvg, coef, bias_rows, ga, gg, w_out, t_new):
    n_tok = x.shape[0]
    n_seq, _, w_buf = cache_kt.shape
    step_seqs = SEQ_GROUP * GROUPS_PER_STEP
    assert n_seq % step_seqs == 0 and n_tok == n_seq * t_new
    assert t_new & (t_new - 1) == 0 and w_buf & (w_buf - 1) == 0 and 8 % t_new == 0
    assert step_seqs * t_new == w_buf == LANES
    cache_spec = pl.BlockSpec((step_seqs, KV_W, w_buf), lambda i: (i, 0, 0))
    return pl.pallas_call(
        functools.partial(_mix_sample_kernel, t_new=t_new, w_buf=w_buf),
        grid=(n_seq // step_seqs,),
        in_specs=[_resident((n_tok, D_MODEL)), cache_spec, cache_spec, _resident((8, LANES)),
                  _resident((8, LANES)), _resident((1, D_MODEL)), _resident((D_MODEL, D_IN)),
                  pl.BlockSpec(memory_space=pltpu.SMEM), _resident((1, D_GMLP)), _resident((t_new, 8, D_GMLP)),
                  _resident((8, D_GMLP)), _resident((1, Q_W)), _resident((1, D_GMLP)),
                  _resident((D_MODEL, D_MODEL))],
        out_specs=[pl.BlockSpec((n_tok, D_MODEL), lambda i: (0, 0)), cache_spec, cache_spec,
                   pl.BlockSpec((n_tok, D_GMLP), lambda i: (0, 0))],
        out_shape=[jax.ShapeDtypeStruct((n_tok, D_MODEL), F32), jax.ShapeDtypeStruct(cache_kt.shape, F32),
                   jax.ShapeDtypeStruct(cache_vt.shape, F32), jax.ShapeDtypeStruct((n_tok, D_GMLP), F32)],
        scratch_shapes=[pltpu.VMEM((n_tok, Q_W), BF16), pltpu.VMEM((n_tok, KV_W), F32), pltpu.VMEM((n_tok, KV_W), F32),
                        pltpu.VMEM((n_tok, Q_W), F32), pltpu.VMEM((n_tok, D_GMLP), F32)],
        compiler_params=pltpu.CompilerParams(dimension_semantics=("arbitrary",), vmem_limit_bytes=VMEM_LIMIT_BYTES),
        name="mix_sample",
    )(x, cache_kt, cache_vt, cos, sin, gmix, w_in, sinks, gvg, coef, bias_rows, ga, gg, w_out)


def _rope_tables(pos):
    inv_freq = ROPE_THETA ** (-jnp.arange(0, HEAD_DIM, 2, dtype=F32) / HEAD_DIM)
    ang = pos.astype(F32)[:, None] * inv_freq[None, :]
    return jnp.tile(jnp.cos(ang), (1, 4)), jnp.tile(jnp.sin(ang), (1, 4))


def kernel(x_prompt, x_sample, cache_k_win, cache_v_win, norm_ffn1, ffn1_gate, ffn1_up, ffn1_down, norm_mix, w_in,
           attn_sinks, gmlp_v_norm, gmlp_w_s, gmlp_b_s, norm_attn_out, norm_gmlp_out, w_out, norm_ffn2, ffn2_gate,
           ffn2_up, ffn2_down, norm_final):
    depth = norm_ffn1.shape[0]
    b, s, _ = x_prompt.shape
    bd, t_new, _ = x_sample.shape
    w_buf = cache_k_win.shape[2]

    cos_s, sin_s = _rope_tables(PAST_LEN + jnp.arange(t_new, dtype=jnp.int32))
    cos_s, sin_s = jnp.tile(cos_s, (8 // t_new, 1)), jnp.tile(sin_s, (8 // t_new, 1))

    hp = x_prompt.reshape(b * s, D_MODEL)
    hs = x_sample.reshape(bd * t_new, D_MODEL)
    outs = [[] for _ in range(6)]
    ffn1_w = [ffn1_gate[0], ffn1_up[0], ffn1_down[0]]
    for l in range(depth):
        last = l == depth - 1
        row = lambda a: a[l].reshape(1, -1)

        wcat = gmlp_w_s[l].reshape(G_HEADS // 2, 2, CHUNK, CHUNK).transpose(0, 2, 1, 3).reshape(
            G_HEADS // 2, CHUNK, 2 * CHUNK)
        bias_full = jnp.repeat(gmlp_b_s[l].T, HEAD_DIM, axis=1)
        ws_small = gmlp_w_s[l][:, :t_new, :t_new]
        coef = jnp.stack([
            jnp.stack([ws_small[:, t, t - d] if t >= d else jnp.zeros((G_HEADS,), F32) for t in range(t_new)])
            for d in range(t_new)])
        coef = jnp.tile(jnp.repeat(coef, HEAD_DIM, axis=2), (1, 8 // t_new, 1))
        bias_rows = jnp.tile(jnp.repeat(gmlp_b_s[l][:, :t_new].T, HEAD_DIM, axis=1), (8 // t_new, 1))

        hp, hs, (w_in_b, w_out_b, wg2, wu2, wd2) = _ffn_half(
            hp, hs, norm_ffn1[l], *ffn1_w, cast=(w_in[l], w_out[l], ffn2_gate[l], ffn2_up[l], ffn2_down[l]))

        hp, kpt, vpt, gvpt = _mix_prompt(hp.reshape(b, s, D_MODEL), row(norm_mix), w_in_b, attn_sinks[l],
                                         row(gmlp_v_norm), wcat, bias_full, row(norm_attn_out), row(norm_gmlp_out),
                                         w_out_b)
        to_t = lambda c: c.transpose(0, 2, 3, 1).reshape(bd, KV_W, w_buf)
        hs, kst, vst, gvs = _mix_sample(hs, to_t(cache_k_win[l]), to_t(cache_v_win[l]), cos_s, sin_s, row(norm_mix),
                                        w_in_b, attn_sinks[l], row(gmlp_v_norm), coef, bias_rows, row(norm_attn_out),
                                        row(norm_gmlp_out), w_out_b, t_new)
        from_t = lambda c: c.reshape(bd, N_KV, HEAD_DIM, w_buf).transpose(0, 3, 1, 2)

        next_ffn1 = () if last else (ffn1_gate[l + 1], ffn1_up[l + 1], ffn1_down[l + 1])
        hp, hs, ffn1_w = _ffn_half(hp.reshape(b * s, D_MODEL), hs, norm_ffn2[l], wg2, wu2, wd2,
                                   gf=norm_final if last else None, cast=next_ffn1)

        outs[0].append(kpt.reshape(b, N_KV, HEAD_DIM, WINDOW).transpose(0, 3, 1, 2))
        outs[1].append(vpt.reshape(b, N_KV, HEAD_DIM, WINDOW).transpose(0, 3, 1, 2))
        outs[2].append(from_t(kst))
        outs[3].append(from_t(vst))
        outs[4].append(gvpt.reshape(b, G_HEADS, D_GMLP // G_HEADS, CHUNK).transpose(0, 3, 1, 2))
        outs[5].append(gvs.reshape(bd, t_new, G_HEADS, D_GMLP // G_HEADS))

    return (hp.reshape(b, s, D_MODEL), hs.reshape(bd, t_new, D_MODEL)) + tuple(jnp.stack(o) for o in outs)
```

```python
import functools

import jax
import jax.numpy as jnp
from jax import lax
from jax.experimental import pallas as pl
from jax.experimental.pallas import tpu as pltpu

F32 = jnp.float32
BF16 = jnp.bfloat16

D_MODEL = 1024
D_FF = 2816
HEAD_DIM = 64
N_HEADS = 8
N_KV = 2
GQA = N_HEADS // N_KV
WINDOW = 128
CHUNK = 128
G_HEADS = 8
Q_W = N_HEADS * HEAD_DIM
KV_W = N_KV * HEAD_DIM
D_GMLP = 512
D_IN = Q_W + 2 * KV_W + 2 * D_GMLP
K_OFF = Q_W
V_OFF = K_OFF + KV_W
U_OFF = V_OFF + KV_W
GV_OFF = U_OFF + D_GMLP
ROPE_THETA = 10000.0
PAST_LEN = 16384
EPS = 1e-6
Q_SCALE = HEAD_DIM ** -0.5
LANES = 128
BF16_SUBLANES = 16

TOKEN_TILE = 512
FF_CHUNK = 256
STAGE_ROWS = 64
STAGE_SLOTS = 8
SEQ_GROUP = 8
GROUPS_PER_STEP = 4
VMEM_LIMIT_BYTES = 56 * 1024 * 1024

_NT = (((1,), (1,)), ((), ()))


def _rms(x, g):
    ms = jnp.mean(x * x, axis=-1, keepdims=True)
    return (x * lax.rsqrt(ms + EPS)) * g


def _resident(shape):
    zeros = (0,) * len(shape)
    return pl.BlockSpec(shape, lambda *_: zeros, pipeline_mode=pl.Buffered(1))


def _round_rows_to_bf16(src_hbm, dst_ref, stage_ref, sem):
    n_slots, rows, _ = stage_ref.shape
    n_blocks = src_hbm.shape[0] // rows
    assert n_blocks >= n_slots

    def fetch(i, slot):
        return pltpu.make_async_copy(src_hbm.at[pl.ds(i * rows, rows), :], stage_ref.at[slot], sem.at[slot])

    for i in range(n_slots - 1):
        fetch(i, i).start()

    def body(i, carry):
        slot = i % n_slots

        @pl.when(i + n_slots - 1 < n_blocks)
        def _():
            fetch(i + n_slots - 1, (i + n_slots - 1) % n_slots).start()

        fetch(i, slot).wait()
        dst_ref[pl.ds(pl.multiple_of(i * rows, rows), rows), :] = stage_ref[slot].astype(BF16)
        return carry

    lax.fori_loop(0, n_blocks, body, 0)


def _ffn_kernel(*refs, final_norm, n_cast, f32_weights):
    refs = iter(refs)
    xp_ref, xn_ref, xs_ref, g_ref, wg_in, wu_in, wd_in = (next(refs) for _ in range(7))
    gf_ref = next(refs) if final_norm else None
    cast_in = [next(refs) for _ in range(n_cast)]
    yp_ref, ys_ref = next(refs), next(refs)
    cast_out = [next(refs) for _ in range(n_cast)]
    h_ref, act0_ref, act_ref = next(refs), next(refs), next(refs)
    if f32_weights:
        wg_ref, wu_ref, wd_ref, wide_stage, narrow_stage, wide_sem, narrow_sem = (next(refs) for _ in range(7))
    else:
        wg_ref, wu_ref, wd_ref = wg_in, wu_in, wd_in

    def gate_up(h, c):
        sl = slice(c * FF_CHUNK, (c + 1) * FF_CHUNK)
        gate = jnp.dot(h, wg_ref[:, sl], preferred_element_type=F32)
        up = jnp.dot(h, wu_ref[:, sl], preferred_element_type=F32)
        return (gate * jax.nn.sigmoid(gate) * up).astype(BF16)

    on_sample = pl.program_id(0) == 0

    @pl.when(on_sample)
    def _():
        if f32_weights:
            _round_rows_to_bf16(wg_in, wg_ref, wide_stage, wide_sem)
            _round_rows_to_bf16(wu_in, wu_ref, wide_stage, wide_sem)
            _round_rows_to_bf16(wd_in, wd_ref, narrow_stage, narrow_sem)
        h0 = _rms(xs_ref[...], g_ref[...]).astype(BF16)
        h_ref[...] = h0
        act0_ref[...] = gate_up(h0, 0)

    for c in range(1, D_FF // FF_CHUNK):
        act_ref[:, (c - 1) * FF_CHUNK:c * FF_CHUNK] = gate_up(h_ref[...], c)
    down0 = jnp.dot(act0_ref[...], wd_ref[0:FF_CHUNK, :], preferred_element_type=F32)
    hn = _rms(xn_ref[...], g_ref[...]).astype(BF16)
    h_ref[...] = hn
    act0_ref[...] = gate_up(hn, 0)
    x = jnp.where(on_sample, xs_ref[...], xp_ref[...])
    y = x + 0.5 * (down0 + jnp.dot(act_ref[...], wd_ref[FF_CHUNK:, :], preferred_element_type=F32))
    if final_norm:
        y = _rms(y, gf_ref[...])
    yp_ref[...] = y

    @pl.when(on_sample)
    def _():
        ys_ref[...] = yp_ref[...]

    for src, dst in zip(cast_in, cast_out):
        dst[...] = src[...].astype(BF16)


def _cast_row_blocks(rows, n_steps):
    return max(d for d in range(1, n_steps + 1) if rows % d == 0 and (rows // d) % BF16_SUBLANES == 0)


def _ffn_half(xp, xs, g, wg, wu, wd, gf=None, cast=()):
    tm = TOKEN_TILE
    f32_weights = wg.dtype == F32
    assert all(w.dtype == wg.dtype for w in (wu, wd))
    n = xp.shape[0]
    assert n % tm == 0 and xs.shape[0] == tm
    n_tiles = n // tm
    prompt_spec = pl.BlockSpec((tm, D_MODEL), lambda i: (jnp.maximum(i - 1, 0), 0))
    next_spec = pl.BlockSpec((tm, D_MODEL), lambda i: (jnp.minimum(i, n_tiles - 1), 0))
    sample_in = _resident((tm, D_MODEL))
    sample_out = pl.BlockSpec((tm, D_MODEL), lambda i: (0, 0))
    weight_specs = ([pl.BlockSpec(memory_space=pl.ANY)] * 3 if f32_weights else
                    [_resident((D_MODEL, D_FF)), _resident((D_MODEL, D_FF)), _resident((D_FF, D_MODEL))])
    in_specs = [prompt_spec, next_spec, sample_in, _resident((1, D_MODEL))] + weight_specs
    args = [xp, xp, xs, g.reshape(1, D_MODEL), wg, wu, wd]
    if gf is not None:
        in_specs.append(_resident((1, D_MODEL)))
        args.append(gf.reshape(1, D_MODEL))
    out_specs = [prompt_spec, sample_out]
    out_shape = [jax.ShapeDtypeStruct((n, D_MODEL), F32), jax.ShapeDtypeStruct((tm, D_MODEL), F32)]
    cast_specs = []
    for w in cast:
        rows, cols = w.shape
        nb = _cast_row_blocks(rows, n_tiles)
        cast_specs.append(pl.BlockSpec((rows // nb, cols), lambda i, nb=nb: (jnp.minimum(i, nb - 1), 0)))
        out_shape.append(jax.ShapeDtypeStruct(w.shape, BF16))
    outs = pl.pallas_call(
        functools.partial(_ffn_kernel, final_norm=gf is not None, n_cast=len(cast), f32_weights=f32_weights),
        grid=(n_tiles + 1,),
        in_specs=in_specs + cast_specs,
        out_specs=out_specs + cast_specs,
        out_shape=out_shape,
        scratch_shapes=[pltpu.VMEM((tm, D_MODEL), BF16), pltpu.VMEM((tm, FF_CHUNK), BF16),
                        pltpu.VMEM((tm, D_FF - FF_CHUNK), BF16)] + ([
                            pltpu.VMEM((D_MODEL, D_FF), BF16), pltpu.VMEM((D_MODEL, D_FF), BF16),
                            pltpu.VMEM((D_FF, D_MODEL), BF16), pltpu.VMEM((STAGE_SLOTS, STAGE_ROWS, D_FF), F32),
                            pltpu.VMEM((STAGE_SLOTS, STAGE_ROWS, D_MODEL), F32),
                            pltpu.SemaphoreType.DMA((STAGE_SLOTS,)), pltpu.SemaphoreType.DMA((STAGE_SLOTS,))]
                        if f32_weights else []),
        compiler_params=pltpu.CompilerParams(dimension_semantics=("arbitrary",), vmem_limit_bytes=VMEM_LIMIT_BYTES),
        name="ffn_final" if gf is not None else "ffn_half",
    )(*args, *cast)
    return outs[0], outs[1], list(outs[2:])


def _first_half(shape):
    return (lax.broadcasted_iota(jnp.int32, shape, 1) & (HEAD_DIM - 1)) < HEAD_DIM // 2


def _sign_sin(sin):
    return jnp.where(_first_half(sin.shape), -sin, sin)


def _rope(xg, cos, sin_signed):
    swapped = jnp.where(_first_half(xg.shape), pltpu.roll(xg, LANES - HEAD_DIM // 2, 1),
                        pltpu.roll(xg, HEAD_DIM // 2, 1))
    return xg * cos + swapped * sin_signed


def _in_proj(x, gmix, w_in, cos, sin_signed, gvn_gain):
    h = _rms(x, gmix).astype(BF16)
    z = jnp.dot(h, w_in, preferred_element_type=F32)
    q = jnp.concatenate(
        [(_rope(z[:, LANES * i:LANES * (i + 1)], cos, sin_signed) * Q_SCALE).astype(BF16) for i in range(Q_W // LANES)],
        axis=1)
    k = _rope(z[:, K_OFF:V_OFF], cos, sin_signed)
    v = z[:, V_OFF:U_OFF]
    u = jax.nn.gelu(z[:, U_OFF:GV_OFF])
    gvn = _rms(jax.nn.gelu(z[:, GV_OFF:]), gvn_gain)
    return q, k, v, u, gvn


def _out_proj(x, ya, yg, ga, gg, w_out):
    cat = jnp.concatenate([_rms(ya, ga).astype(BF16), _rms(yg, gg).astype(BF16)], axis=1)
    return x + jnp.dot(cat, w_out, preferred_element_type=F32)


def _softmax_sink(s, sink):
    m = jnp.maximum(jnp.max(s, axis=1, keepdims=True), sink)
    p = jnp.exp(s - m)
    return p, jnp.sum(p, axis=1, keepdims=True) + jnp.exp(sink - m)


def _mix_prompt_kernel(xa_ref, xc_ref, coff_ref, soff_ref, cbase_ref, sbase_ref, gmix_ref, win_ref, sinks_ref, gvg_ref,
                       wcat_ref, bias_ref, ga_ref, gg_ref, wout_ref, o_ref, ko_ref, vo_ref, gvo_ref, q_s, k_s, v_s, u_s,
                       gv_s, cat_s, *, tm, tiles_per_seq, n_tiles):
    s = pl.program_id(0)

    @pl.when(s == 0)
    def _():
        q_s[1] = jnp.zeros((tm, Q_W), BF16)
        k_s[1] = jnp.zeros((tm + WINDOW, KV_W), BF16)
        v_s[1] = jnp.zeros((tm + WINDOW, KV_W), BF16)
        u_s[1] = jnp.zeros((tm, D_GMLP), F32)
        gv_s[1] = jnp.zeros((tm, D_GMLP), BF16)
        cat_s[0] = jnp.zeros((tm, D_MODEL), BF16)

    for parity in range(2):
        pl.when(s % 2 == parity)(functools.partial(
            _mix_prompt_step, s, parity, xa_ref, xc_ref, coff_ref, soff_ref, cbase_ref, sbase_ref, gmix_ref, win_ref,
            sinks_ref, gvg_ref, wcat_ref, bias_ref, ga_ref, gg_ref, wout_ref, o_ref, ko_ref, vo_ref, gvo_ref, q_s, k_s,
            v_s, u_s, gv_s, cat_s, tm=tm, tiles_per_seq=tiles_per_seq, n_tiles=n_tiles))


def _mix_prompt_step(s, cur, xa_ref, xc_ref, coff_ref, soff_ref, cbase_ref, sbase_ref, gmix_ref, win_ref, sinks_ref,
                     gvg_ref, wcat_ref, bias_ref, ga_ref, gg_ref, wout_ref, o_ref, ko_ref, vo_ref, gvo_ref, q_s, k_s, v_s,
                     u_s, gv_s, cat_s, *, tm, tiles_per_seq, n_tiles):
    oth = 1 - cur

    h = _rms(xa_ref[0], gmix_ref[...]).astype(BF16)
    tile_in_seq = jnp.minimum(s, n_tiles - 1) % tiles_per_seq
    cb = cbase_ref[pl.ds(tile_in_seq, 1), :]
    sb = sbase_ref[pl.ds(tile_in_seq, 1), :]
    cos = cb * coff_ref[...] - sb * soff_ref[...]
    sin = _sign_sin(sb * coff_ref[...] + cb * soff_ref[...])

    def proj_q():
        z = jnp.dot(h, win_ref[:, 0:Q_W], preferred_element_type=F32)
        for i in range(Q_W // LANES):
            q_s[cur, :, LANES * i:LANES * (i + 1)] = (
                _rope(z[:, LANES * i:LANES * (i + 1)], cos, sin) * Q_SCALE).astype(BF16)

    def proj_kv():
        z = jnp.dot(h, win_ref[:, K_OFF:U_OFF], preferred_element_type=F32)
        k = _rope(z[:, 0:KV_W], cos, sin)
        v = z[:, KV_W:]
        k_s[cur, 0:WINDOW, :] = k_s[oth, tm:tm + WINDOW, :]
        v_s[cur, 0:WINDOW, :] = v_s[oth, tm:tm + WINDOW, :]
        k_s[cur, WINDOW:, :] = k.astype(BF16)
        v_s[cur, WINDOW:, :] = v.astype(BF16)
        ko_ref[0] = k[tm - WINDOW:].T
        vo_ref[0] = v[tm - WINDOW:].T

    def proj_u():
        u_s[cur] = jax.nn.gelu(jnp.dot(h, win_ref[:, U_OFF:GV_OFF], preferred_element_type=F32))

    def proj_gv():
        gvn = _rms(jax.nn.gelu(jnp.dot(h, win_ref[:, GV_OFF:], preferred_element_type=F32)), gvg_ref[...])
        gv_s[cur] = gvn.astype(BF16)
        gvo_ref[0] = gvn[tm - CHUNK:].T

    def out_half(c):
        cols = slice(c * (D_MODEL // 2), (c + 1) * (D_MODEL // 2))

        def run():
            o_ref[0, :, cols] = xc_ref[0, :, cols] + jnp.dot(cat_s[cur], wout_ref[:, cols], preferred_element_type=F32)
        return run

    mid_fill = [[out_half(0)], [proj_q], [proj_kv, proj_u], [proj_gv]]
    end_fill = [[], [], [], [out_half(1)]]
    assert len(mid_fill) == len(end_fill) == tm // WINDOW

    wrow = lax.broadcasted_iota(jnp.int32, (CHUNK, 2 * CHUNK), 0)
    wcol = lax.broadcasted_iota(jnp.int32, (CHUNK, 2 * CHUNK), 1) & (CHUNK - 1)
    wmix = [jnp.where(wcol <= wrow, wcat_ref[p], 0.0).astype(BF16) for p in range(G_HEADS // 2)]
    lane = lax.broadcasted_iota(jnp.int32, (CHUNK, LANES), 1)
    low_head = lane < HEAD_DIM

    qi = lax.broadcasted_iota(jnp.int32, (WINDOW, 2 * WINDOW), 0)
    sj = lax.broadcasted_iota(jnp.int32, (WINDOW, 2 * WINDOW), 1)
    dist = WINDOW + qi - sj
    band = (dist >= 0) & (dist < WINDOW)
    first_lo = jnp.where((s + tiles_per_seq - 1) % tiles_per_seq == 0, WINDOW, 0)

    for j in range(tm // WINDOW):
        rows = slice(j * WINDOW, (j + 1) * WINDOW)
        mask = band & (sj >= first_lo) if j == 0 else band
        qb = q_s[oth, rows, :]
        scores = []
        for kv in range(N_KV):
            qs = jnp.concatenate(
                [qb[:, HEAD_DIM * hd:HEAD_DIM * (hd + 1)] for hd in range(kv * GQA, (kv + 1) * GQA)], axis=0)
            kb = k_s[oth, j * WINDOW:(j + 2) * WINDOW, HEAD_DIM * kv:HEAD_DIM * (kv + 1)]
            scores.append(lax.dot_general(qs, kb, _NT, preferred_element_type=F32))
        for run in mid_fill[j]:
            run()
        outs, dens = [], []
        for kv in range(N_KV):
            vb = v_s[oth, j * WINDOW:(j + 2) * WINDOW, HEAD_DIM * kv:HEAD_DIM * (kv + 1)]
            ps = []
            for g in range(GQA):
                sg = jnp.where(mask, scores[kv][g * WINDOW:(g + 1) * WINDOW], -jnp.inf)
                p, den = _softmax_sink(sg, sinks_ref[kv * GQA + g])
                ps.append(p.astype(BF16))
                dens.append(den)
            outs.append(jnp.dot(jnp.concatenate(ps, axis=0), vb, preferred_element_type=F32))
        mixed = []
        for p in range(G_HEADS // 2):
            r = gv_s[oth, rows, LANES * p:LANES * (p + 1)]
            zero = jnp.zeros_like(r)
            rhs = jnp.concatenate([jnp.where(low_head, r, zero), jnp.where(low_head, zero, r)], axis=0)
            mixed.append(jnp.dot(wmix[p], rhs, preferred_element_type=F32))
        for run in end_fill[j]:
            run()
        ya = [outs[hd // GQA][(hd % GQA) * WINDOW:(hd % GQA + 1) * WINDOW] / dens[hd] for hd in range(N_HEADS)]
        cat_s[oth, rows, 0:Q_W] = _rms(jnp.concatenate(ya, axis=1), ga_ref[...]).astype(BF16)
        yg = u_s[oth, rows, :] * (jnp.concatenate(mixed, axis=1) + bias_ref[...])
        cat_s[oth, rows, Q_W:] = _rms(yg, gg_ref[...]).astype(BF16)


def _mix_prompt(x, gmix, w_in, sinks, gvg, wcat, bias_full, ga, gg, w_out):
    b, s, _ = x.shape
    tm = TOKEN_TILE
    assert s % tm == 0 and tm % WINDOW == 0
    tiles_per_seq = s // tm
    n_tiles = b * tiles_per_seq
    cos_off, sin_off = _rope_tables(jnp.arange(tm, dtype=jnp.int32))
    cos_base, sin_base = _rope_tables(tm * jnp.arange(tiles_per_seq, dtype=jnp.int32))
    proj_tile = lambda i: jnp.minimum(i, n_tiles - 1)
    out_tile = lambda i: jnp.maximum(i - 2, 0)
    x_tiles = x.reshape(n_tiles, tm, D_MODEL)
    last = lambda width: pl.BlockSpec((1, width, WINDOW), lambda i: (proj_tile(i) // tiles_per_seq, 0, 0))
    out, ko, vo, gvo = pl.pallas_call(
        functools.partial(_mix_prompt_kernel, tm=tm, tiles_per_seq=tiles_per_seq, n_tiles=n_tiles),
        grid=(n_tiles + 2,),
        in_specs=[pl.BlockSpec((1, tm, D_MODEL), lambda i: (proj_tile(i), 0, 0)),
                  pl.BlockSpec((1, tm, D_MODEL), lambda i: (out_tile(i), 0, 0)),
                  _resident((tm, LANES)), _resident((tm, LANES)), _resident((tiles_per_seq, LANES)),
                  _resident((tiles_per_seq, LANES)), _resident((1, D_MODEL)), _resident((D_MODEL, D_IN)),
                  pl.BlockSpec(memory_space=pltpu.SMEM), _resident((1, D_GMLP)),
                  _resident((G_HEADS // 2, CHUNK, 2 * CHUNK)), _resident((CHUNK, D_GMLP)), _resident((1, Q_W)),
                  _resident((1, D_GMLP)), _resident((D_MODEL, D_MODEL))],
        out_specs=[pl.BlockSpec((1, tm, D_MODEL), lambda i: (out_tile(i), 0, 0)), last(KV_W), last(KV_W),
                   last(D_GMLP)],
        out_shape=[jax.ShapeDtypeStruct((n_tiles, tm, D_MODEL), F32), jax.ShapeDtypeStruct((b, KV_W, WINDOW), F32),
                   jax.ShapeDtypeStruct((b, KV_W, WINDOW), F32), jax.ShapeDtypeStruct((b, D_GMLP, CHUNK), F32)],
        scratch_shapes=[pltpu.VMEM((2, tm, Q_W), BF16), pltpu.VMEM((2, tm + WINDOW, KV_W), BF16),
                        pltpu.VMEM((2, tm + WINDOW, KV_W), BF16), pltpu.VMEM((2, tm, D_GMLP), F32),
                        pltpu.VMEM((2, tm, D_GMLP), BF16), pltpu.VMEM((2, tm, D_MODEL), BF16)],
        compiler_params=pltpu.CompilerParams(dimension_semantics=("arbitrary",), vmem_limit_bytes=VMEM_LIMIT_BYTES),
        name="mix_prompt",
    )(x_tiles, x_tiles, cos_off, sin_off, cos_base, sin_base, gmix, w_in, sinks, gvg, wcat, bias_full, ga, gg, w_out)
    return out.reshape(b, s, D_MODEL), ko, vo, gvo


def _mix_sample_kernel(x_ref, ckt_ref, cvt_ref, cos_ref, sin_ref, gmix_ref, win_ref, sinks_ref, gvg_ref, coef_ref,
                       biasr_ref, ga_ref, gg_ref, wout_ref, o_ref, kot_ref, vot_ref, gvo_ref, q_s, k_s, v_s, ya_s,
                       yg_s, *, t_new, w_buf):
    step = pl.program_id(0)
    n_tok = x_ref.shape[0]
    step_seqs = SEQ_GROUP * GROUPS_PER_STEP
    step_rows = step_seqs * t_new
    grp_rows = SEQ_GROUP * t_new
    grp_keys = SEQ_GROUP * w_buf
    sub = 8

    @pl.when(step == 0)
    def _():
        tile = lambda tab: jnp.broadcast_to(tab[None], (n_tok // sub, sub, LANES)).reshape(n_tok, LANES)
        q, k, v, u, gvn = _in_proj(x_ref[...], gmix_ref[...], win_ref[...], tile(cos_ref[...]),
                                   tile(_sign_sin(sin_ref[...])), gvg_ref[...])
        q_s[...] = q
        k_s[...] = k
        v_s[...] = v
        gvo_ref[...] = gvn
        g3 = gvn.reshape(n_tok // sub, sub, D_GMLP)
        trow = lax.broadcasted_iota(jnp.int32, (1, sub, D_GMLP), 1) & (t_new - 1)
        mixed = biasr_ref[...][None] + coef_ref[0][None] * g3
        for d in range(1, t_new):
            shifted = jnp.where(trow >= d, pltpu.roll(g3, d, 1), 0.0)
            mixed = mixed + coef_ref[d][None] * shifted
        yg_s[...] = (u.reshape(n_tok // sub, sub, D_GMLP) * mixed).reshape(n_tok, D_GMLP)

    row0 = pl.multiple_of(step * step_rows, step_rows)
    q_step = q_s[pl.ds(row0, step_rows), :]
    kn = k_s[pl.ds(row0, step_rows), :]
    vn = v_s[pl.ds(row0, step_rows), :]

    knt = kn.T
    vnt = vn.T
    tail = lax.broadcasted_iota(jnp.int32, (KV_W, w_buf), 1) >= w_buf - t_new
    for b in range(step_seqs):
        shift = (w_buf - t_new - t_new * b) % w_buf
        kot_ref[b] = jnp.where(tail, pltpu.roll(knt, shift, 1) if shift else knt,
                               pltpu.roll(ckt_ref[b], w_buf - t_new, 1))
        vot_ref[b] = jnp.where(tail, pltpu.roll(vnt, shift, 1) if shift else vnt,
                               pltpu.roll(cvt_ref[b], w_buf - t_new, 1))

    knb = kn.astype(BF16)
    vnb = vn.astype(BF16)

    n_rows = GQA * grp_rows
    shift_t = t_new.bit_length() - 1
    shift_w = w_buf.bit_length() - 1
    r = lax.broadcasted_iota(jnp.int32, (n_rows, grp_keys), 0)
    c = lax.broadcasted_iota(jnp.int32, (n_rows, grp_keys), 1)
    mask_c = ((c >> shift_w) == ((r & (grp_rows - 1)) >> shift_t)) & ((c & (w_buf - 1)) > (r & (t_new - 1)))
    r2 = lax.broadcasted_iota(jnp.int32, (n_rows, grp_rows), 0)
    c2 = lax.broadcasted_iota(jnp.int32, (n_rows, grp_rows), 1)
    mask_n = ((c2 >> shift_t) == ((r2 & (grp_rows - 1)) >> shift_t)) & ((c2 & (t_new - 1)) <= (r2 & (t_new - 1)))
    row_head = lax.broadcasted_iota(jnp.int32, (n_rows, 1), 0) >> (grp_rows.bit_length() - 1)

    for grp in range(GROUPS_PER_STEP):
        rows = slice(grp * grp_rows, (grp + 1) * grp_rows)
        seqs = range(grp * SEQ_GROUP, (grp + 1) * SEQ_GROUP)
        for kv in range(N_KV):
            heads = [kv * GQA + i for i in range(GQA)]
            lanes = slice(HEAD_DIM * kv, HEAD_DIM * (kv + 1))
            kt = jnp.concatenate([ckt_ref[b, lanes, :] for b in seqs], axis=1).astype(BF16)
            vt = jnp.concatenate([cvt_ref[b, lanes, :] for b in seqs], axis=1).astype(BF16)
            qs = jnp.concatenate([q_step[rows, HEAD_DIM * hd:HEAD_DIM * (hd + 1)] for hd in heads], axis=0)
            s_c = jnp.where(mask_c, jnp.dot(qs, kt, preferred_element_type=F32), -jnp.inf)
            s_n = jnp.where(mask_n, lax.dot_general(qs, knb[rows, lanes], _NT, preferred_element_type=F32), -jnp.inf)
            sink = jnp.full((n_rows, 1), sinks_ref[heads[0]], F32)
            for i in range(1, GQA):
                sink = jnp.where(row_head == i, sinks_ref[heads[i]], sink)
            m = jnp.maximum(jnp.maximum(jnp.max(s_c, axis=1, keepdims=True), jnp.max(s_n, axis=1, keepdims=True)),
                            sink)
            p_c = jnp.exp(s_c - m)
            p_n = jnp.exp(s_n - m)
            den = jnp.sum(p_c, axis=1, keepdims=True) + jnp.sum(p_n, axis=1, keepdims=True) + jnp.exp(sink - m)
            o = (lax.dot_general(p_c.astype(BF16), vt, _NT, preferred_element_type=F32)
                 + jnp.dot(p_n.astype(BF16), vnb[rows, lanes], preferred_element_type=F32)) / den
            for i, hd in enumerate(heads):
                ya_s[pl.ds(row0 + grp * grp_rows, grp_rows), HEAD_DIM * hd:HEAD_DIM * (hd + 1)] = (
                    o[i * grp_rows:(i + 1) * grp_rows])

    @pl.when(step == pl.num_programs(0) - 1)
    def _():
        o_ref[...] = _out_proj(x_ref[...], ya_s[...], yg_s[...], ga_ref[...], gg_ref[...], wout_ref[...])


def _mix_sample(x, cache_kt, cache_vt, cos, sin, gmix, w_in, sinks, gvg, coef, bias_rows, ga, gg, w_out, t_new):
    n_tok = x.shape[0]
    n_seq, _, w_buf = cache_kt.shape
    step_seqs = SEQ_GROUP * GROUPS_PER_STEP
    assert n_seq % step_seqs == 0 and n_tok == n_seq * t_new
    assert t_new & (t_new - 1) == 0 and w_buf & (w_buf - 1) == 0 and 8 % t_new == 0
    assert step_seqs * t_new == w_buf == LANES
    cache_spec = pl.BlockSpec((step_seqs, KV_W, w_buf), lambda i: (i, 0, 0))
    return pl.pallas_call(
        functools.partial(_mix_sample_kernel, t_new=t_new, w_buf=w_buf),
        grid=(n_seq // step_seqs,),
        in_specs=[_resident((n_tok, D_MODEL)), cache_spec, cache_spec, _resident((8, LANES)),
                  _resident((8, LANES)), _resident((1, D_MODEL)), _resident((D_MODEL, D_IN)),
                  pl.BlockSpec(memory_space=pltpu.SMEM), _resident((1, D_GMLP)), _resident((t_new, 8, D_GMLP)),
                  _resident((8, D_GMLP)), _resident((1, Q_W)), _resident((1, D_GMLP)),
                  _resident((D_MODEL, D_MODEL))],
        out_specs=[pl.BlockSpec((n_tok, D_MODEL), lambda i: (0, 0)), cache_spec, cache_spec,
                   pl.BlockSpec((n_tok, D_GMLP), lambda i: (0, 0))],
        out_shape=[jax.ShapeDtypeStruct((n_tok, D_MODEL), F32), jax.ShapeDtypeStruct(cache_kt.shape, F32),
                   jax.ShapeDtypeStruct(cache_vt.shape, F32), jax.ShapeDtypeStruct((n_tok, D_GMLP), F32)],
        scratch_shapes=[pltpu.VMEM((n_tok, Q_W), BF16), pltpu.VMEM((n_tok, KV_W), F32), pltpu.VMEM((n_tok, KV_W), F32),
                        pltpu.VMEM((n_tok, Q_W), F32), pltpu.VMEM((n_tok, D_GMLP), F32)],
        compiler_params=pltpu.CompilerParams(dimension_semantics=("arbitrary",), vmem_limit_bytes=VMEM_LIMIT_BYTES),
        name="mix_sample",
    )(x, cache_kt, cache_vt, cos, sin, gmix, w_in, sinks, gvg, coef, bias_rows, ga, gg, w_out)


def _rope_tables(pos):
    inv_freq = ROPE_THETA ** (-jnp.arange(0, HEAD_DIM, 2, dtype=F32) / HEAD_DIM)
    ang = pos.astype(F32)[:, None] * inv_freq[None, :]
    return jnp.tile(jnp.cos(ang), (1, 4)), jnp.tile(jnp.sin(ang), (1, 4))


def kernel(x_prompt, x_sample, cache_k_win, cache_v_win, norm_ffn1, ffn1_gate, ffn1_up, ffn1_down, norm_mix, w_in,
           attn_sinks, gmlp_v_norm, gmlp_w_s, gmlp_b_s, norm_attn_out, norm_gmlp_out, w_out, norm_ffn2, ffn2_gate,
           ffn2_up, ffn2_down, norm_final):
    depth = norm_ffn1.shape[0]
    b, s, _ = x_prompt.shape
    bd, t_new, _ = x_sample.shape
    w_buf = cache_k_win.shape[2]

    cos_s, sin_s = _rope_tables(PAST_LEN + jnp.arange(t_new, dtype=jnp.int32))
    cos_s, sin_s = jnp.tile(cos_s, (8 // t_new, 1)), jnp.tile(sin_s, (8 // t_new, 1))

    hp = x_prompt.reshape(b * s, D_MODEL)
    hs = x_sample.reshape(bd * t_new, D_MODEL)
    outs = [[] for _ in range(6)]
    ffn1_w = [ffn1_gate[0], ffn1_up[0], ffn1_down[0]]
    for l in range(depth):
        last = l == depth - 1
        row = lambda a: a[l].reshape(1, -1)

        wcat = gmlp_w_s[l].reshape(G_HEADS // 2, 2, CHUNK, CHUNK).transpose(0, 2, 1, 3).reshape(
            G_HEADS // 2, CHUNK, 2 * CHUNK)
        bias_full = jnp.repeat(gmlp_b_s[l].T, HEAD_DIM, axis=1)
        ws_small = gmlp_w_s[l][:, :t_new, :t_new]
        coef = jnp.stack([
            jnp.stack([ws_small[:, t, t - d] if t >= d else jnp.zeros((G_HEADS,), F32) for t in range(t_new)])
            for d in range(t_new)])
        coef = jnp.tile(jnp.repeat(coef, HEAD_DIM, axis=2), (1, 8 // t_new, 1))
        bias_rows = jnp.tile(jnp.repeat(gmlp_b_s[l][:, :t_new].T, HEAD_DIM, axis=1), (8 // t_new, 1))

        hp, hs, (w_in_b, w_out_b, wg2, wu2, wd2) = _ffn_half(
            hp, hs, norm_ffn1[l], *ffn1_w, cast=(w_in[l], w_out[l], ffn2_gate[l], ffn2_up[l], ffn2_down[l]))

        hp, kpt, vpt, gvpt = _mix_prompt(hp.reshape(b, s, D_MODEL), row(norm_mix), w_in_b, attn_sinks[l],
                                         row(gmlp_v_norm), wcat, bias_full, row(norm_attn_out), row(norm_gmlp_out),
                                         w_out_b)
        to_t = lambda c: c.transpose(0, 2, 3, 1).reshape(bd, KV_W, w_buf)
        hs, kst, vst, gvs = _mix_sample(hs, to_t(cache_k_win[l]), to_t(cache_v_win[l]), cos_s, sin_s, row(norm_mix),
                                        w_in_b, attn_sinks[l], row(gmlp_v_norm), coef, bias_rows, row(norm_attn_out),
                                        row(norm_gmlp_out), w_out_b, t_new)
        from_t = lambda c: c.reshape(bd, N_KV, HEAD_DIM, w_buf).transpose(0, 3, 1, 2)

        next_ffn1 = () if last else (ffn1_gate[l + 1], ffn1_up[l + 1], ffn1_down[l + 1])
        hp, hs, ffn1_w = _ffn_half(hp.reshape(b * s, D_MODEL), hs, norm_ffn2[l], wg2, wu2, wd2,
                                   gf=norm_final if last else None, cast=next_ffn1)

        outs[0].append(kpt.reshape(b, N_KV, HEAD_DIM, WINDOW).transpose(0, 3, 1, 2))
        outs[1].append(vpt.reshape(b, N_KV, HEAD_DIM, WINDOW).transpose(0, 3, 1, 2))
        outs[2].append(from_t(kst))
        outs[3].append(from_t(vst))
        outs[4].append(gvpt.reshape(b, G_HEADS, D_GMLP // G_HEADS, CHUNK).transpose(0, 3, 1, 2))
        outs[5].append(gvs.reshape(bd, t_new, G_HEADS, D_GMLP // G_HEADS))

    return (hp.reshape(b, s, D_MODEL), hs.reshape(bd, t_new, D_MODEL)) + tuple(jnp.stack(o) for o in outs)
```

```python
import functools

import jax
import jax.numpy as jnp
from jax import lax
from jax.experimental import pallas as pl
from jax.experimental.pallas import tpu as pltpu

F32 = jnp.float32
BF16 = jnp.bfloat16

D_MODEL = 1024
D_FF = 2816
HEAD_DIM = 64
N_HEADS = 8
N_KV = 2
GQA = N_HEADS // N_KV
WINDOW = 128
CHUNK = 128
G_HEADS = 8
Q_W = N_HEADS * HEAD_DIM
KV_W = N_KV * HEAD_DIM
D_GMLP = 512
D_IN = Q_W + 2 * KV_W + 2 * D_GMLP
K_OFF = Q_W
V_OFF = K_OFF + KV_W
U_OFF = V_OFF + KV_W
GV_OFF = U_OFF + D_GMLP
ROPE_THETA = 10000.0
PAST_LEN = 16384
EPS = 1e-6
LOG2E = 1.4426950408889634
Q_SCALE = HEAD_DIM ** -0.5 * LOG2E
LANES = 128
BF16_SUBLANES = 16

TOKEN_TILE = 512
FF_CHUNK = 256
STAGE_ROWS = 64
STAGE_SLOTS = 8
SEQ_GROUP = 8
GROUPS_PER_STEP = 4
VMEM_LIMIT_BYTES = 56 * 1024 * 1024

_NT = (((1,), (1,)), ((), ()))


def _rms(x, g):
    ms = jnp.mean(x * x, axis=-1, keepdims=True)
    return (x * lax.rsqrt(ms + EPS)) * g


def _resident(shape):
    zeros = (0,) * len(shape)
    return pl.BlockSpec(shape, lambda *_: zeros, pipeline_mode=pl.Buffered(1))


def _round_rows_to_bf16(src_hbm, dst_ref, stage_ref, sem):
    n_slots, rows, _ = stage_ref.shape
    n_blocks = src_hbm.shape[0] // rows
    assert n_blocks >= n_slots

    def fetch(i, slot):
        return pltpu.make_async_copy(src_hbm.at[pl.ds(i * rows, rows), :], stage_ref.at[slot], sem.at[slot])

    for i in range(n_slots - 1):
        fetch(i, i).start()

    def body(i, carry):
        slot = i % n_slots

        @pl.when(i + n_slots - 1 < n_blocks)
        def _():
            fetch(i + n_slots - 1, (i + n_slots - 1) % n_slots).start()

        fetch(i, slot).wait()
        dst_ref[pl.ds(pl.multiple_of(i * rows, rows), rows), :] = stage_ref[slot].astype(BF16)
        return carry

    lax.fori_loop(0, n_blocks, body, 0)


def _ffn_kernel(*refs, final_norm, n_cast, f32_weights):
    refs = iter(refs)
    xp_ref, xn_ref, xs_ref, g_ref, wg_in, wu_in, wd_in = (next(refs) for _ in range(7))
    gf_ref = next(refs) if final_norm else None
    cast_in = [next(refs) for _ in range(n_cast)]
    yp_ref, ys_ref = next(refs), next(refs)
    cast_out = [next(refs) for _ in range(n_cast)]
    h_ref, act0_ref, act_ref = next(refs), next(refs), next(refs)
    if f32_weights:
        wg_ref, wu_ref, wd_ref, wide_stage, narrow_stage, wide_sem, narrow_sem = (next(refs) for _ in range(7))
    else:
        wg_ref, wu_ref, wd_ref = wg_in, wu_in, wd_in

    def gate_up(h, c):
        sl = slice(c * FF_CHUNK, (c + 1) * FF_CHUNK)
        gate = jnp.dot(h, wg_ref[:, sl], preferred_element_type=F32)
        up = jnp.dot(h, wu_ref[:, sl], preferred_element_type=F32)
        return (gate * jax.nn.sigmoid(gate) * up).astype(BF16)

    on_sample = pl.program_id(0) == 0

    @pl.when(on_sample)
    def _():
        if f32_weights:
            _round_rows_to_bf16(wg_in, wg_ref, wide_stage, wide_sem)
            _round_rows_to_bf16(wu_in, wu_ref, wide_stage, wide_sem)
            _round_rows_to_bf16(wd_in, wd_ref, narrow_stage, narrow_sem)
        h0 = _rms(xs_ref[...], g_ref[...]).astype(BF16)
        h_ref[...] = h0
        act0_ref[...] = gate_up(h0, 0)

    for c in range(1, D_FF // FF_CHUNK):
        act_ref[:, (c - 1) * FF_CHUNK:c * FF_CHUNK] = gate_up(h_ref[...], c)
    down0 = jnp.dot(act0_ref[...], wd_ref[0:FF_CHUNK, :], preferred_element_type=F32)
    hn = _rms(xn_ref[...], g_ref[...]).astype(BF16)
    h_ref[...] = hn
    act0_ref[...] = gate_up(hn, 0)
    x = jnp.where(on_sample, xs_ref[...], xp_ref[...])
    y = x + 0.5 * (down0 + jnp.dot(act_ref[...], wd_ref[FF_CHUNK:, :], preferred_element_type=F32))
    if final_norm:
        y = _rms(y, gf_ref[...])
    yp_ref[...] = y

    @pl.when(on_sample)
    def _():
        ys_ref[...] = yp_ref[...]

    for src, dst in zip(cast_in, cast_out):
        dst[...] = src[...].astype(BF16)


def _cast_row_blocks(rows, n_steps):
    return max(d for d in range(1, n_steps + 1) if rows % d == 0 and (rows // d) % BF16_SUBLANES == 0)


def _ffn_half(xp, xs, g, wg, wu, wd, gf=None, cast=()):
    tm = TOKEN_TILE
    f32_weights = wg.dtype == F32
    assert all(w.dtype == wg.dtype for w in (wu, wd))
    n = xp.shape[0]
    assert n % tm == 0 and xs.shape[0] == tm
    n_tiles = n // tm
    prompt_spec = pl.BlockSpec((tm, D_MODEL), lambda i: (jnp.maximum(i - 1, 0), 0))
    next_spec = pl.BlockSpec((tm, D_MODEL), lambda i: (jnp.minimum(i, n_tiles - 1), 0))
    sample_in = _resident((tm, D_MODEL))
    sample_out = pl.BlockSpec((tm, D_MODEL), lambda i: (0, 0))
    weight_specs = ([pl.BlockSpec(memory_space=pl.ANY)] * 3 if f32_weights else
                    [_resident((D_MODEL, D_FF)), _resident((D_MODEL, D_FF)), _resident((D_FF, D_MODEL))])
    in_specs = [prompt_spec, next_spec, sample_in, _resident((1, D_MODEL))] + weight_specs
    args = [xp, xp, xs, g.reshape(1, D_MODEL), wg, wu, wd]
    if gf is not None:
        in_specs.append(_resident((1, D_MODEL)))
        args.append(gf.reshape(1, D_MODEL))
    out_specs = [prompt_spec, sample_out]
    out_shape = [jax.ShapeDtypeStruct((n, D_MODEL), F32), jax.ShapeDtypeStruct((tm, D_MODEL), F32)]
    cast_specs = []
    for w in cast:
        rows, cols = w.shape
        nb = _cast_row_blocks(rows, n_tiles)
        cast_specs.append(pl.BlockSpec((rows // nb, cols), lambda i, nb=nb: (jnp.minimum(i, nb - 1), 0)))
        out_shape.append(jax.ShapeDtypeStruct(w.shape, BF16))
    outs = pl.pallas_call(
        functools.partial(_ffn_kernel, final_norm=gf is not None, n_cast=len(cast), f32_weights=f32_weights),
        grid=(n_tiles + 1,),
        in_specs=in_specs + cast_specs,
        out_specs=out_specs + cast_specs,
        out_shape=out_shape,
        scratch_shapes=[pltpu.VMEM((tm, D_MODEL), BF16), pltpu.VMEM((tm, FF_CHUNK), BF16),
                        pltpu.VMEM((tm, D_FF - FF_CHUNK), BF16)] + ([
                            pltpu.VMEM((D_MODEL, D_FF), BF16), pltpu.VMEM((D_MODEL, D_FF), BF16),
                            pltpu.VMEM((D_FF, D_MODEL), BF16), pltpu.VMEM((STAGE_SLOTS, STAGE_ROWS, D_FF), F32),
                            pltpu.VMEM((STAGE_SLOTS, STAGE_ROWS, D_MODEL), F32),
                            pltpu.SemaphoreType.DMA((STAGE_SLOTS,)), pltpu.SemaphoreType.DMA((STAGE_SLOTS,))]
                        if f32_weights else []),
        compiler_params=pltpu.CompilerParams(dimension_semantics=("arbitrary",), vmem_limit_bytes=VMEM_LIMIT_BYTES),
        name="ffn_final" if gf is not None else "ffn_half",
    )(*args, *cast)
    return outs[0], outs[1], list(outs[2:])


def _first_half(shape):
    return (lax.broadcasted_iota(jnp.int32, shape, 1) & (HEAD_DIM - 1)) < HEAD_DIM // 2


def _sign_sin(sin):
    return jnp.where(_first_half(sin.shape), -sin, sin)


def _rope(xg, cos, sin_signed):
    swapped = jnp.where(_first_half(xg.shape), pltpu.roll(xg, LANES - HEAD_DIM // 2, 1),
                        pltpu.roll(xg, HEAD_DIM // 2, 1))
    return xg * cos + swapped * sin_signed


def _in_proj(x, gmix, w_in, cos, sin_signed, gvn_gain):
    h = _rms(x, gmix).astype(BF16)
    z = jnp.dot(h, w_in, preferred_element_type=F32)
    q = jnp.concatenate(
        [(_rope(z[:, LANES * i:LANES * (i + 1)], cos, sin_signed) * Q_SCALE).astype(BF16) for i in range(Q_W // LANES)],
        axis=1)
    k = _rope(z[:, K_OFF:V_OFF], cos, sin_signed)
    v = z[:, V_OFF:U_OFF]
    u = _gelu(z[:, U_OFF:GV_OFF])
    gvn = _rms(_gelu(z[:, GV_OFF:]), gvn_gain)
    return q, k, v, u, gvn


def _out_proj(x, ya, yg, ga, gg, w_out):
    cat = jnp.concatenate([_rms(ya, ga).astype(BF16), _rms(yg, gg).astype(BF16)], axis=1)
    return x + jnp.dot(cat, w_out, preferred_element_type=F32)


def _softmax_sink(s, sink):
    m = jnp.maximum(jnp.max(s, axis=1, keepdims=True), sink)
    p = jnp.exp2(s - m)
    return p, jnp.sum(p, axis=1, keepdims=True) + jnp.exp2(sink - m)


def _gelu(x):
    k = -2.0 * (2.0 / jnp.pi) ** 0.5 * LOG2E
    return x / (1.0 + jnp.exp2(x * (k + (0.044715 * k) * (x * x))))


def _mix_prompt_kernel(xa_ref, xc_ref, coff_ref, soff_ref, cbase_ref, sbase_ref, gmix_ref, win_ref, sinks_ref, gvg_ref,
                       wcat_ref, bias_ref, ga_ref, gg_ref, wout_ref, o_ref, ko_ref, vo_ref, gvo_ref, q_s, k_s, v_s, u_s,
                       gv_s, cat_s, *, tm, tiles_per_seq, n_tiles):
    s = pl.program_id(0)

    @pl.when(s == 0)
    def _():
        q_s[1] = jnp.zeros((tm, Q_W), BF16)
        k_s[1] = jnp.zeros((tm + WINDOW, KV_W), BF16)
        v_s[1] = jnp.zeros((tm + WINDOW, KV_W), BF16)
        u_s[1] = jnp.zeros((tm, D_GMLP), F32)
        gv_s[1] = jnp.zeros((tm, D_GMLP), BF16)
        cat_s[0] = jnp.zeros((tm, D_MODEL), BF16)

    for parity in range(2):
        pl.when(s % 2 == parity)(functools.partial(
            _mix_prompt_step, s, parity, xa_ref, xc_ref, coff_ref, soff_ref, cbase_ref, sbase_ref, gmix_ref, win_ref,
            sinks_ref, gvg_ref, wcat_ref, bias_ref, ga_ref, gg_ref, wout_ref, o_ref, ko_ref, vo_ref, gvo_ref, q_s, k_s,
            v_s, u_s, gv_s, cat_s, tm=tm, tiles_per_seq=tiles_per_seq, n_tiles=n_tiles))


def _mix_prompt_step(s, cur, xa_ref, xc_ref, coff_ref, soff_ref, cbase_ref, sbase_ref, gmix_ref, win_ref, sinks_ref,
                     gvg_ref, wcat_ref, bias_ref, ga_ref, gg_ref, wout_ref, o_ref, ko_ref, vo_ref, gvo_ref, q_s, k_s, v_s,
                     u_s, gv_s, cat_s, *, tm, tiles_per_seq, n_tiles):
    oth = 1 - cur

    h = _rms(xa_ref[0], gmix_ref[...]).astype(BF16)
    tile_in_seq = jnp.minimum(s, n_tiles - 1) % tiles_per_seq
    cb = cbase_ref[pl.ds(tile_in_seq, 1), :]
    sb = sbase_ref[pl.ds(tile_in_seq, 1), :]
    cos = cb * coff_ref[...] - sb * soff_ref[...]
    sin = _sign_sin(sb * coff_ref[...] + cb * soff_ref[...])

    def proj_q():
        z = jnp.dot(h, win_ref[:, 0:Q_W], preferred_element_type=F32)
        for i in range(Q_W // LANES):
            q_s[cur, :, LANES * i:LANES * (i + 1)] = (
                _rope(z[:, LANES * i:LANES * (i + 1)], cos, sin) * Q_SCALE).astype(BF16)

    def proj_kv():
        z = jnp.dot(h, win_ref[:, K_OFF:U_OFF], preferred_element_type=F32)
        k = _rope(z[:, 0:KV_W], cos, sin)
        v = z[:, KV_W:]
        k_s[cur, 0:WINDOW, :] = k_s[oth, tm:tm + WINDOW, :]
        v_s[cur, 0:WINDOW, :] = v_s[oth, tm:tm + WINDOW, :]
        k_s[cur, WINDOW:, :] = k.astype(BF16)
        v_s[cur, WINDOW:, :] = v.astype(BF16)
        ko_ref[0] = k[tm - WINDOW:].T
        vo_ref[0] = v[tm - WINDOW:].T

    def proj_u():
        u_s[cur] = _gelu(jnp.dot(h, win_ref[:, U_OFF:GV_OFF], preferred_element_type=F32))

    def proj_gv():
        gvn = _rms(_gelu(jnp.dot(h, win_ref[:, GV_OFF:], preferred_element_type=F32)), gvg_ref[...])
        gv_s[cur] = gvn.astype(BF16)
        gvo_ref[0] = gvn[tm - CHUNK:].T

    def out_half(c):
        cols = slice(c * (D_MODEL // 2), (c + 1) * (D_MODEL // 2))

        def run():
            o_ref[0, :, cols] = xc_ref[0, :, cols] + jnp.dot(cat_s[cur], wout_ref[:, cols], preferred_element_type=F32)
        return run

    mid_fill = [[out_half(0)], [proj_q], [proj_kv, proj_u], [proj_gv]]
    end_fill = [[], [], [], [out_half(1)]]
    assert len(mid_fill) == len(end_fill) == tm // WINDOW

    wrow = lax.broadcasted_iota(jnp.int32, (CHUNK, 2 * CHUNK), 0)
    wcol = lax.broadcasted_iota(jnp.int32, (CHUNK, 2 * CHUNK), 1) & (CHUNK - 1)
    wmix = [jnp.where(wcol <= wrow, wcat_ref[p], 0.0).astype(BF16) for p in range(G_HEADS // 2)]
    lane = lax.broadcasted_iota(jnp.int32, (CHUNK, LANES), 1)
    low_head = lane < HEAD_DIM

    qi = lax.broadcasted_iota(jnp.int32, (WINDOW, 2 * WINDOW), 0)
    sj = lax.broadcasted_iota(jnp.int32, (WINDOW, 2 * WINDOW), 1)
    dist = WINDOW + qi - sj
    band = (dist >= 0) & (dist < WINDOW)
    first_lo = jnp.where((s + tiles_per_seq - 1) % tiles_per_seq == 0, WINDOW, 0)

    for j in range(tm // WINDOW):
        rows = slice(j * WINDOW, (j + 1) * WINDOW)
        mask = band & (sj >= first_lo) if j == 0 else band
        qb = q_s[oth, rows, :]
        scores = []
        for kv in range(N_KV):
            qs = jnp.concatenate(
                [qb[:, HEAD_DIM * hd:HEAD_DIM * (hd + 1)] for hd in range(kv * GQA, (kv + 1) * GQA)], axis=0)
            kb = k_s[oth, j * WINDOW:(j + 2) * WINDOW, HEAD_DIM * kv:HEAD_DIM * (kv + 1)]
            scores.append(lax.dot_general(qs, kb, _NT, preferred_element_type=F32))
        for run in mid_fill[j]:
            run()
        outs, dens = [], []
        for kv in range(N_KV):
            vb = v_s[oth, j * WINDOW:(j + 2) * WINDOW, HEAD_DIM * kv:HEAD_DIM * (kv + 1)]
            ps = []
            for g in range(GQA):
                sg = jnp.where(mask, scores[kv][g * WINDOW:(g + 1) * WINDOW], -jnp.inf)
                p, den = _softmax_sink(sg, sinks_ref[kv * GQA + g] * LOG2E)
                ps.append(p.astype(BF16))
                dens.append(den)
            outs.append(jnp.dot(jnp.concatenate(ps, axis=0), vb, preferred_element_type=F32))
        mixed = []
        for p in range(G_HEADS // 2):
            r = gv_s[oth, rows, LANES * p:LANES * (p + 1)]
            zero = jnp.zeros_like(r)
            rhs = jnp.concatenate([jnp.where(low_head, r, zero), jnp.where(low_head, zero, r)], axis=0)
            mixed.append(jnp.dot(wmix[p], rhs, preferred_element_type=F32))
        for run in end_fill[j]:
            run()
        ya = [outs[hd // GQA][(hd % GQA) * WINDOW:(hd % GQA + 1) * WINDOW] / dens[hd] for hd in range(N_HEADS)]
        cat_s[oth, rows, 0:Q_W] = _rms(jnp.concatenate(ya, axis=1), ga_ref[...]).astype(BF16)
        yg = u_s[oth, rows, :] * (jnp.concatenate(mixed, axis=1) + bias_ref[...])
        cat_s[oth, rows, Q_W:] = _rms(yg, gg_ref[...]).astype(BF16)


def _mix_prompt(x, gmix, w_in, sinks, gvg, wcat, bias_full, ga, gg, w_out):
    b, s, _ = x.shape
    tm = TOKEN_TILE
    assert s % tm == 0 and tm % WINDOW == 0
    tiles_per_seq = s // tm
    n_tiles = b * tiles_per_seq
    cos_off, sin_off = _rope_tables(jnp.arange(tm, dtype=jnp.int32))
    cos_base, sin_base = _rope_tables(tm * jnp.arange(tiles_per_seq, dtype=jnp.int32))
    proj_tile = lambda i: jnp.minimum(i, n_tiles - 1)
    out_tile = lambda i: jnp.maximum(i - 2, 0)
    x_tiles = x.reshape(n_tiles, tm, D_MODEL)
    last = lambda width: pl.BlockSpec((1, width, WINDOW), lambda i: (proj_tile(i) // tiles_per_seq, 0, 0))
    out, ko, vo, gvo = pl.pallas_call(
        functools.partial(_mix_prompt_kernel, tm=tm, tiles_per_seq=tiles_per_seq, n_tiles=n_tiles),
        grid=(n_tiles + 2,),
        in_specs=[pl.BlockSpec((1, tm, D_MODEL), lambda i: (proj_tile(i), 0, 0)),
                  pl.BlockSpec((1, tm, D_MODEL), lambda i: (out_tile(i), 0, 0)),
                  _resident((tm, LANES)), _resident((tm, LANES)), _resident((tiles_per_seq, LANES)),
                  _resident((tiles_per_seq, LANES)), _resident((1, D_MODEL)), _resident((D_MODEL, D_IN)),
                  pl.BlockSpec(memory_space=pltpu.SMEM), _resident((1, D_GMLP)),
                  _resident((G_HEADS // 2, CHUNK, 2 * CHUNK)), _resident((CHUNK, D_GMLP)), _resident((1, Q_W)),
                  _resident((1, D_GMLP)), _resident((D_MODEL, D_MODEL))],
        out_specs=[pl.BlockSpec((1, tm, D_MODEL), lambda i: (out_tile(i), 0, 0)), last(KV_W), last(KV_W),
                   last(D_GMLP)],
        out_shape=[jax.ShapeDtypeStruct((n_tiles, tm, D_MODEL), F32), jax.ShapeDtypeStruct((b, KV_W, WINDOW), F32),
                   jax.ShapeDtypeStruct((b, KV_W, WINDOW), F32), jax.ShapeDtypeStruct((b, D_GMLP, CHUNK), F32)],
        scratch_shapes=[pltpu.VMEM((2, tm, Q_W), BF16), pltpu.VMEM((2, tm + WINDOW, KV_W), BF16),
                        pltpu.VMEM((2, tm + WINDOW, KV_W), BF16), pltpu.VMEM((2, tm, D_GMLP), F32),
                        pltpu.VMEM((2, tm, D_GMLP), BF16), pltpu.VMEM((2, tm, D_MODEL), BF16)],
        compiler_params=pltpu.CompilerParams(dimension_semantics=("arbitrary",), vmem_limit_bytes=VMEM_LIMIT_BYTES),
        name="mix_prompt",
    )(x_tiles, x_tiles, cos_off, sin_off, cos_base, sin_base, gmix, w_in, sinks, gvg, wcat, bias_full, ga, gg, w_out)
    return out.reshape(b, s, D_MODEL), ko, vo, gvo


def _mix_sample_kernel(x_ref, ckt_ref, cvt_ref, cos_ref, sin_ref, gmix_ref, win_ref, sinks_ref, gvg_ref, coef_ref,
                       biasr_ref, ga_ref, gg_ref, wout_ref, o_ref, kot_ref, vot_ref, gvo_ref, q_s, k_s, v_s, ya_s,
                       yg_s, *, t_new, w_buf):
    step = pl.program_id(0)
    n_tok = x_ref.shape[0]
    step_seqs = SEQ_GROUP * GROUPS_PER_STEP
    step_rows = step_seqs * t_new
    grp_rows = SEQ_GROUP * t_new
    grp_keys = SEQ_GROUP * w_buf
    sub = 8

    @pl.when(step == 0)
    def _():
        tile = lambda tab: jnp.broadcast_to(tab[None], (n_tok // sub, sub, LANES)).reshape(n_tok, LANES)
        q, k, v, u, gvn = _in_proj(x_ref[...], gmix_ref[...], win_ref[...], tile(cos_ref[...]),
                                   tile(_sign_sin(sin_ref[...])), gvg_ref[...])
        q_s[...] = q
        k_s[...] = k
        v_s[...] = v
        gvo_ref[...] = gvn
        g3 = gvn.reshape(n_tok // sub, sub, D_GMLP)
        trow = lax.broadcasted_iota(jnp.int32, (1, sub, D_GMLP), 1) & (t_new - 1)
        mixed = biasr_ref[...][None] + coef_ref[0][None] * g3
        for d in range(1, t_new):
            shifted = jnp.where(trow >= d, pltpu.roll(g3, d, 1), 0.0)
            mixed = mixed + coef_ref[d][None] * shifted
        yg_s[...] = (u.reshape(n_tok // sub, sub, D_GMLP) * mixed).reshape(n_tok, D_GMLP)

    row0 = pl.multiple_of(step * step_rows, step_rows)
    q_step = q_s[pl.ds(row0, step_rows), :]
    kn = k_s[pl.ds(row0, step_rows), :]
    vn = v_s[pl.ds(row0, step_rows), :]

    knt = kn.T
    vnt = vn.T
    tail = lax.broadcasted_iota(jnp.int32, (KV_W, w_buf), 1) >= w_buf - t_new
    for b in range(step_seqs):
        shift = (w_buf - t_new - t_new * b) % w_buf
        kot_ref[b] = jnp.where(tail, pltpu.roll(knt, shift, 1) if shift else knt,
                               pltpu.roll(ckt_ref[b], w_buf - t_new, 1))
        vot_ref[b] = jnp.where(tail, pltpu.roll(vnt, shift, 1) if shift else vnt,
                               pltpu.roll(cvt_ref[b], w_buf - t_new, 1))

    knb = kn.astype(BF16)
    vnb = vn.astype(BF16)

    n_rows = GQA * grp_rows
    shift_t = t_new.bit_length() - 1
    shift_w = w_buf.bit_length() - 1
    r = lax.broadcasted_iota(jnp.int32, (n_rows, grp_keys), 0)
    c = lax.broadcasted_iota(jnp.int32, (n_rows, grp_keys), 1)
    mask_c = ((c >> shift_w) == ((r & (grp_rows - 1)) >> shift_t)) & ((c & (w_buf - 1)) > (r & (t_new - 1)))
    r2 = lax.broadcasted_iota(jnp.int32, (n_rows, grp_rows), 0)
    c2 = lax.broadcasted_iota(jnp.int32, (n_rows, grp_rows), 1)
    mask_n = ((c2 >> shift_t) == ((r2 & (grp_rows - 1)) >> shift_t)) & ((c2 & (t_new - 1)) <= (r2 & (t_new - 1)))
    row_head = lax.broadcasted_iota(jnp.int32, (n_rows, 1), 0) >> (grp_rows.bit_length() - 1)

    for grp in range(GROUPS_PER_STEP):
        rows = slice(grp * grp_rows, (grp + 1) * grp_rows)
        seqs = range(grp * SEQ_GROUP, (grp + 1) * SEQ_GROUP)
        for kv in range(N_KV):
            heads = [kv * GQA + i for i in range(GQA)]
            lanes = slice(HEAD_DIM * kv, HEAD_DIM * (kv + 1))
            kt = jnp.concatenate([ckt_ref[b, lanes, :] for b in seqs], axis=1).astype(BF16)
            vt = jnp.concatenate([cvt_ref[b, lanes, :] for b in seqs], axis=1).astype(BF16)
            qs = jnp.concatenate([q_step[rows, HEAD_DIM * hd:HEAD_DIM * (hd + 1)] for hd in heads], axis=0)
            s_c = jnp.where(mask_c, jnp.dot(qs, kt, preferred_element_type=F32), -jnp.inf)
            s_n = jnp.where(mask_n, lax.dot_general(qs, knb[rows, lanes], _NT, preferred_element_type=F32), -jnp.inf)
            sink = jnp.full((n_rows, 1), sinks_ref[heads[0]] * LOG2E, F32)
            for i in range(1, GQA):
                sink = jnp.where(row_head == i, sinks_ref[heads[i]] * LOG2E, sink)
            m = jnp.maximum(jnp.maximum(jnp.max(s_c, axis=1, keepdims=True), jnp.max(s_n, axis=1, keepdims=True)),
                            sink)
            p_c = jnp.exp2(s_c - m)
            p_n = jnp.exp2(s_n - m)
            den = jnp.sum(p_c, axis=1, keepdims=True) + jnp.sum(p_n, axis=1, keepdims=True) + jnp.exp2(sink - m)
            o = (lax.dot_general(p_c.astype(BF16), vt, _NT, preferred_element_type=F32)
                 + jnp.dot(p_n.astype(BF16), vnb[rows, lanes], preferred_element_type=F32)) / den
            for i, hd in enumerate(heads):
                ya_s[pl.ds(row0 + grp * grp_rows, grp_rows), HEAD_DIM * hd:HEAD_DIM * (hd + 1)] = (
                    o[i * grp_rows:(i + 1) * grp_rows])

    @pl.when(step == pl.num_programs(0) - 1)
    def _():
        o_ref[...] = _out_proj(x_ref[...], ya_s[...], yg_s[...], ga_ref[...], gg_ref[...], wout_ref[...])


def _mix_sample(x, cache_kt, cache_vt, cos, sin, gmix, w_in, sinks, gvg, coef, bias_rows, ga, gg, w_out, t_new):
    n_tok = x.shape[0]
    n_seq, _, w_buf = cache_kt.shape
    step_seqs = SEQ_GROUP * GROUPS_PER_STEP
    assert n_seq % step_seqs == 0 and n_tok == n_seq * t_new
    assert t_new & (t_new - 1) == 0 and w_buf & (w_buf - 1) == 0 and 8 % t_new == 0
    assert step_seqs * t_new == w_buf == LANES
    cache_spec = pl.BlockSpec((step_seqs, KV_W, w_buf), lambda i: (i, 0, 0))
    return pl.pallas_call(
        functools.partial(_mix_sample_kernel, t_new=t_new, w_buf=w_buf),
        grid=(n_seq // step_seqs,),
        in_specs=[_resident((n_tok, D_MODEL)), cache_spec, cache_spec, _resident((8, LANES)),
                  _resident((8, LANES)), _resident((1, D_MODEL)), _resident((D_MODEL, D_IN)),
                  pl.BlockSpec(memory_space=pltpu.SMEM), _resident((1, D_GMLP)), _resident((t_new, 8, D_GMLP)),
                  _resident((8, D_GMLP)), _resident((1, Q_W)), _resident((1, D_GMLP)),
                  _resident((D_MODEL, D_MODEL))],
        out_specs=[pl.BlockSpec((n_tok, D_MODEL), lambda i: (0, 0)), cache_spec, cache_spec,
                   pl.BlockSpec((n_tok, D_GMLP), lambda i: (0, 0))],
        out_shape=[jax.ShapeDtypeStruct((n_tok, D_MODEL), F32), jax.ShapeDtypeStruct(cache_kt.shape, F32),
                   jax.ShapeDtypeStruct(cache_vt.shape, F32), jax.ShapeDtypeStruct((n_tok, D_GMLP), F32)],
        scratch_shapes=[pltpu.VMEM((n_tok, Q_W), BF16), pltpu.VMEM((n_tok, KV_W), F32), pltpu.VMEM((n_tok, KV_W), F32),
                        pltpu.VMEM((n_tok, Q_W), F32), pltpu.VMEM((n_tok, D_GMLP), F32)],
        compiler_params=pltpu.CompilerParams(dimension_semantics=("arbitrary",), vmem_limit_bytes=VMEM_LIMIT_BYTES),
        name="mix_sample",
    )(x, cache_kt, cache_vt, cos, sin, gmix, w_in, sinks, gvg, coef, bias_rows, ga, gg, w_out)


def _rope_tables(pos):
    inv_freq = ROPE_THETA ** (-jnp.arange(0, HEAD_DIM, 2, dtype=F32) / HEAD_DIM)
    ang = pos.astype(F32)[:, None] * inv_freq[None, :]
    return jnp.tile(jnp.cos(ang), (1, 4)), jnp.tile(jnp.sin(ang), (1, 4))


def kernel(x_prompt, x_sample, cache_k_win, cache_v_win, norm_ffn1, ffn1_gate, ffn1_up, ffn1_down, norm_mix, w_in,
           attn_sinks, gmlp_v_norm, gmlp_w_s, gmlp_b_s, norm_attn_out, norm_gmlp_out, w_out, norm_ffn2, ffn2_gate,
           ffn2_up, ffn2_down, norm_final):
    depth = norm_ffn1.shape[0]
    b, s, _ = x_prompt.shape
    bd, t_new, _ = x_sample.shape
    w_buf = cache_k_win.shape[2]

    cos_s, sin_s = _rope_tables(PAST_LEN + jnp.arange(t_new, dtype=jnp.int32))
    cos_s, sin_s = jnp.tile(cos_s, (8 // t_new, 1)), jnp.tile(sin_s, (8 // t_new, 1))

    hp = x_prompt.reshape(b * s, D_MODEL)
    hs = x_sample.reshape(bd * t_new, D_MODEL)
    outs = [[] for _ in range(6)]
    ffn1_w = [ffn1_gate[0], ffn1_up[0], ffn1_down[0]]
    for l in range(depth):
        last = l == depth - 1
        row = lambda a: a[l].reshape(1, -1)

        wcat = gmlp_w_s[l].reshape(G_HEADS // 2, 2, CHUNK, CHUNK).transpose(0, 2, 1, 3).reshape(
            G_HEADS // 2, CHUNK, 2 * CHUNK)
        bias_full = jnp.repeat(gmlp_b_s[l].T, HEAD_DIM, axis=1)
        ws_small = gmlp_w_s[l][:, :t_new, :t_new]
        coef = jnp.stack([
            jnp.stack([ws_small[:, t, t - d] if t >= d else jnp.zeros((G_HEADS,), F32) for t in range(t_new)])
            for d in range(t_new)])
        coef = jnp.tile(jnp.repeat(coef, HEAD_DIM, axis=2), (1, 8 // t_new, 1))
        bias_rows = jnp.tile(jnp.repeat(gmlp_b_s[l][:, :t_new].T, HEAD_DIM, axis=1), (8 // t_new, 1))

        hp, hs, (w_in_b, w_out_b, wg2, wu2, wd2) = _ffn_half(
            hp, hs, norm_ffn1[l], *ffn1_w, cast=(w_in[l], w_out[l], ffn2_gate[l], ffn2_up[l], ffn2_down[l]))

        hp, kpt, vpt, gvpt = _mix_prompt(hp.reshape(b, s, D_MODEL), row(norm_mix), w_in_b, attn_sinks[l],
                                         row(gmlp_v_norm), wcat, bias_full, row(norm_attn_out), row(norm_gmlp_out),
                                         w_out_b)
        to_t = lambda c: c.transpose(0, 2, 3, 1).reshape(bd, KV_W, w_buf)
        hs, kst, vst, gvs = _mix_sample(hs, to_t(cache_k_win[l]), to_t(cache_v_win[l]), cos_s, sin_s, row(norm_mix),
                                        w_in_b, attn_sinks[l], row(gmlp_v_norm), coef, bias_rows, row(norm_attn_out),
                                        row(norm_gmlp_out), w_out_b, t_new)
        from_t = lambda c: c.reshape(bd, N_KV, HEAD_DIM, w_buf).transpose(0, 3, 1, 2)

        next_ffn1 = () if last else (ffn1_gate[l + 1], ffn1_up[l + 1], ffn1_down[l + 1])
        hp, hs, ffn1_w = _ffn_half(hp.reshape(b * s, D_MODEL), hs, norm_ffn2[l], wg2, wu2, wd2,
                                   gf=norm_final if last else None, cast=next_ffn1)

        outs[0].append(kpt.reshape(b, N_KV, HEAD_DIM, WINDOW).transpose(0, 3, 1, 2))
        outs[1].append(vpt.reshape(b, N_KV, HEAD_DIM, WINDOW).transpose(0, 3, 1, 2))
        outs[2].append(from_t(kst))
        outs[3].append(from_t(vst))
        outs[4].append(gvpt.reshape(b, G_HEADS, D_GMLP // G_HEADS, CHUNK).transpose(0, 3, 1, 2))
        outs[5].append(gvs.reshape(bd, t_new, G_HEADS, D_GMLP // G_HEADS))

    return (hp.reshape(b, s, D_MODEL), hs.reshape(bd, t_new, D_MODEL)) + tuple(jnp.stack(o) for o in outs)
```

```python
import functools

import jax
import jax.numpy as jnp
from jax import lax
from jax.experimental import pallas as pl
from jax.experimental.pallas import tpu as pltpu

F32 = jnp.float32
BF16 = jnp.bfloat16

D_MODEL = 1024
D_FF = 2816
HEAD_DIM = 64
N_HEADS = 8
N_KV = 2
GQA = N_HEADS // N_KV
WINDOW = 128
CHUNK = 128
G_HEADS = 8
Q_W = N_HEADS * HEAD_DIM
KV_W = N_KV * HEAD_DIM
D_GMLP = 512
D_IN = Q_W + 2 * KV_W + 2 * D_GMLP
K_OFF = Q_W
V_OFF = K_OFF + KV_W
U_OFF = V_OFF + KV_W
GV_OFF = U_OFF + D_GMLP
ROPE_THETA = 10000.0
PAST_LEN = 16384
EPS = 1e-6
LOG2E = 1.4426950408889634
Q_SCALE = HEAD_DIM ** -0.5 * LOG2E
LANES = 128
BF16_SUBLANES = 16

TOKEN_TILE = 512
FF_CHUNK = 256
STAGE_ROWS = 64
STAGE_SLOTS = 8
SEQ_GROUP = 8
GROUPS_PER_STEP = 4
VMEM_LIMIT_BYTES = 56 * 1024 * 1024

_NT = (((1,), (1,)), ((), ()))


def _rms(x, g):
    ms = jnp.mean(x * x, axis=-1, keepdims=True)
    return (x * lax.rsqrt(ms + EPS)) * g


def _resident(shape):
    zeros = (0,) * len(shape)
    return pl.BlockSpec(shape, lambda *_: zeros, pipeline_mode=pl.Buffered(1))


def _round_rows_to_bf16(src_hbm, dst_ref, stage_ref, sem):
    n_slots, rows, _ = stage_ref.shape
    n_blocks = src_hbm.shape[0] // rows
    assert n_blocks >= n_slots

    def fetch(i, slot):
        return pltpu.make_async_copy(src_hbm.at[pl.ds(i * rows, rows), :], stage_ref.at[slot], sem.at[slot])

    for i in range(n_slots - 1):
        fetch(i, i).start()

    def body(i, carry):
        slot = i % n_slots

        @pl.when(i + n_slots - 1 < n_blocks)
        def _():
            fetch(i + n_slots - 1, (i + n_slots - 1) % n_slots).start()

        fetch(i, slot).wait()
        dst_ref[pl.ds(pl.multiple_of(i * rows, rows), rows), :] = stage_ref[slot].astype(BF16)
        return carry

    lax.fori_loop(0, n_blocks, body, 0)


def _ffn_kernel(*refs, final_norm, n_cast, f32_weights):
    refs = iter(refs)
    xp_ref, xn_ref, xs_ref, g_ref, wg_in, wu_in, wd_in = (next(refs) for _ in range(7))
    gf_ref = next(refs) if final_norm else None
    cast_in = [next(refs) for _ in range(n_cast)]
    yp_ref, ys_ref = next(refs), next(refs)
    cast_out = [next(refs) for _ in range(n_cast)]
    h_ref, act0_ref, act_ref = next(refs), next(refs), next(refs)
    if f32_weights:
        wg_ref, wu_ref, wd_ref, wide_stage, narrow_stage, wide_sem, narrow_sem = (next(refs) for _ in range(7))
    else:
        wg_ref, wu_ref, wd_ref = wg_in, wu_in, wd_in

    def gate_up(h, c):
        sl = slice(c * FF_CHUNK, (c + 1) * FF_CHUNK)
        gate = jnp.dot(h, wg_ref[:, sl], preferred_element_type=F32)
        up = jnp.dot(h, wu_ref[:, sl], preferred_element_type=F32)
        return (gate * jax.nn.sigmoid(gate) * up).astype(BF16)

    on_sample = pl.program_id(0) == 0

    @pl.when(on_sample)
    def _():
        if f32_weights:
            _round_rows_to_bf16(wg_in, wg_ref, wide_stage, wide_sem)
            _round_rows_to_bf16(wu_in, wu_ref, wide_stage, wide_sem)
            _round_rows_to_bf16(wd_in, wd_ref, narrow_stage, narrow_sem)
        h0 = _rms(xs_ref[...], g_ref[...]).astype(BF16)
        h_ref[...] = h0
        act0_ref[...] = gate_up(h0, 0)

    for c in range(1, D_FF // FF_CHUNK):
        act_ref[:, (c - 1) * FF_CHUNK:c * FF_CHUNK] = gate_up(h_ref[...], c)
    down0 = jnp.dot(act0_ref[...], wd_ref[0:FF_CHUNK, :], preferred_element_type=F32)
    hn = _rms(xn_ref[...], g_ref[...]).astype(BF16)
    h_ref[...] = hn
    act0_ref[...] = gate_up(hn, 0)
    x = jnp.where(on_sample, xs_ref[...], xp_ref[...])
    y = x + 0.5 * (down0 + jnp.dot(act_ref[...], wd_ref[FF_CHUNK:, :], preferred_element_type=F32))
    if final_norm:
        y = _rms(y, gf_ref[...])
    yp_ref[...] = y

    @pl.when(on_sample)
    def _():
        ys_ref[...] = yp_ref[...]

    for src, dst in zip(cast_in, cast_out):
        dst[...] = src[...].astype(BF16)


def _cast_row_blocks(rows, n_steps):
    return max(d for d in range(1, n_steps + 1) if rows % d == 0 and (rows // d) % BF16_SUBLANES == 0)


def _ffn_half(xp, xs, g, wg, wu, wd, gf=None, cast=()):
    tm = TOKEN_TILE
    f32_weights = wg.dtype == F32
    assert all(w.dtype == wg.dtype for w in (wu, wd))
    n = xp.shape[0]
    assert n % tm == 0 and xs.shape[0] == tm
    n_tiles = n // tm
    prompt_spec = pl.BlockSpec((tm, D_MODEL), lambda i: (jnp.maximum(i - 1, 0), 0))
    next_spec = pl.BlockSpec((tm, D_MODEL), lambda i: (jnp.minimum(i, n_tiles - 1), 0))
    sample_in = _resident((tm, D_MODEL))
    sample_out = pl.BlockSpec((tm, D_MODEL), lambda i: (0, 0))
    weight_specs = ([pl.BlockSpec(memory_space=pl.ANY)] * 3 if f32_weights else
                    [_resident((D_MODEL, D_FF)), _resident((D_MODEL, D_FF)), _resident((D_FF, D_MODEL))])
    in_specs = [prompt_spec, next_spec, sample_in, _resident((1, D_MODEL))] + weight_specs
    args = [xp, xp, xs, g.reshape(1, D_MODEL), wg, wu, wd]
    if gf is not None:
        in_specs.append(_resident((1, D_MODEL)))
        args.append(gf.reshape(1, D_MODEL))
    out_specs = [prompt_spec, sample_out]
    out_shape = [jax.ShapeDtypeStruct((n, D_MODEL), F32), jax.ShapeDtypeStruct((tm, D_MODEL), F32)]
    cast_specs = []
    for w in cast:
        rows, cols = w.shape
        nb = _cast_row_blocks(rows, n_tiles)
        cast_specs.append(pl.BlockSpec((rows // nb, cols), lambda i, nb=nb: (jnp.minimum(i, nb - 1), 0)))
        out_shape.append(jax.ShapeDtypeStruct(w.shape, BF16))
    outs = pl.pallas_call(
        functools.partial(_ffn_kernel, final_norm=gf is not None, n_cast=len(cast), f32_weights=f32_weights),
        grid=(n_tiles + 1,),
        in_specs=in_specs + cast_specs,
        out_specs=out_specs + cast_specs,
        out_shape=out_shape,
        scratch_shapes=[pltpu.VMEM((tm, D_MODEL), BF16), pltpu.VMEM((tm, FF_CHUNK), BF16),
                        pltpu.VMEM((tm, D_FF - FF_CHUNK), BF16)] + ([
                            pltpu.VMEM((D_MODEL, D_FF), BF16), pltpu.VMEM((D_MODEL, D_FF), BF16),
                            pltpu.VMEM((D_FF, D_MODEL), BF16), pltpu.VMEM((STAGE_SLOTS, STAGE_ROWS, D_FF), F32),
                            pltpu.VMEM((STAGE_SLOTS, STAGE_ROWS, D_MODEL), F32),
                            pltpu.SemaphoreType.DMA((STAGE_SLOTS,)), pltpu.SemaphoreType.DMA((STAGE_SLOTS,))]
                        if f32_weights else []),
        compiler_params=pltpu.CompilerParams(dimension_semantics=("arbitrary",), vmem_limit_bytes=VMEM_LIMIT_BYTES),
        name="ffn_final" if gf is not None else "ffn_half",
    )(*args, *cast)
    return outs[0], outs[1], list(outs[2:])


def _first_half(shape):
    return (lax.broadcasted_iota(jnp.int32, shape, 1) & (HEAD_DIM - 1)) < HEAD_DIM // 2


def _sign_sin(sin):
    return jnp.where(_first_half(sin.shape), -sin, sin)


def _rope(xg, cos, sin_signed):
    swapped = jnp.where(_first_half(xg.shape), pltpu.roll(xg, LANES - HEAD_DIM // 2, 1),
                        pltpu.roll(xg, HEAD_DIM // 2, 1))
    return xg * cos + swapped * sin_signed


def _in_proj(x, gmix, w_in, cos, sin_signed, gvn_gain):
    h = _rms(x, gmix).astype(BF16)
    z = jnp.dot(h, w_in, preferred_element_type=F32)
    q = jnp.concatenate(
        [(_rope(z[:, LANES * i:LANES * (i + 1)], cos, sin_signed) * Q_SCALE).astype(BF16) for i in range(Q_W // LANES)],
        axis=1)
    k = _rope(z[:, K_OFF:V_OFF], cos, sin_signed)
    v = z[:, V_OFF:U_OFF]
    u = _gelu(z[:, U_OFF:GV_OFF])
    gvn = _rms(_gelu(z[:, GV_OFF:]), gvn_gain)
    return q, k, v, u, gvn


def _out_proj(x, ya, yg, ga, gg, w_out):
    cat = jnp.concatenate([_rms(ya, ga).astype(BF16), _rms(yg, gg).astype(BF16)], axis=1)
    return x + jnp.dot(cat, w_out, preferred_element_type=F32)


def _softmax_sink(s, sink):
    m = jnp.maximum(jnp.max(s, axis=1, keepdims=True), sink)
    p = jnp.exp2(s - m)
    return p, jnp.sum(p, axis=1, keepdims=True) + jnp.exp2(sink - m)


def _gelu(x):
    k = -2.0 * (2.0 / jnp.pi) ** 0.5 * LOG2E
    return x / (1.0 + jnp.exp2(x * (k + (0.044715 * k) * (x * x))))


def _mix_prompt_kernel(xa_ref, xc_ref, coff_ref, soff_ref, cbase_ref, sbase_ref, gmix_ref, win_ref, sinks_ref, gvg_ref,
                       wcat_ref, bias_ref, ga_ref, gg_ref, wout_ref, o_ref, ko_ref, vo_ref, gvo_ref, q_s, k_s, v_s, u_s,
                       gv_s, cat_s, *, tm, tiles_per_seq, n_tiles):
    s = pl.program_id(0)

    @pl.when(s == 0)
    def _():
        q_s[1] = jnp.zeros((tm, Q_W), BF16)
        k_s[1] = jnp.zeros((tm + WINDOW, KV_W), BF16)
        v_s[1] = jnp.zeros((tm + WINDOW, KV_W), BF16)
        u_s[1] = jnp.zeros((tm, D_GMLP), F32)
        gv_s[1] = jnp.zeros((tm, D_GMLP), BF16)
        cat_s[0] = jnp.zeros((tm, D_MODEL), BF16)

    for parity in range(2):
        pl.when(s % 2 == parity)(functools.partial(
            _mix_prompt_step, s, parity, xa_ref, xc_ref, coff_ref, soff_ref, cbase_ref, sbase_ref, gmix_ref, win_ref,
            sinks_ref, gvg_ref, wcat_ref, bias_ref, ga_ref, gg_ref, wout_ref, o_ref, ko_ref, vo_ref, gvo_ref, q_s, k_s,
            v_s, u_s, gv_s, cat_s, tm=tm, tiles_per_seq=tiles_per_seq, n_tiles=n_tiles))


def _mix_prompt_step(s, cur, xa_ref, xc_ref, coff_ref, soff_ref, cbase_ref, sbase_ref, gmix_ref, win_ref, sinks_ref,
                     gvg_ref, wcat_ref, bias_ref, ga_ref, gg_ref, wout_ref, o_ref, ko_ref, vo_ref, gvo_ref, q_s, k_s, v_s,
                     u_s, gv_s, cat_s, *, tm, tiles_per_seq, n_tiles):
    oth = 1 - cur

    h = _rms(xa_ref[0], gmix_ref[...]).astype(BF16)
    tile_in_seq = jnp.minimum(s, n_tiles - 1) % tiles_per_seq
    cb = cbase_ref[pl.ds(tile_in_seq, 1), :]
    sb = sbase_ref[pl.ds(tile_in_seq, 1), :]
    cos = cb * coff_ref[...] - sb * soff_ref[...]
    sin = _sign_sin(sb * coff_ref[...] + cb * soff_ref[...])

    def proj_q():
        z = jnp.dot(h, win_ref[:, 0:Q_W], preferred_element_type=F32)
        for i in range(Q_W // LANES):
            q_s[cur, :, LANES * i:LANES * (i + 1)] = (
                _rope(z[:, LANES * i:LANES * (i + 1)], cos, sin) * Q_SCALE).astype(BF16)

    def proj_kv():
        z = jnp.dot(h, win_ref[:, K_OFF:U_OFF], preferred_element_type=F32)
        k = _rope(z[:, 0:KV_W], cos, sin)
        v = z[:, KV_W:]
        k_s[cur, 0:WINDOW, :] = k_s[oth, tm:tm + WINDOW, :]
        v_s[cur, 0:WINDOW, :] = v_s[oth, tm:tm + WINDOW, :]
        k_s[cur, WINDOW:, :] = k.astype(BF16)
        v_s[cur, WINDOW:, :] = v.astype(BF16)
        ko_ref[0] = k[tm - WINDOW:].T
        vo_ref[0] = v[tm - WINDOW:].T

    def proj_u():
        u_s[cur] = _gelu(jnp.dot(h, win_ref[:, U_OFF:GV_OFF], preferred_element_type=F32))

    def proj_gv():
        gvn = _rms(_gelu(jnp.dot(h, win_ref[:, GV_OFF:], preferred_element_type=F32)), gvg_ref[...])
        gv_s[cur] = gvn.astype(BF16)
        gvo_ref[0] = gvn[tm - CHUNK:].T

    def out_half(c):
        cols = slice(c * (D_MODEL // 2), (c + 1) * (D_MODEL // 2))

        def run():
            o_ref[0, :, cols] = xc_ref[0, :, cols] + jnp.dot(cat_s[cur], wout_ref[:, cols], preferred_element_type=F32)
        return run

    mid_fill = [[out_half(0)], [proj_q], [proj_kv, proj_u], [proj_gv]]
    end_fill = [[], [], [], [out_half(1)]]
    assert len(mid_fill) == len(end_fill) == tm // WINDOW

    wrow = lax.broadcasted_iota(jnp.int32, (CHUNK, 2 * CHUNK), 0)
    wcol = lax.broadcasted_iota(jnp.int32, (CHUNK, 2 * CHUNK), 1) & (CHUNK - 1)
    wmix = [jnp.where(wcol <= wrow, wcat_ref[p], 0.0).astype(BF16) for p in range(G_HEADS // 2)]
    lane = lax.broadcasted_iota(jnp.int32, (CHUNK, LANES), 1)
    low_head = lane < HEAD_DIM

    sj = lax.broadcasted_iota(jnp.int32, (2 * WINDOW, GQA * WINDOW), 0)
    qi = lax.broadcasted_iota(jnp.int32, (2 * WINDOW, GQA * WINDOW), 1) & (WINDOW - 1)
    dist = WINDOW + qi - sj
    band = (dist >= 0) & (dist < WINDOW)
    first_lo = jnp.where((s + tiles_per_seq - 1) % tiles_per_seq == 0, WINDOW, 0)
    lane_head = lax.broadcasted_iota(jnp.int32, (1, GQA * WINDOW), 1) >> (WINDOW.bit_length() - 1)
    sink_rows = []
    for kv in range(N_KV):
        sink = jnp.full((1, GQA * WINDOW), sinks_ref[kv * GQA] * LOG2E, F32)
        for g in range(1, GQA):
            sink = jnp.where(lane_head == g, sinks_ref[kv * GQA + g] * LOG2E, sink)
        sink_rows.append(sink)
    _TN = (((0,), (0,)), ((), ()))

    for j in range(tm // WINDOW):
        rows = slice(j * WINDOW, (j + 1) * WINDOW)
        mask = band & (sj >= first_lo) if j == 0 else band
        qb = q_s[oth, rows, :]
        scores = []
        for kv in range(N_KV):
            qs = jnp.concatenate(
                [qb[:, HEAD_DIM * hd:HEAD_DIM * (hd + 1)] for hd in range(kv * GQA, (kv + 1) * GQA)], axis=0)
            kb = k_s[oth, j * WINDOW:(j + 2) * WINDOW, HEAD_DIM * kv:HEAD_DIM * (kv + 1)]
            scores.append(lax.dot_general(kb, qs, _NT, preferred_element_type=F32))
        for run in mid_fill[j]:
            run()
        outs = []
        for kv in range(N_KV):
            vb = v_s[oth, j * WINDOW:(j + 2) * WINDOW, HEAD_DIM * kv:HEAD_DIM * (kv + 1)]
            st = jnp.where(mask, scores[kv], -jnp.inf)
            m = jnp.maximum(jnp.max(st, axis=0, keepdims=True), sink_rows[kv])
            p = jnp.exp2(st - m)
            den = jnp.sum(p, axis=0, keepdims=True) + jnp.exp2(sink_rows[kv] - m)
            ot = lax.dot_general(vb, p.astype(BF16), _TN, preferred_element_type=F32) / den
            outs += [ot[:, g * WINDOW:(g + 1) * WINDOW].T for g in range(GQA)]
        mixed = []
        for p in range(G_HEADS // 2):
            r = gv_s[oth, rows, LANES * p:LANES * (p + 1)]
            zero = jnp.zeros_like(r)
            rhs = jnp.concatenate([jnp.where(low_head, r, zero), jnp.where(low_head, zero, r)], axis=0)
            mixed.append(jnp.dot(wmix[p], rhs, preferred_element_type=F32))
        for run in end_fill[j]:
            run()
        cat_s[oth, rows, 0:Q_W] = _rms(jnp.concatenate(outs, axis=1), ga_ref[...]).astype(BF16)
        yg = u_s[oth, rows, :] * (jnp.concatenate(mixed, axis=1) + bias_ref[...])
        cat_s[oth, rows, Q_W:] = _rms(yg, gg_ref[...]).astype(BF16)


def _mix_prompt(x, gmix, w_in, sinks, gvg, wcat, bias_full, ga, gg, w_out):
    b, s, _ = x.shape
    tm = TOKEN_TILE
    assert s % tm == 0 and tm % WINDOW == 0
    tiles_per_seq = s // tm
    n_tiles = b * tiles_per_seq
    cos_off, sin_off = _rope_tables(jnp.arange(tm, dtype=jnp.int32))
    cos_base, sin_base = _rope_tables(tm * jnp.arange(tiles_per_seq, dtype=jnp.int32))
    proj_tile = lambda i: jnp.minimum(i, n_tiles - 1)
    out_tile = lambda i: jnp.maximum(i - 2, 0)
    x_tiles = x.reshape(n_tiles, tm, D_MODEL)
    last = lambda width: pl.BlockSpec((1, width, WINDOW), lambda i: (proj_tile(i) // tiles_per_seq, 0, 0))
    out, ko, vo, gvo = pl.pallas_call(
        functools.partial(_mix_prompt_kernel, tm=tm, tiles_per_seq=tiles_per_seq, n_tiles=n_tiles),
        grid=(n_tiles + 2,),
        in_specs=[pl.BlockSpec((1, tm, D_MODEL), lambda i: (proj_tile(i), 0, 0)),
                  pl.BlockSpec((1, tm, D_MODEL), lambda i: (out_tile(i), 0, 0)),
                  _resident((tm, LANES)), _resident((tm, LANES)), _resident((tiles_per_seq, LANES)),
                  _resident((tiles_per_seq, LANES)), _resident((1, D_MODEL)), _resident((D_MODEL, D_IN)),
                  pl.BlockSpec(memory_space=pltpu.SMEM), _resident((1, D_GMLP)),
                  _resident((G_HEADS // 2, CHUNK, 2 * CHUNK)), _resident((CHUNK, D_GMLP)), _resident((1, Q_W)),
                  _resident((1, D_GMLP)), _resident((D_MODEL, D_MODEL))],
        out_specs=[pl.BlockSpec((1, tm, D_MODEL), lambda i: (out_tile(i), 0, 0)), last(KV_W), last(KV_W),
                   last(D_GMLP)],
        out_shape=[jax.ShapeDtypeStruct((n_tiles, tm, D_MODEL), F32), jax.ShapeDtypeStruct((b, KV_W, WINDOW), F32),
                   jax.ShapeDtypeStruct((b, KV_W, WINDOW), F32), jax.ShapeDtypeStruct((b, D_GMLP, CHUNK), F32)],
        scratch_shapes=[pltpu.VMEM((2, tm, Q_W), BF16), pltpu.VMEM((2, tm + WINDOW, KV_W), BF16),
                        pltpu.VMEM((2, tm + WINDOW, KV_W), BF16), pltpu.VMEM((2, tm, D_GMLP), F32),
                        pltpu.VMEM((2, tm, D_GMLP), BF16), pltpu.VMEM((2, tm, D_MODEL), BF16)],
        compiler_params=pltpu.CompilerParams(dimension_semantics=("arbitrary",), vmem_limit_bytes=VMEM_LIMIT_BYTES),
        name="mix_prompt",
    )(x_tiles, x_tiles, cos_off, sin_off, cos_base, sin_base, gmix, w_in, sinks, gvg, wcat, bias_full, ga, gg, w_out)
    return out.reshape(b, s, D_MODEL), ko, vo, gvo


def _mix_sample_kernel(x_ref, ckt_ref, cvt_ref, cos_ref, sin_ref, gmix_ref, win_ref, sinks_ref, gvg_ref, coef_ref,
                       biasr_ref, ga_ref, gg_ref, wout_ref, o_ref, kot_ref, vot_ref, gvo_ref, q_s, k_s, v_s, ya_s,
                       yg_s, *, t_new, w_buf):
    step = pl.program_id(0)
    n_tok = x_ref.shape[0]
    step_seqs = SEQ_GROUP * GROUPS_PER_STEP
    step_rows = step_seqs * t_new
    grp_rows = SEQ_GROUP * t_new
    grp_keys = SEQ_GROUP * w_buf
    sub = 8

    @pl.when(step == 0)
    def _():
        tile = lambda tab: jnp.broadcast_to(tab[None], (n_tok // sub, sub, LANES)).reshape(n_tok, LANES)
        q, k, v, u, gvn = _in_proj(x_ref[...], gmix_ref[...], win_ref[...], tile(cos_ref[...]),
                                   tile(_sign_sin(sin_ref[...])), gvg_ref[...])
        q_s[...] = q
        k_s[...] = k
        v_s[...] = v
        gvo_ref[...] = gvn
        g3 = gvn.reshape(n_tok // sub, sub, D_GMLP)
        trow = lax.broadcasted_iota(jnp.int32, (1, sub, D_GMLP), 1) & (t_new - 1)
        mixed = biasr_ref[...][None] + coef_ref[0][None] * g3
        for d in range(1, t_new):
            shifted = jnp.where(trow >= d, pltpu.roll(g3, d, 1), 0.0)
            mixed = mixed + coef_ref[d][None] * shifted
        yg_s[...] = (u.reshape(n_tok // sub, sub, D_GMLP) * mixed).reshape(n_tok, D_GMLP)

    row0 = pl.multiple_of(step * step_rows, step_rows)
    q_step = q_s[pl.ds(row0, step_rows), :]
    kn = k_s[pl.ds(row0, step_rows), :]
    vn = v_s[pl.ds(row0, step_rows), :]

    knt = kn.T
    vnt = vn.T
    tail = lax.broadcasted_iota(jnp.int32, (KV_W, w_buf), 1) >= w_buf - t_new
    for b in range(step_seqs):
        shift = (w_buf - t_new - t_new * b) % w_buf
        kot_ref[b] = jnp.where(tail, pltpu.roll(knt, shift, 1) if shift else knt,
                               pltpu.roll(ckt_ref[b], w_buf - t_new, 1))
        vot_ref[b] = jnp.where(tail, pltpu.roll(vnt, shift, 1) if shift else vnt,
                               pltpu.roll(cvt_ref[b], w_buf - t_new, 1))

    knb = kn.astype(BF16)
    vnb = vn.astype(BF16)

    n_rows = GQA * grp_rows
    shift_t = t_new.bit_length() - 1
    shift_w = w_buf.bit_length() - 1
    r = lax.broadcasted_iota(jnp.int32, (n_rows, grp_keys), 0)
    c = lax.broadcasted_iota(jnp.int32, (n_rows, grp_keys), 1)
    mask_c = ((c >> shift_w) == ((r & (grp_rows - 1)) >> shift_t)) & ((c & (w_buf - 1)) > (r & (t_new - 1)))
    r2 = lax.broadcasted_iota(jnp.int32, (n_rows, grp_rows), 0)
    c2 = lax.broadcasted_iota(jnp.int32, (n_rows, grp_rows), 1)
    mask_n = ((c2 >> shift_t) == ((r2 & (grp_rows - 1)) >> shift_t)) & ((c2 & (t_new - 1)) <= (r2 & (t_new - 1)))
    row_head = lax.broadcasted_iota(jnp.int32, (n_rows, 1), 0) >> (grp_rows.bit_length() - 1)

    for grp in range(GROUPS_PER_STEP):
        rows = slice(grp * grp_rows, (grp + 1) * grp_rows)
        seqs = range(grp * SEQ_GROUP, (grp + 1) * SEQ_GROUP)
        for kv in range(N_KV):
            heads = [kv * GQA + i for i in range(GQA)]
            lanes = slice(HEAD_DIM * kv, HEAD_DIM * (kv + 1))
            kt = jnp.concatenate([ckt_ref[b, lanes, :] for b in seqs], axis=1).astype(BF16)
            vt = jnp.concatenate([cvt_ref[b, lanes, :] for b in seqs], axis=1).astype(BF16)
            qs = jnp.concatenate([q_step[rows, HEAD_DIM * hd:HEAD_DIM * (hd + 1)] for hd in heads], axis=0)
            s_c = jnp.where(mask_c, jnp.dot(qs, kt, preferred_element_type=F32), -jnp.inf)
            s_n = jnp.where(mask_n, lax.dot_general(qs, knb[rows, lanes], _NT, preferred_element_type=F32), -jnp.inf)
            sink = jnp.full((n_rows, 1), sinks_ref[heads[0]] * LOG2E, F32)
            for i in range(1, GQA):
                sink = jnp.where(row_head == i, sinks_ref[heads[i]] * LOG2E, sink)
            m = jnp.maximum(jnp.maximum(jnp.max(s_c, axis=1, keepdims=True), jnp.max(s_n, axis=1, keepdims=True)),
                            sink)
            p_c = jnp.exp2(s_c - m)
            p_n = jnp.exp2(s_n - m)
            den = jnp.sum(p_c, axis=1, keepdims=True) + jnp.sum(p_n, axis=1, keepdims=True) + jnp.exp2(sink - m)
            o = (lax.dot_general(p_c.astype(BF16), vt, _NT, preferred_element_type=F32)
                 + jnp.dot(p_n.astype(BF16), vnb[rows, lanes], preferred_element_type=F32)) / den
            for i, hd in enumerate(heads):
                ya_s[pl.ds(row0 + grp * grp_rows, grp_rows), HEAD_DIM * hd:HEAD_DIM * (hd + 1)] = (
                    o[i * grp_rows:(i + 1) * grp_rows])

    @pl.when(step == pl.num_programs(0) - 1)
    def _():
        o_ref[...] = _out_proj(x_ref[...], ya_s[...], yg_s[...], ga_ref[...], gg_ref[...], wout_ref[...])


def _mix_sample(x, cache_kt, cache_vt, cos, sin, gmix, w_in, sinks, gvg, coef, bias_rows, ga, gg, w_out, t_new):
    n_tok = x.shape[0]
    n_seq, _, w_buf = cache_kt.shape
    step_seqs = SEQ_GROUP * GROUPS_PER_STEP
    assert n_seq % step_seqs == 0 and n_tok == n_seq * t_new
    assert t_new & (t_new - 1) == 0 and w_buf & (w_buf - 1) == 0 and 8 % t_new == 0
    assert step_seqs * t_new == w_buf == LANES
    cache_spec = pl.BlockSpec((step_seqs, KV_W, w_buf), lambda i: (i, 0, 0))
    return pl.pallas_call(
        functools.partial(_mix_sample_kernel, t_new=t_new, w_buf=w_buf),
        grid=(n_seq // step_seqs,),
        in_specs=[_resident((n_tok, D_MODEL)), cache_spec, cache_spec, _resident((8, LANES)),
                  _resident((8, LANES)), _resident((1, D_MODEL)), _resident((D_MODEL, D_IN)),
                  pl.BlockSpec(memory_space=pltpu.SMEM), _resident((1, D_GMLP)), _resident((t_new, 8, D_GMLP)),
                  _resident((8, D_GMLP)), _resident((1, Q_W)), _resident((1, D_GMLP)),
                  _resident((D_MODEL, D_MODEL))],
        out_specs=[pl.BlockSpec((n_tok, D_MODEL), lambda i: (0, 0)), cache_spec, cache_spec,
                   pl.BlockSpec((n_tok, D_GMLP), lambda i: (0, 0))],
        out_shape=[jax.ShapeDtypeStruct((n_tok, D_MODEL), F32), jax.ShapeDtypeStruct(cache_kt.shape, F32),
                   jax.ShapeDtypeStruct(cache_vt.shape, F32), jax.ShapeDtypeStruct((n_tok, D_GMLP), F32)],
        scratch_shapes=[pltpu.VMEM((n_tok, Q_W), BF16), pltpu.VMEM((n_tok, KV_W), F32), pltpu.VMEM((n_tok, KV_W), F32),
                        pltpu.VMEM((n_tok, Q_W), F32), pltpu.VMEM((n_tok, D_GMLP), F32)],
        compiler_params=pltpu.CompilerParams(dimension_semantics=("arbitrary",), vmem_limit_bytes=VMEM_LIMIT_BYTES),
        name="mix_sample",
    )(x, cache_kt, cache_vt, cos, sin, gmix, w_in, sinks, gvg, coef, bias_rows, ga, gg, w_out)


def _rope_tables(pos):
    inv_freq = ROPE_THETA ** (-jnp.arange(0, HEAD_DIM, 2, dtype=F32) / HEAD_DIM)
    ang = pos.astype(F32)[:, None] * inv_freq[None, :]
    return jnp.tile(jnp.cos(ang), (1, 4)), jnp.tile(jnp.sin(ang), (1, 4))


def kernel(x_prompt, x_sample, cache_k_win, cache_v_win, norm_ffn1, ffn1_gate, ffn1_up, ffn1_down, norm_mix, w_in,
           attn_sinks, gmlp_v_norm, gmlp_w_s, gmlp_b_s, norm_attn_out, norm_gmlp_out, w_out, norm_ffn2, ffn2_gate,
           ffn2_up, ffn2_down, norm_final):
    depth = norm_ffn1.shape[0]
    b, s, _ = x_prompt.shape
    bd, t_new, _ = x_sample.shape
    w_buf = cache_k_win.shape[2]

    cos_s, sin_s = _rope_tables(PAST_LEN + jnp.arange(t_new, dtype=jnp.int32))
    cos_s, sin_s = jnp.tile(cos_s, (8 // t_new, 1)), jnp.tile(sin_s, (8 // t_new, 1))

    hp = x_prompt.reshape(b * s, D_MODEL)
    hs = x_sample.reshape(bd * t_new, D_MODEL)
    outs = [[] for _ in range(6)]
    ffn1_w = [ffn1_gate[0], ffn1_up[0], ffn1_down[0]]
    for l in range(depth):
        last = l == depth - 1
        row = lambda a: a[l].reshape(1, -1)

        wcat = gmlp_w_s[l].reshape(G_HEADS // 2, 2, CHUNK, CHUNK).transpose(0, 2, 1, 3).reshape(
            G_HEADS // 2, CHUNK, 2 * CHUNK)
        bias_full = jnp.repeat(gmlp_b_s[l].T, HEAD_DIM, axis=1)
        ws_small = gmlp_w_s[l][:, :t_new, :t_new]
        coef = jnp.stack([
            jnp.stack([ws_small[:, t, t - d] if t >= d else jnp.zeros((G_HEADS,), F32) for t in range(t_new)])
            for d in range(t_new)])
        coef = jnp.tile(jnp.repeat(coef, HEAD_DIM, axis=2), (1, 8 // t_new, 1))
        bias_rows = jnp.tile(jnp.repeat(gmlp_b_s[l][:, :t_new].T, HEAD_DIM, axis=1), (8 // t_new, 1))

        hp, hs, (w_in_b, w_out_b, wg2, wu2, wd2) = _ffn_half(
            hp, hs, norm_ffn1[l], *ffn1_w, cast=(w_in[l], w_out[l], ffn2_gate[l], ffn2_up[l], ffn2_down[l]))

        hp, kpt, vpt, gvpt = _mix_prompt(hp.reshape(b, s, D_MODEL), row(norm_mix), w_in_b, attn_sinks[l],
                                         row(gmlp_v_norm), wcat, bias_full, row(norm_attn_out), row(norm_gmlp_out),
                                         w_out_b)
        to_t = lambda c: c.transpose(0, 2, 3, 1).reshape(bd, KV_W, w_buf)
        hs, kst, vst, gvs = _mix_sample(hs, to_t(cache_k_win[l]), to_t(cache_v_win[l]), cos_s, sin_s, row(norm_mix),
                                        w_in_b, attn_sinks[l], row(gmlp_v_norm), coef, bias_rows, row(norm_attn_out),
                                        row(norm_gmlp_out), w_out_b, t_new)
        from_t = lambda c: c.reshape(bd, N_KV, HEAD_DIM, w_buf).transpose(0, 3, 1, 2)

        next_ffn1 = () if last else (ffn1_gate[l + 1], ffn1_up[l + 1], ffn1_down[l + 1])
        hp, hs, ffn1_w = _ffn_half(hp.reshape(b * s, D_MODEL), hs, norm_ffn2[l], wg2, wu2, wd2,
                                   gf=norm_final if last else None, cast=next_ffn1)

        outs[0].append(kpt.reshape(b, N_KV, HEAD_DIM, WINDOW).transpose(0, 3, 1, 2))
        outs[1].append(vpt.reshape(b, N_KV, HEAD_DIM, WINDOW).transpose(0, 3, 1, 2))
        outs[2].append(from_t(kst))
        outs[3].append(from_t(vst))
        outs[4].append(gvpt.reshape(b, G_HEADS, D_GMLP // G_HEADS, CHUNK).transpose(0, 3, 1, 2))
        outs[5].append(gvs.reshape(bd, t_new, G_HEADS, D_GMLP // G_HEADS))

    return (hp.reshape(b, s, D_MODEL), hs.reshape(bd, t_new, D_MODEL)) + tuple(jnp.stack(o) for o in outs)
```

```python
import functools

import jax
import jax.numpy as jnp
from jax import lax
from jax.experimental import pallas as pl
from jax.experimental.pallas import tpu as pltpu

F32 = jnp.float32
BF16 = jnp.bfloat16

D_MODEL = 1024
D_FF = 2816
HEAD_DIM = 64
N_HEADS = 8
N_KV = 2
GQA = N_HEADS // N_KV
WINDOW = 128
CHUNK = 128
G_HEADS = 8
Q_W = N_HEADS * HEAD_DIM
KV_W = N_KV * HEAD_DIM
D_GMLP = 512
D_IN = Q_W + 2 * KV_W + 2 * D_GMLP
K_OFF = Q_W
V_OFF = K_OFF + KV_W
U_OFF = V_OFF + KV_W
GV_OFF = U_OFF + D_GMLP
ROPE_THETA = 10000.0
PAST_LEN = 16384
EPS = 1e-6
LOG2E = 1.4426950408889634
Q_SCALE = HEAD_DIM ** -0.5 * LOG2E
LANES = 128
BF16_SUBLANES = 16

TOKEN_TILE = 512
FF_CHUNK = 256
STAGE_ROWS = 64
STAGE_SLOTS = 8
SEQ_GROUP = 8
GROUPS_PER_STEP = 4
VMEM_LIMIT_BYTES = 56 * 1024 * 1024

_NT = (((1,), (1,)), ((), ()))


def _rms(x, g):
    ms = jnp.mean(x * x, axis=-1, keepdims=True)
    return (x * lax.rsqrt(ms + EPS)) * g


def _resident(shape):
    zeros = (0,) * len(shape)
    return pl.BlockSpec(shape, lambda *_: zeros, pipeline_mode=pl.Buffered(1))


def _round_rows_to_bf16(src_hbm, dst_ref, stage_ref, sem):
    n_slots, rows, _ = stage_ref.shape
    n_blocks = src_hbm.shape[0] // rows
    assert n_blocks >= n_slots

    def fetch(i, slot):
        return pltpu.make_async_copy(src_hbm.at[pl.ds(i * rows, rows), :], stage_ref.at[slot], sem.at[slot])

    for i in range(n_slots - 1):
        fetch(i, i).start()

    def body(i, carry):
        slot = i % n_slots

        @pl.when(i + n_slots - 1 < n_blocks)
        def _():
            fetch(i + n_slots - 1, (i + n_slots - 1) % n_slots).start()

        fetch(i, slot).wait()
        dst_ref[pl.ds(pl.multiple_of(i * rows, rows), rows), :] = stage_ref[slot].astype(BF16)
        return carry

    lax.fori_loop(0, n_blocks, body, 0)


def _ffn_kernel(*refs, final_norm, n_cast, f32_weights):
    refs = iter(refs)
    xp_ref, xn_ref, xs_ref, g_ref, wg_in, wu_in, wd_in = (next(refs) for _ in range(7))
    gf_ref = next(refs) if final_norm else None
    cast_in = [next(refs) for _ in range(n_cast)]
    yp_ref, ys_ref = next(refs), next(refs)
    cast_out = [next(refs) for _ in range(n_cast)]
    h_ref, act0_ref, act_ref = next(refs), next(refs), next(refs)
    if f32_weights:
        wg_ref, wu_ref, wd_ref, wide_stage, narrow_stage, wide_sem, narrow_sem = (next(refs) for _ in range(7))
    else:
        wg_ref, wu_ref, wd_ref = wg_in, wu_in, wd_in

    def gate_up(h, c):
        sl = slice(c * FF_CHUNK, (c + 1) * FF_CHUNK)
        gate = jnp.dot(h, wg_ref[:, sl], preferred_element_type=F32)
        up = jnp.dot(h, wu_ref[:, sl], preferred_element_type=F32)
        return (gate * jax.nn.sigmoid(gate) * up).astype(BF16)

    on_sample = pl.program_id(0) == 0

    @pl.when(on_sample)
    def _():
        if f32_weights:
            _round_rows_to_bf16(wg_in, wg_ref, wide_stage, wide_sem)
            _round_rows_to_bf16(wu_in, wu_ref, wide_stage, wide_sem)
            _round_rows_to_bf16(wd_in, wd_ref, narrow_stage, narrow_sem)
        h0 = _rms(xs_ref[...], g_ref[...]).astype(BF16)
        h_ref[...] = h0
        act0_ref[...] = gate_up(h0, 0)

    for c in range(1, D_FF // FF_CHUNK):
        act_ref[:, (c - 1) * FF_CHUNK:c * FF_CHUNK] = gate_up(h_ref[...], c)
    down0 = jnp.dot(act0_ref[...], wd_ref[0:FF_CHUNK, :], preferred_element_type=F32)
    hn = _rms(xn_ref[...], g_ref[...]).astype(BF16)
    h_ref[...] = hn
    act0_ref[...] = gate_up(hn, 0)
    if final_norm:
        x = jnp.where(on_sample, xs_ref[...], xp_ref[...])
        y = x + 0.5 * (down0 + jnp.dot(act_ref[...], wd_ref[FF_CHUNK:, :], preferred_element_type=F32))
        yp_ref[...] = _rms(y, gf_ref[...])
    else:
        for cols in (slice(0, D_MODEL // 2), slice(D_MODEL // 2, D_MODEL)):
            x = jnp.where(on_sample, xs_ref[:, cols], xp_ref[:, cols])
            yp_ref[:, cols] = x + 0.5 * (down0[:, cols] + jnp.dot(act_ref[...], wd_ref[FF_CHUNK:, cols],
                                                                   preferred_element_type=F32))

    @pl.when(on_sample)
    def _():
        ys_ref[...] = yp_ref[...]

    for src, dst in zip(cast_in, cast_out):
        dst[...] = src[...].astype(BF16)


def _cast_row_blocks(rows, n_steps):
    return max(d for d in range(1, n_steps + 1) if rows % d == 0 and (rows // d) % BF16_SUBLANES == 0)


def _ffn_half(xp, xs, g, wg, wu, wd, gf=None, cast=()):
    tm = TOKEN_TILE
    f32_weights = wg.dtype == F32
    assert all(w.dtype == wg.dtype for w in (wu, wd))
    n = xp.shape[0]
    assert n % tm == 0 and xs.shape[0] == tm
    n_tiles = n // tm
    prompt_spec = pl.BlockSpec((tm, D_MODEL), lambda i: (jnp.maximum(i - 1, 0), 0))
    next_spec = pl.BlockSpec((tm, D_MODEL), lambda i: (jnp.minimum(i, n_tiles - 1), 0))
    sample_in = _resident((tm, D_MODEL))
    sample_out = pl.BlockSpec((tm, D_MODEL), lambda i: (0, 0))
    weight_specs = ([pl.BlockSpec(memory_space=pl.ANY)] * 3 if f32_weights else
                    [_resident((D_MODEL, D_FF)), _resident((D_MODEL, D_FF)), _resident((D_FF, D_MODEL))])
    in_specs = [prompt_spec, next_spec, sample_in, _resident((1, D_MODEL))] + weight_specs
    args = [xp, xp, xs, g.reshape(1, D_MODEL), wg, wu, wd]
    if gf is not None:
        in_specs.append(_resident((1, D_MODEL)))
        args.append(gf.reshape(1, D_MODEL))
    out_specs = [prompt_spec, sample_out]
    out_shape = [jax.ShapeDtypeStruct((n, D_MODEL), F32), jax.ShapeDtypeStruct((tm, D_MODEL), F32)]
    cast_specs = []
    for w in cast:
        rows, cols = w.shape
        nb = _cast_row_blocks(rows, n_tiles)
        cast_specs.append(pl.BlockSpec((rows // nb, cols), lambda i, nb=nb: (jnp.minimum(i, nb - 1), 0)))
        out_shape.append(jax.ShapeDtypeStruct(w.shape, BF16))
    outs = pl.pallas_call(
        functools.partial(_ffn_kernel, final_norm=gf is not None, n_cast=len(cast), f32_weights=f32_weights),
        grid=(n_tiles + 1,),
        in_specs=in_specs + cast_specs,
        out_specs=out_specs + cast_specs,
        out_shape=out_shape,
        scratch_shapes=[pltpu.VMEM((tm, D_MODEL), BF16), pltpu.VMEM((tm, FF_CHUNK), BF16),
                        pltpu.VMEM((tm, D_FF - FF_CHUNK), BF16)] + ([
                            pltpu.VMEM((D_MODEL, D_FF), BF16), pltpu.VMEM((D_MODEL, D_FF), BF16),
                            pltpu.VMEM((D_FF, D_MODEL), BF16), pltpu.VMEM((STAGE_SLOTS, STAGE_ROWS, D_FF), F32),
                            pltpu.VMEM((STAGE_SLOTS, STAGE_ROWS, D_MODEL), F32),
                            pltpu.SemaphoreType.DMA((STAGE_SLOTS,)), pltpu.SemaphoreType.DMA((STAGE_SLOTS,))]
                        if f32_weights else []),
        compiler_params=pltpu.CompilerParams(dimension_semantics=("arbitrary",), vmem_limit_bytes=VMEM_LIMIT_BYTES),
        name="ffn_final" if gf is not None else "ffn_half",
    )(*args, *cast)
    return outs[0], outs[1], list(outs[2:])


def _first_half(shape):
    return (lax.broadcasted_iota(jnp.int32, shape, 1) & (HEAD_DIM - 1)) < HEAD_DIM // 2


def _sign_sin(sin):
    return jnp.where(_first_half(sin.shape), -sin, sin)


def _rope(xg, cos, sin_signed):
    swapped = jnp.where(_first_half(xg.shape), pltpu.roll(xg, LANES - HEAD_DIM // 2, 1),
                        pltpu.roll(xg, HEAD_DIM // 2, 1))
    return xg * cos + swapped * sin_signed


def _in_proj(x, gmix, w_in, cos, sin_signed, gvn_gain):
    h = _rms(x, gmix).astype(BF16)
    z = jnp.dot(h, w_in, preferred_element_type=F32)
    q = jnp.concatenate(
        [(_rope(z[:, LANES * i:LANES * (i + 1)], cos, sin_signed) * Q_SCALE).astype(BF16) for i in range(Q_W // LANES)],
        axis=1)
    k = _rope(z[:, K_OFF:V_OFF], cos, sin_signed)
    v = z[:, V_OFF:U_OFF]
    u = _gelu(z[:, U_OFF:GV_OFF])
    gvn = _rms(_gelu(z[:, GV_OFF:]), gvn_gain)
    return q, k, v, u, gvn


def _out_proj(x, ya, yg, ga, gg, w_out):
    cat = jnp.concatenate([_rms(ya, ga).astype(BF16), _rms(yg, gg).astype(BF16)], axis=1)
    return x + jnp.dot(cat, w_out, preferred_element_type=F32)


def _softmax_sink(s, sink):
    m = jnp.maximum(jnp.max(s, axis=1, keepdims=True), sink)
    p = jnp.exp2(s - m)
    return p, jnp.sum(p, axis=1, keepdims=True) + jnp.exp2(sink - m)


def _gelu(x):
    k = -2.0 * (2.0 / jnp.pi) ** 0.5 * LOG2E
    return x / (1.0 + jnp.exp2(x * (k + (0.044715 * k) * (x * x))))


def _mix_prompt_kernel(xa_ref, xc_ref, coff_ref, soff_ref, cbase_ref, sbase_ref, gmix_ref, win_ref, sinks_ref, gvg_ref,
                       wcat_ref, bias_ref, ga_ref, gg_ref, wout_ref, o_ref, ko_ref, vo_ref, gvo_ref, q_s, k_s, v_s, u_s,
                       gv_s, cat_s, *, tm, tiles_per_seq, n_tiles):
    s = pl.program_id(0)

    @pl.when(s == 0)
    def _():
        q_s[1] = jnp.zeros((tm, Q_W), BF16)
        k_s[1] = jnp.zeros((tm + WINDOW, KV_W), BF16)
        v_s[1] = jnp.zeros((tm + WINDOW, KV_W), BF16)
        u_s[1] = jnp.zeros((tm, D_GMLP), F32)
        gv_s[1] = jnp.zeros((tm, D_GMLP), BF16)
        cat_s[0] = jnp.zeros((tm, D_MODEL), BF16)

    for parity in range(2):
        pl.when(s % 2 == parity)(functools.partial(
            _mix_prompt_step, s, parity, xa_ref, xc_ref, coff_ref, soff_ref, cbase_ref, sbase_ref, gmix_ref, win_ref,
            sinks_ref, gvg_ref, wcat_ref, bias_ref, ga_ref, gg_ref, wout_ref, o_ref, ko_ref, vo_ref, gvo_ref, q_s, k_s,
            v_s, u_s, gv_s, cat_s, tm=tm, tiles_per_seq=tiles_per_seq, n_tiles=n_tiles))


def _mix_prompt_step(s, cur, xa_ref, xc_ref, coff_ref, soff_ref, cbase_ref, sbase_ref, gmix_ref, win_ref, sinks_ref,
                     gvg_ref, wcat_ref, bias_ref, ga_ref, gg_ref, wout_ref, o_ref, ko_ref, vo_ref, gvo_ref, q_s, k_s, v_s,
                     u_s, gv_s, cat_s, *, tm, tiles_per_seq, n_tiles):
    oth = 1 - cur

    h = _rms(xa_ref[0], gmix_ref[...]).astype(BF16)
    tile_in_seq = jnp.minimum(s, n_tiles - 1) % tiles_per_seq
    cb = cbase_ref[pl.ds(tile_in_seq, 1), :]
    sb = sbase_ref[pl.ds(tile_in_seq, 1), :]
    cos = cb * coff_ref[...] - sb * soff_ref[...]
    sin = _sign_sin(sb * coff_ref[...] + cb * soff_ref[...])

    def proj_q():
        z = jnp.dot(h, win_ref[:, 0:Q_W], preferred_element_type=F32)
        for i in range(Q_W // LANES):
            q_s[cur, :, LANES * i:LANES * (i + 1)] = (
                _rope(z[:, LANES * i:LANES * (i + 1)], cos, sin) * Q_SCALE).astype(BF16)

    def proj_kv():
        z = jnp.dot(h, win_ref[:, K_OFF:U_OFF], preferred_element_type=F32)
        k = _rope(z[:, 0:KV_W], cos, sin)
        v = z[:, KV_W:]
        k_s[cur, 0:WINDOW, :] = k_s[oth, tm:tm + WINDOW, :]
        v_s[cur, 0:WINDOW, :] = v_s[oth, tm:tm + WINDOW, :]
        k_s[cur, WINDOW:, :] = k.astype(BF16)
        v_s[cur, WINDOW:, :] = v.astype(BF16)
        ko_ref[0] = k[tm - WINDOW:].T
        vo_ref[0] = v[tm - WINDOW:].T

    def proj_u():
        u_s[cur] = _gelu(jnp.dot(h, win_ref[:, U_OFF:GV_OFF], preferred_element_type=F32))

    def proj_gv():
        gvn = _rms(_gelu(jnp.dot(h, win_ref[:, GV_OFF:], preferred_element_type=F32)), gvg_ref[...])
        gv_s[cur] = gvn.astype(BF16)
        gvo_ref[0] = gvn[tm - CHUNK:].T

    def out_half(c):
        cols = slice(c * (D_MODEL // 2), (c + 1) * (D_MODEL // 2))

        def run():
            o_ref[0, :, cols] = xc_ref[0, :, cols] + jnp.dot(cat_s[cur], wout_ref[:, cols], preferred_element_type=F32)
        return run

    mid_fill = [[out_half(0)], [proj_q], [proj_kv, proj_u], [proj_gv]]
    end_fill = [[], [], [], [out_half(1)]]
    assert len(mid_fill) == len(end_fill) == tm // WINDOW

    wrow = lax.broadcasted_iota(jnp.int32, (CHUNK, CHUNK), 0)
    wcol = lax.broadcasted_iota(jnp.int32, (CHUNK, CHUNK), 1)
    wtril = [jnp.where(wcol <= wrow, wcat_ref[hd], 0.0).astype(BF16) for hd in range(G_HEADS)]
    wmix = [jnp.concatenate(wtril[2 * p:2 * p + 2], axis=1) for p in range(G_HEADS // 2)]
    lane = lax.broadcasted_iota(jnp.int32, (CHUNK, LANES), 1)
    low_head = lane < HEAD_DIM

    qi = lax.broadcasted_iota(jnp.int32, (WINDOW, 2 * WINDOW), 0)
    sj = lax.broadcasted_iota(jnp.int32, (WINDOW, 2 * WINDOW), 1)
    dist = WINDOW + qi - sj
    band = (dist >= 0) & (dist < WINDOW)
    first_lo = jnp.where((s + tiles_per_seq - 1) % tiles_per_seq == 0, WINDOW, 0)

    for j in range(tm // WINDOW):
        rows = slice(j * WINDOW, (j + 1) * WINDOW)
        mask = band & (sj >= first_lo) if j == 0 else band
        qb = q_s[oth, rows, :]
        scores = []
        for kv in range(N_KV):
            qs = jnp.concatenate(
                [qb[:, HEAD_DIM * hd:HEAD_DIM * (hd + 1)] for hd in range(kv * GQA, (kv + 1) * GQA)], axis=0)
            kb = k_s[oth, j * WINDOW:(j + 2) * WINDOW, HEAD_DIM * kv:HEAD_DIM * (kv + 1)]
            scores.append(lax.dot_general(qs, kb, _NT, preferred_element_type=F32))
        for run in mid_fill[j]:
            run()
        outs, dens = [], []
        for kv in range(N_KV):
            vb = v_s[oth, j * WINDOW:(j + 2) * WINDOW, HEAD_DIM * kv:HEAD_DIM * (kv + 1)]
            ps = []
            for g in range(GQA):
                sg = jnp.where(mask, scores[kv][g * WINDOW:(g + 1) * WINDOW], -jnp.inf)
                p, den = _softmax_sink(sg, sinks_ref[kv * GQA + g] * LOG2E)
                ps.append(p.astype(BF16))
                dens.append(den)
            outs.append(jnp.dot(jnp.concatenate(ps, axis=0), vb, preferred_element_type=F32))
        mixed = []
        for p in range(G_HEADS // 2):
            r = gv_s[oth, rows, LANES * p:LANES * (p + 1)]
            zero = jnp.zeros_like(r)
            rhs = jnp.concatenate([jnp.where(low_head, r, zero), jnp.where(low_head, zero, r)], axis=0)
            mixed.append(jnp.dot(wmix[p], rhs, preferred_element_type=F32))
        for run in end_fill[j]:
            run()
        ya = [outs[hd // GQA][(hd % GQA) * WINDOW:(hd % GQA + 1) * WINDOW] / dens[hd] for hd in range(N_HEADS)]
        cat_s[oth, rows, 0:Q_W] = _rms(jnp.concatenate(ya, axis=1), ga_ref[...]).astype(BF16)
        yg = u_s[oth, rows, :] * (jnp.concatenate(mixed, axis=1) + bias_ref[...])
        cat_s[oth, rows, Q_W:] = _rms(yg, gg_ref[...]).astype(BF16)


def _mix_prompt(x, gmix, w_in, sinks, gvg, wcat, bias_full, ga, gg, w_out):
    b, s, _ = x.shape
    tm = TOKEN_TILE
    assert s % tm == 0 and tm % WINDOW == 0
    tiles_per_seq = s // tm
    n_tiles = b * tiles_per_seq
    cos_off, sin_off = _rope_tables(jnp.arange(tm, dtype=jnp.int32))
    cos_base, sin_base = _rope_tables(tm * jnp.arange(tiles_per_seq, dtype=jnp.int32))
    proj_tile = lambda i: jnp.minimum(i, n_tiles - 1)
    out_tile = lambda i: jnp.maximum(i - 2, 0)
    x_tiles = x.reshape(n_tiles, tm, D_MODEL)
    last = lambda width: pl.BlockSpec((1, width, WINDOW), lambda i: (proj_tile(i) // tiles_per_seq, 0, 0))
    out, ko, vo, gvo = pl.pallas_call(
        functools.partial(_mix_prompt_kernel, tm=tm, tiles_per_seq=tiles_per_seq, n_tiles=n_tiles),
        grid=(n_tiles + 2,),
        in_specs=[pl.BlockSpec((1, tm, D_MODEL), lambda i: (proj_tile(i), 0, 0)),
                  pl.BlockSpec((1, tm, D_MODEL), lambda i: (out_tile(i), 0, 0)),
                  _resident((tm, LANES)), _resident((tm, LANES)), _resident((tiles_per_seq, LANES)),
                  _resident((tiles_per_seq, LANES)), _resident((1, D_MODEL)), _resident((D_MODEL, D_IN)),
                  pl.BlockSpec(memory_space=pltpu.SMEM), _resident((1, D_GMLP)),
                  _resident((G_HEADS, CHUNK, CHUNK)), _resident((CHUNK, D_GMLP)), _resident((1, Q_W)),
                  _resident((1, D_GMLP)), _resident((D_MODEL, D_MODEL))],
        out_specs=[pl.BlockSpec((1, tm, D_MODEL), lambda i: (out_tile(i), 0, 0)), last(KV_W), last(KV_W),
                   last(D_GMLP)],
        out_shape=[jax.ShapeDtypeStruct((n_tiles, tm, D_MODEL), F32), jax.ShapeDtypeStruct((b, KV_W, WINDOW), F32),
                   jax.ShapeDtypeStruct((b, KV_W, WINDOW), F32), jax.ShapeDtypeStruct((b, D_GMLP, CHUNK), F32)],
        scratch_shapes=[pltpu.VMEM((2, tm, Q_W), BF16), pltpu.VMEM((2, tm + WINDOW, KV_W), BF16),
                        pltpu.VMEM((2, tm + WINDOW, KV_W), BF16), pltpu.VMEM((2, tm, D_GMLP), F32),
                        pltpu.VMEM((2, tm, D_GMLP), BF16), pltpu.VMEM((2, tm, D_MODEL), BF16)],
        compiler_params=pltpu.CompilerParams(dimension_semantics=("arbitrary",), vmem_limit_bytes=VMEM_LIMIT_BYTES),
        name="mix_prompt",
    )(x_tiles, x_tiles, cos_off, sin_off, cos_base, sin_base, gmix, w_in, sinks, gvg, wcat, bias_full, ga, gg, w_out)
    return out.reshape(b, s, D_MODEL), ko, vo, gvo


def _mix_sample_kernel(x_ref, ckt_ref, cvt_ref, cos_ref, sin_ref, gmix_ref, win_ref, sinks_ref, gvg_ref, coef_ref,
                       biasr_ref, ga_ref, gg_ref, wout_ref, o_ref, kot_ref, vot_ref, gvo_ref, q_s, k_s, v_s, ya_s,
                       yg_s, *, t_new, w_buf):
    step = pl.program_id(0)
    n_tok = x_ref.shape[0]
    step_seqs = SEQ_GROUP * GROUPS_PER_STEP
    step_rows = step_seqs * t_new
    grp_rows = SEQ_GROUP * t_new
    grp_keys = SEQ_GROUP * w_buf
    sub = 8

    @pl.when(step == 0)
    def _():
        tile = lambda tab: jnp.broadcast_to(tab[None], (n_tok // sub, sub, LANES)).reshape(n_tok, LANES)
        q, k, v, u, gvn = _in_proj(x_ref[...], gmix_ref[...], win_ref[...], tile(cos_ref[...]),
                                   tile(_sign_sin(sin_ref[...])), gvg_ref[...])
        q_s[...] = q
        k_s[...] = k
        v_s[...] = v
        gvo_ref[...] = gvn
        g3 = gvn.reshape(n_tok // sub, sub, D_GMLP)
        trow = lax.broadcasted_iota(jnp.int32, (1, sub, D_GMLP), 1) & (t_new - 1)
        mixed = biasr_ref[...][None] + coef_ref[0][None] * g3
        for d in range(1, t_new):
            shifted = jnp.where(trow >= d, pltpu.roll(g3, d, 1), 0.0)
            mixed = mixed + coef_ref[d][None] * shifted
        yg_s[...] = (u.reshape(n_tok // sub, sub, D_GMLP) * mixed).reshape(n_tok, D_GMLP)

    row0 = pl.multiple_of(step * step_rows, step_rows)
    q_step = q_s[pl.ds(row0, step_rows), :]
    kn = k_s[pl.ds(row0, step_rows), :]
    vn = v_s[pl.ds(row0, step_rows), :]

    knt = kn.T
    vnt = vn.T
    tail = lax.broadcasted_iota(jnp.int32, (KV_W, w_buf), 1) >= w_buf - t_new
    for b in range(step_seqs):
        shift = (w_buf - t_new - t_new * b) % w_buf
        kot_ref[b] = jnp.where(tail, pltpu.roll(knt, shift, 1) if shift else knt,
                               pltpu.roll(ckt_ref[b], w_buf - t_new, 1))
        vot_ref[b] = jnp.where(tail, pltpu.roll(vnt, shift, 1) if shift else vnt,
                               pltpu.roll(cvt_ref[b], w_buf - t_new, 1))

    knb = kn.astype(BF16)
    vnb = vn.astype(BF16)

    n_rows = GQA * grp_rows
    shift_t = t_new.bit_length() - 1
    shift_w = w_buf.bit_length() - 1
    r = lax.broadcasted_iota(jnp.int32, (n_rows, grp_keys), 0)
    c = lax.broadcasted_iota(jnp.int32, (n_rows, grp_keys), 1)
    mask_c = ((c >> shift_w) == ((r & (grp_rows - 1)) >> shift_t)) & ((c & (w_buf - 1)) > (r & (t_new - 1)))
    r2 = lax.broadcasted_iota(jnp.int32, (n_rows, grp_rows), 0)
    c2 = lax.broadcasted_iota(jnp.int32, (n_rows, grp_rows), 1)
    mask_n = ((c2 >> shift_t) == ((r2 & (grp_rows - 1)) >> shift_t)) & ((c2 & (t_new - 1)) <= (r2 & (t_new - 1)))
    row_head = lax.broadcasted_iota(jnp.int32, (n_rows, 1), 0) >> (grp_rows.bit_length() - 1)

    for grp in range(GROUPS_PER_STEP):
        rows = slice(grp * grp_rows, (grp + 1) * grp_rows)
        seqs = range(grp * SEQ_GROUP, (grp + 1) * SEQ_GROUP)
        for kv in range(N_KV):
            heads = [kv * GQA + i for i in range(GQA)]
            lanes = slice(HEAD_DIM * kv, HEAD_DIM * (kv + 1))
            kt = jnp.concatenate([ckt_ref[b, lanes, :] for b in seqs], axis=1).astype(BF16)
            vt = jnp.concatenate([cvt_ref[b, lanes, :] for b in seqs], axis=1).astype(BF16)
            qs = jnp.concatenate([q_step[rows, HEAD_DIM * hd:HEAD_DIM * (hd + 1)] for hd in heads], axis=0)
            s_c = jnp.where(mask_c, jnp.dot(qs, kt, preferred_element_type=F32), -jnp.inf)
            s_n = jnp.where(mask_n, lax.dot_general(qs, knb[rows, lanes], _NT, preferred_element_type=F32), -jnp.inf)
            sink = jnp.full((n_rows, 1), sinks_ref[heads[0]] * LOG2E, F32)
            for i in range(1, GQA):
                sink = jnp.where(row_head == i, sinks_ref[heads[i]] * LOG2E, sink)
            m = jnp.maximum(jnp.maximum(jnp.max(s_c, axis=1, keepdims=True), jnp.max(s_n, axis=1, keepdims=True)),
                            sink)
            p_c = jnp.exp2(s_c - m)
            p_n = jnp.exp2(s_n - m)
            den = jnp.sum(p_c, axis=1, keepdims=True) + jnp.sum(p_n, axis=1, keepdims=True) + jnp.exp2(sink - m)
            o = (lax.dot_general(p_c.astype(BF16), vt, _NT, preferred_element_type=F32)
                 + jnp.dot(p_n.astype(BF16), vnb[rows, lanes], preferred_element_type=F32)) / den
            for i, hd in enumerate(heads):
                ya_s[pl.ds(row0 + grp * grp_rows, grp_rows), HEAD_DIM * hd:HEAD_DIM * (hd + 1)] = (
                    o[i * grp_rows:(i + 1) * grp_rows])

    @pl.when(step == pl.num_programs(0) - 1)
    def _():
        o_ref[...] = _out_proj(x_ref[...], ya_s[...], yg_s[...], ga_ref[...], gg_ref[...], wout_ref[...])


def _mix_sample(x, cache_kt, cache_vt, cos, sin, gmix, w_in, sinks, gvg, coef, bias_rows, ga, gg, w_out, t_new):
    n_tok = x.shape[0]
    n_seq, _, w_buf = cache_kt.shape
    step_seqs = SEQ_GROUP * GROUPS_PER_STEP
    assert n_seq % step_seqs == 0 and n_tok == n_seq * t_new
    assert t_new & (t_new - 1) == 0 and w_buf & (w_buf - 1) == 0 and 8 % t_new == 0
    assert step_seqs * t_new == w_buf == LANES
    cache_spec = pl.BlockSpec((step_seqs, KV_W, w_buf), lambda i: (i, 0, 0))
    return pl.pallas_call(
        functools.partial(_mix_sample_kernel, t_new=t_new, w_buf=w_buf),
        grid=(n_seq // step_seqs,),
        in_specs=[_resident((n_tok, D_MODEL)), cache_spec, cache_spec, _resident((8, LANES)),
                  _resident((8, LANES)), _resident((1, D_MODEL)), _resident((D_MODEL, D_IN)),
                  pl.BlockSpec(memory_space=pltpu.SMEM), _resident((1, D_GMLP)), _resident((t_new, 8, D_GMLP)),
                  _resident((8, D_GMLP)), _resident((1, Q_W)), _resident((1, D_GMLP)),
                  _resident((D_MODEL, D_MODEL))],
        out_specs=[pl.BlockSpec((n_tok, D_MODEL), lambda i: (0, 0)), cache_spec, cache_spec,
                   pl.BlockSpec((n_tok, D_GMLP), lambda i: (0, 0))],
        out_shape=[jax.ShapeDtypeStruct((n_tok, D_MODEL), F32), jax.ShapeDtypeStruct(cache_kt.shape, F32),
                   jax.ShapeDtypeStruct(cache_vt.shape, F32), jax.ShapeDtypeStruct((n_tok, D_GMLP), F32)],
        scratch_shapes=[pltpu.VMEM((n_tok, Q_W), BF16), pltpu.VMEM((n_tok, KV_W), F32), pltpu.VMEM((n_tok, KV_W), F32),
                        pltpu.VMEM((n_tok, Q_W), F32), pltpu.VMEM((n_tok, D_GMLP), F32)],
        compiler_params=pltpu.CompilerParams(dimension_semantics=("arbitrary",), vmem_limit_bytes=VMEM_LIMIT_BYTES),
        name="mix_sample",
    )(x, cache_kt, cache_vt, cos, sin, gmix, w_in, sinks, gvg, coef, bias_rows, ga, gg, w_out)


def _rope_tables(pos):
    inv_freq = ROPE_THETA ** (-jnp.arange(0, HEAD_DIM, 2, dtype=F32) / HEAD_DIM)
    ang = pos.astype(F32)[:, None] * inv_freq[None, :]
    return jnp.tile(jnp.cos(ang), (1, 4)), jnp.tile(jnp.sin(ang), (1, 4))


def kernel(x_prompt, x_sample, cache_k_win, cache_v_win, norm_ffn1, ffn1_gate, ffn1_up, ffn1_down, norm_mix, w_in,
           attn_sinks, gmlp_v_norm, gmlp_w_s, gmlp_b_s, norm_attn_out, norm_gmlp_out, w_out, norm_ffn2, ffn2_gate,
           ffn2_up, ffn2_down, norm_final):
    depth = norm_ffn1.shape[0]
    b, s, _ = x_prompt.shape
    bd, t_new, _ = x_sample.shape
    w_buf = cache_k_win.shape[2]

    cos_s, sin_s = _rope_tables(PAST_LEN + jnp.arange(t_new, dtype=jnp.int32))
    cos_s, sin_s = jnp.tile(cos_s, (8 // t_new, 1)), jnp.tile(sin_s, (8 // t_new, 1))

    hp = x_prompt.reshape(b * s, D_MODEL)
    hs = x_sample.reshape(bd * t_new, D_MODEL)
    outs = [[] for _ in range(6)]
    ffn1_w = [ffn1_gate[0], ffn1_up[0], ffn1_down[0]]
    for l in range(depth):
        last = l == depth - 1
        row = lambda a: a[l].reshape(1, -1)

        wcat = gmlp_w_s[l]
        bias_full = jnp.repeat(gmlp_b_s[l].T, HEAD_DIM, axis=1)
        ws_small = gmlp_w_s[l][:, :t_new, :t_new]
        coef = jnp.stack([
            jnp.stack([ws_small[:, t, t - d] if t >= d else jnp.zeros((G_HEADS,), F32) for t in range(t_new)])
            for d in range(t_new)])
        coef = jnp.tile(jnp.repeat(coef, HEAD_DIM, axis=2), (1, 8 // t_new, 1))
        bias_rows = jnp.tile(jnp.repeat(gmlp_b_s[l][:, :t_new].T, HEAD_DIM, axis=1), (8 // t_new, 1))

        hp, hs, (w_in_b, w_out_b, wg2, wu2, wd2) = _ffn_half(
            hp, hs, norm_ffn1[l], *ffn1_w, cast=(w_in[l], w_out[l], ffn2_gate[l], ffn2_up[l], ffn2_down[l]))

        hp, kpt, vpt, gvpt = _mix_prompt(hp.reshape(b, s, D_MODEL), row(norm_mix), w_in_b, attn_sinks[l],
                                         row(gmlp_v_norm), wcat, bias_full, row(norm_attn_out), row(norm_gmlp_out),
                                         w_out_b)
        to_t = lambda c: c.transpose(0, 2, 3, 1).reshape(bd, KV_W, w_buf)
        hs, kst, vst, gvs = _mix_sample(hs, to_t(cache_k_win[l]), to_t(cache_v_win[l]), cos_s, sin_s, row(norm_mix),
                                        w_in_b, attn_sinks[l], row(gmlp_v_norm), coef, bias_rows, row(norm_attn_out),
                                        row(norm_gmlp_out), w_out_b, t_new)
        from_t = lambda c: c.reshape(bd, N_KV, HEAD_DIM, w_buf).transpose(0, 3, 1, 2)

        next_ffn1 = () if last else (ffn1_gate[l + 1], ffn1_up[l + 1], ffn1_down[l + 1])
        hp, hs, ffn1_w = _ffn_half(hp.reshape(b * s, D_MODEL), hs, norm_ffn2[l], wg2, wu2, wd2,
                                   gf=norm_final if last else None, cast=next_ffn1)

        outs[0].append(kpt.reshape(b, N_KV, HEAD_DIM, WINDOW).transpose(0, 3, 1, 2))
        outs[1].append(vpt.reshape(b, N_KV, HEAD_DIM, WINDOW).transpose(0, 3, 1, 2))
        outs[2].append(from_t(kst))
        outs[3].append(from_t(vst))
        outs[4].append(gvpt.reshape(b, G_HEADS, D_GMLP // G_HEADS, CHUNK).transpose(0, 3, 1, 2))
        outs[5].append(gvs.reshape(bd, t_new, G_HEADS, D_GMLP // G_HEADS))

    return (hp.reshape(b, s, D_MODEL), hs.reshape(bd, t_new, D_MODEL)) + tuple(jnp.stack(o) for o in outs)
```

```python
import functools

import jax
import jax.numpy as jnp
from jax import lax
from jax.experimental import pallas as pl
from jax.experimental.pallas import tpu as pltpu

F32 = jnp.float32
BF16 = jnp.bfloat16

D_MODEL = 1024
D_FF = 2816
HEAD_DIM = 64
N_HEADS = 8
N_KV = 2
GQA = N_HEADS // N_KV
WINDOW = 128
CHUNK = 128
G_HEADS = 8
Q_W = N_HEADS * HEAD_DIM
KV_W = N_KV * HEAD_DIM
D_GMLP = 512
D_IN = Q_W + 2 * KV_W + 2 * D_GMLP
K_OFF = Q_W
V_OFF = K_OFF + KV_W
U_OFF = V_OFF + KV_W
GV_OFF = U_OFF + D_GMLP
ROPE_THETA = 10000.0
PAST_LEN = 16384
EPS = 1e-6
LOG2E = 1.4426950408889634
Q_SCALE = HEAD_DIM ** -0.5 * LOG2E
LANES = 128
BF16_SUBLANES = 16

TOKEN_TILE = 512
FF_CHUNK = 256
STAGE_BLOCK_BYTES = 64 * D_FF * 4
STAGE_SLOTS = 8
SEQ_GROUP = 8
GROUPS_PER_STEP = 4
VMEM_LIMIT_BYTES = 56 * 1024 * 1024

_NT = (((1,), (1,)), ((), ()))


def _rms(x, g):
    ms = jnp.mean(x * x, axis=-1, keepdims=True)
    return (x * lax.rsqrt(ms + EPS)) * g


def _resident(shape):
    zeros = (0,) * len(shape)
    return pl.BlockSpec(shape, lambda *_: zeros, pipeline_mode=pl.Buffered(1))


def _round_rows_to_bf16(src_hbm, dst_ref, stage_ref, sem):
    n_slots, rows, _ = stage_ref.shape
    n_blocks = src_hbm.shape[0] // rows
    assert n_blocks >= n_slots

    def fetch(i, slot):
        return pltpu.make_async_copy(src_hbm.at[pl.ds(i * rows, rows), :], stage_ref.at[slot], sem.at[slot])

    for i in range(n_slots - 1):
        fetch(i, i).start()

    def body(i, carry):
        slot = i % n_slots

        @pl.when(i + n_slots - 1 < n_blocks)
        def _():
            fetch(i + n_slots - 1, (i + n_slots - 1) % n_slots).start()

        fetch(i, slot).wait()
        dst_ref[pl.ds(pl.multiple_of(i * rows, rows), rows), :] = stage_ref[slot].astype(BF16)
        return carry

    lax.fori_loop(0, n_blocks, body, 0)


def _ffn_kernel(*refs, final_norm, n_cast, f32_weights):
    refs = iter(refs)
    xp_ref, xn_ref, xs_ref, g_ref, wg_in, wu_in, wd_in = (next(refs) for _ in range(7))
    gf_ref = next(refs) if final_norm else None
    cast_in = [next(refs) for _ in range(n_cast)]
    yp_ref, ys_ref = next(refs), next(refs)
    cast_out = [next(refs) for _ in range(n_cast)]
    h_ref, act0_ref, act_ref = next(refs), next(refs), next(refs)
    if f32_weights:
        wg_ref, wu_ref, wd_ref, wide_stage, narrow_stage, wide_sem, narrow_sem = (next(refs) for _ in range(7))
    else:
        wg_ref, wu_ref, wd_ref = wg_in, wu_in, wd_in

    def gate_up(h, c):
        sl = slice(c * FF_CHUNK, (c + 1) * FF_CHUNK)
        gate = jnp.dot(h, wg_ref[:, sl], preferred_element_type=F32)
        up = jnp.dot(h, wu_ref[:, sl], preferred_element_type=F32)
        return (gate * jax.nn.sigmoid(gate) * up).astype(BF16)

    on_sample = pl.program_id(0) == 0

    @pl.when(on_sample)
    def _():
        if f32_weights:
            _round_rows_to_bf16(wg_in, wg_ref, wide_stage, wide_sem)
            _round_rows_to_bf16(wu_in, wu_ref, wide_stage, wide_sem)
            _round_rows_to_bf16(wd_in, wd_ref, narrow_stage, narrow_sem)
        h0 = _rms(xs_ref[...], g_ref[...]).astype(BF16)
        h_ref[...] = h0
        act0_ref[...] = gate_up(h0, 0)

    for c in range(1, D_FF // FF_CHUNK):
        act_ref[:, (c - 1) * FF_CHUNK:c * FF_CHUNK] = gate_up(h_ref[...], c)
    down0 = jnp.dot(act0_ref[...], wd_ref[0:FF_CHUNK, :], preferred_element_type=F32)
    hn = _rms(xn_ref[...], g_ref[...]).astype(BF16)
    h_ref[...] = hn
    act0_ref[...] = gate_up(hn, 0)
    if final_norm:
        x = jnp.where(on_sample, xs_ref[...], xp_ref[...])
        y = x + 0.5 * (down0 + jnp.dot(act_ref[...], wd_ref[FF_CHUNK:, :], preferred_element_type=F32))
        yp_ref[...] = _rms(y, gf_ref[...])
    else:
        for cols in (slice(0, D_MODEL // 2), slice(D_MODEL // 2, D_MODEL)):
            x = jnp.where(on_sample, xs_ref[:, cols], xp_ref[:, cols])
            yp_ref[:, cols] = x + 0.5 * (down0[:, cols] + jnp.dot(act_ref[...], wd_ref[FF_CHUNK:, cols],
                                                                   preferred_element_type=F32))

    @pl.when(on_sample)
    def _():
        ys_ref[...] = yp_ref[...]

    for src, dst in zip(cast_in, cast_out):
        dst[...] = src[...].astype(BF16)


def _cast_row_blocks(rows, n_steps):
    return max(d for d in range(1, n_steps + 1) if rows % d == 0 and (rows // d) % BF16_SUBLANES == 0)


def _ffn_half(xp, xs, g, wg, wu, wd, gf=None, cast=()):
    tm = TOKEN_TILE
    f32_weights = wg.dtype == F32
    assert all(w.dtype == wg.dtype for w in (wu, wd))
    n = xp.shape[0]
    assert n % tm == 0 and xs.shape[0] == tm
    n_tiles = n // tm
    prompt_spec = pl.BlockSpec((tm, D_MODEL), lambda i: (jnp.maximum(i - 1, 0), 0))
    next_spec = pl.BlockSpec((tm, D_MODEL), lambda i: (jnp.minimum(i, n_tiles - 1), 0))
    sample_in = _resident((tm, D_MODEL))
    sample_out = pl.BlockSpec((tm, D_MODEL), lambda i: (0, 0))
    weight_specs = ([pl.BlockSpec(memory_space=pl.ANY)] * 3 if f32_weights else
                    [_resident((D_MODEL, D_FF)), _resident((D_MODEL, D_FF)), _resident((D_FF, D_MODEL))])
    in_specs = [prompt_spec, next_spec, sample_in, _resident((1, D_MODEL))] + weight_specs
    args = [xp, xp, xs, g.reshape(1, D_MODEL), wg, wu, wd]
    if gf is not None:
        in_specs.append(_resident((1, D_MODEL)))
        args.append(gf.reshape(1, D_MODEL))
    out_specs = [prompt_spec, sample_out]
    out_shape = [jax.ShapeDtypeStruct((n, D_MODEL), F32), jax.ShapeDtypeStruct((tm, D_MODEL), F32)]
    cast_specs = []
    for w in cast:
        rows, cols = w.shape
        nb = _cast_row_blocks(rows, n_tiles)
        cast_specs.append(pl.BlockSpec((rows // nb, cols), lambda i, nb=nb: (jnp.minimum(i, nb - 1), 0)))
        out_shape.append(jax.ShapeDtypeStruct(w.shape, BF16))
    outs = pl.pallas_call(
        functools.partial(_ffn_kernel, final_norm=gf is not None, n_cast=len(cast), f32_weights=f32_weights),
        grid=(n_tiles + 1,),
        in_specs=in_specs + cast_specs,
        out_specs=out_specs + cast_specs,
        out_shape=out_shape,
        scratch_shapes=[pltpu.VMEM((tm, D_MODEL), BF16), pltpu.VMEM((tm, FF_CHUNK), BF16),
                        pltpu.VMEM((tm, D_FF - FF_CHUNK), BF16)] + ([
                            pltpu.VMEM((D_MODEL, D_FF), BF16), pltpu.VMEM((D_MODEL, D_FF), BF16),
                            pltpu.VMEM((D_FF, D_MODEL), BF16),
                            pltpu.VMEM((STAGE_SLOTS, STAGE_BLOCK_BYTES // (4 * D_FF), D_FF), F32),
                            pltpu.VMEM((STAGE_SLOTS, STAGE_BLOCK_BYTES // (4 * D_MODEL), D_MODEL), F32),
                            pltpu.SemaphoreType.DMA((STAGE_SLOTS,)), pltpu.SemaphoreType.DMA((STAGE_SLOTS,))]
                        if f32_weights else []),
        compiler_params=pltpu.CompilerParams(dimension_semantics=("arbitrary",), vmem_limit_bytes=VMEM_LIMIT_BYTES),
        name="ffn_final" if gf is not None else "ffn_half",
    )(*args, *cast)
    return outs[0], outs[1], list(outs[2:])


def _first_half(shape):
    return (lax.broadcasted_iota(jnp.int32, shape, 1) & (HEAD_DIM - 1)) < HEAD_DIM // 2


def _sign_sin(sin):
    return jnp.where(_first_half(sin.shape), -sin, sin)


def _rope(xg, cos, sin_signed):
    swapped = jnp.where(_first_half(xg.shape), pltpu.roll(xg, LANES - HEAD_DIM // 2, 1),
                        pltpu.roll(xg, HEAD_DIM // 2, 1))
    return xg * cos + swapped * sin_signed


def _in_proj(x, gmix, w_in, cos, sin_signed, gvn_gain):
    h = _rms(x, gmix).astype(BF16)
    z = jnp.dot(h, w_in, preferred_element_type=F32)
    q = jnp.concatenate(
        [(_rope(z[:, LANES * i:LANES * (i + 1)], cos, sin_signed) * Q_SCALE).astype(BF16) for i in range(Q_W // LANES)],
        axis=1)
    k = _rope(z[:, K_OFF:V_OFF], cos, sin_signed)
    v = z[:, V_OFF:U_OFF]
    u = _gelu(z[:, U_OFF:GV_OFF])
    gvn = _rms(_gelu(z[:, GV_OFF:]), gvn_gain)
    return q, k, v, u, gvn


def _out_proj(x, ya, yg, ga, gg, w_out):
    cat = jnp.concatenate([_rms(ya, ga).astype(BF16), _rms(yg, gg).astype(BF16)], axis=1)
    return x + jnp.dot(cat, w_out, preferred_element_type=F32)


def _softmax_sink(s, sink):
    m = jnp.maximum(jnp.max(s, axis=1, keepdims=True), sink)
    p = jnp.exp2(s - m)
    return p, jnp.sum(p, axis=1, keepdims=True) + jnp.exp2(sink - m)


def _gelu(x):
    k = -2.0 * (2.0 / jnp.pi) ** 0.5 * LOG2E
    return x / (1.0 + jnp.exp2(x * (k + (0.044715 * k) * (x * x))))


def _mix_prompt_kernel(xa_ref, xc_ref, coff_ref, soff_ref, cbase_ref, sbase_ref, gmix_ref, win_ref, sinks_ref, gvg_ref,
                       wcat_ref, bias_ref, ga_ref, gg_ref, wout_ref, o_ref, ko_ref, vo_ref, gvo_ref, q_s, k_s, v_s, u_s,
                       gv_s, cat_s, *, tm, tiles_per_seq, n_tiles):
    s = pl.program_id(0)

    @pl.when(s == 0)
    def _():
        q_s[1] = jnp.zeros((tm, Q_W), BF16)
        k_s[1] = jnp.zeros((tm + WINDOW, KV_W), BF16)
        v_s[1] = jnp.zeros((tm + WINDOW, KV_W), BF16)
        u_s[1] = jnp.zeros((tm, D_GMLP), F32)
        gv_s[1] = jnp.zeros((tm, D_GMLP), BF16)
        cat_s[0] = jnp.zeros((tm, D_MODEL), BF16)

    for parity in range(2):
        pl.when(s % 2 == parity)(functools.partial(
            _mix_prompt_step, s, parity, xa_ref, xc_ref, coff_ref, soff_ref, cbase_ref, sbase_ref, gmix_ref, win_ref,
            sinks_ref, gvg_ref, wcat_ref, bias_ref, ga_ref, gg_ref, wout_ref, o_ref, ko_ref, vo_ref, gvo_ref, q_s, k_s,
            v_s, u_s, gv_s, cat_s, tm=tm, tiles_per_seq=tiles_per_seq, n_tiles=n_tiles))


def _mix_prompt_step(s, cur, xa_ref, xc_ref, coff_ref, soff_ref, cbase_ref, sbase_ref, gmix_ref, win_ref, sinks_ref,
                     gvg_ref, wcat_ref, bias_ref, ga_ref, gg_ref, wout_ref, o_ref, ko_ref, vo_ref, gvo_ref, q_s, k_s, v_s,
                     u_s, gv_s, cat_s, *, tm, tiles_per_seq, n_tiles):
    oth = 1 - cur

    h = _rms(xa_ref[0], gmix_ref[...]).astype(BF16)
    tile_in_seq = jnp.minimum(s, n_tiles - 1) % tiles_per_seq
    cb = cbase_ref[pl.ds(tile_in_seq, 1), :]
    sb = sbase_ref[pl.ds(tile_in_seq, 1), :]
    cos = cb * coff_ref[...] - sb * soff_ref[...]
    sin = _sign_sin(sb * coff_ref[...] + cb * soff_ref[...])

    def proj_q():
        z = jnp.dot(h, win_ref[:, 0:Q_W], preferred_element_type=F32)
        for i in range(Q_W // LANES):
            q_s[cur, :, LANES * i:LANES * (i + 1)] = (
                _rope(z[:, LANES * i:LANES * (i + 1)], cos, sin) * Q_SCALE).astype(BF16)

    def proj_kv():
        z = jnp.dot(h, win_ref[:, K_OFF:U_OFF], preferred_element_type=F32)
        k = _rope(z[:, 0:KV_W], cos, sin)
        v = z[:, KV_W:]
        k_s[cur, 0:WINDOW, :] = k_s[oth, tm:tm + WINDOW, :]
        v_s[cur, 0:WINDOW, :] = v_s[oth, tm:tm + WINDOW, :]
        k_s[cur, WINDOW:, :] = k.astype(BF16)
        v_s[cur, WINDOW:, :] = v.astype(BF16)
        ko_ref[0] = k[tm - WINDOW:].T
        vo_ref[0] = v[tm - WINDOW:].T

    def proj_u():
        u_s[cur] = _gelu(jnp.dot(h, win_ref[:, U_OFF:GV_OFF], preferred_element_type=F32))

    def proj_gv():
        gvn = _rms(_gelu(jnp.dot(h, win_ref[:, GV_OFF:], preferred_element_type=F32)), gvg_ref[...])
        gv_s[cur] = gvn.astype(BF16)
        gvo_ref[0] = gvn[tm - CHUNK:].T

    def out_half(c):
        cols = slice(c * (D_MODEL // 2), (c + 1) * (D_MODEL // 2))

        def run():
            o_ref[0, :, cols] = xc_ref[0, :, cols] + jnp.dot(cat_s[cur], wout_ref[:, cols], preferred_element_type=F32)
        return run

    mid_fill = [[out_half(0)], [proj_q], [proj_kv, proj_u], [proj_gv]]
    end_fill = [[], [], [], [out_half(1)]]
    assert len(mid_fill) == len(end_fill) == tm // WINDOW

    wrow = lax.broadcasted_iota(jnp.int32, (CHUNK, CHUNK), 0)
    wcol = lax.broadcasted_iota(jnp.int32, (CHUNK, CHUNK), 1)
    wtril = [jnp.where(wcol <= wrow, wcat_ref[hd], 0.0).astype(BF16) for hd in range(G_HEADS)]
    wmix = [jnp.concatenate(wtril[2 * p:2 * p + 2], axis=1) for p in range(G_HEADS // 2)]
    lane = lax.broadcasted_iota(jnp.int32, (CHUNK, LANES), 1)
    low_head = lane < HEAD_DIM

    qi = lax.broadcasted_iota(jnp.int32, (WINDOW, 2 * WINDOW), 0)
    sj = lax.broadcasted_iota(jnp.int32, (WINDOW, 2 * WINDOW), 1)
    dist = WINDOW + qi - sj
    band = (dist >= 0) & (dist < WINDOW)
    first_lo = jnp.where((s + tiles_per_seq - 1) % tiles_per_seq == 0, WINDOW, 0)

    for j in range(tm // WINDOW):
        rows = slice(j * WINDOW, (j + 1) * WINDOW)
        mask = band & (sj >= first_lo) if j == 0 else band
        qb = q_s[oth, rows, :]
        scores = []
        for kv in range(N_KV):
            qs = jnp.concatenate(
                [qb[:, HEAD_DIM * hd:HEAD_DIM * (hd + 1)] for hd in range(kv * GQA, (kv + 1) * GQA)], axis=0)
            kb = k_s[oth, j * WINDOW:(j + 2) * WINDOW, HEAD_DIM * kv:HEAD_DIM * (kv + 1)]
            scores.append(lax.dot_general(qs, kb, _NT, preferred_element_type=F32))
        for run in mid_fill[j]:
            run()
        outs, dens = [], []
        for kv in range(N_KV):
            vb = v_s[oth, j * WINDOW:(j + 2) * WINDOW, HEAD_DIM * kv:HEAD_DIM * (kv + 1)]
            ps = []
            for g in range(GQA):
                sg = jnp.where(mask, scores[kv][g * WINDOW:(g + 1) * WINDOW], -jnp.inf)
                p, den = _softmax_sink(sg, sinks_ref[kv * GQA + g] * LOG2E)
                ps.append(p.astype(BF16))
                dens.append(den)
            outs.append(jnp.dot(jnp.concatenate(ps, axis=0), vb, preferred_element_type=F32))
        mixed = []
        for p in range(G_HEADS // 2):
            r = gv_s[oth, rows, LANES * p:LANES * (p + 1)]
            zero = jnp.zeros_like(r)
            rhs = jnp.concatenate([jnp.where(low_head, r, zero), jnp.where(low_head, zero, r)], axis=0)
            mixed.append(jnp.dot(wmix[p], rhs, preferred_element_type=F32))
        for run in end_fill[j]:
            run()
        ya = [outs[hd // GQA][(hd % GQA) * WINDOW:(hd % GQA + 1) * WINDOW] / dens[hd] for hd in range(N_HEADS)]
        cat_s[oth, rows, 0:Q_W] = _rms(jnp.concatenate(ya, axis=1), ga_ref[...]).astype(BF16)
        yg = u_s[oth, rows, :] * (jnp.concatenate(mixed, axis=1) + bias_ref[...])
        cat_s[oth, rows, Q_W:] = _rms(yg, gg_ref[...]).astype(BF16)


def _mix_prompt(x, gmix, w_in, sinks, gvg, wcat, bias_full, ga, gg, w_out):
    b, s, _ = x.shape
    tm = TOKEN_TILE
    assert s % tm == 0 and tm % WINDOW == 0
    tiles_per_seq = s // tm
    n_tiles = b * tiles_per_seq
    cos_off, sin_off = _rope_tables(jnp.arange(tm, dtype=jnp.int32))
    cos_base, sin_base = _rope_tables(tm * jnp.arange(tiles_per_seq, dtype=jnp.int32))
    proj_tile = lambda i: jnp.minimum(i, n_tiles - 1)
    out_tile = lambda i: jnp.maximum(i - 2, 0)
    x_tiles = x.reshape(n_tiles, tm, D_MODEL)
    last = lambda width: pl.BlockSpec((1, width, WINDOW), lambda i: (proj_tile(i) // tiles_per_seq, 0, 0))
    out, ko, vo, gvo = pl.pallas_call(
        functools.partial(_mix_prompt_kernel, tm=tm, tiles_per_seq=tiles_per_seq, n_tiles=n_tiles),
        grid=(n_tiles + 2,),
        in_specs=[pl.BlockSpec((1, tm, D_MODEL), lambda i: (proj_tile(i), 0, 0)),
                  pl.BlockSpec((1, tm, D_MODEL), lambda i: (out_tile(i), 0, 0)),
                  _resident((tm, LANES)), _resident((tm, LANES)), _resident((tiles_per_seq, LANES)),
                  _resident((tiles_per_seq, LANES)), _resident((1, D_MODEL)), _resident((D_MODEL, D_IN)),
                  pl.BlockSpec(memory_space=pltpu.SMEM), _resident((1, D_GMLP)),
                  _resident((G_HEADS, CHUNK, CHUNK)), _resident((CHUNK, D_GMLP)), _resident((1, Q_W)),
                  _resident((1, D_GMLP)), _resident((D_MODEL, D_MODEL))],
        out_specs=[pl.BlockSpec((1, tm, D_MODEL), lambda i: (out_tile(i), 0, 0)), last(KV_W), last(KV_W),
                   last(D_GMLP)],
        out_shape=[jax.ShapeDtypeStruct((n_tiles, tm, D_MODEL), F32), jax.ShapeDtypeStruct((b, KV_W, WINDOW), F32),
                   jax.ShapeDtypeStruct((b, KV_W, WINDOW), F32), jax.ShapeDtypeStruct((b, D_GMLP, CHUNK), F32)],
        scratch_shapes=[pltpu.VMEM((2, tm, Q_W), BF16), pltpu.VMEM((2, tm + WINDOW, KV_W), BF16),
                        pltpu.VMEM((2, tm + WINDOW, KV_W), BF16), pltpu.VMEM((2, tm, D_GMLP), F32),
                        pltpu.VMEM((2, tm, D_GMLP), BF16), pltpu.VMEM((2, tm, D_MODEL), BF16)],
        compiler_params=pltpu.CompilerParams(dimension_semantics=("arbitrary",), vmem_limit_bytes=VMEM_LIMIT_BYTES),
        name="mix_prompt",
    )(x_tiles, x_tiles, cos_off, sin_off, cos_base, sin_base, gmix, w_in, sinks, gvg, wcat, bias_full, ga, gg, w_out)
    return out.reshape(b, s, D_MODEL), ko, vo, gvo


def _mix_sample_kernel(x_ref, ckt_ref, cvt_ref, cos_ref, sin_ref, gmix_ref, win_ref, sinks_ref, gvg_ref, coef_ref,
                       biasr_ref, ga_ref, gg_ref, wout_ref, o_ref, kot_ref, vot_ref, gvo_ref, q_s, k_s, v_s, ya_s,
                       yg_s, *, t_new, w_buf):
    step = pl.program_id(0)
    n_tok = x_ref.shape[0]
    step_seqs = SEQ_GROUP * GROUPS_PER_STEP
    step_rows = step_seqs * t_new
    grp_rows = SEQ_GROUP * t_new
    grp_keys = SEQ_GROUP * w_buf
    sub = 8

    @pl.when(step == 0)
    def _():
        tile = lambda tab: jnp.broadcast_to(tab[None], (n_tok // sub, sub, LANES)).reshape(n_tok, LANES)
        q, k, v, u, gvn = _in_proj(x_ref[...], gmix_ref[...], win_ref[...], tile(cos_ref[...]),
                                   tile(_sign_sin(sin_ref[...])), gvg_ref[...])
        q_s[...] = q
        k_s[...] = k
        v_s[...] = v
        gvo_ref[...] = gvn
        g3 = gvn.reshape(n_tok // sub, sub, D_GMLP)
        trow = lax.broadcasted_iota(jnp.int32, (1, sub, D_GMLP), 1) & (t_new - 1)
        mixed = biasr_ref[...][None] + coef_ref[0][None] * g3
        for d in range(1, t_new):
            shifted = jnp.where(trow >= d, pltpu.roll(g3, d, 1), 0.0)
            mixed = mixed + coef_ref[d][None] * shifted
        yg_s[...] = (u.reshape(n_tok // sub, sub, D_GMLP) * mixed).reshape(n_tok, D_GMLP)

    row0 = pl.multiple_of(step * step_rows, step_rows)
    q_step = q_s[pl.ds(row0, step_rows), :]
    kn = k_s[pl.ds(row0, step_rows), :]
    vn = v_s[pl.ds(row0, step_rows), :]

    knt = kn.T
    vnt = vn.T
    tail = lax.broadcasted_iota(jnp.int32, (KV_W, w_buf), 1) >= w_buf - t_new
    for b in range(step_seqs):
        shift = (w_buf - t_new - t_new * b) % w_buf
        kot_ref[b] = jnp.where(tail, pltpu.roll(knt, shift, 1) if shift else knt,
                               pltpu.roll(ckt_ref[b], w_buf - t_new, 1))
        vot_ref[b] = jnp.where(tail, pltpu.roll(vnt, shift, 1) if shift else vnt,
                               pltpu.roll(cvt_ref[b], w_buf - t_new, 1))

    knb = kn.astype(BF16)
    vnb = vn.astype(BF16)

    n_rows = GQA * grp_rows
    shift_t = t_new.bit_length() - 1
    shift_w = w_buf.bit_length() - 1
    r = lax.broadcasted_iota(jnp.int32, (n_rows, grp_keys), 0)
    c = lax.broadcasted_iota(jnp.int32, (n_rows, grp_keys), 1)
    mask_c = ((c >> shift_w) == ((r & (grp_rows - 1)) >> shift_t)) & ((c & (w_buf - 1)) > (r & (t_new - 1)))
    r2 = lax.broadcasted_iota(jnp.int32, (n_rows, grp_rows), 0)
    c2 = lax.broadcasted_iota(jnp.int32, (n_rows, grp_rows), 1)
    mask_n = ((c2 >> shift_t) == ((r2 & (grp_rows - 1)) >> shift_t)) & ((c2 & (t_new - 1)) <= (r2 & (t_new - 1)))
    row_head = lax.broadcasted_iota(jnp.int32, (n_rows, 1), 0) >> (grp_rows.bit_length() - 1)

    for grp in range(GROUPS_PER_STEP):
        rows = slice(grp * grp_rows, (grp + 1) * grp_rows)
        seqs = range(grp * SEQ_GROUP, (grp + 1) * SEQ_GROUP)
        for kv in range(N_KV):
            heads = [kv * GQA + i for i in range(GQA)]
            lanes = slice(HEAD_DIM * kv, HEAD_DIM * (kv + 1))
            kt = jnp.concatenate([ckt_ref[b, lanes, :] for b in seqs], axis=1).astype(BF16)
            vt = jnp.concatenate([cvt_ref[b, lanes, :] for b in seqs], axis=1).astype(BF16)
            qs = jnp.concatenate([q_step[rows, HEAD_DIM * hd:HEAD_DIM * (hd + 1)] for hd in heads], axis=0)
            s_c = jnp.where(mask_c, jnp.dot(qs, kt, preferred_element_type=F32), -jnp.inf)
            s_n = jnp.where(mask_n, lax.dot_general(qs, knb[rows, lanes], _NT, preferred_element_type=F32), -jnp.inf)
            sink = jnp.full((n_rows, 1), sinks_ref[heads[0]] * LOG2E, F32)
            for i in range(1, GQA):
                sink = jnp.where(row_head == i, sinks_ref[heads[i]] * LOG2E, sink)
            m = jnp.maximum(jnp.maximum(jnp.max(s_c, axis=1, keepdims=True), jnp.max(s_n, axis=1, keepdims=True)),
                            sink)
            p_c = jnp.exp2(s_c - m)
            p_n = jnp.exp2(s_n - m)
            den = jnp.sum(p_c, axis=1, keepdims=True) + jnp.sum(p_n, axis=1, keepdims=True) + jnp.exp2(sink - m)
            o = (lax.dot_general(p_c.astype(BF16), vt, _NT, preferred_element_type=F32)
                 + jnp.dot(p_n.astype(BF16), vnb[rows, lanes], preferred_element_type=F32)) / den
            for i, hd in enumerate(heads):
                ya_s[pl.ds(row0 + grp * grp_rows, grp_rows), HEAD_DIM * hd:HEAD_DIM * (hd + 1)] = (
                    o[i * grp_rows:(i + 1) * grp_rows])

    @pl.when(step == pl.num_programs(0) - 1)
    def _():
        o_ref[...] = _out_proj(x_ref[...], ya_s[...], yg_s[...], ga_ref[...], gg_ref[...], wout_ref[...])


def _mix_sample(x, cache_kt, cache_vt, cos, sin, gmix, w_in, sinks, gvg, coef, bias_rows, ga, gg, w_out, t_new):
    n_tok = x.shape[0]
    n_seq, _, w_buf = cache_kt.shape
    step_seqs = SEQ_GROUP * GROUPS_PER_STEP
    assert n_seq % step_seqs == 0 and n_tok == n_seq * t_new
    assert t_new & (t_new - 1) == 0 and w_buf & (w_buf - 1) == 0 and 8 % t_new == 0
    assert step_seqs * t_new == w_buf == LANES
    cache_spec = pl.BlockSpec((step_seqs, KV_W, w_buf), lambda i: (i, 0, 0))
    return pl.pallas_call(
        functools.partial(_mix_sample_kernel, t_new=t_new, w_buf=w_buf),
        grid=(n_seq // step_seqs,),
        in_specs=[_resident((n_tok, D_MODEL)), cache_spec, cache_spec, _resident((8, LANES)),
                  _resident((8, LANES)), _resident((1, D_MODEL)), _resident((D_MODEL, D_IN)),
                  pl.BlockSpec(memory_space=pltpu.SMEM), _resident((1, D_GMLP)), _resident((t_new, 8, D_GMLP)),
                  _resident((8, D_GMLP)), _resident((1, Q_W)), _resident((1, D_GMLP)),
                  _resident((D_MODEL, D_MODEL))],
        out_specs=[pl.BlockSpec((n_tok, D_MODEL), lambda i: (0, 0)), cache_spec, cache_spec,
                   pl.BlockSpec((n_tok, D_GMLP), lambda i: (0, 0))],
        out_shape=[jax.ShapeDtypeStruct((n_tok, D_MODEL), F32), jax.ShapeDtypeStruct(cache_kt.shape, F32),
                   jax.ShapeDtypeStruct(cache_vt.shape, F32), jax.ShapeDtypeStruct((n_tok, D_GMLP), F32)],
        scratch_shapes=[pltpu.VMEM((n_tok, Q_W), BF16), pltpu.VMEM((n_tok, KV_W), F32), pltpu.VMEM((n_tok, KV_W), F32),
                        pltpu.VMEM((n_tok, Q_W), F32), pltpu.VMEM((n_tok, D_GMLP), F32)],
        compiler_params=pltpu.CompilerParams(dimension_semantics=("arbitrary",), vmem_limit_bytes=VMEM_LIMIT_BYTES),
        name="mix_sample",
    )(x, cache_kt, cache_vt, cos, sin, gmix, w_in, sinks, gvg, coef, bias_rows, ga, gg, w_out)


def _rope_tables(pos):
    inv_freq = ROPE_THETA ** (-jnp.arange(0, HEAD_DIM, 2, dtype=F32) / HEAD_DIM)
    ang = pos.astype(F32)[:, None] * inv_freq[None, :]
    return jnp.tile(jnp.cos(ang), (1, 4)), jnp.tile(jnp.sin(ang), (1, 4))


def kernel(x_prompt, x_sample, cache_k_win, cache_v_win, norm_ffn1, ffn1_gate, ffn1_up, ffn1_down, norm_mix, w_in,
           attn_sinks, gmlp_v_norm, gmlp_w_s, gmlp_b_s, norm_attn_out, norm_gmlp_out, w_out, norm_ffn2, ffn2_gate,
           ffn2_up, ffn2_down, norm_final):
    depth = norm_ffn1.shape[0]
    b, s, _ = x_prompt.shape
    bd, t_new, _ = x_sample.shape
    w_buf = cache_k_win.shape[2]

    cos_s, sin_s = _rope_tables(PAST_LEN + jnp.arange(t_new, dtype=jnp.int32))
    cos_s, sin_s = jnp.tile(cos_s, (8 // t_new, 1)), jnp.tile(sin_s, (8 // t_new, 1))

    hp = x_prompt.reshape(b * s, D_MODEL)
    hs = x_sample.reshape(bd * t_new, D_MODEL)
    outs = [[] for _ in range(6)]
    ffn1_w = [ffn1_gate[0], ffn1_up[0], ffn1_down[0]]
    for l in range(depth):
        last = l == depth - 1
        row = lambda a: a[l].reshape(1, -1)

        wcat = gmlp_w_s[l]
        bias_full = jnp.repeat(gmlp_b_s[l].T, HEAD_DIM, axis=1)
        ws_small = gmlp_w_s[l][:, :t_new, :t_new]
        coef = jnp.stack([
            jnp.stack([ws_small[:, t, t - d] if t >= d else jnp.zeros((G_HEADS,), F32) for t in range(t_new)])
            for d in range(t_new)])
        coef = jnp.tile(jnp.repeat(coef, HEAD_DIM, axis=2), (1, 8 // t_new, 1))
        bias_rows = jnp.tile(jnp.repeat(gmlp_b_s[l][:, :t_new].T, HEAD_DIM, axis=1), (8 // t_new, 1))

        hp, hs, (w_in_b, w_out_b, wg2, wu2, wd2) = _ffn_half(
            hp, hs, norm_ffn1[l], *ffn1_w, cast=(w_in[l], w_out[l], ffn2_gate[l], ffn2_up[l], ffn2_down[l]))

        hp, kpt, vpt, gvpt = _mix_prompt(hp.reshape(b, s, D_MODEL), row(norm_mix), w_in_b, attn_sinks[l],
                                         row(gmlp_v_norm), wcat, bias_full, row(norm_attn_out), row(norm_gmlp_out),
                                         w_out_b)
        to_t = lambda c: c.transpose(0, 2, 3, 1).reshape(bd, KV_W, w_buf)
        hs, kst, vst, gvs = _mix_sample(hs, to_t(cache_k_win[l]), to_t(cache_v_win[l]), cos_s, sin_s, row(norm_mix),
                                        w_in_b, attn_sinks[l], row(gmlp_v_norm), coef, bias_rows, row(norm_attn_out),
                                        row(norm_gmlp_out), w_out_b, t_new)
        from_t = lambda c: c.reshape(bd, N_KV, HEAD_DIM, w_buf).transpose(0, 3, 1, 2)

        next_ffn1 = () if last else (ffn1_gate[l + 1], ffn1_up[l + 1], ffn1_down[l + 1])
        hp, hs, ffn1_w = _ffn_half(hp.reshape(b * s, D_MODEL), hs, norm_ffn2[l], wg2, wu2, wd2,
                                   gf=norm_final if last else None, cast=next_ffn1)

        outs[0].append(kpt.reshape(b, N_KV, HEAD_DIM, WINDOW).transpose(0, 3, 1, 2))
        outs[1].append(vpt.reshape(b, N_KV, HEAD_DIM, WINDOW).transpose(0, 3, 1, 2))
        outs[2].append(from_t(kst))
        outs[3].append(from_t(vst))
        outs[4].append(gvpt.reshape(b, G_HEADS, D_GMLP // G_HEADS, CHUNK).transpose(0, 3, 1, 2))
        outs[5].append(gvs.reshape(bd, t_new, G_HEADS, D_GMLP // G_HEADS))

    return (hp.reshape(b, s, D_MODEL), hs.reshape(bd, t_new, D_MODEL)) + tuple(jnp.stack(o) for o in outs)
```

```python
import functools

import jax
import jax.numpy as jnp
from jax import lax
from jax.experimental import pallas as pl
from jax.experimental.pallas import tpu as pltpu

F32 = jnp.float32
BF16 = jnp.bfloat16

D_MODEL = 1024
D_FF = 2816
HEAD_DIM = 64
N_HEADS = 8
N_KV = 2
GQA = N_HEADS // N_KV
WINDOW = 128
CHUNK = 128
G_HEADS = 8
Q_W = N_HEADS * HEAD_DIM
KV_W = N_KV * HEAD_DIM
D_GMLP = 512
D_IN = Q_W + 2 * KV_W + 2 * D_GMLP
K_OFF = Q_W
V_OFF = K_OFF + KV_W
U_OFF = V_OFF + KV_W
GV_OFF = U_OFF + D_GMLP
ROPE_THETA = 10000.0
PAST_LEN = 16384
EPS = 1e-6
LOG2E = 1.4426950408889634
Q_SCALE = HEAD_DIM ** -0.5 * LOG2E
LANES = 128
BF16_SUBLANES = 16

TOKEN_TILE = 512
FF_CHUNK = 256
STAGE_BLOCK_BYTES = 64 * D_FF * 4
STAGE_SLOTS = 8
SEQ_GROUP = 8
GROUPS_PER_STEP = 4
VMEM_LIMIT_BYTES = 56 * 1024 * 1024

_NT = (((1,), (1,)), ((), ()))


def _rms(x, g):
    ms = jnp.mean(x * x, axis=-1, keepdims=True)
    return (x * lax.rsqrt(ms + EPS)) * g


def _resident(shape):
    zeros = (0,) * len(shape)
    return pl.BlockSpec(shape, lambda *_: zeros, pipeline_mode=pl.Buffered(1))


def _round_rows_to_bf16(src_hbm, dst_ref, stage_ref, sem):
    n_slots, rows, _ = stage_ref.shape
    assert src_hbm.shape[0] % rows == 0
    n_blocks = src_hbm.shape[0] // rows
    assert n_blocks >= n_slots

    def fetch(i, slot):
        return pltpu.make_async_copy(src_hbm.at[pl.ds(i * rows, rows), :], stage_ref.at[slot], sem.at[slot])

    for i in range(n_slots - 1):
        fetch(i, i).start()

    def body(i, carry):
        slot = i % n_slots

        @pl.when(i + n_slots - 1 < n_blocks)
        def _():
            fetch(i + n_slots - 1, (i + n_slots - 1) % n_slots).start()

        fetch(i, slot).wait()
        dst_ref[pl.ds(pl.multiple_of(i * rows, rows), rows), :] = stage_ref[slot].astype(BF16)
        return carry

    lax.fori_loop(0, n_blocks, body, 0)


def _ffn_kernel(*refs, final_norm, n_cast, f32_weights):
    refs = iter(refs)
    xp_ref, xn_ref, xs_ref, g_ref, wg_in, wu_in, wd_in = (next(refs) for _ in range(7))
    gf_ref = next(refs) if final_norm else None
    cast_in = [next(refs) for _ in range(n_cast)]
    yp_ref, ys_ref = next(refs), next(refs)
    cast_out = [next(refs) for _ in range(n_cast)]
    h_ref, act0_ref, act_ref = next(refs), next(refs), next(refs)
    if f32_weights:
        wg_ref, wu_ref, wd_ref, wide_stage, narrow_stage, wide_sem, narrow_sem = (next(refs) for _ in range(7))
    else:
        wg_ref, wu_ref, wd_ref = wg_in, wu_in, wd_in

    def gate_up(h, c):
        sl = slice(c * FF_CHUNK, (c + 1) * FF_CHUNK)
        gate = jnp.dot(h, wg_ref[:, sl], preferred_element_type=F32)
        up = jnp.dot(h, wu_ref[:, sl], preferred_element_type=F32)
        return (gate * jax.nn.sigmoid(gate) * up).astype(BF16)

    on_sample = pl.program_id(0) == 0

    @pl.when(on_sample)
    def _():
        if f32_weights:
            _round_rows_to_bf16(wg_in, wg_ref, wide_stage, wide_sem)
            _round_rows_to_bf16(wu_in, wu_ref, wide_stage, wide_sem)
            _round_rows_to_bf16(wd_in, wd_ref, narrow_stage, narrow_sem)
        h0 = _rms(xs_ref[...], g_ref[...]).astype(BF16)
        h_ref[...] = h0
        act0_ref[...] = gate_up(h0, 0)

    for c in range(1, D_FF // FF_CHUNK):
        act_ref[:, (c - 1) * FF_CHUNK:c * FF_CHUNK] = gate_up(h_ref[...], c)
    down0 = jnp.dot(act0_ref[...], wd_ref[0:FF_CHUNK, :], preferred_element_type=F32)
    hn = _rms(xn_ref[...], g_ref[...]).astype(BF16)
    h_ref[...] = hn
    act0_ref[...] = gate_up(hn, 0)
    if final_norm:
        x = jnp.where(on_sample, xs_ref[...], xp_ref[...])
        y = x + 0.5 * (down0 + jnp.dot(act_ref[...], wd_ref[FF_CHUNK:, :], preferred_element_type=F32))
        yp_ref[...] = _rms(y, gf_ref[...])
    else:
        for cols in (slice(0, D_MODEL // 2), slice(D_MODEL // 2, D_MODEL)):
            x = jnp.where(on_sample, xs_ref[:, cols], xp_ref[:, cols])
            yp_ref[:, cols] = x + 0.5 * (down0[:, cols] + jnp.dot(act_ref[...], wd_ref[FF_CHUNK:, cols],
                                                                   preferred_element_type=F32))

    @pl.when(on_sample)
    def _():
        ys_ref[...] = yp_ref[...]

    for src, dst in zip(cast_in, cast_out):
        dst[...] = src[...].astype(BF16)


def _cast_row_blocks(rows, n_steps):
    return max(d for d in range(1, n_steps + 1) if rows % d == 0 and (rows // d) % BF16_SUBLANES == 0)


def _ffn_half(xp, xs, g, wg, wu, wd, gf=None, cast=()):
    tm = TOKEN_TILE
    f32_weights = wg.dtype == F32
    assert all(w.dtype == wg.dtype for w in (wu, wd))
    n = xp.shape[0]
    assert n % tm == 0 and xs.shape[0] == tm
    n_tiles = n // tm
    prompt_spec = pl.BlockSpec((tm, D_MODEL), lambda i: (jnp.maximum(i - 1, 0), 0))
    next_spec = pl.BlockSpec((tm, D_MODEL), lambda i: (jnp.minimum(i, n_tiles - 1), 0))
    sample_in = _resident((tm, D_MODEL))
    sample_out = pl.BlockSpec((tm, D_MODEL), lambda i: (0, 0))
    weight_specs = ([pl.BlockSpec(memory_space=pl.ANY)] * 3 if f32_weights else
                    [_resident((D_MODEL, D_FF)), _resident((D_MODEL, D_FF)), _resident((D_FF, D_MODEL))])
    in_specs = [prompt_spec, next_spec, sample_in, _resident((1, D_MODEL))] + weight_specs
    args = [xp, xp, xs, g.reshape(1, D_MODEL), wg, wu, wd]
    if gf is not None:
        in_specs.append(_resident((1, D_MODEL)))
        args.append(gf.reshape(1, D_MODEL))
    out_specs = [prompt_spec, sample_out]
    out_shape = [jax.ShapeDtypeStruct((n, D_MODEL), F32), jax.ShapeDtypeStruct((tm, D_MODEL), F32)]
    cast_specs = []
    for w in cast:
        rows, cols = w.shape
        nb = _cast_row_blocks(rows, n_tiles)
        cast_specs.append(pl.BlockSpec((rows // nb, cols), lambda i, nb=nb: (jnp.minimum(i, nb - 1), 0)))
        out_shape.append(jax.ShapeDtypeStruct(w.shape, BF16))
    outs = pl.pallas_call(
        functools.partial(_ffn_kernel, final_norm=gf is not None, n_cast=len(cast), f32_weights=f32_weights),
        grid=(n_tiles + 1,),
        in_specs=in_specs + cast_specs,
        out_specs=out_specs + cast_specs,
        out_shape=out_shape,
        scratch_shapes=[pltpu.VMEM((tm, D_MODEL), BF16), pltpu.VMEM((tm, FF_CHUNK), BF16),
                        pltpu.VMEM((tm, D_FF - FF_CHUNK), BF16)] + ([
                            pltpu.VMEM((D_MODEL, D_FF), BF16), pltpu.VMEM((D_MODEL, D_FF), BF16),
                            pltpu.VMEM((D_FF, D_MODEL), BF16),
                            pltpu.VMEM((STAGE_SLOTS, STAGE_BLOCK_BYTES // (4 * D_FF), D_FF), F32),
                            pltpu.VMEM((STAGE_SLOTS, STAGE_BLOCK_BYTES // (4 * D_MODEL), D_MODEL), F32),
                            pltpu.SemaphoreType.DMA((STAGE_SLOTS,)), pltpu.SemaphoreType.DMA((STAGE_SLOTS,))]
                        if f32_weights else []),
        compiler_params=pltpu.CompilerParams(dimension_semantics=("arbitrary",), vmem_limit_bytes=VMEM_LIMIT_BYTES),
        name="ffn_final" if gf is not None else "ffn_half",
    )(*args, *cast)
    return outs[0], outs[1], list(outs[2:])


def _first_half(shape):
    return (lax.broadcasted_iota(jnp.int32, shape, 1) & (HEAD_DIM - 1)) < HEAD_DIM // 2


def _sign_sin(sin):
    return jnp.where(_first_half(sin.shape), -sin, sin)


def _rope(xg, cos, sin_signed):
    swapped = jnp.where(_first_half(xg.shape), pltpu.roll(xg, LANES - HEAD_DIM // 2, 1),
                        pltpu.roll(xg, HEAD_DIM // 2, 1))
    return xg * cos + swapped * sin_signed


def _in_proj(x, gmix, w_in, cos, sin_signed, gvn_gain):
    h = _rms(x, gmix).astype(BF16)
    z = jnp.dot(h, w_in, preferred_element_type=F32)
    q = jnp.concatenate(
        [(_rope(z[:, LANES * i:LANES * (i + 1)], cos, sin_signed) * Q_SCALE).astype(BF16) for i in range(Q_W // LANES)],
        axis=1)
    k = _rope(z[:, K_OFF:V_OFF], cos, sin_signed)
    v = z[:, V_OFF:U_OFF]
    u = _gelu(z[:, U_OFF:GV_OFF])
    gvn = _rms(_gelu(z[:, GV_OFF:]), gvn_gain)
    return q, k, v, u, gvn


def _out_proj(x, ya, yg, ga, gg, w_out):
    cat = jnp.concatenate([_rms(ya, ga).astype(BF16), _rms(yg, gg).astype(BF16)], axis=1)
    return x + jnp.dot(cat, w_out, preferred_element_type=F32)


def _softmax_sink(s, sink):
    m = jnp.maximum(jnp.max(s, axis=1, keepdims=True), sink)
    p = jnp.exp2(s - m)
    return p, jnp.sum(p, axis=1, keepdims=True) + jnp.exp2(sink - m)


def _gelu(x):
    k = -2.0 * (2.0 / jnp.pi) ** 0.5 * LOG2E
    return x / (1.0 + jnp.exp2(x * (k + (0.044715 * k) * (x * x))))


def _mix_prompt_kernel(xa_ref, xc_ref, coff_ref, soff_ref, cbase_ref, sbase_ref, gmix_ref, win_ref, sinks_ref, gvg_ref,
                       wcat_ref, bias_ref, ga_ref, gg_ref, wout_ref, o_ref, ko_ref, vo_ref, gvo_ref, q_s, k_s, v_s, u_s,
                       gv_s, cat_s, *, tm, tiles_per_seq, n_tiles):
    s = pl.program_id(0)

    @pl.when(s == 0)
    def _():
        q_s[1] = jnp.zeros((tm, Q_W), BF16)
        k_s[1] = jnp.zeros((tm + WINDOW, KV_W), BF16)
        v_s[1] = jnp.zeros((tm + WINDOW, KV_W), BF16)
        u_s[1] = jnp.zeros((tm, D_GMLP), F32)
        gv_s[1] = jnp.zeros((tm, D_GMLP), BF16)
        cat_s[0] = jnp.zeros((tm, D_MODEL), BF16)

    for parity in range(2):
        pl.when(s % 2 == parity)(functools.partial(
            _mix_prompt_step, s, parity, xa_ref, xc_ref, coff_ref, soff_ref, cbase_ref, sbase_ref, gmix_ref, win_ref,
            sinks_ref, gvg_ref, wcat_ref, bias_ref, ga_ref, gg_ref, wout_ref, o_ref, ko_ref, vo_ref, gvo_ref, q_s, k_s,
            v_s, u_s, gv_s, cat_s, tm=tm, tiles_per_seq=tiles_per_seq, n_tiles=n_tiles))


def _mix_prompt_step(s, cur, xa_ref, xc_ref, coff_ref, soff_ref, cbase_ref, sbase_ref, gmix_ref, win_ref, sinks_ref,
                     gvg_ref, wcat_ref, bias_ref, ga_ref, gg_ref, wout_ref, o_ref, ko_ref, vo_ref, gvo_ref, q_s, k_s, v_s,
                     u_s, gv_s, cat_s, *, tm, tiles_per_seq, n_tiles):
    oth = 1 - cur

    h = _rms(xa_ref[0], gmix_ref[...]).astype(BF16)
    tile_in_seq = jnp.minimum(s, n_tiles - 1) % tiles_per_seq
    cb = cbase_ref[pl.ds(tile_in_seq, 1), :]
    sb = sbase_ref[pl.ds(tile_in_seq, 1), :]
    cos = cb * coff_ref[...] - sb * soff_ref[...]
    sin = _sign_sin(sb * coff_ref[...] + cb * soff_ref[...])

    def proj_q():
        z = jnp.dot(h, win_ref[:, 0:Q_W], preferred_element_type=F32)
        for i in range(Q_W // LANES):
            q_s[cur, :, LANES * i:LANES * (i + 1)] = (
                _rope(z[:, LANES * i:LANES * (i + 1)], cos, sin) * Q_SCALE).astype(BF16)

    def proj_kv():
        z = jnp.dot(h, win_ref[:, K_OFF:U_OFF], preferred_element_type=F32)
        k = _rope(z[:, 0:KV_W], cos, sin)
        v = z[:, KV_W:]
        k_s[cur, 0:WINDOW, :] = k_s[oth, tm:tm + WINDOW, :]
        v_s[cur, 0:WINDOW, :] = v_s[oth, tm:tm + WINDOW, :]
        k_s[cur, WINDOW:, :] = k.astype(BF16)
        v_s[cur, WINDOW:, :] = v.astype(BF16)
        ko_ref[0] = k[tm - WINDOW:].T
        vo_ref[0] = v[tm - WINDOW:].T

    def proj_u():
        u_s[cur] = _gelu(jnp.dot(h, win_ref[:, U_OFF:GV_OFF], preferred_element_type=F32))

    def proj_gv():
        gvn = _rms(_gelu(jnp.dot(h, win_ref[:, GV_OFF:], preferred_element_type=F32)), gvg_ref[...])
        gv_s[cur] = gvn.astype(BF16)
        gvo_ref[0] = gvn[tm - CHUNK:].T

    def out_half(c):
        cols = slice(c * (D_MODEL // 2), (c + 1) * (D_MODEL // 2))

        def run():
            o_ref[0, :, cols] = xc_ref[0, :, cols] + jnp.dot(cat_s[cur], wout_ref[:, cols], preferred_element_type=F32)
        return run

    mid_fill = [[out_half(0)], [proj_q], [proj_kv, proj_u], [proj_gv]]
    end_fill = [[], [], [], [out_half(1)]]
    assert len(mid_fill) == len(end_fill) == tm // WINDOW

    wrow = lax.broadcasted_iota(jnp.int32, (CHUNK, CHUNK), 0)
    wcol = lax.broadcasted_iota(jnp.int32, (CHUNK, CHUNK), 1)
    wtril = [jnp.where(wcol <= wrow, wcat_ref[hd], 0.0).astype(BF16) for hd in range(G_HEADS)]
    wmix = [jnp.concatenate(wtril[2 * p:2 * p + 2], axis=1) for p in range(G_HEADS // 2)]
    lane = lax.broadcasted_iota(jnp.int32, (CHUNK, LANES), 1)
    low_head = lane < HEAD_DIM

    qi = lax.broadcasted_iota(jnp.int32, (WINDOW, 2 * WINDOW), 0)
    sj = lax.broadcasted_iota(jnp.int32, (WINDOW, 2 * WINDOW), 1)
    dist = WINDOW + qi - sj
    band = (dist >= 0) & (dist < WINDOW)
    first_lo = jnp.where((s + tiles_per_seq - 1) % tiles_per_seq == 0, WINDOW, 0)

    for j in range(tm // WINDOW):
        rows = slice(j * WINDOW, (j + 1) * WINDOW)
        mask = band & (sj >= first_lo) if j == 0 else band
        qb = q_s[oth, rows, :]
        scores = []
        for kv in range(N_KV):
            qs = jnp.concatenate(
                [qb[:, HEAD_DIM * hd:HEAD_DIM * (hd + 1)] for hd in range(kv * GQA, (kv + 1) * GQA)], axis=0)
            kb = k_s[oth, j * WINDOW:(j + 2) * WINDOW, HEAD_DIM * kv:HEAD_DIM * (kv + 1)]
            scores.append(lax.dot_general(qs, kb, _NT, preferred_element_type=F32))
        for run in mid_fill[j]:
            run()
        outs, dens = [], []
        for kv in range(N_KV):
            vb = v_s[oth, j * WINDOW:(j + 2) * WINDOW, HEAD_DIM * kv:HEAD_DIM * (kv + 1)]
            ps = []
            for g in range(GQA):
                sg = jnp.where(mask, scores[kv][g * WINDOW:(g + 1) * WINDOW], -jnp.inf)
                p, den = _softmax_sink(sg, sinks_ref[kv * GQA + g] * LOG2E)
                ps.append(p.astype(BF16))
                dens.append(den)
            outs.append(jnp.dot(jnp.concatenate(ps, axis=0), vb, preferred_element_type=F32))
        mixed = []
        for p in range(G_HEADS // 2):
            r = gv_s[oth, rows, LANES * p:LANES * (p + 1)]
            zero = jnp.zeros_like(r)
            rhs = jnp.concatenate([jnp.where(low_head, r, zero), jnp.where(low_head, zero, r)], axis=0)
            mixed.append(jnp.dot(wmix[p], rhs, preferred_element_type=F32))
        for run in end_fill[j]:
            run()
        ya = [outs[hd // GQA][(hd % GQA) * WINDOW:(hd % GQA + 1) * WINDOW] / dens[hd] for hd in range(N_HEADS)]
        cat_s[oth, rows, 0:Q_W] = _rms(jnp.concatenate(ya, axis=1), ga_ref[...]).astype(BF16)
        yg = u_s[oth, rows, :] * (jnp.concatenate(mixed, axis=1) + bias_ref[...])
        cat_s[oth, rows, Q_W:] = _rms(yg, gg_ref[...]).astype(BF16)


def _mix_prompt(x, gmix, w_in, sinks, gvg, wcat, bias_full, ga, gg, w_out):
    b, s, _ = x.shape
    tm = TOKEN_TILE
    assert s % tm == 0 and tm % WINDOW == 0
    tiles_per_seq = s // tm
    n_tiles = b * tiles_per_seq
    cos_off, sin_off = _rope_tables(jnp.arange(tm, dtype=jnp.int32))
    cos_base, sin_base = _rope_tables(tm * jnp.arange(tiles_per_seq, dtype=jnp.int32))
    proj_tile = lambda i: jnp.minimum(i, n_tiles - 1)
    out_tile = lambda i: jnp.maximum(i - 2, 0)
    x_tiles = x.reshape(n_tiles, tm, D_MODEL)
    last = lambda width: pl.BlockSpec((1, width, WINDOW), lambda i: (proj_tile(i) // tiles_per_seq, 0, 0))
    out, ko, vo, gvo = pl.pallas_call(
        functools.partial(_mix_prompt_kernel, tm=tm, tiles_per_seq=tiles_per_seq, n_tiles=n_tiles),
        grid=(n_tiles + 2,),
        in_specs=[pl.BlockSpec((1, tm, D_MODEL), lambda i: (proj_tile(i), 0, 0)),
                  pl.BlockSpec((1, tm, D_MODEL), lambda i: (out_tile(i), 0, 0)),
                  _resident((tm, LANES)), _resident((tm, LANES)), _resident((tiles_per_seq, LANES)),
                  _resident((tiles_per_seq, LANES)), _resident((1, D_MODEL)), _resident((D_MODEL, D_IN)),
                  pl.BlockSpec(memory_space=pltpu.SMEM), _resident((1, D_GMLP)),
                  _resident((G_HEADS, CHUNK, CHUNK)), _resident((CHUNK, D_GMLP)), _resident((1, Q_W)),
                  _resident((1, D_GMLP)), _resident((D_MODEL, D_MODEL))],
        out_specs=[pl.BlockSpec((1, tm, D_MODEL), lambda i: (out_tile(i), 0, 0)), last(KV_W), last(KV_W),
                   last(D_GMLP)],
        out_shape=[jax.ShapeDtypeStruct((n_tiles, tm, D_MODEL), F32), jax.ShapeDtypeStruct((b, KV_W, WINDOW), F32),
                   jax.ShapeDtypeStruct((b, KV_W, WINDOW), F32), jax.ShapeDtypeStruct((b, D_GMLP, CHUNK), F32)],
        scratch_shapes=[pltpu.VMEM((2, tm, Q_W), BF16), pltpu.VMEM((2, tm + WINDOW, KV_W), BF16),
                        pltpu.VMEM((2, tm + WINDOW, KV_W), BF16), pltpu.VMEM((2, tm, D_GMLP), F32),
                        pltpu.VMEM((2, tm, D_GMLP), BF16), pltpu.VMEM((2, tm, D_MODEL), BF16)],
        compiler_params=pltpu.CompilerParams(dimension_semantics=("arbitrary",), vmem_limit_bytes=VMEM_LIMIT_BYTES),
        name="mix_prompt",
    )(x_tiles, x_tiles, cos_off, sin_off, cos_base, sin_base, gmix, w_in, sinks, gvg, wcat, bias_full, ga, gg, w_out)
    return out.reshape(b, s, D_MODEL), ko, vo, gvo


def _mix_sample_kernel(x_ref, ckt_ref, cvt_ref, cos_ref, sin_ref, gmix_ref, win_ref, sinks_ref, gvg_ref, coef_ref,
                       biasr_ref, ga_ref, gg_ref, wout_ref, o_ref, kot_ref, vot_ref, gvo_ref, q_s, k_s, v_s, ya_s,
                       yg_s, *, t_new, w_buf):
    step = pl.program_id(0)
    n_tok = x_ref.shape[0]
    step_seqs = SEQ_GROUP * GROUPS_PER_STEP
    step_rows = step_seqs * t_new
    grp_rows = SEQ_GROUP * t_new
    grp_keys = SEQ_GROUP * w_buf
    sub = 8

    @pl.when(step == 0)
    def _():
        tile = lambda tab: jnp.broadcast_to(tab[None], (n_tok // sub, sub, LANES)).reshape(n_tok, LANES)
        q, k, v, u, gvn = _in_proj(x_ref[...], gmix_ref[...], win_ref[...], tile(cos_ref[...]),
                                   tile(_sign_sin(sin_ref[...])), gvg_ref[...])
        q_s[...] = q
        k_s[...] = k
        v_s[...] = v
        gvo_ref[...] = gvn
        g3 = gvn.reshape(n_tok // sub, sub, D_GMLP)
        trow = lax.broadcasted_iota(jnp.int32, (1, sub, D_GMLP), 1) & (t_new - 1)
        mixed = biasr_ref[...][None] + coef_ref[0][None] * g3
        for d in range(1, t_new):
            shifted = jnp.where(trow >= d, pltpu.roll(g3, d, 1), 0.0)
            mixed = mixed + coef_ref[d][None] * shifted
        yg_s[...] = (u.reshape(n_tok // sub, sub, D_GMLP) * mixed).reshape(n_tok, D_GMLP)

    row0 = pl.multiple_of(step * step_rows, step_rows)
    q_step = q_s[pl.ds(row0, step_rows), :]
    kn = k_s[pl.ds(row0, step_rows), :]
    vn = v_s[pl.ds(row0, step_rows), :]

    knt = kn.T
    vnt = vn.T
    tail = lax.broadcasted_iota(jnp.int32, (KV_W, w_buf), 1) >= w_buf - t_new
    for b in range(step_seqs):
        shift = (w_buf - t_new - t_new * b) % w_buf
        kot_ref[b] = jnp.where(tail, pltpu.roll(knt, shift, 1) if shift else knt,
                               pltpu.roll(ckt_ref[b], w_buf - t_new, 1))
        vot_ref[b] = jnp.where(tail, pltpu.roll(vnt, shift, 1) if shift else vnt,
                               pltpu.roll(cvt_ref[b], w_buf - t_new, 1))

    knb = kn.astype(BF16)
    vnb = vn.astype(BF16)

    n_rows = GQA * grp_rows
    shift_t = t_new.bit_length() - 1
    shift_w = w_buf.bit_length() - 1
    r = lax.broadcasted_iota(jnp.int32, (n_rows, grp_keys), 0)
    c = lax.broadcasted_iota(jnp.int32, (n_rows, grp_keys), 1)
    mask_c = ((c >> shift_w) == ((r & (grp_rows - 1)) >> shift_t)) & ((c & (w_buf - 1)) > (r & (t_new - 1)))
    r2 = lax.broadcasted_iota(jnp.int32, (n_rows, grp_rows), 0)
    c2 = lax.broadcasted_iota(jnp.int32, (n_rows, grp_rows), 1)
    mask_n = ((c2 >> shift_t) == ((r2 & (grp_rows - 1)) >> shift_t)) & ((c2 & (t_new - 1)) <= (r2 & (t_new - 1)))
    row_head = lax.broadcasted_iota(jnp.int32, (n_rows, 1), 0) >> (grp_rows.bit_length() - 1)

    for grp in range(GROUPS_PER_STEP):
        rows = slice(grp * grp_rows, (grp + 1) * grp_rows)
        seqs = range(grp * SEQ_GROUP, (grp + 1) * SEQ_GROUP)
        for kv in range(N_KV):
            heads = [kv * GQA + i for i in range(GQA)]
            lanes = slice(HEAD_DIM * kv, HEAD_DIM * (kv + 1))
            kt = jnp.concatenate([ckt_ref[b, lanes, :] for b in seqs], axis=1).astype(BF16)
            vt = jnp.concatenate([cvt_ref[b, lanes, :] for b in seqs], axis=1).astype(BF16)
            qs = jnp.concatenate([q_step[rows, HEAD_DIM * hd:HEAD_DIM * (hd + 1)] for hd in heads], axis=0)
            s_c = jnp.where(mask_c, jnp.dot(qs, kt, preferred_element_type=F32), -jnp.inf)
            s_n = jnp.where(mask_n, lax.dot_general(qs, knb[rows, lanes], _NT, preferred_element_type=F32), -jnp.inf)
            sink = jnp.full((n_rows, 1), sinks_ref[heads[0]] * LOG2E, F32)
            for i in range(1, GQA):
                sink = jnp.where(row_head == i, sinks_ref[heads[i]] * LOG2E, sink)
            m = jnp.maximum(jnp.maximum(jnp.max(s_c, axis=1, keepdims=True), jnp.max(s_n, axis=1, keepdims=True)),
                            sink)
            p_c = jnp.exp2(s_c - m)
            p_n = jnp.exp2(s_n - m)
            den = jnp.sum(p_c, axis=1, keepdims=True) + jnp.sum(p_n, axis=1, keepdims=True) + jnp.exp2(sink - m)
            o = (lax.dot_general(p_c.astype(BF16), vt, _NT, preferred_element_type=F32)
                 + jnp.dot(p_n.astype(BF16), vnb[rows, lanes], preferred_element_type=F32)) / den
            for i, hd in enumerate(heads):
                ya_s[pl.ds(row0 + grp * grp_rows, grp_rows), HEAD_DIM * hd:HEAD_DIM * (hd + 1)] = (
                    o[i * grp_rows:(i + 1) * grp_rows])

    @pl.when(step == pl.num_programs(0) - 1)
    def _():
        o_ref[...] = _out_proj(x_ref[...], ya_s[...], yg_s[...], ga_ref[...], gg_ref[...], wout_ref[...])


def _mix_sample(x, cache_kt, cache_vt, cos, sin, gmix, w_in, sinks, gvg, coef, bias_rows, ga, gg, w_out, t_new):
    n_tok = x.shape[0]
    n_seq, _, w_buf = cache_kt.shape
    step_seqs = SEQ_GROUP * GROUPS_PER_STEP
    assert n_seq % step_seqs == 0 and n_tok == n_seq * t_new
    assert t_new & (t_new - 1) == 0 and w_buf & (w_buf - 1) == 0 and 8 % t_new == 0
    assert step_seqs * t_new == w_buf == LANES
    cache_spec = pl.BlockSpec((step_seqs, KV_W, w_buf), lambda i: (i, 0, 0))
    return pl.pallas_call(
        functools.partial(_mix_sample_kernel, t_new=t_new, w_buf=w_buf),
        grid=(n_seq // step_seqs,),
        in_specs=[_resident((n_tok, D_MODEL)), cache_spec, cache_spec, _resident((8, LANES)),
                  _resident((8, LANES)), _resident((1, D_MODEL)), _resident((D_MODEL, D_IN)),
                  pl.BlockSpec(memory_space=pltpu.SMEM), _resident((1, D_GMLP)), _resident((t_new, 8, D_GMLP)),
                  _resident((8, D_GMLP)), _resident((1, Q_W)), _resident((1, D_GMLP)),
                  _resident((D_MODEL, D_MODEL))],
        out_specs=[pl.BlockSpec((n_tok, D_MODEL), lambda i: (0, 0)), cache_spec, cache_spec,
                   pl.BlockSpec((n_tok, D_GMLP), lambda i: (0, 0))],
        out_shape=[jax.ShapeDtypeStruct((n_tok, D_MODEL), F32), jax.ShapeDtypeStruct(cache_kt.shape, F32),
                   jax.ShapeDtypeStruct(cache_vt.shape, F32), jax.ShapeDtypeStruct((n_tok, D_GMLP), F32)],
        scratch_shapes=[pltpu.VMEM((n_tok, Q_W), BF16), pltpu.VMEM((n_tok, KV_W), F32), pltpu.VMEM((n_tok, KV_W), F32),
                        pltpu.VMEM((n_tok, Q_W), F32), pltpu.VMEM((n_tok, D_GMLP), F32)],
        compiler_params=pltpu.CompilerParams(dimension_semantics=("arbitrary",), vmem_limit_bytes=VMEM_LIMIT_BYTES),
        name="mix_sample",
    )(x, cache_kt, cache_vt, cos, sin, gmix, w_in, sinks, gvg, coef, bias_rows, ga, gg, w_out)


def _rope_tables(pos):
    inv_freq = ROPE_THETA ** (-jnp.arange(0, HEAD_DIM, 2, dtype=F32) / HEAD_DIM)
    ang = pos.astype(F32)[:, None] * inv_freq[None, :]
    return jnp.tile(jnp.cos(ang), (1, 4)), jnp.tile(jnp.sin(ang), (1, 4))


def kernel(x_prompt, x_sample, cache_k_win, cache_v_win, norm_ffn1, ffn1_gate, ffn1_up, ffn1_down, norm_mix, w_in,
           attn_sinks, gmlp_v_norm, gmlp_w_s, gmlp_b_s, norm_attn_out, norm_gmlp_out, w_out, norm_ffn2, ffn2_gate,
           ffn2_up, ffn2_down, norm_final):
    depth = norm_ffn1.shape[0]
    b, s, _ = x_prompt.shape
    bd, t_new, _ = x_sample.shape
    w_buf = cache_k_win.shape[2]

    cos_s, sin_s = _rope_tables(PAST_LEN + jnp.arange(t_new, dtype=jnp.int32))
    cos_s, sin_s = jnp.tile(cos_s, (8 // t_new, 1)), jnp.tile(sin_s, (8 // t_new, 1))

    hp = x_prompt.reshape(b * s, D_MODEL)
    hs = x_sample.reshape(bd * t_new, D_MODEL)
    outs = [[] for _ in range(6)]
    ffn1_w = [ffn1_gate[0], ffn1_up[0], ffn1_down[0]]
    for l in range(depth):
        last = l == depth - 1
        row = lambda a: a[l].reshape(1, -1)

        wcat = gmlp_w_s[l]
        bias_full = jnp.repeat(gmlp_b_s[l].T, HEAD_DIM, axis=1)
        ws_small = gmlp_w_s[l][:, :t_new, :t_new]
        coef = jnp.stack([
            jnp.stack([ws_small[:, t, t - d] if t >= d else jnp.zeros((G_HEADS,), F32) for t in range(t_new)])
            for d in range(t_new)])
        coef = jnp.tile(jnp.repeat(coef, HEAD_DIM, axis=2), (1, 8 // t_new, 1))
        bias_rows = jnp.tile(jnp.repeat(gmlp_b_s[l][:, :t_new].T, HEAD_DIM, axis=1), (8 // t_new, 1))

        hp, hs, (w_in_b, w_out_b, wg2, wu2, wd2) = _ffn_half(
            hp, hs, norm_ffn1[l], *ffn1_w, cast=(w_in[l], w_out[l], ffn2_gate[l], ffn2_up[l], ffn2_down[l]))

        hp, kpt, vpt, gvpt = _mix_prompt(hp.reshape(b, s, D_MODEL), row(norm_mix), w_in_b, attn_sinks[l],
                                         row(gmlp_v_norm), wcat, bias_full, row(norm_attn_out), row(norm_gmlp_out),
                                         w_out_b)
        to_t = lambda c: c.transpose(0, 2, 3, 1).reshape(bd, KV_W, w_buf)
        hs, kst, vst, gvs = _mix_sample(hs, to_t(cache_k_win[l]), to_t(cache_v_win[l]), cos_s, sin_s, row(norm_mix),
                                        w_in_b, attn_sinks[l], row(gmlp_v_norm), coef, bias_rows, row(norm_attn_out),
                                        row(norm_gmlp_out), w_out_b, t_new)
        from_t = lambda c: c.reshape(bd, N_KV, HEAD_DIM, w_buf).transpose(0, 3, 1, 2)

        next_ffn1 = () if last else (ffn1_gate[l + 1], ffn1_up[l + 1], ffn1_down[l + 1])
        hp, hs, ffn1_w = _ffn_half(hp.reshape(b * s, D_MODEL), hs, norm_ffn2[l], wg2, wu2, wd2,
                                   gf=norm_final if last else None, cast=next_ffn1)

        outs[0].append(kpt.reshape(b, N_KV, HEAD_DIM, WINDOW).transpose(0, 3, 1, 2))
        outs[1].append(vpt.reshape(b, N_KV, HEAD_DIM, WINDOW).transpose(0, 3, 1, 2))
        outs[2].append(from_t(kst))
        outs[3].append(from_t(vst))
        outs[4].append(gvpt.reshape(b, G_HEADS, D_GMLP // G_HEADS, CHUNK).transpose(0, 3, 1, 2))
        outs[5].append(gvs.reshape(bd, t_new, G_HEADS, D_GMLP // G_HEADS))

    return (hp.reshape(b, s, D_MODEL), hs.reshape(bd, t_new, D_MODEL)) + tuple(jnp.stack(o) for o in outs)
```

```python
import functools

import jax
import jax.numpy as jnp
import numpy as np
from jax import lax
from jax.experimental import pallas as pl
from jax.experimental.pallas import tpu as pltpu

F32 = jnp.float32
BF16 = jnp.bfloat16

D_MODEL = 1024
D_FF = 2816
HEAD_DIM = 64
N_HEADS = 8
N_KV = 2
GQA = N_HEADS // N_KV
WINDOW = 128
CHUNK = 128
G_HEADS = 8
Q_W = N_HEADS * HEAD_DIM
KV_W = N_KV * HEAD_DIM
D_GMLP = 512
D_IN = Q_W + 2 * KV_W + 2 * D_GMLP
K_OFF = Q_W
V_OFF = K_OFF + KV_W
U_OFF = V_OFF + KV_W
GV_OFF = U_OFF + D_GMLP
ROPE_THETA = 10000.0
PAST_LEN = 16384
EPS = 1e-6
LOG2E = 1.4426950408889634
Q_SCALE = HEAD_DIM ** -0.5 * LOG2E
LANES = 128
BF16_SUBLANES = 16

TOKEN_TILE = 512
FF_CHUNK = 256
STAGE_BLOCK_BYTES = 64 * D_FF * 4
STAGE_SLOTS = 8
SEQ_GROUP = 8
GROUPS_PER_STEP = 4
VMEM_LIMIT_BYTES = 56 * 1024 * 1024

_NT = (((1,), (1,)), ((), ()))


def _rms(x, g):
    ms = jnp.mean(x * x, axis=-1, keepdims=True)
    return (x * lax.rsqrt(ms + EPS)) * g


def _resident(shape):
    zeros = (0,) * len(shape)
    return pl.BlockSpec(shape, lambda *_: zeros, pipeline_mode=pl.Buffered(1))


def _round_rows_to_bf16(src_hbm, dst_ref, stage_ref, sem):
    n_slots, rows, _ = stage_ref.shape
    assert src_hbm.shape[0] % rows == 0
    n_blocks = src_hbm.shape[0] // rows
    assert n_blocks >= n_slots

    def fetch(i, slot):
        return pltpu.make_async_copy(src_hbm.at[pl.ds(i * rows, rows), :], stage_ref.at[slot], sem.at[slot])

    for i in range(n_slots - 1):
        fetch(i, i).start()

    def body(i, carry):
        slot = i % n_slots

        @pl.when(i + n_slots - 1 < n_blocks)
        def _():
            fetch(i + n_slots - 1, (i + n_slots - 1) % n_slots).start()

        fetch(i, slot).wait()
        dst_ref[pl.ds(pl.multiple_of(i * rows, rows), rows), :] = stage_ref[slot].astype(BF16)
        return carry

    lax.fori_loop(0, n_blocks, body, 0)


def _ffn_kernel(*refs, final_norm, n_cast, f32_weights):
    refs = iter(refs)
    xp_ref, xn_ref, xs_ref, g_ref, wg_in, wu_in, wd_in = (next(refs) for _ in range(7))
    gf_ref = next(refs) if final_norm else None
    cast_in = [next(refs) for _ in range(n_cast)]
    yp_ref, ys_ref = next(refs), next(refs)
    cast_out = [next(refs) for _ in range(n_cast)]
    h_ref, act0_ref, act_ref = next(refs), next(refs), next(refs)
    if f32_weights:
        wg_ref, wu_ref, wd_ref, wide_stage, narrow_stage, wide_sem, narrow_sem = (next(refs) for _ in range(7))
    else:
        wg_ref, wu_ref, wd_ref = wg_in, wu_in, wd_in

    def gate_up(h, c):
        sl = slice(c * FF_CHUNK, (c + 1) * FF_CHUNK)
        gate = jnp.dot(h, wg_ref[:, sl], preferred_element_type=F32)
        up = jnp.dot(h, wu_ref[:, sl], preferred_element_type=F32)
        return (gate * jax.nn.sigmoid(gate) * up).astype(BF16)

    on_sample = pl.program_id(0) == 0

    @pl.when(on_sample)
    def _():
        if f32_weights:
            _round_rows_to_bf16(wg_in, wg_ref, wide_stage, wide_sem)
            _round_rows_to_bf16(wu_in, wu_ref, wide_stage, wide_sem)
            _round_rows_to_bf16(wd_in, wd_ref, narrow_stage, narrow_sem)
        h0 = _rms(xs_ref[...], g_ref[...]).astype(BF16)
        h_ref[...] = h0
        act0_ref[...] = gate_up(h0, 0)

    for c in range(1, D_FF // FF_CHUNK):
        act_ref[:, (c - 1) * FF_CHUNK:c * FF_CHUNK] = gate_up(h_ref[...], c)
    down0 = jnp.dot(act0_ref[...], wd_ref[0:FF_CHUNK, :], preferred_element_type=F32)
    hn = _rms(xn_ref[...], g_ref[...]).astype(BF16)
    h_ref[...] = hn
    act0_ref[...] = gate_up(hn, 0)
    if final_norm:
        x = jnp.where(on_sample, xs_ref[...], xp_ref[...])
        y = x + 0.5 * (down0 + jnp.dot(act_ref[...], wd_ref[FF_CHUNK:, :], preferred_element_type=F32))
        yp_ref[...] = _rms(y, gf_ref[...])
    else:
        for cols in (slice(0, D_MODEL // 2), slice(D_MODEL // 2, D_MODEL)):
            x = jnp.where(on_sample, xs_ref[:, cols], xp_ref[:, cols])
            yp_ref[:, cols] = x + 0.5 * (down0[:, cols] + jnp.dot(act_ref[...], wd_ref[FF_CHUNK:, cols],
                                                                   preferred_element_type=F32))

    @pl.when(on_sample)
    def _():
        ys_ref[...] = yp_ref[...]

    for src, dst in zip(cast_in, cast_out):
        dst[...] = src[...].astype(BF16)


def _cast_row_blocks(rows, n_steps):
    return max(d for d in range(1, n_steps + 1) if rows % d == 0 and (rows // d) % BF16_SUBLANES == 0)


def _ffn_half(xp, xs, g, wg, wu, wd, gf=None, cast=()):
    tm = TOKEN_TILE
    f32_weights = wg.dtype == F32
    assert all(w.dtype == wg.dtype for w in (wu, wd))
    n = xp.shape[0]
    assert n % tm == 0 and xs.shape[0] == tm
    n_tiles = n // tm
    prompt_spec = pl.BlockSpec((tm, D_MODEL), lambda i: (jnp.maximum(i - 1, 0), 0))
    next_spec = pl.BlockSpec((tm, D_MODEL), lambda i: (jnp.minimum(i, n_tiles - 1), 0))
    sample_in = _resident((tm, D_MODEL))
    sample_out = pl.BlockSpec((tm, D_MODEL), lambda i: (0, 0))
    weight_specs = ([pl.BlockSpec(memory_space=pl.ANY)] * 3 if f32_weights else
                    [_resident((D_MODEL, D_FF)), _resident((D_MODEL, D_FF)), _resident((D_FF, D_MODEL))])
    in_specs = [prompt_spec, next_spec, sample_in, _resident((1, D_MODEL))] + weight_specs
    args = [xp, xp, xs, g.reshape(1, D_MODEL), wg, wu, wd]
    if gf is not None:
        in_specs.append(_resident((1, D_MODEL)))
        args.append(gf.reshape(1, D_MODEL))
    out_specs = [prompt_spec, sample_out]
    out_shape = [jax.ShapeDtypeStruct((n, D_MODEL), F32), jax.ShapeDtypeStruct((tm, D_MODEL), F32)]
    cast_specs = []
    for w in cast:
        rows, cols = w.shape
        nb = _cast_row_blocks(rows, n_tiles)
        cast_specs.append(pl.BlockSpec((rows // nb, cols), lambda i, nb=nb: (jnp.minimum(i, nb - 1), 0)))
        out_shape.append(jax.ShapeDtypeStruct(w.shape, BF16))
    outs = pl.pallas_call(
        functools.partial(_ffn_kernel, final_norm=gf is not None, n_cast=len(cast), f32_weights=f32_weights),
        grid=(n_tiles + 1,),
        in_specs=in_specs + cast_specs,
        out_specs=out_specs + cast_specs,
        out_shape=out_shape,
        scratch_shapes=[pltpu.VMEM((tm, D_MODEL), BF16), pltpu.VMEM((tm, FF_CHUNK), BF16),
                        pltpu.VMEM((tm, D_FF - FF_CHUNK), BF16)] + ([
                            pltpu.VMEM((D_MODEL, D_FF), BF16), pltpu.VMEM((D_MODEL, D_FF), BF16),
                            pltpu.VMEM((D_FF, D_MODEL), BF16),
                            pltpu.VMEM((STAGE_SLOTS, STAGE_BLOCK_BYTES // (4 * D_FF), D_FF), F32),
                            pltpu.VMEM((STAGE_SLOTS, STAGE_BLOCK_BYTES // (4 * D_MODEL), D_MODEL), F32),
                            pltpu.SemaphoreType.DMA((STAGE_SLOTS,)), pltpu.SemaphoreType.DMA((STAGE_SLOTS,))]
                        if f32_weights else []),
        compiler_params=pltpu.CompilerParams(dimension_semantics=("arbitrary",), vmem_limit_bytes=VMEM_LIMIT_BYTES),
        name="ffn_final" if gf is not None else "ffn_half",
    )(*args, *cast)
    return outs[0], outs[1], list(outs[2:])


def _first_half(shape):
    return (lax.broadcasted_iota(jnp.int32, shape, 1) & (HEAD_DIM - 1)) < HEAD_DIM // 2


def _sign_sin(sin):
    return jnp.where(_first_half(sin.shape), -sin, sin)


def _rope(xg, cos, sin_signed):
    swapped = jnp.where(_first_half(xg.shape), pltpu.roll(xg, LANES - HEAD_DIM // 2, 1),
                        pltpu.roll(xg, HEAD_DIM // 2, 1))
    return xg * cos + swapped * sin_signed


def _in_proj(x, gmix, w_in, cos, sin_signed, gvn_gain):
    h = _rms(x, gmix).astype(BF16)
    z = jnp.dot(h, w_in, preferred_element_type=F32)
    q = jnp.concatenate(
        [(_rope(z[:, LANES * i:LANES * (i + 1)], cos, sin_signed) * Q_SCALE).astype(BF16) for i in range(Q_W // LANES)],
        axis=1)
    k = _rope(z[:, K_OFF:V_OFF], cos, sin_signed)
    v = z[:, V_OFF:U_OFF]
    u = _gelu(z[:, U_OFF:GV_OFF])
    gvn = _rms(_gelu(z[:, GV_OFF:]), gvn_gain)
    return q, k, v, u, gvn


def _out_proj(x, ya, yg, ga, gg, w_out):
    cat = jnp.concatenate([_rms(ya, ga).astype(BF16), _rms(yg, gg).astype(BF16)], axis=1)
    return x + jnp.dot(cat, w_out, preferred_element_type=F32)


def _softmax_sink(s, sink):
    m = jnp.maximum(jnp.max(s, axis=1, keepdims=True), sink)
    p = jnp.exp2(s - m)
    return p, jnp.sum(p, axis=1, keepdims=True) + jnp.exp2(sink - m)


def _gelu(x):
    k = -2.0 * (2.0 / jnp.pi) ** 0.5 * LOG2E
    return x / (1.0 + jnp.exp2(x * (k + (0.044715 * k) * (x * x))))


def _mix_prompt_kernel(xa_ref, xc_ref, coff_ref, soff_ref, cbase_ref, sbase_ref, gmix_ref, win_ref, sinks_ref, gvg_ref,
                       wcat_ref, bias_ref, ga_ref, gg_ref, wout_ref, o_ref, ko_ref, vo_ref, gvo_ref, q_s, k_s, v_s, u_s,
                       gv_s, cat_s, *, tm, tiles_per_seq, n_tiles):
    s = pl.program_id(0)

    @pl.when(s == 0)
    def _():
        q_s[1] = jnp.zeros((tm, Q_W), BF16)
        k_s[1] = jnp.zeros((tm + WINDOW, KV_W), BF16)
        v_s[1] = jnp.zeros((tm + WINDOW, KV_W), BF16)
        u_s[1] = jnp.zeros((tm, D_GMLP), F32)
        gv_s[1] = jnp.zeros((tm, D_GMLP), BF16)
        cat_s[0] = jnp.zeros((tm, D_MODEL), BF16)

    for parity in range(2):
        pl.when(s % 2 == parity)(functools.partial(
            _mix_prompt_step, s, parity, xa_ref, xc_ref, coff_ref, soff_ref, cbase_ref, sbase_ref, gmix_ref, win_ref,
            sinks_ref, gvg_ref, wcat_ref, bias_ref, ga_ref, gg_ref, wout_ref, o_ref, ko_ref, vo_ref, gvo_ref, q_s, k_s,
            v_s, u_s, gv_s, cat_s, tm=tm, tiles_per_seq=tiles_per_seq, n_tiles=n_tiles))


def _mix_prompt_step(s, cur, xa_ref, xc_ref, coff_ref, soff_ref, cbase_ref, sbase_ref, gmix_ref, win_ref, sinks_ref,
                     gvg_ref, wcat_ref, bias_ref, ga_ref, gg_ref, wout_ref, o_ref, ko_ref, vo_ref, gvo_ref, q_s, k_s, v_s,
                     u_s, gv_s, cat_s, *, tm, tiles_per_seq, n_tiles):
    oth = 1 - cur

    h = _rms(xa_ref[0], gmix_ref[...]).astype(BF16)
    tile_in_seq = jnp.minimum(s, n_tiles - 1) % tiles_per_seq
    cb = cbase_ref[pl.ds(tile_in_seq, 1), :]
    sb = sbase_ref[pl.ds(tile_in_seq, 1), :]
    cos = cb * coff_ref[...] - sb * soff_ref[...]
    sin = _sign_sin(sb * coff_ref[...] + cb * soff_ref[...])

    def proj_q():
        z = jnp.dot(h, win_ref[:, 0:Q_W], preferred_element_type=F32)
        for i in range(Q_W // LANES):
            q_s[cur, :, LANES * i:LANES * (i + 1)] = (
                _rope(z[:, LANES * i:LANES * (i + 1)], cos, sin) * Q_SCALE).astype(BF16)

    def proj_kv():
        z = jnp.dot(h, win_ref[:, K_OFF:U_OFF], preferred_element_type=F32)
        k = _rope(z[:, 0:KV_W], cos, sin)
        v = z[:, KV_W:]
        k_s[cur, 0:WINDOW, :] = k_s[oth, tm:tm + WINDOW, :]
        v_s[cur, 0:WINDOW, :] = v_s[oth, tm:tm + WINDOW, :]
        k_s[cur, WINDOW:, :] = k.astype(BF16)
        v_s[cur, WINDOW:, :] = v.astype(BF16)
        ko_ref[0] = k[tm - WINDOW:].T
        vo_ref[0] = v[tm - WINDOW:].T

    def proj_u():
        u_s[cur] = _gelu(jnp.dot(h, win_ref[:, U_OFF:GV_OFF], preferred_element_type=F32))

    def proj_gv():
        gvn = _rms(_gelu(jnp.dot(h, win_ref[:, GV_OFF:], preferred_element_type=F32)), gvg_ref[...])
        gv_s[cur] = gvn.astype(BF16)
        gvo_ref[0] = gvn[tm - CHUNK:].T

    def out_half(c):
        cols = slice(c * (D_MODEL // 2), (c + 1) * (D_MODEL // 2))

        def run():
            o_ref[0, :, cols] = xc_ref[0, :, cols] + jnp.dot(cat_s[cur], wout_ref[:, cols], preferred_element_type=F32)
        return run

    mid_fill = [[out_half(0)], [proj_q], [proj_kv, proj_u], [proj_gv]]
    end_fill = [[], [], [], [out_half(1)]]
    assert len(mid_fill) == len(end_fill) == tm // WINDOW

    wrow = lax.broadcasted_iota(jnp.int32, (CHUNK, CHUNK), 0)
    wcol = lax.broadcasted_iota(jnp.int32, (CHUNK, CHUNK), 1)
    wtril = [jnp.where(wcol <= wrow, wcat_ref[hd], 0.0).astype(BF16) for hd in range(G_HEADS)]
    wmix = [jnp.concatenate(wtril[2 * p:2 * p + 2], axis=1) for p in range(G_HEADS // 2)]
    lane = lax.broadcasted_iota(jnp.int32, (CHUNK, LANES), 1)
    low_head = lane < HEAD_DIM

    qi = lax.broadcasted_iota(jnp.int32, (WINDOW, 2 * WINDOW), 0)
    sj = lax.broadcasted_iota(jnp.int32, (WINDOW, 2 * WINDOW), 1)
    dist = WINDOW + qi - sj
    band = (dist >= 0) & (dist < WINDOW)
    first_lo = jnp.where((s + tiles_per_seq - 1) % tiles_per_seq == 0, WINDOW, 0)

    for j in range(tm // WINDOW):
        rows = slice(j * WINDOW, (j + 1) * WINDOW)
        mask = band & (sj >= first_lo) if j == 0 else band
        qb = q_s[oth, rows, :]
        scores = []
        for kv in range(N_KV):
            qs = jnp.concatenate(
                [qb[:, HEAD_DIM * hd:HEAD_DIM * (hd + 1)] for hd in range(kv * GQA, (kv + 1) * GQA)], axis=0)
            kb = k_s[oth, j * WINDOW:(j + 2) * WINDOW, HEAD_DIM * kv:HEAD_DIM * (kv + 1)]
            scores.append(lax.dot_general(qs, kb, _NT, preferred_element_type=F32))
        for run in mid_fill[j]:
            run()
        outs, dens = [], []
        for kv in range(N_KV):
            vb = v_s[oth, j * WINDOW:(j + 2) * WINDOW, HEAD_DIM * kv:HEAD_DIM * (kv + 1)]
            ps = []
            for g in range(GQA):
                sg = jnp.where(mask, scores[kv][g * WINDOW:(g + 1) * WINDOW], -jnp.inf)
                p, den = _softmax_sink(sg, sinks_ref[kv * GQA + g] * LOG2E)
                ps.append(p.astype(BF16))
                dens.append(den)
            outs.append(jnp.dot(jnp.concatenate(ps, axis=0), vb, preferred_element_type=F32))
        mixed = []
        for p in range(G_HEADS // 2):
            r = gv_s[oth, rows, LANES * p:LANES * (p + 1)]
            zero = jnp.zeros_like(r)
            rhs = jnp.concatenate([jnp.where(low_head, r, zero), jnp.where(low_head, zero, r)], axis=0)
            mixed.append(jnp.dot(wmix[p], rhs, preferred_element_type=F32))
        for run in end_fill[j]:
            run()
        ya = [outs[hd // GQA][(hd % GQA) * WINDOW:(hd % GQA + 1) * WINDOW] / dens[hd] for hd in range(N_HEADS)]
        cat_s[oth, rows, 0:Q_W] = _rms(jnp.concatenate(ya, axis=1), ga_ref[...]).astype(BF16)
        yg = u_s[oth, rows, :] * (jnp.concatenate(mixed, axis=1) + bias_ref[...])
        cat_s[oth, rows, Q_W:] = _rms(yg, gg_ref[...]).astype(BF16)


def _mix_prompt(x, gmix, w_in, sinks, gvg, wcat, bias_full, ga, gg, w_out):
    b, s, _ = x.shape
    tm = TOKEN_TILE
    assert s % tm == 0 and tm % WINDOW == 0
    tiles_per_seq = s // tm
    n_tiles = b * tiles_per_seq
    cos_off, sin_off = _rope_tables(jnp.arange(tm, dtype=jnp.int32))
    cos_base, sin_base = _rope_tables(tm * jnp.arange(tiles_per_seq, dtype=jnp.int32))
    proj_tile = lambda i: jnp.minimum(i, n_tiles - 1)
    out_tile = lambda i: jnp.maximum(i - 2, 0)
    x_tiles = x.reshape(n_tiles, tm, D_MODEL)
    last = lambda width: pl.BlockSpec((1, width, WINDOW), lambda i: (proj_tile(i) // tiles_per_seq, 0, 0))
    out, ko, vo, gvo = pl.pallas_call(
        functools.partial(_mix_prompt_kernel, tm=tm, tiles_per_seq=tiles_per_seq, n_tiles=n_tiles),
        grid=(n_tiles + 2,),
        in_specs=[pl.BlockSpec((1, tm, D_MODEL), lambda i: (proj_tile(i), 0, 0)),
                  pl.BlockSpec((1, tm, D_MODEL), lambda i: (out_tile(i), 0, 0)),
                  _resident((tm, LANES)), _resident((tm, LANES)), _resident((tiles_per_seq, LANES)),
                  _resident((tiles_per_seq, LANES)), _resident((1, D_MODEL)), _resident((D_MODEL, D_IN)),
                  pl.BlockSpec(memory_space=pltpu.SMEM), _resident((1, D_GMLP)),
                  _resident((G_HEADS, CHUNK, CHUNK)), _resident((CHUNK, D_GMLP)), _resident((1, Q_W)),
                  _resident((1, D_GMLP)), _resident((D_MODEL, D_MODEL))],
        out_specs=[pl.BlockSpec((1, tm, D_MODEL), lambda i: (out_tile(i), 0, 0)), last(KV_W), last(KV_W),
                   last(D_GMLP)],
        out_shape=[jax.ShapeDtypeStruct((n_tiles, tm, D_MODEL), F32), jax.ShapeDtypeStruct((b, KV_W, WINDOW), F32),
                   jax.ShapeDtypeStruct((b, KV_W, WINDOW), F32), jax.ShapeDtypeStruct((b, D_GMLP, CHUNK), F32)],
        scratch_shapes=[pltpu.VMEM((2, tm, Q_W), BF16), pltpu.VMEM((2, tm + WINDOW, KV_W), BF16),
                        pltpu.VMEM((2, tm + WINDOW, KV_W), BF16), pltpu.VMEM((2, tm, D_GMLP), F32),
                        pltpu.VMEM((2, tm, D_GMLP), BF16), pltpu.VMEM((2, tm, D_MODEL), BF16)],
        compiler_params=pltpu.CompilerParams(dimension_semantics=("arbitrary",), vmem_limit_bytes=VMEM_LIMIT_BYTES),
        name="mix_prompt",
    )(x_tiles, x_tiles, cos_off, sin_off, cos_base, sin_base, gmix, w_in, sinks, gvg, wcat, bias_full, ga, gg, w_out)
    return out.reshape(b, s, D_MODEL), ko, vo, gvo


def _mix_sample_kernel(x_ref, ckt_ref, cvt_ref, cos_ref, sin_ref, gmix_ref, win_ref, sinks_ref, gvg_ref, coef_ref,
                       biasr_ref, ga_ref, gg_ref, wout_ref, o_ref, kot_ref, vot_ref, gvo_ref, q_s, k_s, v_s, ya_s,
                       yg_s, *, t_new, w_buf):
    step = pl.program_id(0)
    n_tok = x_ref.shape[0]
    step_seqs = SEQ_GROUP * GROUPS_PER_STEP
    step_rows = step_seqs * t_new
    grp_rows = SEQ_GROUP * t_new
    grp_keys = SEQ_GROUP * w_buf
    sub = 8

    @pl.when(step == 0)
    def _():
        tile = lambda tab: jnp.broadcast_to(tab[None], (n_tok // sub, sub, LANES)).reshape(n_tok, LANES)
        q, k, v, u, gvn = _in_proj(x_ref[...], gmix_ref[...], win_ref[...], tile(cos_ref[...]),
                                   tile(_sign_sin(sin_ref[...])), gvg_ref[...])
        q_s[...] = q
        k_s[...] = k
        v_s[...] = v
        gvo_ref[...] = gvn
        g3 = gvn.reshape(n_tok // sub, sub, D_GMLP)
        trow = lax.broadcasted_iota(jnp.int32, (1, sub, D_GMLP), 1) & (t_new - 1)
        mixed = biasr_ref[...][None] + coef_ref[0][None] * g3
        for d in range(1, t_new):
            shifted = jnp.where(trow >= d, pltpu.roll(g3, d, 1), 0.0)
            mixed = mixed + coef_ref[d][None] * shifted
        yg_s[...] = (u.reshape(n_tok // sub, sub, D_GMLP) * mixed).reshape(n_tok, D_GMLP)

    row0 = pl.multiple_of(step * step_rows, step_rows)
    q_step = q_s[pl.ds(row0, step_rows), :]
    kn = k_s[pl.ds(row0, step_rows), :]
    vn = v_s[pl.ds(row0, step_rows), :]

    knt = kn.T
    vnt = vn.T
    tail = lax.broadcasted_iota(jnp.int32, (KV_W, w_buf), 1) >= w_buf - t_new
    for b in range(step_seqs):
        shift = (w_buf - t_new - t_new * b) % w_buf
        kot_ref[b] = jnp.where(tail, pltpu.roll(knt, shift, 1) if shift else knt,
                               pltpu.roll(ckt_ref[b], w_buf - t_new, 1))
        vot_ref[b] = jnp.where(tail, pltpu.roll(vnt, shift, 1) if shift else vnt,
                               pltpu.roll(cvt_ref[b], w_buf - t_new, 1))

    knb = kn.astype(BF16)
    vnb = vn.astype(BF16)

    n_rows = GQA * grp_rows
    shift_t = t_new.bit_length() - 1
    shift_w = w_buf.bit_length() - 1
    r = lax.broadcasted_iota(jnp.int32, (n_rows, grp_keys), 0)
    c = lax.broadcasted_iota(jnp.int32, (n_rows, grp_keys), 1)
    mask_c = ((c >> shift_w) == ((r & (grp_rows - 1)) >> shift_t)) & ((c & (w_buf - 1)) > (r & (t_new - 1)))
    r2 = lax.broadcasted_iota(jnp.int32, (n_rows, grp_rows), 0)
    c2 = lax.broadcasted_iota(jnp.int32, (n_rows, grp_rows), 1)
    mask_n = ((c2 >> shift_t) == ((r2 & (grp_rows - 1)) >> shift_t)) & ((c2 & (t_new - 1)) <= (r2 & (t_new - 1)))
    row_head = lax.broadcasted_iota(jnp.int32, (n_rows, 1), 0) >> (grp_rows.bit_length() - 1)

    for grp in range(GROUPS_PER_STEP):
        rows = slice(grp * grp_rows, (grp + 1) * grp_rows)
        seqs = range(grp * SEQ_GROUP, (grp + 1) * SEQ_GROUP)
        for kv in range(N_KV):
            heads = [kv * GQA + i for i in range(GQA)]
            lanes = slice(HEAD_DIM * kv, HEAD_DIM * (kv + 1))
            kt = jnp.concatenate([ckt_ref[b, lanes, :] for b in seqs], axis=1).astype(BF16)
            vt = jnp.concatenate([cvt_ref[b, lanes, :] for b in seqs], axis=1).astype(BF16)
            qs = jnp.concatenate([q_step[rows, HEAD_DIM * hd:HEAD_DIM * (hd + 1)] for hd in heads], axis=0)
            s_c = jnp.where(mask_c, jnp.dot(qs, kt, preferred_element_type=F32), -jnp.inf)
            s_n = jnp.where(mask_n, lax.dot_general(qs, knb[rows, lanes], _NT, preferred_element_type=F32), -jnp.inf)
            sink = jnp.full((n_rows, 1), sinks_ref[heads[0]] * LOG2E, F32)
            for i in range(1, GQA):
                sink = jnp.where(row_head == i, sinks_ref[heads[i]] * LOG2E, sink)
            m = jnp.maximum(jnp.maximum(jnp.max(s_c, axis=1, keepdims=True), jnp.max(s_n, axis=1, keepdims=True)),
                            sink)
            p_c = jnp.exp2(s_c - m)
            p_n = jnp.exp2(s_n - m)
            den = jnp.sum(p_c, axis=1, keepdims=True) + jnp.sum(p_n, axis=1, keepdims=True) + jnp.exp2(sink - m)
            o = (lax.dot_general(p_c.astype(BF16), vt, _NT, preferred_element_type=F32)
                 + jnp.dot(p_n.astype(BF16), vnb[rows, lanes], preferred_element_type=F32)) / den
            for i, hd in enumerate(heads):
                ya_s[pl.ds(row0 + grp * grp_rows, grp_rows), HEAD_DIM * hd:HEAD_DIM * (hd + 1)] = (
                    o[i * grp_rows:(i + 1) * grp_rows])

    @pl.when(step == pl.num_programs(0) - 1)
    def _():
        o_ref[...] = _out_proj(x_ref[...], ya_s[...], yg_s[...], ga_ref[...], gg_ref[...], wout_ref[...])


def _mix_sample(x, cache_kt, cache_vt, cos, sin, gmix, w_in, sinks, gvg, coef, bias_rows, ga, gg, w_out, t_new):
    n_tok = x.shape[0]
    n_seq, _, w_buf = cache_kt.shape
    step_seqs = SEQ_GROUP * GROUPS_PER_STEP
    assert n_seq % step_seqs == 0 and n_tok == n_seq * t_new
    assert t_new & (t_new - 1) == 0 and w_buf & (w_buf - 1) == 0 and 8 % t_new == 0
    assert step_seqs * t_new == w_buf == LANES
    cache_spec = pl.BlockSpec((step_seqs, KV_W, w_buf), lambda i: (i, 0, 0))
    return pl.pallas_call(
        functools.partial(_mix_sample_kernel, t_new=t_new, w_buf=w_buf),
        grid=(n_seq // step_seqs,),
        in_specs=[_resident((n_tok, D_MODEL)), cache_spec, cache_spec, _resident((8, LANES)),
                  _resident((8, LANES)), _resident((1, D_MODEL)), _resident((D_MODEL, D_IN)),
                  pl.BlockSpec(memory_space=pltpu.SMEM), _resident((1, D_GMLP)), _resident((t_new, 8, D_GMLP)),
                  _resident((8, D_GMLP)), _resident((1, Q_W)), _resident((1, D_GMLP)),
                  _resident((D_MODEL, D_MODEL))],
        out_specs=[pl.BlockSpec((n_tok, D_MODEL), lambda i: (0, 0)), cache_spec, cache_spec,
                   pl.BlockSpec((n_tok, D_GMLP), lambda i: (0, 0))],
        out_shape=[jax.ShapeDtypeStruct((n_tok, D_MODEL), F32), jax.ShapeDtypeStruct(cache_kt.shape, F32),
                   jax.ShapeDtypeStruct(cache_vt.shape, F32), jax.ShapeDtypeStruct((n_tok, D_GMLP), F32)],
        scratch_shapes=[pltpu.VMEM((n_tok, Q_W), BF16), pltpu.VMEM((n_tok, KV_W), F32), pltpu.VMEM((n_tok, KV_W), F32),
                        pltpu.VMEM((n_tok, Q_W), F32), pltpu.VMEM((n_tok, D_GMLP), F32)],
        compiler_params=pltpu.CompilerParams(dimension_semantics=("arbitrary",), vmem_limit_bytes=VMEM_LIMIT_BYTES),
        name="mix_sample",
    )(x, cache_kt, cache_vt, cos, sin, gmix, w_in, sinks, gvg, coef, bias_rows, ga, gg, w_out)


def _rope_tables(pos):
    inv_freq = ROPE_THETA ** (-jnp.arange(0, HEAD_DIM, 2, dtype=F32) / HEAD_DIM)
    ang = pos.astype(F32)[:, None] * jnp.tile(inv_freq, 2 * LANES // HEAD_DIM)[None, :]
    return jnp.cos(ang), jnp.sin(ang)


def kernel(x_prompt, x_sample, cache_k_win, cache_v_win, norm_ffn1, ffn1_gate, ffn1_up, ffn1_down, norm_mix, w_in,
           attn_sinks, gmlp_v_norm, gmlp_w_s, gmlp_b_s, norm_attn_out, norm_gmlp_out, w_out, norm_ffn2, ffn2_gate,
           ffn2_up, ffn2_down, norm_final):
    depth = norm_ffn1.shape[0]
    b, s, _ = x_prompt.shape
    bd, t_new, _ = x_sample.shape
    w_buf = cache_k_win.shape[2]

    cos_s, sin_s = _rope_tables(PAST_LEN + jnp.arange(t_new, dtype=jnp.int32))
    cos_s, sin_s = jnp.tile(cos_s, (8 // t_new, 1)), jnp.tile(sin_s, (8 // t_new, 1))

    hp = x_prompt.reshape(b * s, D_MODEL)
    hs = x_sample.reshape(bd * t_new, D_MODEL)
    outs = [[] for _ in range(6)]
    ffn1_w = [ffn1_gate[0], ffn1_up[0], ffn1_down[0]]
    for l in range(depth):
        last = l == depth - 1
        row = lambda a: a[l].reshape(1, -1)

        wcat = gmlp_w_s[l]
        bias_full = jnp.repeat(gmlp_b_s[l].T, HEAD_DIM, axis=1)
        d_idx, t_idx = np.meshgrid(np.arange(t_new), np.arange(t_new), indexing="ij")
        pick = (np.arange(t_new)[None, None, :] == (t_idx - d_idx)[:, :, None]).astype(np.float32)
        coef = (gmlp_w_s[l][:, None, :t_new, :t_new] * pick[None]).sum(-1)
        coef = jnp.tile(jnp.repeat(coef.transpose(1, 2, 0), HEAD_DIM, axis=2), (1, 8 // t_new, 1))
        bias_rows = jnp.tile(jnp.repeat(gmlp_b_s[l][:, :t_new].T, HEAD_DIM, axis=1), (8 // t_new, 1))

        hp, hs, (w_in_b, w_out_b, wg2, wu2, wd2) = _ffn_half(
            hp, hs, norm_ffn1[l], *ffn1_w, cast=(w_in[l], w_out[l], ffn2_gate[l], ffn2_up[l], ffn2_down[l]))

        hp, kpt, vpt, gvpt = _mix_prompt(hp.reshape(b, s, D_MODEL), row(norm_mix), w_in_b, attn_sinks[l],
                                         row(gmlp_v_norm), wcat, bias_full, row(norm_attn_out), row(norm_gmlp_out),
                                         w_out_b)
        to_t = lambda c: c.transpose(0, 2, 3, 1).reshape(bd, KV_W, w_buf)
        hs, kst, vst, gvs = _mix_sample(hs, to_t(cache_k_win[l]), to_t(cache_v_win[l]), cos_s, sin_s, row(norm_mix),
                                        w_in_b, attn_sinks[l], row(gmlp_v_norm), coef, bias_rows, row(norm_attn_out),
                                        row(norm_gmlp_out), w_out_b, t_new)
        from_t = lambda c: c.reshape(bd, N_KV, HEAD_DIM, w_buf).transpose(0, 3, 1, 2)

        next_ffn1 = () if last else (ffn1_gate[l + 1], ffn1_up[l + 1], ffn1_down[l + 1])
        hp, hs, ffn1_w = _ffn_half(hp.reshape(b * s, D_MODEL), hs, norm_ffn2[l], wg2, wu2, wd2,
                                   gf=norm_final if last else None, cast=next_ffn1)

        outs[0].append(kpt.reshape(b, N_KV, HEAD_DIM, WINDOW).transpose(0, 3, 1, 2))
        outs[1].append(vpt.reshape(b, N_KV, HEAD_DIM, WINDOW).transpose(0, 3, 1, 2))
        outs[2].append(from_t(kst))
        outs[3].append(from_t(vst))
        outs[4].append(gvpt.reshape(b, G_HEADS, D_GMLP // G_HEADS, CHUNK).transpose(0, 3, 1, 2))
        outs[5].append(gvs.reshape(bd, t_new, G_HEADS, D_GMLP // G_HEADS))

    return (hp.reshape(b, s, D_MODEL), hs.reshape(bd, t_new, D_MODEL)) + tuple(jnp.stack(o) for o in outs)
```

```python
import functools

import jax
import jax.numpy as jnp
from jax import lax
from jax.experimental import pallas as pl
from jax.experimental.pallas import tpu as pltpu

F32 = jnp.float32
BF16 = jnp.bfloat16

D_MODEL = 1024
D_FF = 2816
HEAD_DIM = 64
N_HEADS = 8
N_KV = 2
GQA = N_HEADS // N_KV
WINDOW = 128
CHUNK = 128
G_HEADS = 8
Q_W = N_HEADS * HEAD_DIM
KV_W = N_KV * HEAD_DIM
D_GMLP = 512
D_IN = Q_W + 2 * KV_W + 2 * D_GMLP
K_OFF = Q_W
V_OFF = K_OFF + KV_W
U_OFF = V_OFF + KV_W
GV_OFF = U_OFF + D_GMLP
ROPE_THETA = 10000.0
PAST_LEN = 16384
EPS = 1e-6
LOG2E = 1.4426950408889634
Q_SCALE = HEAD_DIM ** -0.5 * LOG2E
LANES = 128
BF16_SUBLANES = 16

TOKEN_TILE = 512
FF_CHUNK = 256
STAGE_BLOCK_BYTES = 64 * D_FF * 4
STAGE_SLOTS = 8
SEQ_GROUP = 8
GROUPS_PER_STEP = 4
VMEM_LIMIT_BYTES = 56 * 1024 * 1024

_NT = (((1,), (1,)), ((), ()))


def _rms(x, g):
    ms = jnp.mean(x * x, axis=-1, keepdims=True)
    return (x * lax.rsqrt(ms + EPS)) * g


def _resident(shape):
    zeros = (0,) * len(shape)
    return pl.BlockSpec(shape, lambda *_: zeros, pipeline_mode=pl.Buffered(1))


def _round_rows_to_bf16(src_hbm, dst_ref, stage_ref, sem):
    n_slots, rows, _ = stage_ref.shape
    assert src_hbm.shape[0] % rows == 0
    n_blocks = src_hbm.shape[0] // rows
    assert n_blocks >= n_slots

    def fetch(i, slot):
        return pltpu.make_async_copy(src_hbm.at[pl.ds(i * rows, rows), :], stage_ref.at[slot], sem.at[slot])

    for i in range(n_slots - 1):
        fetch(i, i).start()

    def body(i, carry):
        slot = i % n_slots

        @pl.when(i + n_slots - 1 < n_blocks)
        def _():
            fetch(i + n_slots - 1, (i + n_slots - 1) % n_slots).start()

        fetch(i, slot).wait()
        dst_ref[pl.ds(pl.multiple_of(i * rows, rows), rows), :] = stage_ref[slot].astype(BF16)
        return carry

    lax.fori_loop(0, n_blocks, body, 0)


def _ffn_kernel(*refs, final_norm, n_cast, f32_weights):
    refs = iter(refs)
    xp_ref, xn_ref, xs_ref, g_ref, wg_in, wu_in, wd_in = (next(refs) for _ in range(7))
    gf_ref = next(refs) if final_norm else None
    cast_in = [next(refs) for _ in range(n_cast)]
    yp_ref, ys_ref = next(refs), next(refs)
    cast_out = [next(refs) for _ in range(n_cast)]
    h_ref, act0_ref, act_ref = next(refs), next(refs), next(refs)
    if f32_weights:
        wg_ref, wu_ref, wd_ref, wide_stage, narrow_stage, wide_sem, narrow_sem = (next(refs) for _ in range(7))
    else:
        wg_ref, wu_ref, wd_ref = wg_in, wu_in, wd_in

    def gate_up(h, c):
        sl = slice(c * FF_CHUNK, (c + 1) * FF_CHUNK)
        gate = jnp.dot(h, wg_ref[:, sl], preferred_element_type=F32)
        up = jnp.dot(h, wu_ref[:, sl], preferred_element_type=F32)
        return (gate * jax.nn.sigmoid(gate) * up).astype(BF16)

    on_sample = pl.program_id(0) == 0

    @pl.when(on_sample)
    def _():
        if f32_weights:
            _round_rows_to_bf16(wg_in, wg_ref, wide_stage, wide_sem)
            _round_rows_to_bf16(wu_in, wu_ref, wide_stage, wide_sem)
            _round_rows_to_bf16(wd_in, wd_ref, narrow_stage, narrow_sem)
        h0 = _rms(xs_ref[...], g_ref[...]).astype(BF16)
        h_ref[...] = h0
        act0_ref[...] = gate_up(h0, 0)

    for c in range(1, D_FF // FF_CHUNK):
        act_ref[:, (c - 1) * FF_CHUNK:c * FF_CHUNK] = gate_up(h_ref[...], c)
    down0 = jnp.dot(act0_ref[...], wd_ref[0:FF_CHUNK, :], preferred_element_type=F32)
    hn = _rms(xn_ref[...], g_ref[...]).astype(BF16)
    h_ref[...] = hn
    act0_ref[...] = gate_up(hn, 0)
    if final_norm:
        x = jnp.where(on_sample, xs_ref[...], xp_ref[...])
        y = x + 0.5 * (down0 + jnp.dot(act_ref[...], wd_ref[FF_CHUNK:, :], preferred_element_type=F32))
        yp_ref[...] = _rms(y, gf_ref[...])
    else:
        for cols in (slice(0, D_MODEL // 2), slice(D_MODEL // 2, D_MODEL)):
            x = jnp.where(on_sample, xs_ref[:, cols], xp_ref[:, cols])
            yp_ref[:, cols] = x + 0.5 * (down0[:, cols] + jnp.dot(act_ref[...], wd_ref[FF_CHUNK:, cols],
                                                                   preferred_element_type=F32))

    @pl.when(on_sample)
    def _():
        ys_ref[...] = yp_ref[...]

    for src, dst in zip(cast_in, cast_out):
        dst[...] = src[...].astype(BF16)


def _cast_row_blocks(rows, n_steps):
    return max(d for d in range(1, n_steps + 1) if rows % d == 0 and (rows // d) % BF16_SUBLANES == 0)


def _ffn_half(xp, xs, g, wg, wu, wd, gf=None, cast=()):
    tm = TOKEN_TILE
    f32_weights = wg.dtype == F32
    assert all(w.dtype == wg.dtype for w in (wu, wd))
    n = xp.shape[0]
    assert n % tm == 0 and xs.shape[0] == tm
    n_tiles = n // tm
    prompt_spec = pl.BlockSpec((tm, D_MODEL), lambda i: (jnp.maximum(i - 1, 0), 0))
    next_spec = pl.BlockSpec((tm, D_MODEL), lambda i: (jnp.minimum(i, n_tiles - 1), 0))
    sample_in = _resident((tm, D_MODEL))
    sample_out = pl.BlockSpec((tm, D_MODEL), lambda i: (0, 0))
    weight_specs = ([pl.BlockSpec(memory_space=pl.ANY)] * 3 if f32_weights else
                    [_resident((D_MODEL, D_FF)), _resident((D_MODEL, D_FF)), _resident((D_FF, D_MODEL))])
    in_specs = [prompt_spec, next_spec, sample_in, _resident((1, D_MODEL))] + weight_specs
    args = [xp, xp, xs, g.reshape(1, D_MODEL), wg, wu, wd]
    if gf is not None:
        in_specs.append(_resident((1, D_MODEL)))
        args.append(gf.reshape(1, D_MODEL))
    out_specs = [prompt_spec, sample_out]
    out_shape = [jax.ShapeDtypeStruct((n, D_MODEL), F32), jax.ShapeDtypeStruct((tm, D_MODEL), F32)]
    cast_specs = []
    for w in cast:
        rows, cols = w.shape
        nb = _cast_row_blocks(rows, n_tiles)
        cast_specs.append(pl.BlockSpec((rows // nb, cols), lambda i, nb=nb: (jnp.minimum(i, nb - 1), 0)))
        out_shape.append(jax.ShapeDtypeStruct(w.shape, BF16))
    outs = pl.pallas_call(
        functools.partial(_ffn_kernel, final_norm=gf is not None, n_cast=len(cast), f32_weights=f32_weights),
        grid=(n_tiles + 1,),
        in_specs=in_specs + cast_specs,
        out_specs=out_specs + cast_specs,
        out_shape=out_shape,
        scratch_shapes=[pltpu.VMEM((tm, D_MODEL), BF16), pltpu.VMEM((tm, FF_CHUNK), BF16),
                        pltpu.VMEM((tm, D_FF - FF_CHUNK), BF16)] + ([
                            pltpu.VMEM((D_MODEL, D_FF), BF16), pltpu.VMEM((D_MODEL, D_FF), BF16),
                            pltpu.VMEM((D_FF, D_MODEL), BF16),
                            pltpu.VMEM((STAGE_SLOTS, STAGE_BLOCK_BYTES // (4 * D_FF), D_FF), F32),
                            pltpu.VMEM((STAGE_SLOTS, STAGE_BLOCK_BYTES // (4 * D_MODEL), D_MODEL), F32),
                            pltpu.SemaphoreType.DMA((STAGE_SLOTS,)), pltpu.SemaphoreType.DMA((STAGE_SLOTS,))]
                        if f32_weights else []),
        compiler_params=pltpu.CompilerParams(dimension_semantics=("arbitrary",), vmem_limit_bytes=VMEM_LIMIT_BYTES),
        name="ffn_final" if gf is not None else "ffn_half",
    )(*args, *cast)
    return outs[0], outs[1], list(outs[2:])


def _first_half(shape):
    return (lax.broadcasted_iota(jnp.int32, shape, 1) & (HEAD_DIM - 1)) < HEAD_DIM // 2


def _sign_sin(sin):
    return jnp.where(_first_half(sin.shape), -sin, sin)


def _rope(xg, cos, sin_signed):
    swapped = jnp.where(_first_half(xg.shape), pltpu.roll(xg, LANES - HEAD_DIM // 2, 1),
                        pltpu.roll(xg, HEAD_DIM // 2, 1))
    return xg * cos + swapped * sin_signed


def _in_proj(x, gmix, w_in, cos, sin_signed, gvn_gain):
    h = _rms(x, gmix).astype(BF16)
    z = jnp.dot(h, w_in, preferred_element_type=F32)
    q = jnp.concatenate(
        [(_rope(z[:, LANES * i:LANES * (i + 1)], cos, sin_signed) * Q_SCALE).astype(BF16) for i in range(Q_W // LANES)],
        axis=1)
    k = _rope(z[:, K_OFF:V_OFF], cos, sin_signed)
    v = z[:, V_OFF:U_OFF]
    u = _gelu(z[:, U_OFF:GV_OFF])
    gvn = _rms(_gelu(z[:, GV_OFF:]), gvn_gain)
    return q, k, v, u, gvn


def _out_proj(x, ya, yg, ga, gg, w_out):
    cat = jnp.concatenate([_rms(ya, ga).astype(BF16), _rms(yg, gg).astype(BF16)], axis=1)
    return x + jnp.dot(cat, w_out, preferred_element_type=F32)


def _softmax_sink(s, sink):
    m = jnp.maximum(jnp.max(s, axis=1, keepdims=True), sink)
    p = jnp.exp2(s - m)
    return p, jnp.sum(p, axis=1, keepdims=True) + jnp.exp2(sink - m)


def _gelu(x):
    k = -2.0 * (2.0 / jnp.pi) ** 0.5 * LOG2E
    return x / (1.0 + jnp.exp2(x * (k + (0.044715 * k) * (x * x))))


def _mix_prompt_kernel(xa_ref, xc_ref, coff_ref, soff_ref, cbase_ref, sbase_ref, gmix_ref, win_ref, sinks_ref, gvg_ref,
                       wcat_ref, bias_ref, ga_ref, gg_ref, wout_ref, o_ref, ko_ref, vo_ref, gvo_ref, q_s, k_s, v_s, u_s,
                       gv_s, cat_s, *, tm, tiles_per_seq, n_tiles):
    s = pl.program_id(0)

    @pl.when(s == 0)
    def _():
        q_s[1] = jnp.zeros((tm, Q_W), BF16)
        k_s[1] = jnp.zeros((tm + WINDOW, KV_W), BF16)
        v_s[1] = jnp.zeros((tm + WINDOW, KV_W), BF16)
        u_s[1] = jnp.zeros((tm, D_GMLP), F32)
        gv_s[1] = jnp.zeros((tm, D_GMLP), BF16)
        cat_s[0] = jnp.zeros((tm, D_MODEL), BF16)

    for parity in range(2):
        pl.when(s % 2 == parity)(functools.partial(
            _mix_prompt_step, s, parity, xa_ref, xc_ref, coff_ref, soff_ref, cbase_ref, sbase_ref, gmix_ref, win_ref,
            sinks_ref, gvg_ref, wcat_ref, bias_ref, ga_ref, gg_ref, wout_ref, o_ref, ko_ref, vo_ref, gvo_ref, q_s, k_s,
            v_s, u_s, gv_s, cat_s, tm=tm, tiles_per_seq=tiles_per_seq, n_tiles=n_tiles))


def _mix_prompt_step(s, cur, xa_ref, xc_ref, coff_ref, soff_ref, cbase_ref, sbase_ref, gmix_ref, win_ref, sinks_ref,
                     gvg_ref, wcat_ref, bias_ref, ga_ref, gg_ref, wout_ref, o_ref, ko_ref, vo_ref, gvo_ref, q_s, k_s, v_s,
                     u_s, gv_s, cat_s, *, tm, tiles_per_seq, n_tiles):
    oth = 1 - cur

    h = _rms(xa_ref[0], gmix_ref[...]).astype(BF16)
    tile_in_seq = jnp.minimum(s, n_tiles - 1) % tiles_per_seq
    cb = cbase_ref[pl.ds(tile_in_seq, 1), :]
    sb = sbase_ref[pl.ds(tile_in_seq, 1), :]
    cos = cb * coff_ref[...] - sb * soff_ref[...]
    sin = _sign_sin(sb * coff_ref[...] + cb * soff_ref[...])

    def proj_q():
        z = jnp.dot(h, win_ref[:, 0:Q_W], preferred_element_type=F32)
        for i in range(Q_W // LANES):
            q_s[cur, :, LANES * i:LANES * (i + 1)] = (
                _rope(z[:, LANES * i:LANES * (i + 1)], cos, sin) * Q_SCALE).astype(BF16)

    def proj_kv():
        z = jnp.dot(h, win_ref[:, K_OFF:U_OFF], preferred_element_type=F32)
        k = _rope(z[:, 0:KV_W], cos, sin)
        v = z[:, KV_W:]
        k_s[cur, 0:WINDOW, :] = k_s[oth, tm:tm + WINDOW, :]
        v_s[cur, 0:WINDOW, :] = v_s[oth, tm:tm + WINDOW, :]
        k_s[cur, WINDOW:, :] = k.astype(BF16)
        v_s[cur, WINDOW:, :] = v.astype(BF16)
        ko_ref[0] = k[tm - WINDOW:].T
        vo_ref[0] = v[tm - WINDOW:].T

    def proj_u():
        u_s[cur] = _gelu(jnp.dot(h, win_ref[:, U_OFF:GV_OFF], preferred_element_type=F32))

    def proj_gv():
        gvn = _rms(_gelu(jnp.dot(h, win_ref[:, GV_OFF:], preferred_element_type=F32)), gvg_ref[...])
        gv_s[cur] = gvn.astype(BF16)
        gvo_ref[0] = gvn[tm - CHUNK:].T

    def out_half(c):
        cols = slice(c * (D_MODEL // 2), (c + 1) * (D_MODEL // 2))

        def run():
            o_ref[0, :, cols] = xc_ref[0, :, cols] + jnp.dot(cat_s[cur], wout_ref[:, cols], preferred_element_type=F32)
        return run

    mid_fill = [[out_half(0)], [proj_q], [proj_kv, proj_u], [proj_gv]]
    end_fill = [[], [], [], [out_half(1)]]
    assert len(mid_fill) == len(end_fill) == tm // WINDOW

    wrow = lax.broadcasted_iota(jnp.int32, (CHUNK, CHUNK), 0)
    wcol = lax.broadcasted_iota(jnp.int32, (CHUNK, CHUNK), 1)
    wtril = [jnp.where(wcol <= wrow, wcat_ref[hd], 0.0).astype(BF16) for hd in range(G_HEADS)]
    wmix = [jnp.concatenate(wtril[2 * p:2 * p + 2], axis=1) for p in range(G_HEADS // 2)]
    lane = lax.broadcasted_iota(jnp.int32, (CHUNK, LANES), 1)
    low_head = lane < HEAD_DIM

    qi = lax.broadcasted_iota(jnp.int32, (WINDOW, 2 * WINDOW), 0)
    sj = lax.broadcasted_iota(jnp.int32, (WINDOW, 2 * WINDOW), 1)
    dist = WINDOW + qi - sj
    band = (dist >= 0) & (dist < WINDOW)
    first_lo = jnp.where((s + tiles_per_seq - 1) % tiles_per_seq == 0, WINDOW, 0)

    for j in range(tm // WINDOW):
        rows = slice(j * WINDOW, (j + 1) * WINDOW)
        mask = band & (sj >= first_lo) if j == 0 else band
        qb = q_s[oth, rows, :]
        scores = []
        for kv in range(N_KV):
            qs = jnp.concatenate(
                [qb[:, HEAD_DIM * hd:HEAD_DIM * (hd + 1)] for hd in range(kv * GQA, (kv + 1) * GQA)], axis=0)
            kb = k_s[oth, j * WINDOW:(j + 2) * WINDOW, HEAD_DIM * kv:HEAD_DIM * (kv + 1)]
            scores.append(lax.dot_general(qs, kb, _NT, preferred_element_type=F32))
        for run in mid_fill[j]:
            run()
        outs, dens = [], []
        for kv in range(N_KV):
            vb = v_s[oth, j * WINDOW:(j + 2) * WINDOW, HEAD_DIM * kv:HEAD_DIM * (kv + 1)]
            ps = []
            for g in range(GQA):
                sg = jnp.where(mask, scores[kv][g * WINDOW:(g + 1) * WINDOW], -jnp.inf)
                p, den = _softmax_sink(sg, sinks_ref[kv * GQA + g] * LOG2E)
                ps.append(p.astype(BF16))
                dens.append(den)
            outs.append(jnp.dot(jnp.concatenate(ps, axis=0), vb, preferred_element_type=F32))
        mixed = []
        for p in range(G_HEADS // 2):
            r = gv_s[oth, rows, LANES * p:LANES * (p + 1)]
            zero = jnp.zeros_like(r)
            rhs = jnp.concatenate([jnp.where(low_head, r, zero), jnp.where(low_head, zero, r)], axis=0)
            mixed.append(jnp.dot(wmix[p], rhs, preferred_element_type=F32))
        for run in end_fill[j]:
            run()
        ya = [outs[hd // GQA][(hd % GQA) * WINDOW:(hd % GQA + 1) * WINDOW] / dens[hd] for hd in range(N_HEADS)]
        cat_s[oth, rows, 0:Q_W] = _rms(jnp.concatenate(ya, axis=1), ga_ref[...]).astype(BF16)
        yg = u_s[oth, rows, :] * (jnp.concatenate(mixed, axis=1) + bias_ref[...])
        cat_s[oth, rows, Q_W:] = _rms(yg, gg_ref[...]).astype(BF16)


def _mix_prompt(x, gmix, w_in, sinks, gvg, wcat, bias_full, ga, gg, w_out):
    b, s, _ = x.shape
    tm = TOKEN_TILE
    assert s % tm == 0 and tm % WINDOW == 0
    tiles_per_seq = s // tm
    n_tiles = b * tiles_per_seq
    cos_off, sin_off = _rope_tables(jnp.arange(tm, dtype=jnp.int32))
    cos_base, sin_base = _rope_tables(tm * jnp.arange(tiles_per_seq, dtype=jnp.int32))
    proj_tile = lambda i: jnp.minimum(i, n_tiles - 1)
    out_tile = lambda i: jnp.maximum(i - 2, 0)
    x_tiles = x.reshape(n_tiles, tm, D_MODEL)
    last = lambda width: pl.BlockSpec((1, width, WINDOW), lambda i: (proj_tile(i) // tiles_per_seq, 0, 0))
    out, ko, vo, gvo = pl.pallas_call(
        functools.partial(_mix_prompt_kernel, tm=tm, tiles_per_seq=tiles_per_seq, n_tiles=n_tiles),
        grid=(n_tiles + 2,),
        in_specs=[pl.BlockSpec((1, tm, D_MODEL), lambda i: (proj_tile(i), 0, 0)),
                  pl.BlockSpec((1, tm, D_MODEL), lambda i: (out_tile(i), 0, 0)),
                  _resident((tm, LANES)), _resident((tm, LANES)), _resident((tiles_per_seq, LANES)),
                  _resident((tiles_per_seq, LANES)), _resident((1, D_MODEL)), _resident((D_MODEL, D_IN)),
                  pl.BlockSpec(memory_space=pltpu.SMEM), _resident((1, D_GMLP)),
                  _resident((G_HEADS, CHUNK, CHUNK)), _resident((CHUNK, D_GMLP)), _resident((1, Q_W)),
                  _resident((1, D_GMLP)), _resident((D_MODEL, D_MODEL))],
        out_specs=[pl.BlockSpec((1, tm, D_MODEL), lambda i: (out_tile(i), 0, 0)), last(KV_W), last(KV_W),
                   last(D_GMLP)],
        out_shape=[jax.ShapeDtypeStruct((n_tiles, tm, D_MODEL), F32), jax.ShapeDtypeStruct((b, KV_W, WINDOW), F32),
                   jax.ShapeDtypeStruct((b, KV_W, WINDOW), F32), jax.ShapeDtypeStruct((b, D_GMLP, CHUNK), F32)],
        scratch_shapes=[pltpu.VMEM((2, tm, Q_W), BF16), pltpu.VMEM((2, tm + WINDOW, KV_W), BF16),
                        pltpu.VMEM((2, tm + WINDOW, KV_W), BF16), pltpu.VMEM((2, tm, D_GMLP), F32),
                        pltpu.VMEM((2, tm, D_GMLP), BF16), pltpu.VMEM((2, tm, D_MODEL), BF16)],
        compiler_params=pltpu.CompilerParams(dimension_semantics=("arbitrary",), vmem_limit_bytes=VMEM_LIMIT_BYTES),
        name="mix_prompt",
    )(x_tiles, x_tiles, cos_off, sin_off, cos_base, sin_base, gmix, w_in, sinks, gvg, wcat, bias_full, ga, gg, w_out)
    return out.reshape(b, s, D_MODEL), ko, vo, gvo


def _mix_sample_kernel(x_ref, ckt_ref, cvt_ref, cos_ref, sin_ref, gmix_ref, win_ref, sinks_ref, gvg_ref, coef_ref,
                       biasr_ref, ga_ref, gg_ref, wout_ref, o_ref, kot_ref, vot_ref, gvo_ref, q_s, k_s, v_s, ya_s,
                       yg_s, *, t_new, w_buf):
    step = pl.program_id(0)
    n_tok = x_ref.shape[0]
    step_seqs = SEQ_GROUP * GROUPS_PER_STEP
    step_rows = step_seqs * t_new
    grp_rows = SEQ_GROUP * t_new
    grp_keys = SEQ_GROUP * w_buf
    sub = 8

    @pl.when(step == 0)
    def _():
        tile = lambda tab: jnp.broadcast_to(tab[None], (n_tok // sub, sub, LANES)).reshape(n_tok, LANES)
        q, k, v, u, gvn = _in_proj(x_ref[...], gmix_ref[...], win_ref[...], tile(cos_ref[...]),
                                   tile(_sign_sin(sin_ref[...])), gvg_ref[...])
        q_s[...] = q
        k_s[...] = k
        v_s[...] = v
        gvo_ref[...] = gvn
        g3 = gvn.reshape(n_tok // sub, sub, D_GMLP)
        trow = lax.broadcasted_iota(jnp.int32, (1, sub, D_GMLP), 1) & (t_new - 1)
        mixed = biasr_ref[...][None] + coef_ref[0][None] * g3
        for d in range(1, t_new):
            shifted = jnp.where(trow >= d, pltpu.roll(g3, d, 1), 0.0)
            mixed = mixed + coef_ref[d][None] * shifted
        yg_s[...] = (u.reshape(n_tok // sub, sub, D_GMLP) * mixed).reshape(n_tok, D_GMLP)

    row0 = pl.multiple_of(step * step_rows, step_rows)
    q_step = q_s[pl.ds(row0, step_rows), :]
    kn = k_s[pl.ds(row0, step_rows), :]
    vn = v_s[pl.ds(row0, step_rows), :]

    knt = kn.T
    vnt = vn.T
    tail = lax.broadcasted_iota(jnp.int32, (KV_W, w_buf), 1) >= w_buf - t_new
    for b in range(step_seqs):
        shift = (w_buf - t_new - t_new * b) % w_buf
        kot_ref[b] = jnp.where(tail, pltpu.roll(knt, shift, 1) if shift else knt,
                               pltpu.roll(ckt_ref[b], w_buf - t_new, 1))
        vot_ref[b] = jnp.where(tail, pltpu.roll(vnt, shift, 1) if shift else vnt,
                               pltpu.roll(cvt_ref[b], w_buf - t_new, 1))

    knb = kn.astype(BF16)
    vnb = vn.astype(BF16)

    n_rows = GQA * grp_rows
    shift_t = t_new.bit_length() - 1
    shift_w = w_buf.bit_length() - 1
    r = lax.broadcasted_iota(jnp.int32, (n_rows, grp_keys), 0)
    c = lax.broadcasted_iota(jnp.int32, (n_rows, grp_keys), 1)
    mask_c = ((c >> shift_w) == ((r & (grp_rows - 1)) >> shift_t)) & ((c & (w_buf - 1)) > (r & (t_new - 1)))
    r2 = lax.broadcasted_iota(jnp.int32, (n_rows, grp_rows), 0)
    c2 = lax.broadcasted_iota(jnp.int32, (n_rows, grp_rows), 1)
    mask_n = ((c2 >> shift_t) == ((r2 & (grp_rows - 1)) >> shift_t)) & ((c2 & (t_new - 1)) <= (r2 & (t_new - 1)))
    row_head = lax.broadcasted_iota(jnp.int32, (n_rows, 1), 0) >> (grp_rows.bit_length() - 1)

    for grp in range(GROUPS_PER_STEP):
        rows = slice(grp * grp_rows, (grp + 1) * grp_rows)
        seqs = range(grp * SEQ_GROUP, (grp + 1) * SEQ_GROUP)
        for kv in range(N_KV):
            heads = [kv * GQA + i for i in range(GQA)]
            lanes = slice(HEAD_DIM * kv, HEAD_DIM * (kv + 1))
            kt = jnp.concatenate([ckt_ref[b, lanes, :] for b in seqs], axis=1).astype(BF16)
            vt = jnp.concatenate([cvt_ref[b, lanes, :] for b in seqs], axis=1).astype(BF16)
            qs = jnp.concatenate([q_step[rows, HEAD_DIM * hd:HEAD_DIM * (hd + 1)] for hd in heads], axis=0)
            s_c = jnp.where(mask_c, jnp.dot(qs, kt, preferred_element_type=F32), -jnp.inf)
            s_n = jnp.where(mask_n, lax.dot_general(qs, knb[rows, lanes], _NT, preferred_element_type=F32), -jnp.inf)
            sink = jnp.full((n_rows, 1), sinks_ref[heads[0]] * LOG2E, F32)
            for i in range(1, GQA):
                sink = jnp.where(row_head == i, sinks_ref[heads[i]] * LOG2E, sink)
            m = jnp.maximum(jnp.maximum(jnp.max(s_c, axis=1, keepdims=True), jnp.max(s_n, axis=1, keepdims=True)),
                            sink)
            p_c = jnp.exp2(s_c - m)
            p_n = jnp.exp2(s_n - m)
            den = jnp.sum(p_c, axis=1, keepdims=True) + jnp.sum(p_n, axis=1, keepdims=True) + jnp.exp2(sink - m)
            o = (lax.dot_general(p_c.astype(BF16), vt, _NT, preferred_element_type=F32)
                 + jnp.dot(p_n.astype(BF16), vnb[rows, lanes], preferred_element_type=F32)) / den
            for i, hd in enumerate(heads):
                ya_s[pl.ds(row0 + grp * grp_rows, grp_rows), HEAD_DIM * hd:HEAD_DIM * (hd + 1)] = (
                    o[i * grp_rows:(i + 1) * grp_rows])

    @pl.when(step == pl.num_programs(0) - 1)
    def _():
        o_ref[...] = _out_proj(x_ref[...], ya_s[...], yg_s[...], ga_ref[...], gg_ref[...], wout_ref[...])


def _mix_sample(x, cache_kt, cache_vt, cos, sin, gmix, w_in, sinks, gvg, coef, bias_rows, ga, gg, w_out, t_new):
    n_tok = x.shape[0]
    n_seq, _, w_buf = cache_kt.shape
    step_seqs = SEQ_GROUP * GROUPS_PER_STEP
    assert n_seq % step_seqs == 0 and n_tok == n_seq * t_new
    assert t_new & (t_new - 1) == 0 and w_buf & (w_buf - 1) == 0 and 8 % t_new == 0
    assert step_seqs * t_new == w_buf == LANES
    cache_spec = pl.BlockSpec((step_seqs, KV_W, w_buf), lambda i: (i, 0, 0))
    return pl.pallas_call(
        functools.partial(_mix_sample_kernel, t_new=t_new, w_buf=w_buf),
        grid=(n_seq // step_seqs,),
        in_specs=[_resident((n_tok, D_MODEL)), cache_spec, cache_spec, _resident((8, LANES)),
                  _resident((8, LANES)), _resident((1, D_MODEL)), _resident((D_MODEL, D_IN)),
                  pl.BlockSpec(memory_space=pltpu.SMEM), _resident((1, D_GMLP)), _resident((t_new, 8, D_GMLP)),
                  _resident((8, D_GMLP)), _resident((1, Q_W)), _resident((1, D_GMLP)),
                  _resident((D_MODEL, D_MODEL))],
        out_specs=[pl.BlockSpec((n_tok, D_MODEL), lambda i: (0, 0)), cache_spec, cache_spec,
                   pl.BlockSpec((n_tok, D_GMLP), lambda i: (0, 0))],
        out_shape=[jax.ShapeDtypeStruct((n_tok, D_MODEL), F32), jax.ShapeDtypeStruct(cache_kt.shape, F32),
                   jax.ShapeDtypeStruct(cache_vt.shape, F32), jax.ShapeDtypeStruct((n_tok, D_GMLP), F32)],
        scratch_shapes=[pltpu.VMEM((n_tok, Q_W), BF16), pltpu.VMEM((n_tok, KV_W), F32), pltpu.VMEM((n_tok, KV_W), F32),
                        pltpu.VMEM((n_tok, Q_W), F32), pltpu.VMEM((n_tok, D_GMLP), F32)],
        compiler_params=pltpu.CompilerParams(dimension_semantics=("arbitrary",), vmem_limit_bytes=VMEM_LIMIT_BYTES),
        name="mix_sample",
    )(x, cache_kt, cache_vt, cos, sin, gmix, w_in, sinks, gvg, coef, bias_rows, ga, gg, w_out)


def _gmlp_tables_kernel(ws_ref, bs_ref, bias_ref, coef_ref, biasr_ref, *, t_new):
    sub = 8
    expand = lambda col: jnp.broadcast_to(col, (col.shape[0], D_GMLP // G_HEADS))
    row = lax.broadcasted_iota(jnp.int32, (CHUNK, CHUNK), 0)
    lane = lax.broadcasted_iota(jnp.int32, (CHUNK, CHUNK), 1)
    bias = jnp.concatenate(
        [expand(jnp.sum(jnp.where(lane == row, bs_ref[hd:hd + 1, :], 0.0), axis=1, keepdims=True))
         for hd in range(G_HEADS)], axis=1)
    bias_ref[...] = bias
    biasr_ref[...] = jnp.concatenate([bias[0:t_new]] * (sub // t_new), axis=0)
    t = lax.broadcasted_iota(jnp.int32, (sub, CHUNK), 0) & (t_new - 1)
    s = lax.broadcasted_iota(jnp.int32, (sub, CHUNK), 1)
    for d in range(t_new):
        cols = []
        for hd in range(G_HEADS):
            w_rows = jnp.concatenate([ws_ref[hd, 0:t_new, :]] * (sub // t_new), axis=0)
            cols.append(expand(jnp.sum(jnp.where(s == t - d, w_rows, 0.0), axis=1, keepdims=True)))
        coef_ref[d] = jnp.concatenate(cols, axis=1)


def _gmlp_tables(ws, bs, t_new):
    assert 8 % t_new == 0
    return pl.pallas_call(
        functools.partial(_gmlp_tables_kernel, t_new=t_new),
        out_shape=[jax.ShapeDtypeStruct((CHUNK, D_GMLP), F32), jax.ShapeDtypeStruct((t_new, 8, D_GMLP), F32),
                   jax.ShapeDtypeStruct((8, D_GMLP), F32)],
        name="gmlp_tables",
    )(ws, bs)


def _rope_tables(pos):
    inv_freq = ROPE_THETA ** (-jnp.arange(0, HEAD_DIM, 2, dtype=F32) / HEAD_DIM)
    ang = pos.astype(F32)[:, None] * jnp.tile(inv_freq, 2 * LANES // HEAD_DIM)[None, :]
    return jnp.cos(ang), jnp.sin(ang)


def kernel(x_prompt, x_sample, cache_k_win, cache_v_win, norm_ffn1, ffn1_gate, ffn1_up, ffn1_down, norm_mix, w_in,
           attn_sinks, gmlp_v_norm, gmlp_w_s, gmlp_b_s, norm_attn_out, norm_gmlp_out, w_out, norm_ffn2, ffn2_gate,
           ffn2_up, ffn2_down, norm_final):
    depth = norm_ffn1.shape[0]
    b, s, _ = x_prompt.shape
    bd, t_new, _ = x_sample.shape
    w_buf = cache_k_win.shape[2]

    cos_s, sin_s = _rope_tables(PAST_LEN + jnp.arange(t_new, dtype=jnp.int32))
    cos_s, sin_s = jnp.tile(cos_s, (8 // t_new, 1)), jnp.tile(sin_s, (8 // t_new, 1))

    hp = x_prompt.reshape(b * s, D_MODEL)
    hs = x_sample.reshape(bd * t_new, D_MODEL)
    outs = [[] for _ in range(6)]
    ffn1_w = [ffn1_gate[0], ffn1_up[0], ffn1_down[0]]
    for l in range(depth):
        last = l == depth - 1
        row = lambda a: a[l].reshape(1, -1)

        wcat = gmlp_w_s[l]
        bias_full, coef, bias_rows = _gmlp_tables(gmlp_w_s[l], gmlp_b_s[l], t_new)

        hp, hs, (w_in_b, w_out_b, wg2, wu2, wd2) = _ffn_half(
            hp, hs, norm_ffn1[l], *ffn1_w, cast=(w_in[l], w_out[l], ffn2_gate[l], ffn2_up[l], ffn2_down[l]))

        hp, kpt, vpt, gvpt = _mix_prompt(hp.reshape(b, s, D_MODEL), row(norm_mix), w_in_b, attn_sinks[l],
                                         row(gmlp_v_norm), wcat, bias_full, row(norm_attn_out), row(norm_gmlp_out),
                                         w_out_b)
        to_t = lambda c: c.transpose(0, 2, 3, 1).reshape(bd, KV_W, w_buf)
        hs, kst, vst, gvs = _mix_sample(hs, to_t(cache_k_win[l]), to_t(cache_v_win[l]), cos_s, sin_s, row(norm_mix),
                                        w_in_b, attn_sinks[l], row(gmlp_v_norm), coef, bias_rows, row(norm_attn_out),
                                        row(norm_gmlp_out), w_out_b, t_new)
        from_t = lambda c: c.reshape(bd, N_KV, HEAD_DIM, w_buf).transpose(0, 3, 1, 2)

        next_ffn1 = () if last else (ffn1_gate[l + 1], ffn1_up[l + 1], ffn1_down[l + 1])
        hp, hs, ffn1_w = _ffn_half(hp.reshape(b * s, D_MODEL), hs, norm_ffn2[l], wg2, wu2, wd2,
                                   gf=norm_final if last else None, cast=next_ffn1)

        outs[0].append(kpt.reshape(b, N_KV, HEAD_DIM, WINDOW).transpose(0, 3, 1, 2))
        outs[1].append(vpt.reshape(b, N_KV, HEAD_DIM, WINDOW).transpose(0, 3, 1, 2))
        outs[2].append(from_t(kst))
        outs[3].append(from_t(vst))
        outs[4].append(gvpt.reshape(b, G_HEADS, D_GMLP // G_HEADS, CHUNK).transpose(0, 3, 1, 2))
        outs[5].append(gvs.reshape(bd, t_new, G_HEADS, D_GMLP // G_HEADS))

    return (hp.reshape(b, s, D_MODEL), hs.reshape(bd, t_new, D_MODEL)) + tuple(jnp.stack(o) for o in outs)
```

```python
import functools

import jax
import jax.numpy as jnp
from jax import lax
from jax.experimental import pallas as pl
from jax.experimental.pallas import tpu as pltpu

F32 = jnp.float32
BF16 = jnp.bfloat16

D_MODEL = 1024
D_FF = 2816
HEAD_DIM = 64
N_HEADS = 8
N_KV = 2
GQA = N_HEADS // N_KV
WINDOW = 128
CHUNK = 128
G_HEADS = 8
Q_W = N_HEADS * HEAD_DIM
KV_W = N_KV * HEAD_DIM
D_GMLP = 512
D_IN = Q_W + 2 * KV_W + 2 * D_GMLP
K_OFF = Q_W
V_OFF = K_OFF + KV_W
U_OFF = V_OFF + KV_W
GV_OFF = U_OFF + D_GMLP
ROPE_THETA = 10000.0
PAST_LEN = 16384
EPS = 1e-6
LOG2E = 1.4426950408889634
Q_SCALE = HEAD_DIM ** -0.5 * LOG2E
LANES = 128
BF16_SUBLANES = 16

TOKEN_TILE = 512
FF_CHUNK = 256
STAGE_BLOCK_BYTES = 64 * D_FF * 4
STAGE_SLOTS = 8
SEQ_GROUP = 8
GROUPS_PER_STEP = 4
VMEM_LIMIT_BYTES = 56 * 1024 * 1024

_NT = (((1,), (1,)), ((), ()))


def _rms(x, g):
    ms = jnp.mean(x * x, axis=-1, keepdims=True)
    return (x * lax.rsqrt(ms + EPS)) * g


def _resident(shape):
    zeros = (0,) * len(shape)
    return pl.BlockSpec(shape, lambda *_: zeros, pipeline_mode=pl.Buffered(1))


def _round_rows_to_bf16(src_hbm, dst_ref, stage_ref, sem):
    n_slots, rows, _ = stage_ref.shape
    assert src_hbm.shape[0] % rows == 0
    n_blocks = src_hbm.shape[0] // rows
    assert n_blocks >= n_slots

    def fetch(i, slot):
        return pltpu.make_async_copy(src_hbm.at[pl.ds(i * rows, rows), :], stage_ref.at[slot], sem.at[slot])

    for i in range(n_slots - 1):
        fetch(i, i).start()

    def body(i, carry):
        slot = i % n_slots

        @pl.when(i + n_slots - 1 < n_blocks)
        def _():
            fetch(i + n_slots - 1, (i + n_slots - 1) % n_slots).start()

        fetch(i, slot).wait()
        dst_ref[pl.ds(pl.multiple_of(i * rows, rows), rows), :] = stage_ref[slot].astype(BF16)
        return carry

    lax.fori_loop(0, n_blocks, body, 0)


def _ffn_kernel(*refs, final_norm, n_cast, f32_weights):
    refs = iter(refs)
    xp_ref, xn_ref, xs_ref, g_ref, wg_in, wu_in, wd_in = (next(refs) for _ in range(7))
    gf_ref = next(refs) if final_norm else None
    cast_in = [next(refs) for _ in range(n_cast)]
    yp_ref, ys_ref = next(refs), next(refs)
    cast_out = [next(refs) for _ in range(n_cast)]
    h_ref, act0_ref, act_ref = next(refs), next(refs), next(refs)
    if f32_weights:
        wg_ref, wu_ref, wd_ref, wide_stage, narrow_stage, wide_sem, narrow_sem = (next(refs) for _ in range(7))
    else:
        wg_ref, wu_ref, wd_ref = wg_in, wu_in, wd_in

    def gate_up(h, c):
        sl = slice(c * FF_CHUNK, (c + 1) * FF_CHUNK)
        gate = jnp.dot(h, wg_ref[:, sl], preferred_element_type=F32)
        up = jnp.dot(h, wu_ref[:, sl], preferred_element_type=F32)
        return (gate * jax.nn.sigmoid(gate) * up).astype(BF16)

    on_sample = pl.program_id(0) == 0

    @pl.when(on_sample)
    def _():
        if f32_weights:
            _round_rows_to_bf16(wg_in, wg_ref, wide_stage, wide_sem)
            _round_rows_to_bf16(wu_in, wu_ref, wide_stage, wide_sem)
            _round_rows_to_bf16(wd_in, wd_ref, narrow_stage, narrow_sem)
        h0 = _rms(xs_ref[...], g_ref[...]).astype(BF16)
        h_ref[...] = h0
        act0_ref[...] = gate_up(h0, 0)

    for c in range(1, D_FF // FF_CHUNK):
        act_ref[:, (c - 1) * FF_CHUNK:c * FF_CHUNK] = gate_up(h_ref[...], c)
    down0 = jnp.dot(act0_ref[...], wd_ref[0:FF_CHUNK, :], preferred_element_type=F32)
    hn = _rms(xn_ref[...], g_ref[...]).astype(BF16)
    h_ref[...] = hn
    act0_ref[...] = gate_up(hn, 0)
    if final_norm:
        x = jnp.where(on_sample, xs_ref[...], xp_ref[...])
        y = x + 0.5 * (down0 + jnp.dot(act_ref[...], wd_ref[FF_CHUNK:, :], preferred_element_type=F32))
        yp_ref[...] = _rms(y, gf_ref[...])
    else:
        for cols in (slice(0, D_MODEL // 2), slice(D_MODEL // 2, D_MODEL)):
            x = jnp.where(on_sample, xs_ref[:, cols], xp_ref[:, cols])
            yp_ref[:, cols] = x + 0.5 * (down0[:, cols] + jnp.dot(act_ref[...], wd_ref[FF_CHUNK:, cols],
                                                                   preferred_element_type=F32))

    @pl.when(on_sample)
    def _():
        ys_ref[...] = yp_ref[...].reshape(ys_ref.shape)

    for src, dst in zip(cast_in, cast_out):
        dst[...] = src[...].astype(BF16)


def _cast_row_blocks(rows, n_steps):
    return max(d for d in range(1, n_steps + 1) if rows % d == 0 and (rows // d) % BF16_SUBLANES == 0)


def _ffn_half(xp, xs, g, wg, wu, wd, gf=None, cast=()):
    tm = TOKEN_TILE
    f32_weights = wg.dtype == F32
    assert all(w.dtype == wg.dtype for w in (wu, wd))
    n = xp.shape[0]
    assert n % tm == 0 and xs.shape[0] == tm
    n_tiles = n // tm
    prompt_spec = pl.BlockSpec((tm, D_MODEL), lambda i: (jnp.maximum(i - 1, 0), 0))
    next_spec = pl.BlockSpec((tm, D_MODEL), lambda i: (jnp.minimum(i, n_tiles - 1), 0))
    sample_in = _resident((tm, D_MODEL))
    sample_shape = (tm // 4, 4, D_MODEL) if gf is not None else (tm, D_MODEL)
    sample_out = pl.BlockSpec(sample_shape, lambda i: (0,) * len(sample_shape))
    weight_specs = ([pl.BlockSpec(memory_space=pl.ANY)] * 3 if f32_weights else
                    [_resident((D_MODEL, D_FF)), _resident((D_MODEL, D_FF)), _resident((D_FF, D_MODEL))])
    in_specs = [prompt_spec, next_spec, sample_in, _resident((1, D_MODEL))] + weight_specs
    args = [xp, xp, xs, g.reshape(1, D_MODEL), wg, wu, wd]
    if gf is not None:
        in_specs.append(_resident((1, D_MODEL)))
        args.append(gf.reshape(1, D_MODEL))
    out_specs = [prompt_spec, sample_out]
    out_shape = [jax.ShapeDtypeStruct((n, D_MODEL), F32), jax.ShapeDtypeStruct(sample_shape, F32)]
    cast_specs = []
    for w in cast:
        rows, cols = w.shape
        nb = _cast_row_blocks(rows, n_tiles)
        cast_specs.append(pl.BlockSpec((rows // nb, cols), lambda i, nb=nb: (jnp.minimum(i, nb - 1), 0)))
        out_shape.append(jax.ShapeDtypeStruct(w.shape, BF16))
    outs = pl.pallas_call(
        functools.partial(_ffn_kernel, final_norm=gf is not None, n_cast=len(cast), f32_weights=f32_weights),
        grid=(n_tiles + 1,),
        in_specs=in_specs + cast_specs,
        out_specs=out_specs + cast_specs,
        out_shape=out_shape,
        scratch_shapes=[pltpu.VMEM((tm, D_MODEL), BF16), pltpu.VMEM((tm, FF_CHUNK), BF16),
                        pltpu.VMEM((tm, D_FF - FF_CHUNK), BF16)] + ([
                            pltpu.VMEM((D_MODEL, D_FF), BF16), pltpu.VMEM((D_MODEL, D_FF), BF16),
                            pltpu.VMEM((D_FF, D_MODEL), BF16),
                            pltpu.VMEM((STAGE_SLOTS, STAGE_BLOCK_BYTES // (4 * D_FF), D_FF), F32),
                            pltpu.VMEM((STAGE_SLOTS, STAGE_BLOCK_BYTES // (4 * D_MODEL), D_MODEL), F32),
                            pltpu.SemaphoreType.DMA((STAGE_SLOTS,)), pltpu.SemaphoreType.DMA((STAGE_SLOTS,))]
                        if f32_weights else []),
        compiler_params=pltpu.CompilerParams(dimension_semantics=("arbitrary",), vmem_limit_bytes=VMEM_LIMIT_BYTES),
        name="ffn_final" if gf is not None else "ffn_half",
    )(*args, *cast)
    return outs[0], outs[1], list(outs[2:])


def _first_half(shape):
    return (lax.broadcasted_iota(jnp.int32, shape, 1) & (HEAD_DIM - 1)) < HEAD_DIM // 2


def _sign_sin(sin):
    return jnp.where(_first_half(sin.shape), -sin, sin)


def _rope(xg, cos, sin_signed):
    swapped = jnp.where(_first_half(xg.shape), pltpu.roll(xg, LANES - HEAD_DIM // 2, 1),
                        pltpu.roll(xg, HEAD_DIM // 2, 1))
    return xg * cos + swapped * sin_signed


def _in_proj(x, gmix, w_in, cos, sin_signed, gvn_gain):
    h = _rms(x, gmix).astype(BF16)
    z = jnp.dot(h, w_in, preferred_element_type=F32)
    q = jnp.concatenate(
        [(_rope(z[:, LANES * i:LANES * (i + 1)], cos, sin_signed) * Q_SCALE).astype(BF16) for i in range(Q_W // LANES)],
        axis=1)
    k = _rope(z[:, K_OFF:V_OFF], cos, sin_signed)
    v = z[:, V_OFF:U_OFF]
    u = _gelu(z[:, U_OFF:GV_OFF])
    gvn = _rms(_gelu(z[:, GV_OFF:]), gvn_gain)
    return q, k, v, u, gvn


def _out_proj(x, ya, yg, ga, gg, w_out):
    cat = jnp.concatenate([_rms(ya, ga).astype(BF16), _rms(yg, gg).astype(BF16)], axis=1)
    return x + jnp.dot(cat, w_out, preferred_element_type=F32)


def _softmax_sink(s, sink):
    m = jnp.maximum(jnp.max(s, axis=1, keepdims=True), sink)
    p = jnp.exp2(s - m)
    return p, jnp.sum(p, axis=1, keepdims=True) + jnp.exp2(sink - m)


def _gelu(x):
    k = -2.0 * (2.0 / jnp.pi) ** 0.5 * LOG2E
    return x / (1.0 + jnp.exp2(x * (k + (0.044715 * k) * (x * x))))


def _mix_prompt_kernel(xa_ref, xc_ref, coff_ref, soff_ref, cbase_ref, sbase_ref, gmix_ref, win_ref, sinks_ref, gvg_ref,
                       wcat_ref, bias_ref, ga_ref, gg_ref, wout_ref, o_ref, ko_ref, vo_ref, gvo_ref, q_s, k_s, v_s, u_s,
                       gv_s, cat_s, *, tm, tiles_per_seq, n_tiles):
    s = pl.program_id(0)

    @pl.when(s == 0)
    def _():
        q_s[1] = jnp.zeros((tm, Q_W), BF16)
        k_s[1] = jnp.zeros((tm + WINDOW, KV_W), BF16)
        v_s[1] = jnp.zeros((tm + WINDOW, KV_W), BF16)
        u_s[1] = jnp.zeros((tm, D_GMLP), F32)
        gv_s[1] = jnp.zeros((tm, D_GMLP), BF16)
        cat_s[0] = jnp.zeros((tm, D_MODEL), BF16)

    for parity in range(2):
        pl.when(s % 2 == parity)(functools.partial(
            _mix_prompt_step, s, parity, xa_ref, xc_ref, coff_ref, soff_ref, cbase_ref, sbase_ref, gmix_ref, win_ref,
            sinks_ref, gvg_ref, wcat_ref, bias_ref, ga_ref, gg_ref, wout_ref, o_ref, ko_ref, vo_ref, gvo_ref, q_s, k_s,
            v_s, u_s, gv_s, cat_s, tm=tm, tiles_per_seq=tiles_per_seq, n_tiles=n_tiles))


def _mix_prompt_step(s, cur, xa_ref, xc_ref, coff_ref, soff_ref, cbase_ref, sbase_ref, gmix_ref, win_ref, sinks_ref,
                     gvg_ref, wcat_ref, bias_ref, ga_ref, gg_ref, wout_ref, o_ref, ko_ref, vo_ref, gvo_ref, q_s, k_s, v_s,
                     u_s, gv_s, cat_s, *, tm, tiles_per_seq, n_tiles):
    oth = 1 - cur

    h = _rms(xa_ref[0], gmix_ref[...]).astype(BF16)
    tile_in_seq = jnp.minimum(s, n_tiles - 1) % tiles_per_seq
    cb = cbase_ref[pl.ds(tile_in_seq, 1), :]
    sb = sbase_ref[pl.ds(tile_in_seq, 1), :]
    cos = cb * coff_ref[...] - sb * soff_ref[...]
    sin = _sign_sin(sb * coff_ref[...] + cb * soff_ref[...])

    def proj_q():
        z = jnp.dot(h, win_ref[:, 0:Q_W], preferred_element_type=F32)
        for i in range(Q_W // LANES):
            q_s[cur, :, LANES * i:LANES * (i + 1)] = (
                _rope(z[:, LANES * i:LANES * (i + 1)], cos, sin) * Q_SCALE).astype(BF16)

    def proj_kv():
        z = jnp.dot(h, win_ref[:, K_OFF:U_OFF], preferred_element_type=F32)
        k = _rope(z[:, 0:KV_W], cos, sin)
        v = z[:, KV_W:]
        k_s[cur, 0:WINDOW, :] = k_s[oth, tm:tm + WINDOW, :]
        v_s[cur, 0:WINDOW, :] = v_s[oth, tm:tm + WINDOW, :]
        k_s[cur, WINDOW:, :] = k.astype(BF16)
        v_s[cur, WINDOW:, :] = v.astype(BF16)
        ko_ref[0] = k[tm - WINDOW:].T
        vo_ref[0] = v[tm - WINDOW:].T

    def proj_u():
        u_s[cur] = _gelu(jnp.dot(h, win_ref[:, U_OFF:GV_OFF], preferred_element_type=F32))

    def proj_gv():
        gvn = _rms(_gelu(jnp.dot(h, win_ref[:, GV_OFF:], preferred_element_type=F32)), gvg_ref[...])
        gv_s[cur] = gvn.astype(BF16)
        gvo_ref[0] = gvn[tm - CHUNK:].T

    def out_half(c):
        cols = slice(c * (D_MODEL // 2), (c + 1) * (D_MODEL // 2))

        def run():
            o_ref[0, :, cols] = xc_ref[0, :, cols] + jnp.dot(cat_s[cur], wout_ref[:, cols], preferred_element_type=F32)
        return run

    mid_fill = [[out_half(0)], [proj_q], [proj_kv, proj_u], [proj_gv]]
    end_fill = [[], [], [], [out_half(1)]]
    assert len(mid_fill) == len(end_fill) == tm // WINDOW

    wrow = lax.broadcasted_iota(jnp.int32, (CHUNK, CHUNK), 0)
    wcol = lax.broadcasted_iota(jnp.int32, (CHUNK, CHUNK), 1)
    wtril = [jnp.where(wcol <= wrow, wcat_ref[hd], 0.0).astype(BF16) for hd in range(G_HEADS)]
    wmix = [jnp.concatenate(wtril[2 * p:2 * p + 2], axis=1) for p in range(G_HEADS // 2)]
    lane = lax.broadcasted_iota(jnp.int32, (CHUNK, LANES), 1)
    low_head = lane < HEAD_DIM

    qi = lax.broadcasted_iota(jnp.int32, (WINDOW, 2 * WINDOW), 0)
    sj = lax.broadcasted_iota(jnp.int32, (WINDOW, 2 * WINDOW), 1)
    dist = WINDOW + qi - sj
    band = (dist >= 0) & (dist < WINDOW)
    first_lo = jnp.where((s + tiles_per_seq - 1) % tiles_per_seq == 0, WINDOW, 0)

    for j in range(tm // WINDOW):
        rows = slice(j * WINDOW, (j + 1) * WINDOW)
        mask = band & (sj >= first_lo) if j == 0 else band
        qb = q_s[oth, rows, :]
        scores = []
        for kv in range(N_KV):
            qs = jnp.concatenate(
                [qb[:, HEAD_DIM * hd:HEAD_DIM * (hd + 1)] for hd in range(kv * GQA, (kv + 1) * GQA)], axis=0)
            kb = k_s[oth, j * WINDOW:(j + 2) * WINDOW, HEAD_DIM * kv:HEAD_DIM * (kv + 1)]
            scores.append(lax.dot_general(qs, kb, _NT, preferred_element_type=F32))
        for run in mid_fill[j]:
            run()
        outs, dens = [], []
        for kv in range(N_KV):
            vb = v_s[oth, j * WINDOW:(j + 2) * WINDOW, HEAD_DIM * kv:HEAD_DIM * (kv + 1)]
            ps = []
            for g in range(GQA):
                sg = jnp.where(mask, scores[kv][g * WINDOW:(g + 1) * WINDOW], -jnp.inf)
                p, den = _softmax_sink(sg, sinks_ref[kv * GQA + g] * LOG2E)
                ps.append(p.astype(BF16))
                dens.append(den)
            outs.append(jnp.dot(jnp.concatenate(ps, axis=0), vb, preferred_element_type=F32))
        mixed = []
        for p in range(G_HEADS // 2):
            r = gv_s[oth, rows, LANES * p:LANES * (p + 1)]
            zero = jnp.zeros_like(r)
            rhs = jnp.concatenate([jnp.where(low_head, r, zero), jnp.where(low_head, zero, r)], axis=0)
            mixed.append(jnp.dot(wmix[p], rhs, preferred_element_type=F32))
        for run in end_fill[j]:
            run()
        ya = [outs[hd // GQA][(hd % GQA) * WINDOW:(hd % GQA + 1) * WINDOW] / dens[hd] for hd in range(N_HEADS)]
        cat_s[oth, rows, 0:Q_W] = _rms(jnp.concatenate(ya, axis=1), ga_ref[...]).astype(BF16)
        yg = u_s[oth, rows, :] * (jnp.concatenate(mixed, axis=1) + bias_ref[...])
        cat_s[oth, rows, Q_W:] = _rms(yg, gg_ref[...]).astype(BF16)


def _mix_prompt(x, gmix, w_in, sinks, gvg, wcat, bias_full, ga, gg, w_out):
    b, s, _ = x.shape
    tm = TOKEN_TILE
    assert s % tm == 0 and tm % WINDOW == 0
    tiles_per_seq = s // tm
    n_tiles = b * tiles_per_seq
    cos_off, sin_off = _rope_tables(jnp.arange(tm, dtype=jnp.int32))
    cos_base, sin_base = _rope_tables(tm * jnp.arange(tiles_per_seq, dtype=jnp.int32))
    proj_tile = lambda i: jnp.minimum(i, n_tiles - 1)
    out_tile = lambda i: jnp.maximum(i - 2, 0)
    x_tiles = x.reshape(n_tiles, tm, D_MODEL)
    last = lambda width: pl.BlockSpec((1, width, WINDOW), lambda i: (proj_tile(i) // tiles_per_seq, 0, 0))
    out, ko, vo, gvo = pl.pallas_call(
        functools.partial(_mix_prompt_kernel, tm=tm, tiles_per_seq=tiles_per_seq, n_tiles=n_tiles),
        grid=(n_tiles + 2,),
        in_specs=[pl.BlockSpec((1, tm, D_MODEL), lambda i: (proj_tile(i), 0, 0)),
                  pl.BlockSpec((1, tm, D_MODEL), lambda i: (out_tile(i), 0, 0)),
                  _resident((tm, LANES)), _resident((tm, LANES)), _resident((tiles_per_seq, LANES)),
                  _resident((tiles_per_seq, LANES)), _resident((1, D_MODEL)), _resident((D_MODEL, D_IN)),
                  pl.BlockSpec(memory_space=pltpu.SMEM), _resident((1, D_GMLP)),
                  _resident((G_HEADS, CHUNK, CHUNK)), _resident((CHUNK, D_GMLP)), _resident((1, Q_W)),
                  _resident((1, D_GMLP)), _resident((D_MODEL, D_MODEL))],
        out_specs=[pl.BlockSpec((1, tm, D_MODEL), lambda i: (out_tile(i), 0, 0)), last(KV_W), last(KV_W),
                   last(D_GMLP)],
        out_shape=[jax.ShapeDtypeStruct((n_tiles, tm, D_MODEL), F32), jax.ShapeDtypeStruct((b, KV_W, WINDOW), F32),
                   jax.ShapeDtypeStruct((b, KV_W, WINDOW), F32), jax.ShapeDtypeStruct((b, D_GMLP, CHUNK), F32)],
        scratch_shapes=[pltpu.VMEM((2, tm, Q_W), BF16), pltpu.VMEM((2, tm + WINDOW, KV_W), BF16),
                        pltpu.VMEM((2, tm + WINDOW, KV_W), BF16), pltpu.VMEM((2, tm, D_GMLP), F32),
                        pltpu.VMEM((2, tm, D_GMLP), BF16), pltpu.VMEM((2, tm, D_MODEL), BF16)],
        compiler_params=pltpu.CompilerParams(dimension_semantics=("arbitrary",), vmem_limit_bytes=VMEM_LIMIT_BYTES),
        name="mix_prompt",
    )(x_tiles, x_tiles, cos_off, sin_off, cos_base, sin_base, gmix, w_in, sinks, gvg, wcat, bias_full, ga, gg, w_out)
    return out.reshape(b, s, D_MODEL), ko, vo, gvo


def _mix_sample_kernel(x_ref, ckt_ref, cvt_ref, cos_ref, sin_ref, gmix_ref, win_ref, sinks_ref, gvg_ref, coef_ref,
                       biasr_ref, ga_ref, gg_ref, wout_ref, o_ref, kot_ref, vot_ref, gvo_ref, q_s, k_s, v_s, ya_s,
                       yg_s, *, t_new, w_buf):
    step = pl.program_id(0)
    n_tok = x_ref.shape[0]
    step_seqs = SEQ_GROUP * GROUPS_PER_STEP
    step_rows = step_seqs * t_new
    grp_rows = SEQ_GROUP * t_new
    grp_keys = SEQ_GROUP * w_buf
    sub = 8

    @pl.when(step == 0)
    def _():
        tile = lambda tab: jnp.broadcast_to(tab[None], (n_tok // sub, sub, LANES)).reshape(n_tok, LANES)
        q, k, v, u, gvn = _in_proj(x_ref[...], gmix_ref[...], win_ref[...], tile(cos_ref[...]),
                                   tile(_sign_sin(sin_ref[...])), gvg_ref[...])
        q_s[...] = q
        k_s[...] = k
        v_s[...] = v
        gv_seq = gvn.reshape(n_tok // t_new, t_new, D_GMLP)
        for tok in range(t_new):
            gvo_ref[tok] = gv_seq[:, tok, :].T
        g3 = gvn.reshape(n_tok // sub, sub, D_GMLP)
        trow = lax.broadcasted_iota(jnp.int32, (1, sub, D_GMLP), 1) & (t_new - 1)
        mixed = biasr_ref[...][None] + coef_ref[0][None] * g3
        for d in range(1, t_new):
            shifted = jnp.where(trow >= d, pltpu.roll(g3, d, 1), 0.0)
            mixed = mixed + coef_ref[d][None] * shifted
        yg_s[...] = (u.reshape(n_tok // sub, sub, D_GMLP) * mixed).reshape(n_tok, D_GMLP)

    row0 = pl.multiple_of(step * step_rows, step_rows)
    q_step = q_s[pl.ds(row0, step_rows), :]
    kn = k_s[pl.ds(row0, step_rows), :]
    vn = v_s[pl.ds(row0, step_rows), :]

    knt = kn.T
    vnt = vn.T
    tail = lax.broadcasted_iota(jnp.int32, (KV_W, w_buf), 1) >= w_buf - t_new
    for b in range(step_seqs):
        shift = (w_buf - t_new - t_new * b) % w_buf
        kot_ref[b] = jnp.where(tail, pltpu.roll(knt, shift, 1) if shift else knt,
                               pltpu.roll(ckt_ref[b], w_buf - t_new, 1))
        vot_ref[b] = jnp.where(tail, pltpu.roll(vnt, shift, 1) if shift else vnt,
                               pltpu.roll(cvt_ref[b], w_buf - t_new, 1))

    knb = kn.astype(BF16)
    vnb = vn.astype(BF16)

    n_rows = GQA * grp_rows
    shift_t = t_new.bit_length() - 1
    shift_w = w_buf.bit_length() - 1
    r = lax.broadcasted_iota(jnp.int32, (n_rows, grp_keys), 0)
    c = lax.broadcasted_iota(jnp.int32, (n_rows, grp_keys), 1)
    mask_c = ((c >> shift_w) == ((r & (grp_rows - 1)) >> shift_t)) & ((c & (w_buf - 1)) > (r & (t_new - 1)))
    r2 = lax.broadcasted_iota(jnp.int32, (n_rows, grp_rows), 0)
    c2 = lax.broadcasted_iota(jnp.int32, (n_rows, grp_rows), 1)
    mask_n = ((c2 >> shift_t) == ((r2 & (grp_rows - 1)) >> shift_t)) & ((c2 & (t_new - 1)) <= (r2 & (t_new - 1)))
    row_head = lax.broadcasted_iota(jnp.int32, (n_rows, 1), 0) >> (grp_rows.bit_length() - 1)

    for grp in range(GROUPS_PER_STEP):
        rows = slice(grp * grp_rows, (grp + 1) * grp_rows)
        seqs = range(grp * SEQ_GROUP, (grp + 1) * SEQ_GROUP)
        for kv in range(N_KV):
            heads = [kv * GQA + i for i in range(GQA)]
            lanes = slice(HEAD_DIM * kv, HEAD_DIM * (kv + 1))
            kt = jnp.concatenate([ckt_ref[b, lanes, :] for b in seqs], axis=1).astype(BF16)
            vt = jnp.concatenate([cvt_ref[b, lanes, :] for b in seqs], axis=1).astype(BF16)
            qs = jnp.concatenate([q_step[rows, HEAD_DIM * hd:HEAD_DIM * (hd + 1)] for hd in heads], axis=0)
            s_c = jnp.where(mask_c, jnp.dot(qs, kt, preferred_element_type=F32), -jnp.inf)
            s_n = jnp.where(mask_n, lax.dot_general(qs, knb[rows, lanes], _NT, preferred_element_type=F32), -jnp.inf)
            sink = jnp.full((n_rows, 1), sinks_ref[heads[0]] * LOG2E, F32)
            for i in range(1, GQA):
                sink = jnp.where(row_head == i, sinks_ref[heads[i]] * LOG2E, sink)
            m = jnp.maximum(jnp.maximum(jnp.max(s_c, axis=1, keepdims=True), jnp.max(s_n, axis=1, keepdims=True)),
                            sink)
            p_c = jnp.exp2(s_c - m)
            p_n = jnp.exp2(s_n - m)
            den = jnp.sum(p_c, axis=1, keepdims=True) + jnp.sum(p_n, axis=1, keepdims=True) + jnp.exp2(sink - m)
            o = (lax.dot_general(p_c.astype(BF16), vt, _NT, preferred_element_type=F32)
                 + jnp.dot(p_n.astype(BF16), vnb[rows, lanes], preferred_element_type=F32)) / den
            for i, hd in enumerate(heads):
                ya_s[pl.ds(row0 + grp * grp_rows, grp_rows), HEAD_DIM * hd:HEAD_DIM * (hd + 1)] = (
                    o[i * grp_rows:(i + 1) * grp_rows])

    @pl.when(step == pl.num_programs(0) - 1)
    def _():
        o_ref[...] = _out_proj(x_ref[...], ya_s[...], yg_s[...], ga_ref[...], gg_ref[...], wout_ref[...])


def _mix_sample(x, cache_kt, cache_vt, cos, sin, gmix, w_in, sinks, gvg, coef, bias_rows, ga, gg, w_out, t_new):
    n_tok = x.shape[0]
    n_seq, _, w_buf = cache_kt.shape
    step_seqs = SEQ_GROUP * GROUPS_PER_STEP
    assert n_seq % step_seqs == 0 and n_tok == n_seq * t_new
    assert t_new & (t_new - 1) == 0 and w_buf & (w_buf - 1) == 0 and 8 % t_new == 0
    assert step_seqs * t_new == w_buf == LANES
    cache_spec = pl.BlockSpec((step_seqs, KV_W, w_buf), lambda i: (i, 0, 0))
    return pl.pallas_call(
        functools.partial(_mix_sample_kernel, t_new=t_new, w_buf=w_buf),
        grid=(n_seq // step_seqs,),
        in_specs=[_resident((n_tok, D_MODEL)), cache_spec, cache_spec, _resident((8, LANES)),
                  _resident((8, LANES)), _resident((1, D_MODEL)), _resident((D_MODEL, D_IN)),
                  pl.BlockSpec(memory_space=pltpu.SMEM), _resident((1, D_GMLP)), _resident((t_new, 8, D_GMLP)),
                  _resident((8, D_GMLP)), _resident((1, Q_W)), _resident((1, D_GMLP)),
                  _resident((D_MODEL, D_MODEL))],
        out_specs=[pl.BlockSpec((n_tok, D_MODEL), lambda i: (0, 0)), cache_spec, cache_spec,
                   pl.BlockSpec((t_new, D_GMLP, n_seq), lambda i: (0, 0, 0))],
        out_shape=[jax.ShapeDtypeStruct((n_tok, D_MODEL), F32), jax.ShapeDtypeStruct(cache_kt.shape, F32),
                   jax.ShapeDtypeStruct(cache_vt.shape, F32), jax.ShapeDtypeStruct((t_new, D_GMLP, n_seq), F32)],
        scratch_shapes=[pltpu.VMEM((n_tok, Q_W), BF16), pltpu.VMEM((n_tok, KV_W), F32), pltpu.VMEM((n_tok, KV_W), F32),
                        pltpu.VMEM((n_tok, Q_W), F32), pltpu.VMEM((n_tok, D_GMLP), F32)],
        compiler_params=pltpu.CompilerParams(dimension_semantics=("arbitrary",), vmem_limit_bytes=VMEM_LIMIT_BYTES),
        name="mix_sample",
    )(x, cache_kt, cache_vt, cos, sin, gmix, w_in, sinks, gvg, coef, bias_rows, ga, gg, w_out)


def _gmlp_tables_kernel(ws_ref, bs_ref, bias_ref, coef_ref, biasr_ref, *, t_new):
    sub = 8
    expand = lambda col: jnp.broadcast_to(col, (col.shape[0], D_GMLP // G_HEADS))
    row = lax.broadcasted_iota(jnp.int32, (CHUNK, CHUNK), 0)
    lane = lax.broadcasted_iota(jnp.int32, (CHUNK, CHUNK), 1)
    bias = jnp.concatenate(
        [expand(jnp.sum(jnp.where(lane == row, bs_ref[hd:hd + 1, :], 0.0), axis=1, keepdims=True))
         for hd in range(G_HEADS)], axis=1)
    bias_ref[...] = bias
    biasr_ref[...] = jnp.concatenate([bias[0:t_new]] * (sub // t_new), axis=0)
    t = lax.broadcasted_iota(jnp.int32, (sub, CHUNK), 0) & (t_new - 1)
    s = lax.broadcasted_iota(jnp.int32, (sub, CHUNK), 1)
    for d in range(t_new):
        cols = []
        for hd in range(G_HEADS):
            w_rows = jnp.concatenate([ws_ref[hd, 0:t_new, :]] * (sub // t_new), axis=0)
            cols.append(expand(jnp.sum(jnp.where(s == t - d, w_rows, 0.0), axis=1, keepdims=True)))
        coef_ref[d] = jnp.concatenate(cols, axis=1)


def _gmlp_tables(ws, bs, t_new):
    assert 8 % t_new == 0
    return pl.pallas_call(
        functools.partial(_gmlp_tables_kernel, t_new=t_new),
        out_shape=[jax.ShapeDtypeStruct((CHUNK, D_GMLP), F32), jax.ShapeDtypeStruct((t_new, 8, D_GMLP), F32),
                   jax.ShapeDtypeStruct((8, D_GMLP), F32)],
        name="gmlp_tables",
    )(ws, bs)


def _rope_tables(pos):
    inv_freq = ROPE_THETA ** (-jnp.arange(0, HEAD_DIM, 2, dtype=F32) / HEAD_DIM)
    ang = pos.astype(F32)[:, None] * jnp.tile(inv_freq, 2 * LANES // HEAD_DIM)[None, :]
    return jnp.cos(ang), jnp.sin(ang)


def kernel(x_prompt, x_sample, cache_k_win, cache_v_win, norm_ffn1, ffn1_gate, ffn1_up, ffn1_down, norm_mix, w_in,
           attn_sinks, gmlp_v_norm, gmlp_w_s, gmlp_b_s, norm_attn_out, norm_gmlp_out, w_out, norm_ffn2, ffn2_gate,
           ffn2_up, ffn2_down, norm_final):
    depth = norm_ffn1.shape[0]
    b, s, _ = x_prompt.shape
    bd, t_new, _ = x_sample.shape
    w_buf = cache_k_win.shape[2]

    cos_s, sin_s = _rope_tables(PAST_LEN + jnp.arange(t_new, dtype=jnp.int32))
    cos_s, sin_s = jnp.tile(cos_s, (8 // t_new, 1)), jnp.tile(sin_s, (8 // t_new, 1))

    hp = x_prompt.reshape(b * s, D_MODEL)
    hs = x_sample.reshape(bd * t_new, D_MODEL)
    outs = [[] for _ in range(6)]
    ffn1_w = [ffn1_gate[0], ffn1_up[0], ffn1_down[0]]
    for l in range(depth):
        last = l == depth - 1
        row = lambda a: a[l].reshape(1, -1)

        wcat = gmlp_w_s[l]
        bias_full, coef, bias_rows = _gmlp_tables(gmlp_w_s[l], gmlp_b_s[l], t_new)

        hp, hs, (w_in_b, w_out_b, wg2, wu2, wd2) = _ffn_half(
            hp, hs, norm_ffn1[l], *ffn1_w, cast=(w_in[l], w_out[l], ffn2_gate[l], ffn2_up[l], ffn2_down[l]))

        hp, kpt, vpt, gvpt = _mix_prompt(hp.reshape(b, s, D_MODEL), row(norm_mix), w_in_b, attn_sinks[l],
                                         row(gmlp_v_norm), wcat, bias_full, row(norm_attn_out), row(norm_gmlp_out),
                                         w_out_b)
        to_t = lambda c: c.transpose(0, 2, 3, 1).reshape(bd, KV_W, w_buf)
        hs, kst, vst, gvs = _mix_sample(hs, to_t(cache_k_win[l]), to_t(cache_v_win[l]), cos_s, sin_s, row(norm_mix),
                                        w_in_b, attn_sinks[l], row(gmlp_v_norm), coef, bias_rows, row(norm_attn_out),
                                        row(norm_gmlp_out), w_out_b, t_new)
        from_t = lambda c: c.reshape(bd, N_KV, HEAD_DIM, w_buf).transpose(0, 3, 1, 2)

        next_ffn1 = () if last else (ffn1_gate[l + 1], ffn1_up[l + 1], ffn1_down[l + 1])
        hp, hs, ffn1_w = _ffn_half(hp.reshape(b * s, D_MODEL), hs, norm_ffn2[l], wg2, wu2, wd2,
                                   gf=norm_final if last else None, cast=next_ffn1)

        outs[0].append(kpt.reshape(b, N_KV, HEAD_DIM, WINDOW).transpose(0, 3, 1, 2))
        outs[1].append(vpt.reshape(b, N_KV, HEAD_DIM, WINDOW).transpose(0, 3, 1, 2))
        outs[2].append(from_t(kst))
        outs[3].append(from_t(vst))
        outs[4].append(gvpt.reshape(b, G_HEADS, D_GMLP // G_HEADS, CHUNK).transpose(0, 3, 1, 2))
        outs[5].append(gvs.reshape(t_new, G_HEADS, D_GMLP // G_HEADS, bd).transpose(3, 0, 1, 2))

    return (hp.reshape(b, s, D_MODEL), hs.reshape(bd, t_new, D_MODEL)) + tuple(jnp.stack(o) for o in outs)
```

```python
import functools

import jax
import jax.numpy as jnp
from jax import lax
from jax.experimental import pallas as pl
from jax.experimental.pallas import tpu as pltpu

F32 = jnp.float32
BF16 = jnp.bfloat16

D_MODEL = 1024
D_FF = 2816
HEAD_DIM = 64
N_HEADS = 8
N_KV = 2
GQA = N_HEADS // N_KV
WINDOW = 128
CHUNK = 128
G_HEADS = 8
Q_W = N_HEADS * HEAD_DIM
KV_W = N_KV * HEAD_DIM
D_GMLP = 512
D_IN = Q_W + 2 * KV_W + 2 * D_GMLP
K_OFF = Q_W
V_OFF = K_OFF + KV_W
U_OFF = V_OFF + KV_W
GV_OFF = U_OFF + D_GMLP
ROPE_THETA = 10000.0
PAST_LEN = 16384
EPS = 1e-6
LOG2E = 1.4426950408889634
Q_SCALE = HEAD_DIM ** -0.5 * LOG2E
LANES = 128
BF16_SUBLANES = 16

TOKEN_TILE = 512
FF_CHUNK = 256
STAGE_BLOCK_BYTES = 64 * D_FF * 4
STAGE_SLOTS = 8
SEQ_GROUP = 8
GROUPS_PER_STEP = 4
VMEM_LIMIT_BYTES = 56 * 1024 * 1024

_NT = (((1,), (1,)), ((), ()))


def _rms(x, g):
    ms = jnp.mean(x * x, axis=-1, keepdims=True)
    return (x * lax.rsqrt(ms + EPS)) * g


def _resident(shape):
    zeros = (0,) * len(shape)
    return pl.BlockSpec(shape, lambda *_: zeros, pipeline_mode=pl.Buffered(1))


def _round_rows_to_bf16(src_hbm, dst_ref, stage_ref, sem):
    n_slots, rows, _ = stage_ref.shape
    assert src_hbm.shape[0] % rows == 0
    n_blocks = src_hbm.shape[0] // rows
    assert n_blocks >= n_slots

    def fetch(i, slot):
        return pltpu.make_async_copy(src_hbm.at[pl.ds(i * rows, rows), :], stage_ref.at[slot], sem.at[slot])

    for i in range(n_slots - 1):
        fetch(i, i).start()

    def body(i, carry):
        slot = i % n_slots

        @pl.when(i + n_slots - 1 < n_blocks)
        def _():
            fetch(i + n_slots - 1, (i + n_slots - 1) % n_slots).start()

        fetch(i, slot).wait()
        dst_ref[pl.ds(pl.multiple_of(i * rows, rows), rows), :] = stage_ref[slot].astype(BF16)
        return carry

    lax.fori_loop(0, n_blocks, body, 0)


def _ffn_kernel(*refs, final_norm, n_cast, f32_weights):
    refs = iter(refs)
    xp_ref, xn_ref, xs_ref, g_ref, wg_in, wu_in, wd_in = (next(refs) for _ in range(7))
    gf_ref = next(refs) if final_norm else None
    cast_in = [next(refs) for _ in range(n_cast)]
    yp_ref, ys_ref = next(refs), next(refs)
    cast_out = [next(refs) for _ in range(n_cast)]
    h_ref, act0_ref, act_ref = next(refs), next(refs), next(refs)
    if f32_weights:
        wg_ref, wu_ref, wd_ref, wide_stage, narrow_stage, wide_sem, narrow_sem = (next(refs) for _ in range(7))
    else:
        wg_ref, wu_ref, wd_ref = wg_in, wu_in, wd_in

    def gate_up(h, c):
        sl = slice(c * FF_CHUNK, (c + 1) * FF_CHUNK)
        gate = jnp.dot(h, wg_ref[:, sl], preferred_element_type=F32)
        up = jnp.dot(h, wu_ref[:, sl], preferred_element_type=F32)
        return (gate * jax.nn.sigmoid(gate) * up).astype(BF16)

    on_sample = pl.program_id(0) == 0

    @pl.when(on_sample)
    def _():
        if f32_weights:
            _round_rows_to_bf16(wg_in, wg_ref, wide_stage, wide_sem)
            _round_rows_to_bf16(wu_in, wu_ref, wide_stage, wide_sem)
            _round_rows_to_bf16(wd_in, wd_ref, narrow_stage, narrow_sem)
        h0 = _rms(xs_ref[...], g_ref[...]).astype(BF16)
        h_ref[...] = h0
        act0_ref[...] = gate_up(h0, 0)

    for c in range(1, D_FF // FF_CHUNK):
        act_ref[:, (c - 1) * FF_CHUNK:c * FF_CHUNK] = gate_up(h_ref[...], c)
    down0 = jnp.dot(act0_ref[...], wd_ref[0:FF_CHUNK, :], preferred_element_type=F32)
    hn = _rms(xn_ref[...], g_ref[...]).astype(BF16)
    h_ref[...] = hn
    act0_ref[...] = gate_up(hn, 0)
    x = jnp.where(on_sample, xs_ref[...], xp_ref[...])
    y = x + 0.5 * (down0 + jnp.dot(act_ref[...], wd_ref[FF_CHUNK:, :], preferred_element_type=F32))
    yp_ref[...] = _rms(y, gf_ref[...]) if final_norm else y

    @pl.when(on_sample)
    def _():
        ys_ref[...] = yp_ref[...].reshape(ys_ref.shape)

    for src, dst in zip(cast_in, cast_out):
        dst[...] = src[...].astype(BF16)


def _cast_row_blocks(rows, n_steps):
    return max(d for d in range(1, n_steps + 1) if rows % d == 0 and (rows // d) % BF16_SUBLANES == 0)


def _ffn_half(xp, xs, g, wg, wu, wd, gf=None, cast=(), sample_out_shape=None):
    tm = TOKEN_TILE
    f32_weights = wg.dtype == F32
    assert all(w.dtype == wg.dtype for w in (wu, wd))
    n = xp.shape[0]
    assert n % tm == 0 and xs.shape[0] == tm
    n_tiles = n // tm
    prompt_spec = pl.BlockSpec((tm, D_MODEL), lambda i: (jnp.maximum(i - 1, 0), 0))
    next_spec = pl.BlockSpec((tm, D_MODEL), lambda i: (jnp.minimum(i, n_tiles - 1), 0))
    sample_in = _resident((tm, D_MODEL))
    sample_shape = sample_out_shape or (tm, D_MODEL)
    sample_out = pl.BlockSpec(sample_shape, lambda i: (0,) * len(sample_shape))
    weight_specs = ([pl.BlockSpec(memory_space=pl.ANY)] * 3 if f32_weights else
                    [_resident((D_MODEL, D_FF)), _resident((D_MODEL, D_FF)), _resident((D_FF, D_MODEL))])
    in_specs = [prompt_spec, next_spec, sample_in, _resident((1, D_MODEL))] + weight_specs
    args = [xp, xp, xs, g.reshape(1, D_MODEL), wg, wu, wd]
    if gf is not None:
        in_specs.append(_resident((1, D_MODEL)))
        args.append(gf.reshape(1, D_MODEL))
    out_specs = [prompt_spec, sample_out]
    out_shape = [jax.ShapeDtypeStruct((n, D_MODEL), F32), jax.ShapeDtypeStruct(sample_shape, F32)]
    cast_specs = []
    for w in cast:
        rows, cols = w.shape
        nb = _cast_row_blocks(rows, n_tiles)
        cast_specs.append(pl.BlockSpec((rows // nb, cols), lambda i, nb=nb: (jnp.minimum(i, nb - 1), 0)))
        out_shape.append(jax.ShapeDtypeStruct(w.shape, BF16))
    outs = pl.pallas_call(
        functools.partial(_ffn_kernel, final_norm=gf is not None, n_cast=len(cast), f32_weights=f32_weights),
        grid=(n_tiles + 1,),
        in_specs=in_specs + cast_specs,
        out_specs=out_specs + cast_specs,
        out_shape=out_shape,
        scratch_shapes=[pltpu.VMEM((tm, D_MODEL), BF16), pltpu.VMEM((tm, FF_CHUNK), BF16),
                        pltpu.VMEM((tm, D_FF - FF_CHUNK), BF16)] + ([
                            pltpu.VMEM((D_MODEL, D_FF), BF16), pltpu.VMEM((D_MODEL, D_FF), BF16),
                            pltpu.VMEM((D_FF, D_MODEL), BF16),
                            pltpu.VMEM((STAGE_SLOTS, STAGE_BLOCK_BYTES // (4 * D_FF), D_FF), F32),
                            pltpu.VMEM((STAGE_SLOTS, STAGE_BLOCK_BYTES // (4 * D_MODEL), D_MODEL), F32),
                            pltpu.SemaphoreType.DMA((STAGE_SLOTS,)), pltpu.SemaphoreType.DMA((STAGE_SLOTS,))]
                        if f32_weights else []),
        compiler_params=pltpu.CompilerParams(dimension_semantics=("arbitrary",), vmem_limit_bytes=VMEM_LIMIT_BYTES),
        name="ffn_final" if gf is not None else "ffn_half",
    )(*args, *cast)
    return outs[0], outs[1], list(outs[2:])


def _first_half(shape):
    return (lax.broadcasted_iota(jnp.int32, shape, 1) & (HEAD_DIM - 1)) < HEAD_DIM // 2


def _sign_sin(sin):
    return jnp.where(_first_half(sin.shape), -sin, sin)


def _rope(xg, cos, sin_signed):
    swapped = jnp.where(_first_half(xg.shape), pltpu.roll(xg, LANES - HEAD_DIM // 2, 1),
                        pltpu.roll(xg, HEAD_DIM // 2, 1))
    return xg * cos + swapped * sin_signed


def _in_proj(x, gmix, w_in, cos, sin_signed, gvn_gain):
    h = _rms(x, gmix).astype(BF16)
    z = jnp.dot(h, w_in, preferred_element_type=F32)
    q = jnp.concatenate(
        [(_rope(z[:, LANES * i:LANES * (i + 1)], cos, sin_signed) * Q_SCALE).astype(BF16) for i in range(Q_W // LANES)],
        axis=1)
    k = _rope(z[:, K_OFF:V_OFF], cos, sin_signed)
    v = z[:, V_OFF:U_OFF]
    u = _gelu(z[:, U_OFF:GV_OFF])
    gvn = _rms(_gelu(z[:, GV_OFF:]), gvn_gain)
    return q, k, v, u, gvn


def _out_proj(x, ya, yg, ga, gg, w_out):
    cat = jnp.concatenate([_rms(ya, ga).astype(BF16), _rms(yg, gg).astype(BF16)], axis=1)
    return x + jnp.dot(cat, w_out, preferred_element_type=F32)


def _softmax_sink(s, sink):
    m = jnp.maximum(jnp.max(s, axis=1, keepdims=True), sink)
    p = jnp.exp2(s - m)
    return p, jnp.sum(p, axis=1, keepdims=True) + jnp.exp2(sink - m)


def _gelu(x):
    k = -2.0 * (2.0 / jnp.pi) ** 0.5 * LOG2E
    return x / (1.0 + jnp.exp2(x * (k + (0.044715 * k) * (x * x))))


def _mix_prompt_kernel(xa_ref, xc_ref, coff_ref, soff_ref, cbase_ref, sbase_ref, gmix_ref, win_ref, sinks_ref, gvg_ref,
                       wcat_ref, bias_ref, ga_ref, gg_ref, wout_ref, o_ref, ko_ref, vo_ref, gvo_ref, q_s, k_s, v_s, u_s,
                       gv_s, cat_s, *, tm, tiles_per_seq, n_tiles):
    s = pl.program_id(0)

    @pl.when(s == 0)
    def _():
        q_s[1] = jnp.zeros((tm, Q_W), BF16)
        k_s[1] = jnp.zeros((tm + WINDOW, KV_W), BF16)
        v_s[1] = jnp.zeros((tm + WINDOW, KV_W), BF16)
        u_s[1] = jnp.zeros((tm, D_GMLP), F32)
        gv_s[1] = jnp.zeros((tm, D_GMLP), BF16)
        cat_s[0] = jnp.zeros((tm, D_MODEL), BF16)

    for parity in range(2):
        pl.when(s % 2 == parity)(functools.partial(
            _mix_prompt_step, s, parity, xa_ref, xc_ref, coff_ref, soff_ref, cbase_ref, sbase_ref, gmix_ref, win_ref,
            sinks_ref, gvg_ref, wcat_ref, bias_ref, ga_ref, gg_ref, wout_ref, o_ref, ko_ref, vo_ref, gvo_ref, q_s, k_s,
            v_s, u_s, gv_s, cat_s, tm=tm, tiles_per_seq=tiles_per_seq, n_tiles=n_tiles))


def _mix_prompt_step(s, cur, xa_ref, xc_ref, coff_ref, soff_ref, cbase_ref, sbase_ref, gmix_ref, win_ref, sinks_ref,
                     gvg_ref, wcat_ref, bias_ref, ga_ref, gg_ref, wout_ref, o_ref, ko_ref, vo_ref, gvo_ref, q_s, k_s, v_s,
                     u_s, gv_s, cat_s, *, tm, tiles_per_seq, n_tiles):
    oth = 1 - cur

    h = _rms(xa_ref[0], gmix_ref[...]).astype(BF16)
    tile_in_seq = jnp.minimum(s, n_tiles - 1) % tiles_per_seq
    cb = cbase_ref[pl.ds(tile_in_seq, 1), :]
    sb = sbase_ref[pl.ds(tile_in_seq, 1), :]
    cos = cb * coff_ref[...] - sb * soff_ref[...]
    sin = _sign_sin(sb * coff_ref[...] + cb * soff_ref[...])

    def proj_q():
        z = jnp.dot(h, win_ref[:, 0:Q_W], preferred_element_type=F32)
        for i in range(Q_W // LANES):
            q_s[cur, :, LANES * i:LANES * (i + 1)] = (
                _rope(z[:, LANES * i:LANES * (i + 1)], cos, sin) * Q_SCALE).astype(BF16)

    def proj_kv():
        z = jnp.dot(h, win_ref[:, K_OFF:U_OFF], preferred_element_type=F32)
        k = _rope(z[:, 0:KV_W], cos, sin)
        v = z[:, KV_W:]
        k_s[cur, 0:WINDOW, :] = k_s[oth, tm:tm + WINDOW, :]
        v_s[cur, 0:WINDOW, :] = v_s[oth, tm:tm + WINDOW, :]
        k_s[cur, WINDOW:, :] = k.astype(BF16)
        v_s[cur, WINDOW:, :] = v.astype(BF16)
        ko_ref[0] = k[tm - WINDOW:].T
        vo_ref[0] = v[tm - WINDOW:].T

    def proj_u():
        u_s[cur] = _gelu(jnp.dot(h, win_ref[:, U_OFF:GV_OFF], preferred_element_type=F32))

    def proj_gv():
        gvn = _rms(_gelu(jnp.dot(h, win_ref[:, GV_OFF:], preferred_element_type=F32)), gvg_ref[...])
        gv_s[cur] = gvn.astype(BF16)
        gvo_ref[0] = gvn[tm - CHUNK:].T

    def out_half(c):
        cols = slice(c * (D_MODEL // 2), (c + 1) * (D_MODEL // 2))

        def run():
            o_ref[0, :, cols] = xc_ref[0, :, cols] + jnp.dot(cat_s[cur], wout_ref[:, cols], preferred_element_type=F32)
        return run

    mid_fill = [[out_half(0)], [proj_q], [proj_kv, proj_u], [proj_gv]]
    end_fill = [[], [], [], [out_half(1)]]
    assert len(mid_fill) == len(end_fill) == tm // WINDOW

    wrow = lax.broadcasted_iota(jnp.int32, (CHUNK, CHUNK), 0)
    wcol = lax.broadcasted_iota(jnp.int32, (CHUNK, CHUNK), 1)
    wtril = [jnp.where(wcol <= wrow, wcat_ref[hd], 0.0).astype(BF16) for hd in range(G_HEADS)]
    wmix = [jnp.concatenate(wtril[2 * p:2 * p + 2], axis=1) for p in range(G_HEADS // 2)]
    lane = lax.broadcasted_iota(jnp.int32, (CHUNK, LANES), 1)
    low_head = lane < HEAD_DIM

    qi = lax.broadcasted_iota(jnp.int32, (WINDOW, 2 * WINDOW), 0)
    sj = lax.broadcasted_iota(jnp.int32, (WINDOW, 2 * WINDOW), 1)
    dist = WINDOW + qi - sj
    band = (dist >= 0) & (dist < WINDOW)
    first_lo = jnp.where((s + tiles_per_seq - 1) % tiles_per_seq == 0, WINDOW, 0)

    for j in range(tm // WINDOW):
        rows = slice(j * WINDOW, (j + 1) * WINDOW)
        mask = band & (sj >= first_lo) if j == 0 else band
        qb = q_s[oth, rows, :]
        scores = []
        for kv in range(N_KV):
            qs = jnp.concatenate(
                [qb[:, HEAD_DIM * hd:HEAD_DIM * (hd + 1)] for hd in range(kv * GQA, (kv + 1) * GQA)], axis=0)
            kb = k_s[oth, j * WINDOW:(j + 2) * WINDOW, HEAD_DIM * kv:HEAD_DIM * (kv + 1)]
            scores.append(lax.dot_general(qs, kb, _NT, preferred_element_type=F32))
        for run in mid_fill[j]:
            run()
        outs, dens = [], []
        for kv in range(N_KV):
            vb = v_s[oth, j * WINDOW:(j + 2) * WINDOW, HEAD_DIM * kv:HEAD_DIM * (kv + 1)]
            ps = []
            for g in range(GQA):
                sg = jnp.where(mask, scores[kv][g * WINDOW:(g + 1) * WINDOW], -jnp.inf)
                p, den = _softmax_sink(sg, sinks_ref[kv * GQA + g] * LOG2E)
                ps.append(p.astype(BF16))
                dens.append(den)
            outs.append(jnp.dot(jnp.concatenate(ps, axis=0), vb, preferred_element_type=F32))
        mixed = []
        for p in range(G_HEADS // 2):
            r = gv_s[oth, rows, LANES * p:LANES * (p + 1)]
            zero = jnp.zeros_like(r)
            rhs = jnp.concatenate([jnp.where(low_head, r, zero), jnp.where(low_head, zero, r)], axis=0)
            mixed.append(jnp.dot(wmix[p], rhs, preferred_element_type=F32))
        for run in end_fill[j]:
            run()
        ya = [outs[hd // GQA][(hd % GQA) * WINDOW:(hd % GQA + 1) * WINDOW] / dens[hd] for hd in range(N_HEADS)]
        cat_s[oth, rows, 0:Q_W] = _rms(jnp.concatenate(ya, axis=1), ga_ref[...]).astype(BF16)
        yg = u_s[oth, rows, :] * (jnp.concatenate(mixed, axis=1) + bias_ref[...])
        cat_s[oth, rows, Q_W:] = _rms(yg, gg_ref[...]).astype(BF16)


def _mix_prompt(x, gmix, w_in, sinks, gvg, wcat, bias_full, ga, gg, w_out):
    b, s, _ = x.shape
    tm = TOKEN_TILE
    assert s % tm == 0 and tm % WINDOW == 0
    tiles_per_seq = s // tm
    n_tiles = b * tiles_per_seq
    cos_off, sin_off = _rope_tables(jnp.arange(tm, dtype=jnp.int32))
    cos_base, sin_base = _rope_tables(tm * jnp.arange(tiles_per_seq, dtype=jnp.int32))
    proj_tile = lambda i: jnp.minimum(i, n_tiles - 1)
    out_tile = lambda i: jnp.maximum(i - 2, 0)
    x_tiles = x.reshape(n_tiles, tm, D_MODEL)
    last = lambda width: pl.BlockSpec((1, width, WINDOW), lambda i: (proj_tile(i) // tiles_per_seq, 0, 0))
    out, ko, vo, gvo = pl.pallas_call(
        functools.partial(_mix_prompt_kernel, tm=tm, tiles_per_seq=tiles_per_seq, n_tiles=n_tiles),
        grid=(n_tiles + 2,),
        in_specs=[pl.BlockSpec((1, tm, D_MODEL), lambda i: (proj_tile(i), 0, 0)),
                  pl.BlockSpec((1, tm, D_MODEL), lambda i: (out_tile(i), 0, 0)),
                  _resident((tm, LANES)), _resident((tm, LANES)), _resident((tiles_per_seq, LANES)),
                  _resident((tiles_per_seq, LANES)), _resident((1, D_MODEL)), _resident((D_MODEL, D_IN)),
                  pl.BlockSpec(memory_space=pltpu.SMEM), _resident((1, D_GMLP)),
                  _resident((G_HEADS, CHUNK, CHUNK)), _resident((CHUNK, D_GMLP)), _resident((1, Q_W)),
                  _resident((1, D_GMLP)), _resident((D_MODEL, D_MODEL))],
        out_specs=[pl.BlockSpec((1, tm, D_MODEL), lambda i: (out_tile(i), 0, 0)), last(KV_W), last(KV_W),
                   last(D_GMLP)],
        out_shape=[jax.ShapeDtypeStruct((n_tiles, tm, D_MODEL), F32), jax.ShapeDtypeStruct((b, KV_W, WINDOW), F32),
                   jax.ShapeDtypeStruct((b, KV_W, WINDOW), F32), jax.ShapeDtypeStruct((b, D_GMLP, CHUNK), F32)],
        scratch_shapes=[pltpu.VMEM((2, tm, Q_W), BF16), pltpu.VMEM((2, tm + WINDOW, KV_W), BF16),
                        pltpu.VMEM((2, tm + WINDOW, KV_W), BF16), pltpu.VMEM((2, tm, D_GMLP), F32),
                        pltpu.VMEM((2, tm, D_GMLP), BF16), pltpu.VMEM((2, tm, D_MODEL), BF16)],
        compiler_params=pltpu.CompilerParams(dimension_semantics=("arbitrary",), vmem_limit_bytes=VMEM_LIMIT_BYTES),
        name="mix_prompt",
    )(x_tiles, x_tiles, cos_off, sin_off, cos_base, sin_base, gmix, w_in, sinks, gvg, wcat, bias_full, ga, gg, w_out)
    return out.reshape(b, s, D_MODEL), ko, vo, gvo


def _mix_sample_kernel(x_ref, ckt_ref, cvt_ref, cos_ref, sin_ref, gmix_ref, win_ref, sinks_ref, gvg_ref, coef_ref,
                       biasr_ref, ga_ref, gg_ref, wout_ref, o_ref, kot_ref, vot_ref, gvo_ref, q_s, k_s, v_s, ya_s,
                       yg_s, *, t_new, w_buf):
    step = pl.program_id(0)
    n_tok = x_ref.shape[0]
    step_seqs = SEQ_GROUP * GROUPS_PER_STEP
    step_rows = step_seqs * t_new
    grp_rows = SEQ_GROUP * t_new
    grp_keys = SEQ_GROUP * w_buf
    sub = 8

    @pl.when(step == 0)
    def _():
        tile = lambda tab: jnp.broadcast_to(tab[None], (n_tok // sub, sub, LANES)).reshape(n_tok, LANES)
        q, k, v, u, gvn = _in_proj(x_ref[...], gmix_ref[...], win_ref[...], tile(cos_ref[...]),
                                   tile(_sign_sin(sin_ref[...])), gvg_ref[...])
        q_s[...] = q
        k_s[...] = k
        v_s[...] = v
        gv_seq = gvn.reshape(n_tok // t_new, t_new, D_GMLP)
        for tok in range(t_new):
            gvo_ref[tok] = gv_seq[:, tok, :].T
        g3 = gvn.reshape(n_tok // sub, sub, D_GMLP)
        trow = lax.broadcasted_iota(jnp.int32, (1, sub, D_GMLP), 1) & (t_new - 1)
        mixed = biasr_ref[...][None] + coef_ref[0][None] * g3
        for d in range(1, t_new):
            shifted = jnp.where(trow >= d, pltpu.roll(g3, d, 1), 0.0)
            mixed = mixed + coef_ref[d][None] * shifted
        yg_s[...] = (u.reshape(n_tok // sub, sub, D_GMLP) * mixed).reshape(n_tok, D_GMLP)

    row0 = pl.multiple_of(step * step_rows, step_rows)
    q_step = q_s[pl.ds(row0, step_rows), :]
    kn = k_s[pl.ds(row0, step_rows), :]
    vn = v_s[pl.ds(row0, step_rows), :]

    knt = kn.T
    vnt = vn.T
    tail = lax.broadcasted_iota(jnp.int32, (KV_W, w_buf), 1) >= w_buf - t_new
    for b in range(step_seqs):
        shift = (w_buf - t_new - t_new * b) % w_buf
        kot_ref[b] = jnp.where(tail, pltpu.roll(knt, shift, 1) if shift else knt,
                               pltpu.roll(ckt_ref[b], w_buf - t_new, 1))
        vot_ref[b] = jnp.where(tail, pltpu.roll(vnt, shift, 1) if shift else vnt,
                               pltpu.roll(cvt_ref[b], w_buf - t_new, 1))

    knb = kn.astype(BF16)
    vnb = vn.astype(BF16)

    n_rows = GQA * grp_rows
    shift_t = t_new.bit_length() - 1
    shift_w = w_buf.bit_length() - 1
    r = lax.broadcasted_iota(jnp.int32, (n_rows, grp_keys), 0)
    c = lax.broadcasted_iota(jnp.int32, (n_rows, grp_keys), 1)
    mask_c = ((c >> shift_w) == ((r & (grp_rows - 1)) >> shift_t)) & ((c & (w_buf - 1)) > (r & (t_new - 1)))
    r2 = lax.broadcasted_iota(jnp.int32, (n_rows, grp_rows), 0)
    c2 = lax.broadcasted_iota(jnp.int32, (n_rows, grp_rows), 1)
    mask_n = ((c2 >> shift_t) == ((r2 & (grp_rows - 1)) >> shift_t)) & ((c2 & (t_new - 1)) <= (r2 & (t_new - 1)))
    row_head = lax.broadcasted_iota(jnp.int32, (n_rows, 1), 0) >> (grp_rows.bit_length() - 1)

    for grp in range(GROUPS_PER_STEP):
        rows = slice(grp * grp_rows, (grp + 1) * grp_rows)
        seqs = range(grp * SEQ_GROUP, (grp + 1) * SEQ_GROUP)
        for kv in range(N_KV):
            heads = [kv * GQA + i for i in range(GQA)]
            lanes = slice(HEAD_DIM * kv, HEAD_DIM * (kv + 1))
            kt = jnp.concatenate([ckt_ref[b, lanes, :] for b in seqs], axis=1).astype(BF16)
            vt = jnp.concatenate([cvt_ref[b, lanes, :] for b in seqs], axis=1).astype(BF16)
            qs = jnp.concatenate([q_step[rows, HEAD_DIM * hd:HEAD_DIM * (hd + 1)] for hd in heads], axis=0)
            s_c = jnp.where(mask_c, jnp.dot(qs, kt, preferred_element_type=F32), -jnp.inf)
            s_n = jnp.where(mask_n, lax.dot_general(qs, knb[rows, lanes], _NT, preferred_element_type=F32), -jnp.inf)
            sink = jnp.full((n_rows, 1), sinks_ref[heads[0]] * LOG2E, F32)
            for i in range(1, GQA):
                sink = jnp.where(row_head == i, sinks_ref[heads[i]] * LOG2E, sink)
            m = jnp.maximum(jnp.maximum(jnp.max(s_c, axis=1, keepdims=True), jnp.max(s_n, axis=1, keepdims=True)),
                            sink)
            p_c = jnp.exp2(s_c - m)
            p_n = jnp.exp2(s_n - m)
            den = jnp.sum(p_c, axis=1, keepdims=True) + jnp.sum(p_n, axis=1, keepdims=True) + jnp.exp2(sink - m)
            o = (lax.dot_general(p_c.astype(BF16), vt, _NT, preferred_element_type=F32)
                 + jnp.dot(p_n.astype(BF16), vnb[rows, lanes], preferred_element_type=F32)) / den
            for i, hd in enumerate(heads):
                ya_s[pl.ds(row0 + grp * grp_rows, grp_rows), HEAD_DIM * hd:HEAD_DIM * (hd + 1)] = (
                    o[i * grp_rows:(i + 1) * grp_rows])

    @pl.when(step == pl.num_programs(0) - 1)
    def _():
        o_ref[...] = _out_proj(x_ref[...], ya_s[...], yg_s[...], ga_ref[...], gg_ref[...], wout_ref[...])


def _mix_sample(x, cache_kt, cache_vt, cos, sin, gmix, w_in, sinks, gvg, coef, bias_rows, ga, gg, w_out, t_new):
    n_tok = x.shape[0]
    n_seq, _, w_buf = cache_kt.shape
    step_seqs = SEQ_GROUP * GROUPS_PER_STEP
    assert n_seq % step_seqs == 0 and n_tok == n_seq * t_new
    assert t_new & (t_new - 1) == 0 and w_buf & (w_buf - 1) == 0 and 8 % t_new == 0
    assert step_seqs * t_new == w_buf == LANES
    cache_spec = pl.BlockSpec((step_seqs, KV_W, w_buf), lambda i: (i, 0, 0))
    return pl.pallas_call(
        functools.partial(_mix_sample_kernel, t_new=t_new, w_buf=w_buf),
        grid=(n_seq // step_seqs,),
        in_specs=[_resident((n_tok, D_MODEL)), cache_spec, cache_spec, _resident((8, LANES)),
                  _resident((8, LANES)), _resident((1, D_MODEL)), _resident((D_MODEL, D_IN)),
                  pl.BlockSpec(memory_space=pltpu.SMEM), _resident((1, D_GMLP)), _resident((t_new, 8, D_GMLP)),
                  _resident((8, D_GMLP)), _resident((1, Q_W)), _resident((1, D_GMLP)),
                  _resident((D_MODEL, D_MODEL))],
        out_specs=[pl.BlockSpec((n_tok, D_MODEL), lambda i: (0, 0)), cache_spec, cache_spec,
                   pl.BlockSpec((t_new, D_GMLP, n_seq), lambda i: (0, 0, 0))],
        out_shape=[jax.ShapeDtypeStruct((n_tok, D_MODEL), F32), jax.ShapeDtypeStruct(cache_kt.shape, F32),
                   jax.ShapeDtypeStruct(cache_vt.shape, F32), jax.ShapeDtypeStruct((t_new, D_GMLP, n_seq), F32)],
        scratch_shapes=[pltpu.VMEM((n_tok, Q_W), BF16), pltpu.VMEM((n_tok, KV_W), F32), pltpu.VMEM((n_tok, KV_W), F32),
                        pltpu.VMEM((n_tok, Q_W), F32), pltpu.VMEM((n_tok, D_GMLP), F32)],
        compiler_params=pltpu.CompilerParams(dimension_semantics=("arbitrary",), vmem_limit_bytes=VMEM_LIMIT_BYTES),
        name="mix_sample",
    )(x, cache_kt, cache_vt, cos, sin, gmix, w_in, sinks, gvg, coef, bias_rows, ga, gg, w_out)


def _gmlp_tables_kernel(ws_ref, bs_ref, bias_ref, coef_ref, biasr_ref, *, t_new):
    sub = 8
    expand = lambda col: jnp.broadcast_to(col, (col.shape[0], D_GMLP // G_HEADS))
    row = lax.broadcasted_iota(jnp.int32, (CHUNK, CHUNK), 0)
    lane = lax.broadcasted_iota(jnp.int32, (CHUNK, CHUNK), 1)
    bias = jnp.concatenate(
        [expand(jnp.sum(jnp.where(lane == row, bs_ref[hd:hd + 1, :], 0.0), axis=1, keepdims=True))
         for hd in range(G_HEADS)], axis=1)
    bias_ref[...] = bias
    biasr_ref[...] = jnp.concatenate([bias[0:t_new]] * (sub // t_new), axis=0)
    t = lax.broadcasted_iota(jnp.int32, (sub, CHUNK), 0) & (t_new - 1)
    s = lax.broadcasted_iota(jnp.int32, (sub, CHUNK), 1)
    for d in range(t_new):
        cols = []
        for hd in range(G_HEADS):
            w_rows = jnp.concatenate([ws_ref[hd, 0:t_new, :]] * (sub // t_new), axis=0)
            cols.append(expand(jnp.sum(jnp.where(s == t - d, w_rows, 0.0), axis=1, keepdims=True)))
        coef_ref[d] = jnp.concatenate(cols, axis=1)


def _gmlp_tables(ws, bs, t_new):
    assert 8 % t_new == 0
    return pl.pallas_call(
        functools.partial(_gmlp_tables_kernel, t_new=t_new),
        out_shape=[jax.ShapeDtypeStruct((CHUNK, D_GMLP), F32), jax.ShapeDtypeStruct((t_new, 8, D_GMLP), F32),
                   jax.ShapeDtypeStruct((8, D_GMLP), F32)],
        name="gmlp_tables",
    )(ws, bs)


def _rope_tables(pos):
    inv_freq = ROPE_THETA ** (-jnp.arange(0, HEAD_DIM, 2, dtype=F32) / HEAD_DIM)
    ang = pos.astype(F32)[:, None] * jnp.tile(inv_freq, 2 * LANES // HEAD_DIM)[None, :]
    return jnp.cos(ang), jnp.sin(ang)


def kernel(x_prompt, x_sample, cache_k_win, cache_v_win, norm_ffn1, ffn1_gate, ffn1_up, ffn1_down, norm_mix, w_in,
           attn_sinks, gmlp_v_norm, gmlp_w_s, gmlp_b_s, norm_attn_out, norm_gmlp_out, w_out, norm_ffn2, ffn2_gate,
           ffn2_up, ffn2_down, norm_final):
    depth = norm_ffn1.shape[0]
    b, s, _ = x_prompt.shape
    bd, t_new, _ = x_sample.shape
    w_buf = cache_k_win.shape[2]

    cos_s, sin_s = _rope_tables(PAST_LEN + jnp.arange(t_new, dtype=jnp.int32))
    cos_s, sin_s = jnp.tile(cos_s, (8 // t_new, 1)), jnp.tile(sin_s, (8 // t_new, 1))

    hp = x_prompt.reshape(b * s, D_MODEL)
    hs = x_sample.reshape(bd * t_new, D_MODEL)
    outs = [[] for _ in range(6)]
    ffn1_w = [ffn1_gate[0], ffn1_up[0], ffn1_down[0]]
    for l in range(depth):
        last = l == depth - 1
        row = lambda a: a[l].reshape(1, -1)

        wcat = gmlp_w_s[l]
        bias_full, coef, bias_rows = _gmlp_tables(gmlp_w_s[l], gmlp_b_s[l], t_new)

        hp, hs, (w_in_b, w_out_b, wg2, wu2, wd2) = _ffn_half(
            hp, hs, norm_ffn1[l], *ffn1_w, cast=(w_in[l], w_out[l], ffn2_gate[l], ffn2_up[l], ffn2_down[l]))

        hp, kpt, vpt, gvpt = _mix_prompt(hp.reshape(b, s, D_MODEL), row(norm_mix), w_in_b, attn_sinks[l],
                                         row(gmlp_v_norm), wcat, bias_full, row(norm_attn_out), row(norm_gmlp_out),
                                         w_out_b)
        to_t = lambda c: c.transpose(0, 2, 3, 1).reshape(bd, KV_W, w_buf)
        hs, kst, vst, gvs = _mix_sample(hs, to_t(cache_k_win[l]), to_t(cache_v_win[l]), cos_s, sin_s, row(norm_mix),
                                        w_in_b, attn_sinks[l], row(gmlp_v_norm), coef, bias_rows, row(norm_attn_out),
                                        row(norm_gmlp_out), w_out_b, t_new)
        from_t = lambda c: c.reshape(bd, N_KV, HEAD_DIM, w_buf).transpose(0, 3, 1, 2)

        next_ffn1 = () if last else (ffn1_gate[l + 1], ffn1_up[l + 1], ffn1_down[l + 1])
        hp, hs, ffn1_w = _ffn_half(hp.reshape(b * s, D_MODEL), hs, norm_ffn2[l], wg2, wu2, wd2,
                                   gf=norm_final if last else None, cast=next_ffn1,
                                   sample_out_shape=(bd, t_new, D_MODEL) if last else None)

        outs[0].append(kpt.reshape(b, N_KV, HEAD_DIM, WINDOW).transpose(0, 3, 1, 2))
        outs[1].append(vpt.reshape(b, N_KV, HEAD_DIM, WINDOW).transpose(0, 3, 1, 2))
        outs[2].append(from_t(kst))
        outs[3].append(from_t(vst))
        outs[4].append(gvpt.reshape(b, G_HEADS, D_GMLP // G_HEADS, CHUNK).transpose(0, 3, 1, 2))
        outs[5].append(gvs.reshape(t_new, G_HEADS, D_GMLP // G_HEADS, bd).transpose(3, 0, 1, 2))

    return (hp.reshape(b, s, D_MODEL), hs) + tuple(jnp.stack(o) for o in outs)
```

```python
import functools

import jax
import jax.numpy as jnp
from jax import lax
from jax.experimental import pallas as pl
from jax.experimental.pallas import tpu as pltpu

F32 = jnp.float32
BF16 = jnp.bfloat16

D_MODEL = 1024
D_FF = 2816
HEAD_DIM = 64
N_HEADS = 8
N_KV = 2
GQA = N_HEADS // N_KV
WINDOW = 128
CHUNK = 128
G_HEADS = 8
Q_W = N_HEADS * HEAD_DIM
KV_W = N_KV * HEAD_DIM
D_GMLP = 512
D_IN = Q_W + 2 * KV_W + 2 * D_GMLP
K_OFF = Q_W
V_OFF = K_OFF + KV_W
U_OFF = V_OFF + KV_W
GV_OFF = U_OFF + D_GMLP
ROPE_THETA = 10000.0
PAST_LEN = 16384
EPS = 1e-6
LOG2E = 1.4426950408889634
Q_SCALE = HEAD_DIM ** -0.5 * LOG2E
LANES = 128
BF16_SUBLANES = 16

TOKEN_TILE = 512
FF_CHUNK = 256
STAGE_BLOCK_BYTES = 64 * D_FF * 4
STAGE_SLOTS = 8
SEQ_GROUP = 8
GROUPS_PER_STEP = 4
VMEM_LIMIT_BYTES = 56 * 1024 * 1024

_NT = (((1,), (1,)), ((), ()))


def _rms(x, g):
    ms = jnp.mean(x * x, axis=-1, keepdims=True)
    return (x * lax.rsqrt(ms + EPS)) * g


def _resident(shape):
    zeros = (0,) * len(shape)
    return pl.BlockSpec(shape, lambda *_: zeros, pipeline_mode=pl.Buffered(1))


def _round_rows_to_bf16(src_hbm, dst_ref, stage_ref, sem):
    n_slots, rows, _ = stage_ref.shape
    assert src_hbm.shape[0] % rows == 0
    n_blocks = src_hbm.shape[0] // rows
    assert n_blocks >= n_slots

    def fetch(i, slot):
        return pltpu.make_async_copy(src_hbm.at[pl.ds(i * rows, rows), :], stage_ref.at[slot], sem.at[slot])

    for i in range(n_slots - 1):
        fetch(i, i).start()

    def body(i, carry):
        slot = i % n_slots

        @pl.when(i + n_slots - 1 < n_blocks)
        def _():
            fetch(i + n_slots - 1, (i + n_slots - 1) % n_slots).start()

        fetch(i, slot).wait()
        dst_ref[pl.ds(pl.multiple_of(i * rows, rows), rows), :] = stage_ref[slot].astype(BF16)
        return carry

    lax.fori_loop(0, n_blocks, body, 0)


def _ffn_kernel(*refs, final_norm, n_cast, f32_weights):
    refs = iter(refs)
    xp_ref, xn_ref, xs_ref, g_ref, wg_in, wu_in, wd_in = (next(refs) for _ in range(7))
    gf_ref = next(refs) if final_norm else None
    cast_in = [next(refs) for _ in range(n_cast)]
    yp_ref, ys_ref = next(refs), next(refs)
    cast_out = [next(refs) for _ in range(n_cast)]
    h_ref, act0_ref, act_ref = next(refs), next(refs), next(refs)
    if f32_weights:
        wg_ref, wu_ref, wd_ref, wide_stage, narrow_stage, wide_sem, narrow_sem = (next(refs) for _ in range(7))
    else:
        wg_ref, wu_ref, wd_ref = wg_in, wu_in, wd_in

    def gate_up(h, c):
        sl = slice(c * FF_CHUNK, (c + 1) * FF_CHUNK)
        gate = jnp.dot(h, wg_ref[:, sl], preferred_element_type=F32)
        up = jnp.dot(h, wu_ref[:, sl], preferred_element_type=F32)
        return (gate * jax.nn.sigmoid(gate) * up).astype(BF16)

    on_sample = pl.program_id(0) == 0

    @pl.when(on_sample)
    def _():
        if f32_weights:
            _round_rows_to_bf16(wg_in, wg_ref, wide_stage, wide_sem)
            _round_rows_to_bf16(wu_in, wu_ref, wide_stage, wide_sem)
            _round_rows_to_bf16(wd_in, wd_ref, narrow_stage, narrow_sem)
        h0 = _rms(xs_ref[...], g_ref[...]).astype(BF16)
        h_ref[...] = h0
        act0_ref[...] = gate_up(h0, 0)

    for c in range(1, D_FF // FF_CHUNK):
        act_ref[:, (c - 1) * FF_CHUNK:c * FF_CHUNK] = gate_up(h_ref[...], c)
    down0 = jnp.dot(act0_ref[...], wd_ref[0:FF_CHUNK, :], preferred_element_type=F32)
    hn = _rms(xn_ref[...], g_ref[...]).astype(BF16)
    h_ref[...] = hn
    act0_ref[...] = gate_up(hn, 0)
    x = jnp.where(on_sample, xs_ref[...], xp_ref[...])
    y = x + 0.5 * (down0 + jnp.dot(act_ref[...], wd_ref[FF_CHUNK:, :], preferred_element_type=F32))
    yp_ref[...] = _rms(y, gf_ref[...]) if final_norm else y

    @pl.when(on_sample)
    def _():
        ys_ref[...] = yp_ref[...].reshape(ys_ref.shape)

    for src, dst in zip(cast_in, cast_out):
        dst[...] = src[...].astype(BF16)


def _cast_row_blocks(rows, n_steps):
    return max(d for d in range(1, n_steps + 1) if rows % d == 0 and (rows // d) % BF16_SUBLANES == 0)


def _ffn_half(xp, xs, g, wg, wu, wd, gf=None, cast=(), sample_out_shape=None):
    tm = TOKEN_TILE
    f32_weights = wg.dtype == F32
    assert all(w.dtype == wg.dtype for w in (wu, wd))
    n = xp.shape[0]
    assert n % tm == 0 and xs.shape[0] == tm
    n_tiles = n // tm
    prompt_spec = pl.BlockSpec((tm, D_MODEL), lambda i: (jnp.maximum(i - 1, 0), 0))
    next_spec = pl.BlockSpec((tm, D_MODEL), lambda i: (jnp.minimum(i, n_tiles - 1), 0))
    sample_in = _resident((tm, D_MODEL))
    sample_shape = sample_out_shape or (tm, D_MODEL)
    sample_out = pl.BlockSpec(sample_shape, lambda i: (0,) * len(sample_shape))
    weight_specs = ([pl.BlockSpec(memory_space=pl.ANY)] * 3 if f32_weights else
                    [_resident((D_MODEL, D_FF)), _resident((D_MODEL, D_FF)), _resident((D_FF, D_MODEL))])
    in_specs = [prompt_spec, next_spec, sample_in, _resident((1, D_MODEL))] + weight_specs
    args = [xp, xp, xs, g.reshape(1, D_MODEL), wg, wu, wd]
    if gf is not None:
        in_specs.append(_resident((1, D_MODEL)))
        args.append(gf.reshape(1, D_MODEL))
    out_specs = [prompt_spec, sample_out]
    out_shape = [jax.ShapeDtypeStruct((n, D_MODEL), F32), jax.ShapeDtypeStruct(sample_shape, F32)]
    cast_specs = []
    for w in cast:
        rows, cols = w.shape
        nb = _cast_row_blocks(rows, n_tiles)
        cast_specs.append(pl.BlockSpec((rows // nb, cols), lambda i, nb=nb: (jnp.minimum(i, nb - 1), 0)))
        out_shape.append(jax.ShapeDtypeStruct(w.shape, BF16))
    outs = pl.pallas_call(
        functools.partial(_ffn_kernel, final_norm=gf is not None, n_cast=len(cast), f32_weights=f32_weights),
        grid=(n_tiles + 1,),
        in_specs=in_specs + cast_specs,
        out_specs=out_specs + cast_specs,
        out_shape=out_shape,
        scratch_shapes=[pltpu.VMEM((tm, D_MODEL), BF16), pltpu.VMEM((tm, FF_CHUNK), BF16),
                        pltpu.VMEM((tm, D_FF - FF_CHUNK), BF16)] + ([
                            pltpu.VMEM((D_MODEL, D_FF), BF16), pltpu.VMEM((D_MODEL, D_FF), BF16),
                            pltpu.VMEM((D_FF, D_MODEL), BF16),
                            pltpu.VMEM((STAGE_SLOTS, STAGE_BLOCK_BYTES // (4 * D_FF), D_FF), F32),
                            pltpu.VMEM((STAGE_SLOTS, STAGE_BLOCK_BYTES // (4 * D_MODEL), D_MODEL), F32),
                            pltpu.SemaphoreType.DMA((STAGE_SLOTS,)), pltpu.SemaphoreType.DMA((STAGE_SLOTS,))]
                        if f32_weights else []),
        compiler_params=pltpu.CompilerParams(dimension_semantics=("arbitrary",), vmem_limit_bytes=VMEM_LIMIT_BYTES),
        name="ffn_final" if gf is not None else "ffn_half",
    )(*args, *cast)
    return outs[0], outs[1], list(outs[2:])


def _first_half(shape):
    return (lax.broadcasted_iota(jnp.int32, shape, 1) & (HEAD_DIM - 1)) < HEAD_DIM // 2


def _sign_sin(sin):
    return jnp.where(_first_half(sin.shape), -sin, sin)


def _rope(xg, cos, sin_signed):
    swapped = jnp.where(_first_half(xg.shape), pltpu.roll(xg, LANES - HEAD_DIM // 2, 1),
                        pltpu.roll(xg, HEAD_DIM // 2, 1))
    return xg * cos + swapped * sin_signed


def _in_proj(x, gmix, w_in, cos, sin_signed, gvn_gain):
    h = _rms(x, gmix).astype(BF16)
    z = jnp.dot(h, w_in, preferred_element_type=F32)
    q = jnp.concatenate(
        [(_rope(z[:, LANES * i:LANES * (i + 1)], cos, sin_signed) * Q_SCALE).astype(BF16) for i in range(Q_W // LANES)],
        axis=1)
    k = _rope(z[:, K_OFF:V_OFF], cos, sin_signed)
    v = z[:, V_OFF:U_OFF]
    u = _gelu(z[:, U_OFF:GV_OFF])
    gvn = _rms(_gelu(z[:, GV_OFF:]), gvn_gain)
    return q, k, v, u, gvn


def _out_proj(x, ya, yg, ga, gg, w_out):
    cat = jnp.concatenate([_rms(ya, ga).astype(BF16), _rms(yg, gg).astype(BF16)], axis=1)
    return x + jnp.dot(cat, w_out, preferred_element_type=F32)


def _softmax_sink(s, sink):
    m = jnp.maximum(jnp.max(s, axis=1, keepdims=True), sink)
    p = jnp.exp2(s - m)
    return p, jnp.sum(p, axis=1, keepdims=True) + jnp.exp2(sink - m)


def _gelu(x):
    k = -2.0 * (2.0 / jnp.pi) ** 0.5 * LOG2E
    return x / (1.0 + jnp.exp2(x * (k + (0.044715 * k) * (x * x))))


def _mix_prompt_kernel(xa_ref, xc_ref, coff_ref, soff_ref, cbase_ref, sbase_ref, gmix_ref, win_ref, sinks_ref, gvg_ref,
                       wcat_ref, bias_ref, ga_ref, gg_ref, wout_ref, o_ref, ko_ref, vo_ref, gvo_ref, q_s, k_s, v_s, u_s,
                       gv_s, cat_s, *, tm, tiles_per_seq, n_tiles):
    s = pl.program_id(0)

    @pl.when(s == 0)
    def _():
        k_s[1, tm:tm + WINDOW, :] = jnp.zeros((WINDOW, KV_W), BF16)
        v_s[1, tm:tm + WINDOW, :] = jnp.zeros((WINDOW, KV_W), BF16)
        cat_s[1] = jnp.zeros((tm, D_MODEL), BF16)

    step = functools.partial(
        _mix_prompt_step, s, xa_ref, xc_ref, coff_ref, soff_ref, cbase_ref, sbase_ref, gmix_ref, win_ref, sinks_ref,
        gvg_ref, wcat_ref, bias_ref, ga_ref, gg_ref, wout_ref, o_ref, ko_ref, vo_ref, gvo_ref, q_s, k_s, v_s, u_s, gv_s,
        cat_s, tm=tm, tiles_per_seq=tiles_per_seq, n_tiles=n_tiles)
    last = n_tiles + 1
    pl.when(s == 0)(functools.partial(step, cur=0, stages="proj"))
    pl.when(s == last)(functools.partial(step, cur=last % 2, stages="out"))
    for parity in range(2):
        pl.when((s > 0) & (s < last) & (s % 2 == parity))(functools.partial(step, cur=parity, stages="all"))


def _mix_prompt_step(s, xa_ref, xc_ref, coff_ref, soff_ref, cbase_ref, sbase_ref, gmix_ref, win_ref, sinks_ref, gvg_ref,
                     wcat_ref, bias_ref, ga_ref, gg_ref, wout_ref, o_ref, ko_ref, vo_ref, gvo_ref, q_s, k_s, v_s, u_s, gv_s,
                     cat_s, *, tm, tiles_per_seq, n_tiles, cur, stages):
    oth = 1 - cur

    h = _rms(xa_ref[0], gmix_ref[...]).astype(BF16)
    tile_in_seq = jnp.minimum(s, n_tiles - 1) % tiles_per_seq
    cb = cbase_ref[pl.ds(tile_in_seq, 1), :]
    sb = sbase_ref[pl.ds(tile_in_seq, 1), :]
    cos = cb * coff_ref[...] - sb * soff_ref[...]
    sin = _sign_sin(sb * coff_ref[...] + cb * soff_ref[...])

    def proj_q():
        z = jnp.dot(h, win_ref[:, 0:Q_W], preferred_element_type=F32)
        for i in range(Q_W // LANES):
            q_s[cur, :, LANES * i:LANES * (i + 1)] = (
                _rope(z[:, LANES * i:LANES * (i + 1)], cos, sin) * Q_SCALE).astype(BF16)

    def proj_kv():
        z = jnp.dot(h, win_ref[:, K_OFF:U_OFF], preferred_element_type=F32)
        k = _rope(z[:, 0:KV_W], cos, sin)
        v = z[:, KV_W:]
        k_s[cur, 0:WINDOW, :] = k_s[oth, tm:tm + WINDOW, :]
        v_s[cur, 0:WINDOW, :] = v_s[oth, tm:tm + WINDOW, :]
        k_s[cur, WINDOW:, :] = k.astype(BF16)
        v_s[cur, WINDOW:, :] = v.astype(BF16)
        ko_ref[0] = k[tm - WINDOW:].T
        vo_ref[0] = v[tm - WINDOW:].T

    def proj_u():
        u_s[cur] = _gelu(jnp.dot(h, win_ref[:, U_OFF:GV_OFF], preferred_element_type=F32))

    def proj_gv():
        gvn = _rms(_gelu(jnp.dot(h, win_ref[:, GV_OFF:], preferred_element_type=F32)), gvg_ref[...])
        gv_s[cur] = gvn.astype(BF16)
        gvo_ref[0] = gvn[tm - CHUNK:].T

    def out_half(c):
        cols = slice(c * (D_MODEL // 2), (c + 1) * (D_MODEL // 2))

        def run():
            o_ref[0, :, cols] = xc_ref[0, :, cols] + jnp.dot(cat_s[cur], wout_ref[:, cols], preferred_element_type=F32)
        return run

    mid_fill = [[out_half(0)], [proj_q], [proj_kv, proj_u], [proj_gv]]
    end_fill = [[], [], [], [out_half(1)]]
    assert len(mid_fill) == len(end_fill) == tm // WINDOW
    if stages == "proj":
        for run in (proj_q, proj_kv, proj_u, proj_gv):
            run()
        return
    if stages == "out":
        out_half(0)()
        out_half(1)()
        return

    wrow = lax.broadcasted_iota(jnp.int32, (CHUNK, CHUNK), 0)
    wcol = lax.broadcasted_iota(jnp.int32, (CHUNK, CHUNK), 1)
    wtril = [jnp.where(wcol <= wrow, wcat_ref[hd], 0.0).astype(BF16) for hd in range(G_HEADS)]
    wmix = [jnp.concatenate(wtril[2 * p:2 * p + 2], axis=1) for p in range(G_HEADS // 2)]
    lane = lax.broadcasted_iota(jnp.int32, (CHUNK, LANES), 1)
    low_head = lane < HEAD_DIM

    qi = lax.broadcasted_iota(jnp.int32, (WINDOW, 2 * WINDOW), 0)
    sj = lax.broadcasted_iota(jnp.int32, (WINDOW, 2 * WINDOW), 1)
    dist = WINDOW + qi - sj
    band = (dist >= 0) & (dist < WINDOW)
    first_lo = jnp.where((s + tiles_per_seq - 1) % tiles_per_seq == 0, WINDOW, 0)

    for j in range(tm // WINDOW):
        rows = slice(j * WINDOW, (j + 1) * WINDOW)
        mask = band & (sj >= first_lo) if j == 0 else band
        qb = q_s[oth, rows, :]
        scores = []
        for kv in range(N_KV):
            qs = jnp.concatenate(
                [qb[:, HEAD_DIM * hd:HEAD_DIM * (hd + 1)] for hd in range(kv * GQA, (kv + 1) * GQA)], axis=0)
            kb = k_s[oth, j * WINDOW:(j + 2) * WINDOW, HEAD_DIM * kv:HEAD_DIM * (kv + 1)]
            scores.append(lax.dot_general(qs, kb, _NT, preferred_element_type=F32))
        for run in mid_fill[j]:
            run()
        outs, dens = [], []
        for kv in range(N_KV):
            vb = v_s[oth, j * WINDOW:(j + 2) * WINDOW, HEAD_DIM * kv:HEAD_DIM * (kv + 1)]
            ps = []
            for g in range(GQA):
                sg = jnp.where(mask, scores[kv][g * WINDOW:(g + 1) * WINDOW], -jnp.inf)
                p, den = _softmax_sink(sg, sinks_ref[kv * GQA + g] * LOG2E)
                ps.append(p.astype(BF16))
                dens.append(den)
            outs.append(jnp.dot(jnp.concatenate(ps, axis=0), vb, preferred_element_type=F32))
        mixed = []
        for p in range(G_HEADS // 2):
            r = gv_s[oth, rows, LANES * p:LANES * (p + 1)]
            zero = jnp.zeros_like(r)
            rhs = jnp.concatenate([jnp.where(low_head, r, zero), jnp.where(low_head, zero, r)], axis=0)
            mixed.append(jnp.dot(wmix[p], rhs, preferred_element_type=F32))
        for run in end_fill[j]:
            run()
        ya = [outs[hd // GQA][(hd % GQA) * WINDOW:(hd % GQA + 1) * WINDOW] / dens[hd] for hd in range(N_HEADS)]
        cat_s[oth, rows, 0:Q_W] = _rms(jnp.concatenate(ya, axis=1), ga_ref[...]).astype(BF16)
        yg = u_s[oth, rows, :] * (jnp.concatenate(mixed, axis=1) + bias_ref[...])
        cat_s[oth, rows, Q_W:] = _rms(yg, gg_ref[...]).astype(BF16)


def _mix_prompt(x, gmix, w_in, sinks, gvg, wcat, bias_full, ga, gg, w_out):
    b, s, _ = x.shape
    tm = TOKEN_TILE
    assert s % tm == 0 and tm % WINDOW == 0
    tiles_per_seq = s // tm
    n_tiles = b * tiles_per_seq
    cos_off, sin_off = _rope_tables(jnp.arange(tm, dtype=jnp.int32))
    cos_base, sin_base = _rope_tables(tm * jnp.arange(tiles_per_seq, dtype=jnp.int32))
    proj_tile = lambda i: jnp.minimum(i, n_tiles - 1)
    out_tile = lambda i: jnp.maximum(i - 2, 0)
    x_tiles = x.reshape(n_tiles, tm, D_MODEL)
    last = lambda width: pl.BlockSpec((1, width, WINDOW), lambda i: (proj_tile(i) // tiles_per_seq, 0, 0))
    out, ko, vo, gvo = pl.pallas_call(
        functools.partial(_mix_prompt_kernel, tm=tm, tiles_per_seq=tiles_per_seq, n_tiles=n_tiles),
        grid=(n_tiles + 2,),
        in_specs=[pl.BlockSpec((1, tm, D_MODEL), lambda i: (proj_tile(i), 0, 0)),
                  pl.BlockSpec((1, tm, D_MODEL), lambda i: (out_tile(i), 0, 0)),
                  _resident((tm, LANES)), _resident((tm, LANES)), _resident((tiles_per_seq, LANES)),
                  _resident((tiles_per_seq, LANES)), _resident((1, D_MODEL)), _resident((D_MODEL, D_IN)),
                  pl.BlockSpec(memory_space=pltpu.SMEM), _resident((1, D_GMLP)),
                  _resident((G_HEADS, CHUNK, CHUNK)), _resident((CHUNK, D_GMLP)), _resident((1, Q_W)),
                  _resident((1, D_GMLP)), _resident((D_MODEL, D_MODEL))],
        out_specs=[pl.BlockSpec((1, tm, D_MODEL), lambda i: (out_tile(i), 0, 0)), last(KV_W), last(KV_W),
                   last(D_GMLP)],
        out_shape=[jax.ShapeDtypeStruct((n_tiles, tm, D_MODEL), F32), jax.ShapeDtypeStruct((b, KV_W, WINDOW), F32),
                   jax.ShapeDtypeStruct((b, KV_W, WINDOW), F32), jax.ShapeDtypeStruct((b, D_GMLP, CHUNK), F32)],
        scratch_shapes=[pltpu.VMEM((2, tm, Q_W), BF16), pltpu.VMEM((2, tm + WINDOW, KV_W), BF16),
                        pltpu.VMEM((2, tm + WINDOW, KV_W), BF16), pltpu.VMEM((2, tm, D_GMLP), F32),
                        pltpu.VMEM((2, tm, D_GMLP), BF16), pltpu.VMEM((2, tm, D_MODEL), BF16)],
        compiler_params=pltpu.CompilerParams(dimension_semantics=("arbitrary",), vmem_limit_bytes=VMEM_LIMIT_BYTES),
        name="mix_prompt",
    )(x_tiles, x_tiles, cos_off, sin_off, cos_base, sin_base, gmix, w_in, sinks, gvg, wcat, bias_full, ga, gg, w_out)
    return out.reshape(b, s, D_MODEL), ko, vo, gvo


def _mix_sample_kernel(x_ref, ckt_ref, cvt_ref, cos_ref, sin_ref, gmix_ref, win_ref, sinks_ref, gvg_ref, coef_ref,
                       biasr_ref, ga_ref, gg_ref, wout_ref, o_ref, kot_ref, vot_ref, gvo_ref, q_s, k_s, v_s, ya_s,
                       yg_s, *, t_new, w_buf):
    step = pl.program_id(0)
    n_tok = x_ref.shape[0]
    step_seqs = SEQ_GROUP * GROUPS_PER_STEP
    step_rows = step_seqs * t_new
    grp_rows = SEQ_GROUP * t_new
    grp_keys = SEQ_GROUP * w_buf
    sub = 8

    @pl.when(step == 0)
    def _():
        tile = lambda tab: jnp.broadcast_to(tab[None], (n_tok // sub, sub, LANES)).reshape(n_tok, LANES)
        q, k, v, u, gvn = _in_proj(x_ref[...], gmix_ref[...], win_ref[...], tile(cos_ref[...]),
                                   tile(_sign_sin(sin_ref[...])), gvg_ref[...])
        q_s[...] = q
        k_s[...] = k
        v_s[...] = v
        gv_seq = gvn.reshape(n_tok // t_new, t_new, D_GMLP)
        for tok in range(t_new):
            gvo_ref[tok] = gv_seq[:, tok, :].T
        g3 = gvn.reshape(n_tok // sub, sub, D_GMLP)
        trow = lax.broadcasted_iota(jnp.int32, (1, sub, D_GMLP), 1) & (t_new - 1)
        mixed = biasr_ref[...][None] + coef_ref[0][None] * g3
        for d in range(1, t_new):
            shifted = jnp.where(trow >= d, pltpu.roll(g3, d, 1), 0.0)
            mixed = mixed + coef_ref[d][None] * shifted
        yg_s[...] = (u.reshape(n_tok // sub, sub, D_GMLP) * mixed).reshape(n_tok, D_GMLP)

    row0 = pl.multiple_of(step * step_rows, step_rows)
    q_step = q_s[pl.ds(row0, step_rows), :]
    kn = k_s[pl.ds(row0, step_rows), :]
    vn = v_s[pl.ds(row0, step_rows), :]

    knt = kn.T
    vnt = vn.T
    tail = lax.broadcasted_iota(jnp.int32, (KV_W, w_buf), 1) >= w_buf - t_new
    for b in range(step_seqs):
        shift = (w_buf - t_new - t_new * b) % w_buf
        kot_ref[b] = jnp.where(tail, pltpu.roll(knt, shift, 1) if shift else knt,
                               pltpu.roll(ckt_ref[b], w_buf - t_new, 1))
        vot_ref[b] = jnp.where(tail, pltpu.roll(vnt, shift, 1) if shift else vnt,
                               pltpu.roll(cvt_ref[b], w_buf - t_new, 1))

    knb = kn.astype(BF16)
    vnb = vn.astype(BF16)

    n_rows = GQA * grp_rows
    shift_t = t_new.bit_length() - 1
    shift_w = w_buf.bit_length() - 1
    r = lax.broadcasted_iota(jnp.int32, (n_rows, grp_keys), 0)
    c = lax.broadcasted_iota(jnp.int32, (n_rows, grp_keys), 1)
    mask_c = ((c >> shift_w) == ((r & (grp_rows - 1)) >> shift_t)) & ((c & (w_buf - 1)) > (r & (t_new - 1)))
    r2 = lax.broadcasted_iota(jnp.int32, (n_rows, grp_rows), 0)
    c2 = lax.broadcasted_iota(jnp.int32, (n_rows, grp_rows), 1)
    mask_n = ((c2 >> shift_t) == ((r2 & (grp_rows - 1)) >> shift_t)) & ((c2 & (t_new - 1)) <= (r2 & (t_new - 1)))
    row_head = lax.broadcasted_iota(jnp.int32, (n_rows, 1), 0) >> (grp_rows.bit_length() - 1)

    for grp in range(GROUPS_PER_STEP):
        rows = slice(grp * grp_rows, (grp + 1) * grp_rows)
        seqs = range(grp * SEQ_GROUP, (grp + 1) * SEQ_GROUP)
        for kv in range(N_KV):
            heads = [kv * GQA + i for i in range(GQA)]
            lanes = slice(HEAD_DIM * kv, HEAD_DIM * (kv + 1))
            kt = jnp.concatenate([ckt_ref[b, lanes, :] for b in seqs], axis=1).astype(BF16)
            vt = jnp.concatenate([cvt_ref[b, lanes, :] for b in seqs], axis=1).astype(BF16)
            qs = jnp.concatenate([q_step[rows, HEAD_DIM * hd:HEAD_DIM * (hd + 1)] for hd in heads], axis=0)
            s_c = jnp.where(mask_c, jnp.dot(qs, kt, preferred_element_type=F32), -jnp.inf)
            s_n = jnp.where(mask_n, lax.dot_general(qs, knb[rows, lanes], _NT, preferred_element_type=F32), -jnp.inf)
            sink = jnp.full((n_rows, 1), sinks_ref[heads[0]] * LOG2E, F32)
            for i in range(1, GQA):
                sink = jnp.where(row_head == i, sinks_ref[heads[i]] * LOG2E, sink)
            m = jnp.maximum(jnp.maximum(jnp.max(s_c, axis=1, keepdims=True), jnp.max(s_n, axis=1, keepdims=True)),
                            sink)
            p_c = jnp.exp2(s_c - m)
            p_n = jnp.exp2(s_n - m)
            den = jnp.sum(p_c, axis=1, keepdims=True) + jnp.sum(p_n, axis=1, keepdims=True) + jnp.exp2(sink - m)
            o = (lax.dot_general(p_c.astype(BF16), vt, _NT, preferred_element_type=F32)
                 + jnp.dot(p_n.astype(BF16), vnb[rows, lanes], preferred_element_type=F32)) / den
            for i, hd in enumerate(heads):
                ya_s[pl.ds(row0 + grp * grp_rows, grp_rows), HEAD_DIM * hd:HEAD_DIM * (hd + 1)] = (
                    o[i * grp_rows:(i + 1) * grp_rows])

    @pl.when(step == pl.num_programs(0) - 1)
    def _():
        o_ref[...] = _out_proj(x_ref[...], ya_s[...], yg_s[...], ga_ref[...], gg_ref[...], wout_ref[...])


def _mix_sample(x, cache_kt, cache_vt, cos, sin, gmix, w_in, sinks, gvg, coef, bias_rows, ga, gg, w_out, t_new):
    n_tok = x.shape[0]
    n_seq, _, w_buf = cache_kt.shape
    step_seqs = SEQ_GROUP * GROUPS_PER_STEP
    assert n_seq % step_seqs == 0 and n_tok == n_seq * t_new
    assert t_new & (t_new - 1) == 0 and w_buf & (w_buf - 1) == 0 and 8 % t_new == 0
    assert step_seqs * t_new == w_buf == LANES
    cache_spec = pl.BlockSpec((step_seqs, KV_W, w_buf), lambda i: (i, 0, 0))
    return pl.pallas_call(
        functools.partial(_mix_sample_kernel, t_new=t_new, w_buf=w_buf),
        grid=(n_seq // step_seqs,),
        in_specs=[_resident((n_tok, D_MODEL)), cache_spec, cache_spec, _resident((8, LANES)),
                  _resident((8, LANES)), _resident((1, D_MODEL)), _resident((D_MODEL, D_IN)),
                  pl.BlockSpec(memory_space=pltpu.SMEM), _resident((1, D_GMLP)), _resident((t_new, 8, D_GMLP)),
                  _resident((8, D_GMLP)), _resident((1, Q_W)), _resident((1, D_GMLP)),
                  _resident((D_MODEL, D_MODEL))],
        out_specs=[pl.BlockSpec((n_tok, D_MODEL), lambda i: (0, 0)), cache_spec, cache_spec,
                   pl.BlockSpec((t_new, D_GMLP, n_seq), lambda i: (0, 0, 0))],
        out_shape=[jax.ShapeDtypeStruct((n_tok, D_MODEL), F32), jax.ShapeDtypeStruct(cache_kt.shape, F32),
                   jax.ShapeDtypeStruct(cache_vt.shape, F32), jax.ShapeDtypeStruct((t_new, D_GMLP, n_seq), F32)],
        scratch_shapes=[pltpu.VMEM((n_tok, Q_W), BF16), pltpu.VMEM((n_tok, KV_W), F32), pltpu.VMEM((n_tok, KV_W), F32),
                        pltpu.VMEM((n_tok, Q_W), F32), pltpu.VMEM((n_tok, D_GMLP), F32)],
        compiler_params=pltpu.CompilerParams(dimension_semantics=("arbitrary",), vmem_limit_bytes=VMEM_LIMIT_BYTES),
        name="mix_sample",
    )(x, cache_kt, cache_vt, cos, sin, gmix, w_in, sinks, gvg, coef, bias_rows, ga, gg, w_out)


def _gmlp_tables_kernel(ws_ref, bs_ref, bias_ref, coef_ref, biasr_ref, *, t_new):
    sub = 8
    expand = lambda col: jnp.broadcast_to(col, (col.shape[0], D_GMLP // G_HEADS))
    row = lax.broadcasted_iota(jnp.int32, (CHUNK, CHUNK), 0)
    lane = lax.broadcasted_iota(jnp.int32, (CHUNK, CHUNK), 1)
    bias = jnp.concatenate(
        [expand(jnp.sum(jnp.where(lane == row, bs_ref[hd:hd + 1, :], 0.0), axis=1, keepdims=True))
         for hd in range(G_HEADS)], axis=1)
    bias_ref[...] = bias
    biasr_ref[...] = jnp.concatenate([bias[0:t_new]] * (sub // t_new), axis=0)
    t = lax.broadcasted_iota(jnp.int32, (sub, CHUNK), 0) & (t_new - 1)
    s = lax.broadcasted_iota(jnp.int32, (sub, CHUNK), 1)
    for d in range(t_new):
        cols = []
        for hd in range(G_HEADS):
            w_rows = jnp.concatenate([ws_ref[hd, 0:t_new, :]] * (sub // t_new), axis=0)
            cols.append(expand(jnp.sum(jnp.where(s == t - d, w_rows, 0.0), axis=1, keepdims=True)))
        coef_ref[d] = jnp.concatenate(cols, axis=1)


def _gmlp_tables(ws, bs, t_new):
    assert 8 % t_new == 0
    return pl.pallas_call(
        functools.partial(_gmlp_tables_kernel, t_new=t_new),
        out_shape=[jax.ShapeDtypeStruct((CHUNK, D_GMLP), F32), jax.ShapeDtypeStruct((t_new, 8, D_GMLP), F32),
                   jax.ShapeDtypeStruct((8, D_GMLP), F32)],
        name="gmlp_tables",
    )(ws, bs)


def _rope_tables(pos):
    inv_freq = ROPE_THETA ** (-jnp.arange(0, HEAD_DIM, 2, dtype=F32) / HEAD_DIM)
    ang = pos.astype(F32)[:, None] * jnp.tile(inv_freq, 2 * LANES // HEAD_DIM)[None, :]
    return jnp.cos(ang), jnp.sin(ang)


def kernel(x_prompt, x_sample, cache_k_win, cache_v_win, norm_ffn1, ffn1_gate, ffn1_up, ffn1_down, norm_mix, w_in,
           attn_sinks, gmlp_v_norm, gmlp_w_s, gmlp_b_s, norm_attn_out, norm_gmlp_out, w_out, norm_ffn2, ffn2_gate,
           ffn2_up, ffn2_down, norm_final):
    depth = norm_ffn1.shape[0]
    b, s, _ = x_prompt.shape
    bd, t_new, _ = x_sample.shape
    w_buf = cache_k_win.shape[2]

    cos_s, sin_s = _rope_tables(PAST_LEN + jnp.arange(t_new, dtype=jnp.int32))
    cos_s, sin_s = jnp.tile(cos_s, (8 // t_new, 1)), jnp.tile(sin_s, (8 // t_new, 1))

    hp = x_prompt.reshape(b * s, D_MODEL)
    hs = x_sample.reshape(bd * t_new, D_MODEL)
    outs = [[] for _ in range(6)]
    ffn1_w = [ffn1_gate[0], ffn1_up[0], ffn1_down[0]]
    for l in range(depth):
        last = l == depth - 1
        row = lambda a: a[l].reshape(1, -1)

        wcat = gmlp_w_s[l]
        bias_full, coef, bias_rows = _gmlp_tables(gmlp_w_s[l], gmlp_b_s[l], t_new)

        hp, hs, (w_in_b, w_out_b, wg2, wu2, wd2) = _ffn_half(
            hp, hs, norm_ffn1[l], *ffn1_w, cast=(w_in[l], w_out[l], ffn2_gate[l], ffn2_up[l], ffn2_down[l]))

        hp, kpt, vpt, gvpt = _mix_prompt(hp.reshape(b, s, D_MODEL), row(norm_mix), w_in_b, attn_sinks[l],
                                         row(gmlp_v_norm), wcat, bias_full, row(norm_attn_out), row(norm_gmlp_out),
                                         w_out_b)
        to_t = lambda c: c.transpose(0, 2, 3, 1).reshape(bd, KV_W, w_buf)
        hs, kst, vst, gvs = _mix_sample(hs, to_t(cache_k_win[l]), to_t(cache_v_win[l]), cos_s, sin_s, row(norm_mix),
                                        w_in_b, attn_sinks[l], row(gmlp_v_norm), coef, bias_rows, row(norm_attn_out),
                                        row(norm_gmlp_out), w_out_b, t_new)
        from_t = lambda c: c.reshape(bd, N_KV, HEAD_DIM, w_buf).transpose(0, 3, 1, 2)

        next_ffn1 = () if last else (ffn1_gate[l + 1], ffn1_up[l + 1], ffn1_down[l + 1])
        hp, hs, ffn1_w = _ffn_half(hp.reshape(b * s, D_MODEL), hs, norm_ffn2[l], wg2, wu2, wd2,
                                   gf=norm_final if last else None, cast=next_ffn1,
                                   sample_out_shape=(bd, t_new, D_MODEL) if last else None)

        outs[0].append(kpt.reshape(b, N_KV, HEAD_DIM, WINDOW).transpose(0, 3, 1, 2))
        outs[1].append(vpt.reshape(b, N_KV, HEAD_DIM, WINDOW).transpose(0, 3, 1, 2))
        outs[2].append(from_t(kst))
        outs[3].append(from_t(vst))
        outs[4].append(gvpt.reshape(b, G_HEADS, D_GMLP // G_HEADS, CHUNK).transpose(0, 3, 1, 2))
        outs[5].append(gvs.reshape(t_new, G_HEADS, D_GMLP // G_HEADS, bd).transpose(3, 0, 1, 2))

    return (hp.reshape(b, s, D_MODEL), hs) + tuple(jnp.stack(o) for o in outs)
```

```python
import functools

import jax
import jax.numpy as jnp
from jax import lax
from jax.experimental import pallas as pl
from jax.experimental.pallas import tpu as pltpu

F32 = jnp.float32
BF16 = jnp.bfloat16

D_MODEL = 1024
D_FF = 2816
HEAD_DIM = 64
N_HEADS = 8
N_KV = 2
GQA = N_HEADS // N_KV
WINDOW = 128
CHUNK = 128
G_HEADS = 8
Q_W = N_HEADS * HEAD_DIM
KV_W = N_KV * HEAD_DIM
D_GMLP = 512
D_IN = Q_W + 2 * KV_W + 2 * D_GMLP
K_OFF = Q_W
V_OFF = K_OFF + KV_W
U_OFF = V_OFF + KV_W
GV_OFF = U_OFF + D_GMLP
ROPE_THETA = 10000.0
PAST_LEN = 16384
EPS = 1e-6
LOG2E = 1.4426950408889634
Q_SCALE = HEAD_DIM ** -0.5 * LOG2E
LANES = 128
BF16_SUBLANES = 16

TOKEN_TILE = 512
FF_CHUNK = 256
SEQ_GROUP = 8
GROUPS_PER_STEP = 4
VMEM_LIMIT_BYTES = 56 * 1024 * 1024

_NT = (((1,), (1,)), ((), ()))


def _rms(x, g):
    ms = jnp.mean(x * x, axis=-1, keepdims=True)
    return (x * lax.rsqrt(ms + EPS)) * g


def _resident(shape):
    zeros = (0,) * len(shape)
    return pl.BlockSpec(shape, lambda *_: zeros, pipeline_mode=pl.Buffered(1))


def _silu_gate(h, wg, wu):
    gate = jnp.dot(h, wg, preferred_element_type=F32)
    up = jnp.dot(h, wu, preferred_element_type=F32)
    return (gate * jax.nn.sigmoid(gate) * up).astype(BF16)


def _ffn_first_kernel(*refs, n_cast):
    refs = iter(refs)
    xp_ref, xn_ref, xs_ref, g_ref, wg_hbm, wu_hbm, wd_hbm = (next(refs) for _ in range(7))
    cast_in = [next(refs) for _ in range(n_cast)]
    yp_ref, ys_ref = next(refs), next(refs)
    cast_out = [next(refs) for _ in range(n_cast)]
    h_ref, act0_ref, act_ref, wg_ref, wu_ref, wd_ref, col_stage, row_stage, sem = (next(refs) for _ in range(9))
    n_chunks = D_FF // FF_CHUNK
    gate_up = lambda h, c: _silu_gate(h, wg_ref[c], wu_ref[c])

    @pl.when(pl.program_id(0) == 0)
    def _():
        def fetch(c, slot):
            span = pl.ds(pl.multiple_of(c * FF_CHUNK, FF_CHUNK), FF_CHUNK)
            return (pltpu.make_async_copy(wg_hbm.at[:, span], col_stage.at[slot, 0], sem.at[slot, 0]),
                    pltpu.make_async_copy(wu_hbm.at[:, span], col_stage.at[slot, 1], sem.at[slot, 1]),
                    pltpu.make_async_copy(wd_hbm.at[span, :], row_stage.at[slot], sem.at[slot, 2]))

        h_ref[...] = _rms(xs_ref[...], g_ref[...]).astype(BF16)
        ys_ref[...] = jnp.zeros(ys_ref.shape, F32)
        for copy in fetch(0, 0):
            copy.start()

        def body(c, carry):
            slot = c % 2

            @pl.when(c + 1 < n_chunks)
            def _():
                for copy in fetch(c + 1, 1 - slot):
                    copy.start()

            for copy in fetch(c, slot):
                copy.wait()
            rows = pl.ds(pl.multiple_of(c * FF_CHUNK, FF_CHUNK), FF_CHUNK)
            wg_ref[c] = col_stage[slot, 0].astype(BF16)
            wu_ref[c] = col_stage[slot, 1].astype(BF16)
            wd_ref[rows, :] = row_stage[slot].astype(BF16)
            ys_ref[...] += jnp.dot(gate_up(h_ref[...], c), wd_ref[rows, :], preferred_element_type=F32)
            return carry

        lax.fori_loop(0, n_chunks, body, 0)
        ys_ref[...] = xs_ref[...] + 0.5 * ys_ref[...]
        h0 = _rms(xp_ref[...], g_ref[...]).astype(BF16)
        h_ref[...] = h0
        act0_ref[...] = gate_up(h0, 0)

    for c in range(1, n_chunks):
        act_ref[:, (c - 1) * FF_CHUNK:c * FF_CHUNK] = gate_up(h_ref[...], c)
    down0 = jnp.dot(act0_ref[...], wd_ref[0:FF_CHUNK, :], preferred_element_type=F32)
    hn = _rms(xn_ref[...], g_ref[...]).astype(BF16)
    h_ref[...] = hn
    act0_ref[...] = gate_up(hn, 0)
    yp_ref[...] = xp_ref[...] + 0.5 * (down0 + jnp.dot(act_ref[...], wd_ref[FF_CHUNK:, :], preferred_element_type=F32))

    for src, dst in zip(cast_in, cast_out):
        dst[...] = src[...].astype(BF16)


def _ffn_kernel(*refs, final_norm, n_cast):
    refs = iter(refs)
    xp_ref, xn_ref, xs_ref, g_ref, wg_ref, wu_ref, wd_ref = (next(refs) for _ in range(7))
    gf_ref = next(refs) if final_norm else None
    cast_in = [next(refs) for _ in range(n_cast)]
    yp_ref, ys_ref = next(refs), next(refs)
    cast_out = [next(refs) for _ in range(n_cast)]
    h_ref, act0_ref, act_ref = next(refs), next(refs), next(refs)

    def gate_up(h, c):
        sl = slice(c * FF_CHUNK, (c + 1) * FF_CHUNK)
        return _silu_gate(h, wg_ref[:, sl], wu_ref[:, sl])

    on_sample = pl.program_id(0) == 0

    @pl.when(on_sample)
    def _():
        h0 = _rms(xs_ref[...], g_ref[...]).astype(BF16)
        h_ref[...] = h0
        act0_ref[...] = gate_up(h0, 0)

    for c in range(1, D_FF // FF_CHUNK):
        act_ref[:, (c - 1) * FF_CHUNK:c * FF_CHUNK] = gate_up(h_ref[...], c)
    down0 = jnp.dot(act0_ref[...], wd_ref[0:FF_CHUNK, :], preferred_element_type=F32)
    hn = _rms(xn_ref[...], g_ref[...]).astype(BF16)
    h_ref[...] = hn
    act0_ref[...] = gate_up(hn, 0)
    x = jnp.where(on_sample, xs_ref[...], xp_ref[...])
    y = x + 0.5 * (down0 + jnp.dot(act_ref[...], wd_ref[FF_CHUNK:, :], preferred_element_type=F32))
    yp_ref[...] = _rms(y, gf_ref[...]) if final_norm else y

    @pl.when(on_sample)
    def _():
        ys_ref[...] = yp_ref[...].reshape(ys_ref.shape)

    for src, dst in zip(cast_in, cast_out):
        dst[...] = src[...].astype(BF16)


def _cast_row_blocks(rows, n_steps):
    return max(d for d in range(1, n_steps + 1) if rows % d == 0 and (rows // d) % BF16_SUBLANES == 0)


def _cast_specs(mats, n_steps):
    specs = []
    for w in mats:
        rows, cols = w.shape
        nb = _cast_row_blocks(rows, n_steps)
        specs.append(pl.BlockSpec((rows // nb, cols), lambda i, nb=nb: (jnp.minimum(i, nb - 1), 0)))
    return specs


def _ffn_half(xp, xs, g, wg, wu, wd, gf=None, cast=(), sample_out_shape=None):
    tm = TOKEN_TILE
    n = xp.shape[0]
    assert n % tm == 0 and xs.shape[0] == tm
    n_tiles = n // tm
    prompt_spec = pl.BlockSpec((tm, D_MODEL), lambda i: (jnp.maximum(i - 1, 0), 0))
    next_spec = pl.BlockSpec((tm, D_MODEL), lambda i: (jnp.minimum(i, n_tiles - 1), 0))
    sample_in = _resident((tm, D_MODEL))
    sample_shape = sample_out_shape or (tm, D_MODEL)
    sample_out = pl.BlockSpec(sample_shape, lambda i: (0,) * len(sample_shape))
    in_specs = [prompt_spec, next_spec, sample_in, _resident((1, D_MODEL)), _resident((D_MODEL, D_FF)),
                _resident((D_MODEL, D_FF)), _resident((D_FF, D_MODEL))]
    args = [xp, xp, xs, g.reshape(1, D_MODEL), wg, wu, wd]
    if gf is not None:
        in_specs.append(_resident((1, D_MODEL)))
        args.append(gf.reshape(1, D_MODEL))
    out_specs = [prompt_spec, sample_out]
    out_shape = [jax.ShapeDtypeStruct((n, D_MODEL), F32), jax.ShapeDtypeStruct(sample_shape, F32)]
    cast_specs = _cast_specs(cast, n_tiles)
    out_shape += [jax.ShapeDtypeStruct(w.shape, BF16) for w in cast]
    outs = pl.pallas_call(
        functools.partial(_ffn_kernel, final_norm=gf is not None, n_cast=len(cast)),
        grid=(n_tiles + 1,),
        in_specs=in_specs + cast_specs,
        out_specs=out_specs + cast_specs,
        out_shape=out_shape,
        scratch_shapes=[pltpu.VMEM((tm, D_MODEL), BF16), pltpu.VMEM((tm, FF_CHUNK), BF16),
                        pltpu.VMEM((tm, D_FF - FF_CHUNK), BF16)],
        compiler_params=pltpu.CompilerParams(dimension_semantics=("arbitrary",), vmem_limit_bytes=VMEM_LIMIT_BYTES),
        name="ffn_final" if gf is not None else "ffn_half",
    )(*args, *cast)
    return outs[0], outs[1], list(outs[2:])


def _ffn_first(xp, xs, g, wg, wu, wd, cast=()):
    tm = TOKEN_TILE
    n = xp.shape[0]
    assert n % tm == 0 and xs.shape[0] == tm and D_FF % FF_CHUNK == 0
    n_tiles = n // tm
    n_chunks = D_FF // FF_CHUNK
    prompt_spec = pl.BlockSpec((tm, D_MODEL), lambda i: (i, 0))
    next_spec = pl.BlockSpec((tm, D_MODEL), lambda i: (jnp.minimum(i + 1, n_tiles - 1), 0))
    hbm = pl.BlockSpec(memory_space=pl.ANY)
    cast_specs = _cast_specs(cast, n_tiles)
    outs = pl.pallas_call(
        functools.partial(_ffn_first_kernel, n_cast=len(cast)),
        grid=(n_tiles,),
        in_specs=[prompt_spec, next_spec, _resident((tm, D_MODEL)), _resident((1, D_MODEL)), hbm, hbm, hbm] + cast_specs,
        out_specs=[prompt_spec, pl.BlockSpec((tm, D_MODEL), lambda i: (0, 0))] + cast_specs,
        out_shape=[jax.ShapeDtypeStruct((n, D_MODEL), F32), jax.ShapeDtypeStruct((tm, D_MODEL), F32)]
        + [jax.ShapeDtypeStruct(w.shape, BF16) for w in cast],
        scratch_shapes=[pltpu.VMEM((tm, D_MODEL), BF16), pltpu.VMEM((tm, FF_CHUNK), BF16),
                        pltpu.VMEM((tm, D_FF - FF_CHUNK), BF16),
                        pltpu.VMEM((n_chunks, D_MODEL, FF_CHUNK), BF16), pltpu.VMEM((n_chunks, D_MODEL, FF_CHUNK), BF16),
                        pltpu.VMEM((D_FF, D_MODEL), BF16), pltpu.VMEM((2, 2, D_MODEL, FF_CHUNK), F32),
                        pltpu.VMEM((2, FF_CHUNK, D_MODEL), F32), pltpu.SemaphoreType.DMA((2, 3))],
        compiler_params=pltpu.CompilerParams(dimension_semantics=("arbitrary",), vmem_limit_bytes=VMEM_LIMIT_BYTES),
        name="ffn_first",
    )(xp, xp, xs, g.reshape(1, D_MODEL), wg, wu, wd, *cast)
    return outs[0], outs[1], list(outs[2:])


def _first_half(shape):
    return (lax.broadcasted_iota(jnp.int32, shape, 1) & (HEAD_DIM - 1)) < HEAD_DIM // 2


def _sign_sin(sin):
    return jnp.where(_first_half(sin.shape), -sin, sin)


def _rope(xg, cos, sin_signed):
    swapped = jnp.where(_first_half(xg.shape), pltpu.roll(xg, LANES - HEAD_DIM // 2, 1),
                        pltpu.roll(xg, HEAD_DIM // 2, 1))
    return xg * cos + swapped * sin_signed


def _in_proj(x, gmix, w_in, cos, sin_signed, gvn_gain):
    h = _rms(x, gmix).astype(BF16)
    z = jnp.dot(h, w_in, preferred_element_type=F32)
    q = jnp.concatenate(
        [(_rope(z[:, LANES * i:LANES * (i + 1)], cos, sin_signed) * Q_SCALE).astype(BF16) for i in range(Q_W // LANES)],
        axis=1)
    k = _rope(z[:, K_OFF:V_OFF], cos, sin_signed)
    v = z[:, V_OFF:U_OFF]
    u = _gelu(z[:, U_OFF:GV_OFF])
    gvn = _rms(_gelu(z[:, GV_OFF:]), gvn_gain)
    return q, k, v, u, gvn


def _out_proj(x, ya, yg, ga, gg, w_out):
    cat = jnp.concatenate([_rms(ya, ga).astype(BF16), _rms(yg, gg).astype(BF16)], axis=1)
    return x + jnp.dot(cat, w_out, preferred_element_type=F32)


def _softmax_sink(s, sink):
    m = jnp.maximum(jnp.max(s, axis=1, keepdims=True), sink)
    p = jnp.exp2(s - m)
    return p, jnp.sum(p, axis=1, keepdims=True) + jnp.exp2(sink - m)


def _gelu(x):
    k = -2.0 * (2.0 / jnp.pi) ** 0.5 * LOG2E
    return x / (1.0 + jnp.exp2(x * (k + (0.044715 * k) * (x * x))))


def _mix_prompt_kernel(xa_ref, xc_ref, coff_ref, soff_ref, cbase_ref, sbase_ref, gmix_ref, win_ref, sinks_ref, gvg_ref,
                       wcat_ref, bias_ref, ga_ref, gg_ref, wout_ref, o_ref, ko_ref, vo_ref, gvo_ref, q_s, k_s, v_s, u_s,
                       gv_s, cat_s, *, tm, tiles_per_seq, n_tiles):
    s = pl.program_id(0)

    @pl.when(s == 0)
    def _():
        k_s[1, tm:tm + WINDOW, :] = jnp.zeros((WINDOW, KV_W), BF16)
        v_s[1, tm:tm + WINDOW, :] = jnp.zeros((WINDOW, KV_W), BF16)
        cat_s[1] = jnp.zeros((tm, D_MODEL), BF16)

    step = functools.partial(
        _mix_prompt_step, s, xa_ref, xc_ref, coff_ref, soff_ref, cbase_ref, sbase_ref, gmix_ref, win_ref, sinks_ref,
        gvg_ref, wcat_ref, bias_ref, ga_ref, gg_ref, wout_ref, o_ref, ko_ref, vo_ref, gvo_ref, q_s, k_s, v_s, u_s, gv_s,
        cat_s, tm=tm, tiles_per_seq=tiles_per_seq, n_tiles=n_tiles)
    last = n_tiles + 1
    pl.when(s == 0)(functools.partial(step, cur=0, stages="proj"))
    pl.when(s == last)(functools.partial(step, cur=last % 2, stages="out"))
    for parity in range(2):
        pl.when((s > 0) & (s < last) & (s % 2 == parity))(functools.partial(step, cur=parity, stages="all"))


def _mix_prompt_step(s, xa_ref, xc_ref, coff_ref, soff_ref, cbase_ref, sbase_ref, gmix_ref, win_ref, sinks_ref, gvg_ref,
                     wcat_ref, bias_ref, ga_ref, gg_ref, wout_ref, o_ref, ko_ref, vo_ref, gvo_ref, q_s, k_s, v_s, u_s, gv_s,
                     cat_s, *, tm, tiles_per_seq, n_tiles, cur, stages):
    oth = 1 - cur

    h = _rms(xa_ref[0], gmix_ref[...]).astype(BF16)
    tile_in_seq = jnp.minimum(s, n_tiles - 1) % tiles_per_seq
    cb = cbase_ref[pl.ds(tile_in_seq, 1), :]
    sb = sbase_ref[pl.ds(tile_in_seq, 1), :]
    cos = cb * coff_ref[...] - sb * soff_ref[...]
    sin = _sign_sin(sb * coff_ref[...] + cb * soff_ref[...])

    def proj_q():
        z = jnp.dot(h, win_ref[:, 0:Q_W], preferred_element_type=F32)
        for i in range(Q_W // LANES):
            q_s[cur, :, LANES * i:LANES * (i + 1)] = (
                _rope(z[:, LANES * i:LANES * (i + 1)], cos, sin) * Q_SCALE).astype(BF16)

    def proj_kv():
        z = jnp.dot(h, win_ref[:, K_OFF:U_OFF], preferred_element_type=F32)
        k = _rope(z[:, 0:KV_W], cos, sin)
        v = z[:, KV_W:]
        k_s[cur, 0:WINDOW, :] = k_s[oth, tm:tm + WINDOW, :]
        v_s[cur, 0:WINDOW, :] = v_s[oth, tm:tm + WINDOW, :]
        k_s[cur, WINDOW:, :] = k.astype(BF16)
        v_s[cur, WINDOW:, :] = v.astype(BF16)
        ko_ref[0] = k[tm - WINDOW:].T
        vo_ref[0] = v[tm - WINDOW:].T

    def proj_u():
        u_s[cur] = _gelu(jnp.dot(h, win_ref[:, U_OFF:GV_OFF], preferred_element_type=F32))

    def proj_gv():
        gvn = _rms(_gelu(jnp.dot(h, win_ref[:, GV_OFF:], preferred_element_type=F32)), gvg_ref[...])
        gv_s[cur] = gvn.astype(BF16)
        gvo_ref[0] = gvn[tm - CHUNK:].T

    def out_half(c):
        cols = slice(c * (D_MODEL // 2), (c + 1) * (D_MODEL // 2))

        def run():
            o_ref[0, :, cols] = xc_ref[0, :, cols] + jnp.dot(cat_s[cur], wout_ref[:, cols], preferred_element_type=F32)
        return run

    mid_fill = [[out_half(0)], [proj_q], [proj_kv, proj_u], [proj_gv]]
    end_fill = [[], [], [], [out_half(1)]]
    assert len(mid_fill) == len(end_fill) == tm // WINDOW
    if stages == "proj":
        for run in (proj_q, proj_kv, proj_u, proj_gv):
            run()
        return
    if stages == "out":
        out_half(0)()
        out_half(1)()
        return

    wrow = lax.broadcasted_iota(jnp.int32, (CHUNK, CHUNK), 0)
    wcol = lax.broadcasted_iota(jnp.int32, (CHUNK, CHUNK), 1)
    wtril = [jnp.where(wcol <= wrow, wcat_ref[hd], 0.0).astype(BF16) for hd in range(G_HEADS)]
    wmix = [jnp.concatenate(wtril[2 * p:2 * p + 2], axis=1) for p in range(G_HEADS // 2)]
    lane = lax.broadcasted_iota(jnp.int32, (CHUNK, LANES), 1)
    low_head = lane < HEAD_DIM

    qi = lax.broadcasted_iota(jnp.int32, (WINDOW, 2 * WINDOW), 0)
    sj = lax.broadcasted_iota(jnp.int32, (WINDOW, 2 * WINDOW), 1)
    dist = WINDOW + qi - sj
    band = (dist >= 0) & (dist < WINDOW)
    first_lo = jnp.where((s + tiles_per_seq - 1) % tiles_per_seq == 0, WINDOW, 0)

    for j in range(tm // WINDOW):
        rows = slice(j * WINDOW, (j + 1) * WINDOW)
        mask = band & (sj >= first_lo) if j == 0 else band
        qb = q_s[oth, rows, :]
        scores = []
        for kv in range(N_KV):
            qs = jnp.concatenate(
                [qb[:, HEAD_DIM * hd:HEAD_DIM * (hd + 1)] for hd in range(kv * GQA, (kv + 1) * GQA)], axis=0)
            kb = k_s[oth, j * WINDOW:(j + 2) * WINDOW, HEAD_DIM * kv:HEAD_DIM * (kv + 1)]
            scores.append(lax.dot_general(qs, kb, _NT, preferred_element_type=F32))
        for run in mid_fill[j]:
            run()
        outs, dens = [], []
        for kv in range(N_KV):
            vb = v_s[oth, j * WINDOW:(j + 2) * WINDOW, HEAD_DIM * kv:HEAD_DIM * (kv + 1)]
            ps = []
            for g in range(GQA):
                sg = jnp.where(mask, scores[kv][g * WINDOW:(g + 1) * WINDOW], -jnp.inf)
                p, den = _softmax_sink(sg, sinks_ref[kv * GQA + g] * LOG2E)
                ps.append(p.astype(BF16))
                dens.append(den)
            outs.append(jnp.dot(jnp.concatenate(ps, axis=0), vb, preferred_element_type=F32))
        mixed = []
        for p in range(G_HEADS // 2):
            r = gv_s[oth, rows, LANES * p:LANES * (p + 1)]
            zero = jnp.zeros_like(r)
            rhs = jnp.concatenate([jnp.where(low_head, r, zero), jnp.where(low_head, zero, r)], axis=0)
            mixed.append(jnp.dot(wmix[p], rhs, preferred_element_type=F32))
        for run in end_fill[j]:
            run()
        ya = [outs[hd // GQA][(hd % GQA) * WINDOW:(hd % GQA + 1) * WINDOW] / dens[hd] for hd in range(N_HEADS)]
        cat_s[oth, rows, 0:Q_W] = _rms(jnp.concatenate(ya, axis=1), ga_ref[...]).astype(BF16)
        yg = u_s[oth, rows, :] * (jnp.concatenate(mixed, axis=1) + bias_ref[...])
        cat_s[oth, rows, Q_W:] = _rms(yg, gg_ref[...]).astype(BF16)


def _mix_prompt(x, gmix, w_in, sinks, gvg, wcat, bias_full, ga, gg, w_out):
    b, s, _ = x.shape
    tm = TOKEN_TILE
    assert s % tm == 0 and tm % WINDOW == 0
    tiles_per_seq = s // tm
    n_tiles = b * tiles_per_seq
    cos_off, sin_off = _rope_tables(jnp.arange(tm, dtype=jnp.int32))
    cos_base, sin_base = _rope_tables(tm * jnp.arange(tiles_per_seq, dtype=jnp.int32))
    proj_tile = lambda i: jnp.minimum(i, n_tiles - 1)
    out_tile = lambda i: jnp.maximum(i - 2, 0)
    x_tiles = x.reshape(n_tiles, tm, D_MODEL)
    last = lambda width: pl.BlockSpec((1, width, WINDOW), lambda i: (proj_tile(i) // tiles_per_seq, 0, 0))
    out, ko, vo, gvo = pl.pallas_call(
        functools.partial(_mix_prompt_kernel, tm=tm, tiles_per_seq=tiles_per_seq, n_tiles=n_tiles),
        grid=(n_tiles + 2,),
        in_specs=[pl.BlockSpec((1, tm, D_MODEL), lambda i: (proj_tile(i), 0, 0)),
                  pl.BlockSpec((1, tm, D_MODEL), lambda i: (out_tile(i), 0, 0)),
                  _resident((tm, LANES)), _resident((tm, LANES)), _resident((tiles_per_seq, LANES)),
                  _resident((tiles_per_seq, LANES)), _resident((1, D_MODEL)), _resident((D_MODEL, D_IN)),
                  pl.BlockSpec(memory_space=pltpu.SMEM), _resident((1, D_GMLP)),
                  _resident((G_HEADS, CHUNK, CHUNK)), _resident((CHUNK, D_GMLP)), _resident((1, Q_W)),
                  _resident((1, D_GMLP)), _resident((D_MODEL, D_MODEL))],
        out_specs=[pl.BlockSpec((1, tm, D_MODEL), lambda i: (out_tile(i), 0, 0)), last(KV_W), last(KV_W),
                   last(D_GMLP)],
        out_shape=[jax.ShapeDtypeStruct((n_tiles, tm, D_MODEL), F32), jax.ShapeDtypeStruct((b, KV_W, WINDOW), F32),
                   jax.ShapeDtypeStruct((b, KV_W, WINDOW), F32), jax.ShapeDtypeStruct((b, D_GMLP, CHUNK), F32)],
        scratch_shapes=[pltpu.VMEM((2, tm, Q_W), BF16), pltpu.VMEM((2, tm + WINDOW, KV_W), BF16),
                        pltpu.VMEM((2, tm + WINDOW, KV_W), BF16), pltpu.VMEM((2, tm, D_GMLP), F32),
                        pltpu.VMEM((2, tm, D_GMLP), BF16), pltpu.VMEM((2, tm, D_MODEL), BF16)],
        compiler_params=pltpu.CompilerParams(dimension_semantics=("arbitrary",), vmem_limit_bytes=VMEM_LIMIT_BYTES),
        name="mix_prompt",
    )(x_tiles, x_tiles, cos_off, sin_off, cos_base, sin_base, gmix, w_in, sinks, gvg, wcat, bias_full, ga, gg, w_out)
    return out.reshape(b, s, D_MODEL), ko, vo, gvo


def _mix_sample_kernel(x_ref, ckt_ref, cvt_ref, cos_ref, sin_ref, gmix_ref, win_ref, sinks_ref, gvg_ref, coef_ref,
                       biasr_ref, ga_ref, gg_ref, wout_ref, o_ref, kot_ref, vot_ref, gvo_ref, q_s, k_s, v_s, ya_s,
                       yg_s, *, t_new, w_buf):
    step = pl.program_id(0)
    n_tok = x_ref.shape[0]
    step_seqs = SEQ_GROUP * GROUPS_PER_STEP
    step_rows = step_seqs * t_new
    grp_rows = SEQ_GROUP * t_new
    grp_keys = SEQ_GROUP * w_buf
    sub = 8

    @pl.when(step == 0)
    def _():
        tile = lambda tab: jnp.broadcast_to(tab[None], (n_tok // sub, sub, LANES)).reshape(n_tok, LANES)
        q, k, v, u, gvn = _in_proj(x_ref[...], gmix_ref[...], win_ref[...], tile(cos_ref[...]),
                                   tile(_sign_sin(sin_ref[...])), gvg_ref[...])
        q_s[...] = q
        k_s[...] = k
        v_s[...] = v
        gv_seq = gvn.reshape(n_tok // t_new, t_new, D_GMLP)
        for tok in range(t_new):
            gvo_ref[tok] = gv_seq[:, tok, :].T
        g3 = gvn.reshape(n_tok // sub, sub, D_GMLP)
        trow = lax.broadcasted_iota(jnp.int32, (1, sub, D_GMLP), 1) & (t_new - 1)
        mixed = biasr_ref[...][None] + coef_ref[0][None] * g3
        for d in range(1, t_new):
            shifted = jnp.where(trow >= d, pltpu.roll(g3, d, 1), 0.0)
            mixed = mixed + coef_ref[d][None] * shifted
        yg_s[...] = (u.reshape(n_tok // sub, sub, D_GMLP) * mixed).reshape(n_tok, D_GMLP)

    row0 = pl.multiple_of(step * step_rows, step_rows)
    q_step = q_s[pl.ds(row0, step_rows), :]
    kn = k_s[pl.ds(row0, step_rows), :]
    vn = v_s[pl.ds(row0, step_rows), :]

    knt = kn.T
    vnt = vn.T
    tail = lax.broadcasted_iota(jnp.int32, (KV_W, w_buf), 1) >= w_buf - t_new
    for b in range(step_seqs):
        shift = (w_buf - t_new - t_new * b) % w_buf
        kot_ref[b] = jnp.where(tail, pltpu.roll(knt, shift, 1) if shift else knt,
                               pltpu.roll(ckt_ref[b], w_buf - t_new, 1))
        vot_ref[b] = jnp.where(tail, pltpu.roll(vnt, shift, 1) if shift else vnt,
                               pltpu.roll(cvt_ref[b], w_buf - t_new, 1))

    knb = kn.astype(BF16)
    vnb = vn.astype(BF16)

    n_rows = GQA * grp_rows
    shift_t = t_new.bit_length() - 1
    shift_w = w_buf.bit_length() - 1
    r = lax.broadcasted_iota(jnp.int32, (n_rows, grp_keys), 0)
    c = lax.broadcasted_iota(jnp.int32, (n_rows, grp_keys), 1)
    mask_c = ((c >> shift_w) == ((r & (grp_rows - 1)) >> shift_t)) & ((c & (w_buf - 1)) > (r & (t_new - 1)))
    r2 = lax.broadcasted_iota(jnp.int32, (n_rows, grp_rows), 0)
    c2 = lax.broadcasted_iota(jnp.int32, (n_rows, grp_rows), 1)
    mask_n = ((c2 >> shift_t) == ((r2 & (grp_rows - 1)) >> shift_t)) & ((c2 & (t_new - 1)) <= (r2 & (t_new - 1)))
    row_head = lax.broadcasted_iota(jnp.int32, (n_rows, 1), 0) >> (grp_rows.bit_length() - 1)

    for grp in range(GROUPS_PER_STEP):
        rows = slice(grp * grp_rows, (grp + 1) * grp_rows)
        seqs = range(grp * SEQ_GROUP, (grp + 1) * SEQ_GROUP)
        for kv in range(N_KV):
            heads = [kv * GQA + i for i in range(GQA)]
            lanes = slice(HEAD_DIM * kv, HEAD_DIM * (kv + 1))
            kt = jnp.concatenate([ckt_ref[b, lanes, :] for b in seqs], axis=1).astype(BF16)
            vt = jnp.concatenate([cvt_ref[b, lanes, :] for b in seqs], axis=1).astype(BF16)
            qs = jnp.concatenate([q_step[rows, HEAD_DIM * hd:HEAD_DIM * (hd + 1)] for hd in heads], axis=0)
            s_c = jnp.where(mask_c, jnp.dot(qs, kt, preferred_element_type=F32), -jnp.inf)
            s_n = jnp.where(mask_n, lax.dot_general(qs, knb[rows, lanes], _NT, preferred_element_type=F32), -jnp.inf)
            sink = jnp.full((n_rows, 1), sinks_ref[heads[0]] * LOG2E, F32)
            for i in range(1, GQA):
                sink = jnp.where(row_head == i, sinks_ref[heads[i]] * LOG2E, sink)
            m = jnp.maximum(jnp.maximum(jnp.max(s_c, axis=1, keepdims=True), jnp.max(s_n, axis=1, keepdims=True)),
                            sink)
            p_c = jnp.exp2(s_c - m)
            p_n = jnp.exp2(s_n - m)
            den = jnp.sum(p_c, axis=1, keepdims=True) + jnp.sum(p_n, axis=1, keepdims=True) + jnp.exp2(sink - m)
            o = (lax.dot_general(p_c.astype(BF16), vt, _NT, preferred_element_type=F32)
                 + jnp.dot(p_n.astype(BF16), vnb[rows, lanes], preferred_element_type=F32)) / den
            for i, hd in enumerate(heads):
                ya_s[pl.ds(row0 + grp * grp_rows, grp_rows), HEAD_DIM * hd:HEAD_DIM * (hd + 1)] = (
                    o[i * grp_rows:(i + 1) * grp_rows])

    @pl.when(step == pl.num_programs(0) - 1)
    def _():
        o_ref[...] = _out_proj(x_ref[...], ya_s[...], yg_s[...], ga_ref[...], gg_ref[...], wout_ref[...])


def _mix_sample(x, cache_kt, cache_vt, cos, sin, gmix, w_in, sinks, gvg, coef, bias_rows, ga, gg, w_out, t_new):
    n_tok = x.shape[0]
    n_seq, _, w_buf = cache_kt.shape
    step_seqs = SEQ_GROUP * GROUPS_PER_STEP
    assert n_seq % step_seqs == 0 and n_tok == n_seq * t_new
    assert t_new & (t_new - 1) == 0 and w_buf & (w_buf - 1) == 0 and 8 % t_new == 0
    assert step_seqs * t_new == w_buf == LANES
    cache_spec = pl.BlockSpec((step_seqs, KV_W, w_buf), lambda i: (i, 0, 0))
    return pl.pallas_call(
        functools.partial(_mix_sample_kernel, t_new=t_new, w_buf=w_buf),
        grid=(n_seq // step_seqs,),
        in_specs=[_resident((n_tok, D_MODEL)), cache_spec, cache_spec, _resident((8, LANES)),
                  _resident((8, LANES)), _resident((1, D_MODEL)), _resident((D_MODEL, D_IN)),
                  pl.BlockSpec(memory_space=pltpu.SMEM), _resident((1, D_GMLP)), _resident((t_new, 8, D_GMLP)),
                  _resident((8, D_GMLP)), _resident((1, Q_W)), _resident((1, D_GMLP)),
                  _resident((D_MODEL, D_MODEL))],
        out_specs=[pl.BlockSpec((n_tok, D_MODEL), lambda i: (0, 0)), cache_spec, cache_spec,
                   pl.BlockSpec((t_new, D_GMLP, n_seq), lambda i: (0, 0, 0))],
        out_shape=[jax.ShapeDtypeStruct((n_tok, D_MODEL), F32), jax.ShapeDtypeStruct(cache_kt.shape, F32),
                   jax.ShapeDtypeStruct(cache_vt.shape, F32), jax.ShapeDtypeStruct((t_new, D_GMLP, n_seq), F32)],
        scratch_shapes=[pltpu.VMEM((n_tok, Q_W), BF16), pltpu.VMEM((n_tok, KV_W), F32), pltpu.VMEM((n_tok, KV_W), F32),
                        pltpu.VMEM((n_tok, Q_W), F32), pltpu.VMEM((n_tok, D_GMLP), F32)],
        compiler_params=pltpu.CompilerParams(dimension_semantics=("arbitrary",), vmem_limit_bytes=VMEM_LIMIT_BYTES),
        name="mix_sample",
    )(x, cache_kt, cache_vt, cos, sin, gmix, w_in, sinks, gvg, coef, bias_rows, ga, gg, w_out)


def _gmlp_tables_kernel(ws_ref, bs_ref, bias_ref, coef_ref, biasr_ref, *, t_new):
    sub = 8
    expand = lambda col: jnp.broadcast_to(col, (col.shape[0], D_GMLP // G_HEADS))
    row = lax.broadcasted_iota(jnp.int32, (CHUNK, CHUNK), 0)
    lane = lax.broadcasted_iota(jnp.int32, (CHUNK, CHUNK), 1)
    bias = jnp.concatenate(
        [expand(jnp.sum(jnp.where(lane == row, bs_ref[hd:hd + 1, :], 0.0), axis=1, keepdims=True))
         for hd in range(G_HEADS)], axis=1)
    bias_ref[...] = bias
    biasr_ref[...] = jnp.concatenate([bias[0:t_new]] * (sub // t_new), axis=0)
    t = lax.broadcasted_iota(jnp.int32, (sub, CHUNK), 0) & (t_new - 1)
    s = lax.broadcasted_iota(jnp.int32, (sub, CHUNK), 1)
    for d in range(t_new):
        cols = []
        for hd in range(G_HEADS):
            w_rows = jnp.concatenate([ws_ref[hd, 0:t_new, :]] * (sub // t_new), axis=0)
            cols.append(expand(jnp.sum(jnp.where(s == t - d, w_rows, 0.0), axis=1, keepdims=True)))
        coef_ref[d] = jnp.concatenate(cols, axis=1)


def _gmlp_tables(ws, bs, t_new):
    assert 8 % t_new == 0
    return pl.pallas_call(
        functools.partial(_gmlp_tables_kernel, t_new=t_new),
        out_shape=[jax.ShapeDtypeStruct((CHUNK, D_GMLP), F32), jax.ShapeDtypeStruct((t_new, 8, D_GMLP), F32),
                   jax.ShapeDtypeStruct((8, D_GMLP), F32)],
        name="gmlp_tables",
    )(ws, bs)


def _rope_tables(pos):
    inv_freq = ROPE_THETA ** (-jnp.arange(0, HEAD_DIM, 2, dtype=F32) / HEAD_DIM)
    ang = pos.astype(F32)[:, None] * jnp.tile(inv_freq, 2 * LANES // HEAD_DIM)[None, :]
    return jnp.cos(ang), jnp.sin(ang)


def kernel(x_prompt, x_sample, cache_k_win, cache_v_win, norm_ffn1, ffn1_gate, ffn1_up, ffn1_down, norm_mix, w_in,
           attn_sinks, gmlp_v_norm, gmlp_w_s, gmlp_b_s, norm_attn_out, norm_gmlp_out, w_out, norm_ffn2, ffn2_gate,
           ffn2_up, ffn2_down, norm_final):
    depth = norm_ffn1.shape[0]
    b, s, _ = x_prompt.shape
    bd, t_new, _ = x_sample.shape
    w_buf = cache_k_win.shape[2]

    cos_s, sin_s = _rope_tables(PAST_LEN + jnp.arange(t_new, dtype=jnp.int32))
    cos_s, sin_s = jnp.tile(cos_s, (8 // t_new, 1)), jnp.tile(sin_s, (8 // t_new, 1))

    hp = x_prompt.reshape(b * s, D_MODEL)
    hs = x_sample.reshape(bd * t_new, D_MODEL)
    outs = [[] for _ in range(6)]
    ffn1_w = [ffn1_gate[0], ffn1_up[0], ffn1_down[0]]
    for l in range(depth):
        last = l == depth - 1
        row = lambda a: a[l].reshape(1, -1)

        wcat = gmlp_w_s[l]
        bias_full, coef, bias_rows = _gmlp_tables(gmlp_w_s[l], gmlp_b_s[l], t_new)

        hp, hs, (w_in_b, w_out_b, wg2, wu2, wd2) = (_ffn_first if l == 0 else _ffn_half)(
            hp, hs, norm_ffn1[l], *ffn1_w, cast=(w_in[l], w_out[l], ffn2_gate[l], ffn2_up[l], ffn2_down[l]))

        hp, kpt, vpt, gvpt = _mix_prompt(hp.reshape(b, s, D_MODEL), row(norm_mix), w_in_b, attn_sinks[l],
                                         row(gmlp_v_norm), wcat, bias_full, row(norm_attn_out), row(norm_gmlp_out),
                                         w_out_b)
        to_t = lambda c: c.transpose(0, 2, 3, 1).reshape(bd, KV_W, w_buf)
        hs, kst, vst, gvs = _mix_sample(hs, to_t(cache_k_win[l]), to_t(cache_v_win[l]), cos_s, sin_s, row(norm_mix),
                                        w_in_b, attn_sinks[l], row(gmlp_v_norm), coef, bias_rows, row(norm_attn_out),
                                        row(norm_gmlp_out), w_out_b, t_new)
        from_t = lambda c: c.reshape(bd, N_KV, HEAD_DIM, w_buf).transpose(0, 3, 1, 2)

        next_ffn1 = () if last else (ffn1_gate[l + 1], ffn1_up[l + 1], ffn1_down[l + 1])
        hp, hs, ffn1_w = _ffn_half(hp.reshape(b * s, D_MODEL), hs, norm_ffn2[l], wg2, wu2, wd2,
                                   gf=norm_final if last else None, cast=next_ffn1,
                                   sample_out_shape=(bd, t_new, D_MODEL) if last else None)

        outs[0].append(kpt.reshape(b, N_KV, HEAD_DIM, WINDOW).transpose(0, 3, 1, 2))
        outs[1].append(vpt.reshape(b, N_KV, HEAD_DIM, WINDOW).transpose(0, 3, 1, 2))
        outs[2].append(from_t(kst))
        outs[3].append(from_t(vst))
        outs[4].append(gvpt.reshape(b, G_HEADS, D_GMLP // G_HEADS, CHUNK).transpose(0, 3, 1, 2))
        outs[5].append(gvs.reshape(t_new, G_HEADS, D_GMLP // G_HEADS, bd).transpose(3, 0, 1, 2))

    return (hp.reshape(b, s, D_MODEL), hs) + tuple(jnp.stack(o) for o in outs)
```

```python
import functools

import jax
import jax.numpy as jnp
from jax import lax
from jax.experimental import pallas as pl
from jax.experimental.pallas import tpu as pltpu

F32 = jnp.float32
BF16 = jnp.bfloat16

D_MODEL = 1024
D_FF = 2816
HEAD_DIM = 64
N_HEADS = 8
N_KV = 2
GQA = N_HEADS // N_KV
WINDOW = 128
CHUNK = 128
G_HEADS = 8
Q_W = N_HEADS * HEAD_DIM
KV_W = N_KV * HEAD_DIM
D_GMLP = 512
D_IN = Q_W + 2 * KV_W + 2 * D_GMLP
K_OFF = Q_W
V_OFF = K_OFF + KV_W
U_OFF = V_OFF + KV_W
GV_OFF = U_OFF + D_GMLP
ROPE_THETA = 10000.0
PAST_LEN = 16384
EPS = 1e-6
LOG2E = 1.4426950408889634
Q_SCALE = HEAD_DIM ** -0.5 * LOG2E
LANES = 128
BF16_SUBLANES = 16

TOKEN_TILE = 512
FF_CHUNK = 256
STAGE_SLOTS = 3
SEQ_GROUP = 8
GROUPS_PER_STEP = 4
VMEM_LIMIT_BYTES = 56 * 1024 * 1024

_NT = (((1,), (1,)), ((), ()))


def _rms(x, g):
    ms = jnp.mean(x * x, axis=-1, keepdims=True)
    return (x * lax.rsqrt(ms + EPS)) * g


def _resident(shape):
    zeros = (0,) * len(shape)
    return pl.BlockSpec(shape, lambda *_: zeros, pipeline_mode=pl.Buffered(1))


def _silu_gate(h, wg, wu):
    gate = jnp.dot(h, wg, preferred_element_type=F32)
    up = jnp.dot(h, wu, preferred_element_type=F32)
    return (gate * jax.nn.sigmoid(gate) * up).astype(BF16)


def _ffn_first_kernel(*refs, n_cast):
    refs = iter(refs)
    xp_ref, xn_ref, xs_ref, g_ref, wg_hbm, wu_hbm, wd_hbm = (next(refs) for _ in range(7))
    cast_in = [next(refs) for _ in range(n_cast)]
    yp_ref, ys_ref = next(refs), next(refs)
    cast_out = [next(refs) for _ in range(n_cast)]
    h_ref, act0_ref, act_ref, wg_ref, wu_ref, wd_ref, col_stage, row_stage, sem = (next(refs) for _ in range(9))
    n_chunks = D_FF // FF_CHUNK
    gate_up = lambda h, c: _silu_gate(h, wg_ref[c], wu_ref[c])

    @pl.when(pl.program_id(0) == 0)
    def _():
        def fetch(c, slot):
            span = pl.ds(pl.multiple_of(c * FF_CHUNK, FF_CHUNK), FF_CHUNK)
            return (pltpu.make_async_copy(wg_hbm.at[:, span], col_stage.at[slot, 0], sem.at[slot, 0]),
                    pltpu.make_async_copy(wu_hbm.at[:, span], col_stage.at[slot, 1], sem.at[slot, 1]),
                    pltpu.make_async_copy(wd_hbm.at[span, :], row_stage.at[slot], sem.at[slot, 2]))

        for c in range(STAGE_SLOTS - 1):
            for copy in fetch(c, c):
                copy.start()
        h_ref[...] = _rms(xs_ref[...], g_ref[...]).astype(BF16)
        ys_ref[...] = jnp.zeros(ys_ref.shape, F32)

        def body(c, carry):
            slot = c % STAGE_SLOTS
            ahead = c + STAGE_SLOTS - 1

            @pl.when(ahead < n_chunks)
            def _():
                for copy in fetch(ahead, ahead % STAGE_SLOTS):
                    copy.start()

            for copy in fetch(c, slot):
                copy.wait()
            rows = pl.ds(pl.multiple_of(c * FF_CHUNK, FF_CHUNK), FF_CHUNK)
            wg_ref[c] = col_stage[slot, 0].astype(BF16)
            wu_ref[c] = col_stage[slot, 1].astype(BF16)
            wd_ref[rows, :] = row_stage[slot].astype(BF16)
            ys_ref[...] += jnp.dot(gate_up(h_ref[...], c), wd_ref[rows, :], preferred_element_type=F32)
            return carry

        lax.fori_loop(0, n_chunks, body, 0)
        ys_ref[...] = xs_ref[...] + 0.5 * ys_ref[...]
        h0 = _rms(xp_ref[...], g_ref[...]).astype(BF16)
        h_ref[...] = h0
        act0_ref[...] = gate_up(h0, 0)

    for c in range(1, n_chunks):
        act_ref[:, (c - 1) * FF_CHUNK:c * FF_CHUNK] = gate_up(h_ref[...], c)
    down0 = jnp.dot(act0_ref[...], wd_ref[0:FF_CHUNK, :], preferred_element_type=F32)
    hn = _rms(xn_ref[...], g_ref[...]).astype(BF16)
    h_ref[...] = hn
    act0_ref[...] = gate_up(hn, 0)
    yp_ref[...] = xp_ref[...] + 0.5 * (down0 + jnp.dot(act_ref[...], wd_ref[FF_CHUNK:, :], preferred_element_type=F32))

    for src, dst in zip(cast_in, cast_out):
        dst[...] = src[...].astype(BF16)


def _ffn_kernel(*refs, final_norm, n_cast):
    refs = iter(refs)
    xp_ref, xn_ref, xs_ref, g_ref, wg_ref, wu_ref, wd_ref = (next(refs) for _ in range(7))
    gf_ref = next(refs) if final_norm else None
    cast_in = [next(refs) for _ in range(n_cast)]
    yp_ref, ys_ref = next(refs), next(refs)
    cast_out = [next(refs) for _ in range(n_cast)]
    h_ref, act0_ref, act_ref = next(refs), next(refs), next(refs)

    def gate_up(h, c):
        sl = slice(c * FF_CHUNK, (c + 1) * FF_CHUNK)
        return _silu_gate(h, wg_ref[:, sl], wu_ref[:, sl])

    on_sample = pl.program_id(0) == 0

    @pl.when(on_sample)
    def _():
        h0 = _rms(xs_ref[...], g_ref[...]).astype(BF16)
        h_ref[...] = h0
        act0_ref[...] = gate_up(h0, 0)

    for c in range(1, D_FF // FF_CHUNK):
        act_ref[:, (c - 1) * FF_CHUNK:c * FF_CHUNK] = gate_up(h_ref[...], c)
    down0 = jnp.dot(act0_ref[...], wd_ref[0:FF_CHUNK, :], preferred_element_type=F32)
    hn = _rms(xn_ref[...], g_ref[...]).astype(BF16)
    h_ref[...] = hn
    act0_ref[...] = gate_up(hn, 0)
    x = jnp.where(on_sample, xs_ref[...], xp_ref[...])
    y = x + 0.5 * (down0 + jnp.dot(act_ref[...], wd_ref[FF_CHUNK:, :], preferred_element_type=F32))
    yp_ref[...] = _rms(y, gf_ref[...]) if final_norm else y

    @pl.when(on_sample)
    def _():
        ys_ref[...] = yp_ref[...].reshape(ys_ref.shape)

    for src, dst in zip(cast_in, cast_out):
        dst[...] = src[...].astype(BF16)


def _cast_row_blocks(rows, n_steps):
    return max(d for d in range(1, n_steps + 1) if rows % d == 0 and (rows // d) % BF16_SUBLANES == 0)


def _cast_specs(mats, n_steps):
    specs = []
    for w in mats:
        rows, cols = w.shape
        nb = _cast_row_blocks(rows, n_steps)
        specs.append(pl.BlockSpec((rows // nb, cols), lambda i, nb=nb: (jnp.minimum(i, nb - 1), 0)))
    return specs


def _ffn_half(xp, xs, g, wg, wu, wd, gf=None, cast=(), sample_out_shape=None):
    tm = TOKEN_TILE
    n = xp.shape[0]
    assert n % tm == 0 and xs.shape[0] == tm
    n_tiles = n // tm
    prompt_spec = pl.BlockSpec((tm, D_MODEL), lambda i: (jnp.maximum(i - 1, 0), 0))
    next_spec = pl.BlockSpec((tm, D_MODEL), lambda i: (jnp.minimum(i, n_tiles - 1), 0))
    sample_in = _resident((tm, D_MODEL))
    sample_shape = sample_out_shape or (tm, D_MODEL)
    sample_out = pl.BlockSpec(sample_shape, lambda i: (0,) * len(sample_shape))
    in_specs = [prompt_spec, next_spec, sample_in, _resident((1, D_MODEL)), _resident((D_MODEL, D_FF)),
                _resident((D_MODEL, D_FF)), _resident((D_FF, D_MODEL))]
    args = [xp, xp, xs, g.reshape(1, D_MODEL), wg, wu, wd]
    if gf is not None:
        in_specs.append(_resident((1, D_MODEL)))
        args.append(gf.reshape(1, D_MODEL))
    out_specs = [prompt_spec, sample_out]
    out_shape = [jax.ShapeDtypeStruct((n, D_MODEL), F32), jax.ShapeDtypeStruct(sample_shape, F32)]
    cast_specs = _cast_specs(cast, n_tiles)
    out_shape += [jax.ShapeDtypeStruct(w.shape, BF16) for w in cast]
    outs = pl.pallas_call(
        functools.partial(_ffn_kernel, final_norm=gf is not None, n_cast=len(cast)),
        grid=(n_tiles + 1,),
        in_specs=in_specs + cast_specs,
        out_specs=out_specs + cast_specs,
        out_shape=out_shape,
        scratch_shapes=[pltpu.VMEM((tm, D_MODEL), BF16), pltpu.VMEM((tm, FF_CHUNK), BF16),
                        pltpu.VMEM((tm, D_FF - FF_CHUNK), BF16)],
        compiler_params=pltpu.CompilerParams(dimension_semantics=("arbitrary",), vmem_limit_bytes=VMEM_LIMIT_BYTES),
        name="ffn_final" if gf is not None else "ffn_half",
    )(*args, *cast)
    return outs[0], outs[1], list(outs[2:])


def _ffn_first(xp, xs, g, wg, wu, wd, cast=()):
    tm = TOKEN_TILE
    n = xp.shape[0]
    n_chunks = D_FF // FF_CHUNK
    assert n % tm == 0 and xs.shape[0] == tm and D_FF % FF_CHUNK == 0 and n_chunks >= STAGE_SLOTS
    n_tiles = n // tm
    prompt_spec = pl.BlockSpec((tm, D_MODEL), lambda i: (i, 0))
    next_spec = pl.BlockSpec((tm, D_MODEL), lambda i: (jnp.minimum(i + 1, n_tiles - 1), 0))
    hbm = pl.BlockSpec(memory_space=pl.ANY)
    cast_specs = _cast_specs(cast, n_tiles)
    outs = pl.pallas_call(
        functools.partial(_ffn_first_kernel, n_cast=len(cast)),
        grid=(n_tiles,),
        in_specs=[prompt_spec, next_spec, _resident((tm, D_MODEL)), _resident((1, D_MODEL)), hbm, hbm, hbm] + cast_specs,
        out_specs=[prompt_spec, pl.BlockSpec((tm, D_MODEL), lambda i: (0, 0))] + cast_specs,
        out_shape=[jax.ShapeDtypeStruct((n, D_MODEL), F32), jax.ShapeDtypeStruct((tm, D_MODEL), F32)]
        + [jax.ShapeDtypeStruct(w.shape, BF16) for w in cast],
        scratch_shapes=[pltpu.VMEM((tm, D_MODEL), BF16), pltpu.VMEM((tm, FF_CHUNK), BF16),
                        pltpu.VMEM((tm, D_FF - FF_CHUNK), BF16),
                        pltpu.VMEM((n_chunks, D_MODEL, FF_CHUNK), BF16), pltpu.VMEM((n_chunks, D_MODEL, FF_CHUNK), BF16),
                        pltpu.VMEM((D_FF, D_MODEL), BF16), pltpu.VMEM((STAGE_SLOTS, 2, D_MODEL, FF_CHUNK), F32),
                        pltpu.VMEM((STAGE_SLOTS, FF_CHUNK, D_MODEL), F32), pltpu.SemaphoreType.DMA((STAGE_SLOTS, 3))],
        compiler_params=pltpu.CompilerParams(dimension_semantics=("arbitrary",), vmem_limit_bytes=VMEM_LIMIT_BYTES),
        name="ffn_first",
    )(xp, xp, xs, g.reshape(1, D_MODEL), wg, wu, wd, *cast)
    return outs[0], outs[1], list(outs[2:])


def _first_half(shape):
    return (lax.broadcasted_iota(jnp.int32, shape, 1) & (HEAD_DIM - 1)) < HEAD_DIM // 2


def _sign_sin(sin):
    return jnp.where(_first_half(sin.shape), -sin, sin)


def _rope(xg, cos, sin_signed):
    swapped = jnp.where(_first_half(xg.shape), pltpu.roll(xg, LANES - HEAD_DIM // 2, 1),
                        pltpu.roll(xg, HEAD_DIM // 2, 1))
    return xg * cos + swapped * sin_signed


def _in_proj(x, gmix, w_in, cos, sin_signed, gvn_gain):
    h = _rms(x, gmix).astype(BF16)
    z = jnp.dot(h, w_in, preferred_element_type=F32)
    q = jnp.concatenate(
        [(_rope(z[:, LANES * i:LANES * (i + 1)], cos, sin_signed) * Q_SCALE).astype(BF16) for i in range(Q_W // LANES)],
        axis=1)
    k = _rope(z[:, K_OFF:V_OFF], cos, sin_signed)
    v = z[:, V_OFF:U_OFF]
    u = _gelu(z[:, U_OFF:GV_OFF])
    gvn = _rms(_gelu(z[:, GV_OFF:]), gvn_gain)
    return q, k, v, u, gvn


def _out_proj(x, ya, yg, ga, gg, w_out):
    cat = jnp.concatenate([_rms(ya, ga).astype(BF16), _rms(yg, gg).astype(BF16)], axis=1)
    return x + jnp.dot(cat, w_out, preferred_element_type=F32)


def _softmax_sink(s, sink):
    m = jnp.maximum(jnp.max(s, axis=1, keepdims=True), sink)
    p = jnp.exp2(s - m)
    return p, jnp.sum(p, axis=1, keepdims=True) + jnp.exp2(sink - m)


def _gelu(x):
    k = -2.0 * (2.0 / jnp.pi) ** 0.5 * LOG2E
    return x / (1.0 + jnp.exp2(x * (k + (0.044715 * k) * (x * x))))


def _mix_prompt_kernel(xa_ref, xc_ref, coff_ref, soff_ref, cbase_ref, sbase_ref, gmix_ref, win_ref, sinks_ref, gvg_ref,
                       wcat_ref, bias_ref, ga_ref, gg_ref, wout_ref, o_ref, ko_ref, vo_ref, gvo_ref, q_s, k_s, v_s, u_s,
                       gv_s, cat_s, *, tm, tiles_per_seq, n_tiles):
    s = pl.program_id(0)

    @pl.when(s == 0)
    def _():
        k_s[1, tm:tm + WINDOW, :] = jnp.zeros((WINDOW, KV_W), BF16)
        v_s[1, tm:tm + WINDOW, :] = jnp.zeros((WINDOW, KV_W), BF16)
        cat_s[1] = jnp.zeros((tm, D_MODEL), BF16)

    step = functools.partial(
        _mix_prompt_step, s, xa_ref, xc_ref, coff_ref, soff_ref, cbase_ref, sbase_ref, gmix_ref, win_ref, sinks_ref,
        gvg_ref, wcat_ref, bias_ref, ga_ref, gg_ref, wout_ref, o_ref, ko_ref, vo_ref, gvo_ref, q_s, k_s, v_s, u_s, gv_s,
        cat_s, tm=tm, tiles_per_seq=tiles_per_seq, n_tiles=n_tiles)
    last = n_tiles + 1
    pl.when(s == 0)(functools.partial(step, cur=0, stages="proj"))
    pl.when(s == last)(functools.partial(step, cur=last % 2, stages="out"))
    for parity in range(2):
        pl.when((s > 0) & (s < last) & (s % 2 == parity))(functools.partial(step, cur=parity, stages="all"))


def _mix_prompt_step(s, xa_ref, xc_ref, coff_ref, soff_ref, cbase_ref, sbase_ref, gmix_ref, win_ref, sinks_ref, gvg_ref,
                     wcat_ref, bias_ref, ga_ref, gg_ref, wout_ref, o_ref, ko_ref, vo_ref, gvo_ref, q_s, k_s, v_s, u_s, gv_s,
                     cat_s, *, tm, tiles_per_seq, n_tiles, cur, stages):
    oth = 1 - cur

    h = _rms(xa_ref[0], gmix_ref[...]).astype(BF16)
    tile_in_seq = jnp.minimum(s, n_tiles - 1) % tiles_per_seq
    cb = cbase_ref[pl.ds(tile_in_seq, 1), :]
    sb = sbase_ref[pl.ds(tile_in_seq, 1), :]
    cos = cb * coff_ref[...] - sb * soff_ref[...]
    sin = _sign_sin(sb * coff_ref[...] + cb * soff_ref[...])

    def proj_q():
        z = jnp.dot(h, win_ref[:, 0:Q_W], preferred_element_type=F32)
        for i in range(Q_W // LANES):
            q_s[cur, :, LANES * i:LANES * (i + 1)] = (
                _rope(z[:, LANES * i:LANES * (i + 1)], cos, sin) * Q_SCALE).astype(BF16)

    def proj_kv():
        z = jnp.dot(h, win_ref[:, K_OFF:U_OFF], preferred_element_type=F32)
        k = _rope(z[:, 0:KV_W], cos, sin)
        v = z[:, KV_W:]
        k_s[cur, 0:WINDOW, :] = k_s[oth, tm:tm + WINDOW, :]
        v_s[cur, 0:WINDOW, :] = v_s[oth, tm:tm + WINDOW, :]
        k_s[cur, WINDOW:, :] = k.astype(BF16)
        v_s[cur, WINDOW:, :] = v.astype(BF16)
        ko_ref[0] = k[tm - WINDOW:].T
        vo_ref[0] = v[tm - WINDOW:].T

    def proj_u():
        u_s[cur] = _gelu(jnp.dot(h, win_ref[:, U_OFF:GV_OFF], preferred_element_type=F32))

    def proj_gv():
        gvn = _rms(_gelu(jnp.dot(h, win_ref[:, GV_OFF:], preferred_element_type=F32)), gvg_ref[...])
        gv_s[cur] = gvn.astype(BF16)
        gvo_ref[0] = gvn[tm - CHUNK:].T

    def out_half(c):
        cols = slice(c * (D_MODEL // 2), (c + 1) * (D_MODEL // 2))

        def run():
            o_ref[0, :, cols] = xc_ref[0, :, cols] + jnp.dot(cat_s[cur], wout_ref[:, cols], preferred_element_type=F32)
        return run

    mid_fill = [[out_half(0)], [proj_q], [proj_kv, proj_u], [proj_gv]]
    end_fill = [[], [], [], [out_half(1)]]
    assert len(mid_fill) == len(end_fill) == tm // WINDOW
    if stages == "proj":
        for run in (proj_q, proj_kv, proj_u, proj_gv):
            run()
        return
    if stages == "out":
        out_half(0)()
        out_half(1)()
        return

    wrow = lax.broadcasted_iota(jnp.int32, (CHUNK, CHUNK), 0)
    wcol = lax.broadcasted_iota(jnp.int32, (CHUNK, CHUNK), 1)
    wtril = [jnp.where(wcol <= wrow, wcat_ref[hd], 0.0).astype(BF16) for hd in range(G_HEADS)]
    wmix = [jnp.concatenate(wtril[2 * p:2 * p + 2], axis=1) for p in range(G_HEADS // 2)]
    lane = lax.broadcasted_iota(jnp.int32, (CHUNK, LANES), 1)
    low_head = lane < HEAD_DIM

    qi = lax.broadcasted_iota(jnp.int32, (WINDOW, 2 * WINDOW), 0)
    sj = lax.broadcasted_iota(jnp.int32, (WINDOW, 2 * WINDOW), 1)
    dist = WINDOW + qi - sj
    band = (dist >= 0) & (dist < WINDOW)
    first_lo = jnp.where((s + tiles_per_seq - 1) % tiles_per_seq == 0, WINDOW, 0)

    for j in range(tm // WINDOW):
        rows = slice(j * WINDOW, (j + 1) * WINDOW)
        mask = band & (sj >= first_lo) if j == 0 else band
        qb = q_s[oth, rows, :]
        scores = []
        for kv in range(N_KV):
            qs = jnp.concatenate(
                [qb[:, HEAD_DIM * hd:HEAD_DIM * (hd + 1)] for hd in range(kv * GQA, (kv + 1) * GQA)], axis=0)
            kb = k_s[oth, j * WINDOW:(j + 2) * WINDOW, HEAD_DIM * kv:HEAD_DIM * (kv + 1)]
            scores.append(lax.dot_general(qs, kb, _NT, preferred_element_type=F32))
        for run in mid_fill[j]:
            run()
        outs, dens = [], []
        for kv in range(N_KV):
            vb = v_s[oth, j * WINDOW:(j + 2) * WINDOW, HEAD_DIM * kv:HEAD_DIM * (kv + 1)]
            ps = []
            for g in range(GQA):
                sg = jnp.where(mask, scores[kv][g * WINDOW:(g + 1) * WINDOW], -jnp.inf)
                p, den = _softmax_sink(sg, sinks_ref[kv * GQA + g] * LOG2E)
                ps.append(p.astype(BF16))
                dens.append(den)
            outs.append(jnp.dot(jnp.concatenate(ps, axis=0), vb, preferred_element_type=F32))
        mixed = []
        for p in range(G_HEADS // 2):
            r = gv_s[oth, rows, LANES * p:LANES * (p + 1)]
            zero = jnp.zeros_like(r)
            rhs = jnp.concatenate([jnp.where(low_head, r, zero), jnp.where(low_head, zero, r)], axis=0)
            mixed.append(jnp.dot(wmix[p], rhs, preferred_element_type=F32))
        for run in end_fill[j]:
            run()
        ya = [outs[hd // GQA][(hd % GQA) * WINDOW:(hd % GQA + 1) * WINDOW] / dens[hd] for hd in range(N_HEADS)]
        cat_s[oth, rows, 0:Q_W] = _rms(jnp.concatenate(ya, axis=1), ga_ref[...]).astype(BF16)
        yg = u_s[oth, rows, :] * (jnp.concatenate(mixed, axis=1) + bias_ref[...])
        cat_s[oth, rows, Q_W:] = _rms(yg, gg_ref[...]).astype(BF16)


def _mix_prompt(x, gmix, w_in, sinks, gvg, wcat, bias_full, ga, gg, w_out):
    b, s, _ = x.shape
    tm = TOKEN_TILE
    assert s % tm == 0 and tm % WINDOW == 0
    tiles_per_seq = s // tm
    n_tiles = b * tiles_per_seq
    cos_off, sin_off = _rope_tables(jnp.arange(tm, dtype=jnp.int32))
    cos_base, sin_base = _rope_tables(tm * jnp.arange(tiles_per_seq, dtype=jnp.int32))
    proj_tile = lambda i: jnp.minimum(i, n_tiles - 1)
    out_tile = lambda i: jnp.maximum(i - 2, 0)
    x_tiles = x.reshape(n_tiles, tm, D_MODEL)
    last = lambda width: pl.BlockSpec((1, width, WINDOW), lambda i: (proj_tile(i) // tiles_per_seq, 0, 0))
    out, ko, vo, gvo = pl.pallas_call(
        functools.partial(_mix_prompt_kernel, tm=tm, tiles_per_seq=tiles_per_seq, n_tiles=n_tiles),
        grid=(n_tiles + 2,),
        in_specs=[pl.BlockSpec((1, tm, D_MODEL), lambda i: (proj_tile(i), 0, 0)),
                  pl.BlockSpec((1, tm, D_MODEL), lambda i: (out_tile(i), 0, 0)),
                  _resident((tm, LANES)), _resident((tm, LANES)), _resident((tiles_per_seq, LANES)),
                  _resident((tiles_per_seq, LANES)), _resident((1, D_MODEL)), _resident((D_MODEL, D_IN)),
                  pl.BlockSpec(memory_space=pltpu.SMEM), _resident((1, D_GMLP)),
                  _resident((G_HEADS, CHUNK, CHUNK)), _resident((CHUNK, D_GMLP)), _resident((1, Q_W)),
                  _resident((1, D_GMLP)), _resident((D_MODEL, D_MODEL))],
        out_specs=[pl.BlockSpec((1, tm, D_MODEL), lambda i: (out_tile(i), 0, 0)), last(KV_W), last(KV_W),
                   last(D_GMLP)],
        out_shape=[jax.ShapeDtypeStruct((n_tiles, tm, D_MODEL), F32), jax.ShapeDtypeStruct((b, KV_W, WINDOW), F32),
                   jax.ShapeDtypeStruct((b, KV_W, WINDOW), F32), jax.ShapeDtypeStruct((b, D_GMLP, CHUNK), F32)],
        scratch_shapes=[pltpu.VMEM((2, tm, Q_W), BF16), pltpu.VMEM((2, tm + WINDOW, KV_W), BF16),
                        pltpu.VMEM((2, tm + WINDOW, KV_W), BF16), pltpu.VMEM((2, tm, D_GMLP), F32),
                        pltpu.VMEM((2, tm, D_GMLP), BF16), pltpu.VMEM((2, tm, D_MODEL), BF16)],
        compiler_params=pltpu.CompilerParams(dimension_semantics=("arbitrary",), vmem_limit_bytes=VMEM_LIMIT_BYTES),
        name="mix_prompt",
    )(x_tiles, x_tiles, cos_off, sin_off, cos_base, sin_base, gmix, w_in, sinks, gvg, wcat, bias_full, ga, gg, w_out)
    return out.reshape(b, s, D_MODEL), ko, vo, gvo


def _mix_sample_kernel(x_ref, ckt_ref, cvt_ref, cos_ref, sin_ref, gmix_ref, win_ref, sinks_ref, gvg_ref, coef_ref,
                       biasr_ref, ga_ref, gg_ref, wout_ref, o_ref, kot_ref, vot_ref, gvo_ref, q_s, k_s, v_s, ya_s,
                       yg_s, *, t_new, w_buf):
    step = pl.program_id(0)
    n_tok = x_ref.shape[0]
    step_seqs = SEQ_GROUP * GROUPS_PER_STEP
    step_rows = step_seqs * t_new
    grp_rows = SEQ_GROUP * t_new
    grp_keys = SEQ_GROUP * w_buf
    sub = 8

    @pl.when(step == 0)
    def _():
        tile = lambda tab: jnp.broadcast_to(tab[None], (n_tok // sub, sub, LANES)).reshape(n_tok, LANES)
        q, k, v, u, gvn = _in_proj(x_ref[...], gmix_ref[...], win_ref[...], tile(cos_ref[...]),
                                   tile(_sign_sin(sin_ref[...])), gvg_ref[...])
        q_s[...] = q
        k_s[...] = k
        v_s[...] = v
        gv_seq = gvn.reshape(n_tok // t_new, t_new, D_GMLP)
        for tok in range(t_new):
            gvo_ref[tok] = gv_seq[:, tok, :].T
        g3 = gvn.reshape(n_tok // sub, sub, D_GMLP)
        trow = lax.broadcasted_iota(jnp.int32, (1, sub, D_GMLP), 1) & (t_new - 1)
        mixed = biasr_ref[...][None] + coef_ref[0][None] * g3
        for d in range(1, t_new):
            shifted = jnp.where(trow >= d, pltpu.roll(g3, d, 1), 0.0)
            mixed = mixed + coef_ref[d][None] * shifted
        yg_s[...] = (u.reshape(n_tok // sub, sub, D_GMLP) * mixed).reshape(n_tok, D_GMLP)

    row0 = pl.multiple_of(step * step_rows, step_rows)
    q_step = q_s[pl.ds(row0, step_rows), :]
    kn = k_s[pl.ds(row0, step_rows), :]
    vn = v_s[pl.ds(row0, step_rows), :]

    knt = kn.T
    vnt = vn.T
    tail = lax.broadcasted_iota(jnp.int32, (KV_W, w_buf), 1) >= w_buf - t_new
    for b in range(step_seqs):
        shift = (w_buf - t_new - t_new * b) % w_buf
        kot_ref[b] = jnp.where(tail, pltpu.roll(knt, shift, 1) if shift else knt,
                               pltpu.roll(ckt_ref[b], w_buf - t_new, 1))
        vot_ref[b] = jnp.where(tail, pltpu.roll(vnt, shift, 1) if shift else vnt,
                               pltpu.roll(cvt_ref[b], w_buf - t_new, 1))

    knb = kn.astype(BF16)
    vnb = vn.astype(BF16)

    n_rows = GQA * grp_rows
    shift_t = t_new.bit_length() - 1
    shift_w = w_buf.bit_length() - 1
    r = lax.broadcasted_iota(jnp.int32, (n_rows, grp_keys), 0)
    c = lax.broadcasted_iota(jnp.int32, (n_rows, grp_keys), 1)
    mask_c = ((c >> shift_w) == ((r & (grp_rows - 1)) >> shift_t)) & ((c & (w_buf - 1)) > (r & (t_new - 1)))
    r2 = lax.broadcasted_iota(jnp.int32, (n_rows, grp_rows), 0)
    c2 = lax.broadcasted_iota(jnp.int32, (n_rows, grp_rows), 1)
    mask_n = ((c2 >> shift_t) == ((r2 & (grp_rows - 1)) >> shift_t)) & ((c2 & (t_new - 1)) <= (r2 & (t_new - 1)))
    row_head = lax.broadcasted_iota(jnp.int32, (n_rows, 1), 0) >> (grp_rows.bit_length() - 1)

    for grp in range(GROUPS_PER_STEP):
        rows = slice(grp * grp_rows, (grp + 1) * grp_rows)
        seqs = range(grp * SEQ_GROUP, (grp + 1) * SEQ_GROUP)
        for kv in range(N_KV):
            heads = [kv * GQA + i for i in range(GQA)]
            lanes = slice(HEAD_DIM * kv, HEAD_DIM * (kv + 1))
            kt = jnp.concatenate([ckt_ref[b, lanes, :] for b in seqs], axis=1).astype(BF16)
            vt = jnp.concatenate([cvt_ref[b, lanes, :] for b in seqs], axis=1).astype(BF16)
            qs = jnp.concatenate([q_step[rows, HEAD_DIM * hd:HEAD_DIM * (hd + 1)] for hd in heads], axis=0)
            s_c = jnp.where(mask_c, jnp.dot(qs, kt, preferred_element_type=F32), -jnp.inf)
            s_n = jnp.where(mask_n, lax.dot_general(qs, knb[rows, lanes], _NT, preferred_element_type=F32), -jnp.inf)
            sink = jnp.full((n_rows, 1), sinks_ref[heads[0]] * LOG2E, F32)
            for i in range(1, GQA):
                sink = jnp.where(row_head == i, sinks_ref[heads[i]] * LOG2E, sink)
            m = jnp.maximum(jnp.maximum(jnp.max(s_c, axis=1, keepdims=True), jnp.max(s_n, axis=1, keepdims=True)),
                            sink)
            p_c = jnp.exp2(s_c - m)
            p_n = jnp.exp2(s_n - m)
            den = jnp.sum(p_c, axis=1, keepdims=True) + jnp.sum(p_n, axis=1, keepdims=True) + jnp.exp2(sink - m)
            o = (lax.dot_general(p_c.astype(BF16), vt, _NT, preferred_element_type=F32)
                 + jnp.dot(p_n.astype(BF16), vnb[rows, lanes], preferred_element_type=F32)) / den
            for i, hd in enumerate(heads):
                ya_s[pl.ds(row0 + grp * grp_rows, grp_rows), HEAD_DIM * hd:HEAD_DIM * (hd + 1)] = (
                    o[i * grp_rows:(i + 1) * grp_rows])

    @pl.when(step == pl.num_programs(0) - 1)
    def _():
        o_ref[...] = _out_proj(x_ref[...], ya_s[...], yg_s[...], ga_ref[...], gg_ref[...], wout_ref[...])


def _mix_sample(x, cache_kt, cache_vt, cos, sin, gmix, w_in, sinks, gvg, coef, bias_rows, ga, gg, w_out, t_new):
    n_tok = x.shape[0]
    n_seq, _, w_buf = cache_kt.shape
    step_seqs = SEQ_GROUP * GROUPS_PER_STEP
    assert n_seq % step_seqs == 0 and n_tok == n_seq * t_new
    assert t_new & (t_new - 1) == 0 and w_buf & (w_buf - 1) == 0 and 8 % t_new == 0
    assert step_seqs * t_new == w_buf == LANES
    cache_spec = pl.BlockSpec((step_seqs, KV_W, w_buf), lambda i: (i, 0, 0))
    return pl.pallas_call(
        functools.partial(_mix_sample_kernel, t_new=t_new, w_buf=w_buf),
        grid=(n_seq // step_seqs,),
        in_specs=[_resident((n_tok, D_MODEL)), cache_spec, cache_spec, _resident((8, LANES)),
                  _resident((8, LANES)), _resident((1, D_MODEL)), _resident((D_MODEL, D_IN)),
                  pl.BlockSpec(memory_space=pltpu.SMEM), _resident((1, D_GMLP)), _resident((t_new, 8, D_GMLP)),
                  _resident((8, D_GMLP)), _resident((1, Q_W)), _resident((1, D_GMLP)),
                  _resident((D_MODEL, D_MODEL))],
        out_specs=[pl.BlockSpec((n_tok, D_MODEL), lambda i: (0, 0)), cache_spec, cache_spec,
                   pl.BlockSpec((t_new, D_GMLP, n_seq), lambda i: (0, 0, 0))],
        out_shape=[jax.ShapeDtypeStruct((n_tok, D_MODEL), F32), jax.ShapeDtypeStruct(cache_kt.shape, F32),
                   jax.ShapeDtypeStruct(cache_vt.shape, F32), jax.ShapeDtypeStruct((t_new, D_GMLP, n_seq), F32)],
        scratch_shapes=[pltpu.VMEM((n_tok, Q_W), BF16), pltpu.VMEM((n_tok, KV_W), F32), pltpu.VMEM((n_tok, KV_W), F32),
                        pltpu.VMEM((n_tok, Q_W), F32), pltpu.VMEM((n_tok, D_GMLP), F32)],
        compiler_params=pltpu.CompilerParams(dimension_semantics=("arbitrary",), vmem_limit_bytes=VMEM_LIMIT_BYTES),
        name="mix_sample",
    )(x, cache_kt, cache_vt, cos, sin, gmix, w_in, sinks, gvg, coef, bias_rows, ga, gg, w_out)


def _gmlp_tables_kernel(ws_ref, bs_ref, bias_ref, coef_ref, biasr_ref, *, t_new):
    sub = 8
    expand = lambda col: jnp.broadcast_to(col, (col.shape[0], D_GMLP // G_HEADS))
    row = lax.broadcasted_iota(jnp.int32, (CHUNK, CHUNK), 0)
    lane = lax.broadcasted_iota(jnp.int32, (CHUNK, CHUNK), 1)
    bias = jnp.concatenate(
        [expand(jnp.sum(jnp.where(lane == row, bs_ref[hd:hd + 1, :], 0.0), axis=1, keepdims=True))
         for hd in range(G_HEADS)], axis=1)
    bias_ref[...] = bias
    biasr_ref[...] = jnp.concatenate([bias[0:t_new]] * (sub // t_new), axis=0)
    t = lax.broadcasted_iota(jnp.int32, (sub, CHUNK), 0) & (t_new - 1)
    s = lax.broadcasted_iota(jnp.int32, (sub, CHUNK), 1)
    for d in range(t_new):
        cols = []
        for hd in range(G_HEADS):
            w_rows = jnp.concatenate([ws_ref[hd, 0:t_new, :]] * (sub // t_new), axis=0)
            cols.append(expand(jnp.sum(jnp.where(s == t - d, w_rows, 0.0), axis=1, keepdims=True)))
        coef_ref[d] = jnp.concatenate(cols, axis=1)


def _gmlp_tables(ws, bs, t_new):
    assert 8 % t_new == 0
    return pl.pallas_call(
        functools.partial(_gmlp_tables_kernel, t_new=t_new),
        out_shape=[jax.ShapeDtypeStruct((CHUNK, D_GMLP), F32), jax.ShapeDtypeStruct((t_new, 8, D_GMLP), F32),
                   jax.ShapeDtypeStruct((8, D_GMLP), F32)],
        name="gmlp_tables",
    )(ws, bs)


def _rope_tables(pos):
    inv_freq = ROPE_THETA ** (-jnp.arange(0, HEAD_DIM, 2, dtype=F32) / HEAD_DIM)
    ang = pos.astype(F32)[:, None] * jnp.tile(inv_freq, 2 * LANES // HEAD_DIM)[None, :]
    return jnp.cos(ang), jnp.sin(ang)


def kernel(x_prompt, x_sample, cache_k_win, cache_v_win, norm_ffn1, ffn1_gate, ffn1_up, ffn1_down, norm_mix, w_in,
           attn_sinks, gmlp_v_norm, gmlp_w_s, gmlp_b_s, norm_attn_out, norm_gmlp_out, w_out, norm_ffn2, ffn2_gate,
           ffn2_up, ffn2_down, norm_final):
    depth = norm_ffn1.shape[0]
    b, s, _ = x_prompt.shape
    bd, t_new, _ = x_sample.shape
    w_buf = cache_k_win.shape[2]

    cos_s, sin_s = _rope_tables(PAST_LEN + jnp.arange(t_new, dtype=jnp.int32))
    cos_s, sin_s = jnp.tile(cos_s, (8 // t_new, 1)), jnp.tile(sin_s, (8 // t_new, 1))

    hp = x_prompt.reshape(b * s, D_MODEL)
    hs = x_sample.reshape(bd * t_new, D_MODEL)
    outs = [[] for _ in range(6)]
    ffn1_w = [ffn1_gate[0], ffn1_up[0], ffn1_down[0]]
    for l in range(depth):
        last = l == depth - 1
        row = lambda a: a[l].reshape(1, -1)

        wcat = gmlp_w_s[l]
        bias_full, coef, bias_rows = _gmlp_tables(gmlp_w_s[l], gmlp_b_s[l], t_new)

        hp, hs, (w_in_b, w_out_b, wg2, wu2, wd2) = (_ffn_first if l == 0 else _ffn_half)(
            hp, hs, norm_ffn1[l], *ffn1_w, cast=(w_in[l], w_out[l], ffn2_gate[l], ffn2_up[l], ffn2_down[l]))

        hp, kpt, vpt, gvpt = _mix_prompt(hp.reshape(b, s, D_MODEL), row(norm_mix), w_in_b, attn_sinks[l],
                                         row(gmlp_v_norm), wcat, bias_full, row(norm_attn_out), row(norm_gmlp_out),
                                         w_out_b)
        to_t = lambda c: c.transpose(0, 2, 3, 1).reshape(bd, KV_W, w_buf)
        hs, kst, vst, gvs = _mix_sample(hs, to_t(cache_k_win[l]), to_t(cache_v_win[l]), cos_s, sin_s, row(norm_mix),
                                        w_in_b, attn_sinks[l], row(gmlp_v_norm), coef, bias_rows, row(norm_attn_out),
                                        row(norm_gmlp_out), w_out_b, t_new)
        from_t = lambda c: c.reshape(bd, N_KV, HEAD_DIM, w_buf).transpose(0, 3, 1, 2)

        next_ffn1 = () if last else (ffn1_gate[l + 1], ffn1_up[l + 1], ffn1_down[l + 1])
        hp, hs, ffn1_w = _ffn_half(hp.reshape(b * s, D_MODEL), hs, norm_ffn2[l], wg2, wu2, wd2,
                                   gf=norm_final if last else None, cast=next_ffn1,
                                   sample_out_shape=(bd, t_new, D_MODEL) if last else None)

        outs[0].append(kpt.reshape(b, N_KV, HEAD_DIM, WINDOW).transpose(0, 3, 1, 2))
        outs[1].append(vpt.reshape(b, N_KV, HEAD_DIM, WINDOW).transpose(0, 3, 1, 2))
        outs[2].append(from_t(kst))
        outs[3].append(from_t(vst))
        outs[4].append(gvpt.reshape(b, G_HEADS, D_GMLP // G_HEADS, CHUNK).transpose(0, 3, 1, 2))
        outs[5].append(gvs.reshape(t_new, G_HEADS, D_GMLP // G_HEADS, bd).transpose(3, 0, 1, 2))

    return (hp.reshape(b, s, D_MODEL), hs) + tuple(jnp.stack(o) for o in outs)
```

```python
import functools

import jax
import jax.numpy as jnp
from jax import lax
from jax.experimental import pallas as pl
from jax.experimental.pallas import tpu as pltpu

F32 = jnp.float32
BF16 = jnp.bfloat16

D_MODEL = 1024
D_FF = 2816
HEAD_DIM = 64
N_HEADS = 8
N_KV = 2
GQA = N_HEADS // N_KV
WINDOW = 128
CHUNK = 128
G_HEADS = 8
Q_W = N_HEADS * HEAD_DIM
KV_W = N_KV * HEAD_DIM
D_GMLP = 512
D_IN = Q_W + 2 * KV_W + 2 * D_GMLP
K_OFF = Q_W
V_OFF = K_OFF + KV_W
U_OFF = V_OFF + KV_W
GV_OFF = U_OFF + D_GMLP
ROPE_THETA = 10000.0
PAST_LEN = 16384
EPS = 1e-6
LOG2E = 1.4426950408889634
Q_SCALE = HEAD_DIM ** -0.5 * LOG2E
LANES = 128
BF16_SUBLANES = 16

TOKEN_TILE = 512
FF_CHUNK = 256
STAGE_SLOTS = 3
SEQ_GROUP = 8
GROUPS_PER_STEP = 4
VMEM_LIMIT_BYTES = 56 * 1024 * 1024

_NT = (((1,), (1,)), ((), ()))


def _rms(x, g):
    ms = jnp.mean(x * x, axis=-1, keepdims=True)
    return (x * lax.rsqrt(ms + EPS)) * g


def _resident(shape):
    zeros = (0,) * len(shape)
    return pl.BlockSpec(shape, lambda *_: zeros, pipeline_mode=pl.Buffered(1))


def _silu_gate(h, wg, wu):
    gate = jnp.dot(h, wg, preferred_element_type=F32)
    up = jnp.dot(h, wu, preferred_element_type=F32)
    return (gate * jax.nn.sigmoid(gate) * up).astype(BF16)


def _ffn_first_kernel(*refs, n_cast):
    refs = iter(refs)
    xp_ref, xn_ref, xs_ref, g_ref, wg_hbm, wu_hbm, wd_hbm = (next(refs) for _ in range(7))
    cast_in = [next(refs) for _ in range(n_cast)]
    yp_ref, ys_ref = next(refs), next(refs)
    cast_out = [next(refs) for _ in range(n_cast)]
    h_ref, act0_ref, act_ref, wg_ref, wu_ref, wd_ref, col_stage, row_stage, sem = (next(refs) for _ in range(9))
    n_chunks = D_FF // FF_CHUNK
    gate_up = lambda h, c: _silu_gate(h, wg_ref[c], wu_ref[c])

    @pl.when(pl.program_id(0) == 0)
    def _():
        def fetch(c, slot):
            span = pl.ds(pl.multiple_of(c * FF_CHUNK, FF_CHUNK), FF_CHUNK)
            return (pltpu.make_async_copy(wg_hbm.at[:, span], col_stage.at[slot, 0], sem.at[slot, 0]),
                    pltpu.make_async_copy(wu_hbm.at[:, span], col_stage.at[slot, 1], sem.at[slot, 1]),
                    pltpu.make_async_copy(wd_hbm.at[span, :], row_stage.at[slot], sem.at[slot, 2]))

        for c in range(STAGE_SLOTS - 1):
            for copy in fetch(c, c):
                copy.start()
        yp_ref[...] = xs_ref[...].reshape(yp_ref.shape)
        h_ref[...] = _rms(yp_ref[...], g_ref[...]).astype(BF16)
        ys_ref[...] = jnp.zeros(ys_ref.shape, F32)

        def body(c, carry):
            slot = c % STAGE_SLOTS
            ahead = c + STAGE_SLOTS - 1

            @pl.when(ahead < n_chunks)
            def _():
                for copy in fetch(ahead, ahead % STAGE_SLOTS):
                    copy.start()

            for copy in fetch(c, slot):
                copy.wait()
            rows = pl.ds(pl.multiple_of(c * FF_CHUNK, FF_CHUNK), FF_CHUNK)
            wg_ref[c] = col_stage[slot, 0].astype(BF16)
            wu_ref[c] = col_stage[slot, 1].astype(BF16)
            wd_ref[rows, :] = row_stage[slot].astype(BF16)
            ys_ref[...] += jnp.dot(gate_up(h_ref[...], c), wd_ref[rows, :], preferred_element_type=F32)
            return carry

        lax.fori_loop(0, n_chunks, body, 0)
        ys_ref[...] = yp_ref[...] + 0.5 * ys_ref[...]
        h0 = _rms(xp_ref[...], g_ref[...]).astype(BF16)
        h_ref[...] = h0
        act0_ref[...] = gate_up(h0, 0)

    for c in range(1, n_chunks):
        act_ref[:, (c - 1) * FF_CHUNK:c * FF_CHUNK] = gate_up(h_ref[...], c)
    down0 = jnp.dot(act0_ref[...], wd_ref[0:FF_CHUNK, :], preferred_element_type=F32)
    hn = _rms(xn_ref[...], g_ref[...]).astype(BF16)
    h_ref[...] = hn
    act0_ref[...] = gate_up(hn, 0)
    yp_ref[...] = xp_ref[...] + 0.5 * (down0 + jnp.dot(act_ref[...], wd_ref[FF_CHUNK:, :], preferred_element_type=F32))

    for src, dst in zip(cast_in, cast_out):
        dst[...] = src[...].astype(BF16)


def _ffn_kernel(*refs, final_norm, n_cast):
    refs = iter(refs)
    xp_ref, xn_ref, xs_ref, g_ref, wg_ref, wu_ref, wd_ref = (next(refs) for _ in range(7))
    gf_ref = next(refs) if final_norm else None
    cast_in = [next(refs) for _ in range(n_cast)]
    yp_ref, ys_ref = next(refs), next(refs)
    cast_out = [next(refs) for _ in range(n_cast)]
    h_ref, act0_ref, act_ref = next(refs), next(refs), next(refs)

    def gate_up(h, c):
        sl = slice(c * FF_CHUNK, (c + 1) * FF_CHUNK)
        return _silu_gate(h, wg_ref[:, sl], wu_ref[:, sl])

    on_sample = pl.program_id(0) == 0

    @pl.when(on_sample)
    def _():
        h0 = _rms(xs_ref[...], g_ref[...]).astype(BF16)
        h_ref[...] = h0
        act0_ref[...] = gate_up(h0, 0)

    for c in range(1, D_FF // FF_CHUNK):
        act_ref[:, (c - 1) * FF_CHUNK:c * FF_CHUNK] = gate_up(h_ref[...], c)
    down0 = jnp.dot(act0_ref[...], wd_ref[0:FF_CHUNK, :], preferred_element_type=F32)
    hn = _rms(xn_ref[...], g_ref[...]).astype(BF16)
    h_ref[...] = hn
    act0_ref[...] = gate_up(hn, 0)
    x = jnp.where(on_sample, xs_ref[...], xp_ref[...])
    y = x + 0.5 * (down0 + jnp.dot(act_ref[...], wd_ref[FF_CHUNK:, :], preferred_element_type=F32))
    yp_ref[...] = _rms(y, gf_ref[...]) if final_norm else y

    @pl.when(on_sample)
    def _():
        ys_ref[...] = yp_ref[...].reshape(ys_ref.shape)

    for src, dst in zip(cast_in, cast_out):
        dst[...] = src[...].astype(BF16)


def _cast_row_blocks(rows, n_steps):
    return max(d for d in range(1, n_steps + 1) if rows % d == 0 and (rows // d) % BF16_SUBLANES == 0)


def _cast_specs(mats, n_steps):
    specs = []
    for w in mats:
        rows, cols = w.shape
        nb = _cast_row_blocks(rows, n_steps)
        specs.append(pl.BlockSpec((rows // nb, cols), lambda i, nb=nb: (jnp.minimum(i, nb - 1), 0)))
    return specs


def _ffn_half(xp, xs, g, wg, wu, wd, gf=None, cast=(), sample_out_shape=None):
    tm = TOKEN_TILE
    n = xp.shape[0]
    assert n % tm == 0 and xs.shape[0] == tm
    n_tiles = n // tm
    prompt_spec = pl.BlockSpec((tm, D_MODEL), lambda i: (jnp.maximum(i - 1, 0), 0))
    next_spec = pl.BlockSpec((tm, D_MODEL), lambda i: (jnp.minimum(i, n_tiles - 1), 0))
    sample_in = _resident((tm, D_MODEL))
    sample_shape = sample_out_shape or (tm, D_MODEL)
    sample_out = pl.BlockSpec(sample_shape, lambda i: (0,) * len(sample_shape))
    in_specs = [prompt_spec, next_spec, sample_in, _resident((1, D_MODEL)), _resident((D_MODEL, D_FF)),
                _resident((D_MODEL, D_FF)), _resident((D_FF, D_MODEL))]
    args = [xp, xp, xs, g.reshape(1, D_MODEL), wg, wu, wd]
    if gf is not None:
        in_specs.append(_resident((1, D_MODEL)))
        args.append(gf.reshape(1, D_MODEL))
    out_specs = [prompt_spec, sample_out]
    out_shape = [jax.ShapeDtypeStruct((n, D_MODEL), F32), jax.ShapeDtypeStruct(sample_shape, F32)]
    cast_specs = _cast_specs(cast, n_tiles)
    out_shape += [jax.ShapeDtypeStruct(w.shape, BF16) for w in cast]
    outs = pl.pallas_call(
        functools.partial(_ffn_kernel, final_norm=gf is not None, n_cast=len(cast)),
        grid=(n_tiles + 1,),
        in_specs=in_specs + cast_specs,
        out_specs=out_specs + cast_specs,
        out_shape=out_shape,
        scratch_shapes=[pltpu.VMEM((tm, D_MODEL), BF16), pltpu.VMEM((tm, FF_CHUNK), BF16),
                        pltpu.VMEM((tm, D_FF - FF_CHUNK), BF16)],
        compiler_params=pltpu.CompilerParams(dimension_semantics=("arbitrary",), vmem_limit_bytes=VMEM_LIMIT_BYTES),
        name="ffn_final" if gf is not None else "ffn_half",
    )(*args, *cast)
    return outs[0], outs[1], list(outs[2:])


def _ffn_first(xp, xs, g, wg, wu, wd, cast=()):
    tm = TOKEN_TILE
    n = xp.shape[0]
    n_chunks = D_FF // FF_CHUNK
    assert n % tm == 0 and xs.size == tm * D_MODEL and xs.shape[-1] == D_MODEL
    assert D_FF % FF_CHUNK == 0 and n_chunks >= STAGE_SLOTS
    n_tiles = n // tm
    prompt_spec = pl.BlockSpec((tm, D_MODEL), lambda i: (i, 0))
    next_spec = pl.BlockSpec((tm, D_MODEL), lambda i: (jnp.minimum(i + 1, n_tiles - 1), 0))
    hbm = pl.BlockSpec(memory_space=pl.ANY)
    cast_specs = _cast_specs(cast, n_tiles)
    outs = pl.pallas_call(
        functools.partial(_ffn_first_kernel, n_cast=len(cast)),
        grid=(n_tiles,),
        in_specs=[prompt_spec, next_spec, _resident(xs.shape), _resident((1, D_MODEL)), hbm, hbm, hbm] + cast_specs,
        out_specs=[prompt_spec, pl.BlockSpec((tm, D_MODEL), lambda i: (0, 0))] + cast_specs,
        out_shape=[jax.ShapeDtypeStruct((n, D_MODEL), F32), jax.ShapeDtypeStruct((tm, D_MODEL), F32)]
        + [jax.ShapeDtypeStruct(w.shape, BF16) for w in cast],
        scratch_shapes=[pltpu.VMEM((tm, D_MODEL), BF16), pltpu.VMEM((tm, FF_CHUNK), BF16),
                        pltpu.VMEM((tm, D_FF - FF_CHUNK), BF16),
                        pltpu.VMEM((n_chunks, D_MODEL, FF_CHUNK), BF16), pltpu.VMEM((n_chunks, D_MODEL, FF_CHUNK), BF16),
                        pltpu.VMEM((D_FF, D_MODEL), BF16), pltpu.VMEM((STAGE_SLOTS, 2, D_MODEL, FF_CHUNK), F32),
                        pltpu.VMEM((STAGE_SLOTS, FF_CHUNK, D_MODEL), F32), pltpu.SemaphoreType.DMA((STAGE_SLOTS, 3))],
        compiler_params=pltpu.CompilerParams(dimension_semantics=("arbitrary",), vmem_limit_bytes=VMEM_LIMIT_BYTES),
        name="ffn_first",
    )(xp, xp, xs, g.reshape(1, D_MODEL), wg, wu, wd, *cast)
    return outs[0], outs[1], list(outs[2:])


def _first_half(shape):
    return (lax.broadcasted_iota(jnp.int32, shape, 1) & (HEAD_DIM - 1)) < HEAD_DIM // 2


def _sign_sin(sin):
    return jnp.where(_first_half(sin.shape), -sin, sin)


def _rope(xg, cos, sin_signed):
    swapped = jnp.where(_first_half(xg.shape), pltpu.roll(xg, LANES - HEAD_DIM // 2, 1),
                        pltpu.roll(xg, HEAD_DIM // 2, 1))
    return xg * cos + swapped * sin_signed


def _in_proj(x, gmix, w_in, cos, sin_signed, gvn_gain):
    h = _rms(x, gmix).astype(BF16)
    z = jnp.dot(h, w_in, preferred_element_type=F32)
    q = jnp.concatenate(
        [(_rope(z[:, LANES * i:LANES * (i + 1)], cos, sin_signed) * Q_SCALE).astype(BF16) for i in range(Q_W // LANES)],
        axis=1)
    k = _rope(z[:, K_OFF:V_OFF], cos, sin_signed)
    v = z[:, V_OFF:U_OFF]
    u = _gelu(z[:, U_OFF:GV_OFF])
    gvn = _rms(_gelu(z[:, GV_OFF:]), gvn_gain)
    return q, k, v, u, gvn


def _out_proj(x, ya, yg, ga, gg, w_out):
    cat = jnp.concatenate([_rms(ya, ga).astype(BF16), _rms(yg, gg).astype(BF16)], axis=1)
    return x + jnp.dot(cat, w_out, preferred_element_type=F32)


def _softmax_sink(s, sink):
    m = jnp.maximum(jnp.max(s, axis=1, keepdims=True), sink)
    p = jnp.exp2(s - m)
    return p, jnp.sum(p, axis=1, keepdims=True) + jnp.exp2(sink - m)


def _gelu(x):
    k = -2.0 * (2.0 / jnp.pi) ** 0.5 * LOG2E
    return x / (1.0 + jnp.exp2(x * (k + (0.044715 * k) * (x * x))))


def _mix_prompt_kernel(xa_ref, xc_ref, coff_ref, soff_ref, cbase_ref, sbase_ref, gmix_ref, win_ref, sinks_ref, gvg_ref,
                       wcat_ref, bias_ref, ga_ref, gg_ref, wout_ref, o_ref, ko_ref, vo_ref, gvo_ref, q_s, k_s, v_s, u_s,
                       gv_s, cat_s, *, tm, tiles_per_seq, n_tiles):
    s = pl.program_id(0)

    @pl.when(s == 0)
    def _():
        k_s[1, tm:tm + WINDOW, :] = jnp.zeros((WINDOW, KV_W), BF16)
        v_s[1, tm:tm + WINDOW, :] = jnp.zeros((WINDOW, KV_W), BF16)
        cat_s[1] = jnp.zeros((tm, D_MODEL), BF16)

    step = functools.partial(
        _mix_prompt_step, s, xa_ref, xc_ref, coff_ref, soff_ref, cbase_ref, sbase_ref, gmix_ref, win_ref, sinks_ref,
        gvg_ref, wcat_ref, bias_ref, ga_ref, gg_ref, wout_ref, o_ref, ko_ref, vo_ref, gvo_ref, q_s, k_s, v_s, u_s, gv_s,
        cat_s, tm=tm, tiles_per_seq=tiles_per_seq, n_tiles=n_tiles)
    last = n_tiles + 1
    pl.when(s == 0)(functools.partial(step, cur=0, stages="proj"))
    pl.when(s == last)(functools.partial(step, cur=last % 2, stages="out"))
    for parity in range(2):
        pl.when((s > 0) & (s < last) & (s % 2 == parity))(functools.partial(step, cur=parity, stages="all"))


def _mix_prompt_step(s, xa_ref, xc_ref, coff_ref, soff_ref, cbase_ref, sbase_ref, gmix_ref, win_ref, sinks_ref, gvg_ref,
                     wcat_ref, bias_ref, ga_ref, gg_ref, wout_ref, o_ref, ko_ref, vo_ref, gvo_ref, q_s, k_s, v_s, u_s, gv_s,
                     cat_s, *, tm, tiles_per_seq, n_tiles, cur, stages):
    oth = 1 - cur

    h = _rms(xa_ref[0], gmix_ref[...]).astype(BF16)
    tile_in_seq = jnp.minimum(s, n_tiles - 1) % tiles_per_seq
    cb = cbase_ref[pl.ds(tile_in_seq, 1), :]
    sb = sbase_ref[pl.ds(tile_in_seq, 1), :]
    cos = cb * coff_ref[...] - sb * soff_ref[...]
    sin = _sign_sin(sb * coff_ref[...] + cb * soff_ref[...])

    def proj_q():
        z = jnp.dot(h, win_ref[:, 0:Q_W], preferred_element_type=F32)
        for i in range(Q_W // LANES):
            q_s[cur, :, LANES * i:LANES * (i + 1)] = (
                _rope(z[:, LANES * i:LANES * (i + 1)], cos, sin) * Q_SCALE).astype(BF16)

    def proj_kv():
        z = jnp.dot(h, win_ref[:, K_OFF:U_OFF], preferred_element_type=F32)
        k = _rope(z[:, 0:KV_W], cos, sin)
        v = z[:, KV_W:]
        k_s[cur, 0:WINDOW, :] = k_s[oth, tm:tm + WINDOW, :]
        v_s[cur, 0:WINDOW, :] = v_s[oth, tm:tm + WINDOW, :]
        k_s[cur, WINDOW:, :] = k.astype(BF16)
        v_s[cur, WINDOW:, :] = v.astype(BF16)
        ko_ref[0] = k[tm - WINDOW:].T
        vo_ref[0] = v[tm - WINDOW:].T

    def proj_u():
        u_s[cur] = _gelu(jnp.dot(h, win_ref[:, U_OFF:GV_OFF], preferred_element_type=F32))

    def proj_gv():
        gvn = _rms(_gelu(jnp.dot(h, win_ref[:, GV_OFF:], preferred_element_type=F32)), gvg_ref[...])
        gv_s[cur] = gvn.astype(BF16)
        gvo_ref[0] = gvn[tm - CHUNK:].T

    def out_half(c):
        cols = slice(c * (D_MODEL // 2), (c + 1) * (D_MODEL // 2))

        def run():
            o_ref[0, :, cols] = xc_ref[0, :, cols] + jnp.dot(cat_s[cur], wout_ref[:, cols], preferred_element_type=F32)
        return run

    mid_fill = [[out_half(0)], [proj_q], [proj_kv, proj_u], [proj_gv]]
    end_fill = [[], [], [], [out_half(1)]]
    assert len(mid_fill) == len(end_fill) == tm // WINDOW
    if stages == "proj":
        for run in (proj_q, proj_kv, proj_u, proj_gv):
            run()
        return
    if stages == "out":
        out_half(0)()
        out_half(1)()
        return

    wrow = lax.broadcasted_iota(jnp.int32, (CHUNK, CHUNK), 0)
    wcol = lax.broadcasted_iota(jnp.int32, (CHUNK, CHUNK), 1)
    wtril = [jnp.where(wcol <= wrow, wcat_ref[hd], 0.0).astype(BF16) for hd in range(G_HEADS)]
    wmix = [jnp.concatenate(wtril[2 * p:2 * p + 2], axis=1) for p in range(G_HEADS // 2)]
    lane = lax.broadcasted_iota(jnp.int32, (CHUNK, LANES), 1)
    low_head = lane < HEAD_DIM

    qi = lax.broadcasted_iota(jnp.int32, (WINDOW, 2 * WINDOW), 0)
    sj = lax.broadcasted_iota(jnp.int32, (WINDOW, 2 * WINDOW), 1)
    dist = WINDOW + qi - sj
    band = (dist >= 0) & (dist < WINDOW)
    first_lo = jnp.where((s + tiles_per_seq - 1) % tiles_per_seq == 0, WINDOW, 0)

    for j in range(tm // WINDOW):
        rows = slice(j * WINDOW, (j + 1) * WINDOW)
        mask = band & (sj >= first_lo) if j == 0 else band
        qb = q_s[oth, rows, :]
        scores = []
        for kv in range(N_KV):
            qs = jnp.concatenate(
                [qb[:, HEAD_DIM * hd:HEAD_DIM * (hd + 1)] for hd in range(kv * GQA, (kv + 1) * GQA)], axis=0)
            kb = k_s[oth, j * WINDOW:(j + 2) * WINDOW, HEAD_DIM * kv:HEAD_DIM * (kv + 1)]
            scores.append(lax.dot_general(qs, kb, _NT, preferred_element_type=F32))
        for run in mid_fill[j]:
            run()
        outs, dens = [], []
        for kv in range(N_KV):
            vb = v_s[oth, j * WINDOW:(j + 2) * WINDOW, HEAD_DIM * kv:HEAD_DIM * (kv + 1)]
            ps = []
            for g in range(GQA):
                sg = jnp.where(mask, scores[kv][g * WINDOW:(g + 1) * WINDOW], -jnp.inf)
                p, den = _softmax_sink(sg, sinks_ref[kv * GQA + g] * LOG2E)
                ps.append(p.astype(BF16))
                dens.append(den)
            outs.append(jnp.dot(jnp.concatenate(ps, axis=0), vb, preferred_element_type=F32))
        mixed = []
        for p in range(G_HEADS // 2):
            r = gv_s[oth, rows, LANES * p:LANES * (p + 1)]
            zero = jnp.zeros_like(r)
            rhs = jnp.concatenate([jnp.where(low_head, r, zero), jnp.where(low_head, zero, r)], axis=0)
            mixed.append(jnp.dot(wmix[p], rhs, preferred_element_type=F32))
        for run in end_fill[j]:
            run()
        ya = [outs[hd // GQA][(hd % GQA) * WINDOW:(hd % GQA + 1) * WINDOW] / dens[hd] for hd in range(N_HEADS)]
        cat_s[oth, rows, 0:Q_W] = _rms(jnp.concatenate(ya, axis=1), ga_ref[...]).astype(BF16)
        yg = u_s[oth, rows, :] * (jnp.concatenate(mixed, axis=1) + bias_ref[...])
        cat_s[oth, rows, Q_W:] = _rms(yg, gg_ref[...]).astype(BF16)


def _mix_prompt(x, gmix, w_in, sinks, gvg, wcat, bias_full, ga, gg, w_out):
    b, s, _ = x.shape
    tm = TOKEN_TILE
    assert s % tm == 0 and tm % WINDOW == 0
    tiles_per_seq = s // tm
    n_tiles = b * tiles_per_seq
    cos_off, sin_off = _rope_tables(jnp.arange(tm, dtype=jnp.int32))
    cos_base, sin_base = _rope_tables(tm * jnp.arange(tiles_per_seq, dtype=jnp.int32))
    proj_tile = lambda i: jnp.minimum(i, n_tiles - 1)
    out_tile = lambda i: jnp.maximum(i - 2, 0)
    x_tiles = x.reshape(n_tiles, tm, D_MODEL)
    last = lambda width: pl.BlockSpec((1, width, WINDOW), lambda i: (proj_tile(i) // tiles_per_seq, 0, 0))
    out, ko, vo, gvo = pl.pallas_call(
        functools.partial(_mix_prompt_kernel, tm=tm, tiles_per_seq=tiles_per_seq, n_tiles=n_tiles),
        grid=(n_tiles + 2,),
        in_specs=[pl.BlockSpec((1, tm, D_MODEL), lambda i: (proj_tile(i), 0, 0)),
                  pl.BlockSpec((1, tm, D_MODEL), lambda i: (out_tile(i), 0, 0)),
                  _resident((tm, LANES)), _resident((tm, LANES)), _resident((tiles_per_seq, LANES)),
                  _resident((tiles_per_seq, LANES)), _resident((1, D_MODEL)), _resident((D_MODEL, D_IN)),
                  pl.BlockSpec(memory_space=pltpu.SMEM), _resident((1, D_GMLP)),
                  _resident((G_HEADS, CHUNK, CHUNK)), _resident((CHUNK, D_GMLP)), _resident((1, Q_W)),
                  _resident((1, D_GMLP)), _resident((D_MODEL, D_MODEL))],
        out_specs=[pl.BlockSpec((1, tm, D_MODEL), lambda i: (out_tile(i), 0, 0)), last(KV_W), last(KV_W),
                   last(D_GMLP)],
        out_shape=[jax.ShapeDtypeStruct((n_tiles, tm, D_MODEL), F32), jax.ShapeDtypeStruct((b, KV_W, WINDOW), F32),
                   jax.ShapeDtypeStruct((b, KV_W, WINDOW), F32), jax.ShapeDtypeStruct((b, D_GMLP, CHUNK), F32)],
        scratch_shapes=[pltpu.VMEM((2, tm, Q_W), BF16), pltpu.VMEM((2, tm + WINDOW, KV_W), BF16),
                        pltpu.VMEM((2, tm + WINDOW, KV_W), BF16), pltpu.VMEM((2, tm, D_GMLP), F32),
                        pltpu.VMEM((2, tm, D_GMLP), BF16), pltpu.VMEM((2, tm, D_MODEL), BF16)],
        compiler_params=pltpu.CompilerParams(dimension_semantics=("arbitrary",), vmem_limit_bytes=VMEM_LIMIT_BYTES),
        name="mix_prompt",
    )(x_tiles, x_tiles, cos_off, sin_off, cos_base, sin_base, gmix, w_in, sinks, gvg, wcat, bias_full, ga, gg, w_out)
    return out.reshape(b, s, D_MODEL), ko, vo, gvo


def _mix_sample_kernel(x_ref, ckt_ref, cvt_ref, cos_ref, sin_ref, gmix_ref, win_ref, sinks_ref, gvg_ref, coef_ref,
                       biasr_ref, ga_ref, gg_ref, wout_ref, o_ref, kot_ref, vot_ref, gvo_ref, q_s, k_s, v_s, ya_s,
                       yg_s, *, t_new, w_buf):
    step = pl.program_id(0)
    n_tok = x_ref.shape[0]
    step_seqs = SEQ_GROUP * GROUPS_PER_STEP
    step_rows = step_seqs * t_new
    grp_rows = SEQ_GROUP * t_new
    grp_keys = SEQ_GROUP * w_buf
    sub = 8

    @pl.when(step == 0)
    def _():
        tile = lambda tab: jnp.broadcast_to(tab[None], (n_tok // sub, sub, LANES)).reshape(n_tok, LANES)
        q, k, v, u, gvn = _in_proj(x_ref[...], gmix_ref[...], win_ref[...], tile(cos_ref[...]),
                                   tile(_sign_sin(sin_ref[...])), gvg_ref[...])
        q_s[...] = q
        k_s[...] = k
        v_s[...] = v
        gv_seq = gvn.reshape(n_tok // t_new, t_new, D_GMLP)
        for tok in range(t_new):
            gvo_ref[tok] = gv_seq[:, tok, :].T
        g3 = gvn.reshape(n_tok // sub, sub, D_GMLP)
        trow = lax.broadcasted_iota(jnp.int32, (1, sub, D_GMLP), 1) & (t_new - 1)
        mixed = biasr_ref[...][None] + coef_ref[0][None] * g3
        for d in range(1, t_new):
            shifted = jnp.where(trow >= d, pltpu.roll(g3, d, 1), 0.0)
            mixed = mixed + coef_ref[d][None] * shifted
        yg_s[...] = (u.reshape(n_tok // sub, sub, D_GMLP) * mixed).reshape(n_tok, D_GMLP)

    row0 = pl.multiple_of(step * step_rows, step_rows)
    q_step = q_s[pl.ds(row0, step_rows), :]
    kn = k_s[pl.ds(row0, step_rows), :]
    vn = v_s[pl.ds(row0, step_rows), :]

    knt = kn.T
    vnt = vn.T
    tail = lax.broadcasted_iota(jnp.int32, (KV_W, w_buf), 1) >= w_buf - t_new
    for b in range(step_seqs):
        shift = (w_buf - t_new - t_new * b) % w_buf
        kot_ref[b] = jnp.where(tail, pltpu.roll(knt, shift, 1) if shift else knt,
                               pltpu.roll(ckt_ref[b], w_buf - t_new, 1))
        vot_ref[b] = jnp.where(tail, pltpu.roll(vnt, shift, 1) if shift else vnt,
                               pltpu.roll(cvt_ref[b], w_buf - t_new, 1))

    knb = kn.astype(BF16)
    vnb = vn.astype(BF16)

    n_rows = GQA * grp_rows
    shift_t = t_new.bit_length() - 1
    shift_w = w_buf.bit_length() - 1
    r = lax.broadcasted_iota(jnp.int32, (n_rows, grp_keys), 0)
    c = lax.broadcasted_iota(jnp.int32, (n_rows, grp_keys), 1)
    mask_c = ((c >> shift_w) == ((r & (grp_rows - 1)) >> shift_t)) & ((c & (w_buf - 1)) > (r & (t_new - 1)))
    r2 = lax.broadcasted_iota(jnp.int32, (n_rows, grp_rows), 0)
    c2 = lax.broadcasted_iota(jnp.int32, (n_rows, grp_rows), 1)
    mask_n = ((c2 >> shift_t) == ((r2 & (grp_rows - 1)) >> shift_t)) & ((c2 & (t_new - 1)) <= (r2 & (t_new - 1)))
    row_head = lax.broadcasted_iota(jnp.int32, (n_rows, 1), 0) >> (grp_rows.bit_length() - 1)

    for grp in range(GROUPS_PER_STEP):
        rows = slice(grp * grp_rows, (grp + 1) * grp_rows)
        seqs = range(grp * SEQ_GROUP, (grp + 1) * SEQ_GROUP)
        for kv in range(N_KV):
            heads = [kv * GQA + i for i in range(GQA)]
            lanes = slice(HEAD_DIM * kv, HEAD_DIM * (kv + 1))
            kt = jnp.concatenate([ckt_ref[b, lanes, :] for b in seqs], axis=1).astype(BF16)
            vt = jnp.concatenate([cvt_ref[b, lanes, :] for b in seqs], axis=1).astype(BF16)
            qs = jnp.concatenate([q_step[rows, HEAD_DIM * hd:HEAD_DIM * (hd + 1)] for hd in heads], axis=0)
            s_c = jnp.where(mask_c, jnp.dot(qs, kt, preferred_element_type=F32), -jnp.inf)
            s_n = jnp.where(mask_n, lax.dot_general(qs, knb[rows, lanes], _NT, preferred_element_type=F32), -jnp.inf)
            sink = jnp.full((n_rows, 1), sinks_ref[heads[0]] * LOG2E, F32)
            for i in range(1, GQA):
                sink = jnp.where(row_head == i, sinks_ref[heads[i]] * LOG2E, sink)
            m = jnp.maximum(jnp.maximum(jnp.max(s_c, axis=1, keepdims=True), jnp.max(s_n, axis=1, keepdims=True)),
                            sink)
            p_c = jnp.exp2(s_c - m)
            p_n = jnp.exp2(s_n - m)
            den = jnp.sum(p_c, axis=1, keepdims=True) + jnp.sum(p_n, axis=1, keepdims=True) + jnp.exp2(sink - m)
            o = (lax.dot_general(p_c.astype(BF16), vt, _NT, preferred_element_type=F32)
                 + jnp.dot(p_n.astype(BF16), vnb[rows, lanes], preferred_element_type=F32)) / den
            for i, hd in enumerate(heads):
                ya_s[pl.ds(row0 + grp * grp_rows, grp_rows), HEAD_DIM * hd:HEAD_DIM * (hd + 1)] = (
                    o[i * grp_rows:(i + 1) * grp_rows])

    @pl.when(step == pl.num_programs(0) - 1)
    def _():
        o_ref[...] = _out_proj(x_ref[...], ya_s[...], yg_s[...], ga_ref[...], gg_ref[...], wout_ref[...])


def _mix_sample(x, cache_kt, cache_vt, cos, sin, gmix, w_in, sinks, gvg, coef, bias_rows, ga, gg, w_out, t_new):
    n_tok = x.shape[0]
    n_seq, _, w_buf = cache_kt.shape
    step_seqs = SEQ_GROUP * GROUPS_PER_STEP
    assert n_seq % step_seqs == 0 and n_tok == n_seq * t_new
    assert t_new & (t_new - 1) == 0 and w_buf & (w_buf - 1) == 0 and 8 % t_new == 0
    assert step_seqs * t_new == w_buf == LANES
    cache_spec = pl.BlockSpec((step_seqs, KV_W, w_buf), lambda i: (i, 0, 0))
    return pl.pallas_call(
        functools.partial(_mix_sample_kernel, t_new=t_new, w_buf=w_buf),
        grid=(n_seq // step_seqs,),
        in_specs=[_resident((n_tok, D_MODEL)), cache_spec, cache_spec, _resident((8, LANES)),
                  _resident((8, LANES)), _resident((1, D_MODEL)), _resident((D_MODEL, D_IN)),
                  pl.BlockSpec(memory_space=pltpu.SMEM), _resident((1, D_GMLP)), _resident((t_new, 8, D_GMLP)),
                  _resident((8, D_GMLP)), _resident((1, Q_W)), _resident((1, D_GMLP)),
                  _resident((D_MODEL, D_MODEL))],
        out_specs=[pl.BlockSpec((n_tok, D_MODEL), lambda i: (0, 0)), cache_spec, cache_spec,
                   pl.BlockSpec((t_new, D_GMLP, n_seq), lambda i: (0, 0, 0))],
        out_shape=[jax.ShapeDtypeStruct((n_tok, D_MODEL), F32), jax.ShapeDtypeStruct(cache_kt.shape, F32),
                   jax.ShapeDtypeStruct(cache_vt.shape, F32), jax.ShapeDtypeStruct((t_new, D_GMLP, n_seq), F32)],
        scratch_shapes=[pltpu.VMEM((n_tok, Q_W), BF16), pltpu.VMEM((n_tok, KV_W), F32), pltpu.VMEM((n_tok, KV_W), F32),
                        pltpu.VMEM((n_tok, Q_W), F32), pltpu.VMEM((n_tok, D_GMLP), F32)],
        compiler_params=pltpu.CompilerParams(dimension_semantics=("arbitrary",), vmem_limit_bytes=VMEM_LIMIT_BYTES),
        name="mix_sample",
    )(x, cache_kt, cache_vt, cos, sin, gmix, w_in, sinks, gvg, coef, bias_rows, ga, gg, w_out)


def _gmlp_tables_kernel(ws_ref, bs_ref, bias_ref, coef_ref, biasr_ref, *, t_new):
    sub = 8
    expand = lambda col: jnp.broadcast_to(col, (col.shape[0], D_GMLP // G_HEADS))
    row = lax.broadcasted_iota(jnp.int32, (CHUNK, CHUNK), 0)
    lane = lax.broadcasted_iota(jnp.int32, (CHUNK, CHUNK), 1)
    bias = jnp.concatenate(
        [expand(jnp.sum(jnp.where(lane == row, bs_ref[hd:hd + 1, :], 0.0), axis=1, keepdims=True))
         for hd in range(G_HEADS)], axis=1)
    bias_ref[...] = bias
    biasr_ref[...] = jnp.concatenate([bias[0:t_new]] * (sub // t_new), axis=0)
    t = lax.broadcasted_iota(jnp.int32, (sub, CHUNK), 0) & (t_new - 1)
    s = lax.broadcasted_iota(jnp.int32, (sub, CHUNK), 1)
    for d in range(t_new):
        cols = []
        for hd in range(G_HEADS):
            w_rows = jnp.concatenate([ws_ref[hd, 0:t_new, :]] * (sub // t_new), axis=0)
            cols.append(expand(jnp.sum(jnp.where(s == t - d, w_rows, 0.0), axis=1, keepdims=True)))
        coef_ref[d] = jnp.concatenate(cols, axis=1)


def _gmlp_tables(ws, bs, t_new):
    assert 8 % t_new == 0
    return pl.pallas_call(
        functools.partial(_gmlp_tables_kernel, t_new=t_new),
        out_shape=[jax.ShapeDtypeStruct((CHUNK, D_GMLP), F32), jax.ShapeDtypeStruct((t_new, 8, D_GMLP), F32),
                   jax.ShapeDtypeStruct((8, D_GMLP), F32)],
        name="gmlp_tables",
    )(ws, bs)


def _rope_tables(pos):
    inv_freq = ROPE_THETA ** (-jnp.arange(0, HEAD_DIM, 2, dtype=F32) / HEAD_DIM)
    ang = pos.astype(F32)[:, None] * jnp.tile(inv_freq, 2 * LANES // HEAD_DIM)[None, :]
    return jnp.cos(ang), jnp.sin(ang)


def kernel(x_prompt, x_sample, cache_k_win, cache_v_win, norm_ffn1, ffn1_gate, ffn1_up, ffn1_down, norm_mix, w_in,
           attn_sinks, gmlp_v_norm, gmlp_w_s, gmlp_b_s, norm_attn_out, norm_gmlp_out, w_out, norm_ffn2, ffn2_gate,
           ffn2_up, ffn2_down, norm_final):
    depth = norm_ffn1.shape[0]
    b, s, _ = x_prompt.shape
    bd, t_new, _ = x_sample.shape
    w_buf = cache_k_win.shape[2]

    cos_s, sin_s = _rope_tables(PAST_LEN + jnp.arange(t_new, dtype=jnp.int32))
    cos_s, sin_s = jnp.tile(cos_s, (8 // t_new, 1)), jnp.tile(sin_s, (8 // t_new, 1))

    hp = x_prompt.reshape(b * s, D_MODEL)
    hs = x_sample
    outs = [[] for _ in range(6)]
    ffn1_w = [ffn1_gate[0], ffn1_up[0], ffn1_down[0]]
    for l in range(depth):
        last = l == depth - 1
        row = lambda a: a[l].reshape(1, -1)

        wcat = gmlp_w_s[l]
        bias_full, coef, bias_rows = _gmlp_tables(gmlp_w_s[l], gmlp_b_s[l], t_new)

        hp, hs, (w_in_b, w_out_b, wg2, wu2, wd2) = (_ffn_first if l == 0 else _ffn_half)(
            hp, hs, norm_ffn1[l], *ffn1_w, cast=(w_in[l], w_out[l], ffn2_gate[l], ffn2_up[l], ffn2_down[l]))

        hp, kpt, vpt, gvpt = _mix_prompt(hp.reshape(b, s, D_MODEL), row(norm_mix), w_in_b, attn_sinks[l],
                                         row(gmlp_v_norm), wcat, bias_full, row(norm_attn_out), row(norm_gmlp_out),
                                         w_out_b)
        to_t = lambda c: c.transpose(0, 2, 3, 1).reshape(bd, KV_W, w_buf)
        hs, kst, vst, gvs = _mix_sample(hs, to_t(cache_k_win[l]), to_t(cache_v_win[l]), cos_s, sin_s, row(norm_mix),
                                        w_in_b, attn_sinks[l], row(gmlp_v_norm), coef, bias_rows, row(norm_attn_out),
                                        row(norm_gmlp_out), w_out_b, t_new)
        from_t = lambda c: c.reshape(bd, N_KV, HEAD_DIM, w_buf).transpose(0, 3, 1, 2)

        next_ffn1 = () if last else (ffn1_gate[l + 1], ffn1_up[l + 1], ffn1_down[l + 1])
        hp, hs, ffn1_w = _ffn_half(hp.reshape(b * s, D_MODEL), hs, norm_ffn2[l], wg2, wu2, wd2,
                                   gf=norm_final if last else None, cast=next_ffn1,
                                   sample_out_shape=(bd, t_new, D_MODEL) if last else None)

        outs[0].append(kpt.reshape(b, N_KV, HEAD_DIM, WINDOW).transpose(0, 3, 1, 2))
        outs[1].append(vpt.reshape(b, N_KV, HEAD_DIM, WINDOW).transpose(0, 3, 1, 2))
        outs[2].append(from_t(kst))
        outs[3].append(from_t(vst))
        outs[4].append(gvpt.reshape(b, G_HEADS, D_GMLP // G_HEADS, CHUNK).transpose(0, 3, 1, 2))
        outs[5].append(gvs.reshape(t_new, G_HEADS, D_GMLP // G_HEADS, bd).transpose(3, 0, 1, 2))

    return (hp.reshape(b, s, D_MODEL), hs) + tuple(jnp.stack(o) for o in outs)
```

```python
import functools

import jax
import jax.numpy as jnp
from jax import lax
from jax.experimental import pallas as pl
from jax.experimental.pallas import tpu as pltpu

F32 = jnp.float32
BF16 = jnp.bfloat16

D_MODEL = 1024
D_FF = 2816
HEAD_DIM = 64
N_HEADS = 8
N_KV = 2
GQA = N_HEADS // N_KV
WINDOW = 128
CHUNK = 128
G_HEADS = 8
Q_W = N_HEADS * HEAD_DIM
KV_W = N_KV * HEAD_DIM
D_GMLP = 512
D_IN = Q_W + 2 * KV_W + 2 * D_GMLP
K_OFF = Q_W
V_OFF = K_OFF + KV_W
U_OFF = V_OFF + KV_W
GV_OFF = U_OFF + D_GMLP
ROPE_THETA = 10000.0
PAST_LEN = 16384
EPS = 1e-6
LOG2E = 1.4426950408889634
Q_SCALE = HEAD_DIM ** -0.5 * LOG2E
LANES = 128
BF16_SUBLANES = 16

TOKEN_TILE = 512
FF_CHUNK = 256
STAGE_SLOTS = 3
SEQ_GROUP = 8
GROUPS_PER_STEP = 4
VMEM_LIMIT_BYTES = 56 * 1024 * 1024

_NT = (((1,), (1,)), ((), ()))


def _rms(x, g):
    ms = jnp.mean(x * x, axis=-1, keepdims=True)
    return (x * lax.rsqrt(ms + EPS)) * g


def _resident(shape):
    zeros = (0,) * len(shape)
    return pl.BlockSpec(shape, lambda *_: zeros, pipeline_mode=pl.Buffered(1))


def _silu_gate(h, wg, wu):
    gate = jnp.dot(h, wg, preferred_element_type=F32)
    up = jnp.dot(h, wu, preferred_element_type=F32)
    return (gate * jax.nn.sigmoid(gate) * up).astype(BF16)


def _ffn_kernel(*refs, final_norm, n_cast, staged):
    refs = iter(refs)
    xp_ref, xn_ref, xs_ref, g_ref, wg_hbm, wu_hbm, wd_hbm = (next(refs) for _ in range(7))
    gf_ref = next(refs) if final_norm else None
    cast_in = [next(refs) for _ in range(n_cast)]
    yp_ref, ys_ref = next(refs), next(refs)
    cast_out = [next(refs) for _ in range(n_cast)]
    h_ref, act0_ref, act_ref, wg_ref, wu_ref, wd_ref = (next(refs) for _ in range(6))
    col_stage, row_stage = (next(refs), next(refs)) if staged else (None, None)
    sem = next(refs)
    n_chunks = D_FF // FF_CHUNK
    n_slots = sem.shape[0]
    gate_up = lambda h, c: _silu_gate(h, wg_ref[c], wu_ref[c])
    finish = lambda y: _rms(y, gf_ref[...]) if final_norm else y

    @pl.when(pl.program_id(0) == 0)
    def _():
        def fetch(c, slot):
            span = pl.ds(pl.multiple_of(c * FF_CHUNK, FF_CHUNK), FF_CHUNK)
            dst = ((col_stage.at[slot, 0], col_stage.at[slot, 1], row_stage.at[slot]) if staged else
                   (wg_ref.at[c], wu_ref.at[c], wd_ref.at[span, :]))
            return (pltpu.make_async_copy(wg_hbm.at[:, span], dst[0], sem.at[slot, 0]),
                    pltpu.make_async_copy(wu_hbm.at[:, span], dst[1], sem.at[slot, 1]),
                    pltpu.make_async_copy(wd_hbm.at[span, :], dst[2], sem.at[slot, 2]))

        for c in range(n_slots - 1):
            for copy in fetch(c, c):
                copy.start()
        h_ref[...] = _rms(xs_ref[...].reshape(yp_ref.shape), g_ref[...]).astype(BF16)
        yp_ref[...] = jnp.zeros(yp_ref.shape, F32)

        def body(c, carry):
            slot = c % n_slots
            ahead = c + n_slots - 1

            @pl.when(ahead < n_chunks)
            def _():
                for copy in fetch(ahead, ahead % n_slots):
                    copy.start()

            for copy in fetch(c, slot):
                copy.wait()
            rows = pl.ds(pl.multiple_of(c * FF_CHUNK, FF_CHUNK), FF_CHUNK)
            if staged:
                wg_ref[c] = col_stage[slot, 0].astype(BF16)
                wu_ref[c] = col_stage[slot, 1].astype(BF16)
                wd_ref[rows, :] = row_stage[slot].astype(BF16)
            yp_ref[...] += jnp.dot(gate_up(h_ref[...], c), wd_ref[rows, :], preferred_element_type=F32)
            return carry

        lax.fori_loop(0, n_chunks, body, 0)
        ys_ref[...] = finish(xs_ref[...].reshape(yp_ref.shape) + 0.5 * yp_ref[...]).reshape(ys_ref.shape)
        h0 = _rms(xp_ref[...], g_ref[...]).astype(BF16)
        h_ref[...] = h0
        act0_ref[...] = gate_up(h0, 0)

    for c in range(1, n_chunks):
        act_ref[:, (c - 1) * FF_CHUNK:c * FF_CHUNK] = gate_up(h_ref[...], c)
    down0 = jnp.dot(act0_ref[...], wd_ref[0:FF_CHUNK, :], preferred_element_type=F32)
    hn = _rms(xn_ref[...], g_ref[...]).astype(BF16)
    h_ref[...] = hn
    act0_ref[...] = gate_up(hn, 0)
    yp_ref[...] = finish(
        xp_ref[...] + 0.5 * (down0 + jnp.dot(act_ref[...], wd_ref[FF_CHUNK:, :], preferred_element_type=F32)))

    for src, dst in zip(cast_in, cast_out):
        dst[...] = src[...].astype(BF16)


def _cast_row_blocks(rows, n_steps):
    return max(d for d in range(1, n_steps + 1) if rows % d == 0 and (rows // d) % BF16_SUBLANES == 0)


def _cast_specs(mats, n_steps):
    specs = []
    for w in mats:
        rows, cols = w.shape
        nb = _cast_row_blocks(rows, n_steps)
        specs.append(pl.BlockSpec((rows // nb, cols), lambda i, nb=nb: (jnp.minimum(i, nb - 1), 0)))
    return specs


def _ffn_half(xp, xs, g, wg, wu, wd, gf=None, cast=(), sample_out_shape=None):
    tm = TOKEN_TILE
    n = xp.shape[0]
    n_chunks = D_FF // FF_CHUNK
    staged = wg.dtype == F32
    assert all(w.dtype == wg.dtype for w in (wu, wd))
    assert n % tm == 0 and xs.size == tm * D_MODEL and xs.shape[-1] == D_MODEL
    assert D_FF % FF_CHUNK == 0 and n_chunks >= STAGE_SLOTS
    n_tiles = n // tm
    prompt_spec = pl.BlockSpec((tm, D_MODEL), lambda i: (i, 0))
    next_spec = pl.BlockSpec((tm, D_MODEL), lambda i: (jnp.minimum(i + 1, n_tiles - 1), 0))
    hbm = pl.BlockSpec(memory_space=pl.ANY)
    sample_shape = sample_out_shape or (tm, D_MODEL)
    in_specs = [prompt_spec, next_spec, _resident(xs.shape), _resident((1, D_MODEL)), hbm, hbm, hbm]
    args = [xp, xp, xs, g.reshape(1, D_MODEL), wg, wu, wd]
    if gf is not None:
        in_specs.append(_resident((1, D_MODEL)))
        args.append(gf.reshape(1, D_MODEL))
    cast_specs = _cast_specs(cast, n_tiles)
    n_slots = STAGE_SLOTS if staged else n_chunks
    stage = [pltpu.VMEM((n_slots, 2, D_MODEL, FF_CHUNK), F32), pltpu.VMEM((n_slots, FF_CHUNK, D_MODEL), F32)]
    outs = pl.pallas_call(
        functools.partial(_ffn_kernel, final_norm=gf is not None, n_cast=len(cast), staged=staged),
        grid=(n_tiles,),
        in_specs=in_specs + cast_specs,
        out_specs=[prompt_spec, pl.BlockSpec(sample_shape, lambda i: (0,) * len(sample_shape))] + cast_specs,
        out_shape=[jax.ShapeDtypeStruct((n, D_MODEL), F32), jax.ShapeDtypeStruct(sample_shape, F32)]
        + [jax.ShapeDtypeStruct(w.shape, BF16) for w in cast],
        scratch_shapes=[pltpu.VMEM((tm, D_MODEL), BF16), pltpu.VMEM((tm, FF_CHUNK), BF16),
                        pltpu.VMEM((tm, D_FF - FF_CHUNK), BF16),
                        pltpu.VMEM((n_chunks, D_MODEL, FF_CHUNK), BF16), pltpu.VMEM((n_chunks, D_MODEL, FF_CHUNK), BF16),
                        pltpu.VMEM((D_FF, D_MODEL), BF16)] + (stage if staged else [])
        + [pltpu.SemaphoreType.DMA((n_slots, 3))],
        compiler_params=pltpu.CompilerParams(dimension_semantics=("arbitrary",), vmem_limit_bytes=VMEM_LIMIT_BYTES),
        name="ffn_final" if gf is not None else "ffn_half",
    )(*args, *cast)
    return outs[0], outs[1], list(outs[2:])


def _first_half(shape):
    return (lax.broadcasted_iota(jnp.int32, shape, 1) & (HEAD_DIM - 1)) < HEAD_DIM // 2


def _sign_sin(sin):
    return jnp.where(_first_half(sin.shape), -sin, sin)


def _rope(xg, cos, sin_signed):
    swapped = jnp.where(_first_half(xg.shape), pltpu.roll(xg, LANES - HEAD_DIM // 2, 1),
                        pltpu.roll(xg, HEAD_DIM // 2, 1))
    return xg * cos + swapped * sin_signed


def _in_proj(x, gmix, w_in, cos, sin_signed, gvn_gain):
    h = _rms(x, gmix).astype(BF16)
    z = jnp.dot(h, w_in, preferred_element_type=F32)
    q = jnp.concatenate(
        [(_rope(z[:, LANES * i:LANES * (i + 1)], cos, sin_signed) * Q_SCALE).astype(BF16) for i in range(Q_W // LANES)],
        axis=1)
    k = _rope(z[:, K_OFF:V_OFF], cos, sin_signed)
    v = z[:, V_OFF:U_OFF]
    u = _gelu(z[:, U_OFF:GV_OFF])
    gvn = _rms(_gelu(z[:, GV_OFF:]), gvn_gain)
    return q, k, v, u, gvn


def _out_proj(x, ya, yg, ga, gg, w_out):
    cat = jnp.concatenate([_rms(ya, ga).astype(BF16), _rms(yg, gg).astype(BF16)], axis=1)
    return x + jnp.dot(cat, w_out, preferred_element_type=F32)


def _softmax_sink(s, sink):
    m = jnp.maximum(jnp.max(s, axis=1, keepdims=True), sink)
    p = jnp.exp2(s - m)
    return p, jnp.sum(p, axis=1, keepdims=True) + jnp.exp2(sink - m)


def _gelu(x):
    k = -2.0 * (2.0 / jnp.pi) ** 0.5 * LOG2E
    return x / (1.0 + jnp.exp2(x * (k + (0.044715 * k) * (x * x))))


def _mix_prompt_kernel(xa_ref, xc_ref, coff_ref, soff_ref, cbase_ref, sbase_ref, gmix_ref, win_ref, sinks_ref, gvg_ref,
                       wcat_ref, bias_ref, ga_ref, gg_ref, wout_ref, o_ref, ko_ref, vo_ref, gvo_ref, q_s, k_s, v_s, u_s,
                       gv_s, cat_s, *, tm, tiles_per_seq, n_tiles):
    s = pl.program_id(0)

    @pl.when(s == 0)
    def _():
        k_s[1, tm:tm + WINDOW, :] = jnp.zeros((WINDOW, KV_W), BF16)
        v_s[1, tm:tm + WINDOW, :] = jnp.zeros((WINDOW, KV_W), BF16)
        cat_s[1] = jnp.zeros((tm, D_MODEL), BF16)

    step = functools.partial(
        _mix_prompt_step, s, xa_ref, xc_ref, coff_ref, soff_ref, cbase_ref, sbase_ref, gmix_ref, win_ref, sinks_ref,
        gvg_ref, wcat_ref, bias_ref, ga_ref, gg_ref, wout_ref, o_ref, ko_ref, vo_ref, gvo_ref, q_s, k_s, v_s, u_s, gv_s,
        cat_s, tm=tm, tiles_per_seq=tiles_per_seq, n_tiles=n_tiles)
    last = n_tiles + 1
    pl.when(s == 0)(functools.partial(step, cur=0, stages="proj"))
    pl.when(s == last)(functools.partial(step, cur=last % 2, stages="out"))
    for parity in range(2):
        pl.when((s > 0) & (s < last) & (s % 2 == parity))(functools.partial(step, cur=parity, stages="all"))


def _mix_prompt_step(s, xa_ref, xc_ref, coff_ref, soff_ref, cbase_ref, sbase_ref, gmix_ref, win_ref, sinks_ref, gvg_ref,
                     wcat_ref, bias_ref, ga_ref, gg_ref, wout_ref, o_ref, ko_ref, vo_ref, gvo_ref, q_s, k_s, v_s, u_s, gv_s,
                     cat_s, *, tm, tiles_per_seq, n_tiles, cur, stages):
    oth = 1 - cur

    h = _rms(xa_ref[0], gmix_ref[...]).astype(BF16)
    tile_in_seq = jnp.minimum(s, n_tiles - 1) % tiles_per_seq
    cb = cbase_ref[pl.ds(tile_in_seq, 1), :]
    sb = sbase_ref[pl.ds(tile_in_seq, 1), :]
    cos = cb * coff_ref[...] - sb * soff_ref[...]
    sin = _sign_sin(sb * coff_ref[...] + cb * soff_ref[...])

    def proj_q():
        z = jnp.dot(h, win_ref[:, 0:Q_W], preferred_element_type=F32)
        for i in range(Q_W // LANES):
            q_s[cur, :, LANES * i:LANES * (i + 1)] = (
                _rope(z[:, LANES * i:LANES * (i + 1)], cos, sin) * Q_SCALE).astype(BF16)

    def proj_kv():
        z = jnp.dot(h, win_ref[:, K_OFF:U_OFF], preferred_element_type=F32)
        k = _rope(z[:, 0:KV_W], cos, sin)
        v = z[:, KV_W:]
        k_s[cur, 0:WINDOW, :] = k_s[oth, tm:tm + WINDOW, :]
        v_s[cur, 0:WINDOW, :] = v_s[oth, tm:tm + WINDOW, :]
        k_s[cur, WINDOW:, :] = k.astype(BF16)
        v_s[cur, WINDOW:, :] = v.astype(BF16)
        ko_ref[0] = k[tm - WINDOW:].T
        vo_ref[0] = v[tm - WINDOW:].T

    def proj_u():
        u_s[cur] = _gelu(jnp.dot(h, win_ref[:, U_OFF:GV_OFF], preferred_element_type=F32))

    def proj_gv():
        gvn = _rms(_gelu(jnp.dot(h, win_ref[:, GV_OFF:], preferred_element_type=F32)), gvg_ref[...])
        gv_s[cur] = gvn.astype(BF16)
        gvo_ref[0] = gvn[tm - CHUNK:].T

    def out_half(c):
        cols = slice(c * (D_MODEL // 2), (c + 1) * (D_MODEL // 2))

        def run():
            o_ref[0, :, cols] = xc_ref[0, :, cols] + jnp.dot(cat_s[cur], wout_ref[:, cols], preferred_element_type=F32)
        return run

    mid_fill = [[out_half(0)], [proj_q], [proj_kv, proj_u], [proj_gv]]
    end_fill = [[], [], [], [out_half(1)]]
    assert len(mid_fill) == len(end_fill) == tm // WINDOW
    if stages == "proj":
        for run in (proj_q, proj_kv, proj_u, proj_gv):
            run()
        return
    if stages == "out":
        out_half(0)()
        out_half(1)()
        return

    wrow = lax.broadcasted_iota(jnp.int32, (CHUNK, CHUNK), 0)
    wcol = lax.broadcasted_iota(jnp.int32, (CHUNK, CHUNK), 1)
    wtril = [jnp.where(wcol <= wrow, wcat_ref[hd], 0.0).astype(BF16) for hd in range(G_HEADS)]
    wmix = [jnp.concatenate(wtril[2 * p:2 * p + 2], axis=1) for p in range(G_HEADS // 2)]
    lane = lax.broadcasted_iota(jnp.int32, (CHUNK, LANES), 1)
    low_head = lane < HEAD_DIM

    qi = lax.broadcasted_iota(jnp.int32, (WINDOW, 2 * WINDOW), 0)
    sj = lax.broadcasted_iota(jnp.int32, (WINDOW, 2 * WINDOW), 1)
    dist = WINDOW + qi - sj
    band = (dist >= 0) & (dist < WINDOW)
    first_lo = jnp.where((s + tiles_per_seq - 1) % tiles_per_seq == 0, WINDOW, 0)

    for j in range(tm // WINDOW):
        rows = slice(j * WINDOW, (j + 1) * WINDOW)
        mask = band & (sj >= first_lo) if j == 0 else band
        qb = q_s[oth, rows, :]
        scores = []
        for kv in range(N_KV):
            qs = jnp.concatenate(
                [qb[:, HEAD_DIM * hd:HEAD_DIM * (hd + 1)] for hd in range(kv * GQA, (kv + 1) * GQA)], axis=0)
            kb = k_s[oth, j * WINDOW:(j + 2) * WINDOW, HEAD_DIM * kv:HEAD_DIM * (kv + 1)]
            scores.append(lax.dot_general(qs, kb, _NT, preferred_element_type=F32))
        for run in mid_fill[j]:
            run()
        outs, dens = [], []
        for kv in range(N_KV):
            vb = v_s[oth, j * WINDOW:(j + 2) * WINDOW, HEAD_DIM * kv:HEAD_DIM * (kv + 1)]
            ps = []
            for g in range(GQA):
                sg = jnp.where(mask, scores[kv][g * WINDOW:(g + 1) * WINDOW], -jnp.inf)
                p, den = _softmax_sink(sg, sinks_ref[kv * GQA + g] * LOG2E)
                ps.append(p.astype(BF16))
                dens.append(den)
            outs.append(jnp.dot(jnp.concatenate(ps, axis=0), vb, preferred_element_type=F32))
        mixed = []
        for p in range(G_HEADS // 2):
            r = gv_s[oth, rows, LANES * p:LANES * (p + 1)]
            zero = jnp.zeros_like(r)
            rhs = jnp.concatenate([jnp.where(low_head, r, zero), jnp.where(low_head, zero, r)], axis=0)
            mixed.append(jnp.dot(wmix[p], rhs, preferred_element_type=F32))
        for run in end_fill[j]:
            run()
        ya = [outs[hd // GQA][(hd % GQA) * WINDOW:(hd % GQA + 1) * WINDOW] / dens[hd] for hd in range(N_HEADS)]
        cat_s[oth, rows, 0:Q_W] = _rms(jnp.concatenate(ya, axis=1), ga_ref[...]).astype(BF16)
        yg = u_s[oth, rows, :] * (jnp.concatenate(mixed, axis=1) + bias_ref[...])
        cat_s[oth, rows, Q_W:] = _rms(yg, gg_ref[...]).astype(BF16)


def _mix_prompt(x, gmix, w_in, sinks, gvg, wcat, bias_full, ga, gg, w_out):
    b, s, _ = x.shape
    tm = TOKEN_TILE
    assert s % tm == 0 and tm % WINDOW == 0
    tiles_per_seq = s // tm
    n_tiles = b * tiles_per_seq
    cos_off, sin_off = _rope_tables(jnp.arange(tm, dtype=jnp.int32))
    cos_base, sin_base = _rope_tables(tm * jnp.arange(tiles_per_seq, dtype=jnp.int32))
    proj_tile = lambda i: jnp.minimum(i, n_tiles - 1)
    out_tile = lambda i: jnp.maximum(i - 2, 0)
    x_tiles = x.reshape(n_tiles, tm, D_MODEL)
    last = lambda width: pl.BlockSpec((1, width, WINDOW), lambda i: (proj_tile(i) // tiles_per_seq, 0, 0))
    out, ko, vo, gvo = pl.pallas_call(
        functools.partial(_mix_prompt_kernel, tm=tm, tiles_per_seq=tiles_per_seq, n_tiles=n_tiles),
        grid=(n_tiles + 2,),
        in_specs=[pl.BlockSpec((1, tm, D_MODEL), lambda i: (proj_tile(i), 0, 0)),
                  pl.BlockSpec((1, tm, D_MODEL), lambda i: (out_tile(i), 0, 0)),
                  _resident((tm, LANES)), _resident((tm, LANES)), _resident((tiles_per_seq, LANES)),
                  _resident((tiles_per_seq, LANES)), _resident((1, D_MODEL)), _resident((D_MODEL, D_IN)),
                  pl.BlockSpec(memory_space=pltpu.SMEM), _resident((1, D_GMLP)),
                  _resident((G_HEADS, CHUNK, CHUNK)), _resident((CHUNK, D_GMLP)), _resident((1, Q_W)),
                  _resident((1, D_GMLP)), _resident((D_MODEL, D_MODEL))],
        out_specs=[pl.BlockSpec((1, tm, D_MODEL), lambda i: (out_tile(i), 0, 0)), last(KV_W), last(KV_W),
                   last(D_GMLP)],
        out_shape=[jax.ShapeDtypeStruct((n_tiles, tm, D_MODEL), F32), jax.ShapeDtypeStruct((b, KV_W, WINDOW), F32),
                   jax.ShapeDtypeStruct((b, KV_W, WINDOW), F32), jax.ShapeDtypeStruct((b, D_GMLP, CHUNK), F32)],
        scratch_shapes=[pltpu.VMEM((2, tm, Q_W), BF16), pltpu.VMEM((2, tm + WINDOW, KV_W), BF16),
                        pltpu.VMEM((2, tm + WINDOW, KV_W), BF16), pltpu.VMEM((2, tm, D_GMLP), F32),
                        pltpu.VMEM((2, tm, D_GMLP), BF16), pltpu.VMEM((2, tm, D_MODEL), BF16)],
        compiler_params=pltpu.CompilerParams(dimension_semantics=("arbitrary",), vmem_limit_bytes=VMEM_LIMIT_BYTES),
        name="mix_prompt",
    )(x_tiles, x_tiles, cos_off, sin_off, cos_base, sin_base, gmix, w_in, sinks, gvg, wcat, bias_full, ga, gg, w_out)
    return out.reshape(b, s, D_MODEL), ko, vo, gvo


def _mix_sample_kernel(x_ref, ckt_ref, cvt_ref, cos_ref, sin_ref, gmix_ref, win_ref, sinks_ref, gvg_ref, coef_ref,
                       biasr_ref, ga_ref, gg_ref, wout_ref, o_ref, kot_ref, vot_ref, gvo_ref, q_s, k_s, v_s, ya_s,
                       yg_s, *, t_new, w_buf):
    step = pl.program_id(0)
    n_tok = x_ref.shape[0]
    step_seqs = SEQ_GROUP * GROUPS_PER_STEP
    step_rows = step_seqs * t_new
    grp_rows = SEQ_GROUP * t_new
    grp_keys = SEQ_GROUP * w_buf
    sub = 8

    @pl.when(step == 0)
    def _():
        tile = lambda tab: jnp.broadcast_to(tab[None], (n_tok // sub, sub, LANES)).reshape(n_tok, LANES)
        q, k, v, u, gvn = _in_proj(x_ref[...], gmix_ref[...], win_ref[...], tile(cos_ref[...]),
                                   tile(_sign_sin(sin_ref[...])), gvg_ref[...])
        q_s[...] = q
        k_s[...] = k
        v_s[...] = v
        gv_seq = gvn.reshape(n_tok // t_new, t_new, D_GMLP)
        for tok in range(t_new):
            gvo_ref[tok] = gv_seq[:, tok, :].T
        g3 = gvn.reshape(n_tok // sub, sub, D_GMLP)
        trow = lax.broadcasted_iota(jnp.int32, (1, sub, D_GMLP), 1) & (t_new - 1)
        mixed = biasr_ref[...][None] + coef_ref[0][None] * g3
        for d in range(1, t_new):
            shifted = jnp.where(trow >= d, pltpu.roll(g3, d, 1), 0.0)
            mixed = mixed + coef_ref[d][None] * shifted
        yg_s[...] = (u.reshape(n_tok // sub, sub, D_GMLP) * mixed).reshape(n_tok, D_GMLP)

    row0 = pl.multiple_of(step * step_rows, step_rows)
    q_step = q_s[pl.ds(row0, step_rows), :]
    kn = k_s[pl.ds(row0, step_rows), :]
    vn = v_s[pl.ds(row0, step_rows), :]

    knt = kn.T
    vnt = vn.T
    tail = lax.broadcasted_iota(jnp.int32, (KV_W, w_buf), 1) >= w_buf - t_new
    for b in range(step_seqs):
        shift = (w_buf - t_new - t_new * b) % w_buf
        kot_ref[b] = jnp.where(tail, pltpu.roll(knt, shift, 1) if shift else knt,
                               pltpu.roll(ckt_ref[b], w_buf - t_new, 1))
        vot_ref[b] = jnp.where(tail, pltpu.roll(vnt, shift, 1) if shift else vnt,
                               pltpu.roll(cvt_ref[b], w_buf - t_new, 1))

    knb = kn.astype(BF16)
    vnb = vn.astype(BF16)

    n_rows = GQA * grp_rows
    shift_t = t_new.bit_length() - 1
    shift_w = w_buf.bit_length() - 1
    r = lax.broadcasted_iota(jnp.int32, (n_rows, grp_keys), 0)
    c = lax.broadcasted_iota(jnp.int32, (n_rows, grp_keys), 1)
    mask_c = ((c >> shift_w) == ((r & (grp_rows - 1)) >> shift_t)) & ((c & (w_buf - 1)) > (r & (t_new - 1)))
    r2 = lax.broadcasted_iota(jnp.int32, (n_rows, grp_rows), 0)
    c2 = lax.broadcasted_iota(jnp.int32, (n_rows, grp_rows), 1)
    mask_n = ((c2 >> shift_t) == ((r2 & (grp_rows - 1)) >> shift_t)) & ((c2 & (t_new - 1)) <= (r2 & (t_new - 1)))
    row_head = lax.broadcasted_iota(jnp.int32, (n_rows, 1), 0) >> (grp_rows.bit_length() - 1)

    for grp in range(GROUPS_PER_STEP):
        rows = slice(grp * grp_rows, (grp + 1) * grp_rows)
        seqs = range(grp * SEQ_GROUP, (grp + 1) * SEQ_GROUP)
        for kv in range(N_KV):
            heads = [kv * GQA + i for i in range(GQA)]
            lanes = slice(HEAD_DIM * kv, HEAD_DIM * (kv + 1))
            kt = jnp.concatenate([ckt_ref[b, lanes, :] for b in seqs], axis=1).astype(BF16)
            vt = jnp.concatenate([cvt_ref[b, lanes, :] for b in seqs], axis=1).astype(BF16)
            qs = jnp.concatenate([q_step[rows, HEAD_DIM * hd:HEAD_DIM * (hd + 1)] for hd in heads], axis=0)
            s_c = jnp.where(mask_c, jnp.dot(qs, kt, preferred_element_type=F32), -jnp.inf)
            s_n = jnp.where(mask_n, lax.dot_general(qs, knb[rows, lanes], _NT, preferred_element_type=F32), -jnp.inf)
            sink = jnp.full((n_rows, 1), sinks_ref[heads[0]] * LOG2E, F32)
            for i in range(1, GQA):
                sink = jnp.where(row_head == i, sinks_ref[heads[i]] * LOG2E, sink)
            m = jnp.maximum(jnp.maximum(jnp.max(s_c, axis=1, keepdims=True), jnp.max(s_n, axis=1, keepdims=True)),
                            sink)
            p_c = jnp.exp2(s_c - m)
            p_n = jnp.exp2(s_n - m)
            den = jnp.sum(p_c, axis=1, keepdims=True) + jnp.sum(p_n, axis=1, keepdims=True) + jnp.exp2(sink - m)
            o = (lax.dot_general(p_c.astype(BF16), vt, _NT, preferred_element_type=F32)
                 + jnp.dot(p_n.astype(BF16), vnb[rows, lanes], preferred_element_type=F32)) / den
            for i, hd in enumerate(heads):
                ya_s[pl.ds(row0 + grp * grp_rows, grp_rows), HEAD_DIM * hd:HEAD_DIM * (hd + 1)] = (
                    o[i * grp_rows:(i + 1) * grp_rows])

    @pl.when(step == pl.num_programs(0) - 1)
    def _():
        o_ref[...] = _out_proj(x_ref[...], ya_s[...], yg_s[...], ga_ref[...], gg_ref[...], wout_ref[...])


def _mix_sample(x, cache_kt, cache_vt, cos, sin, gmix, w_in, sinks, gvg, coef, bias_rows, ga, gg, w_out, t_new):
    n_tok = x.shape[0]
    n_seq, _, w_buf = cache_kt.shape
    step_seqs = SEQ_GROUP * GROUPS_PER_STEP
    assert n_seq % step_seqs == 0 and n_tok == n_seq * t_new
    assert t_new & (t_new - 1) == 0 and w_buf & (w_buf - 1) == 0 and 8 % t_new == 0
    assert step_seqs * t_new == w_buf == LANES
    cache_spec = pl.BlockSpec((step_seqs, KV_W, w_buf), lambda i: (i, 0, 0))
    return pl.pallas_call(
        functools.partial(_mix_sample_kernel, t_new=t_new, w_buf=w_buf),
        grid=(n_seq // step_seqs,),
        in_specs=[_resident((n_tok, D_MODEL)), cache_spec, cache_spec, _resident((8, LANES)),
                  _resident((8, LANES)), _resident((1, D_MODEL)), _resident((D_MODEL, D_IN)),
                  pl.BlockSpec(memory_space=pltpu.SMEM), _resident((1, D_GMLP)), _resident((t_new, 8, D_GMLP)),
                  _resident((8, D_GMLP)), _resident((1, Q_W)), _resident((1, D_GMLP)),
                  _resident((D_MODEL, D_MODEL))],
        out_specs=[pl.BlockSpec((n_tok, D_MODEL), lambda i: (0, 0)), cache_spec, cache_spec,
                   pl.BlockSpec((t_new, D_GMLP, n_seq), lambda i: (0, 0, 0))],
        out_shape=[jax.ShapeDtypeStruct((n_tok, D_MODEL), F32), jax.ShapeDtypeStruct(cache_kt.shape, F32),
                   jax.ShapeDtypeStruct(cache_vt.shape, F32), jax.ShapeDtypeStruct((t_new, D_GMLP, n_seq), F32)],
        scratch_shapes=[pltpu.VMEM((n_tok, Q_W), BF16), pltpu.VMEM((n_tok, KV_W), F32), pltpu.VMEM((n_tok, KV_W), F32),
                        pltpu.VMEM((n_tok, Q_W), F32), pltpu.VMEM((n_tok, D_GMLP), F32)],
        compiler_params=pltpu.CompilerParams(dimension_semantics=("arbitrary",), vmem_limit_bytes=VMEM_LIMIT_BYTES),
        name="mix_sample",
    )(x, cache_kt, cache_vt, cos, sin, gmix, w_in, sinks, gvg, coef, bias_rows, ga, gg, w_out)


def _gmlp_tables_kernel(ws_ref, bs_ref, bias_ref, coef_ref, biasr_ref, *, t_new):
    sub = 8
    expand = lambda col: jnp.broadcast_to(col, (col.shape[0], D_GMLP // G_HEADS))
    row = lax.broadcasted_iota(jnp.int32, (CHUNK, CHUNK), 0)
    lane = lax.broadcasted_iota(jnp.int32, (CHUNK, CHUNK), 1)
    bias = jnp.concatenate(
        [expand(jnp.sum(jnp.where(lane == row, bs_ref[hd:hd + 1, :], 0.0), axis=1, keepdims=True))
         for hd in range(G_HEADS)], axis=1)
    bias_ref[...] = bias
    biasr_ref[...] = jnp.concatenate([bias[0:t_new]] * (sub // t_new), axis=0)
    t = lax.broadcasted_iota(jnp.int32, (sub, CHUNK), 0) & (t_new - 1)
    s = lax.broadcasted_iota(jnp.int32, (sub, CHUNK), 1)
    for d in range(t_new):
        cols = []
        for hd in range(G_HEADS):
            w_rows = jnp.concatenate([ws_ref[hd, 0:t_new, :]] * (sub // t_new), axis=0)
            cols.append(expand(jnp.sum(jnp.where(s == t - d, w_rows, 0.0), axis=1, keepdims=True)))
        coef_ref[d] = jnp.concatenate(cols, axis=1)


def _gmlp_tables(ws, bs, t_new):
    assert 8 % t_new == 0
    return pl.pallas_call(
        functools.partial(_gmlp_tables_kernel, t_new=t_new),
        out_shape=[jax.ShapeDtypeStruct((CHUNK, D_GMLP), F32), jax.ShapeDtypeStruct((t_new, 8, D_GMLP), F32),
                   jax.ShapeDtypeStruct((8, D_GMLP), F32)],
        name="gmlp_tables",
    )(ws, bs)


def _rope_tables(pos):
    inv_freq = ROPE_THETA ** (-jnp.arange(0, HEAD_DIM, 2, dtype=F32) / HEAD_DIM)
    ang = pos.astype(F32)[:, None] * jnp.tile(inv_freq, 2 * LANES // HEAD_DIM)[None, :]
    return jnp.cos(ang), jnp.sin(ang)


def kernel(x_prompt, x_sample, cache_k_win, cache_v_win, norm_ffn1, ffn1_gate, ffn1_up, ffn1_down, norm_mix, w_in,
           attn_sinks, gmlp_v_norm, gmlp_w_s, gmlp_b_s, norm_attn_out, norm_gmlp_out, w_out, norm_ffn2, ffn2_gate,
           ffn2_up, ffn2_down, norm_final):
    depth = norm_ffn1.shape[0]
    b, s, _ = x_prompt.shape
    bd, t_new, _ = x_sample.shape
    w_buf = cache_k_win.shape[2]

    cos_s, sin_s = _rope_tables(PAST_LEN + jnp.arange(t_new, dtype=jnp.int32))
    cos_s, sin_s = jnp.tile(cos_s, (8 // t_new, 1)), jnp.tile(sin_s, (8 // t_new, 1))

    hp = x_prompt.reshape(b * s, D_MODEL)
    hs = x_sample
    outs = [[] for _ in range(6)]
    ffn1_w = [ffn1_gate[0], ffn1_up[0], ffn1_down[0]]
    for l in range(depth):
        last = l == depth - 1
        row = lambda a: a[l].reshape(1, -1)

        wcat = gmlp_w_s[l]
        bias_full, coef, bias_rows = _gmlp_tables(gmlp_w_s[l], gmlp_b_s[l], t_new)

        hp, hs, (w_in_b, w_out_b, wg2, wu2, wd2) = _ffn_half(
            hp, hs, norm_ffn1[l], *ffn1_w, cast=(w_in[l], w_out[l], ffn2_gate[l], ffn2_up[l], ffn2_down[l]))

        hp, kpt, vpt, gvpt = _mix_prompt(hp.reshape(b, s, D_MODEL), row(norm_mix), w_in_b, attn_sinks[l],
                                         row(gmlp_v_norm), wcat, bias_full, row(norm_attn_out), row(norm_gmlp_out),
                                         w_out_b)
        to_t = lambda c: c.transpose(0, 2, 3, 1).reshape(bd, KV_W, w_buf)
        hs, kst, vst, gvs = _mix_sample(hs, to_t(cache_k_win[l]), to_t(cache_v_win[l]), cos_s, sin_s, row(norm_mix),
                                        w_in_b, attn_sinks[l], row(gmlp_v_norm), coef, bias_rows, row(norm_attn_out),
                                        row(norm_gmlp_out), w_out_b, t_new)
        from_t = lambda c: c.reshape(bd, N_KV, HEAD_DIM, w_buf).transpose(0, 3, 1, 2)

        next_ffn1 = () if last else (ffn1_gate[l + 1], ffn1_up[l + 1], ffn1_down[l + 1])
        hp, hs, ffn1_w = _ffn_half(hp.reshape(b * s, D_MODEL), hs, norm_ffn2[l], wg2, wu2, wd2,
                                   gf=norm_final if last else None, cast=next_ffn1,
                                   sample_out_shape=(bd, t_new, D_MODEL) if last else None)

        outs[0].append(kpt.reshape(b, N_KV, HEAD_DIM, WINDOW).transpose(0, 3, 1, 2))
        outs[1].append(vpt.reshape(b, N_KV, HEAD_DIM, WINDOW).transpose(0, 3, 1, 2))
        outs[2].append(from_t(kst))
        outs[3].append(from_t(vst))
        outs[4].append(gvpt.reshape(b, G_HEADS, D_GMLP // G_HEADS, CHUNK).transpose(0, 3, 1, 2))
        outs[5].append(gvs.reshape(t_new, G_HEADS, D_GMLP // G_HEADS, bd).transpose(3, 0, 1, 2))

    return (hp.reshape(b, s, D_MODEL), hs) + tuple(jnp.stack(o) for o in outs)
```

```python
import functools

import jax
import jax.numpy as jnp
from jax import lax
from jax.experimental import pallas as pl
from jax.experimental.pallas import tpu as pltpu

F32 = jnp.float32
BF16 = jnp.bfloat16

D_MODEL = 1024
D_FF = 2816
HEAD_DIM = 64
N_HEADS = 8
N_KV = 2
GQA = N_HEADS // N_KV
WINDOW = 128
CHUNK = 128
G_HEADS = 8
Q_W = N_HEADS * HEAD_DIM
KV_W = N_KV * HEAD_DIM
D_GMLP = 512
D_IN = Q_W + 2 * KV_W + 2 * D_GMLP
K_OFF = Q_W
V_OFF = K_OFF + KV_W
U_OFF = V_OFF + KV_W
GV_OFF = U_OFF + D_GMLP
ROPE_THETA = 10000.0
PAST_LEN = 16384
EPS = 1e-6
LOG2E = 1.4426950408889634
Q_SCALE = HEAD_DIM ** -0.5 * LOG2E
LANES = 128
BF16_SUBLANES = 16

TOKEN_TILE = 512
FF_CHUNK = 256
STAGE_SLOTS = 3
SEQ_GROUP = 8
GROUPS_PER_STEP = 4
VMEM_LIMIT_BYTES = 56 * 1024 * 1024

_NT = (((1,), (1,)), ((), ()))


def _rms(x, g):
    ms = jnp.mean(x * x, axis=-1, keepdims=True)
    return (x * lax.rsqrt(ms + EPS)) * g


def _resident(shape):
    zeros = (0,) * len(shape)
    return pl.BlockSpec(shape, lambda *_: zeros, pipeline_mode=pl.Buffered(1))


def _silu_gate(h, wg, wu):
    gate = jnp.dot(h, wg, preferred_element_type=F32)
    up = jnp.dot(h, wu, preferred_element_type=F32)
    return (gate * jax.nn.sigmoid(gate) * up).astype(BF16)


def _ffn_kernel(*refs, final_norm, n_cast, n_cast_chunked, staged):
    refs = iter(refs)
    xp_ref, xn_ref, xs_ref, g_ref, wg_hbm, wu_hbm, wd_hbm = (next(refs) for _ in range(7))
    gf_ref = next(refs) if final_norm else None
    cast_in = [next(refs) for _ in range(n_cast)]
    chunked_in = [next(refs) for _ in range(n_cast_chunked)]
    yp_ref, ys_ref = next(refs), next(refs)
    cast_out = [next(refs) for _ in range(n_cast)]
    chunked_out = [next(refs) for _ in range(n_cast_chunked)]
    h_ref, act0_ref, act_ref, wg_ref, wu_ref, wd_ref = (next(refs) for _ in range(6))
    col_stage, row_stage = (next(refs), next(refs)) if staged else (None, None)
    sem = next(refs)
    n_chunks = D_FF // FF_CHUNK
    n_slots = sem.shape[0]
    gate_up = lambda h, c: _silu_gate(h, wg_ref[c], wu_ref[c])
    finish = lambda y: _rms(y, gf_ref[...]) if final_norm else y

    @pl.when(pl.program_id(0) == 0)
    def _():
        def fetch(c, slot):
            span = pl.ds(pl.multiple_of(c * FF_CHUNK, FF_CHUNK), FF_CHUNK)
            src = (wg_hbm.at[:, span], wu_hbm.at[:, span]) if staged else (wg_hbm.at[c], wu_hbm.at[c])
            dst = ((col_stage.at[slot, 0], col_stage.at[slot, 1], row_stage.at[slot]) if staged else
                   (wg_ref.at[c], wu_ref.at[c], wd_ref.at[span, :]))
            return (pltpu.make_async_copy(src[0], dst[0], sem.at[slot, 0]),
                    pltpu.make_async_copy(src[1], dst[1], sem.at[slot, 1]),
                    pltpu.make_async_copy(wd_hbm.at[span, :], dst[2], sem.at[slot, 2]))

        for c in range(n_slots - 1):
            for copy in fetch(c, c):
                copy.start()
        h_ref[...] = _rms(xs_ref[...].reshape(yp_ref.shape), g_ref[...]).astype(BF16)
        yp_ref[...] = jnp.zeros(yp_ref.shape, F32)

        def body(c, carry):
            slot = c % n_slots
            ahead = c + n_slots - 1

            @pl.when(ahead < n_chunks)
            def _():
                for copy in fetch(ahead, ahead % n_slots):
                    copy.start()

            for copy in fetch(c, slot):
                copy.wait()
            rows = pl.ds(pl.multiple_of(c * FF_CHUNK, FF_CHUNK), FF_CHUNK)
            if staged:
                wg_ref[c] = col_stage[slot, 0].astype(BF16)
                wu_ref[c] = col_stage[slot, 1].astype(BF16)
                wd_ref[rows, :] = row_stage[slot].astype(BF16)
            yp_ref[...] += jnp.dot(gate_up(h_ref[...], c), wd_ref[rows, :], preferred_element_type=F32)
            return carry

        lax.fori_loop(0, n_chunks, body, 0)
        ys_ref[...] = finish(xs_ref[...].reshape(yp_ref.shape) + 0.5 * yp_ref[...]).reshape(ys_ref.shape)
        h0 = _rms(xp_ref[...], g_ref[...]).astype(BF16)
        h_ref[...] = h0
        act0_ref[...] = gate_up(h0, 0)

    for c in range(1, n_chunks):
        act_ref[:, (c - 1) * FF_CHUNK:c * FF_CHUNK] = gate_up(h_ref[...], c)
    down0 = jnp.dot(act0_ref[...], wd_ref[0:FF_CHUNK, :], preferred_element_type=F32)
    hn = _rms(xn_ref[...], g_ref[...]).astype(BF16)
    h_ref[...] = hn
    act0_ref[...] = gate_up(hn, 0)
    yp_ref[...] = finish(
        xp_ref[...] + 0.5 * (down0 + jnp.dot(act_ref[...], wd_ref[FF_CHUNK:, :], preferred_element_type=F32)))

    for src, dst in zip(cast_in, cast_out):
        dst[...] = src[...].astype(BF16)
    for src, dst in zip(chunked_in, chunked_out):
        for c in range(n_chunks):
            dst[c] = src[:, c * FF_CHUNK:(c + 1) * FF_CHUNK].astype(BF16)


def _cast_row_blocks(rows, n_steps):
    return max(d for d in range(1, n_steps + 1) if rows % d == 0 and (rows // d) % BF16_SUBLANES == 0)


def _cast_specs(mats, n_steps):
    specs = []
    for w in mats:
        rows, cols = w.shape
        nb = _cast_row_blocks(rows, n_steps)
        specs.append(pl.BlockSpec((rows // nb, cols), lambda i, nb=nb: (jnp.minimum(i, nb - 1), 0)))
    return specs


def _ffn_half(xp, xs, g, wg, wu, wd, gf=None, cast=(), cast_chunked=(), sample_out_shape=None):
    tm = TOKEN_TILE
    n = xp.shape[0]
    n_chunks = D_FF // FF_CHUNK
    staged = wg.dtype == F32
    assert all(w.dtype == wg.dtype for w in (wu, wd))
    chunk_major = (n_chunks, D_MODEL, FF_CHUNK)
    assert all(w.shape == ((D_MODEL, D_FF) if staged else chunk_major) for w in (wg, wu))
    assert all(w.shape == (D_MODEL, D_FF) for w in cast_chunked)
    assert n % tm == 0 and xs.size == tm * D_MODEL and xs.shape[-1] == D_MODEL
    assert D_FF % FF_CHUNK == 0 and n_chunks >= STAGE_SLOTS
    n_tiles = n // tm
    prompt_spec = pl.BlockSpec((tm, D_MODEL), lambda i: (i, 0))
    next_spec = pl.BlockSpec((tm, D_MODEL), lambda i: (jnp.minimum(i + 1, n_tiles - 1), 0))
    hbm = pl.BlockSpec(memory_space=pl.ANY)
    sample_shape = sample_out_shape or (tm, D_MODEL)
    in_specs = [prompt_spec, next_spec, _resident(xs.shape), _resident((1, D_MODEL)), hbm, hbm, hbm]
    args = [xp, xp, xs, g.reshape(1, D_MODEL), wg, wu, wd]
    if gf is not None:
        in_specs.append(_resident((1, D_MODEL)))
        args.append(gf.reshape(1, D_MODEL))
    cast_specs = _cast_specs(cast, n_tiles)
    chunked_in_specs = _cast_specs(cast_chunked, n_tiles)
    chunked_out_specs = [pl.BlockSpec((n_chunks,) + spec.block_shape[:1] + (FF_CHUNK,), lambda i, m=spec.index_map: (0,) + m(i))
                         for spec in chunked_in_specs]
    n_slots = STAGE_SLOTS if staged else n_chunks
    stage = [pltpu.VMEM((n_slots, 2, D_MODEL, FF_CHUNK), F32), pltpu.VMEM((n_slots, FF_CHUNK, D_MODEL), F32)]
    outs = pl.pallas_call(
        functools.partial(_ffn_kernel, final_norm=gf is not None, n_cast=len(cast), n_cast_chunked=len(cast_chunked),
                          staged=staged),
        grid=(n_tiles,),
        in_specs=in_specs + cast_specs + chunked_in_specs,
        out_specs=[prompt_spec, pl.BlockSpec(sample_shape, lambda i: (0,) * len(sample_shape))] + cast_specs
        + chunked_out_specs,
        out_shape=[jax.ShapeDtypeStruct((n, D_MODEL), F32), jax.ShapeDtypeStruct(sample_shape, F32)]
        + [jax.ShapeDtypeStruct(w.shape, BF16) for w in cast] + [jax.ShapeDtypeStruct(chunk_major, BF16)] * len(cast_chunked),
        scratch_shapes=[pltpu.VMEM((tm, D_MODEL), BF16), pltpu.VMEM((tm, FF_CHUNK), BF16),
                        pltpu.VMEM((tm, D_FF - FF_CHUNK), BF16),
                        pltpu.VMEM((n_chunks, D_MODEL, FF_CHUNK), BF16), pltpu.VMEM((n_chunks, D_MODEL, FF_CHUNK), BF16),
                        pltpu.VMEM((D_FF, D_MODEL), BF16)] + (stage if staged else [])
        + [pltpu.SemaphoreType.DMA((n_slots, 3))],
        compiler_params=pltpu.CompilerParams(dimension_semantics=("arbitrary",), vmem_limit_bytes=VMEM_LIMIT_BYTES),
        name="ffn_final" if gf is not None else "ffn_half",
    )(*args, *cast, *cast_chunked)
    return outs[0], outs[1], list(outs[2:])


def _first_half(shape):
    return (lax.broadcasted_iota(jnp.int32, shape, 1) & (HEAD_DIM - 1)) < HEAD_DIM // 2


def _sign_sin(sin):
    return jnp.where(_first_half(sin.shape), -sin, sin)


def _rope(xg, cos, sin_signed):
    swapped = jnp.where(_first_half(xg.shape), pltpu.roll(xg, LANES - HEAD_DIM // 2, 1),
                        pltpu.roll(xg, HEAD_DIM // 2, 1))
    return xg * cos + swapped * sin_signed


def _in_proj(x, gmix, w_in, cos, sin_signed, gvn_gain):
    h = _rms(x, gmix).astype(BF16)
    z = jnp.dot(h, w_in, preferred_element_type=F32)
    q = jnp.concatenate(
        [(_rope(z[:, LANES * i:LANES * (i + 1)], cos, sin_signed) * Q_SCALE).astype(BF16) for i in range(Q_W // LANES)],
        axis=1)
    k = _rope(z[:, K_OFF:V_OFF], cos, sin_signed)
    v = z[:, V_OFF:U_OFF]
    u = _gelu(z[:, U_OFF:GV_OFF])
    gvn = _rms(_gelu(z[:, GV_OFF:]), gvn_gain)
    return q, k, v, u, gvn


def _out_proj(x, ya, yg, ga, gg, w_out):
    cat = jnp.concatenate([_rms(ya, ga).astype(BF16), _rms(yg, gg).astype(BF16)], axis=1)
    return x + jnp.dot(cat, w_out, preferred_element_type=F32)


def _softmax_sink(s, sink):
    m = jnp.maximum(jnp.max(s, axis=1, keepdims=True), sink)
    p = jnp.exp2(s - m)
    return p, jnp.sum(p, axis=1, keepdims=True) + jnp.exp2(sink - m)


def _gelu(x):
    k = -2.0 * (2.0 / jnp.pi) ** 0.5 * LOG2E
    return x / (1.0 + jnp.exp2(x * (k + (0.044715 * k) * (x * x))))


def _mix_prompt_kernel(xa_ref, xc_ref, coff_ref, soff_ref, cbase_ref, sbase_ref, gmix_ref, win_ref, sinks_ref, gvg_ref,
                       wcat_ref, bias_ref, ga_ref, gg_ref, wout_ref, o_ref, ko_ref, vo_ref, gvo_ref, q_s, k_s, v_s, u_s,
                       gv_s, cat_s, *, tm, tiles_per_seq, n_tiles):
    s = pl.program_id(0)

    @pl.when(s == 0)
    def _():
        k_s[1, tm:tm + WINDOW, :] = jnp.zeros((WINDOW, KV_W), BF16)
        v_s[1, tm:tm + WINDOW, :] = jnp.zeros((WINDOW, KV_W), BF16)
        cat_s[1] = jnp.zeros((tm, D_MODEL), BF16)

    step = functools.partial(
        _mix_prompt_step, s, xa_ref, xc_ref, coff_ref, soff_ref, cbase_ref, sbase_ref, gmix_ref, win_ref, sinks_ref,
        gvg_ref, wcat_ref, bias_ref, ga_ref, gg_ref, wout_ref, o_ref, ko_ref, vo_ref, gvo_ref, q_s, k_s, v_s, u_s, gv_s,
        cat_s, tm=tm, tiles_per_seq=tiles_per_seq, n_tiles=n_tiles)
    last = n_tiles + 1
    pl.when(s == 0)(functools.partial(step, cur=0, stages="proj"))
    pl.when(s == last)(functools.partial(step, cur=last % 2, stages="out"))
    for parity in range(2):
        pl.when((s > 0) & (s < last) & (s % 2 == parity))(functools.partial(step, cur=parity, stages="all"))


def _mix_prompt_step(s, xa_ref, xc_ref, coff_ref, soff_ref, cbase_ref, sbase_ref, gmix_ref, win_ref, sinks_ref, gvg_ref,
                     wcat_ref, bias_ref, ga_ref, gg_ref, wout_ref, o_ref, ko_ref, vo_ref, gvo_ref, q_s, k_s, v_s, u_s, gv_s,
                     cat_s, *, tm, tiles_per_seq, n_tiles, cur, stages):
    oth = 1 - cur

    h = _rms(xa_ref[0], gmix_ref[...]).astype(BF16)
    tile_in_seq = jnp.minimum(s, n_tiles - 1) % tiles_per_seq
    cb = cbase_ref[pl.ds(tile_in_seq, 1), :]
    sb = sbase_ref[pl.ds(tile_in_seq, 1), :]
    cos = cb * coff_ref[...] - sb * soff_ref[...]
    sin = _sign_sin(sb * coff_ref[...] + cb * soff_ref[...])

    def proj_q():
        z = jnp.dot(h, win_ref[:, 0:Q_W], preferred_element_type=F32)
        for i in range(Q_W // LANES):
            q_s[cur, :, LANES * i:LANES * (i + 1)] = (
                _rope(z[:, LANES * i:LANES * (i + 1)], cos, sin) * Q_SCALE).astype(BF16)

    def proj_kv():
        z = jnp.dot(h, win_ref[:, K_OFF:U_OFF], preferred_element_type=F32)
        k = _rope(z[:, 0:KV_W], cos, sin)
        v = z[:, KV_W:]
        k_s[cur, 0:WINDOW, :] = k_s[oth, tm:tm + WINDOW, :]
        v_s[cur, 0:WINDOW, :] = v_s[oth, tm:tm + WINDOW, :]
        k_s[cur, WINDOW:, :] = k.astype(BF16)
        v_s[cur, WINDOW:, :] = v.astype(BF16)
        ko_ref[0] = k[tm - WINDOW:].T
        vo_ref[0] = v[tm - WINDOW:].T

    def proj_u():
        u_s[cur] = _gelu(jnp.dot(h, win_ref[:, U_OFF:GV_OFF], preferred_element_type=F32))

    def proj_gv():
        gvn = _rms(_gelu(jnp.dot(h, win_ref[:, GV_OFF:], preferred_element_type=F32)), gvg_ref[...])
        gv_s[cur] = gvn.astype(BF16)
        gvo_ref[0] = gvn[tm - CHUNK:].T

    def out_half(c):
        cols = slice(c * (D_MODEL // 2), (c + 1) * (D_MODEL // 2))

        def run():
            o_ref[0, :, cols] = xc_ref[0, :, cols] + jnp.dot(cat_s[cur], wout_ref[:, cols], preferred_element_type=F32)
        return run

    mid_fill = [[out_half(0)], [proj_q], [proj_kv, proj_u], [proj_gv]]
    end_fill = [[], [], [], [out_half(1)]]
    assert len(mid_fill) == len(end_fill) == tm // WINDOW
    if stages == "proj":
        for run in (proj_q, proj_kv, proj_u, proj_gv):
            run()
        return
    if stages == "out":
        out_half(0)()
        out_half(1)()
        return

    wrow = lax.broadcasted_iota(jnp.int32, (CHUNK, CHUNK), 0)
    wcol = lax.broadcasted_iota(jnp.int32, (CHUNK, CHUNK), 1)
    wtril = [jnp.where(wcol <= wrow, wcat_ref[hd], 0.0).astype(BF16) for hd in range(G_HEADS)]
    wmix = [jnp.concatenate(wtril[2 * p:2 * p + 2], axis=1) for p in range(G_HEADS // 2)]
    lane = lax.broadcasted_iota(jnp.int32, (CHUNK, LANES), 1)
    low_head = lane < HEAD_DIM

    qi = lax.broadcasted_iota(jnp.int32, (WINDOW, 2 * WINDOW), 0)
    sj = lax.broadcasted_iota(jnp.int32, (WINDOW, 2 * WINDOW), 1)
    dist = WINDOW + qi - sj
    band = (dist >= 0) & (dist < WINDOW)
    first_lo = jnp.where((s + tiles_per_seq - 1) % tiles_per_seq == 0, WINDOW, 0)

    for j in range(tm // WINDOW):
        rows = slice(j * WINDOW, (j + 1) * WINDOW)
        mask = band & (sj >= first_lo) if j == 0 else band
        qb = q_s[oth, rows, :]
        scores = []
        for kv in range(N_KV):
            qs = jnp.concatenate(
                [qb[:, HEAD_DIM * hd:HEAD_DIM * (hd + 1)] for hd in range(kv * GQA, (kv + 1) * GQA)], axis=0)
            kb = k_s[oth, j * WINDOW:(j + 2) * WINDOW, HEAD_DIM * kv:HEAD_DIM * (kv + 1)]
            scores.append(lax.dot_general(qs, kb, _NT, preferred_element_type=F32))
        for run in mid_fill[j]:
            run()
        outs, dens = [], []
        for kv in range(N_KV):
            vb = v_s[oth, j * WINDOW:(j + 2) * WINDOW, HEAD_DIM * kv:HEAD_DIM * (kv + 1)]
            ps = []
            for g in range(GQA):
                sg = jnp.where(mask, scores[kv][g * WINDOW:(g + 1) * WINDOW], -jnp.inf)
                p, den = _softmax_sink(sg, sinks_ref[kv * GQA + g] * LOG2E)
                ps.append(p.astype(BF16))
                dens.append(den)
            outs.append(jnp.dot(jnp.concatenate(ps, axis=0), vb, preferred_element_type=F32))
        mixed = []
        for p in range(G_HEADS // 2):
            r = gv_s[oth, rows, LANES * p:LANES * (p + 1)]
            zero = jnp.zeros_like(r)
            rhs = jnp.concatenate([jnp.where(low_head, r, zero), jnp.where(low_head, zero, r)], axis=0)
            mixed.append(jnp.dot(wmix[p], rhs, preferred_element_type=F32))
        for run in end_fill[j]:
            run()
        ya = [outs[hd // GQA][(hd % GQA) * WINDOW:(hd % GQA + 1) * WINDOW] / dens[hd] for hd in range(N_HEADS)]
        cat_s[oth, rows, 0:Q_W] = _rms(jnp.concatenate(ya, axis=1), ga_ref[...]).astype(BF16)
        yg = u_s[oth, rows, :] * (jnp.concatenate(mixed, axis=1) + bias_ref[...])
        cat_s[oth, rows, Q_W:] = _rms(yg, gg_ref[...]).astype(BF16)


def _mix_prompt(x, gmix, w_in, sinks, gvg, wcat, bias_full, ga, gg, w_out):
    b, s, _ = x.shape
    tm = TOKEN_TILE
    assert s % tm == 0 and tm % WINDOW == 0
    tiles_per_seq = s // tm
    n_tiles = b * tiles_per_seq
    cos_off, sin_off = _rope_tables(jnp.arange(tm, dtype=jnp.int32))
    cos_base, sin_base = _rope_tables(tm * jnp.arange(tiles_per_seq, dtype=jnp.int32))
    proj_tile = lambda i: jnp.minimum(i, n_tiles - 1)
    out_tile = lambda i: jnp.maximum(i - 2, 0)
    x_tiles = x.reshape(n_tiles, tm, D_MODEL)
    last = lambda width: pl.BlockSpec((1, width, WINDOW), lambda i: (proj_tile(i) // tiles_per_seq, 0, 0))
    out, ko, vo, gvo = pl.pallas_call(
        functools.partial(_mix_prompt_kernel, tm=tm, tiles_per_seq=tiles_per_seq, n_tiles=n_tiles),
        grid=(n_tiles + 2,),
        in_specs=[pl.BlockSpec((1, tm, D_MODEL), lambda i: (proj_tile(i), 0, 0)),
                  pl.BlockSpec((1, tm, D_MODEL), lambda i: (out_tile(i), 0, 0)),
                  _resident((tm, LANES)), _resident((tm, LANES)), _resident((tiles_per_seq, LANES)),
                  _resident((tiles_per_seq, LANES)), _resident((1, D_MODEL)), _resident((D_MODEL, D_IN)),
                  pl.BlockSpec(memory_space=pltpu.SMEM), _resident((1, D_GMLP)),
                  _resident((G_HEADS, CHUNK, CHUNK)), _resident((CHUNK, D_GMLP)), _resident((1, Q_W)),
                  _resident((1, D_GMLP)), _resident((D_MODEL, D_MODEL))],
        out_specs=[pl.BlockSpec((1, tm, D_MODEL), lambda i: (out_tile(i), 0, 0)), last(KV_W), last(KV_W),
                   last(D_GMLP)],
        out_shape=[jax.ShapeDtypeStruct((n_tiles, tm, D_MODEL), F32), jax.ShapeDtypeStruct((b, KV_W, WINDOW), F32),
                   jax.ShapeDtypeStruct((b, KV_W, WINDOW), F32), jax.ShapeDtypeStruct((b, D_GMLP, CHUNK), F32)],
        scratch_shapes=[pltpu.VMEM((2, tm, Q_W), BF16), pltpu.VMEM((2, tm + WINDOW, KV_W), BF16),
                        pltpu.VMEM((2, tm + WINDOW, KV_W), BF16), pltpu.VMEM((2, tm, D_GMLP), F32),
                        pltpu.VMEM((2, tm, D_GMLP), BF16), pltpu.VMEM((2, tm, D_MODEL), BF16)],
        compiler_params=pltpu.CompilerParams(dimension_semantics=("arbitrary",), vmem_limit_bytes=VMEM_LIMIT_BYTES),
        name="mix_prompt",
    )(x_tiles, x_tiles, cos_off, sin_off, cos_base, sin_base, gmix, w_in, sinks, gvg, wcat, bias_full, ga, gg, w_out)
    return out.reshape(b, s, D_MODEL), ko, vo, gvo


def _mix_sample_kernel(x_ref, ckt_ref, cvt_ref, cos_ref, sin_ref, gmix_ref, win_ref, sinks_ref, gvg_ref, coef_ref,
                       biasr_ref, ga_ref, gg_ref, wout_ref, o_ref, kot_ref, vot_ref, gvo_ref, q_s, k_s, v_s, ya_s,
                       yg_s, *, t_new, w_buf):
    step = pl.program_id(0)
    n_tok = x_ref.shape[0]
    step_seqs = SEQ_GROUP * GROUPS_PER_STEP
    step_rows = step_seqs * t_new
    grp_rows = SEQ_GROUP * t_new
    grp_keys = SEQ_GROUP * w_buf
    sub = 8

    @pl.when(step == 0)
    def _():
        tile = lambda tab: jnp.broadcast_to(tab[None], (n_tok // sub, sub, LANES)).reshape(n_tok, LANES)
        q, k, v, u, gvn = _in_proj(x_ref[...], gmix_ref[...], win_ref[...], tile(cos_ref[...]),
                                   tile(_sign_sin(sin_ref[...])), gvg_ref[...])
        q_s[...] = q
        k_s[...] = k
        v_s[...] = v
        gv_seq = gvn.reshape(n_tok // t_new, t_new, D_GMLP)
        for tok in range(t_new):
            gvo_ref[tok] = gv_seq[:, tok, :].T
        g3 = gvn.reshape(n_tok // sub, sub, D_GMLP)
        trow = lax.broadcasted_iota(jnp.int32, (1, sub, D_GMLP), 1) & (t_new - 1)
        mixed = biasr_ref[...][None] + coef_ref[0][None] * g3
        for d in range(1, t_new):
            shifted = jnp.where(trow >= d, pltpu.roll(g3, d, 1), 0.0)
            mixed = mixed + coef_ref[d][None] * shifted
        yg_s[...] = (u.reshape(n_tok // sub, sub, D_GMLP) * mixed).reshape(n_tok, D_GMLP)

    row0 = pl.multiple_of(step * step_rows, step_rows)
    q_step = q_s[pl.ds(row0, step_rows), :]
    kn = k_s[pl.ds(row0, step_rows), :]
    vn = v_s[pl.ds(row0, step_rows), :]

    knt = kn.T
    vnt = vn.T
    tail = lax.broadcasted_iota(jnp.int32, (KV_W, w_buf), 1) >= w_buf - t_new
    for b in range(step_seqs):
        shift = (w_buf - t_new - t_new * b) % w_buf
        kot_ref[b] = jnp.where(tail, pltpu.roll(knt, shift, 1) if shift else knt,
                               pltpu.roll(ckt_ref[b], w_buf - t_new, 1))
        vot_ref[b] = jnp.where(tail, pltpu.roll(vnt, shift, 1) if shift else vnt,
                               pltpu.roll(cvt_ref[b], w_buf - t_new, 1))

    knb = kn.astype(BF16)
    vnb = vn.astype(BF16)

    n_rows = GQA * grp_rows
    shift_t = t_new.bit_length() - 1
    shift_w = w_buf.bit_length() - 1
    r = lax.broadcasted_iota(jnp.int32, (n_rows, grp_keys), 0)
    c = lax.broadcasted_iota(jnp.int32, (n_rows, grp_keys), 1)
    mask_c = ((c >> shift_w) == ((r & (grp_rows - 1)) >> shift_t)) & ((c & (w_buf - 1)) > (r & (t_new - 1)))
    r2 = lax.broadcasted_iota(jnp.int32, (n_rows, grp_rows), 0)
    c2 = lax.broadcasted_iota(jnp.int32, (n_rows, grp_rows), 1)
    mask_n = ((c2 >> shift_t) == ((r2 & (grp_rows - 1)) >> shift_t)) & ((c2 & (t_new - 1)) <= (r2 & (t_new - 1)))
    row_head = lax.broadcasted_iota(jnp.int32, (n_rows, 1), 0) >> (grp_rows.bit_length() - 1)

    for grp in range(GROUPS_PER_STEP):
        rows = slice(grp * grp_rows, (grp + 1) * grp_rows)
        seqs = range(grp * SEQ_GROUP, (grp + 1) * SEQ_GROUP)
        for kv in range(N_KV):
            heads = [kv * GQA + i for i in range(GQA)]
            lanes = slice(HEAD_DIM * kv, HEAD_DIM * (kv + 1))
            kt = jnp.concatenate([ckt_ref[b, lanes, :] for b in seqs], axis=1).astype(BF16)
            vt = jnp.concatenate([cvt_ref[b, lanes, :] for b in seqs], axis=1).astype(BF16)
            qs = jnp.concatenate([q_step[rows, HEAD_DIM * hd:HEAD_DIM * (hd + 1)] for hd in heads], axis=0)
            s_c = jnp.where(mask_c, jnp.dot(qs, kt, preferred_element_type=F32), -jnp.inf)
            s_n = jnp.where(mask_n, lax.dot_general(qs, knb[rows, lanes], _NT, preferred_element_type=F32), -jnp.inf)
            sink = jnp.full((n_rows, 1), sinks_ref[heads[0]] * LOG2E, F32)
            for i in range(1, GQA):
                sink = jnp.where(row_head == i, sinks_ref[heads[i]] * LOG2E, sink)
            m = jnp.maximum(jnp.maximum(jnp.max(s_c, axis=1, keepdims=True), jnp.max(s_n, axis=1, keepdims=True)),
                            sink)
            p_c = jnp.exp2(s_c - m)
            p_n = jnp.exp2(s_n - m)
            den = jnp.sum(p_c, axis=1, keepdims=True) + jnp.sum(p_n, axis=1, keepdims=True) + jnp.exp2(sink - m)
            o = (lax.dot_general(p_c.astype(BF16), vt, _NT, preferred_element_type=F32)
                 + jnp.dot(p_n.astype(BF16), vnb[rows, lanes], preferred_element_type=F32)) / den
            for i, hd in enumerate(heads):
                ya_s[pl.ds(row0 + grp * grp_rows, grp_rows), HEAD_DIM * hd:HEAD_DIM * (hd + 1)] = (
                    o[i * grp_rows:(i + 1) * grp_rows])

    @pl.when(step == pl.num_programs(0) - 1)
    def _():
        o_ref[...] = _out_proj(x_ref[...], ya_s[...], yg_s[...], ga_ref[...], gg_ref[...], wout_ref[...])


def _mix_sample(x, cache_kt, cache_vt, cos, sin, gmix, w_in, sinks, gvg, coef, bias_rows, ga, gg, w_out, t_new):
    n_tok = x.shape[0]
    n_seq, _, w_buf = cache_kt.shape
    step_seqs = SEQ_GROUP * GROUPS_PER_STEP
    assert n_seq % step_seqs == 0 and n_tok == n_seq * t_new
    assert t_new & (t_new - 1) == 0 and w_buf & (w_buf - 1) == 0 and 8 % t_new == 0
    assert step_seqs * t_new == w_buf == LANES
    cache_spec = pl.BlockSpec((step_seqs, KV_W, w_buf), lambda i: (i, 0, 0))
    return pl.pallas_call(
        functools.partial(_mix_sample_kernel, t_new=t_new, w_buf=w_buf),
        grid=(n_seq // step_seqs,),
        in_specs=[_resident((n_tok, D_MODEL)), cache_spec, cache_spec, _resident((8, LANES)),
                  _resident((8, LANES)), _resident((1, D_MODEL)), _resident((D_MODEL, D_IN)),
                  pl.BlockSpec(memory_space=pltpu.SMEM), _resident((1, D_GMLP)), _resident((t_new, 8, D_GMLP)),
                  _resident((8, D_GMLP)), _resident((1, Q_W)), _resident((1, D_GMLP)),
                  _resident((D_MODEL, D_MODEL))],
        out_specs=[pl.BlockSpec((n_tok, D_MODEL), lambda i: (0, 0)), cache_spec, cache_spec,
                   pl.BlockSpec((t_new, D_GMLP, n_seq), lambda i: (0, 0, 0))],
        out_shape=[jax.ShapeDtypeStruct((n_tok, D_MODEL), F32), jax.ShapeDtypeStruct(cache_kt.shape, F32),
                   jax.ShapeDtypeStruct(cache_vt.shape, F32), jax.ShapeDtypeStruct((t_new, D_GMLP, n_seq), F32)],
        scratch_shapes=[pltpu.VMEM((n_tok, Q_W), BF16), pltpu.VMEM((n_tok, KV_W), F32), pltpu.VMEM((n_tok, KV_W), F32),
                        pltpu.VMEM((n_tok, Q_W), F32), pltpu.VMEM((n_tok, D_GMLP), F32)],
        compiler_params=pltpu.CompilerParams(dimension_semantics=("arbitrary",), vmem_limit_bytes=VMEM_LIMIT_BYTES),
        name="mix_sample",
    )(x, cache_kt, cache_vt, cos, sin, gmix, w_in, sinks, gvg, coef, bias_rows, ga, gg, w_out)


def _gmlp_tables_kernel(ws_ref, bs_ref, bias_ref, coef_ref, biasr_ref, *, t_new):
    sub = 8
    expand = lambda col: jnp.broadcast_to(col, (col.shape[0], D_GMLP // G_HEADS))
    row = lax.broadcasted_iota(jnp.int32, (CHUNK, CHUNK), 0)
    lane = lax.broadcasted_iota(jnp.int32, (CHUNK, CHUNK), 1)
    bias = jnp.concatenate(
        [expand(jnp.sum(jnp.where(lane == row, bs_ref[hd:hd + 1, :], 0.0), axis=1, keepdims=True))
         for hd in range(G_HEADS)], axis=1)
    bias_ref[...] = bias
    biasr_ref[...] = jnp.concatenate([bias[0:t_new]] * (sub // t_new), axis=0)
    t = lax.broadcasted_iota(jnp.int32, (sub, CHUNK), 0) & (t_new - 1)
    s = lax.broadcasted_iota(jnp.int32, (sub, CHUNK), 1)
    for d in range(t_new):
        cols = []
        for hd in range(G_HEADS):
            w_rows = jnp.concatenate([ws_ref[hd, 0:t_new, :]] * (sub // t_new), axis=0)
            cols.append(expand(jnp.sum(jnp.where(s == t - d, w_rows, 0.0), axis=1, keepdims=True)))
        coef_ref[d] = jnp.concatenate(cols, axis=1)


def _gmlp_tables(ws, bs, t_new):
    assert 8 % t_new == 0
    return pl.pallas_call(
        functools.partial(_gmlp_tables_kernel, t_new=t_new),
        out_shape=[jax.ShapeDtypeStruct((CHUNK, D_GMLP), F32), jax.ShapeDtypeStruct((t_new, 8, D_GMLP), F32),
                   jax.ShapeDtypeStruct((8, D_GMLP), F32)],
        name="gmlp_tables",
    )(ws, bs)


def _rope_tables(pos):
    inv_freq = ROPE_THETA ** (-jnp.arange(0, HEAD_DIM, 2, dtype=F32) / HEAD_DIM)
    ang = pos.astype(F32)[:, None] * jnp.tile(inv_freq, 2 * LANES // HEAD_DIM)[None, :]
    return jnp.cos(ang), jnp.sin(ang)


def kernel(x_prompt, x_sample, cache_k_win, cache_v_win, norm_ffn1, ffn1_gate, ffn1_up, ffn1_down, norm_mix, w_in,
           attn_sinks, gmlp_v_norm, gmlp_w_s, gmlp_b_s, norm_attn_out, norm_gmlp_out, w_out, norm_ffn2, ffn2_gate,
           ffn2_up, ffn2_down, norm_final):
    depth = norm_ffn1.shape[0]
    b, s, _ = x_prompt.shape
    bd, t_new, _ = x_sample.shape
    w_buf = cache_k_win.shape[2]

    cos_s, sin_s = _rope_tables(PAST_LEN + jnp.arange(t_new, dtype=jnp.int32))
    cos_s, sin_s = jnp.tile(cos_s, (8 // t_new, 1)), jnp.tile(sin_s, (8 // t_new, 1))

    hp = x_prompt.reshape(b * s, D_MODEL)
    hs = x_sample
    outs = [[] for _ in range(6)]
    ffn1_w = [ffn1_gate[0], ffn1_up[0], ffn1_down[0]]
    for l in range(depth):
        last = l == depth - 1
        row = lambda a: a[l].reshape(1, -1)

        wcat = gmlp_w_s[l]
        bias_full, coef, bias_rows = _gmlp_tables(gmlp_w_s[l], gmlp_b_s[l], t_new)

        hp, hs, (w_in_b, w_out_b, wd2, wg2, wu2) = _ffn_half(
            hp, hs, norm_ffn1[l], *ffn1_w, cast=(w_in[l], w_out[l], ffn2_down[l]), cast_chunked=(ffn2_gate[l], ffn2_up[l]))

        hp, kpt, vpt, gvpt = _mix_prompt(hp.reshape(b, s, D_MODEL), row(norm_mix), w_in_b, attn_sinks[l],
                                         row(gmlp_v_norm), wcat, bias_full, row(norm_attn_out), row(norm_gmlp_out),
                                         w_out_b)
        to_t = lambda c: c.transpose(0, 2, 3, 1).reshape(bd, KV_W, w_buf)
        hs, kst, vst, gvs = _mix_sample(hs, to_t(cache_k_win[l]), to_t(cache_v_win[l]), cos_s, sin_s, row(norm_mix),
                                        w_in_b, attn_sinks[l], row(gmlp_v_norm), coef, bias_rows, row(norm_attn_out),
                                        row(norm_gmlp_out), w_out_b, t_new)
        from_t = lambda c: c.reshape(bd, N_KV, HEAD_DIM, w_buf).transpose(0, 3, 1, 2)

        hp, hs, next_w = _ffn_half(hp.reshape(b * s, D_MODEL), hs, norm_ffn2[l], wg2, wu2, wd2,
                                   gf=norm_final if last else None, cast=() if last else (ffn1_down[l + 1],),
                                   cast_chunked=() if last else (ffn1_gate[l + 1], ffn1_up[l + 1]),
                                   sample_out_shape=(bd, t_new, D_MODEL) if last else None)
        ffn1_w = next_w[1:] + next_w[:1]

        outs[0].append(kpt.reshape(b, N_KV, HEAD_DIM, WINDOW).transpose(0, 3, 1, 2))
        outs[1].append(vpt.reshape(b, N_KV, HEAD_DIM, WINDOW).transpose(0, 3, 1, 2))
        outs[2].append(from_t(kst))
        outs[3].append(from_t(vst))
        outs[4].append(gvpt.reshape(b, G_HEADS, D_GMLP // G_HEADS, CHUNK).transpose(0, 3, 1, 2))
        outs[5].append(gvs.reshape(t_new, G_HEADS, D_GMLP // G_HEADS, bd).transpose(3, 0, 1, 2))

    return (hp.reshape(b, s, D_MODEL), hs) + tuple(jnp.stack(o) for o in outs)
```

```python
import functools

import jax
import jax.numpy as jnp
from jax import lax
from jax.experimental import pallas as pl
from jax.experimental.pallas import tpu as pltpu

F32 = jnp.float32
BF16 = jnp.bfloat16

D_MODEL = 1024
D_FF = 2816
HEAD_DIM = 64
N_HEADS = 8
N_KV = 2
GQA = N_HEADS // N_KV
WINDOW = 128
CHUNK = 128
G_HEADS = 8
Q_W = N_HEADS * HEAD_DIM
KV_W = N_KV * HEAD_DIM
D_GMLP = 512
D_IN = Q_W + 2 * KV_W + 2 * D_GMLP
K_OFF = Q_W
V_OFF = K_OFF + KV_W
U_OFF = V_OFF + KV_W
GV_OFF = U_OFF + D_GMLP
ROPE_THETA = 10000.0
PAST_LEN = 16384
EPS = 1e-6
LOG2E = 1.4426950408889634
Q_SCALE = HEAD_DIM ** -0.5 * LOG2E
LANES = 128
BF16_SUBLANES = 16

TOKEN_TILE = 512
FF_CHUNK = 256
STAGE_SLOTS = 3
SEQ_GROUP = 8
GROUPS_PER_STEP = 4
VMEM_LIMIT_BYTES = 56 * 1024 * 1024

_NT = (((1,), (1,)), ((), ()))


def _rms(x, g):
    ms = jnp.mean(x * x, axis=-1, keepdims=True)
    return (x * lax.rsqrt(ms + EPS)) * g


def _resident(shape):
    zeros = (0,) * len(shape)
    return pl.BlockSpec(shape, lambda *_: zeros, pipeline_mode=pl.Buffered(1))


def _silu_gate(h, wg, wu):
    gate = jnp.dot(h, wg, preferred_element_type=F32)
    up = jnp.dot(h, wu, preferred_element_type=F32)
    return (gate * jax.nn.sigmoid(gate) * up).astype(BF16)


def _ffn_kernel(*refs, final_norm, n_cast, n_cast_chunked, staged):
    refs = iter(refs)
    xp_ref, xn_ref, xs_ref, g_ref, wg_hbm, wu_hbm, wd_hbm = (next(refs) for _ in range(7))
    gf_ref = next(refs) if final_norm else None
    cast_in = [next(refs) for _ in range(n_cast)]
    chunked_in = [next(refs) for _ in range(n_cast_chunked)]
    yp_ref, ys_ref = next(refs), next(refs)
    cast_out = [next(refs) for _ in range(n_cast)]
    chunked_out = [next(refs) for _ in range(n_cast_chunked)]
    h_ref, act0_ref, act_ref, wg_ref, wu_ref, wd_ref = (next(refs) for _ in range(6))
    col_stage, row_stage = (next(refs), next(refs)) if staged else (None, None)
    sem = next(refs)
    n_chunks = D_FF // FF_CHUNK
    n_slots = sem.shape[0]
    gate_up = lambda h, c: _silu_gate(h, wg_ref[c], wu_ref[c])
    finish = lambda y: _rms(y, gf_ref[...]) if final_norm else y

    @pl.when(pl.program_id(0) == 0)
    def _():
        def fetch(c, slot):
            span = pl.ds(pl.multiple_of(c * FF_CHUNK, FF_CHUNK), FF_CHUNK)
            src = (wg_hbm.at[:, span], wu_hbm.at[:, span]) if staged else (wg_hbm.at[c], wu_hbm.at[c])
            dst = ((col_stage.at[slot, 0], col_stage.at[slot, 1], row_stage.at[slot]) if staged else
                   (wg_ref.at[c], wu_ref.at[c], wd_ref.at[span, :]))
            return (pltpu.make_async_copy(src[0], dst[0], sem.at[slot, 0]),
                    pltpu.make_async_copy(src[1], dst[1], sem.at[slot, 1]),
                    pltpu.make_async_copy(wd_hbm.at[span, :], dst[2], sem.at[slot, 2]))

        for c in range(n_slots - 1):
            for copy in fetch(c, c):
                copy.start()
        h_ref[...] = _rms(xs_ref[...].reshape(yp_ref.shape), g_ref[...]).astype(BF16)
        yp_ref[...] = jnp.zeros(yp_ref.shape, F32)

        def body(c, carry):
            slot = c % n_slots
            ahead = c + n_slots - 1

            @pl.when(ahead < n_chunks)
            def _():
                for copy in fetch(ahead, ahead % n_slots):
                    copy.start()

            for copy in fetch(c, slot):
                copy.wait()
            rows = pl.ds(pl.multiple_of(c * FF_CHUNK, FF_CHUNK), FF_CHUNK)
            if staged:
                wg_ref[c] = col_stage[slot, 0].astype(BF16)
                wu_ref[c] = col_stage[slot, 1].astype(BF16)
                wd_ref[rows, :] = row_stage[slot].astype(BF16)
            yp_ref[...] += jnp.dot(gate_up(h_ref[...], c), wd_ref[rows, :], preferred_element_type=F32)
            return carry

        lax.fori_loop(0, n_chunks, body, 0)
        ys_ref[...] = finish(xs_ref[...].reshape(yp_ref.shape) + 0.5 * yp_ref[...]).reshape(ys_ref.shape)
        h0 = _rms(xp_ref[...], g_ref[...]).astype(BF16)
        h_ref[...] = h0
        act0_ref[...] = gate_up(h0, 0)

    for c in range(1, n_chunks):
        act_ref[:, (c - 1) * FF_CHUNK:c * FF_CHUNK] = gate_up(h_ref[...], c)
    down0 = jnp.dot(act0_ref[...], wd_ref[0:FF_CHUNK, :], preferred_element_type=F32)
    hn = _rms(xn_ref[...], g_ref[...]).astype(BF16)
    h_ref[...] = hn
    act0_ref[...] = gate_up(hn, 0)
    yp_ref[...] = finish(
        xp_ref[...] + 0.5 * (down0 + jnp.dot(act_ref[...], wd_ref[FF_CHUNK:, :], preferred_element_type=F32)))

    for src, dst in zip(cast_in, cast_out):
        dst[...] = src[...].astype(BF16)
    for src, dst in zip(chunked_in, chunked_out):
        for c in range(n_chunks):
            dst[c] = src[:, c * FF_CHUNK:(c + 1) * FF_CHUNK].astype(BF16)


def _cast_row_blocks(rows, n_steps):
    return max(d for d in range(1, n_steps + 1) if rows % d == 0 and (rows // d) % BF16_SUBLANES == 0)


def _cast_specs(mats, n_steps):
    specs = []
    for w in mats:
        rows, cols = w.shape
        nb = _cast_row_blocks(rows, n_steps)
        specs.append(pl.BlockSpec((rows // nb, cols), lambda i, nb=nb: (jnp.minimum(i, nb - 1), 0)))
    return specs


def _ffn_half(xp, xs, g, wg, wu, wd, gf=None, cast=(), cast_chunked=(), sample_out_shape=None):
    tm = TOKEN_TILE
    n = xp.shape[0]
    n_chunks = D_FF // FF_CHUNK
    staged = wg.dtype == F32
    assert all(w.dtype == wg.dtype for w in (wu, wd))
    chunk_major = (n_chunks, D_MODEL, FF_CHUNK)
    assert all(w.shape == ((D_MODEL, D_FF) if staged else chunk_major) for w in (wg, wu))
    assert all(w.shape == (D_MODEL, D_FF) for w in cast_chunked)
    assert n % tm == 0 and xs.size == tm * D_MODEL and xs.shape[-1] == D_MODEL
    assert D_FF % FF_CHUNK == 0 and n_chunks >= STAGE_SLOTS
    n_tiles = n // tm
    prompt_spec = pl.BlockSpec((tm, D_MODEL), lambda i: (i, 0))
    next_spec = pl.BlockSpec((tm, D_MODEL), lambda i: (jnp.minimum(i + 1, n_tiles - 1), 0))
    hbm = pl.BlockSpec(memory_space=pl.ANY)
    sample_shape = sample_out_shape or (tm, D_MODEL)
    in_specs = [prompt_spec, next_spec, _resident(xs.shape), _resident((1, D_MODEL)), hbm, hbm, hbm]
    args = [xp, xp, xs, g.reshape(1, D_MODEL), wg, wu, wd]
    if gf is not None:
        in_specs.append(_resident((1, D_MODEL)))
        args.append(gf.reshape(1, D_MODEL))
    cast_specs = _cast_specs(cast, n_tiles)
    chunked_in_specs = _cast_specs(cast_chunked, n_tiles)
    chunked_out_specs = [pl.BlockSpec((n_chunks,) + spec.block_shape[:1] + (FF_CHUNK,), lambda i, m=spec.index_map: (0,) + m(i))
                         for spec in chunked_in_specs]
    n_slots = STAGE_SLOTS if staged else n_chunks
    stage = [pltpu.VMEM((n_slots, 2, D_MODEL, FF_CHUNK), F32), pltpu.VMEM((n_slots, FF_CHUNK, D_MODEL), F32)]
    outs = pl.pallas_call(
        functools.partial(_ffn_kernel, final_norm=gf is not None, n_cast=len(cast), n_cast_chunked=len(cast_chunked),
                          staged=staged),
        grid=(n_tiles,),
        in_specs=in_specs + cast_specs + chunked_in_specs,
        out_specs=[prompt_spec, pl.BlockSpec(sample_shape, lambda i: (0,) * len(sample_shape))] + cast_specs
        + chunked_out_specs,
        out_shape=[jax.ShapeDtypeStruct((n, D_MODEL), F32), jax.ShapeDtypeStruct(sample_shape, F32)]
        + [jax.ShapeDtypeStruct(w.shape, BF16) for w in cast] + [jax.ShapeDtypeStruct(chunk_major, BF16)] * len(cast_chunked),
        scratch_shapes=[pltpu.VMEM((tm, D_MODEL), BF16), pltpu.VMEM((tm, FF_CHUNK), BF16),
                        pltpu.VMEM((tm, D_FF - FF_CHUNK), BF16),
                        pltpu.VMEM((n_chunks, D_MODEL, FF_CHUNK), BF16), pltpu.VMEM((n_chunks, D_MODEL, FF_CHUNK), BF16),
                        pltpu.VMEM((D_FF, D_MODEL), BF16)] + (stage if staged else [])
        + [pltpu.SemaphoreType.DMA((n_slots, 3))],
        compiler_params=pltpu.CompilerParams(dimension_semantics=("arbitrary",), vmem_limit_bytes=VMEM_LIMIT_BYTES),
        name="ffn_final" if gf is not None else "ffn_half",
    )(*args, *cast, *cast_chunked)
    return outs[0], outs[1], list(outs[2:])


def _first_half(shape):
    return (lax.broadcasted_iota(jnp.int32, shape, 1) & (HEAD_DIM - 1)) < HEAD_DIM // 2


def _sign_sin(sin):
    return jnp.where(_first_half(sin.shape), -sin, sin)


def _rope(xg, cos, sin_signed):
    swapped = jnp.where(_first_half(xg.shape), pltpu.roll(xg, LANES - HEAD_DIM // 2, 1),
                        pltpu.roll(xg, HEAD_DIM // 2, 1))
    return xg * cos + swapped * sin_signed


def _in_proj(x, gmix, w_in, cos, sin_signed, gvn_gain):
    h = _rms(x, gmix).astype(BF16)
    z = jnp.dot(h, w_in, preferred_element_type=F32)
    q = jnp.concatenate(
        [(_rope(z[:, LANES * i:LANES * (i + 1)], cos, sin_signed) * Q_SCALE).astype(BF16) for i in range(Q_W // LANES)],
        axis=1)
    k = _rope(z[:, K_OFF:V_OFF], cos, sin_signed)
    v = z[:, V_OFF:U_OFF]
    u = _gelu(z[:, U_OFF:GV_OFF])
    gvn = _rms(_gelu(z[:, GV_OFF:]), gvn_gain)
    return q, k, v, u, gvn


def _out_proj(x, ya, yg, ga, gg, w_out):
    cat = jnp.concatenate([_rms(ya, ga).astype(BF16), _rms(yg, gg).astype(BF16)], axis=1)
    return x + jnp.dot(cat, w_out, preferred_element_type=F32)


def _softmax_sink(s, sink):
    m = jnp.maximum(jnp.max(s, axis=1, keepdims=True), sink)
    p = jnp.exp2(s - m)
    return p, jnp.sum(p, axis=1, keepdims=True) + jnp.exp2(sink - m)


def _gelu(x):
    k = -2.0 * (2.0 / jnp.pi) ** 0.5 * LOG2E
    return x / (1.0 + jnp.exp2(x * (k + (0.044715 * k) * (x * x))))


def _mix_prompt_kernel(xa_ref, xc_ref, coff_ref, soff_ref, cbase_ref, sbase_ref, gmix_ref, win_ref, sinks_ref, gvg_ref,
                       wcat_ref, bias_ref, ga_ref, gg_ref, wout_ref, o_ref, ko_ref, vo_ref, gvo_ref, q_s, k_s, v_s, u_s,
                       gv_s, cat_s, *, tm, tiles_per_seq, n_tiles):
    s = pl.program_id(0)

    @pl.when(s == 0)
    def _():
        k_s[1, tm:tm + WINDOW, :] = jnp.zeros((WINDOW, KV_W), BF16)
        v_s[1, tm:tm + WINDOW, :] = jnp.zeros((WINDOW, KV_W), BF16)
        cat_s[1] = jnp.zeros((tm, D_MODEL), BF16)

    step = functools.partial(
        _mix_prompt_step, s, xa_ref, xc_ref, coff_ref, soff_ref, cbase_ref, sbase_ref, gmix_ref, win_ref, sinks_ref,
        gvg_ref, wcat_ref, bias_ref, ga_ref, gg_ref, wout_ref, o_ref, ko_ref, vo_ref, gvo_ref, q_s, k_s, v_s, u_s, gv_s,
        cat_s, tm=tm, tiles_per_seq=tiles_per_seq, n_tiles=n_tiles)
    last = n_tiles + 1
    pl.when(s == 0)(functools.partial(step, cur=0, stages="proj"))
    pl.when(s == last)(functools.partial(step, cur=last % 2, stages="out"))
    for parity in range(2):
        pl.when((s > 0) & (s < last) & (s % 2 == parity))(functools.partial(step, cur=parity, stages="all"))


def _mix_prompt_step(s, xa_ref, xc_ref, coff_ref, soff_ref, cbase_ref, sbase_ref, gmix_ref, win_ref, sinks_ref, gvg_ref,
                     wcat_ref, bias_ref, ga_ref, gg_ref, wout_ref, o_ref, ko_ref, vo_ref, gvo_ref, q_s, k_s, v_s, u_s, gv_s,
                     cat_s, *, tm, tiles_per_seq, n_tiles, cur, stages):
    oth = 1 - cur

    h = _rms(xa_ref[0], gmix_ref[...]).astype(BF16)
    tile_in_seq = jnp.minimum(s, n_tiles - 1) % tiles_per_seq
    cb = cbase_ref[pl.ds(tile_in_seq, 1), :]
    sb = sbase_ref[pl.ds(tile_in_seq, 1), :]
    cos = cb * coff_ref[...] - sb * soff_ref[...]
    sin = _sign_sin(sb * coff_ref[...] + cb * soff_ref[...])

    def proj_q():
        z = jnp.dot(h, win_ref[:, 0:Q_W], preferred_element_type=F32)
        for i in range(Q_W // LANES):
            q_s[cur, :, LANES * i:LANES * (i + 1)] = (
                _rope(z[:, LANES * i:LANES * (i + 1)], cos, sin) * Q_SCALE).astype(BF16)

    def proj_kv():
        z = jnp.dot(h, win_ref[:, K_OFF:U_OFF], preferred_element_type=F32)
        k = _rope(z[:, 0:KV_W], cos, sin)
        v = z[:, KV_W:]
        k_s[cur, 0:WINDOW, :] = k_s[oth, tm:tm + WINDOW, :]
        v_s[cur, 0:WINDOW, :] = v_s[oth, tm:tm + WINDOW, :]
        k_s[cur, WINDOW:, :] = k.astype(BF16)
        v_s[cur, WINDOW:, :] = v.astype(BF16)
        ko_ref[0] = k[tm - WINDOW:].T
        vo_ref[0] = v[tm - WINDOW:].T

    def proj_u():
        u_s[cur] = _gelu(jnp.dot(h, win_ref[:, U_OFF:GV_OFF], preferred_element_type=F32))

    def proj_gv():
        gvn = _rms(_gelu(jnp.dot(h, win_ref[:, GV_OFF:], preferred_element_type=F32)), gvg_ref[...])
        gv_s[cur] = gvn.astype(BF16)
        gvo_ref[0] = gvn[tm - CHUNK:].T

    def out_half(c):
        cols = slice(c * (D_MODEL // 2), (c + 1) * (D_MODEL // 2))

        def run():
            o_ref[0, :, cols] = xc_ref[0, :, cols] + jnp.dot(cat_s[cur], wout_ref[:, cols], preferred_element_type=F32)
        return run

    mid_fill = [[out_half(0)], [proj_q], [proj_kv, proj_u], [proj_gv]]
    end_fill = [[], [], [], [out_half(1)]]
    assert len(mid_fill) == len(end_fill) == tm // WINDOW
    if stages == "proj":
        for run in (proj_q, proj_kv, proj_u, proj_gv):
            run()
        return
    if stages == "out":
        out_half(0)()
        out_half(1)()
        return

    wrow = lax.broadcasted_iota(jnp.int32, (CHUNK, CHUNK), 0)
    wcol = lax.broadcasted_iota(jnp.int32, (CHUNK, CHUNK), 1)
    wtril = [jnp.where(wcol <= wrow, wcat_ref[hd], 0.0).astype(BF16) for hd in range(G_HEADS)]
    wmix = [jnp.concatenate(wtril[2 * p:2 * p + 2], axis=1) for p in range(G_HEADS // 2)]
    lane = lax.broadcasted_iota(jnp.int32, (CHUNK, LANES), 1)
    low_head = lane < HEAD_DIM

    qi = lax.broadcasted_iota(jnp.int32, (WINDOW, 2 * WINDOW), 0)
    sj = lax.broadcasted_iota(jnp.int32, (WINDOW, 2 * WINDOW), 1)
    dist = WINDOW + qi - sj
    band = (dist >= 0) & (dist < WINDOW)
    first_lo = jnp.where((s + tiles_per_seq - 1) % tiles_per_seq == 0, WINDOW, 0)

    for j in range(tm // WINDOW):
        rows = slice(j * WINDOW, (j + 1) * WINDOW)
        mask = band & (sj >= first_lo) if j == 0 else band
        qb = q_s[oth, rows, :]
        scores = []
        for kv in range(N_KV):
            qs = jnp.concatenate(
                [qb[:, HEAD_DIM * hd:HEAD_DIM * (hd + 1)] for hd in range(kv * GQA, (kv + 1) * GQA)], axis=0)
            kb = k_s[oth, j * WINDOW:(j + 2) * WINDOW, HEAD_DIM * kv:HEAD_DIM * (kv + 1)]
            scores.append(lax.dot_general(qs, kb, _NT, preferred_element_type=F32))
        for run in mid_fill[j]:
            run()
        outs, dens = [], []
        for kv in range(N_KV):
            vb = v_s[oth, j * WINDOW:(j + 2) * WINDOW, HEAD_DIM * kv:HEAD_DIM * (kv + 1)]
            ps = []
            for g in range(GQA):
                sg = jnp.where(mask, scores[kv][g * WINDOW:(g + 1) * WINDOW], -jnp.inf)
                p, den = _softmax_sink(sg, sinks_ref[kv * GQA + g] * LOG2E)
                ps.append(p.astype(BF16))
                dens.append(den)
            outs.append(jnp.dot(jnp.concatenate(ps, axis=0), vb, preferred_element_type=F32))
        mixed = []
        for p in range(G_HEADS // 2):
            r = gv_s[oth, rows, LANES * p:LANES * (p + 1)]
            zero = jnp.zeros_like(r)
            rhs = jnp.concatenate([jnp.where(low_head, r, zero), jnp.where(low_head, zero, r)], axis=0)
            mixed.append(jnp.dot(wmix[p], rhs, preferred_element_type=F32))
        for run in end_fill[j]:
            run()
        ya = [outs[hd // GQA][(hd % GQA) * WINDOW:(hd % GQA + 1) * WINDOW] / dens[hd] for hd in range(N_HEADS)]
        cat_s[oth, rows, 0:Q_W] = _rms(jnp.concatenate(ya, axis=1), ga_ref[...]).astype(BF16)
        yg = u_s[oth, rows, :] * (jnp.concatenate(mixed, axis=1) + bias_ref[...])
        cat_s[oth, rows, Q_W:] = _rms(yg, gg_ref[...]).astype(BF16)


def _mix_prompt(x, gmix, w_in, sinks, gvg, wcat, bias_full, ga, gg, w_out):
    b, s, _ = x.shape
    tm = TOKEN_TILE
    assert s % tm == 0 and tm % WINDOW == 0
    tiles_per_seq = s // tm
    n_tiles = b * tiles_per_seq
    cos_off, sin_off = _rope_tables(jnp.arange(tm, dtype=jnp.int32))
    cos_base, sin_base = _rope_tables(tm * jnp.arange(tiles_per_seq, dtype=jnp.int32))
    proj_tile = lambda i: jnp.minimum(i, n_tiles - 1)
    out_tile = lambda i: jnp.maximum(i - 2, 0)
    x_tiles = x.reshape(n_tiles, tm, D_MODEL)
    last = lambda width: pl.BlockSpec((1, width, WINDOW), lambda i: (proj_tile(i) // tiles_per_seq, 0, 0))
    out, ko, vo, gvo = pl.pallas_call(
        functools.partial(_mix_prompt_kernel, tm=tm, tiles_per_seq=tiles_per_seq, n_tiles=n_tiles),
        grid=(n_tiles + 2,),
        in_specs=[pl.BlockSpec((1, tm, D_MODEL), lambda i: (proj_tile(i), 0, 0)),
                  pl.BlockSpec((1, tm, D_MODEL), lambda i: (out_tile(i), 0, 0)),
                  _resident((tm, LANES)), _resident((tm, LANES)), _resident((tiles_per_seq, LANES)),
                  _resident((tiles_per_seq, LANES)), _resident((1, D_MODEL)), _resident((D_MODEL, D_IN)),
                  pl.BlockSpec(memory_space=pltpu.SMEM), _resident((1, D_GMLP)),
                  _resident((G_HEADS, CHUNK, CHUNK)), _resident((CHUNK, D_GMLP)), _resident((1, Q_W)),
                  _resident((1, D_GMLP)), _resident((D_MODEL, D_MODEL))],
        out_specs=[pl.BlockSpec((1, tm, D_MODEL), lambda i: (out_tile(i), 0, 0)), last(KV_W), last(KV_W),
                   last(D_GMLP)],
        out_shape=[jax.ShapeDtypeStruct((n_tiles, tm, D_MODEL), F32), jax.ShapeDtypeStruct((b, KV_W, WINDOW), F32),
                   jax.ShapeDtypeStruct((b, KV_W, WINDOW), F32), jax.ShapeDtypeStruct((b, D_GMLP, CHUNK), F32)],
        scratch_shapes=[pltpu.VMEM((2, tm, Q_W), BF16), pltpu.VMEM((2, tm + WINDOW, KV_W), BF16),
                        pltpu.VMEM((2, tm + WINDOW, KV_W), BF16), pltpu.VMEM((2, tm, D_GMLP), F32),
                        pltpu.VMEM((2, tm, D_GMLP), BF16), pltpu.VMEM((2, tm, D_MODEL), BF16)],
        compiler_params=pltpu.CompilerParams(dimension_semantics=("arbitrary",), vmem_limit_bytes=VMEM_LIMIT_BYTES),
        name="mix_prompt",
    )(x_tiles, x_tiles, cos_off, sin_off, cos_base, sin_base, gmix, w_in, sinks, gvg, wcat, bias_full, ga, gg, w_out)
    return out.reshape(b, s, D_MODEL), ko, vo, gvo


def _mix_sample_kernel(x_ref, ckt_ref, cvt_ref, cos_ref, sin_ref, gmix_ref, win_ref, sinks_ref, gvg_ref, coef_ref,
                       biasr_ref, ga_ref, gg_ref, wout_ref, o_ref, kot_ref, vot_ref, gvo_ref, q_s, k_s, v_s, ya_s,
                       yg_s, *, t_new, w_buf):
    step = pl.program_id(0)
    n_tok = x_ref.shape[0]
    step_seqs = SEQ_GROUP * GROUPS_PER_STEP
    step_rows = step_seqs * t_new
    grp_rows = SEQ_GROUP * t_new
    grp_keys = SEQ_GROUP * w_buf
    sub = 8

    @pl.when(step == 0)
    def _():
        tile = lambda tab: jnp.broadcast_to(tab[None], (n_tok // sub, sub, LANES)).reshape(n_tok, LANES)
        q, k, v, u, gvn = _in_proj(x_ref[...], gmix_ref[...], win_ref[...], tile(cos_ref[...]),
                                   tile(_sign_sin(sin_ref[...])), gvg_ref[...])
        q_s[...] = q
        k_s[...] = k
        v_s[...] = v
        gv_seq = gvn.reshape(n_tok // t_new, t_new, D_GMLP)
        for tok in range(t_new):
            gvo_ref[tok] = gv_seq[:, tok, :].T
        g3 = gvn.reshape(n_tok // sub, sub, D_GMLP)
        trow = lax.broadcasted_iota(jnp.int32, (1, sub, D_GMLP), 1) & (t_new - 1)
        mixed = biasr_ref[...][None] + coef_ref[0][None] * g3
        for d in range(1, t_new):
            shifted = jnp.where(trow >= d, pltpu.roll(g3, d, 1), 0.0)
            mixed = mixed + coef_ref[d][None] * shifted
        yg_s[...] = (u.reshape(n_tok // sub, sub, D_GMLP) * mixed).reshape(n_tok, D_GMLP)

    row0 = pl.multiple_of(step * step_rows, step_rows)
    q_step = q_s[pl.ds(row0, step_rows), :]
    kn = k_s[pl.ds(row0, step_rows), :]
    vn = v_s[pl.ds(row0, step_rows), :]

    knt = kn.T
    vnt = vn.T
    tail = lax.broadcasted_iota(jnp.int32, (KV_W, w_buf), 1) >= w_buf - t_new
    for b in range(step_seqs):
        shift = (w_buf - t_new - t_new * b) % w_buf
        kot_ref[b] = jnp.where(tail, pltpu.roll(knt, shift, 1) if shift else knt,
                               pltpu.roll(ckt_ref[b], w_buf - t_new, 1))
        vot_ref[b] = jnp.where(tail, pltpu.roll(vnt, shift, 1) if shift else vnt,
                               pltpu.roll(cvt_ref[b], w_buf - t_new, 1))

    knb = kn.astype(BF16)
    vnb = vn.astype(BF16)

    n_rows = GQA * grp_rows
    shift_t = t_new.bit_length() - 1
    shift_w = w_buf.bit_length() - 1
    r = lax.broadcasted_iota(jnp.int32, (n_rows, grp_keys), 0)
    c = lax.broadcasted_iota(jnp.int32, (n_rows, grp_keys), 1)
    mask_c = ((c >> shift_w) == ((r & (grp_rows - 1)) >> shift_t)) & ((c & (w_buf - 1)) > (r & (t_new - 1)))
    r2 = lax.broadcasted_iota(jnp.int32, (n_rows, grp_rows), 0)
    c2 = lax.broadcasted_iota(jnp.int32, (n_rows, grp_rows), 1)
    mask_n = ((c2 >> shift_t) == ((r2 & (grp_rows - 1)) >> shift_t)) & ((c2 & (t_new - 1)) <= (r2 & (t_new - 1)))
    row_head = lax.broadcasted_iota(jnp.int32, (n_rows, 1), 0) >> (grp_rows.bit_length() - 1)

    for grp in range(GROUPS_PER_STEP):
        rows = slice(grp * grp_rows, (grp + 1) * grp_rows)
        seqs = range(grp * SEQ_GROUP, (grp + 1) * SEQ_GROUP)
        for kv in range(N_KV):
            heads = [kv * GQA + i for i in range(GQA)]
            lanes = slice(HEAD_DIM * kv, HEAD_DIM * (kv + 1))
            kt = jnp.concatenate([ckt_ref[b, lanes, :] for b in seqs], axis=1).astype(BF16)
            vt = jnp.concatenate([cvt_ref[b, lanes, :] for b in seqs], axis=1).astype(BF16)
            qs = jnp.concatenate([q_step[rows, HEAD_DIM * hd:HEAD_DIM * (hd + 1)] for hd in heads], axis=0)
            s_c = jnp.where(mask_c, jnp.dot(qs, kt, preferred_element_type=F32), -jnp.inf)
            s_n = jnp.where(mask_n, lax.dot_general(qs, knb[rows, lanes], _NT, preferred_element_type=F32), -jnp.inf)
            sink = jnp.full((n_rows, 1), sinks_ref[heads[0]] * LOG2E, F32)
            for i in range(1, GQA):
                sink = jnp.where(row_head == i, sinks_ref[heads[i]] * LOG2E, sink)
            m = jnp.maximum(jnp.maximum(jnp.max(s_c, axis=1, keepdims=True), jnp.max(s_n, axis=1, keepdims=True)),
                            sink)
            p_c = jnp.exp2(s_c - m)
            p_n = jnp.exp2(s_n - m)
            den = jnp.sum(p_c, axis=1, keepdims=True) + jnp.sum(p_n, axis=1, keepdims=True) + jnp.exp2(sink - m)
            o = (lax.dot_general(p_c.astype(BF16), vt, _NT, preferred_element_type=F32)
                 + jnp.dot(p_n.astype(BF16), vnb[rows, lanes], preferred_element_type=F32)) / den
            for i, hd in enumerate(heads):
                ya_s[pl.ds(row0 + grp * grp_rows, grp_rows), HEAD_DIM * hd:HEAD_DIM * (hd + 1)] = (
                    o[i * grp_rows:(i + 1) * grp_rows])

    @pl.when(step == pl.num_programs(0) - 1)
    def _():
        o_ref[...] = _out_proj(x_ref[...], ya_s[...], yg_s[...], ga_ref[...], gg_ref[...], wout_ref[...])


def _mix_sample(x, cache_kt, cache_vt, cos, sin, gmix, w_in, sinks, gvg, coef, bias_rows, ga, gg, w_out, t_new):
    n_tok = x.shape[0]
    n_seq, _, w_buf = cache_kt.shape
    step_seqs = SEQ_GROUP * GROUPS_PER_STEP
    assert n_seq % step_seqs == 0 and n_tok == n_seq * t_new
    assert t_new & (t_new - 1) == 0 and w_buf & (w_buf - 1) == 0 and 8 % t_new == 0
    assert step_seqs * t_new == w_buf == LANES
    cache_spec = pl.BlockSpec((step_seqs, KV_W, w_buf), lambda i: (i, 0, 0))
    return pl.pallas_call(
        functools.partial(_mix_sample_kernel, t_new=t_new, w_buf=w_buf),
        grid=(n_seq // step_seqs,),
        in_specs=[_resident((n_tok, D_MODEL)), cache_spec, cache_spec, _resident((8, LANES)),
                  _resident((8, LANES)), _resident((1, D_MODEL)), _resident((D_MODEL, D_IN)),
                  pl.BlockSpec(memory_space=pltpu.SMEM), _resident((1, D_GMLP)), _resident((t_new, 8, D_GMLP)),
                  _resident((8, D_GMLP)), _resident((1, Q_W)), _resident((1, D_GMLP)),
                  _resident((D_MODEL, D_MODEL))],
        out_specs=[pl.BlockSpec((n_tok, D_MODEL), lambda i: (0, 0)), cache_spec, cache_spec,
                   pl.BlockSpec((t_new, D_GMLP, n_seq), lambda i: (0, 0, 0))],
        out_shape=[jax.ShapeDtypeStruct((n_tok, D_MODEL), F32), jax.ShapeDtypeStruct(cache_kt.shape, F32),
                   jax.ShapeDtypeStruct(cache_vt.shape, F32), jax.ShapeDtypeStruct((t_new, D_GMLP, n_seq), F32)],
        scratch_shapes=[pltpu.VMEM((n_tok, Q_W), BF16), pltpu.VMEM((n_tok, KV_W), F32), pltpu.VMEM((n_tok, KV_W), F32),
                        pltpu.VMEM((n_tok, Q_W), F32), pltpu.VMEM((n_tok, D_GMLP), F32)],
        compiler_params=pltpu.CompilerParams(dimension_semantics=("arbitrary",), vmem_limit_bytes=VMEM_LIMIT_BYTES),
        name="mix_sample",
    )(x, cache_kt, cache_vt, cos, sin, gmix, w_in, sinks, gvg, coef, bias_rows, ga, gg, w_out)


def _gmlp_tables_kernel(ws_ref, bs_ref, bias_ref, coef_ref, biasr_ref, *, t_new):
    sub = 8
    expand = lambda col: jnp.broadcast_to(col, (col.shape[0], D_GMLP // G_HEADS))
    row = lax.broadcasted_iota(jnp.int32, (CHUNK, CHUNK), 0)
    lane = lax.broadcasted_iota(jnp.int32, (CHUNK, CHUNK), 1)
    bias = jnp.concatenate(
        [expand(jnp.sum(jnp.where(lane == row, bs_ref[hd:hd + 1, :], 0.0), axis=1, keepdims=True))
         for hd in range(G_HEADS)], axis=1)
    bias_ref[...] = bias
    biasr_ref[...] = jnp.concatenate([bias[0:t_new]] * (sub // t_new), axis=0)
    t = lax.broadcasted_iota(jnp.int32, (sub, CHUNK), 0) & (t_new - 1)
    s = lax.broadcasted_iota(jnp.int32, (sub, CHUNK), 1)
    for d in range(t_new):
        cols = []
        for hd in range(G_HEADS):
            w_rows = jnp.concatenate([ws_ref[hd, 0:t_new, :]] * (sub // t_new), axis=0)
            cols.append(expand(jnp.sum(jnp.where(s == t - d, w_rows, 0.0), axis=1, keepdims=True)))
        coef_ref[d] = jnp.concatenate(cols, axis=1)


def _gmlp_tables(ws, bs, t_new):
    assert 8 % t_new == 0
    return pl.pallas_call(
        functools.partial(_gmlp_tables_kernel, t_new=t_new),
        out_shape=[jax.ShapeDtypeStruct((CHUNK, D_GMLP), F32), jax.ShapeDtypeStruct((t_new, 8, D_GMLP), F32),
                   jax.ShapeDtypeStruct((8, D_GMLP), F32)],
        name="gmlp_tables",
    )(ws, bs)


def _rope_tables(pos):
    inv_freq = ROPE_THETA ** (-jnp.arange(0, HEAD_DIM, 2, dtype=F32) / HEAD_DIM)
    ang = pos.astype(F32)[:, None] * jnp.tile(inv_freq, 2 * LANES // HEAD_DIM)[None, :]
    return jnp.cos(ang), jnp.sin(ang)


def kernel(x_prompt, x_sample, cache_k_win, cache_v_win, norm_ffn1, ffn1_gate, ffn1_up, ffn1_down, norm_mix, w_in,
           attn_sinks, gmlp_v_norm, gmlp_w_s, gmlp_b_s, norm_attn_out, norm_gmlp_out, w_out, norm_ffn2, ffn2_gate,
           ffn2_up, ffn2_down, norm_final):
    depth = norm_ffn1.shape[0]
    b, s, _ = x_prompt.shape
    bd, t_new, _ = x_sample.shape
    w_buf = cache_k_win.shape[2]

    cos_s, sin_s = _rope_tables(PAST_LEN + jnp.arange(8, dtype=jnp.int32) % t_new)

    hp = x_prompt.reshape(b * s, D_MODEL)
    hs = x_sample
    outs = [[] for _ in range(6)]
    ffn1_w = [ffn1_gate[0], ffn1_up[0], ffn1_down[0]]
    for l in range(depth):
        last = l == depth - 1
        row = lambda a: a[l].reshape(1, -1)

        wcat = gmlp_w_s[l]
        bias_full, coef, bias_rows = _gmlp_tables(gmlp_w_s[l], gmlp_b_s[l], t_new)

        hp, hs, (w_in_b, w_out_b, wd2, wg2, wu2) = _ffn_half(
            hp, hs, norm_ffn1[l], *ffn1_w, cast=(w_in[l], w_out[l], ffn2_down[l]), cast_chunked=(ffn2_gate[l], ffn2_up[l]))

        hp, kpt, vpt, gvpt = _mix_prompt(hp.reshape(b, s, D_MODEL), row(norm_mix), w_in_b, attn_sinks[l],
                                         row(gmlp_v_norm), wcat, bias_full, row(norm_attn_out), row(norm_gmlp_out),
                                         w_out_b)
        to_t = lambda c: c.transpose(0, 2, 3, 1).reshape(bd, KV_W, w_buf)
        hs, kst, vst, gvs = _mix_sample(hs, to_t(cache_k_win[l]), to_t(cache_v_win[l]), cos_s, sin_s, row(norm_mix),
                                        w_in_b, attn_sinks[l], row(gmlp_v_norm), coef, bias_rows, row(norm_attn_out),
                                        row(norm_gmlp_out), w_out_b, t_new)
        from_t = lambda c: c.reshape(bd, N_KV, HEAD_DIM, w_buf).transpose(0, 3, 1, 2)

        hp, hs, next_w = _ffn_half(hp.reshape(b * s, D_MODEL), hs, norm_ffn2[l], wg2, wu2, wd2,
                                   gf=norm_final if last else None, cast=() if last else (ffn1_down[l + 1],),
                                   cast_chunked=() if last else (ffn1_gate[l + 1], ffn1_up[l + 1]),
                                   sample_out_shape=(bd, t_new, D_MODEL) if last else None)
        ffn1_w = next_w[1:] + next_w[:1]

        outs[0].append(kpt.reshape(b, N_KV, HEAD_DIM, WINDOW).transpose(0, 3, 1, 2))
        outs[1].append(vpt.reshape(b, N_KV, HEAD_DIM, WINDOW).transpose(0, 3, 1, 2))
        outs[2].append(from_t(kst))
        outs[3].append(from_t(vst))
        outs[4].append(gvpt.reshape(b, G_HEADS, D_GMLP // G_HEADS, CHUNK).transpose(0, 3, 1, 2))
        outs[5].append(gvs.reshape(t_new, G_HEADS, D_GMLP // G_HEADS, bd).transpose(3, 0, 1, 2))

    return (hp.reshape(b, s, D_MODEL), hs) + tuple(jnp.stack(o) for o in outs)
```

```python
import functools

import jax
import jax.numpy as jnp
from jax import lax
from jax.experimental import pallas as pl
from jax.experimental.pallas import tpu as pltpu

F32 = jnp.float32
BF16 = jnp.bfloat16

D_MODEL = 1024
D_FF = 2816
HEAD_DIM = 64
N_HEADS = 8
N_KV = 2
GQA = N_HEADS // N_KV
WINDOW = 128
CHUNK = 128
G_HEADS = 8
Q_W = N_HEADS * HEAD_DIM
KV_W = N_KV * HEAD_DIM
D_GMLP = 512
D_IN = Q_W + 2 * KV_W + 2 * D_GMLP
K_OFF = Q_W
V_OFF = K_OFF + KV_W
U_OFF = V_OFF + KV_W
GV_OFF = U_OFF + D_GMLP
ROPE_THETA = 10000.0
PAST_LEN = 16384
EPS = 1e-6
LOG2E = 1.4426950408889634
Q_SCALE = HEAD_DIM ** -0.5 * LOG2E
LANES = 128
BF16_SUBLANES = 16

TOKEN_TILE = 512
FF_CHUNK = 256
STAGE_SLOTS = 4
SEQ_GROUP = 8
GROUPS_PER_STEP = 4
VMEM_LIMIT_BYTES = 60 * 1024 * 1024

_NT = (((1,), (1,)), ((), ()))


def _rms(x, g):
    ms = jnp.mean(x * x, axis=-1, keepdims=True)
    return (x * lax.rsqrt(ms + EPS)) * g


def _resident(shape):
    zeros = (0,) * len(shape)
    return pl.BlockSpec(shape, lambda *_: zeros, pipeline_mode=pl.Buffered(1))


def _silu_gate(h, wg, wu):
    gate = jnp.dot(h, wg, preferred_element_type=F32)
    up = jnp.dot(h, wu, preferred_element_type=F32)
    return (gate * jax.nn.sigmoid(gate) * up).astype(BF16)


def _ffn_kernel(*refs, final_norm, n_cast, n_cast_chunked, staged):
    refs = iter(refs)
    xp_ref, xn_ref, xs_ref, g_ref, wg_hbm, wu_hbm, wd_hbm = (next(refs) for _ in range(7))
    gf_ref = next(refs) if final_norm else None
    cast_in = [next(refs) for _ in range(n_cast)]
    chunked_in = [next(refs) for _ in range(n_cast_chunked)]
    yp_ref, ys_ref = next(refs), next(refs)
    cast_out = [next(refs) for _ in range(n_cast)]
    chunked_out = [next(refs) for _ in range(n_cast_chunked)]
    h_ref, act0_ref, act_ref, wg_ref, wu_ref, wd_ref = (next(refs) for _ in range(6))
    col_stage, row_stage = (next(refs), next(refs)) if staged else (None, None)
    sem = next(refs)
    n_chunks = D_FF // FF_CHUNK
    n_slots = sem.shape[0]
    gate_up = lambda h, c: _silu_gate(h, wg_ref[c], wu_ref[c])
    finish = lambda y: _rms(y, gf_ref[...]) if final_norm else y

    @pl.when(pl.program_id(0) == 0)
    def _():
        def fetch(c, slot):
            span = pl.ds(pl.multiple_of(c * FF_CHUNK, FF_CHUNK), FF_CHUNK)
            src = (wg_hbm.at[:, span], wu_hbm.at[:, span]) if staged else (wg_hbm.at[c], wu_hbm.at[c])
            dst = ((col_stage.at[slot, 0], col_stage.at[slot, 1], row_stage.at[slot]) if staged else
                   (wg_ref.at[c], wu_ref.at[c], wd_ref.at[span, :]))
            return (pltpu.make_async_copy(src[0], dst[0], sem.at[slot, 0]),
                    pltpu.make_async_copy(src[1], dst[1], sem.at[slot, 1]),
                    pltpu.make_async_copy(wd_hbm.at[span, :], dst[2], sem.at[slot, 2]))

        for c in range(n_slots - 1):
            for copy in fetch(c, c):
                copy.start()
        h_ref[...] = _rms(xs_ref[...].reshape(yp_ref.shape), g_ref[...]).astype(BF16)
        yp_ref[...] = jnp.zeros(yp_ref.shape, F32)

        def body(c, carry):
            slot = c % n_slots
            ahead = c + n_slots - 1

            @pl.when(ahead < n_chunks)
            def _():
                for copy in fetch(ahead, ahead % n_slots):
                    copy.start()

            for copy in fetch(c, slot):
                copy.wait()
            rows = pl.ds(pl.multiple_of(c * FF_CHUNK, FF_CHUNK), FF_CHUNK)
            if staged:
                wg_ref[c] = col_stage[slot, 0].astype(BF16)
                wu_ref[c] = col_stage[slot, 1].astype(BF16)
                wd_ref[rows, :] = row_stage[slot].astype(BF16)
            yp_ref[...] += jnp.dot(gate_up(h_ref[...], c), wd_ref[rows, :], preferred_element_type=F32)
            return carry

        lax.fori_loop(0, n_chunks, body, 0)
        ys_ref[...] = finish(xs_ref[...].reshape(yp_ref.shape) + 0.5 * yp_ref[...]).reshape(ys_ref.shape)
        h0 = _rms(xp_ref[...], g_ref[...]).astype(BF16)
        h_ref[...] = h0
        act0_ref[...] = gate_up(h0, 0)

    for c in range(1, n_chunks):
        act_ref[:, (c - 1) * FF_CHUNK:c * FF_CHUNK] = gate_up(h_ref[...], c)
    down0 = jnp.dot(act0_ref[...], wd_ref[0:FF_CHUNK, :], preferred_element_type=F32)
    hn = _rms(xn_ref[...], g_ref[...]).astype(BF16)
    h_ref[...] = hn
    act0_ref[...] = gate_up(hn, 0)
    yp_ref[...] = finish(
        xp_ref[...] + 0.5 * (down0 + jnp.dot(act_ref[...], wd_ref[FF_CHUNK:, :], preferred_element_type=F32)))

    for src, dst in zip(cast_in, cast_out):
        dst[...] = src[...].astype(BF16)
    for src, dst in zip(chunked_in, chunked_out):
        for c in range(n_chunks):
            dst[c] = src[:, c * FF_CHUNK:(c + 1) * FF_CHUNK].astype(BF16)


def _cast_row_blocks(rows, n_steps):
    return max(d for d in range(1, n_steps + 1) if rows % d == 0 and (rows // d) % BF16_SUBLANES == 0)


def _cast_specs(mats, n_steps):
    specs = []
    for w in mats:
        rows, cols = w.shape
        nb = _cast_row_blocks(rows, n_steps)
        specs.append(pl.BlockSpec((rows // nb, cols), lambda i, nb=nb: (jnp.minimum(i, nb - 1), 0)))
    return specs


def _ffn_half(xp, xs, g, wg, wu, wd, gf=None, cast=(), cast_chunked=(), sample_out_shape=None):
    tm = TOKEN_TILE
    n = xp.shape[0]
    n_chunks = D_FF // FF_CHUNK
    staged = wg.dtype == F32
    assert all(w.dtype == wg.dtype for w in (wu, wd))
    chunk_major = (n_chunks, D_MODEL, FF_CHUNK)
    assert all(w.shape == ((D_MODEL, D_FF) if staged else chunk_major) for w in (wg, wu))
    assert all(w.shape == (D_MODEL, D_FF) for w in cast_chunked)
    assert n % tm == 0 and xs.size == tm * D_MODEL and xs.shape[-1] == D_MODEL
    assert D_FF % FF_CHUNK == 0 and n_chunks >= STAGE_SLOTS
    n_tiles = n // tm
    prompt_spec = pl.BlockSpec((tm, D_MODEL), lambda i: (i, 0))
    next_spec = pl.BlockSpec((tm, D_MODEL), lambda i: (jnp.minimum(i + 1, n_tiles - 1), 0))
    hbm = pl.BlockSpec(memory_space=pl.ANY)
    sample_shape = sample_out_shape or (tm, D_MODEL)
    in_specs = [prompt_spec, next_spec, _resident(xs.shape), _resident((1, D_MODEL)), hbm, hbm, hbm]
    args = [xp, xp, xs, g.reshape(1, D_MODEL), wg, wu, wd]
    if gf is not None:
        in_specs.append(_resident((1, D_MODEL)))
        args.append(gf.reshape(1, D_MODEL))
    cast_specs = _cast_specs(cast, n_tiles)
    chunked_in_specs = _cast_specs(cast_chunked, n_tiles)
    chunked_out_specs = [pl.BlockSpec((n_chunks,) + spec.block_shape[:1] + (FF_CHUNK,), lambda i, m=spec.index_map: (0,) + m(i))
                         for spec in chunked_in_specs]
    n_slots = STAGE_SLOTS if staged else n_chunks
    stage = [pltpu.VMEM((n_slots, 2, D_MODEL, FF_CHUNK), F32), pltpu.VMEM((n_slots, FF_CHUNK, D_MODEL), F32)]
    outs = pl.pallas_call(
        functools.partial(_ffn_kernel, final_norm=gf is not None, n_cast=len(cast), n_cast_chunked=len(cast_chunked),
                          staged=staged),
        grid=(n_tiles,),
        in_specs=in_specs + cast_specs + chunked_in_specs,
        out_specs=[prompt_spec, pl.BlockSpec(sample_shape, lambda i: (0,) * len(sample_shape))] + cast_specs
        + chunked_out_specs,
        out_shape=[jax.ShapeDtypeStruct((n, D_MODEL), F32), jax.ShapeDtypeStruct(sample_shape, F32)]
        + [jax.ShapeDtypeStruct(w.shape, BF16) for w in cast] + [jax.ShapeDtypeStruct(chunk_major, BF16)] * len(cast_chunked),
        scratch_shapes=[pltpu.VMEM((tm, D_MODEL), BF16), pltpu.VMEM((tm, FF_CHUNK), BF16),
                        pltpu.VMEM((tm, D_FF - FF_CHUNK), BF16),
                        pltpu.VMEM((n_chunks, D_MODEL, FF_CHUNK), BF16), pltpu.VMEM((n_chunks, D_MODEL, FF_CHUNK), BF16),
                        pltpu.VMEM((D_FF, D_MODEL), BF16)] + (stage if staged else [])
        + [pltpu.SemaphoreType.DMA((n_slots, 3))],
        compiler_params=pltpu.CompilerParams(dimension_semantics=("arbitrary",), vmem_limit_bytes=VMEM_LIMIT_BYTES),
        name="ffn_final" if gf is not None else "ffn_half",
    )(*args, *cast, *cast_chunked)
    return outs[0], outs[1], list(outs[2:])


def _first_half(shape):
    return (lax.broadcasted_iota(jnp.int32, shape, 1) & (HEAD_DIM - 1)) < HEAD_DIM // 2


def _sign_sin(sin):
    return jnp.where(_first_half(sin.shape), -sin, sin)


def _rope(xg, cos, sin_signed):
    swapped = jnp.where(_first_half(xg.shape), pltpu.roll(xg, LANES - HEAD_DIM // 2, 1),
                        pltpu.roll(xg, HEAD_DIM // 2, 1))
    return xg * cos + swapped * sin_signed


def _in_proj(x, gmix, w_in, cos, sin_signed, gvn_gain):
    h = _rms(x, gmix).astype(BF16)
    z = jnp.dot(h, w_in, preferred_element_type=F32)
    q = jnp.concatenate(
        [(_rope(z[:, LANES * i:LANES * (i + 1)], cos, sin_signed) * Q_SCALE).astype(BF16) for i in range(Q_W // LANES)],
        axis=1)
    k = _rope(z[:, K_OFF:V_OFF], cos, sin_signed)
    v = z[:, V_OFF:U_OFF]
    u = _gelu(z[:, U_OFF:GV_OFF])
    gvn = _rms(_gelu(z[:, GV_OFF:]), gvn_gain)
    return q, k, v, u, gvn


def _out_proj(x, ya, yg, ga, gg, w_out):
    cat = jnp.concatenate([_rms(ya, ga).astype(BF16), _rms(yg, gg).astype(BF16)], axis=1)
    return x + jnp.dot(cat, w_out, preferred_element_type=F32)


def _softmax_sink(s, sink):
    m = jnp.maximum(jnp.max(s, axis=1, keepdims=True), sink)
    p = jnp.exp2(s - m)
    return p, jnp.sum(p, axis=1, keepdims=True) + jnp.exp2(sink - m)


def _gelu(x):
    k = -2.0 * (2.0 / jnp.pi) ** 0.5 * LOG2E
    return x / (1.0 + jnp.exp2(x * (k + (0.044715 * k) * (x * x))))


def _mix_prompt_kernel(xa_ref, xc_ref, coff_ref, soff_ref, cbase_ref, sbase_ref, gmix_ref, win_ref, sinks_ref, gvg_ref,
                       wcat_ref, bias_ref, ga_ref, gg_ref, wout_ref, o_ref, ko_ref, vo_ref, gvo_ref, q_s, k_s, v_s, u_s,
                       gv_s, cat_s, *, tm, tiles_per_seq, n_tiles):
    s = pl.program_id(0)

    @pl.when(s == 0)
    def _():
        k_s[1, tm:tm + WINDOW, :] = jnp.zeros((WINDOW, KV_W), BF16)
        v_s[1, tm:tm + WINDOW, :] = jnp.zeros((WINDOW, KV_W), BF16)
        cat_s[1] = jnp.zeros((tm, D_MODEL), BF16)

    step = functools.partial(
        _mix_prompt_step, s, xa_ref, xc_ref, coff_ref, soff_ref, cbase_ref, sbase_ref, gmix_ref, win_ref, sinks_ref,
        gvg_ref, wcat_ref, bias_ref, ga_ref, gg_ref, wout_ref, o_ref, ko_ref, vo_ref, gvo_ref, q_s, k_s, v_s, u_s, gv_s,
        cat_s, tm=tm, tiles_per_seq=tiles_per_seq, n_tiles=n_tiles)
    last = n_tiles + 1
    pl.when(s == 0)(functools.partial(step, cur=0, stages="proj"))
    pl.when(s == last)(functools.partial(step, cur=last % 2, stages="out"))
    for parity in range(2):
        pl.when((s > 0) & (s < last) & (s % 2 == parity))(functools.partial(step, cur=parity, stages="all"))


def _mix_prompt_step(s, xa_ref, xc_ref, coff_ref, soff_ref, cbase_ref, sbase_ref, gmix_ref, win_ref, sinks_ref, gvg_ref,
                     wcat_ref, bias_ref, ga_ref, gg_ref, wout_ref, o_ref, ko_ref, vo_ref, gvo_ref, q_s, k_s, v_s, u_s, gv_s,
                     cat_s, *, tm, tiles_per_seq, n_tiles, cur, stages):
    oth = 1 - cur

    h = _rms(xa_ref[0], gmix_ref[...]).astype(BF16)
    tile_in_seq = jnp.minimum(s, n_tiles - 1) % tiles_per_seq
    cb = cbase_ref[pl.ds(tile_in_seq, 1), :]
    sb = sbase_ref[pl.ds(tile_in_seq, 1), :]
    cos = cb * coff_ref[...] - sb * soff_ref[...]
    sin = _sign_sin(sb * coff_ref[...] + cb * soff_ref[...])

    def proj_q():
        z = jnp.dot(h, win_ref[:, 0:Q_W], preferred_element_type=F32)
        for i in range(Q_W // LANES):
            q_s[cur, :, LANES * i:LANES * (i + 1)] = (
                _rope(z[:, LANES * i:LANES * (i + 1)], cos, sin) * Q_SCALE).astype(BF16)

    def proj_kv():
        z = jnp.dot(h, win_ref[:, K_OFF:U_OFF], preferred_element_type=F32)
        k = _rope(z[:, 0:KV_W], cos, sin)
        v = z[:, KV_W:]
        k_s[cur, 0:WINDOW, :] = k_s[oth, tm:tm + WINDOW, :]
        v_s[cur, 0:WINDOW, :] = v_s[oth, tm:tm + WINDOW, :]
        k_s[cur, WINDOW:, :] = k.astype(BF16)
        v_s[cur, WINDOW:, :] = v.astype(BF16)
        ko_ref[0] = k[tm - WINDOW:].T
        vo_ref[0] = v[tm - WINDOW:].T

    def proj_u():
        u_s[cur] = _gelu(jnp.dot(h, win_ref[:, U_OFF:GV_OFF], preferred_element_type=F32))

    def proj_gv():
        gvn = _rms(_gelu(jnp.dot(h, win_ref[:, GV_OFF:], preferred_element_type=F32)), gvg_ref[...])
        gv_s[cur] = gvn.astype(BF16)
        gvo_ref[0] = gvn[tm - CHUNK:].T

    def out_half(c):
        cols = slice(c * (D_MODEL // 2), (c + 1) * (D_MODEL // 2))

        def run():
            o_ref[0, :, cols] = xc_ref[0, :, cols] + jnp.dot(cat_s[cur], wout_ref[:, cols], preferred_element_type=F32)
        return run

    mid_fill = [[out_half(0)], [proj_q], [proj_kv, proj_u], [proj_gv]]
    end_fill = [[], [], [], [out_half(1)]]
    assert len(mid_fill) == len(end_fill) == tm // WINDOW
    if stages == "proj":
        for run in (proj_q, proj_kv, proj_u, proj_gv):
            run()
        return
    if stages == "out":
        out_half(0)()
        out_half(1)()
        return

    wrow = lax.broadcasted_iota(jnp.int32, (CHUNK, CHUNK), 0)
    wcol = lax.broadcasted_iota(jnp.int32, (CHUNK, CHUNK), 1)
    wtril = [jnp.where(wcol <= wrow, wcat_ref[hd], 0.0).astype(BF16) for hd in range(G_HEADS)]
    wmix = [jnp.concatenate(wtril[2 * p:2 * p + 2], axis=1) for p in range(G_HEADS // 2)]
    lane = lax.broadcasted_iota(jnp.int32, (CHUNK, LANES), 1)
    low_head = lane < HEAD_DIM

    qi = lax.broadcasted_iota(jnp.int32, (WINDOW, 2 * WINDOW), 0)
    sj = lax.broadcasted_iota(jnp.int32, (WINDOW, 2 * WINDOW), 1)
    dist = WINDOW + qi - sj
    band = (dist >= 0) & (dist < WINDOW)
    first_lo = jnp.where((s + tiles_per_seq - 1) % tiles_per_seq == 0, WINDOW, 0)

    for j in range(tm // WINDOW):
        rows = slice(j * WINDOW, (j + 1) * WINDOW)
        mask = band & (sj >= first_lo) if j == 0 else band
        qb = q_s[oth, rows, :]
        scores = []
        for kv in range(N_KV):
            qs = jnp.concatenate(
                [qb[:, HEAD_DIM * hd:HEAD_DIM * (hd + 1)] for hd in range(kv * GQA, (kv + 1) * GQA)], axis=0)
            kb = k_s[oth, j * WINDOW:(j + 2) * WINDOW, HEAD_DIM * kv:HEAD_DIM * (kv + 1)]
            scores.append(lax.dot_general(qs, kb, _NT, preferred_element_type=F32))
        for run in mid_fill[j]:
            run()
        outs, dens = [], []
        for kv in range(N_KV):
            vb = v_s[oth, j * WINDOW:(j + 2) * WINDOW, HEAD_DIM * kv:HEAD_DIM * (kv + 1)]
            ps = []
            for g in range(GQA):
                sg = jnp.where(mask, scores[kv][g * WINDOW:(g + 1) * WINDOW], -jnp.inf)
                p, den = _softmax_sink(sg, sinks_ref[kv * GQA + g] * LOG2E)
                ps.append(p.astype(BF16))
                dens.append(den)
            outs.append(jnp.dot(jnp.concatenate(ps, axis=0), vb, preferred_element_type=F32))
        mixed = []
        for p in range(G_HEADS // 2):
            r = gv_s[oth, rows, LANES * p:LANES * (p + 1)]
            zero = jnp.zeros_like(r)
            rhs = jnp.concatenate([jnp.where(low_head, r, zero), jnp.where(low_head, zero, r)], axis=0)
            mixed.append(jnp.dot(wmix[p], rhs, preferred_element_type=F32))
        for run in end_fill[j]:
            run()
        ya = [outs[hd // GQA][(hd % GQA) * WINDOW:(hd % GQA + 1) * WINDOW] / dens[hd] for hd in range(N_HEADS)]
        cat_s[oth, rows, 0:Q_W] = _rms(jnp.concatenate(ya, axis=1), ga_ref[...]).astype(BF16)
        yg = u_s[oth, rows, :] * (jnp.concatenate(mixed, axis=1) + bias_ref[...])
        cat_s[oth, rows, Q_W:] = _rms(yg, gg_ref[...]).astype(BF16)


def _mix_prompt(x, gmix, w_in, sinks, gvg, wcat, bias_full, ga, gg, w_out):
    b, s, _ = x.shape
    tm = TOKEN_TILE
    assert s % tm == 0 and tm % WINDOW == 0
    tiles_per_seq = s // tm
    n_tiles = b * tiles_per_seq
    cos_off, sin_off = _rope_tables(jnp.arange(tm, dtype=jnp.int32))
    cos_base, sin_base = _rope_tables(tm * jnp.arange(tiles_per_seq, dtype=jnp.int32))
    proj_tile = lambda i: jnp.minimum(i, n_tiles - 1)
    out_tile = lambda i: jnp.maximum(i - 2, 0)
    x_tiles = x.reshape(n_tiles, tm, D_MODEL)
    last = lambda width: pl.BlockSpec((1, width, WINDOW), lambda i: (proj_tile(i) // tiles_per_seq, 0, 0))
    out, ko, vo, gvo = pl.pallas_call(
        functools.partial(_mix_prompt_kernel, tm=tm, tiles_per_seq=tiles_per_seq, n_tiles=n_tiles),
        grid=(n_tiles + 2,),
        in_specs=[pl.BlockSpec((1, tm, D_MODEL), lambda i: (proj_tile(i), 0, 0)),
                  pl.BlockSpec((1, tm, D_MODEL), lambda i: (out_tile(i), 0, 0)),
                  _resident((tm, LANES)), _resident((tm, LANES)), _resident((tiles_per_seq, LANES)),
                  _resident((tiles_per_seq, LANES)), _resident((1, D_MODEL)), _resident((D_MODEL, D_IN)),
                  pl.BlockSpec(memory_space=pltpu.SMEM), _resident((1, D_GMLP)),
                  _resident((G_HEADS, CHUNK, CHUNK)), _resident((CHUNK, D_GMLP)), _resident((1, Q_W)),
                  _resident((1, D_GMLP)), _resident((D_MODEL, D_MODEL))],
        out_specs=[pl.BlockSpec((1, tm, D_MODEL), lambda i: (out_tile(i), 0, 0)), last(KV_W), last(KV_W),
                   last(D_GMLP)],
        out_shape=[jax.ShapeDtypeStruct((n_tiles, tm, D_MODEL), F32), jax.ShapeDtypeStruct((b, KV_W, WINDOW), F32),
                   jax.ShapeDtypeStruct((b, KV_W, WINDOW), F32), jax.ShapeDtypeStruct((b, D_GMLP, CHUNK), F32)],
        scratch_shapes=[pltpu.VMEM((2, tm, Q_W), BF16), pltpu.VMEM((2, tm + WINDOW, KV_W), BF16),
                        pltpu.VMEM((2, tm + WINDOW, KV_W), BF16), pltpu.VMEM((2, tm, D_GMLP), F32),
                        pltpu.VMEM((2, tm, D_GMLP), BF16), pltpu.VMEM((2, tm, D_MODEL), BF16)],
        compiler_params=pltpu.CompilerParams(dimension_semantics=("arbitrary",), vmem_limit_bytes=VMEM_LIMIT_BYTES),
        name="mix_prompt",
    )(x_tiles, x_tiles, cos_off, sin_off, cos_base, sin_base, gmix, w_in, sinks, gvg, wcat, bias_full, ga, gg, w_out)
    return out.reshape(b, s, D_MODEL), ko, vo, gvo


def _mix_sample_kernel(x_ref, ckt_ref, cvt_ref, cos_ref, sin_ref, gmix_ref, win_ref, sinks_ref, gvg_ref, coef_ref,
                       biasr_ref, ga_ref, gg_ref, wout_ref, o_ref, kot_ref, vot_ref, gvo_ref, q_s, k_s, v_s, ya_s,
                       yg_s, *, t_new, w_buf):
    step = pl.program_id(0)
    n_tok = x_ref.shape[0]
    step_seqs = SEQ_GROUP * GROUPS_PER_STEP
    step_rows = step_seqs * t_new
    grp_rows = SEQ_GROUP * t_new
    grp_keys = SEQ_GROUP * w_buf
    sub = 8

    @pl.when(step == 0)
    def _():
        tile = lambda tab: jnp.broadcast_to(tab[None], (n_tok // sub, sub, LANES)).reshape(n_tok, LANES)
        q, k, v, u, gvn = _in_proj(x_ref[...], gmix_ref[...], win_ref[...], tile(cos_ref[...]),
                                   tile(_sign_sin(sin_ref[...])), gvg_ref[...])
        q_s[...] = q
        k_s[...] = k
        v_s[...] = v
        gv_seq = gvn.reshape(n_tok // t_new, t_new, D_GMLP)
        for tok in range(t_new):
            gvo_ref[tok] = gv_seq[:, tok, :].T
        g3 = gvn.reshape(n_tok // sub, sub, D_GMLP)
        trow = lax.broadcasted_iota(jnp.int32, (1, sub, D_GMLP), 1) & (t_new - 1)
        mixed = biasr_ref[...][None] + coef_ref[0][None] * g3
        for d in range(1, t_new):
            shifted = jnp.where(trow >= d, pltpu.roll(g3, d, 1), 0.0)
            mixed = mixed + coef_ref[d][None] * shifted
        yg_s[...] = (u.reshape(n_tok // sub, sub, D_GMLP) * mixed).reshape(n_tok, D_GMLP)

    row0 = pl.multiple_of(step * step_rows, step_rows)
    q_step = q_s[pl.ds(row0, step_rows), :]
    kn = k_s[pl.ds(row0, step_rows), :]
    vn = v_s[pl.ds(row0, step_rows), :]

    knt = kn.T
    vnt = vn.T
    tail = lax.broadcasted_iota(jnp.int32, (KV_W, w_buf), 1) >= w_buf - t_new
    for b in range(step_seqs):
        shift = (w_buf - t_new - t_new * b) % w_buf
        kot_ref[b] = jnp.where(tail, pltpu.roll(knt, shift, 1) if shift else knt,
                               pltpu.roll(ckt_ref[b], w_buf - t_new, 1))
        vot_ref[b] = jnp.where(tail, pltpu.roll(vnt, shift, 1) if shift else vnt,
                               pltpu.roll(cvt_ref[b], w_buf - t_new, 1))

    knb = kn.astype(BF16)
    vnb = vn.astype(BF16)

    n_rows = GQA * grp_rows
    shift_t = t_new.bit_length() - 1
    shift_w = w_buf.bit_length() - 1
    r = lax.broadcasted_iota(jnp.int32, (n_rows, grp_keys), 0)
    c = lax.broadcasted_iota(jnp.int32, (n_rows, grp_keys), 1)
    mask_c = ((c >> shift_w) == ((r & (grp_rows - 1)) >> shift_t)) & ((c & (w_buf - 1)) > (r & (t_new - 1)))
    r2 = lax.broadcasted_iota(jnp.int32, (n_rows, grp_rows), 0)
    c2 = lax.broadcasted_iota(jnp.int32, (n_rows, grp_rows), 1)
    mask_n = ((c2 >> shift_t) == ((r2 & (grp_rows - 1)) >> shift_t)) & ((c2 & (t_new - 1)) <= (r2 & (t_new - 1)))
    row_head = lax.broadcasted_iota(jnp.int32, (n_rows, 1), 0) >> (grp_rows.bit_length() - 1)

    for grp in range(GROUPS_PER_STEP):
        rows = slice(grp * grp_rows, (grp + 1) * grp_rows)
        seqs = range(grp * SEQ_GROUP, (grp + 1) * SEQ_GROUP)
        for kv in range(N_KV):
            heads = [kv * GQA + i for i in range(GQA)]
            lanes = slice(HEAD_DIM * kv, HEAD_DIM * (kv + 1))
            kt = jnp.concatenate([ckt_ref[b, lanes, :] for b in seqs], axis=1).astype(BF16)
            vt = jnp.concatenate([cvt_ref[b, lanes, :] for b in seqs], axis=1).astype(BF16)
            qs = jnp.concatenate([q_step[rows, HEAD_DIM * hd:HEAD_DIM * (hd + 1)] for hd in heads], axis=0)
            s_c = jnp.where(mask_c, jnp.dot(qs, kt, preferred_element_type=F32), -jnp.inf)
            s_n = jnp.where(mask_n, lax.dot_general(qs, knb[rows, lanes], _NT, preferred_element_type=F32), -jnp.inf)
            sink = jnp.full((n_rows, 1), sinks_ref[heads[0]] * LOG2E, F32)
            for i in range(1, GQA):
                sink = jnp.where(row_head == i, sinks_ref[heads[i]] * LOG2E, sink)
            m = jnp.maximum(jnp.maximum(jnp.max(s_c, axis=1, keepdims=True), jnp.max(s_n, axis=1, keepdims=True)),
                            sink)
            p_c = jnp.exp2(s_c - m)
            p_n = jnp.exp2(s_n - m)
            den = jnp.sum(p_c, axis=1, keepdims=True) + jnp.sum(p_n, axis=1, keepdims=True) + jnp.exp2(sink - m)
            o = (lax.dot_general(p_c.astype(BF16), vt, _NT, preferred_element_type=F32)
                 + jnp.dot(p_n.astype(BF16), vnb[rows, lanes], preferred_element_type=F32)) / den
            for i, hd in enumerate(heads):
                ya_s[pl.ds(row0 + grp * grp_rows, grp_rows), HEAD_DIM * hd:HEAD_DIM * (hd + 1)] = (
                    o[i * grp_rows:(i + 1) * grp_rows])

    @pl.when(step == pl.num_programs(0) - 1)
    def _():
        o_ref[...] = _out_proj(x_ref[...], ya_s[...], yg_s[...], ga_ref[...], gg_ref[...], wout_ref[...])


def _mix_sample(x, cache_kt, cache_vt, cos, sin, gmix, w_in, sinks, gvg, coef, bias_rows, ga, gg, w_out, t_new):
    n_tok = x.shape[0]
    n_seq, _, w_buf = cache_kt.shape
    step_seqs = SEQ_GROUP * GROUPS_PER_STEP
    assert n_seq % step_seqs == 0 and n_tok == n_seq * t_new
    assert t_new & (t_new - 1) == 0 and w_buf & (w_buf - 1) == 0 and 8 % t_new == 0
    assert step_seqs * t_new == w_buf == LANES
    cache_spec = pl.BlockSpec((step_seqs, KV_W, w_buf), lambda i: (i, 0, 0))
    return pl.pallas_call(
        functools.partial(_mix_sample_kernel, t_new=t_new, w_buf=w_buf),
        grid=(n_seq // step_seqs,),
        in_specs=[_resident((n_tok, D_MODEL)), cache_spec, cache_spec, _resident((8, LANES)),
                  _resident((8, LANES)), _resident((1, D_MODEL)), _resident((D_MODEL, D_IN)),
                  pl.BlockSpec(memory_space=pltpu.SMEM), _resident((1, D_GMLP)), _resident((t_new, 8, D_GMLP)),
                  _resident((8, D_GMLP)), _resident((1, Q_W)), _resident((1, D_GMLP)),
                  _resident((D_MODEL, D_MODEL))],
        out_specs=[pl.BlockSpec((n_tok, D_MODEL), lambda i: (0, 0)), cache_spec, cache_spec,
                   pl.BlockSpec((t_new, D_GMLP, n_seq), lambda i: (0, 0, 0))],
        out_shape=[jax.ShapeDtypeStruct((n_tok, D_MODEL), F32), jax.ShapeDtypeStruct(cache_kt.shape, F32),
                   jax.ShapeDtypeStruct(cache_vt.shape, F32), jax.ShapeDtypeStruct((t_new, D_GMLP, n_seq), F32)],
        scratch_shapes=[pltpu.VMEM((n_tok, Q_W), BF16), pltpu.VMEM((n_tok, KV_W), F32), pltpu.VMEM((n_tok, KV_W), F32),
                        pltpu.VMEM((n_tok, Q_W), F32), pltpu.VMEM((n_tok, D_GMLP), F32)],
        compiler_params=pltpu.CompilerParams(dimension_semantics=("arbitrary",), vmem_limit_bytes=VMEM_LIMIT_BYTES),
        name="mix_sample",
    )(x, cache_kt, cache_vt, cos, sin, gmix, w_in, sinks, gvg, coef, bias_rows, ga, gg, w_out)


def _gmlp_tables_kernel(ws_ref, bs_ref, bias_ref, coef_ref, biasr_ref, *, t_new):
    sub = 8
    expand = lambda col: jnp.broadcast_to(col, (col.shape[0], D_GMLP // G_HEADS))
    row = lax.broadcasted_iota(jnp.int32, (CHUNK, CHUNK), 0)
    lane = lax.broadcasted_iota(jnp.int32, (CHUNK, CHUNK), 1)
    bias = jnp.concatenate(
        [expand(jnp.sum(jnp.where(lane == row, bs_ref[hd:hd + 1, :], 0.0), axis=1, keepdims=True))
         for hd in range(G_HEADS)], axis=1)
    bias_ref[...] = bias
    biasr_ref[...] = jnp.concatenate([bias[0:t_new]] * (sub // t_new), axis=0)
    t = lax.broadcasted_iota(jnp.int32, (sub, CHUNK), 0) & (t_new - 1)
    s = lax.broadcasted_iota(jnp.int32, (sub, CHUNK), 1)
    for d in range(t_new):
        cols = []
        for hd in range(G_HEADS):
            w_rows = jnp.concatenate([ws_ref[hd, 0:t_new, :]] * (sub // t_new), axis=0)
            cols.append(expand(jnp.sum(jnp.where(s == t - d, w_rows, 0.0), axis=1, keepdims=True)))
        coef_ref[d] = jnp.concatenate(cols, axis=1)


def _gmlp_tables(ws, bs, t_new):
    assert 8 % t_new == 0
    return pl.pallas_call(
        functools.partial(_gmlp_tables_kernel, t_new=t_new),
        out_shape=[jax.ShapeDtypeStruct((CHUNK, D_GMLP), F32), jax.ShapeDtypeStruct((t_new, 8, D_GMLP), F32),
                   jax.ShapeDtypeStruct((8, D_GMLP), F32)],
        name="gmlp_tables",
    )(ws, bs)


def _rope_tables(pos):
    inv_freq = ROPE_THETA ** (-jnp.arange(0, HEAD_DIM, 2, dtype=F32) / HEAD_DIM)
    ang = pos.astype(F32)[:, None] * jnp.tile(inv_freq, 2 * LANES // HEAD_DIM)[None, :]
    return jnp.cos(ang), jnp.sin(ang)


def kernel(x_prompt, x_sample, cache_k_win, cache_v_win, norm_ffn1, ffn1_gate, ffn1_up, ffn1_down, norm_mix, w_in,
           attn_sinks, gmlp_v_norm, gmlp_w_s, gmlp_b_s, norm_attn_out, norm_gmlp_out, w_out, norm_ffn2, ffn2_gate,
           ffn2_up, ffn2_down, norm_final):
    depth = norm_ffn1.shape[0]
    b, s, _ = x_prompt.shape
    bd, t_new, _ = x_sample.shape
    w_buf = cache_k_win.shape[2]

    cos_s, sin_s = _rope_tables(PAST_LEN + jnp.arange(8, dtype=jnp.int32) % t_new)

    hp = x_prompt.reshape(b * s, D_MODEL)
    hs = x_sample
    outs = [[] for _ in range(6)]
    ffn1_w = [ffn1_gate[0], ffn1_up[0], ffn1_down[0]]
    for l in range(depth):
        last = l == depth - 1
        row = lambda a: a[l].reshape(1, -1)

        wcat = gmlp_w_s[l]
        bias_full, coef, bias_rows = _gmlp_tables(gmlp_w_s[l], gmlp_b_s[l], t_new)

        hp, hs, (w_in_b, w_out_b, wd2, wg2, wu2) = _ffn_half(
            hp, hs, norm_ffn1[l], *ffn1_w, cast=(w_in[l], w_out[l], ffn2_down[l]), cast_chunked=(ffn2_gate[l], ffn2_up[l]))

        hp, kpt, vpt, gvpt = _mix_prompt(hp.reshape(b, s, D_MODEL), row(norm_mix), w_in_b, attn_sinks[l],
                                         row(gmlp_v_norm), wcat, bias_full, row(norm_attn_out), row(norm_gmlp_out),
                                         w_out_b)
        to_t = lambda c: c.transpose(0, 2, 3, 1).reshape(bd, KV_W, w_buf)
        hs, kst, vst, gvs = _mix_sample(hs, to_t(cache_k_win[l]), to_t(cache_v_win[l]), cos_s, sin_s, row(norm_mix),
                                        w_in_b, attn_sinks[l], row(gmlp_v_norm), coef, bias_rows, row(norm_attn_out),
                                        row(norm_gmlp_out), w_out_b, t_new)
        from_t = lambda c: c.reshape(bd, N_KV, HEAD_DIM, w_buf).transpose(0, 3, 1, 2)

        hp, hs, next_w = _ffn_half(hp.reshape(b * s, D_MODEL), hs, norm_ffn2[l], wg2, wu2, wd2,
                                   gf=norm_final if last else None, cast=() if last else (ffn1_down[l + 1],),
                                   cast_chunked=() if last else (ffn1_gate[l + 1], ffn1_up[l + 1]),
                                   sample_out_shape=(bd, t_new, D_MODEL) if last else None)
        ffn1_w = next_w[1:] + next_w[:1]

        outs[0].append(kpt.reshape(b, N_KV, HEAD_DIM, WINDOW).transpose(0, 3, 1, 2))
        outs[1].append(vpt.reshape(b, N_KV, HEAD_DIM, WINDOW).transpose(0, 3, 1, 2))
        outs[2].append(from_t(kst))
        outs[3].append(from_t(vst))
        outs[4].append(gvpt.reshape(b, G_HEADS, D_GMLP // G_HEADS, CHUNK).transpose(0, 3, 1, 2))
        outs[5].append(gvs.reshape(t_new, G_HEADS, D_GMLP // G_HEADS, bd).transpose(3, 0, 1, 2))

    return (hp.reshape(b, s, D_MODEL), hs) + tuple(jnp.stack(o) for o in outs)
```

```python
import functools

import jax
import jax.numpy as jnp
from jax import lax
from jax.experimental import pallas as pl
from jax.experimental.pallas import tpu as pltpu

F32 = jnp.float32
BF16 = jnp.bfloat16

D_MODEL = 1024
D_FF = 2816
HEAD_DIM = 64
N_HEADS = 8
N_KV = 2
GQA = N_HEADS // N_KV
WINDOW = 128
CHUNK = 128
G_HEADS = 8
Q_W = N_HEADS * HEAD_DIM
KV_W = N_KV * HEAD_DIM
D_GMLP = 512
D_IN = Q_W + 2 * KV_W + 2 * D_GMLP
K_OFF = Q_W
V_OFF = K_OFF + KV_W
U_OFF = V_OFF + KV_W
GV_OFF = U_OFF + D_GMLP
ROPE_THETA = 10000.0
PAST_LEN = 16384
EPS = 1e-6
LOG2E = 1.4426950408889634
Q_SCALE = HEAD_DIM ** -0.5 * LOG2E
LANES = 128
BF16_SUBLANES = 16

TOKEN_TILE = 512
FF_CHUNK = 256
STAGE_SLOTS = 3
SEQ_GROUP = 8
GROUPS_PER_STEP = 4
VMEM_LIMIT_BYTES = 56 * 1024 * 1024

_NT = (((1,), (1,)), ((), ()))


def _rms(x, g):
    ms = jnp.mean(x * x, axis=-1, keepdims=True)
    return (x * lax.rsqrt(ms + EPS)) * g


def _resident(shape):
    zeros = (0,) * len(shape)
    return pl.BlockSpec(shape, lambda *_: zeros, pipeline_mode=pl.Buffered(1))


def _silu_gate(h, wg, wu):
    gate = jnp.dot(h, wg, preferred_element_type=F32)
    up = jnp.dot(h, wu, preferred_element_type=F32)
    return (gate * jax.nn.sigmoid(gate) * up).astype(BF16)


def _ffn_kernel(*refs, final_norm, n_cast, n_cast_chunked, staged):
    refs = iter(refs)
    xp_ref, xn_ref, xs_ref, g_ref, wg_hbm, wu_hbm, wd_hbm = (next(refs) for _ in range(7))
    gf_ref = next(refs) if final_norm else None
    cast_in = [next(refs) for _ in range(n_cast)]
    chunked_in = [next(refs) for _ in range(n_cast_chunked)]
    yp_ref, ys_ref = next(refs), next(refs)
    cast_out = [next(refs) for _ in range(n_cast)]
    chunked_out = [next(refs) for _ in range(n_cast_chunked)]
    h_ref, act0_ref, act_ref, wg_ref, wu_ref, wd_ref = (next(refs) for _ in range(6))
    col_stage, row_stage = (next(refs), next(refs)) if staged else (None, None)
    sem = next(refs)
    n_chunks = D_FF // FF_CHUNK
    n_slots = sem.shape[0]
    gate_up = lambda h, c: _silu_gate(h, wg_ref[c], wu_ref[c])
    finish = lambda y: _rms(y, gf_ref[...]) if final_norm else y

    @pl.when(pl.program_id(0) == 0)
    def _():
        def fetch(c, slot):
            span = pl.ds(pl.multiple_of(c * FF_CHUNK, FF_CHUNK), FF_CHUNK)
            src = (wg_hbm.at[:, span], wu_hbm.at[:, span]) if staged else (wg_hbm.at[c], wu_hbm.at[c])
            dst = ((col_stage.at[slot, 0], col_stage.at[slot, 1], row_stage.at[slot]) if staged else
                   (wg_ref.at[c], wu_ref.at[c], wd_ref.at[span, :]))
            return (pltpu.make_async_copy(src[0], dst[0], sem.at[slot, 0]),
                    pltpu.make_async_copy(src[1], dst[1], sem.at[slot, 1]),
                    pltpu.make_async_copy(wd_hbm.at[span, :], dst[2], sem.at[slot, 2]))

        for c in range(n_slots - 1):
            for k, copy in enumerate(fetch(c, c)):
                copy.start(priority=k % 2)
        h_ref[...] = _rms(xs_ref[...].reshape(yp_ref.shape), g_ref[...]).astype(BF16)
        yp_ref[...] = jnp.zeros(yp_ref.shape, F32)

        def body(c, carry):
            slot = c % n_slots
            ahead = c + n_slots - 1

            @pl.when(ahead < n_chunks)
            def _():
                for k, copy in enumerate(fetch(ahead, ahead % n_slots)):
                    copy.start(priority=k % 2)

            for copy in fetch(c, slot):
                copy.wait()
            rows = pl.ds(pl.multiple_of(c * FF_CHUNK, FF_CHUNK), FF_CHUNK)
            if staged:
                wg_ref[c] = col_stage[slot, 0].astype(BF16)
                wu_ref[c] = col_stage[slot, 1].astype(BF16)
                wd_ref[rows, :] = row_stage[slot].astype(BF16)
            yp_ref[...] += jnp.dot(gate_up(h_ref[...], c), wd_ref[rows, :], preferred_element_type=F32)
            return carry

        lax.fori_loop(0, n_chunks, body, 0)
        ys_ref[...] = finish(xs_ref[...].reshape(yp_ref.shape) + 0.5 * yp_ref[...]).reshape(ys_ref.shape)
        h0 = _rms(xp_ref[...], g_ref[...]).astype(BF16)
        h_ref[...] = h0
        act0_ref[...] = gate_up(h0, 0)

    for c in range(1, n_chunks):
        act_ref[:, (c - 1) * FF_CHUNK:c * FF_CHUNK] = gate_up(h_ref[...], c)
    down0 = jnp.dot(act0_ref[...], wd_ref[0:FF_CHUNK, :], preferred_element_type=F32)
    hn = _rms(xn_ref[...], g_ref[...]).astype(BF16)
    h_ref[...] = hn
    act0_ref[...] = gate_up(hn, 0)
    yp_ref[...] = finish(
        xp_ref[...] + 0.5 * (down0 + jnp.dot(act_ref[...], wd_ref[FF_CHUNK:, :], preferred_element_type=F32)))

    for src, dst in zip(cast_in, cast_out):
        dst[...] = src[...].astype(BF16)
    for src, dst in zip(chunked_in, chunked_out):
        for c in range(n_chunks):
            dst[c] = src[:, c * FF_CHUNK:(c + 1) * FF_CHUNK].astype(BF16)


def _cast_row_blocks(rows, n_steps):
    return max(d for d in range(1, n_steps + 1) if rows % d == 0 and (rows // d) % BF16_SUBLANES == 0)


def _cast_specs(mats, n_steps):
    specs = []
    for w in mats:
        rows, cols = w.shape
        nb = _cast_row_blocks(rows, n_steps)
        specs.append(pl.BlockSpec((rows // nb, cols), lambda i, nb=nb: (jnp.minimum(i, nb - 1), 0)))
    return specs


def _ffn_half(xp, xs, g, wg, wu, wd, gf=None, cast=(), cast_chunked=(), sample_out_shape=None):
    tm = TOKEN_TILE
    n = xp.shape[0]
    n_chunks = D_FF // FF_CHUNK
    staged = wg.dtype == F32
    assert all(w.dtype == wg.dtype for w in (wu, wd))
    chunk_major = (n_chunks, D_MODEL, FF_CHUNK)
    assert all(w.shape == ((D_MODEL, D_FF) if staged else chunk_major) for w in (wg, wu))
    assert all(w.shape == (D_MODEL, D_FF) for w in cast_chunked)
    assert n % tm == 0 and xs.size == tm * D_MODEL and xs.shape[-1] == D_MODEL
    assert D_FF % FF_CHUNK == 0 and n_chunks >= STAGE_SLOTS
    n_tiles = n // tm
    prompt_spec = pl.BlockSpec((tm, D_MODEL), lambda i: (i, 0))
    next_spec = pl.BlockSpec((tm, D_MODEL), lambda i: (jnp.minimum(i + 1, n_tiles - 1), 0))
    hbm = pl.BlockSpec(memory_space=pl.ANY)
    sample_shape = sample_out_shape or (tm, D_MODEL)
    in_specs = [prompt_spec, next_spec, _resident(xs.shape), _resident((1, D_MODEL)), hbm, hbm, hbm]
    args = [xp, xp, xs, g.reshape(1, D_MODEL), wg, wu, wd]
    if gf is not None:
        in_specs.append(_resident((1, D_MODEL)))
        args.append(gf.reshape(1, D_MODEL))
    cast_specs = _cast_specs(cast, n_tiles)
    chunked_in_specs = _cast_specs(cast_chunked, n_tiles)
    chunked_out_specs = [pl.BlockSpec((n_chunks,) + spec.block_shape[:1] + (FF_CHUNK,), lambda i, m=spec.index_map: (0,) + m(i))
                         for spec in chunked_in_specs]
    n_slots = STAGE_SLOTS if staged else n_chunks
    stage = [pltpu.VMEM((n_slots, 2, D_MODEL, FF_CHUNK), F32), pltpu.VMEM((n_slots, FF_CHUNK, D_MODEL), F32)]
    outs = pl.pallas_call(
        functools.partial(_ffn_kernel, final_norm=gf is not None, n_cast=len(cast), n_cast_chunked=len(cast_chunked),
                          staged=staged),
        grid=(n_tiles,),
        in_specs=in_specs + cast_specs + chunked_in_specs,
        out_specs=[prompt_spec, pl.BlockSpec(sample_shape, lambda i: (0,) * len(sample_shape))] + cast_specs
        + chunked_out_specs,
        out_shape=[jax.ShapeDtypeStruct((n, D_MODEL), F32), jax.ShapeDtypeStruct(sample_shape, F32)]
        + [jax.ShapeDtypeStruct(w.shape, BF16) for w in cast] + [jax.ShapeDtypeStruct(chunk_major, BF16)] * len(cast_chunked),
        scratch_shapes=[pltpu.VMEM((tm, D_MODEL), BF16), pltpu.VMEM((tm, FF_CHUNK), BF16),
                        pltpu.VMEM((tm, D_FF - FF_CHUNK), BF16),
                        pltpu.VMEM((n_chunks, D_MODEL, FF_CHUNK), BF16), pltpu.VMEM((n_chunks, D_MODEL, FF_CHUNK), BF16),
                        pltpu.VMEM((D_FF, D_MODEL), BF16)] + (stage if staged else [])
        + [pltpu.SemaphoreType.DMA((n_slots, 3))],
        compiler_params=pltpu.CompilerParams(dimension_semantics=("arbitrary",), vmem_limit_bytes=VMEM_LIMIT_BYTES),
        name="ffn_final" if gf is not None else "ffn_half",
    )(*args, *cast, *cast_chunked)
    return outs[0], outs[1], list(outs[2:])


def _first_half(shape):
    return (lax.broadcasted_iota(jnp.int32, shape, 1) & (HEAD_DIM - 1)) < HEAD_DIM // 2


def _sign_sin(sin):
    return jnp.where(_first_half(sin.shape), -sin, sin)


def _rope(xg, cos, sin_signed):
    swapped = jnp.where(_first_half(xg.shape), pltpu.roll(xg, LANES - HEAD_DIM // 2, 1),
                        pltpu.roll(xg, HEAD_DIM // 2, 1))
    return xg * cos + swapped * sin_signed


def _in_proj(x, gmix, w_in, cos, sin_signed, gvn_gain):
    h = _rms(x, gmix).astype(BF16)
    z = jnp.dot(h, w_in, preferred_element_type=F32)
    q = jnp.concatenate(
        [(_rope(z[:, LANES * i:LANES * (i + 1)], cos, sin_signed) * Q_SCALE).astype(BF16) for i in range(Q_W // LANES)],
        axis=1)
    k = _rope(z[:, K_OFF:V_OFF], cos, sin_signed)
    v = z[:, V_OFF:U_OFF]
    u = _gelu(z[:, U_OFF:GV_OFF])
    gvn = _rms(_gelu(z[:, GV_OFF:]), gvn_gain)
    return q, k, v, u, gvn


def _out_proj(x, ya, yg, ga, gg, w_out):
    cat = jnp.concatenate([_rms(ya, ga).astype(BF16), _rms(yg, gg).astype(BF16)], axis=1)
    return x + jnp.dot(cat, w_out, preferred_element_type=F32)


def _softmax_sink(s, sink):
    m = jnp.maximum(jnp.max(s, axis=1, keepdims=True), sink)
    p = jnp.exp2(s - m)
    return p, jnp.sum(p, axis=1, keepdims=True) + jnp.exp2(sink - m)


def _gelu(x):
    k = -2.0 * (2.0 / jnp.pi) ** 0.5 * LOG2E
    return x / (1.0 + jnp.exp2(x * (k + (0.044715 * k) * (x * x))))


def _mix_prompt_kernel(xa_ref, xc_ref, coff_ref, soff_ref, cbase_ref, sbase_ref, gmix_ref, win_ref, sinks_ref, gvg_ref,
                       wcat_ref, bias_ref, ga_ref, gg_ref, wout_ref, o_ref, ko_ref, vo_ref, gvo_ref, q_s, k_s, v_s, u_s,
                       gv_s, cat_s, *, tm, tiles_per_seq, n_tiles):
    s = pl.program_id(0)

    @pl.when(s == 0)
    def _():
        k_s[1, tm:tm + WINDOW, :] = jnp.zeros((WINDOW, KV_W), BF16)
        v_s[1, tm:tm + WINDOW, :] = jnp.zeros((WINDOW, KV_W), BF16)
        cat_s[1] = jnp.zeros((tm, D_MODEL), BF16)

    step = functools.partial(
        _mix_prompt_step, s, xa_ref, xc_ref, coff_ref, soff_ref, cbase_ref, sbase_ref, gmix_ref, win_ref, sinks_ref,
        gvg_ref, wcat_ref, bias_ref, ga_ref, gg_ref, wout_ref, o_ref, ko_ref, vo_ref, gvo_ref, q_s, k_s, v_s, u_s, gv_s,
        cat_s, tm=tm, tiles_per_seq=tiles_per_seq, n_tiles=n_tiles)
    last = n_tiles + 1
    pl.when(s == 0)(functools.partial(step, cur=0, stages="proj"))
    pl.when(s == last)(functools.partial(step, cur=last % 2, stages="out"))
    for parity in range(2):
        pl.when((s > 0) & (s < last) & (s % 2 == parity))(functools.partial(step, cur=parity, stages="all"))


def _mix_prompt_step(s, xa_ref, xc_ref, coff_ref, soff_ref, cbase_ref, sbase_ref, gmix_ref, win_ref, sinks_ref, gvg_ref,
                     wcat_ref, bias_ref, ga_ref, gg_ref, wout_ref, o_ref, ko_ref, vo_ref, gvo_ref, q_s, k_s, v_s, u_s, gv_s,
                     cat_s, *, tm, tiles_per_seq, n_tiles, cur, stages):
    oth = 1 - cur

    h = _rms(xa_ref[0], gmix_ref[...]).astype(BF16)
    tile_in_seq = jnp.minimum(s, n_tiles - 1) % tiles_per_seq
    cb = cbase_ref[pl.ds(tile_in_seq, 1), :]
    sb = sbase_ref[pl.ds(tile_in_seq, 1), :]
    cos = cb * coff_ref[...] - sb * soff_ref[...]
    sin = _sign_sin(sb * coff_ref[...] + cb * soff_ref[...])

    def proj_q():
        z = jnp.dot(h, win_ref[:, 0:Q_W], preferred_element_type=F32)
        for i in range(Q_W // LANES):
            q_s[cur, :, LANES * i:LANES * (i + 1)] = (
                _rope(z[:, LANES * i:LANES * (i + 1)], cos, sin) * Q_SCALE).astype(BF16)

    def proj_kv():
        z = jnp.dot(h, win_ref[:, K_OFF:U_OFF], preferred_element_type=F32)
        k = _rope(z[:, 0:KV_W], cos, sin)
        v = z[:, KV_W:]
        k_s[cur, 0:WINDOW, :] = k_s[oth, tm:tm + WINDOW, :]
        v_s[cur, 0:WINDOW, :] = v_s[oth, tm:tm + WINDOW, :]
        k_s[cur, WINDOW:, :] = k.astype(BF16)
        v_s[cur, WINDOW:, :] = v.astype(BF16)
        ko_ref[0] = k[tm - WINDOW:].T
        vo_ref[0] = v[tm - WINDOW:].T

    def proj_u():
        u_s[cur] = _gelu(jnp.dot(h, win_ref[:, U_OFF:GV_OFF], preferred_element_type=F32))

    def proj_gv():
        gvn = _rms(_gelu(jnp.dot(h, win_ref[:, GV_OFF:], preferred_element_type=F32)), gvg_ref[...])
        gv_s[cur] = gvn.astype(BF16)
        gvo_ref[0] = gvn[tm - CHUNK:].T

    def out_half(c):
        cols = slice(c * (D_MODEL // 2), (c + 1) * (D_MODEL // 2))

        def run():
            o_ref[0, :, cols] = xc_ref[0, :, cols] + jnp.dot(cat_s[cur], wout_ref[:, cols], preferred_element_type=F32)
        return run

    mid_fill = [[out_half(0)], [proj_q], [proj_kv, proj_u], [proj_gv]]
    end_fill = [[], [], [], [out_half(1)]]
    assert len(mid_fill) == len(end_fill) == tm // WINDOW
    if stages == "proj":
        for run in (proj_q, proj_kv, proj_u, proj_gv):
            run()
        return
    if stages == "out":
        out_half(0)()
        out_half(1)()
        return

    wrow = lax.broadcasted_iota(jnp.int32, (CHUNK, CHUNK), 0)
    wcol = lax.broadcasted_iota(jnp.int32, (CHUNK, CHUNK), 1)
    wtril = [jnp.where(wcol <= wrow, wcat_ref[hd], 0.0).astype(BF16) for hd in range(G_HEADS)]
    wmix = [jnp.concatenate(wtril[2 * p:2 * p + 2], axis=1) for p in range(G_HEADS // 2)]
    lane = lax.broadcasted_iota(jnp.int32, (CHUNK, LANES), 1)
    low_head = lane < HEAD_DIM

    qi = lax.broadcasted_iota(jnp.int32, (WINDOW, 2 * WINDOW), 0)
    sj = lax.broadcasted_iota(jnp.int32, (WINDOW, 2 * WINDOW), 1)
    dist = WINDOW + qi - sj
    band = (dist >= 0) & (dist < WINDOW)
    first_lo = jnp.where((s + tiles_per_seq - 1) % tiles_per_seq == 0, WINDOW, 0)

    for j in range(tm // WINDOW):
        rows = slice(j * WINDOW, (j + 1) * WINDOW)
        mask = band & (sj >= first_lo) if j == 0 else band
        qb = q_s[oth, rows, :]
        scores = []
        for kv in range(N_KV):
            qs = jnp.concatenate(
                [qb[:, HEAD_DIM * hd:HEAD_DIM * (hd + 1)] for hd in range(kv * GQA, (kv + 1) * GQA)], axis=0)
            kb = k_s[oth, j * WINDOW:(j + 2) * WINDOW, HEAD_DIM * kv:HEAD_DIM * (kv + 1)]
            scores.append(lax.dot_general(qs, kb, _NT, preferred_element_type=F32))
        for run in mid_fill[j]:
            run()
        outs, dens = [], []
        for kv in range(N_KV):
            vb = v_s[oth, j * WINDOW:(j + 2) * WINDOW, HEAD_DIM * kv:HEAD_DIM * (kv + 1)]
            ps = []
            for g in range(GQA):
                sg = jnp.where(mask, scores[kv][g * WINDOW:(g + 1) * WINDOW], -jnp.inf)
                p, den = _softmax_sink(sg, sinks_ref[kv * GQA + g] * LOG2E)
                ps.append(p.astype(BF16))
                dens.append(den)
            outs.append(jnp.dot(jnp.concatenate(ps, axis=0), vb, preferred_element_type=F32))
        mixed = []
        for p in range(G_HEADS // 2):
            r = gv_s[oth, rows, LANES * p:LANES * (p + 1)]
            zero = jnp.zeros_like(r)
            rhs = jnp.concatenate([jnp.where(low_head, r, zero), jnp.where(low_head, zero, r)], axis=0)
            mixed.append(jnp.dot(wmix[p], rhs, preferred_element_type=F32))
        for run in end_fill[j]:
            run()
        ya = [outs[hd // GQA][(hd % GQA) * WINDOW:(hd % GQA + 1) * WINDOW] / dens[hd] for hd in range(N_HEADS)]
        cat_s[oth, rows, 0:Q_W] = _rms(jnp.concatenate(ya, axis=1), ga_ref[...]).astype(BF16)
        yg = u_s[oth, rows, :] * (jnp.concatenate(mixed, axis=1) + bias_ref[...])
        cat_s[oth, rows, Q_W:] = _rms(yg, gg_ref[...]).astype(BF16)


def _mix_prompt(x, gmix, w_in, sinks, gvg, wcat, bias_full, ga, gg, w_out):
    b, s, _ = x.shape
    tm = TOKEN_TILE
    assert s % tm == 0 and tm % WINDOW == 0
    tiles_per_seq = s // tm
    n_tiles = b * tiles_per_seq
    cos_off, sin_off = _rope_tables(jnp.arange(tm, dtype=jnp.int32))
    cos_base, sin_base = _rope_tables(tm * jnp.arange(tiles_per_seq, dtype=jnp.int32))
    proj_tile = lambda i: jnp.minimum(i, n_tiles - 1)
    out_tile = lambda i: jnp.maximum(i - 2, 0)
    x_tiles = x.reshape(n_tiles, tm, D_MODEL)
    last = lambda width: pl.BlockSpec((1, width, WINDOW), lambda i: (proj_tile(i) // tiles_per_seq, 0, 0))
    out, ko, vo, gvo = pl.pallas_call(
        functools.partial(_mix_prompt_kernel, tm=tm, tiles_per_seq=tiles_per_seq, n_tiles=n_tiles),
        grid=(n_tiles + 2,),
        in_specs=[pl.BlockSpec((1, tm, D_MODEL), lambda i: (proj_tile(i), 0, 0)),
                  pl.BlockSpec((1, tm, D_MODEL), lambda i: (out_tile(i), 0, 0)),
                  _resident((tm, LANES)), _resident((tm, LANES)), _resident((tiles_per_seq, LANES)),
                  _resident((tiles_per_seq, LANES)), _resident((1, D_MODEL)), _resident((D_MODEL, D_IN)),
                  pl.BlockSpec(memory_space=pltpu.SMEM), _resident((1, D_GMLP)),
                  _resident((G_HEADS, CHUNK, CHUNK)), _resident((CHUNK, D_GMLP)), _resident((1, Q_W)),
                  _resident((1, D_GMLP)), _resident((D_MODEL, D_MODEL))],
        out_specs=[pl.BlockSpec((1, tm, D_MODEL), lambda i: (out_tile(i), 0, 0)), last(KV_W), last(KV_W),
                   last(D_GMLP)],
        out_shape=[jax.ShapeDtypeStruct((n_tiles, tm, D_MODEL), F32), jax.ShapeDtypeStruct((b, KV_W, WINDOW), F32),
                   jax.ShapeDtypeStruct((b, KV_W, WINDOW), F32), jax.ShapeDtypeStruct((b, D_GMLP, CHUNK), F32)],
        scratch_shapes=[pltpu.VMEM((2, tm, Q_W), BF16), pltpu.VMEM((2, tm + WINDOW, KV_W), BF16),
                        pltpu.VMEM((2, tm + WINDOW, KV_W), BF16), pltpu.VMEM((2, tm, D_GMLP), F32),
                        pltpu.VMEM((2, tm, D_GMLP), BF16), pltpu.VMEM((2, tm, D_MODEL), BF16)],
        compiler_params=pltpu.CompilerParams(dimension_semantics=("arbitrary",), vmem_limit_bytes=VMEM_LIMIT_BYTES),
        name="mix_prompt",
    )(x_tiles, x_tiles, cos_off, sin_off, cos_base, sin_base, gmix, w_in, sinks, gvg, wcat, bias_full, ga, gg, w_out)
    return out.reshape(b, s, D_MODEL), ko, vo, gvo


def _mix_sample_kernel(x_ref, ckt_ref, cvt_ref, cos_ref, sin_ref, gmix_ref, win_ref, sinks_ref, gvg_ref, coef_ref,
                       biasr_ref, ga_ref, gg_ref, wout_ref, o_ref, kot_ref, vot_ref, gvo_ref, q_s, k_s, v_s, ya_s,
                       yg_s, *, t_new, w_buf):
    step = pl.program_id(0)
    n_tok = x_ref.shape[0]
    step_seqs = SEQ_GROUP * GROUPS_PER_STEP
    step_rows = step_seqs * t_new
    grp_rows = SEQ_GROUP * t_new
    grp_keys = SEQ_GROUP * w_buf
    sub = 8

    @pl.when(step == 0)
    def _():
        tile = lambda tab: jnp.broadcast_to(tab[None], (n_tok // sub, sub, LANES)).reshape(n_tok, LANES)
        q, k, v, u, gvn = _in_proj(x_ref[...], gmix_ref[...], win_ref[...], tile(cos_ref[...]),
                                   tile(_sign_sin(sin_ref[...])), gvg_ref[...])
        q_s[...] = q
        k_s[...] = k
        v_s[...] = v
        gv_seq = gvn.reshape(n_tok // t_new, t_new, D_GMLP)
        for tok in range(t_new):
            gvo_ref[tok] = gv_seq[:, tok, :].T
        g3 = gvn.reshape(n_tok // sub, sub, D_GMLP)
        trow = lax.broadcasted_iota(jnp.int32, (1, sub, D_GMLP), 1) & (t_new - 1)
        mixed = biasr_ref[...][None] + coef_ref[0][None] * g3
        for d in range(1, t_new):
            shifted = jnp.where(trow >= d, pltpu.roll(g3, d, 1), 0.0)
            mixed = mixed + coef_ref[d][None] * shifted
        yg_s[...] = (u.reshape(n_tok // sub, sub, D_GMLP) * mixed).reshape(n_tok, D_GMLP)

    row0 = pl.multiple_of(step * step_rows, step_rows)
    q_step = q_s[pl.ds(row0, step_rows), :]
    kn = k_s[pl.ds(row0, step_rows), :]
    vn = v_s[pl.ds(row0, step_rows), :]

    knt = kn.T
    vnt = vn.T
    tail = lax.broadcasted_iota(jnp.int32, (KV_W, w_buf), 1) >= w_buf - t_new
    for b in range(step_seqs):
        shift = (w_buf - t_new - t_new * b) % w_buf
        kot_ref[b] = jnp.where(tail, pltpu.roll(knt, shift, 1) if shift else knt,
                               pltpu.roll(ckt_ref[b], w_buf - t_new, 1))
        vot_ref[b] = jnp.where(tail, pltpu.roll(vnt, shift, 1) if shift else vnt,
                               pltpu.roll(cvt_ref[b], w_buf - t_new, 1))

    knb = kn.astype(BF16)
    vnb = vn.astype(BF16)

    n_rows = GQA * grp_rows
    shift_t = t_new.bit_length() - 1
    shift_w = w_buf.bit_length() - 1
    r = lax.broadcasted_iota(jnp.int32, (n_rows, grp_keys), 0)
    c = lax.broadcasted_iota(jnp.int32, (n_rows, grp_keys), 1)
    mask_c = ((c >> shift_w) == ((r & (grp_rows - 1)) >> shift_t)) & ((c & (w_buf - 1)) > (r & (t_new - 1)))
    r2 = lax.broadcasted_iota(jnp.int32, (n_rows, grp_rows), 0)
    c2 = lax.broadcasted_iota(jnp.int32, (n_rows, grp_rows), 1)
    mask_n = ((c2 >> shift_t) == ((r2 & (grp_rows - 1)) >> shift_t)) & ((c2 & (t_new - 1)) <= (r2 & (t_new - 1)))
    row_head = lax.broadcasted_iota(jnp.int32, (n_rows, 1), 0) >> (grp_rows.bit_length() - 1)

    for grp in range(GROUPS_PER_STEP):
        rows = slice(grp * grp_rows, (grp + 1) * grp_rows)
        seqs = range(grp * SEQ_GROUP, (grp + 1) * SEQ_GROUP)
        for kv in range(N_KV):
            heads = [kv * GQA + i for i in range(GQA)]
            lanes = slice(HEAD_DIM * kv, HEAD_DIM * (kv + 1))
            kt = jnp.concatenate([ckt_ref[b, lanes, :] for b in seqs], axis=1).astype(BF16)
            vt = jnp.concatenate([cvt_ref[b, lanes, :] for b in seqs], axis=1).astype(BF16)
            qs = jnp.concatenate([q_step[rows, HEAD_DIM * hd:HEAD_DIM * (hd + 1)] for hd in heads], axis=0)
            s_c = jnp.where(mask_c, jnp.dot(qs, kt, preferred_element_type=F32), -jnp.inf)
            s_n = jnp.where(mask_n, lax.dot_general(qs, knb[rows, lanes], _NT, preferred_element_type=F32), -jnp.inf)
            sink = jnp.full((n_rows, 1), sinks_ref[heads[0]] * LOG2E, F32)
            for i in range(1, GQA):
                sink = jnp.where(row_head == i, sinks_ref[heads[i]] * LOG2E, sink)
            m = jnp.maximum(jnp.maximum(jnp.max(s_c, axis=1, keepdims=True), jnp.max(s_n, axis=1, keepdims=True)),
                            sink)
            p_c = jnp.exp2(s_c - m)
            p_n = jnp.exp2(s_n - m)
            den = jnp.sum(p_c, axis=1, keepdims=True) + jnp.sum(p_n, axis=1, keepdims=True) + jnp.exp2(sink - m)
            o = (lax.dot_general(p_c.astype(BF16), vt, _NT, preferred_element_type=F32)
                 + jnp.dot(p_n.astype(BF16), vnb[rows, lanes], preferred_element_type=F32)) / den
            for i, hd in enumerate(heads):
                ya_s[pl.ds(row0 + grp * grp_rows, grp_rows), HEAD_DIM * hd:HEAD_DIM * (hd + 1)] = (
                    o[i * grp_rows:(i + 1) * grp_rows])

    @pl.when(step == pl.num_programs(0) - 1)
    def _():
        o_ref[...] = _out_proj(x_ref[...], ya_s[...], yg_s[...], ga_ref[...], gg_ref[...], wout_ref[...])


def _mix_sample(x, cache_kt, cache_vt, cos, sin, gmix, w_in, sinks, gvg, coef, bias_rows, ga, gg, w_out, t_new):
    n_tok = x.shape[0]
    n_seq, _, w_buf = cache_kt.shape
    step_seqs = SEQ_GROUP * GROUPS_PER_STEP
    assert n_seq % step_seqs == 0 and n_tok == n_seq * t_new
    assert t_new & (t_new - 1) == 0 and w_buf & (w_buf - 1) == 0 and 8 % t_new == 0
    assert step_seqs * t_new == w_buf == LANES
    cache_spec = pl.BlockSpec((step_seqs, KV_W, w_buf), lambda i: (i, 0, 0))
    return pl.pallas_call(
        functools.partial(_mix_sample_kernel, t_new=t_new, w_buf=w_buf),
        grid=(n_seq // step_seqs,),
        in_specs=[_resident((n_tok, D_MODEL)), cache_spec, cache_spec, _resident((8, LANES)),
                  _resident((8, LANES)), _resident((1, D_MODEL)), _resident((D_MODEL, D_IN)),
                  pl.BlockSpec(memory_space=pltpu.SMEM), _resident((1, D_GMLP)), _resident((t_new, 8, D_GMLP)),
                  _resident((8, D_GMLP)), _resident((1, Q_W)), _resident((1, D_GMLP)),
                  _resident((D_MODEL, D_MODEL))],
        out_specs=[pl.BlockSpec((n_tok, D_MODEL), lambda i: (0, 0)), cache_spec, cache_spec,
                   pl.BlockSpec((t_new, D_GMLP, n_seq), lambda i: (0, 0, 0))],
        out_shape=[jax.ShapeDtypeStruct((n_tok, D_MODEL), F32), jax.ShapeDtypeStruct(cache_kt.shape, F32),
                   jax.ShapeDtypeStruct(cache_vt.shape, F32), jax.ShapeDtypeStruct((t_new, D_GMLP, n_seq), F32)],
        scratch_shapes=[pltpu.VMEM((n_tok, Q_W), BF16), pltpu.VMEM((n_tok, KV_W), F32), pltpu.VMEM((n_tok, KV_W), F32),
                        pltpu.VMEM((n_tok, Q_W), F32), pltpu.VMEM((n_tok, D_GMLP), F32)],
        compiler_params=pltpu.CompilerParams(dimension_semantics=("arbitrary",), vmem_limit_bytes=VMEM_LIMIT_BYTES),
        name="mix_sample",
    )(x, cache_kt, cache_vt, cos, sin, gmix, w_in, sinks, gvg, coef, bias_rows, ga, gg, w_out)


def _gmlp_tables_kernel(ws_ref, bs_ref, bias_ref, coef_ref, biasr_ref, *, t_new):
    sub = 8
    expand = lambda col: jnp.broadcast_to(col, (col.shape[0], D_GMLP // G_HEADS))
    row = lax.broadcasted_iota(jnp.int32, (CHUNK, CHUNK), 0)
    lane = lax.broadcasted_iota(jnp.int32, (CHUNK, CHUNK), 1)
    bias = jnp.concatenate(
        [expand(jnp.sum(jnp.where(lane == row, bs_ref[hd:hd + 1, :], 0.0), axis=1, keepdims=True))
         for hd in range(G_HEADS)], axis=1)
    bias_ref[...] = bias
    biasr_ref[...] = jnp.concatenate([bias[0:t_new]] * (sub // t_new), axis=0)
    t = lax.broadcasted_iota(jnp.int32, (sub, CHUNK), 0) & (t_new - 1)
    s = lax.broadcasted_iota(jnp.int32, (sub, CHUNK), 1)
    for d in range(t_new):
        cols = []
        for hd in range(G_HEADS):
            w_rows = jnp.concatenate([ws_ref[hd, 0:t_new, :]] * (sub // t_new), axis=0)
            cols.append(expand(jnp.sum(jnp.where(s == t - d, w_rows, 0.0), axis=1, keepdims=True)))
        coef_ref[d] = jnp.concatenate(cols, axis=1)


def _gmlp_tables(ws, bs, t_new):
    assert 8 % t_new == 0
    return pl.pallas_call(
        functools.partial(_gmlp_tables_kernel, t_new=t_new),
        out_shape=[jax.ShapeDtypeStruct((CHUNK, D_GMLP), F32), jax.ShapeDtypeStruct((t_new, 8, D_GMLP), F32),
                   jax.ShapeDtypeStruct((8, D_GMLP), F32)],
        name="gmlp_tables",
    )(ws, bs)


def _rope_tables(pos):
    inv_freq = ROPE_THETA ** (-jnp.arange(0, HEAD_DIM, 2, dtype=F32) / HEAD_DIM)
    ang = pos.astype(F32)[:, None] * jnp.tile(inv_freq, 2 * LANES // HEAD_DIM)[None, :]
    return jnp.cos(ang), jnp.sin(ang)


def kernel(x_prompt, x_sample, cache_k_win, cache_v_win, norm_ffn1, ffn1_gate, ffn1_up, ffn1_down, norm_mix, w_in,
           attn_sinks, gmlp_v_norm, gmlp_w_s, gmlp_b_s, norm_attn_out, norm_gmlp_out, w_out, norm_ffn2, ffn2_gate,
           ffn2_up, ffn2_down, norm_final):
    depth = norm_ffn1.shape[0]
    b, s, _ = x_prompt.shape
    bd, t_new, _ = x_sample.shape
    w_buf = cache_k_win.shape[2]

    cos_s, sin_s = _rope_tables(PAST_LEN + jnp.arange(t_new, dtype=jnp.int32))
    cos_s, sin_s = jnp.tile(cos_s, (8 // t_new, 1)), jnp.tile(sin_s, (8 // t_new, 1))

    hp = x_prompt.reshape(b * s, D_MODEL)
    hs = x_sample
    outs = [[] for _ in range(6)]
    ffn1_w = [ffn1_gate[0], ffn1_up[0], ffn1_down[0]]
    for l in range(depth):
        last = l == depth - 1
        row = lambda a: a[l].reshape(1, -1)

        wcat = gmlp_w_s[l]
        bias_full, coef, bias_rows = _gmlp_tables(gmlp_w_s[l], gmlp_b_s[l], t_new)

        hp, hs, (w_in_b, w_out_b, wd2, wg2, wu2) = _ffn_half(
            hp, hs, norm_ffn1[l], *ffn1_w, cast=(w_in[l], w_out[l], ffn2_down[l]), cast_chunked=(ffn2_gate[l], ffn2_up[l]))

        hp, kpt, vpt, gvpt = _mix_prompt(hp.reshape(b, s, D_MODEL), row(norm_mix), w_in_b, attn_sinks[l],
                                         row(gmlp_v_norm), wcat, bias_full, row(norm_attn_out), row(norm_gmlp_out),
                                         w_out_b)
        to_t = lambda c: c.transpose(0, 2, 3, 1).reshape(bd, KV_W, w_buf)
        hs, kst, vst, gvs = _mix_sample(hs, to_t(cache_k_win[l]), to_t(cache_v_win[l]), cos_s, sin_s, row(norm_mix),
                                        w_in_b, attn_sinks[l], row(gmlp_v_norm), coef, bias_rows, row(norm_attn_out),
                                        row(norm_gmlp_out), w_out_b, t_new)
        from_t = lambda c: c.reshape(bd, N_KV, HEAD_DIM, w_buf).transpose(0, 3, 1, 2)

        hp, hs, next_w = _ffn_half(hp.reshape(b * s, D_MODEL), hs, norm_ffn2[l], wg2, wu2, wd2,
                                   gf=norm_final if last else None, cast=() if last else (ffn1_down[l + 1],),
                                   cast_chunked=() if last else (ffn1_gate[l + 1], ffn1_up[l + 1]),
                                   sample_out_shape=(bd, t_new, D_MODEL) if last else None)
        ffn1_w = next_w[1:] + next_w[:1]

        outs[0].append(kpt.reshape(b, N_KV, HEAD_DIM, WINDOW).transpose(0, 3, 1, 2))
        outs[1].append(vpt.reshape(b, N_KV, HEAD_DIM, WINDOW).transpose(0, 3, 1, 2))
        outs[2].append(from_t(kst))
        outs[3].append(from_t(vst))
        outs[4].append(gvpt.reshape(b, G_HEADS, D_GMLP // G_HEADS, CHUNK).transpose(0, 3, 1, 2))
        outs[5].append(gvs.reshape(t_new, G_HEADS, D_GMLP // G_HEADS, bd).transpose(3, 0, 1, 2))

    return (hp.reshape(b, s, D_MODEL), hs) + tuple(jnp.stack(o) for o in outs)
```
